```python
import numpy as np
import jax
import jax.numpy as jnp
from jax import lax

D_MODEL = 1024
BATCH = 8
SEQ = 2048
DEPTH = 2

PLE_DIM = 256
HEAD_DIM = 64
N_HEADS_A = 8
N_HEADS_B = 8
N_KV_B = 2
GQA_REP = N_HEADS_B // N_KV_B
N_HEADS_TOTAL = N_HEADS_A + N_HEADS_B
D_MIX = N_HEADS_TOTAL * HEAD_DIM
DILATED_PATTERNS = ((128, 1), (512, 4), (2048, 16))
Q_BLOCK_A = 128
CMP_BLOCK = 32
CMP_STRIDE = 16
CMP_HIDDEN = 256
SEL_BLOCK = 64
SEL_TOP = 16
N_LOCAL_BLOCKS = 2
WINDOW = 512
Q_BLOCK_B = 64
N_GROUPS = 4
EXPERTS_PER_GROUP = 4
N_EXPERTS = N_GROUPS * EXPERTS_PER_GROUP
D_EXPERT = 512
TOP_K_IN_GROUP = 2
RMS_EPS = 1e-6
NEG = -1e30
FORCE_BONUS = 1e4
A_W = N_HEADS_A * HEAD_DIM
B_Q = N_HEADS_B * HEAD_DIM
B_KV = N_KV_B * HEAD_DIM
B_GATE = N_HEADS_B * 3
SPLITS = (A_W, A_W, A_W, B_Q, B_KV, B_KV, B_KV, B_KV, B_KV, B_KV, B_GATE)
N_IN = sum(SPLITS)

kernel_name = "hybrid_dilated_nsa_hmoe_block"


def rms_norm(x, g):
    xf = x.astype(jnp.float32)
    y = xf * lax.rsqrt(jnp.mean(xf * xf, axis=-1, keepdims=True) + RMS_EPS)
    return (y * g.astype(jnp.float32)).astype(x.dtype)


def alibi_slopes():
    s = 2.0 ** (-8.0 * np.arange(1, N_HEADS_TOTAL + 1) / N_HEADS_TOTAL)
    return jnp.asarray(s[0::2], jnp.float32), jnp.asarray(s[1::2].reshape(N_KV_B, GQA_REP), jnp.float32)


def dilated_attention(q, k, v, slopes):
    B, H, S, dh = q.shape
    scale = dh ** -0.5
    m = slopes[None, :, None, None]

    def block(t0):
        qb = lax.dynamic_slice_in_dim(q, t0, Q_BLOCK_A, axis=2)
        t = t0 + jnp.arange(Q_BLOCK_A)
        nums, dens, maxs = [], [], []
        for window, dil in DILATED_PATTERNS:
            dist = jnp.arange(window // dil + 1) * dil
            pos = t[:, None] - dist[None, :]
            valid = pos >= 0
            posc = jnp.maximum(pos, 0)
            kg = k[:, :, posc]
            vg = v[:, :, posc]
            s = jnp.einsum('bhqd,bhqjd->bhqj', qb, kg).astype(jnp.float32) * scale - m * dist.astype(jnp.float32)
            s = jnp.where(valid, s, NEG)
            mx = jnp.max(s, axis=-1, keepdims=True)
            e = jnp.exp(s - mx)
            dens.append(jnp.sum(e, axis=-1, keepdims=True))
            nums.append(jnp.einsum('bhqj,bhqjd->bhqd', e.astype(v.dtype), vg).astype(jnp.float32))
            maxs.append(mx)
        M = maxs[0]
        for mx in maxs[1:]:
            M = jnp.maximum(M, mx)
        ws = [jnp.exp(mx - M) for mx in maxs]
        num = sum(w * n for w, n in zip(ws, nums))
        den = sum(w * d for w, d in zip(ws, dens))
        return (num / den).astype(q.dtype)

    outs = lax.map(block, jnp.arange(S // Q_BLOCK_A) * Q_BLOCK_A)
    return jnp.moveaxis(outs, 0, 2).reshape(B, H, S, dh)


def nsa_attention(q, k_c, v_c, k_s, v_s, k_w, v_w, gates, m, w_ck1, w_ck2, w_cv1, w_cv2, cmp_pos):
    B, G, R, S, dh = q.shape
    scale = dh ** -0.5
    t = jnp.arange(S)
    n_cmp = (S - CMP_BLOCK) // CMP_STRIDE + 1
    cidx = np.arange(n_cmp)[:, None] * CMP_STRIDE + np.arange(CMP_BLOCK)[None, :]

    def compress(kv, pos, w1, w2):
        blk = (kv[:, :, cidx] + pos).reshape(B, G, n_cmp, CMP_BLOCK * dh)
        return jax.nn.gelu(blk @ w1) @ w2

    k_cmp = compress(k_c, cmp_pos[0], w_ck1, w_ck2)
    v_cmp = compress(v_c, cmp_pos[1], w_cv1, w_cv2)
    cmp_end = jnp.asarray(cidx[:, -1])
    dist_c = (t[:, None] - cmp_end[None, :]).astype(jnp.float32)
    valid_c = dist_c >= 0
    s = jnp.einsum('bgrqd,bgcd->bgrqc', q, k_cmp).astype(jnp.float32) * scale - m * dist_c
    p_cmp = jax.nn.softmax(jnp.where(valid_c, s, NEG), axis=-1) * valid_c
    o_cmp = jnp.einsum('bgrqc,bgcd->bgrqd', p_cmp.astype(v_cmp.dtype), v_cmp)

    nb = S // SEL_BLOCK
    cs = cidx[:, 0]
    bs = np.arange(nb) * SEL_BLOCK
    ov = np.clip(np.minimum(cs[:, None] + CMP_BLOCK, bs[None, :] + SEL_BLOCK) - np.maximum(cs[:, None], bs[None, :]), 0, None) / CMP_BLOCK
    ov = jnp.asarray(ov, jnp.float32)
    imp = jnp.einsum('bgrqc,cn->bgqn', p_cmp, ov)
    back = (t // SEL_BLOCK)[:, None] - jnp.arange(nb)[None, :]
    valid_s = back >= 0
    forced = (jnp.arange(nb)[None, :] == 0) | (valid_s & (back < N_LOCAL_BLOCKS))
    score = jnp.where(valid_s, imp + jnp.where(forced, FORCE_BONUS, 0.0), NEG)
    n_top = min(SEL_TOP, nb)
    top_val, top_idx = lax.top_k(score, n_top)
    top_ok = top_val > 0.5 * NEG

    kb = k_s.reshape(B, G, nb, SEL_BLOCK, dh)
    vb = v_s.reshape(B, G, nb, SEL_BLOCK, dh)
    kw_pad = jnp.pad(k_w, ((0, 0), (0, 0), (WINDOW, 0), (0, 0)))
    vw_pad = jnp.pad(v_w, ((0, 0), (0, 0), (WINDOW, 0), (0, 0)))
    bi = jnp.arange(B)[:, None, None, None]
    gi = jnp.arange(G)[None, :, None, None]
    n_keys = n_top * SEL_BLOCK

    def block(t0):
        qb = lax.dynamic_slice_in_dim(q, t0, Q_BLOCK_B, axis=3)
        tq = t0 + jnp.arange(Q_BLOCK_B)
        idx = lax.dynamic_slice_in_dim(top_idx, t0, Q_BLOCK_B, axis=2)
        ok = lax.dynamic_slice_in_dim(top_ok, t0, Q_BLOCK_B, axis=2)
        ks = kb[bi, gi, idx].reshape(B, G, Q_BLOCK_B, n_keys, dh)
        vs = vb[bi, gi, idx].reshape(B, G, Q_BLOCK_B, n_keys, dh)
        spos = (idx[..., None] * SEL_BLOCK + jnp.arange(SEL_BLOCK)).reshape(B, G, Q_BLOCK_B, n_keys)
        dist = (tq[:, None] - spos).astype(jnp.float32)
        okk = jnp.broadcast_to(ok[..., None], ok.shape + (SEL_BLOCK,)).reshape(B, G, Q_BLOCK_B, n_keys) & (dist >= 0)
        s = jnp.einsum('bgrqd,bgqkd->bgrqk', qb, ks).astype(jnp.float32) * scale - m * dist[:, :, None]
        p = jax.nn.softmax(jnp.where(okk[:, :, None], s, NEG), axis=-1)
        o_s = jnp.einsum('bgrqk,bgqkd->bgrqd', p.astype(vs.dtype), vs)
        kw = lax.dynamic_slice_in_dim(kw_pad, t0, WINDOW + Q_BLOCK_B, axis=2)
        vw = lax.dynamic_slice_in_dim(vw_pad, t0, WINDOW + Q_BLOCK_B, axis=2)
        wpos = t0 - WINDOW + jnp.arange(WINDOW + Q_BLOCK_B)
        wdist = tq[:, None] - wpos[None, :]
        wok = (wdist >= 0) & (wdist < WINDOW) & (wpos[None, :] >= 0)
        s = jnp.einsum('bgrqd,bgkd->bgrqk', qb, kw).astype(jnp.float32) * scale - m * wdist.astype(jnp.float32)
        p = jax.nn.softmax(jnp.where(wok, s, NEG), axis=-1)
        o_w = jnp.einsum('bgrqk,bgkd->bgrqd', p.astype(vw.dtype), vw)
        return o_s, o_w

    o_s, o_w = lax.map(block, jnp.arange(S // Q_BLOCK_B) * Q_BLOCK_B)
    unblock = lambda o: jnp.moveaxis(o, 0, 3).reshape(B, G, R, S, dh)
    g = gates.astype(q.dtype)
    return g[..., 0:1] * o_cmp + g[..., 1:2] * unblock(o_s) + g[..., 2:3] * unblock(o_w)


def mixing_layer(h, w_in, w_out, w_ck1, w_ck2, w_cv1, w_cv2, cmp_pos):
    B, S, _ = h.shape
    proj = h @ w_in
    qa, ka, va, qb, kc, vc, ks, vs, kw, vw, gl = jnp.split(proj, list(np.cumsum(SPLITS)[:-1]), axis=-1)
    heads = lambda z, n: z.reshape(B, S, n, HEAD_DIM).transpose(0, 2, 1, 3)
    slopes_a, slopes_b = alibi_slopes()
    oa = dilated_attention(heads(qa, N_HEADS_A), heads(ka, N_HEADS_A), heads(va, N_HEADS_A), slopes_a)
    qB = heads(qb, N_HEADS_B).reshape(B, N_KV_B, GQA_REP, S, HEAD_DIM)
    gates = jax.nn.sigmoid(gl.astype(jnp.float32)).reshape(B, S, N_KV_B, GQA_REP, 3).transpose(0, 2, 3, 1, 4)
    ob = nsa_attention(qB, heads(kc, N_KV_B), heads(vc, N_KV_B), heads(ks, N_KV_B), heads(vs, N_KV_B),
                       heads(kw, N_KV_B), heads(vw, N_KV_B), gates, slopes_b[None, :, :, None, None],
                       w_ck1, w_ck2, w_cv1, w_cv2, cmp_pos)
    ob = ob.reshape(B, N_HEADS_B, S, HEAD_DIM)
    o = jnp.concatenate([oa, ob], axis=1).transpose(0, 2, 1, 3).reshape(B, S, D_MIX)
    return o @ w_out


def hierarchical_moe(h, w_rg, b_rg, w_re, b_re, w_gate, w_up, w_down):
    B, S, D = h.shape
    tok = h.reshape(-1, D)
    glog = (tok @ w_rg).astype(jnp.float32) + b_rg.astype(jnp.float32)
    gprob = jax.nn.softmax(glog, axis=-1)
    gsel = jnp.argmax(glog, axis=-1)
    gw = jnp.take_along_axis(gprob, gsel[:, None], axis=-1)
    elog = ((tok @ w_re).astype(jnp.float32) + b_re.astype(jnp.float32)).reshape(-1, N_GROUPS, EXPERTS_PER_GROUP)
    within = jnp.take_along_axis(elog, gsel[:, None, None], axis=1)[:, 0]
    tv, ti = lax.top_k(within, TOP_K_IN_GROUP)
    ew = jax.nn.softmax(tv, axis=-1) * gw
    local = jnp.sum(jax.nn.one_hot(ti, EXPERTS_PER_GROUP, dtype=jnp.float32) * ew[..., None], axis=1)
    y = jnp.zeros_like(tok)
    for g in range(N_GROUPS):
        gate_g = jnp.where((gsel == g)[:, None], local, 0.0).astype(tok.dtype)
        sl = slice(g * EXPERTS_PER_GROUP, (g + 1) * EXPERTS_PER_GROUP)
        a = jax.nn.silu(jnp.einsum('td,edh->teh', tok, w_gate[sl])) * jnp.einsum('td,edh->teh', tok, w_up[sl])
        y = y + jnp.einsum('teh,ehd->td', a * gate_g[..., None], w_down[sl])
    return y.reshape(B, S, D)


def setup_inputs(seed: int = 0) -> dict:
    key = jax.random.key(seed)
    ks = jax.random.split(key, 22)
    f32 = jnp.float32
    nrm = lambda k, shape, fan_in: jax.random.normal(k, shape, f32) * fan_in ** -0.5
    gain = lambda k, shape: 1.0 + 0.02 * jax.random.normal(k, shape, f32)
    return {
        "x": jax.random.normal(ks[0], (BATCH, SEQ, D_MODEL), f32),
        "p": jax.random.normal(ks[1], (DEPTH, BATCH, SEQ, PLE_DIM), f32),
        "attn_norm": gain(ks[2], (DEPTH, D_MODEL)),
        "w_in": nrm(ks[3], (DEPTH, D_MODEL, N_IN), D_MODEL),
        "w_out": nrm(ks[4], (DEPTH, D_MIX, D_MODEL), D_MIX),
        "w_cmp_k1": nrm(ks[5], (DEPTH, CMP_BLOCK * HEAD_DIM, CMP_HIDDEN), CMP_BLOCK * HEAD_DIM),
        "w_cmp_k2": nrm(ks[6], (DEPTH, CMP_HIDDEN, HEAD_DIM), CMP_HIDDEN),
        "w_cmp_v1": nrm(ks[7], (DEPTH, CMP_BLOCK * HEAD_DIM, CMP_HIDDEN), CMP_BLOCK * HEAD_DIM),
        "w_cmp_v2": nrm(ks[8], (DEPTH, CMP_HIDDEN, HEAD_DIM), CMP_HIDDEN),
        "cmp_pos": 0.02 * jax.random.normal(ks[9], (DEPTH, 2, CMP_BLOCK, HEAD_DIM), f32),
        "ffn_norm": gain(ks[10], (DEPTH, D_MODEL)),
        "w_route_group": nrm(ks[11], (DEPTH, D_MODEL, N_GROUPS), D_MODEL),
        "b_route_group": 0.01 * jax.random.normal(ks[12], (DEPTH, N_GROUPS), f32),
        "w_route_expert": nrm(ks[13], (DEPTH, D_MODEL, N_EXPERTS), D_MODEL),
        "b_route_expert": 0.01 * jax.random.normal(ks[14], (DEPTH, N_EXPERTS), f32),
        "w_expert_gate": nrm(ks[15], (DEPTH, N_EXPERTS, D_MODEL, D_EXPERT), D_MODEL),
        "w_expert_up": nrm(ks[16], (DEPTH, N_EXPERTS, D_MODEL, D_EXPERT), D_MODEL),
        "w_expert_down": nrm(ks[17], (DEPTH, N_EXPERTS, D_EXPERT, D_MODEL), D_EXPERT),
        "ple_norm": gain(ks[18], (DEPTH, D_MODEL)),
        "w_ple_gate": nrm(ks[19], (DEPTH, D_MODEL, D_MODEL), D_MODEL),
        "w_ple_proj": nrm(ks[20], (DEPTH, PLE_DIM, D_MODEL), PLE_DIM),
        "final_norm": gain(ks[21], (D_MODEL,)),
    }


def reference(x, p, attn_norm, w_in, w_out, w_cmp_k1, w_cmp_k2, w_cmp_v1, w_cmp_v2, cmp_pos,
              ffn_norm, w_route_group, b_route_group, w_route_expert, b_route_expert,
              w_expert_gate, w_expert_up, w_expert_down, ple_norm, w_ple_gate, w_ple_proj, final_norm):
    for i in range(DEPTH):
        h = rms_norm(x, attn_norm[i])
        x = x + mixing_layer(h, w_in[i], w_out[i], w_cmp_k1[i], w_cmp_k2[i], w_cmp_v1[i], w_cmp_v2[i], cmp_pos[i])
        h = rms_norm(x, ffn_norm[i])
        x = x + hierarchical_moe(h, w_route_group[i], b_route_group[i], w_route_expert[i], b_route_expert[i],
                                 w_expert_gate[i], w_expert_up[i], w_expert_down[i])
        gate = jax.nn.sigmoid(rms_norm(x, ple_norm[i]) @ w_ple_gate[i])
        x = x + gate * (p[i] @ w_ple_proj[i])
    return rms_norm(x, final_norm)
```

```python
import functools

import numpy as np
import jax
import jax.numpy as jnp
from jax import lax
from jax.experimental import pallas as pl
from jax.experimental.pallas import tpu as pltpu

D_MODEL = 1024
PLE_DIM = 256
HEAD_DIM = 64
N_HEADS_A = 8
N_HEADS_B = 8
N_KV_B = 2
GQA_REP = N_HEADS_B // N_KV_B
N_HEADS_TOTAL = N_HEADS_A + N_HEADS_B
DILATED_PATTERNS = ((128, 1), (512, 4), (2048, 16))
CMP_BLOCK = 32
CMP_STRIDE = 16
CMP_HIDDEN = 256
SEL_BLOCK = 64
SEL_TOP = 16
N_LOCAL_BLOCKS = 2
WINDOW = 512
N_GROUPS = 4
EXPERTS_PER_GROUP = 4
N_EXPERTS = N_GROUPS * EXPERTS_PER_GROUP
D_EXPERT = 512
RMS_EPS = 1e-6
NEG = -1e30
FORCE_BONUS = 1e4
SCALE = HEAD_DIM ** -0.5

LANES = 128
QB = 128
A_W = N_HEADS_A * HEAD_DIM
N_MAIN = 3 * A_W + N_HEADS_B * HEAD_DIM + 6 * N_KV_B * HEAD_DIM
N_PROJ = N_MAIN + N_KV_B * LANES
COL_QB = (3 * A_W) // (2 * LANES)
COL_KC = (3 * A_W + N_HEADS_B * HEAD_DIM) // LANES
COL_GATE = N_MAIN // LANES
VMEM_LIMIT = 56 * 1024 * 1024

f32 = jnp.float32
bf16 = jnp.bfloat16


def _dot(a, b):
    return jnp.dot(a, b, preferred_element_type=f32)


def _dot_nt(a, b):
    return lax.dot_general(a, b, (((1,), (1,)), ((), ())), preferred_element_type=f32)


def _rms(x, g):
    return x * lax.rsqrt(jnp.mean(x * x, axis=-1, keepdims=True) + RMS_EPS) * g


def _params(*sem):
    return pltpu.CompilerParams(dimension_semantics=sem, vmem_limit_bytes=VMEM_LIMIT)


def _in_proj_kernel(x_ref, g_ref, w_ref, o_ref):
    h = _rms(x_ref[...], g_ref[...]).astype(bf16)
    for n0 in range(0, N_PROJ, 512):
        o_ref[:, n0:n0 + 512] = _dot(h, w_ref[:, n0:n0 + 512])


def _in_proj(x2d, g, w):
    T = x2d.shape[0]
    tm = 512
    return pl.pallas_call(
        _in_proj_kernel,
        grid=(T // tm,),
        in_specs=[pl.BlockSpec((tm, D_MODEL), lambda i: (i, 0)),
                  pl.BlockSpec((1, D_MODEL), lambda i: (0, 0)),
                  pl.BlockSpec((D_MODEL, N_PROJ), lambda i: (0, 0))],
        out_specs=pl.BlockSpec((tm, N_PROJ), lambda i: (i, 0)),
        out_shape=jax.ShapeDtypeStruct((T, N_PROJ), f32),
        compiler_params=_params("parallel"),
        name="in_proj",
    )(x2d, g, w)


def _dil_kernel(slope_ref, q_ref, k_ref, v_ref, o_ref, num_ref, m_ref, l_ref, *, seq):
    hp = pl.program_id(1)
    lane = lax.broadcasted_iota(jnp.int32, (QB, LANES), 1)
    left = lane < HEAD_DIM
    slopes = (slope_ref[2 * hp], slope_ref[2 * hp + 1])

    def rel_of(nk):
        i = lax.broadcasted_iota(jnp.int32, (QB, nk), 0)
        j = lax.broadcasted_iota(jnp.int32, (QB, nk), 1)
        return i - j + (nk - QB)

    def block(p, dil, row0, key0, nk):
        qc = q_ref[pl.ds(row0, QB, stride=dil), :] * SCALE
        kc = k_ref[pl.ds(key0, nk, stride=dil), :].astype(bf16)
        vc = v_ref[pl.ds(key0, nk, stride=dil), :].astype(bf16)
        rel = rel_of(nk)
        valid = (rel >= 0) & (rel <= QB)
        relf = rel.astype(f32)
        res = []
        for hh in range(2):
            qm = jnp.where(left if hh == 0 else jnp.logical_not(left), qc, 0.0).astype(bf16)
            s = _dot_nt(qm, kc)
            s = jnp.where(valid, s - (slopes[hh] * float(dil)) * relf, NEG)
            mx = jnp.max(s, axis=1, keepdims=True)
            e = jnp.exp(s - mx)
            den = jnp.sum(e, axis=1, keepdims=True)
            num = _dot(e.astype(bf16), vc)
            res.append((num, mx, den))
        rows = pl.ds(row0, QB, stride=dil)
        num_ref[p, rows, :] = jnp.where(left, res[0][0], res[1][0])
        m_ref[p, rows, :] = jnp.where(left, res[0][1], res[1][1])
        l_ref[p, rows, :] = jnp.where(left, res[0][2], res[1][2])

    for p, (window, dil) in enumerate(DILATED_PATTERNS):
        assert window // dil == QB
        nblk = seq // dil // QB

        def per_class(r, _, p=p, dil=dil, nblk=nblk):
            block(p, dil, r, r, QB)

            def per_blk(a, _):
                block(p, dil, r + dil * QB * a, r + dil * QB * (a - 1), 2 * QB)
                return 0

            if nblk > 1:
                lax.fori_loop(1, nblk, per_blk, 0)
            return 0

        if dil == 1:
            per_class(0, 0)
        else:
            lax.fori_loop(0, dil, per_class, 0)

    ch = 256

    def combine(c, _):
        rows = pl.ds(pl.multiple_of(c * ch, ch), ch)
        ms = [m_ref[p, rows, :] for p in range(3)]
        big = jnp.maximum(jnp.maximum(ms[0], ms[1]), ms[2])
        num = jnp.zeros((ch, LANES), f32)
        den = jnp.zeros((ch, LANES), f32)
        for p in range(3):
            w = jnp.exp(ms[p] - big)
            num = num + w * num_ref[p, rows, :]
            den = den + w * l_ref[p, rows, :]
        o_ref[rows, :] = num / den
        return 0

    lax.fori_loop(0, seq // ch, combine, 0)


def _dilated(proj3, slopes_a):
    B, S, _ = proj3.shape
    npair = N_HEADS_A // 2
    blk = lambda off: pl.BlockSpec((None, S, LANES), lambda b, hp, off=off: (b, 0, off + hp))
    return pl.pallas_call(
        functools.partial(_dil_kernel, seq=S),
        grid=(B, npair),
        in_specs=[pl.BlockSpec(memory_space=pltpu.SMEM), blk(0), blk(npair), blk(2 * npair)],
        out_specs=pl.BlockSpec((None, S, LANES), lambda b, hp: (b, 0, hp)),
        out_shape=jax.ShapeDtypeStruct((B, S, A_W), f32),
        scratch_shapes=[pltpu.VMEM((3, S, LANES), f32)] * 3,
        compiler_params=_params("parallel", "parallel"),
        name="dilated_attn",
    )(slopes_a, proj3, proj3, proj3)


def _cmp_kernel(kc_ref, vc_ref, pos_ref, w1k_ref, w2k_ref, w1v_ref, w2v_ref, ko_ref, vo_ref, *, ncb):
    half = CMP_BLOCK // 2
    for x_ref, pi, w1_ref, w2_ref, o_ref in ((kc_ref, 0, w1k_ref, w2k_ref, ko_ref),
                                             (vc_ref, 1, w1v_ref, w2v_ref, vo_ref)):
        lo = jnp.zeros((ncb, 2 * CMP_HIDDEN), f32)
        hi = jnp.zeros((ncb, 2 * CMP_HIDDEN), f32)
        for r in range(half):
            x = x_ref[pl.ds(r, ncb, stride=CMP_STRIDE), :]
            lo = lo + _dot((x + pos_ref[pi, r:r + 1, :]).astype(bf16), w1_ref[r])
            hi = hi + _dot((x + pos_ref[pi, r + half:r + half + 1, :]).astype(bf16), w1_ref[r + half])
        h1 = lo + pltpu.roll(hi, ncb - 1, axis=0)
        o_ref[...] = _dot(jax.nn.gelu(h1).astype(bf16), w2_ref[...])


def _compress(proj3, pos_dup, w1k, w2k, w1v, w2v):
    B, S, _ = proj3.shape
    ncb = S // CMP_STRIDE
    full = lambda a: pl.BlockSpec(a.shape, lambda b: (0,) * a.ndim)
    out = jax.ShapeDtypeStruct((B, ncb, LANES), f32)
    return pl.pallas_call(
        functools.partial(_cmp_kernel, ncb=ncb),
        grid=(B,),
        in_specs=[pl.BlockSpec((None, S, LANES), lambda b: (b, 0, COL_KC)),
                  pl.BlockSpec((None, S, LANES), lambda b: (b, 0, COL_KC + 1)),
                  full(pos_dup), full(w1k), full(w2k), full(w1v), full(w2v)],
        out_specs=[pl.BlockSpec((None, ncb, LANES), lambda b: (b, 0, 0))] * 2,
        out_shape=[out, out],
        compiler_params=_params("parallel"),
        name="nsa_compress",
    )(proj3, proj3, pos_dup, w1k, w2k, w1v, w2v)


def _nsa_kernel(slope_ref, q_ref, ks_ref, vs_ref, kw_ref, vw_ref, gl_ref, kcmp_ref, vcmp_ref,
                ovt_ref, exp_ref, o_ref, ksd, vsd, kwd, vwd, kcd, vcd, *, seq):
    g = pl.program_id(1)
    qi = pl.program_id(2)
    nsel = seq // SEL_BLOCK
    ncb = seq // CMP_STRIDE

    @pl.when(qi == 0)
    def _prep():
        ch = 256
        keep = (lax.broadcasted_iota(jnp.int32, (ch, LANES), 1) // HEAD_DIM) == g
        for src, dst in ((ks_ref, ksd), (vs_ref, vsd), (kw_ref, kwd), (vw_ref, vwd)):
            for c in range(seq // ch):
                x = src[c * ch:(c + 1) * ch, :]
                dst[c * ch:(c + 1) * ch, :] = jnp.where(keep, x, pltpu.roll(x, HEAD_DIM, axis=1)).astype(bf16)
        keepc = (lax.broadcasted_iota(jnp.int32, (ncb, LANES), 1) // HEAD_DIM) == g
        for src, dst in ((kcmp_ref, kcd), (vcmp_ref, vcd)):
            x = src[...]
            dst[...] = jnp.where(keepc, x, pltpu.roll(x, HEAD_DIM, axis=1)).astype(bf16)

    lane = lax.broadcasted_iota(jnp.int32, (QB, LANES), 1)
    left = lane < HEAD_DIM
    ii = lax.broadcasted_iota(jnp.int32, (QB, LANES), 0)
    rel = (ii - lane).astype(f32)
    t_row = qi * QB + ii

    q_pairs = [q_ref[:, pr * LANES:(pr + 1) * LANES] * SCALE for pr in range(GQA_REP // 2)]

    def q_head(r):
        keep = left if r % 2 == 0 else jnp.logical_not(left)
        return jnp.where(keep, q_pairs[r // 2], 0.0).astype(bf16)

    slopes = [slope_ref[g * GQA_REP + r] for r in range(GQA_REP)]

    cmp_end = (lane * CMP_STRIDE + (CMP_BLOCK - 1))
    dist_c = (t_row - cmp_end).astype(f32)
    valid_c = dist_c >= 0
    o_cmp = []
    p_sum = jnp.zeros((QB, LANES), f32)
    for r in range(GQA_REP):
        s = _dot_nt(q_head(r), kcd[...])
        s = jnp.where(valid_c, s - slopes[r] * dist_c, NEG)
        e = jnp.exp(s - jnp.max(s, axis=1, keepdims=True))
        p = jnp.where(valid_c, e, 0.0) / jnp.sum(e, axis=1, keepdims=True)
        p_sum = p_sum + p
        o_cmp.append(_dot(p.astype(bf16), vcd[...]))

    p_hi = p_sum.astype(bf16)
    p_lo = (p_sum - p_hi.astype(f32)).astype(bf16)
    imp = _dot(p_hi, ovt_ref[...]) + _dot(p_lo, ovt_ref[...])
    back = t_row // SEL_BLOCK - lane
    valid_s = (back >= 0) & (lane < nsel)
    forced = (lane == 0) | (valid_s & (back < N_LOCAL_BLOCKS))
    score = jnp.where(valid_s, imp + jnp.where(forced, FORCE_BONUS, 0.0), NEG)
    score = jnp.where(lane < nsel, score, 2.0 * NEG)
    rank = jnp.zeros((QB, LANES), jnp.int32)
    for n in range(nsel):
        col = score[:, n:n + 1]
        ahead = (col > score) | ((col == score) & (lane > n))
        rank = rank + ahead.astype(jnp.int32)
    sel = jnp.where((rank < min(SEL_TOP, nsel)) & valid_s, 1.0, 0.0).astype(bf16)

    sig = jax.nn.sigmoid(gl_ref[...])

    def attend(r, k_ref, v_ref, lo, hi, window):
        qh = q_head(r)

        def body(kj, carry):
            m, l, acc = carry
            rows = pl.ds(pl.multiple_of(kj * QB, QB), QB)
            s = _dot_nt(qh, k_ref[rows, :])
            dist = rel + ((qi - kj) * QB).astype(f32)
            if window is None:
                ok = (_dot(sel, exp_ref[kj]) > 0.5) & (dist >= 0)
            else:
                ok = (dist >= 0) & (dist < window)
            s = jnp.where(ok, s - slopes[r] * dist, NEG)
            m_new = jnp.maximum(m, jnp.max(s, axis=1, keepdims=True))
            alpha = jnp.exp(m - m_new)
            e = jnp.exp(s - m_new)
            l = alpha * l + jnp.sum(e, axis=1, keepdims=True)
            acc = alpha * acc + _dot(e.astype(bf16), v_ref[rows, :])
            return m_new, l, acc

        init = (jnp.full((QB, 1), NEG, f32), jnp.zeros((QB, 1), f32), jnp.zeros((QB, LANES), f32))
        _, l, acc = lax.fori_loop(lo, hi, body, init)
        return acc / l

    outs = []
    for r in range(GQA_REP):
        o_s = attend(r, ksd, vsd, 0, qi + 1, None)
        o_w = attend(r, kwd, vwd, jnp.maximum(qi - WINDOW // QB, 0), qi + 1, WINDOW)
        outs.append(sig[:, 3 * r:3 * r + 1] * o_cmp[r] + sig[:, 3 * r + 1:3 * r + 2] * o_s
                    + sig[:, 3 * r + 2:3 * r + 3] * o_w)
    for pr in range(GQA_REP // 2):
        o_ref[:, pr * LANES:(pr + 1) * LANES] = jnp.where(left, outs[2 * pr], outs[2 * pr + 1])


def _nsa(proj3, kcmp, vcmp, slopes_b, ovt, expand):
    B, S, _ = proj3.shape
    ncb = kcmp.shape[1]
    kv = lambda j: pl.BlockSpec((None, S, LANES), lambda b, g, qi, j=j: (b, 0, COL_KC + j))
    full = lambda a: pl.BlockSpec(a.shape, lambda b, g, qi: (0,) * a.ndim)
    return pl.pallas_call(
        functools.partial(_nsa_kernel, seq=S),
        grid=(B, N_KV_B, S // QB),
        in_specs=[pl.BlockSpec(memory_space=pltpu.SMEM),
                  pl.BlockSpec((None, QB, 2 * LANES), lambda b, g, qi: (b, qi, COL_QB + g)),
                  kv(2), kv(3), kv(4), kv(5),
                  pl.BlockSpec((None, QB, LANES), lambda b, g, qi: (b, qi, COL_GATE + g)),
                  pl.BlockSpec((None, ncb, LANES), lambda b, g, qi: (b, 0, 0)),
                  pl.BlockSpec((None, ncb, LANES), lambda b, g, qi: (b, 0, 0)),
                  full(ovt), full(expand)],
        out_specs=pl.BlockSpec((None, QB, 2 * LANES), lambda b, g, qi: (b, qi, g)),
        out_shape=jax.ShapeDtypeStruct((B, S, N_HEADS_B * HEAD_DIM), f32),
        scratch_shapes=[pltpu.VMEM((S, LANES), bf16)] * 4 + [pltpu.VMEM((ncb, LANES), bf16)] * 2,
        compiler_params=_params("parallel", "parallel", "arbitrary"),
        name="nsa_attn",
    )(slopes_b, proj3, proj3, proj3, proj3, proj3, proj3, kcmp, vcmp, ovt, expand)


def _out_proj_kernel(x_ref, oa_ref, ob_ref, wa_ref, wb_ref, o_ref):
    o_ref[...] = (x_ref[...] + _dot(oa_ref[...].astype(bf16), wa_ref[...])
                  + _dot(ob_ref[...].astype(bf16), wb_ref[...]))


def _out_proj(x2d, oa, ob, wa, wb):
    T = x2d.shape[0]
    tm = 512
    row = lambda w: pl.BlockSpec((tm, w), lambda i: (i, 0))
    full = lambda a: pl.BlockSpec(a.shape, lambda i: (0,) * a.ndim)
    return pl.pallas_call(
        _out_proj_kernel,
        grid=(T // tm,),
        in_specs=[row(D_MODEL), row(oa.shape[1]), row(ob.shape[1]), full(wa), full(wb)],
        out_specs=row(D_MODEL),
        out_shape=jax.ShapeDtypeStruct((T, D_MODEL), f32),
        compiler_params=_params("parallel"),
        name="out_proj",
    )(x2d, oa, ob, wa, wb)


def _router_kernel(x_ref, g_ref, w_ref, b_ref, h_ref, gate_ref):
    h = _rms(x_ref[...], g_ref[...])
    h_ref[...] = h.astype(bf16)
    logit = jnp.dot(h, w_ref[...], preferred_element_type=f32, precision=lax.Precision.HIGHEST) + b_ref[...]
    tm = logit.shape[0]
    lane = lax.broadcasted_iota(jnp.int32, (tm, LANES), 1)
    big = jnp.int32(LANES)
    is_g = lane < N_GROUPS
    gl = jnp.where(is_g, logit, NEG)
    gmax = jnp.max(gl, axis=1, keepdims=True)
    gsum = jnp.sum(jnp.where(is_g, jnp.exp(gl - gmax), 0.0), axis=1, keepdims=True)
    gsel = jnp.min(jnp.where(is_g & (gl == gmax), lane, big), axis=1, keepdims=True)
    gw = 1.0 / gsum
    e_lane = lane - N_GROUPS
    in_grp = (e_lane >= 0) & (e_lane < N_EXPERTS) & (e_lane // EXPERTS_PER_GROUP == gsel)
    el = jnp.where(in_grp, logit, NEG)
    t1 = jnp.max(el, axis=1, keepdims=True)
    i1 = jnp.min(jnp.where(in_grp & (el == t1), lane, big), axis=1, keepdims=True)
    el2 = jnp.where(lane == i1, NEG, el)
    t2 = jnp.max(el2, axis=1, keepdims=True)
    i2 = jnp.min(jnp.where(in_grp & (lane != i1) & (el2 == t2), lane, big), axis=1, keepdims=True)
    e2 = jnp.exp(t2 - t1)
    w1 = gw / (1.0 + e2)
    w2 = gw * e2 / (1.0 + e2)
    gate_ref[...] = jnp.where(lane == i1, w1, jnp.where(lane == i2, w2, 0.0))


def _router(x2d, g, w, b):
    T = x2d.shape[0]
    tm = 512
    return pl.pallas_call(
        _router_kernel,
        grid=(T // tm,),
        in_specs=[pl.BlockSpec((tm, D_MODEL), lambda i: (i, 0)),
                  pl.BlockSpec((1, D_MODEL), lambda i: (0, 0)),
                  pl.BlockSpec((D_MODEL, LANES), lambda i: (0, 0)),
                  pl.BlockSpec((1, LANES), lambda i: (0, 0))],
        out_specs=[pl.BlockSpec((tm, D_MODEL), lambda i: (i, 0)),
                   pl.BlockSpec((tm, LANES), lambda i: (i, 0))],
        out_shape=[jax.ShapeDtypeStruct((T, D_MODEL), bf16), jax.ShapeDtypeStruct((T, LANES), f32)],
        compiler_params=_params("parallel"),
        name="router",
    )(x2d, g, w, b)


def _moe_kernel(x_ref, h_ref, gate_ref, wg_ref, wu_ref, wd_ref, o_ref):
    e = pl.program_id(1)

    @pl.when(e == 0)
    def _init():
        o_ref[...] = x_ref[...]

    h = h_ref[...]
    lane = lax.broadcasted_iota(jnp.int32, gate_ref.shape, 1)
    gcol = jnp.sum(jnp.where(lane == e + N_GROUPS, gate_ref[...], 0.0), axis=1, keepdims=True)
    a = jax.nn.silu(_dot(h, wg_ref[...])) * _dot(h, wu_ref[...])
    o_ref[...] += _dot((a * gcol).astype(bf16), wd_ref[...])


def _moe(x2d, h, gates, wg, wu, wd, layer):
    T = x2d.shape[0]
    tm = 1024
    return pl.pallas_call(
        _moe_kernel,
        grid=(T // tm, N_EXPERTS),
        in_specs=[pl.BlockSpec((tm, D_MODEL), lambda i, e: (i, 0)),
                  pl.BlockSpec((tm, D_MODEL), lambda i, e: (i, 0)),
                  pl.BlockSpec((tm, LANES), lambda i, e: (i, 0)),
                  pl.BlockSpec((None, None, D_MODEL, D_EXPERT), lambda i, e: (layer, e, 0, 0)),
                  pl.BlockSpec((None, None, D_MODEL, D_EXPERT), lambda i, e: (layer, e, 0, 0)),
                  pl.BlockSpec((None, None, D_EXPERT, D_MODEL), lambda i, e: (layer, e, 0, 0))],
        out_specs=pl.BlockSpec((tm, D_MODEL), lambda i, e: (i, 0)),
        out_shape=jax.ShapeDtypeStruct((T, D_MODEL), f32),
        compiler_params=_params("parallel", "arbitrary"),
        name="moe_ffn",
    )(x2d, h, gates, wg, wu, wd)


def _ple_kernel(x_ref, p_ref, g_ref, wg_ref, wp_ref, fg_ref, o_ref, *, final):
    x = x_ref[...]
    gate = jax.nn.sigmoid(_dot(_rms(x, g_ref[...]).astype(bf16), wg_ref[...]))
    y = x + gate * _dot(p_ref[...].astype(bf16), wp_ref[...])
    o_ref[...] = _rms(y, fg_ref[...]) if final else y


def _ple(x2d, p3, g, wg, wp, fg, layer, final):
    T = x2d.shape[0]
    tm = 512
    full = lambda a: pl.BlockSpec(a.shape, lambda i: (0,) * a.ndim)
    return pl.pallas_call(
        functools.partial(_ple_kernel, final=final),
        grid=(T // tm,),
        in_specs=[pl.BlockSpec((tm, D_MODEL), lambda i: (i, 0)),
                  pl.BlockSpec((None, tm, PLE_DIM), lambda i: (layer, i, 0)),
                  full(g), full(wg), full(wp), full(fg)],
        out_specs=pl.BlockSpec((tm, D_MODEL), lambda i: (i, 0)),
        out_shape=jax.ShapeDtypeStruct((T, D_MODEL), f32),
        compiler_params=_params("parallel"),
        name="ple",
    )(x2d, p3, g, wg, wp, fg)


def _alibi_slopes():
    s = 2.0 ** (-8.0 * np.arange(1, N_HEADS_TOTAL + 1) / N_HEADS_TOTAL)
    return jnp.asarray(s[0::2], f32), jnp.asarray(s[1::2], f32)


def _selection_constants(seq):
    ncb = seq // CMP_STRIDE
    nsel = seq // SEL_BLOCK
    n_cmp = (seq - CMP_BLOCK) // CMP_STRIDE + 1
    cs = np.arange(n_cmp) * CMP_STRIDE
    bs = np.arange(nsel) * SEL_BLOCK
    ov = np.clip(np.minimum(cs[:, None] + CMP_BLOCK, bs[None, :] + SEL_BLOCK)
                 - np.maximum(cs[:, None], bs[None, :]), 0, None) / CMP_BLOCK
    ovt = np.zeros((ncb, LANES), np.float32)
    ovt[:n_cmp, :nsel] = ov
    key_blk = (np.arange(seq) // SEL_BLOCK).reshape(seq // QB, 1, QB)
    expand = (key_blk == np.arange(LANES).reshape(1, LANES, 1)).astype(np.float32)
    return jnp.asarray(ovt, bf16), jnp.asarray(expand, bf16)


def _block_diag2(w):
    z = jnp.zeros_like(w)
    return jnp.concatenate([jnp.concatenate([w, z], axis=-1), jnp.concatenate([z, w], axis=-1)], axis=-2)


def _layout_w_in(w):
    gate = w[:, N_MAIN:]
    per = GQA_REP * 3
    blocks = [jnp.pad(gate[:, g * per:(g + 1) * per], ((0, 0), (0, LANES - per))) for g in range(N_KV_B)]
    return jnp.concatenate([w[:, :N_MAIN]] + blocks, axis=1).astype(bf16)


def kernel(x, p, attn_norm, w_in, w_out, w_cmp_k1, w_cmp_k2, w_cmp_v1, w_cmp_v2, cmp_pos, ffn_norm, w_route_group, b_route_group, w_route_expert, b_route_expert, w_expert_gate, w_expert_up, w_expert_down, ple_norm, w_ple_gate, w_ple_proj, final_norm):
    B, S, D = x.shape
    depth = w_in.shape[0]
    T = B * S
    slopes_a, slopes_b = _alibi_slopes()
    ovt, expand = _selection_constants(S)
    wg_all = w_expert_gate.astype(bf16)
    wu_all = w_expert_up.astype(bf16)
    wd_all = w_expert_down.astype(bf16)
    p3 = p.reshape(depth, T, PLE_DIM)
    row = lambda v: v.reshape(1, -1)
    n_route = N_GROUPS + N_EXPERTS

    x2d = x.reshape(T, D)
    for i in range(depth):
        proj3 = _in_proj(x2d, row(attn_norm[i]), _layout_w_in(w_in[i])).reshape(B, S, N_PROJ)
        oa = _dilated(proj3, slopes_a)
        w1 = lambda w: _block_diag2(w.reshape(CMP_BLOCK, HEAD_DIM, CMP_HIDDEN)).astype(bf16)
        pos_dup = jnp.concatenate([cmp_pos[i], cmp_pos[i]], axis=-1)
        kcmp, vcmp = _compress(proj3, pos_dup, w1(w_cmp_k1[i]), _block_diag2(w_cmp_k2[i]).astype(bf16),
                               w1(w_cmp_v1[i]), _block_diag2(w_cmp_v2[i]).astype(bf16))
        ob = _nsa(proj3, kcmp, vcmp, slopes_b, ovt, expand)
        wo = w_out[i].astype(bf16)
        x2d = _out_proj(x2d, oa.reshape(T, A_W), ob.reshape(T, -1), wo[:A_W], wo[A_W:])
        w_route = jnp.pad(jnp.concatenate([w_route_group[i], w_route_expert[i]], axis=1),
                          ((0, 0), (0, LANES - n_route)))
        b_route = jnp.pad(jnp.concatenate([b_route_group[i], b_route_expert[i]]), (0, LANES - n_route))
        h, gates = _router(x2d, row(ffn_norm[i]), w_route, row(b_route))
        x2d = _moe(x2d, h, gates, wg_all, wu_all, wd_all, i)
        x2d = _ple(x2d, p3, row(ple_norm[i]), w_ple_gate[i].astype(bf16), w_ple_proj[i].astype(bf16),
                   row(final_norm), i, i == depth - 1)
    return x2d.reshape(B, S, D)
```

```python
import functools

import numpy as np
import jax
import jax.numpy as jnp
from jax import lax
from jax.experimental import pallas as pl
from jax.experimental.pallas import tpu as pltpu

D_MODEL = 1024
PLE_DIM = 256
HEAD_DIM = 64
N_HEADS_A = 8
N_HEADS_B = 8
N_KV_B = 2
GQA_REP = N_HEADS_B // N_KV_B
N_HEADS_TOTAL = N_HEADS_A + N_HEADS_B
DILATED_PATTERNS = ((128, 1), (512, 4), (2048, 16))
CMP_BLOCK = 32
CMP_STRIDE = 16
CMP_HIDDEN = 256
SEL_BLOCK = 64
SEL_TOP = 16
N_LOCAL_BLOCKS = 2
WINDOW = 512
N_GROUPS = 4
EXPERTS_PER_GROUP = 4
N_EXPERTS = N_GROUPS * EXPERTS_PER_GROUP
D_EXPERT = 512
RMS_EPS = 1e-6
NEG = -1e30
FORCE_BONUS = 1e4
SCALE = HEAD_DIM ** -0.5

LANES = 128
QB = 128
SEL_CHUNK = 512
A_W = N_HEADS_A * HEAD_DIM
N_MAIN = 3 * A_W + N_HEADS_B * HEAD_DIM + 6 * N_KV_B * HEAD_DIM
N_PROJ = N_MAIN + N_KV_B * LANES
COL_QB = (3 * A_W) // (2 * LANES)
COL_KC = (3 * A_W + N_HEADS_B * HEAD_DIM) // LANES
COL_GATE = N_MAIN // LANES
VMEM_LIMIT = 56 * 1024 * 1024

f32 = jnp.float32
bf16 = jnp.bfloat16


def _dot(a, b):
    return jnp.dot(a, b, preferred_element_type=f32)


def _dot_nt(a, b):
    return lax.dot_general(a, b, (((1,), (1,)), ((), ())), preferred_element_type=f32)


def _rms(x, g):
    return x * lax.rsqrt(jnp.mean(x * x, axis=-1, keepdims=True) + RMS_EPS) * g


def _params(*sem):
    return pltpu.CompilerParams(dimension_semantics=sem, vmem_limit_bytes=VMEM_LIMIT)


def _in_proj_kernel(x_ref, g_ref, w_ref, o_ref):
    h = _rms(x_ref[...], g_ref[...]).astype(bf16)
    for n0 in range(0, N_PROJ, 512):
        o_ref[:, n0:n0 + 512] = _dot(h, w_ref[:, n0:n0 + 512])


def _in_proj(x2d, g, w):
    T = x2d.shape[0]
    tm = 512
    return pl.pallas_call(
        _in_proj_kernel,
        grid=(T // tm,),
        in_specs=[pl.BlockSpec((tm, D_MODEL), lambda i: (i, 0)),
                  pl.BlockSpec((1, D_MODEL), lambda i: (0, 0)),
                  pl.BlockSpec((D_MODEL, N_PROJ), lambda i: (0, 0))],
        out_specs=pl.BlockSpec((tm, N_PROJ), lambda i: (i, 0)),
        out_shape=jax.ShapeDtypeStruct((T, N_PROJ), f32),
        compiler_params=_params("parallel"),
        name="in_proj",
    )(x2d, g, w)


def _dil_kernel(slope_ref, q_ref, k_ref, v_ref, o_ref, num_ref, m_ref, l_ref, *, seq):
    hp = pl.program_id(1)
    lane = lax.broadcasted_iota(jnp.int32, (QB, LANES), 1)
    left = lane < HEAD_DIM
    slopes = (slope_ref[2 * hp], slope_ref[2 * hp + 1])

    def rel_of(nk):
        i = lax.broadcasted_iota(jnp.int32, (QB, nk), 0)
        j = lax.broadcasted_iota(jnp.int32, (QB, nk), 1)
        return i - j + (nk - QB)

    def block(p, dil, row0, key0, nk):
        qc = q_ref[pl.ds(row0, QB, stride=dil), :] * SCALE
        kc = k_ref[pl.ds(key0, nk, stride=dil), :].astype(bf16)
        vc = v_ref[pl.ds(key0, nk, stride=dil), :].astype(bf16)
        rel = rel_of(nk)
        valid = (rel >= 0) & (rel <= QB)
        relf = rel.astype(f32)
        q2 = jnp.concatenate([jnp.where(left, qc, 0.0), jnp.where(left, 0.0, qc)], axis=0).astype(bf16)
        s2 = _dot_nt(q2, kc)
        es, ms, ls = [], [], []
        for hh in range(2):
            s = jnp.where(valid, s2[hh * QB:(hh + 1) * QB] - (slopes[hh] * float(dil)) * relf, NEG)
            mx = jnp.max(s, axis=1, keepdims=True)
            e = jnp.exp(s - mx)
            ms.append(mx)
            ls.append(jnp.sum(e, axis=1, keepdims=True))
            es.append(e.astype(bf16))
        num2 = _dot(jnp.concatenate(es, axis=0), vc)
        rows = pl.ds(row0, QB, stride=dil)
        num_ref[p, rows, :] = jnp.where(left, num2[:QB], num2[QB:])
        m_ref[p, rows, :] = jnp.where(left, ms[0], ms[1])
        l_ref[p, rows, :] = jnp.where(left, ls[0], ls[1])

    for p, (window, dil) in enumerate(DILATED_PATTERNS):
        assert window // dil == QB
        nblk = seq // dil // QB

        def per_class(r, _, p=p, dil=dil, nblk=nblk):
            block(p, dil, r, r, QB)

            def per_blk(a, _):
                block(p, dil, r + dil * QB * a, r + dil * QB * (a - 1), 2 * QB)
                return 0

            if nblk > 1:
                lax.fori_loop(1, nblk, per_blk, 0, unroll=3)
            return 0

        if dil == 1:
            per_class(0, 0)
        else:
            lax.fori_loop(0, dil, per_class, 0, unroll=4 if nblk == 1 else 1)

    ch = 256

    def combine(c, _):
        rows = pl.ds(pl.multiple_of(c * ch, ch), ch)
        ms = [m_ref[p, rows, :] for p in range(3)]
        big = jnp.maximum(jnp.maximum(ms[0], ms[1]), ms[2])
        num = jnp.zeros((ch, LANES), f32)
        den = jnp.zeros((ch, LANES), f32)
        for p in range(3):
            w = jnp.exp(ms[p] - big)
            num = num + w * num_ref[p, rows, :]
            den = den + w * l_ref[p, rows, :]
        o_ref[rows, :] = num / den
        return 0

    lax.fori_loop(0, seq // ch, combine, 0)


def _dilated(proj3, slopes_a):
    B, S, _ = proj3.shape
    npair = N_HEADS_A // 2
    blk = lambda off: pl.BlockSpec((None, S, LANES), lambda b, hp, off=off: (b, 0, off + hp))
    return pl.pallas_call(
        functools.partial(_dil_kernel, seq=S),
        grid=(B, npair),
        in_specs=[pl.BlockSpec(memory_space=pltpu.SMEM), blk(0), blk(npair), blk(2 * npair)],
        out_specs=pl.BlockSpec((None, S, LANES), lambda b, hp: (b, 0, hp)),
        out_shape=jax.ShapeDtypeStruct((B, S, A_W), f32),
        scratch_shapes=[pltpu.VMEM((3, S, LANES), f32)] * 3,
        compiler_params=_params("parallel", "parallel"),
        name="dilated_attn",
    )(slopes_a, proj3, proj3, proj3)


def _cmp_kernel(kc_ref, vc_ref, pos_ref, w1k_ref, w2k_ref, w1v_ref, w2v_ref, ko_ref, vo_ref, *, ncb):
    half = CMP_BLOCK // 2
    for x_ref, pi, w1_ref, w2_ref, o_ref in ((kc_ref, 0, w1k_ref, w2k_ref, ko_ref),
                                             (vc_ref, 1, w1v_ref, w2v_ref, vo_ref)):
        lo = jnp.zeros((ncb, 2 * CMP_HIDDEN), f32)
        hi = jnp.zeros((ncb, 2 * CMP_HIDDEN), f32)
        for r in range(half):
            x = x_ref[pl.ds(r, ncb, stride=CMP_STRIDE), :]
            lo = lo + _dot((x + pos_ref[pi, r:r + 1, :]).astype(bf16), w1_ref[r])
            hi = hi + _dot((x + pos_ref[pi, r + half:r + half + 1, :]).astype(bf16), w1_ref[r + half])
        h1 = lo + pltpu.roll(hi, ncb - 1, axis=0)
        o_ref[...] = _dot(jax.nn.gelu(h1).astype(bf16), w2_ref[...])


def _compress(proj3, pos_dup, w1k, w2k, w1v, w2v):
    B, S, _ = proj3.shape
    ncb = S // CMP_STRIDE
    full = lambda a: pl.BlockSpec(a.shape, lambda b: (0,) * a.ndim)
    out = jax.ShapeDtypeStruct((B, ncb, LANES), f32)
    return pl.pallas_call(
        functools.partial(_cmp_kernel, ncb=ncb),
        grid=(B,),
        in_specs=[pl.BlockSpec((None, S, LANES), lambda b: (b, 0, COL_KC)),
                  pl.BlockSpec((None, S, LANES), lambda b: (b, 0, COL_KC + 1)),
                  full(pos_dup), full(w1k), full(w2k), full(w1v), full(w2v)],
        out_specs=[pl.BlockSpec((None, ncb, LANES), lambda b: (b, 0, 0))] * 2,
        out_shape=[out, out],
        compiler_params=_params("parallel"),
        name="nsa_compress",
    )(proj3, proj3, pos_dup, w1k, w2k, w1v, w2v)


def _nsa_kernel(slope_ref, q_ref, ks_ref, vs_ref, kw_ref, vw_ref, gl_ref, kcmp_ref, vcmp_ref,
                ovt_ref, exp_ref, o_ref, ksd, vsd, kwd, vwd, kcd, vcd, *, seq):
    g = pl.program_id(1)
    qi = pl.program_id(2)
    nsel = seq // SEL_BLOCK
    ncb = seq // CMP_STRIDE

    @pl.when(qi == 0)
    def _prep():
        ch = 256
        keep = (lax.broadcasted_iota(jnp.int32, (ch, LANES), 1) // HEAD_DIM) == g
        for src, dst in ((ks_ref, ksd), (vs_ref, vsd), (kw_ref, kwd), (vw_ref, vwd)):
            for c in range(seq // ch):
                x = src[c * ch:(c + 1) * ch, :]
                dst[c * ch:(c + 1) * ch, :] = jnp.where(keep, x, pltpu.roll(x, HEAD_DIM, axis=1)).astype(bf16)
        keepc = (lax.broadcasted_iota(jnp.int32, (ncb, LANES), 1) // HEAD_DIM) == g
        for src, dst in ((kcmp_ref, kcd), (vcmp_ref, vcd)):
            x = src[...]
            dst[...] = jnp.where(keepc, x, pltpu.roll(x, HEAD_DIM, axis=1)).astype(bf16)

    lane = lax.broadcasted_iota(jnp.int32, (QB, LANES), 1)
    left = lane < HEAD_DIM
    ii = lax.broadcasted_iota(jnp.int32, (QB, LANES), 0)
    t_row = qi * QB + ii

    q_pairs = [q_ref[:, pr * LANES:(pr + 1) * LANES] * SCALE for pr in range(GQA_REP // 2)]

    def q_head(r):
        keep = left if r % 2 == 0 else jnp.logical_not(left)
        return jnp.where(keep, q_pairs[r // 2], 0.0)

    q4 = jnp.concatenate([q_head(r) for r in range(GQA_REP)], axis=0).astype(bf16)
    head = lambda a, r: a[r * QB:(r + 1) * QB]
    slopes = [slope_ref[g * GQA_REP + r] for r in range(GQA_REP)]

    cmp_end = (lane * CMP_STRIDE + (CMP_BLOCK - 1))
    dist_c = (t_row - cmp_end).astype(f32)
    valid_c = dist_c >= 0
    s4 = _dot_nt(q4, kcd[...])
    ps = []
    p_sum = jnp.zeros((QB, LANES), f32)
    for r in range(GQA_REP):
        s = jnp.where(valid_c, head(s4, r) - slopes[r] * dist_c, NEG)
        e = jnp.exp(s - jnp.max(s, axis=1, keepdims=True))
        p = jnp.where(valid_c, e, 0.0) / jnp.sum(e, axis=1, keepdims=True)
        p_sum = p_sum + p
        ps.append(p.astype(bf16))
    o_cmp4 = _dot(jnp.concatenate(ps, axis=0), vcd[...])

    p_hi = p_sum.astype(bf16)
    p_lo = (p_sum - p_hi.astype(f32)).astype(bf16)
    imp = _dot(p_hi, ovt_ref[...]) + _dot(p_lo, ovt_ref[...])
    back = t_row // SEL_BLOCK - lane
    valid_s = (back >= 0) & (lane < nsel)
    forced = (lane == 0) | (valid_s & (back < N_LOCAL_BLOCKS))
    score = jnp.where(valid_s, imp + jnp.where(forced, FORCE_BONUS, 0.0), NEG)
    score = jnp.where(lane < nsel, score, 2.0 * NEG)
    rank = jnp.zeros((QB, LANES), jnp.int32)
    for n in range(nsel):
        col = score[:, n:n + 1]
        ahead = (col > score) | ((col == score) & (lane > n))
        rank = rank + ahead.astype(jnp.int32)
    sel = jnp.where((rank < min(SEL_TOP, nsel)) & valid_s, 1.0, 0.0).astype(bf16)

    sig = jax.nn.sigmoid(gl_ref[...])

    def rel_tile(nk, offset):
        i = lax.broadcasted_iota(jnp.int32, (QB, nk), 0)
        j = lax.broadcasted_iota(jnp.int32, (QB, nk), 1)
        return (i - j + offset).astype(f32)

    nwin = WINDOW + QB
    w0 = pl.multiple_of(jnp.maximum(qi - WINDOW // QB, 0) * QB, QB)
    dist_w = rel_tile(nwin, qi * QB - w0)
    ok_w = (dist_w >= 0) & (dist_w < WINDOW)
    s4 = _dot_nt(q4, kwd[pl.ds(w0, nwin), :])
    es, l_w = [], []
    for r in range(GQA_REP):
        s = jnp.where(ok_w, head(s4, r) - slopes[r] * dist_w, NEG)
        e = jnp.exp(s - jnp.max(s, axis=1, keepdims=True))
        l_w.append(jnp.sum(e, axis=1, keepdims=True))
        es.append(e.astype(bf16))
    o_w4 = _dot(jnp.concatenate(es, axis=0), vwd[pl.ds(w0, nwin), :])

    def sel_body(kc, carry):
        m4, l4, acc4 = carry
        rows = pl.ds(pl.multiple_of(kc * SEL_CHUNK, SEL_CHUNK), SEL_CHUNK)
        s4 = _dot_nt(q4, ksd[rows, :])
        dist = rel_tile(SEL_CHUNK, qi * QB - kc * SEL_CHUNK)
        ok = (_dot(sel, exp_ref[kc]) > 0.5) & (dist >= 0)
        ms, ls, al, es = [], [], [], []
        for r in range(GQA_REP):
            s = jnp.where(ok, head(s4, r) - slopes[r] * dist, NEG)
            m_old = head(m4, r)
            m_new = jnp.maximum(m_old, jnp.max(s, axis=1, keepdims=True))
            alpha = jnp.exp(m_old - m_new)
            e = jnp.exp(s - m_new)
            ms.append(m_new)
            al.append(alpha)
            ls.append(alpha * head(l4, r) + jnp.sum(e, axis=1, keepdims=True))
            es.append(e.astype(bf16))
        acc4 = jnp.concatenate(al, axis=0) * acc4 + _dot(jnp.concatenate(es, axis=0), vsd[rows, :])
        return jnp.concatenate(ms, axis=0), jnp.concatenate(ls, axis=0), acc4

    nrow = GQA_REP * QB
    init = (jnp.full((nrow, 1), NEG, f32), jnp.zeros((nrow, 1), f32), jnp.zeros((nrow, LANES), f32))
    per = SEL_CHUNK // QB
    _, l_s4, o_s4 = lax.fori_loop(0, (qi + per) // per, sel_body, init)

    outs = []
    for r in range(GQA_REP):
        outs.append(sig[:, 3 * r:3 * r + 1] * head(o_cmp4, r)
                    + sig[:, 3 * r + 1:3 * r + 2] * (head(o_s4, r) / head(l_s4, r))
                    + sig[:, 3 * r + 2:3 * r + 3] * (head(o_w4, r) / l_w[r]))
    for pr in range(GQA_REP // 2):
        o_ref[:, pr * LANES:(pr + 1) * LANES] = jnp.where(left, outs[2 * pr], outs[2 * pr + 1])


def _nsa(proj3, kcmp, vcmp, slopes_b, ovt, expand):
    B, S, _ = proj3.shape
    ncb = kcmp.shape[1]
    kv = lambda j: pl.BlockSpec((None, S, LANES), lambda b, g, qi, j=j: (b, 0, COL_KC + j))
    full = lambda a: pl.BlockSpec(a.shape, lambda b, g, qi: (0,) * a.ndim)
    return pl.pallas_call(
        functools.partial(_nsa_kernel, seq=S),
        grid=(B, N_KV_B, S // QB),
        in_specs=[pl.BlockSpec(memory_space=pltpu.SMEM),
                  pl.BlockSpec((None, QB, 2 * LANES), lambda b, g, qi: (b, qi, COL_QB + g)),
                  kv(2), kv(3), kv(4), kv(5),
                  pl.BlockSpec((None, QB, LANES), lambda b, g, qi: (b, qi, COL_GATE + g)),
                  pl.BlockSpec((None, ncb, LANES), lambda b, g, qi: (b, 0, 0)),
                  pl.BlockSpec((None, ncb, LANES), lambda b, g, qi: (b, 0, 0)),
                  full(ovt), full(expand)],
        out_specs=pl.BlockSpec((None, QB, 2 * LANES), lambda b, g, qi: (b, qi, g)),
        out_shape=jax.ShapeDtypeStruct((B, S, N_HEADS_B * HEAD_DIM), f32),
        scratch_shapes=[pltpu.VMEM((S, LANES), bf16)] * 4 + [pltpu.VMEM((ncb, LANES), bf16)] * 2,
        compiler_params=_params("parallel", "parallel", "arbitrary"),
        name="nsa_attn",
    )(slopes_b, proj3, proj3, proj3, proj3, proj3, proj3, kcmp, vcmp, ovt, expand)


def _out_proj_kernel(x_ref, oa_ref, ob_ref, wa_ref, wb_ref, o_ref):
    o_ref[...] = (x_ref[...] + _dot(oa_ref[...].astype(bf16), wa_ref[...])
                  + _dot(ob_ref[...].astype(bf16), wb_ref[...]))


def _out_proj(x2d, oa, ob, wa, wb):
    T = x2d.shape[0]
    tm = 512
    row = lambda w: pl.BlockSpec((tm, w), lambda i: (i, 0))
    full = lambda a: pl.BlockSpec(a.shape, lambda i: (0,) * a.ndim)
    return pl.pallas_call(
        _out_proj_kernel,
        grid=(T // tm,),
        in_specs=[row(D_MODEL), row(oa.shape[1]), row(ob.shape[1]), full(wa), full(wb)],
        out_specs=row(D_MODEL),
        out_shape=jax.ShapeDtypeStruct((T, D_MODEL), f32),
        compiler_params=_params("parallel"),
        name="out_proj",
    )(x2d, oa, ob, wa, wb)


def _router_kernel(x_ref, g_ref, w_ref, b_ref, h_ref, gate_ref):
    h = _rms(x_ref[...], g_ref[...])
    h_ref[...] = h.astype(bf16)
    logit = jnp.dot(h, w_ref[...], preferred_element_type=f32, precision=lax.Precision.HIGHEST) + b_ref[...]
    tm = logit.shape[0]
    lane = lax.broadcasted_iota(jnp.int32, (tm, LANES), 1)
    big = jnp.int32(LANES)
    is_g = lane < N_GROUPS
    gl = jnp.where(is_g, logit, NEG)
    gmax = jnp.max(gl, axis=1, keepdims=True)
    gsum = jnp.sum(jnp.where(is_g, jnp.exp(gl - gmax), 0.0), axis=1, keepdims=True)
    gsel = jnp.min(jnp.where(is_g & (gl == gmax), lane, big), axis=1, keepdims=True)
    gw = 1.0 / gsum
    e_lane = lane - N_GROUPS
    in_grp = (e_lane >= 0) & (e_lane < N_EXPERTS) & (e_lane // EXPERTS_PER_GROUP == gsel)
    el = jnp.where(in_grp, logit, NEG)
    t1 = jnp.max(el, axis=1, keepdims=True)
    i1 = jnp.min(jnp.where(in_grp & (el == t1), lane, big), axis=1, keepdims=True)
    el2 = jnp.where(lane == i1, NEG, el)
    t2 = jnp.max(el2, axis=1, keepdims=True)
    i2 = jnp.min(jnp.where(in_grp & (lane != i1) & (el2 == t2), lane, big), axis=1, keepdims=True)
    e2 = jnp.exp(t2 - t1)
    w1 = gw / (1.0 + e2)
    w2 = gw * e2 / (1.0 + e2)
    gate_ref[...] = jnp.where(lane == i1, w1, jnp.where(lane == i2, w2, 0.0))


def _router(x2d, g, w, b):
    T = x2d.shape[0]
    tm = 512
    return pl.pallas_call(
        _router_kernel,
        grid=(T // tm,),
        in_specs=[pl.BlockSpec((tm, D_MODEL), lambda i: (i, 0)),
                  pl.BlockSpec((1, D_MODEL), lambda i: (0, 0)),
                  pl.BlockSpec((D_MODEL, LANES), lambda i: (0, 0)),
                  pl.BlockSpec((1, LANES), lambda i: (0, 0))],
        out_specs=[pl.BlockSpec((tm, D_MODEL), lambda i: (i, 0)),
                   pl.BlockSpec((tm, LANES), lambda i: (i, 0))],
        out_shape=[jax.ShapeDtypeStruct((T, D_MODEL), bf16), jax.ShapeDtypeStruct((T, LANES), f32)],
        compiler_params=_params("parallel"),
        name="router",
    )(x2d, g, w, b)


def _moe_kernel(x_ref, h_ref, gate_ref, wg_ref, wu_ref, wd_ref, o_ref):
    e = pl.program_id(1)

    @pl.when(e == 0)
    def _init():
        o_ref[...] = x_ref[...]

    h = h_ref[...]
    lane = lax.broadcasted_iota(jnp.int32, gate_ref.shape, 1)
    gcol = jnp.sum(jnp.where(lane == e + N_GROUPS, gate_ref[...], 0.0), axis=1, keepdims=True)
    a = jax.nn.silu(_dot(h, wg_ref[...])) * _dot(h, wu_ref[...])
    o_ref[...] += _dot((a * gcol).astype(bf16), wd_ref[...])


def _moe(x2d, h, gates, wg, wu, wd, layer):
    T = x2d.shape[0]
    tm = 1024
    return pl.pallas_call(
        _moe_kernel,
        grid=(T // tm, N_EXPERTS),
        in_specs=[pl.BlockSpec((tm, D_MODEL), lambda i, e: (i, 0)),
                  pl.BlockSpec((tm, D_MODEL), lambda i, e: (i, 0)),
                  pl.BlockSpec((tm, LANES), lambda i, e: (i, 0)),
                  pl.BlockSpec((None, None, D_MODEL, D_EXPERT), lambda i, e: (layer, e, 0, 0)),
                  pl.BlockSpec((None, None, D_MODEL, D_EXPERT), lambda i, e: (layer, e, 0, 0)),
                  pl.BlockSpec((None, None, D_EXPERT, D_MODEL), lambda i, e: (layer, e, 0, 0))],
        out_specs=pl.BlockSpec((tm, D_MODEL), lambda i, e: (i, 0)),
        out_shape=jax.ShapeDtypeStruct((T, D_MODEL), f32),
        compiler_params=_params("parallel", "arbitrary"),
        name="moe_ffn",
    )(x2d, h, gates, wg, wu, wd)


def _ple_kernel(x_ref, p_ref, g_ref, wg_ref, wp_ref, fg_ref, o_ref, *, final):
    x = x_ref[...]
    gate = jax.nn.sigmoid(_dot(_rms(x, g_ref[...]).astype(bf16), wg_ref[...]))
    y = x + gate * _dot(p_ref[...].astype(bf16), wp_ref[...])
    o_ref[...] = _rms(y, fg_ref[...]) if final else y


def _ple(x2d, p3, g, wg, wp, fg, layer, final):
    T = x2d.shape[0]
    tm = 512
    full = lambda a: pl.BlockSpec(a.shape, lambda i: (0,) * a.ndim)
    return pl.pallas_call(
        functools.partial(_ple_kernel, final=final),
        grid=(T // tm,),
        in_specs=[pl.BlockSpec((tm, D_MODEL), lambda i: (i, 0)),
                  pl.BlockSpec((None, tm, PLE_DIM), lambda i: (layer, i, 0)),
                  full(g), full(wg), full(wp), full(fg)],
        out_specs=pl.BlockSpec((tm, D_MODEL), lambda i: (i, 0)),
        out_shape=jax.ShapeDtypeStruct((T, D_MODEL), f32),
        compiler_params=_params("parallel"),
        name="ple",
    )(x2d, p3, g, wg, wp, fg)


def _alibi_slopes():
    s = 2.0 ** (-8.0 * np.arange(1, N_HEADS_TOTAL + 1) / N_HEADS_TOTAL)
    return jnp.asarray(s[0::2], f32), jnp.asarray(s[1::2], f32)


def _selection_constants(seq):
    ncb = seq // CMP_STRIDE
    nsel = seq // SEL_BLOCK
    n_cmp = (seq - CMP_BLOCK) // CMP_STRIDE + 1
    cs = np.arange(n_cmp) * CMP_STRIDE
    bs = np.arange(nsel) * SEL_BLOCK
    ov = np.clip(np.minimum(cs[:, None] + CMP_BLOCK, bs[None, :] + SEL_BLOCK)
                 - np.maximum(cs[:, None], bs[None, :]), 0, None) / CMP_BLOCK
    ovt = np.zeros((ncb, LANES), np.float32)
    ovt[:n_cmp, :nsel] = ov
    key_blk = (np.arange(seq) // SEL_BLOCK).reshape(seq // SEL_CHUNK, 1, SEL_CHUNK)
    expand = (key_blk == np.arange(LANES).reshape(1, LANES, 1)).astype(np.float32)
    return jnp.asarray(ovt, bf16), jnp.asarray(expand, bf16)


def _block_diag2(w):
    z = jnp.zeros_like(w)
    return jnp.concatenate([jnp.concatenate([w, z], axis=-1), jnp.concatenate([z, w], axis=-1)], axis=-2)


def _layout_w_in(w):
    gate = w[:, N_MAIN:]
    per = GQA_REP * 3
    blocks = [jnp.pad(gate[:, g * per:(g + 1) * per], ((0, 0), (0, LANES - per))) for g in range(N_KV_B)]
    return jnp.concatenate([w[:, :N_MAIN]] + blocks, axis=1).astype(bf16)


def kernel(x, p, attn_norm, w_in, w_out, w_cmp_k1, w_cmp_k2, w_cmp_v1, w_cmp_v2, cmp_pos, ffn_norm, w_route_group, b_route_group, w_route_expert, b_route_expert, w_expert_gate, w_expert_up, w_expert_down, ple_norm, w_ple_gate, w_ple_proj, final_norm):
    B, S, D = x.shape
    depth = w_in.shape[0]
    T = B * S
    slopes_a, slopes_b = _alibi_slopes()
    ovt, expand = _selection_constants(S)
    wg_all = w_expert_gate.astype(bf16)
    wu_all = w_expert_up.astype(bf16)
    wd_all = w_expert_down.astype(bf16)
    p3 = p.reshape(depth, T, PLE_DIM)
    row = lambda v: v.reshape(1, -1)
    n_route = N_GROUPS + N_EXPERTS

    x2d = x.reshape(T, D)
    for i in range(depth):
        proj3 = _in_proj(x2d, row(attn_norm[i]), _layout_w_in(w_in[i])).reshape(B, S, N_PROJ)
        oa = _dilated(proj3, slopes_a)
        w1 = lambda w: _block_diag2(w.reshape(CMP_BLOCK, HEAD_DIM, CMP_HIDDEN)).astype(bf16)
        pos_dup = jnp.concatenate([cmp_pos[i], cmp_pos[i]], axis=-1)
        kcmp, vcmp = _compress(proj3, pos_dup, w1(w_cmp_k1[i]), _block_diag2(w_cmp_k2[i]).astype(bf16),
                               w1(w_cmp_v1[i]), _block_diag2(w_cmp_v2[i]).astype(bf16))
        ob = _nsa(proj3, kcmp, vcmp, slopes_b, ovt, expand)
        wo = w_out[i].astype(bf16)
        x2d = _out_proj(x2d, oa.reshape(T, A_W), ob.reshape(T, -1), wo[:A_W], wo[A_W:])
        w_route = jnp.pad(jnp.concatenate([w_route_group[i], w_route_expert[i]], axis=1),
                          ((0, 0), (0, LANES - n_route)))
        b_route = jnp.pad(jnp.concatenate([b_route_group[i], b_route_expert[i]]), (0, LANES - n_route))
        h, gates = _router(x2d, row(ffn_norm[i]), w_route, row(b_route))
        x2d = _moe(x2d, h, gates, wg_all, wu_all, wd_all, i)
        x2d = _ple(x2d, p3, row(ple_norm[i]), w_ple_gate[i].astype(bf16), w_ple_proj[i].astype(bf16),
                   row(final_norm), i, i == depth - 1)
    return x2d.reshape(B, S, D)
```

```python
import functools

import numpy as np
import jax
import jax.numpy as jnp
from jax import lax
from jax.experimental import pallas as pl
from jax.experimental.pallas import tpu as pltpu

D_MODEL = 1024
PLE_DIM = 256
HEAD_DIM = 64
N_HEADS_A = 8
N_HEADS_B = 8
N_KV_B = 2
GQA_REP = N_HEADS_B // N_KV_B
N_HEADS_TOTAL = N_HEADS_A + N_HEADS_B
DILATED_PATTERNS = ((128, 1), (512, 4), (2048, 16))
CMP_BLOCK = 32
CMP_STRIDE = 16
CMP_HIDDEN = 256
SEL_BLOCK = 64
SEL_TOP = 16
N_LOCAL_BLOCKS = 2
WINDOW = 512
N_GROUPS = 4
EXPERTS_PER_GROUP = 4
N_EXPERTS = N_GROUPS * EXPERTS_PER_GROUP
D_EXPERT = 512
RMS_EPS = 1e-6
NEG = -1e30
FORCE_BONUS = 1e4
SCALE = HEAD_DIM ** -0.5

LANES = 128
QB = 128
SEL_CHUNK = 512
POS_LANE0 = HEAD_DIM
POS_SPLIT = 16
SEL_LANE0 = POS_LANE0 + 4
SEL_LANES = 32
MASK_BIG = -(2.0 ** 100)
A_W = N_HEADS_A * HEAD_DIM
N_MAIN = 3 * A_W + N_HEADS_B * HEAD_DIM + 6 * N_KV_B * HEAD_DIM
N_PROJ = N_MAIN + N_KV_B * LANES
COL_QB = (3 * A_W) // (2 * LANES)
COL_KC = (3 * A_W + N_HEADS_B * HEAD_DIM) // LANES
COL_GATE = N_MAIN // LANES
VMEM_LIMIT = 56 * 1024 * 1024

f32 = jnp.float32
bf16 = jnp.bfloat16


def _dot(a, b):
    return jnp.dot(a, b, preferred_element_type=f32)


def _dot_nt(a, b):
    return lax.dot_general(a, b, (((1,), (1,)), ((), ())), preferred_element_type=f32)


def _rms(x, g):
    return x * lax.rsqrt(jnp.mean(x * x, axis=-1, keepdims=True) + RMS_EPS) * g


def _params(*sem):
    return pltpu.CompilerParams(dimension_semantics=sem, vmem_limit_bytes=VMEM_LIMIT)


def _in_proj_kernel(x_ref, g_ref, w_ref, o_ref):
    h = _rms(x_ref[...], g_ref[...]).astype(bf16)
    for n0 in range(0, N_PROJ, 512):
        o_ref[:, n0:n0 + 512] = _dot(h, w_ref[:, n0:n0 + 512])


def _in_proj(x2d, g, w):
    T = x2d.shape[0]
    tm = 512
    return pl.pallas_call(
        _in_proj_kernel,
        grid=(T // tm,),
        in_specs=[pl.BlockSpec((tm, D_MODEL), lambda i: (i, 0)),
                  pl.BlockSpec((1, D_MODEL), lambda i: (0, 0)),
                  pl.BlockSpec((D_MODEL, N_PROJ), lambda i: (0, 0))],
        out_specs=pl.BlockSpec((tm, N_PROJ), lambda i: (i, 0)),
        out_shape=jax.ShapeDtypeStruct((T, N_PROJ), f32),
        compiler_params=_params("parallel"),
        name="in_proj",
    )(x2d, g, w)


def _dil_kernel(slope_ref, q_ref, k_ref, v_ref, o_ref, num_ref, m_ref, l_ref, *, seq):
    hp = pl.program_id(1)
    lane = lax.broadcasted_iota(jnp.int32, (QB, LANES), 1)
    left = lane < HEAD_DIM
    slopes = (slope_ref[2 * hp], slope_ref[2 * hp + 1])

    def rel_of(nk):
        i = lax.broadcasted_iota(jnp.int32, (QB, nk), 0)
        j = lax.broadcasted_iota(jnp.int32, (QB, nk), 1)
        return i - j + (nk - QB)

    def block(p, dil, row0, key0, nk):
        qc = q_ref[pl.ds(row0, QB, stride=dil), :] * SCALE
        kc = k_ref[pl.ds(key0, nk, stride=dil), :].astype(bf16)
        vc = v_ref[pl.ds(key0, nk, stride=dil), :].astype(bf16)
        rel = rel_of(nk)
        valid = (rel >= 0) & (rel <= QB)
        relf = rel.astype(f32)
        q2 = jnp.concatenate([jnp.where(left, qc, 0.0), jnp.where(left, 0.0, qc)], axis=0).astype(bf16)
        s2 = _dot_nt(q2, kc)
        es, ms, ls = [], [], []
        for hh in range(2):
            s = jnp.where(valid, s2[hh * QB:(hh + 1) * QB] - (slopes[hh] * float(dil)) * relf, NEG)
            mx = jnp.max(s, axis=1, keepdims=True)
            e = jnp.exp(s - mx)
            ms.append(mx)
            ls.append(jnp.sum(e, axis=1, keepdims=True))
            es.append(e.astype(bf16))
        num2 = _dot(jnp.concatenate(es, axis=0), vc)
        rows = pl.ds(row0, QB, stride=dil)
        num_ref[p, rows, :] = jnp.where(left, num2[:QB], num2[QB:])
        m_ref[p, rows, :] = jnp.where(left, ms[0], ms[1])
        l_ref[p, rows, :] = jnp.where(left, ls[0], ls[1])

    for p, (window, dil) in enumerate(DILATED_PATTERNS):
        assert window // dil == QB
        nblk = seq // dil // QB

        def per_class(r, _, p=p, dil=dil, nblk=nblk):
            block(p, dil, r, r, QB)

            def per_blk(a, _):
                block(p, dil, r + dil * QB * a, r + dil * QB * (a - 1), 2 * QB)
                return 0

            if nblk > 1:
                lax.fori_loop(1, nblk, per_blk, 0, unroll=3)
            return 0

        if dil == 1:
            per_class(0, 0)
        else:
            lax.fori_loop(0, dil, per_class, 0, unroll=4 if nblk == 1 else 1)

    ch = 256

    def combine(c, _):
        rows = pl.ds(pl.multiple_of(c * ch, ch), ch)
        ms = [m_ref[p, rows, :] for p in range(3)]
        big = jnp.maximum(jnp.maximum(ms[0], ms[1]), ms[2])
        num = jnp.zeros((ch, LANES), f32)
        den = jnp.zeros((ch, LANES), f32)
        for p in range(3):
            w = jnp.exp(ms[p] - big)
            num = num + w * num_ref[p, rows, :]
            den = den + w * l_ref[p, rows, :]
        o_ref[rows, :] = num / den
        return 0

    lax.fori_loop(0, seq // ch, combine, 0)


def _dilated(proj3, slopes_a):
    B, S, _ = proj3.shape
    npair = N_HEADS_A // 2
    blk = lambda off: pl.BlockSpec((None, S, LANES), lambda b, hp, off=off: (b, 0, off + hp))
    return pl.pallas_call(
        functools.partial(_dil_kernel, seq=S),
        grid=(B, npair),
        in_specs=[pl.BlockSpec(memory_space=pltpu.SMEM), blk(0), blk(npair), blk(2 * npair)],
        out_specs=pl.BlockSpec((None, S, LANES), lambda b, hp: (b, 0, hp)),
        out_shape=jax.ShapeDtypeStruct((B, S, A_W), f32),
        scratch_shapes=[pltpu.VMEM((3, S, LANES), f32)] * 3,
        compiler_params=_params("parallel", "parallel"),
        name="dilated_attn",
    )(slopes_a, proj3, proj3, proj3)


def _cmp_kernel(kc_ref, vc_ref, pos_ref, w1k_ref, w2k_ref, w1v_ref, w2v_ref, ko_ref, vo_ref, *, ncb):
    half = CMP_BLOCK // 2
    for x_ref, pi, w1_ref, w2_ref, o_ref in ((kc_ref, 0, w1k_ref, w2k_ref, ko_ref),
                                             (vc_ref, 1, w1v_ref, w2v_ref, vo_ref)):
        lo = jnp.zeros((ncb, 2 * CMP_HIDDEN), f32)
        hi = jnp.zeros((ncb, 2 * CMP_HIDDEN), f32)
        for r in range(half):
            x = x_ref[pl.ds(r, ncb, stride=CMP_STRIDE), :]
            lo = lo + _dot((x + pos_ref[pi, r:r + 1, :]).astype(bf16), w1_ref[r])
            hi = hi + _dot((x + pos_ref[pi, r + half:r + half + 1, :]).astype(bf16), w1_ref[r + half])
        h1 = lo + pltpu.roll(hi, ncb - 1, axis=0)
        o_ref[...] = _dot(jax.nn.gelu(h1).astype(bf16), w2_ref[...])


def _compress(proj3, pos_dup, w1k, w2k, w1v, w2v):
    B, S, _ = proj3.shape
    ncb = S // CMP_STRIDE
    full = lambda a: pl.BlockSpec(a.shape, lambda b: (0,) * a.ndim)
    out = jax.ShapeDtypeStruct((B, ncb, LANES), f32)
    return pl.pallas_call(
        functools.partial(_cmp_kernel, ncb=ncb),
        grid=(B,),
        in_specs=[pl.BlockSpec((None, S, LANES), lambda b: (b, 0, COL_KC)),
                  pl.BlockSpec((None, S, LANES), lambda b: (b, 0, COL_KC + 1)),
                  full(pos_dup), full(w1k), full(w2k), full(w1v), full(w2v)],
        out_specs=[pl.BlockSpec((None, ncb, LANES), lambda b: (b, 0, 0))] * 2,
        out_shape=[out, out],
        compiler_params=_params("parallel"),
        name="nsa_compress",
    )(proj3, proj3, pos_dup, w1k, w2k, w1v, w2v)


def _nsa_kernel(slope_ref, q_ref, ks_ref, vs_ref, kw_ref, vw_ref, gl_ref, kcmp_ref, vcmp_ref,
                ovt_ref, kaug_ref, caug_ref, o_ref,
                ksa, vsa, kwa, vwa, kca, vca, q_scr, s_scr, e_scr, m_scr, al_scr, acc_scr, sel_scr, *, seq):
    g = pl.program_id(1)
    qi = pl.program_id(2)
    nsel = seq // SEL_BLOCK
    n_top = min(SEL_TOP, nsel)
    assert nsel <= SEL_LANES and N_LOCAL_BLOCKS * SEL_BLOCK >= QB and n_top > N_LOCAL_BLOCKS
    nrow = GQA_REP * QB
    slab = 32

    def group_lanes(x):
        return jnp.where(g == 0, x, pltpu.roll(x, HEAD_DIM, axis=1))

    @pl.when(qi == 0)
    def _prep():
        ch = 256
        lane_c = lax.broadcasted_iota(jnp.int32, (ch, LANES), 1)
        is_k = lane_c < HEAD_DIM
        for c in range(seq // ch):
            sl = slice(c * ch, (c + 1) * ch)
            aug = kaug_ref[sl, :]
            ksa[sl, :] = jnp.where(is_k, group_lanes(ks_ref[sl, :]), aug).astype(bf16)
            kwa[sl, :] = jnp.where(is_k, group_lanes(kw_ref[sl, :]),
                                   jnp.where(lane_c < SEL_LANE0, aug, 0.0)).astype(bf16)
            vsa[sl, :] = jnp.where(is_k, group_lanes(vs_ref[sl, :]), 1.0).astype(bf16)
            vwa[sl, :] = jnp.where(is_k, group_lanes(vw_ref[sl, :]), 1.0).astype(bf16)
        lane_k = lax.broadcasted_iota(jnp.int32, kca.shape, 1)
        kca[...] = jnp.where(lane_k < HEAD_DIM, group_lanes(kcmp_ref[...]), caug_ref[...]).astype(bf16)
        vca[...] = group_lanes(vcmp_ref[...]).astype(bf16)

    lane = lax.broadcasted_iota(jnp.int32, (QB, LANES), 1)
    left = lane < HEAD_DIM
    ii = lax.broadcasted_iota(jnp.int32, (QB, LANES), 0)
    t_row = qi * QB + ii
    t_hi = (t_row // POS_SPLIT).astype(f32)
    t_lo = (t_row % POS_SPLIT).astype(f32)
    slopes = [slope_ref[g * GQA_REP + r] for r in range(GQA_REP)]
    head = lambda a, r: a[r * QB:(r + 1) * QB]

    def q_head(r):
        x = q_ref[:, (r // 2) * LANES:(r // 2 + 1) * LANES] * SCALE
        if r % 2:
            x = pltpu.roll(x, HEAD_DIM, axis=1)
        m = slopes[r]
        pos = jnp.where(lane == POS_LANE0, (-POS_SPLIT * m) * t_hi,
                        jnp.where(lane == POS_LANE0 + 1, (-m) * t_lo,
                                  jnp.where(lane == POS_LANE0 + 2, POS_SPLIT * m,
                                            jnp.where(lane == POS_LANE0 + 3, m, 0.0))))
        return jnp.where(left, x, pos)

    q4 = jnp.concatenate([q_head(r) for r in range(GQA_REP)], axis=0)
    q4b = q4.astype(bf16)

    cmp_end = (lane * CMP_STRIDE + (CMP_BLOCK - 1))
    valid_c = t_row >= cmp_end
    s4 = _dot_nt(q4b, kca[...])
    ps = []
    p_sum = jnp.zeros((QB, LANES), f32)
    for r in range(GQA_REP):
        s = jnp.where(valid_c, head(s4, r), NEG)
        e = jnp.exp(s - jnp.max(s, axis=1, keepdims=True))
        p = jnp.where(valid_c, e, 0.0) / jnp.sum(e, axis=1, keepdims=True)
        p_sum = p_sum + p
        ps.append(p.astype(bf16))
    o_cmp4 = _dot(jnp.concatenate(ps, axis=0), vca[...])

    back = t_row // SEL_BLOCK - lane
    valid_s = (back >= 0) & (lane < nsel)

    @pl.when(2 * qi + 2 <= n_top)
    def _all_valid():
        sel_scr[...] = jnp.where(valid_s, 1.0, 0.0)

    @pl.when(2 * qi + 2 > n_top)
    def _top_k():
        p_hi = p_sum.astype(bf16)
        p_lo = (p_sum - p_hi.astype(f32)).astype(bf16)
        imp_t = (_dot_nt(ovt_ref[...], p_hi) + _dot_nt(ovt_ref[...], p_lo))[:SEL_LANES]
        blk = lax.broadcasted_iota(jnp.int32, (SEL_LANES, QB), 0)
        tq = qi * QB + lax.broadcasted_iota(jnp.int32, (SEL_LANES, QB), 1)
        back_t = tq // SEL_BLOCK - blk
        valid_t = (back_t >= 0) & (blk < nsel)
        forced = (blk == 0) | (valid_t & (back_t < N_LOCAL_BLOCKS))
        score = jnp.where(valid_t, imp_t + jnp.where(forced, FORCE_BONUS, 0.0), NEG)
        score = jnp.where(blk < nsel, score, 2.0 * NEG)
        rank = jnp.zeros((SEL_LANES, QB), jnp.int32)
        for n in range(nsel):
            row = score[n:n + 1, :]
            ahead = (row > score) | ((row == score) & (blk > n))
            rank = rank + ahead.astype(jnp.int32)
        sel_t = jnp.where((rank < n_top) & valid_t, 1.0, 0.0)
        sel_t = jnp.concatenate([sel_t, jnp.zeros((LANES - SEL_LANES, QB), f32)], axis=0)
        sel_scr[...] = sel_t.T

    sel_bias = jnp.where((sel_scr[...] > 0.5) & (lane < 2 * qi), 0.0, MASK_BIG)
    sel_bias = pltpu.roll(sel_bias, SEL_LANE0, axis=1)
    in_sel = (lane >= SEL_LANE0) & (lane < SEL_LANE0 + SEL_LANES)
    for r in range(GQA_REP):
        q_scr[r * QB:(r + 1) * QB, :] = jnp.where(in_sel, sel_bias, head(q4, r)).astype(bf16)

    sig = jax.nn.sigmoid(gl_ref[...])

    def rel_tile(nk, offset):
        i = lax.broadcasted_iota(jnp.int32, (QB, nk), 0)
        j = lax.broadcasted_iota(jnp.int32, (QB, nk), 1)
        return (i - j + offset).astype(f32)

    def softmax_slabs(nk, bias, running):
        for sl in range(nrow // slab):
            rows = slice(sl * slab, (sl + 1) * slab)
            s = s_scr[rows, :nk]
            if bias is not None:
                i0 = (sl * slab) % QB
                s = s + bias[i0:i0 + slab, :]
            m_new = jnp.max(s, axis=1, keepdims=True)
            if running:
                m_old = m_scr[rows, :]
                m_new = jnp.maximum(m_old, m_new)
                al_scr[rows, :] = jnp.exp(m_old - m_new)
            m_scr[rows, :] = m_new
            e_scr[rows, :nk] = jnp.exp(s - m_new).astype(bf16)

    nwin = WINDOW + QB
    w0 = pl.multiple_of(jnp.maximum(qi - WINDOW // QB, 0) * QB, QB)
    dist_w = rel_tile(nwin, qi * QB - w0)
    bias_w = jnp.where((dist_w >= 0) & (dist_w < WINDOW), 0.0, NEG)
    s_scr[:, :nwin] = _dot_nt(q4b, kwa[pl.ds(w0, nwin), :])
    softmax_slabs(nwin, bias_w, False)
    o_w4 = _dot(e_scr[:, :nwin], vwa[pl.ds(w0, nwin), :])

    d0 = pl.multiple_of(qi * QB, QB)
    bias_d = jnp.where(rel_tile(QB, 0) >= 0, 0.0, NEG)
    s_scr[:, :QB] = _dot_nt(q4b, ksa[pl.ds(d0, QB), :])
    softmax_slabs(QB, bias_d, False)
    acc_scr[...] = _dot(e_scr[:, :QB], vsa[pl.ds(d0, QB), :])

    def sel_body(kc, _):
        rows = pl.ds(pl.multiple_of(kc * SEL_CHUNK, SEL_CHUNK), SEL_CHUNK)
        s_scr[:, :SEL_CHUNK] = _dot_nt(q_scr[...], ksa[rows, :])
        softmax_slabs(SEL_CHUNK, None, True)
        acc_scr[...] = al_scr[...] * acc_scr[...] + _dot(e_scr[:, :SEL_CHUNK], vsa[rows, :])
        return 0

    per = SEL_CHUNK // QB
    lax.fori_loop(0, (qi + per - 1) // per, sel_body, 0)

    def normalised(a):
        return a / pltpu.roll(a, HEAD_DIM, axis=1)

    acc_s = acc_scr[...]
    outs = []
    for r in range(GQA_REP):
        outs.append(sig[:, 3 * r:3 * r + 1] * head(o_cmp4, r)
                    + sig[:, 3 * r + 1:3 * r + 2] * normalised(head(acc_s, r))
                    + sig[:, 3 * r + 2:3 * r + 3] * normalised(head(o_w4, r)))
    for pr in range(GQA_REP // 2):
        o_ref[:, pr * LANES:(pr + 1) * LANES] = jnp.where(left, outs[2 * pr],
                                                          pltpu.roll(outs[2 * pr + 1], HEAD_DIM, axis=1))


def _nsa(proj3, kcmp, vcmp, slopes_b, ovt, kaug, caug):
    B, S, _ = proj3.shape
    ncb = kcmp.shape[1]
    nrow = GQA_REP * QB
    nwin = WINDOW + QB
    kv = lambda j: pl.BlockSpec((None, S, LANES), lambda b, g, qi, j=j: (b, 0, COL_KC + j))
    full = lambda a: pl.BlockSpec(a.shape, lambda b, g, qi: (0,) * a.ndim)
    return pl.pallas_call(
        functools.partial(_nsa_kernel, seq=S),
        grid=(B, N_KV_B, S // QB),
        in_specs=[pl.BlockSpec(memory_space=pltpu.SMEM),
                  pl.BlockSpec((None, QB, 2 * LANES), lambda b, g, qi: (b, qi, COL_QB + g)),
                  kv(2), kv(3), kv(4), kv(5),
                  pl.BlockSpec((None, QB, LANES), lambda b, g, qi: (b, qi, COL_GATE + g)),
                  pl.BlockSpec((None, ncb, LANES), lambda b, g, qi: (b, 0, 0)),
                  pl.BlockSpec((None, ncb, LANES), lambda b, g, qi: (b, 0, 0)),
                  full(ovt), full(kaug), full(caug)],
        out_specs=pl.BlockSpec((None, QB, 2 * LANES), lambda b, g, qi: (b, qi, g)),
        out_shape=jax.ShapeDtypeStruct((B, S, N_HEADS_B * HEAD_DIM), f32),
        scratch_shapes=([pltpu.VMEM((S, LANES), bf16)] * 4 + [pltpu.VMEM((ncb, LANES), bf16)] * 2
                        + [pltpu.VMEM((nrow, LANES), bf16),
                           pltpu.VMEM((nrow, max(nwin, SEL_CHUNK)), f32),
                           pltpu.VMEM((nrow, max(nwin, SEL_CHUNK)), bf16),
                           pltpu.VMEM((nrow, 1), f32), pltpu.VMEM((nrow, 1), f32),
                           pltpu.VMEM((nrow, LANES), f32), pltpu.VMEM((QB, LANES), f32)]),
        compiler_params=_params("parallel", "parallel", "arbitrary"),
        name="nsa_attn",
    )(slopes_b, proj3, proj3, proj3, proj3, proj3, proj3, kcmp, vcmp, ovt, kaug, caug)


def _out_proj_kernel(x_ref, oa_ref, ob_ref, wa_ref, wb_ref, o_ref):
    o_ref[...] = (x_ref[...] + _dot(oa_ref[...].astype(bf16), wa_ref[...])
                  + _dot(ob_ref[...].astype(bf16), wb_ref[...]))


def _out_proj(x2d, oa, ob, wa, wb):
    T = x2d.shape[0]
    tm = 512
    row = lambda w: pl.BlockSpec((tm, w), lambda i: (i, 0))
    full = lambda a: pl.BlockSpec(a.shape, lambda i: (0,) * a.ndim)
    return pl.pallas_call(
        _out_proj_kernel,
        grid=(T // tm,),
        in_specs=[row(D_MODEL), row(oa.shape[1]), row(ob.shape[1]), full(wa), full(wb)],
        out_specs=row(D_MODEL),
        out_shape=jax.ShapeDtypeStruct((T, D_MODEL), f32),
        compiler_params=_params("parallel"),
        name="out_proj",
    )(x2d, oa, ob, wa, wb)


def _router_kernel(x_ref, g_ref, w_ref, b_ref, h_ref, gate_ref):
    h = _rms(x_ref[...], g_ref[...])
    h_ref[...] = h.astype(bf16)
    logit = jnp.dot(h, w_ref[...], preferred_element_type=f32, precision=lax.Precision.HIGHEST) + b_ref[...]
    tm = logit.shape[0]
    lane = lax.broadcasted_iota(jnp.int32, (tm, LANES), 1)
    big = jnp.int32(LANES)
    is_g = lane < N_GROUPS
    gl = jnp.where(is_g, logit, NEG)
    gmax = jnp.max(gl, axis=1, keepdims=True)
    gsum = jnp.sum(jnp.where(is_g, jnp.exp(gl - gmax), 0.0), axis=1, keepdims=True)
    gsel = jnp.min(jnp.where(is_g & (gl == gmax), lane, big), axis=1, keepdims=True)
    gw = 1.0 / gsum
    e_lane = lane - N_GROUPS
    in_grp = (e_lane >= 0) & (e_lane < N_EXPERTS) & (e_lane // EXPERTS_PER_GROUP == gsel)
    el = jnp.where(in_grp, logit, NEG)
    t1 = jnp.max(el, axis=1, keepdims=True)
    i1 = jnp.min(jnp.where(in_grp & (el == t1), lane, big), axis=1, keepdims=True)
    el2 = jnp.where(lane == i1, NEG, el)
    t2 = jnp.max(el2, axis=1, keepdims=True)
    i2 = jnp.min(jnp.where(in_grp & (lane != i1) & (el2 == t2), lane, big), axis=1, keepdims=True)
    e2 = jnp.exp(t2 - t1)
    w1 = gw / (1.0 + e2)
    w2 = gw * e2 / (1.0 + e2)
    gate_ref[...] = jnp.where(lane == i1, w1, jnp.where(lane == i2, w2, 0.0))


def _router(x2d, g, w, b):
    T = x2d.shape[0]
    tm = 512
    return pl.pallas_call(
        _router_kernel,
        grid=(T // tm,),
        in_specs=[pl.BlockSpec((tm, D_MODEL), lambda i: (i, 0)),
                  pl.BlockSpec((1, D_MODEL), lambda i: (0, 0)),
                  pl.BlockSpec((D_MODEL, LANES), lambda i: (0, 0)),
                  pl.BlockSpec((1, LANES), lambda i: (0, 0))],
        out_specs=[pl.BlockSpec((tm, D_MODEL), lambda i: (i, 0)),
                   pl.BlockSpec((tm, LANES), lambda i: (i, 0))],
        out_shape=[jax.ShapeDtypeStruct((T, D_MODEL), bf16), jax.ShapeDtypeStruct((T, LANES), f32)],
        compiler_params=_params("parallel"),
        name="router",
    )(x2d, g, w, b)


def _moe_kernel(x_ref, h_ref, gate_ref, wg_ref, wu_ref, wd_ref, o_ref):
    e = pl.program_id(1)

    @pl.when(e == 0)
    def _init():
        o_ref[...] = x_ref[...]

    h = h_ref[...]
    lane = lax.broadcasted_iota(jnp.int32, gate_ref.shape, 1)
    gcol = jnp.sum(jnp.where(lane == e + N_GROUPS, gate_ref[...], 0.0), axis=1, keepdims=True)
    a = jax.nn.silu(_dot(h, wg_ref[...])) * _dot(h, wu_ref[...])
    o_ref[...] += _dot((a * gcol).astype(bf16), wd_ref[...])


def _moe(x2d, h, gates, wg, wu, wd, layer):
    T = x2d.shape[0]
    tm = 1024
    return pl.pallas_call(
        _moe_kernel,
        grid=(T // tm, N_EXPERTS),
        in_specs=[pl.BlockSpec((tm, D_MODEL), lambda i, e: (i, 0)),
                  pl.BlockSpec((tm, D_MODEL), lambda i, e: (i, 0)),
                  pl.BlockSpec((tm, LANES), lambda i, e: (i, 0)),
                  pl.BlockSpec((None, None, D_MODEL, D_EXPERT), lambda i, e: (layer, e, 0, 0)),
                  pl.BlockSpec((None, None, D_MODEL, D_EXPERT), lambda i, e: (layer, e, 0, 0)),
                  pl.BlockSpec((None, None, D_EXPERT, D_MODEL), lambda i, e: (layer, e, 0, 0))],
        out_specs=pl.BlockSpec((tm, D_MODEL), lambda i, e: (i, 0)),
        out_shape=jax.ShapeDtypeStruct((T, D_MODEL), f32),
        compiler_params=_params("parallel", "arbitrary"),
        name="moe_ffn",
    )(x2d, h, gates, wg, wu, wd)


def _ple_kernel(x_ref, p_ref, g_ref, wg_ref, wp_ref, fg_ref, o_ref, *, final):
    x = x_ref[...]
    gate = jax.nn.sigmoid(_dot(_rms(x, g_ref[...]).astype(bf16), wg_ref[...]))
    y = x + gate * _dot(p_ref[...].astype(bf16), wp_ref[...])
    o_ref[...] = _rms(y, fg_ref[...]) if final else y


def _ple(x2d, p3, g, wg, wp, fg, layer, final):
    T = x2d.shape[0]
    tm = 512
    full = lambda a: pl.BlockSpec(a.shape, lambda i: (0,) * a.ndim)
    return pl.pallas_call(
        functools.partial(_ple_kernel, final=final),
        grid=(T // tm,),
        in_specs=[pl.BlockSpec((tm, D_MODEL), lambda i: (i, 0)),
                  pl.BlockSpec((None, tm, PLE_DIM), lambda i: (layer, i, 0)),
                  full(g), full(wg), full(wp), full(fg)],
        out_specs=pl.BlockSpec((tm, D_MODEL), lambda i: (i, 0)),
        out_shape=jax.ShapeDtypeStruct((T, D_MODEL), f32),
        compiler_params=_params("parallel"),
        name="ple",
    )(x2d, p3, g, wg, wp, fg)


def _alibi_slopes():
    s = 2.0 ** (-8.0 * np.arange(1, N_HEADS_TOTAL + 1) / N_HEADS_TOTAL)
    assert np.all(np.log2(s[1::2]) == np.round(np.log2(s[1::2])))
    return jnp.asarray(s[0::2], f32), jnp.asarray(s[1::2], f32)


def _selection_constants(seq):
    ncb = seq // CMP_STRIDE
    nsel = seq // SEL_BLOCK
    n_cmp = (seq - CMP_BLOCK) // CMP_STRIDE + 1
    cs = np.arange(n_cmp) * CMP_STRIDE
    bs = np.arange(nsel) * SEL_BLOCK
    ov = np.clip(np.minimum(cs[:, None] + CMP_BLOCK, bs[None, :] + SEL_BLOCK)
                 - np.maximum(cs[:, None], bs[None, :]), 0, None) / CMP_BLOCK
    assert ncb == LANES and seq <= POS_SPLIT * 256
    ovt = np.zeros((LANES, ncb), np.float32)
    ovt[:nsel, :n_cmp] = ov.T
    pos = np.arange(seq)
    kaug = np.zeros((seq, LANES), np.float32)
    kaug[:, POS_LANE0:POS_LANE0 + 2] = 1.0
    kaug[:, POS_LANE0 + 2] = pos // POS_SPLIT
    kaug[:, POS_LANE0 + 3] = pos % POS_SPLIT
    kaug[pos, SEL_LANE0 + pos // SEL_BLOCK] = 1.0
    cend = np.arange(ncb) * CMP_STRIDE + CMP_BLOCK - 1
    caug = np.zeros((ncb, LANES), np.float32)
    caug[:, POS_LANE0:POS_LANE0 + 2] = 1.0
    caug[:, POS_LANE0 + 2] = cend // POS_SPLIT
    caug[:, POS_LANE0 + 3] = cend % POS_SPLIT
    return jnp.asarray(ovt, bf16), jnp.asarray(kaug), jnp.asarray(caug)


def _block_diag2(w):
    z = jnp.zeros_like(w)
    return jnp.concatenate([jnp.concatenate([w, z], axis=-1), jnp.concatenate([z, w], axis=-1)], axis=-2)


def _layout_w_in(w):
    gate = w[:, N_MAIN:]
    per = GQA_REP * 3
    blocks = [jnp.pad(gate[:, g * per:(g + 1) * per], ((0, 0), (0, LANES - per))) for g in range(N_KV_B)]
    return jnp.concatenate([w[:, :N_MAIN]] + blocks, axis=1).astype(bf16)


def kernel(x, p, attn_norm, w_in, w_out, w_cmp_k1, w_cmp_k2, w_cmp_v1, w_cmp_v2, cmp_pos, ffn_norm, w_route_group, b_route_group, w_route_expert, b_route_expert, w_expert_gate, w_expert_up, w_expert_down, ple_norm, w_ple_gate, w_ple_proj, final_norm):
    B, S, D = x.shape
    depth = w_in.shape[0]
    T = B * S
    slopes_a, slopes_b = _alibi_slopes()
    ovt, kaug, caug = _selection_constants(S)
    wg_all = w_expert_gate.astype(bf16)
    wu_all = w_expert_up.astype(bf16)
    wd_all = w_expert_down.astype(bf16)
    p3 = p.reshape(depth, T, PLE_DIM)
    row = lambda v: v.reshape(1, -1)
    n_route = N_GROUPS + N_EXPERTS

    x2d = x.reshape(T, D)
    for i in range(depth):
        proj3 = _in_proj(x2d, row(attn_norm[i]), _layout_w_in(w_in[i])).reshape(B, S, N_PROJ)
        oa = _dilated(proj3, slopes_a)
        w1 = lambda w: _block_diag2(w.reshape(CMP_BLOCK, HEAD_DIM, CMP_HIDDEN)).astype(bf16)
        pos_dup = jnp.concatenate([cmp_pos[i], cmp_pos[i]], axis=-1)
        kcmp, vcmp = _compress(proj3, pos_dup, w1(w_cmp_k1[i]), _block_diag2(w_cmp_k2[i]).astype(bf16),
                               w1(w_cmp_v1[i]), _block_diag2(w_cmp_v2[i]).astype(bf16))
        ob = _nsa(proj3, kcmp, vcmp, slopes_b, ovt, kaug, caug)
        wo = w_out[i].astype(bf16)
        x2d = _out_proj(x2d, oa.reshape(T, A_W), ob.reshape(T, -1), wo[:A_W], wo[A_W:])
        w_route = jnp.pad(jnp.concatenate([w_route_group[i], w_route_expert[i]], axis=1),
                          ((0, 0), (0, LANES - n_route)))
        b_route = jnp.pad(jnp.concatenate([b_route_group[i], b_route_expert[i]]), (0, LANES - n_route))
        h, gates = _router(x2d, row(ffn_norm[i]), w_route, row(b_route))
        x2d = _moe(x2d, h, gates, wg_all, wu_all, wd_all, i)
        x2d = _ple(x2d, p3, row(ple_norm[i]), w_ple_gate[i].astype(bf16), w_ple_proj[i].astype(bf16),
                   row(final_norm), i, i == depth - 1)
    return x2d.reshape(B, S, D)
```

```python
import functools

import numpy as np
import jax
import jax.numpy as jnp
from jax import lax
from jax.experimental import pallas as pl
from jax.experimental.pallas import tpu as pltpu

D_MODEL = 1024
PLE_DIM = 256
HEAD_DIM = 64
N_HEADS_A = 8
N_HEADS_B = 8
N_KV_B = 2
GQA_REP = N_HEADS_B // N_KV_B
N_HEADS_TOTAL = N_HEADS_A + N_HEADS_B
DILATED_PATTERNS = ((128, 1), (512, 4), (2048, 16))
CMP_BLOCK = 32
CMP_STRIDE = 16
CMP_HIDDEN = 256
SEL_BLOCK = 64
SEL_TOP = 16
N_LOCAL_BLOCKS = 2
WINDOW = 512
N_GROUPS = 4
EXPERTS_PER_GROUP = 4
N_EXPERTS = N_GROUPS * EXPERTS_PER_GROUP
D_EXPERT = 512
RMS_EPS = 1e-6
NEG = -1e30
FORCE_BONUS = 1e4
SCALE = HEAD_DIM ** -0.5

LANES = 128
QB = 128
SEL_CHUNK = 512
NSA_UNITS = 1
POS_LANE0 = HEAD_DIM
POS_SPLIT = 16
SEL_LANE0 = POS_LANE0 + 4
SEL_LANES = 32
MASK_BIG = -(2.0 ** 100)
A_W = N_HEADS_A * HEAD_DIM
N_MAIN = 3 * A_W + N_HEADS_B * HEAD_DIM + 6 * N_KV_B * HEAD_DIM
N_PROJ = N_MAIN + N_KV_B * LANES
COL_QB = (3 * A_W) // (2 * LANES)
COL_KC = (3 * A_W + N_HEADS_B * HEAD_DIM) // LANES
COL_GATE = N_MAIN // LANES
VMEM_LIMIT = 56 * 1024 * 1024

f32 = jnp.float32
bf16 = jnp.bfloat16


def _dot(a, b):
    return jnp.dot(a, b, preferred_element_type=f32)


def _dot_nt(a, b):
    return lax.dot_general(a, b, (((1,), (1,)), ((), ())), preferred_element_type=f32)


def _rms(x, g):
    return x * lax.rsqrt(jnp.mean(x * x, axis=-1, keepdims=True) + RMS_EPS) * g


def _params(*sem):
    return pltpu.CompilerParams(dimension_semantics=sem, vmem_limit_bytes=VMEM_LIMIT)


def _in_proj_kernel(x_ref, g_ref, w_ref, o_ref):
    h = _rms(x_ref[...], g_ref[...]).astype(bf16)
    for n0 in range(0, N_PROJ, 512):
        o_ref[:, n0:n0 + 512] = _dot(h, w_ref[:, n0:n0 + 512])


def _in_proj(x2d, g, w):
    T = x2d.shape[0]
    tm = 512
    return pl.pallas_call(
        _in_proj_kernel,
        grid=(T // tm,),
        in_specs=[pl.BlockSpec((tm, D_MODEL), lambda i: (i, 0)),
                  pl.BlockSpec((1, D_MODEL), lambda i: (0, 0)),
                  pl.BlockSpec((D_MODEL, N_PROJ), lambda i: (0, 0))],
        out_specs=pl.BlockSpec((tm, N_PROJ), lambda i: (i, 0)),
        out_shape=jax.ShapeDtypeStruct((T, N_PROJ), f32),
        compiler_params=_params("parallel"),
        name="in_proj",
    )(x2d, g, w)


def _dil_kernel(slope_ref, q_ref, k_ref, v_ref, o_ref, num_ref, m_ref, l_ref, *, seq):
    hp = pl.program_id(1)
    lane = lax.broadcasted_iota(jnp.int32, (QB, LANES), 1)
    left = lane < HEAD_DIM
    slopes = (slope_ref[2 * hp], slope_ref[2 * hp + 1])

    def rel_of(nk):
        i = lax.broadcasted_iota(jnp.int32, (QB, nk), 0)
        j = lax.broadcasted_iota(jnp.int32, (QB, nk), 1)
        return i - j + (nk - QB)

    def block(p, dil, row0, key0, nk):
        qc = q_ref[pl.ds(row0, QB, stride=dil), :] * SCALE
        kc = k_ref[pl.ds(key0, nk, stride=dil), :].astype(bf16)
        vc = v_ref[pl.ds(key0, nk, stride=dil), :].astype(bf16)
        rel = rel_of(nk)
        valid = (rel >= 0) & (rel <= QB)
        relf = rel.astype(f32)
        q2 = jnp.concatenate([jnp.where(left, qc, 0.0), jnp.where(left, 0.0, qc)], axis=0).astype(bf16)
        s2 = _dot_nt(q2, kc)
        es, ms, ls = [], [], []
        for hh in range(2):
            s = jnp.where(valid, s2[hh * QB:(hh + 1) * QB] - (slopes[hh] * float(dil)) * relf, NEG)
            mx = jnp.max(s, axis=1, keepdims=True)
            e = jnp.exp(s - mx)
            ms.append(mx)
            ls.append(jnp.sum(e, axis=1, keepdims=True))
            es.append(e.astype(bf16))
        num2 = _dot(jnp.concatenate(es, axis=0), vc)
        rows = pl.ds(row0, QB, stride=dil)
        num_ref[p, rows, :] = jnp.where(left, num2[:QB], num2[QB:])
        m_ref[p, rows, :] = jnp.where(left, ms[0], ms[1])
        l_ref[p, rows, :] = jnp.where(left, ls[0], ls[1])

    for p, (window, dil) in enumerate(DILATED_PATTERNS):
        assert window // dil == QB
        nblk = seq // dil // QB

        def per_class(r, _, p=p, dil=dil, nblk=nblk):
            block(p, dil, r, r, QB)

            def per_blk(a, _):
                block(p, dil, r + dil * QB * a, r + dil * QB * (a - 1), 2 * QB)
                return 0

            if nblk > 1:
                lax.fori_loop(1, nblk, per_blk, 0, unroll=3)
            return 0

        if dil == 1:
            per_class(0, 0)
        else:
            lax.fori_loop(0, dil, per_class, 0, unroll=4 if nblk == 1 else 1)

    ch = 256

    def combine(c, _):
        rows = pl.ds(pl.multiple_of(c * ch, ch), ch)
        ms = [m_ref[p, rows, :] for p in range(3)]
        big = jnp.maximum(jnp.maximum(ms[0], ms[1]), ms[2])
        num = jnp.zeros((ch, LANES), f32)
        den = jnp.zeros((ch, LANES), f32)
        for p in range(3):
            w = jnp.exp(ms[p] - big)
            num = num + w * num_ref[p, rows, :]
            den = den + w * l_ref[p, rows, :]
        o_ref[rows, :] = num / den
        return 0

    lax.fori_loop(0, seq // ch, combine, 0)


def _dilated(proj3, slopes_a):
    B, S, _ = proj3.shape
    npair = N_HEADS_A // 2
    blk = lambda off: pl.BlockSpec((None, S, LANES), lambda b, hp, off=off: (b, 0, off + hp))
    return pl.pallas_call(
        functools.partial(_dil_kernel, seq=S),
        grid=(B, npair),
        in_specs=[pl.BlockSpec(memory_space=pltpu.SMEM), blk(0), blk(npair), blk(2 * npair)],
        out_specs=pl.BlockSpec((None, S, LANES), lambda b, hp: (b, 0, hp)),
        out_shape=jax.ShapeDtypeStruct((B, S, A_W), f32),
        scratch_shapes=[pltpu.VMEM((3, S, LANES), f32)] * 3,
        compiler_params=_params("parallel", "parallel"),
        name="dilated_attn",
    )(slopes_a, proj3, proj3, proj3)


def _cmp_kernel(kc_ref, vc_ref, pos_ref, w1k_ref, w2k_ref, w1v_ref, w2v_ref, ko_ref, vo_ref, *, ncb):
    half = CMP_BLOCK // 2
    for x_ref, pi, w1_ref, w2_ref, o_ref in ((kc_ref, 0, w1k_ref, w2k_ref, ko_ref),
                                             (vc_ref, 1, w1v_ref, w2v_ref, vo_ref)):
        lo = jnp.zeros((ncb, 2 * CMP_HIDDEN), f32)
        hi = jnp.zeros((ncb, 2 * CMP_HIDDEN), f32)
        for r in range(half):
            x = x_ref[pl.ds(r, ncb, stride=CMP_STRIDE), :]
            lo = lo + _dot((x + pos_ref[pi, r:r + 1, :]).astype(bf16), w1_ref[r])
            hi = hi + _dot((x + pos_ref[pi, r + half:r + half + 1, :]).astype(bf16), w1_ref[r + half])
        h1 = lo + pltpu.roll(hi, ncb - 1, axis=0)
        o_ref[...] = _dot(jax.nn.gelu(h1).astype(bf16), w2_ref[...])


def _compress(proj3, pos_dup, w1k, w2k, w1v, w2v):
    B, S, _ = proj3.shape
    ncb = S // CMP_STRIDE
    full = lambda a: pl.BlockSpec(a.shape, lambda b: (0,) * a.ndim)
    out = jax.ShapeDtypeStruct((B, ncb, LANES), f32)
    return pl.pallas_call(
        functools.partial(_cmp_kernel, ncb=ncb),
        grid=(B,),
        in_specs=[pl.BlockSpec((None, S, LANES), lambda b: (b, 0, COL_KC)),
                  pl.BlockSpec((None, S, LANES), lambda b: (b, 0, COL_KC + 1)),
                  full(pos_dup), full(w1k), full(w2k), full(w1v), full(w2v)],
        out_specs=[pl.BlockSpec((None, ncb, LANES), lambda b: (b, 0, 0))] * 2,
        out_shape=[out, out],
        compiler_params=_params("parallel"),
        name="nsa_compress",
    )(proj3, proj3, pos_dup, w1k, w2k, w1v, w2v)


def _nsa_kernel(slope_ref, q_ref, ks_ref, vs_ref, kw_ref, vw_ref, gl_ref, kcmp_ref, vcmp_ref,
                ovt_ref, kaug_ref, caug_ref, gsel_ref, o_ref,
                ksa, vsa, kwa, vwa, kca, vca, q_scr, sel_scr, *per_head, seq):
    g = pl.program_id(1)
    qi = pl.program_id(2)
    nsel = seq // SEL_BLOCK
    n_top = min(SEL_TOP, nsel)
    assert nsel <= SEL_LANES and N_LOCAL_BLOCKS * SEL_BLOCK >= QB and n_top > N_LOCAL_BLOCKS
    slab = 32
    ur = GQA_REP * QB // NSA_UNITS
    s_w, e_w, s_m, e_m, m_h, al_h, acc = [per_head[i * NSA_UNITS:(i + 1) * NSA_UNITS] for i in range(7)]

    def group_lanes(x):
        return jnp.where(g == 0, x, pltpu.roll(x, HEAD_DIM, axis=1))

    @pl.when(qi == 0)
    def _prep():
        ch = 256
        lane_c = lax.broadcasted_iota(jnp.int32, (ch, LANES), 1)
        is_k = lane_c < HEAD_DIM
        for c in range(seq // ch):
            sl = slice(c * ch, (c + 1) * ch)
            aug = kaug_ref[sl, :]
            ksa[sl, :] = jnp.where(is_k, group_lanes(ks_ref[sl, :]), aug).astype(bf16)
            kwa[sl, :] = jnp.where(is_k, group_lanes(kw_ref[sl, :]),
                                   jnp.where(lane_c < SEL_LANE0, aug, 0.0)).astype(bf16)
            for src, dst in ((vs_ref, vsa), (vw_ref, vwa)):
                v = group_lanes(src[sl, :])
                dst[sl, :LANES] = jnp.where(is_k, v, 1.0).astype(bf16)
                dst[sl, LANES:] = jnp.where(is_k, 1.0, pltpu.roll(v, HEAD_DIM, axis=1)).astype(bf16)
        lane_k = lax.broadcasted_iota(jnp.int32, kca.shape, 1)
        kca[...] = jnp.where(lane_k < HEAD_DIM, group_lanes(kcmp_ref[...]), caug_ref[...]).astype(bf16)
        vc = group_lanes(vcmp_ref[...])
        vca[...] = jnp.where(lane_k < HEAD_DIM, vc, pltpu.roll(vc, HEAD_DIM, axis=1)).astype(bf16)

    lane = lax.broadcasted_iota(jnp.int32, (QB, LANES), 1)
    left = lane < HEAD_DIM
    ii = lax.broadcasted_iota(jnp.int32, (QB, LANES), 0)
    t_row = qi * QB + ii
    t_hi = (t_row // POS_SPLIT).astype(f32)
    t_lo = (t_row % POS_SPLIT).astype(f32)
    slopes = [slope_ref[g * GQA_REP + r] for r in range(GQA_REP)]
    head = lambda a, r: a[r * QB:(r + 1) * QB]

    def q_head(r):
        x = q_ref[:, (r // 2) * LANES:(r // 2 + 1) * LANES] * SCALE
        if r % 2:
            x = pltpu.roll(x, HEAD_DIM, axis=1)
        m = slopes[r]
        pos = jnp.where(lane == POS_LANE0, (-POS_SPLIT * m) * t_hi,
                        jnp.where(lane == POS_LANE0 + 1, (-m) * t_lo,
                                  jnp.where(lane == POS_LANE0 + 2, POS_SPLIT * m,
                                            jnp.where(lane == POS_LANE0 + 3, m, 0.0))))
        return jnp.where(left, x, pos)

    q4 = jnp.concatenate([q_head(r) for r in range(GQA_REP)], axis=0)
    q4b = q4.astype(bf16)

    cmp_end = (lane * CMP_STRIDE + (CMP_BLOCK - 1))
    valid_c = t_row >= cmp_end
    s4 = _dot_nt(q4b, kca[...])
    ps = []
    p_sum = jnp.zeros((QB, LANES), f32)
    row_bcast = lambda col: jnp.broadcast_to(col, (QB, LANES))
    for r in range(GQA_REP):
        s = jnp.where(valid_c, head(s4, r), NEG)
        e = jnp.exp(s - row_bcast(jnp.max(s, axis=1, keepdims=True)))
        p = jnp.where(valid_c, e, 0.0) / row_bcast(jnp.sum(e, axis=1, keepdims=True))
        p_sum = p_sum + p
        ps.append(p.astype(bf16))
    o_cmp4 = _dot(jnp.concatenate(ps, axis=0), vca[...])

    back = t_row // SEL_BLOCK - lane
    valid_s = (back >= 0) & (lane < nsel)

    @pl.when(2 * qi + 2 <= n_top)
    def _all_valid():
        sel_scr[...] = jnp.where(valid_s, 1.0, 0.0)

    @pl.when(2 * qi + 2 > n_top)
    def _top_k():
        p_hi = p_sum.astype(bf16)
        p_lo = (p_sum - p_hi.astype(f32)).astype(bf16)
        imp_t = (_dot_nt(ovt_ref[...], p_hi) + _dot_nt(ovt_ref[...], p_lo))[:SEL_LANES]
        blk = lax.broadcasted_iota(jnp.int32, (SEL_LANES, QB), 0)
        tq = qi * QB + lax.broadcasted_iota(jnp.int32, (SEL_LANES, QB), 1)
        back_t = tq // SEL_BLOCK - blk
        valid_t = (back_t >= 0) & (blk < nsel)
        forced = (blk == 0) | (valid_t & (back_t < N_LOCAL_BLOCKS))
        score = jnp.where(valid_t, imp_t + jnp.where(forced, FORCE_BONUS, 0.0), NEG)
        score = jnp.where(blk < nsel, score, 2.0 * NEG)
        rank = jnp.zeros((SEL_LANES, QB), jnp.int32)
        for n in range(nsel):
            row = score[n:n + 1, :]
            ahead = (row > score) | ((row == score) & (blk > n))
            rank = rank + ahead.astype(jnp.int32)
        sel_t = jnp.where((rank < n_top) & valid_t, 1.0, 0.0)
        sel_t = jnp.concatenate([sel_t, jnp.zeros((LANES - SEL_LANES, QB), f32)], axis=0)
        sel_scr[...] = sel_t.T

    sel_bias = jnp.where((sel_scr[...] > 0.5) & (lane < 2 * qi), 0.0, MASK_BIG)
    sel_bias = pltpu.roll(sel_bias, SEL_LANE0, axis=1)
    in_sel = (lane >= SEL_LANE0) & (lane < SEL_LANE0 + SEL_LANES)
    for r in range(GQA_REP):
        q_scr[r * QB:(r + 1) * QB, :] = jnp.where(in_sel, sel_bias, head(q4, r)).astype(bf16)

    sig = jax.nn.sigmoid(gl_ref[...])
    g_hi = sig.astype(bf16)
    g_mid = (sig - g_hi.astype(f32)).astype(bf16)
    g_lo = (sig - g_hi.astype(f32) - g_mid.astype(f32)).astype(bf16)
    gate_b = _dot(jnp.concatenate([g_hi, g_mid, g_lo], axis=1), gsel_ref[...])

    def rel_tile(nk, offset):
        i = lax.broadcasted_iota(jnp.int32, (QB, nk), 0)
        j = lax.broadcasted_iota(jnp.int32, (QB, nk), 1)
        return (i - j + offset).astype(f32)

    def softmax_rows(s_ref, e_ref, m_ref, al_ref, nk, bias, running):
        for sl in range(ur // slab):
            rows = slice(sl * slab, (sl + 1) * slab)
            cols = [slice(j * LANES, (j + 1) * LANES) for j in range(nk // LANES)]
            i0 = (sl * slab) % QB
            tiles = [s_ref[rows, c] if bias is None else s_ref[rows, c] + bias[i0:i0 + slab, c] for c in cols]
            mx = tiles[0]
            for t in tiles[1:]:
                mx = jnp.maximum(mx, t)
            m_new = jnp.broadcast_to(jnp.max(mx, axis=1, keepdims=True), (slab, LANES))
            if running:
                m_old = m_ref[rows, :]
                m_new = jnp.maximum(m_old, m_new)
                al_ref[rows, :] = jnp.exp(m_old - m_new)
            if m_ref is not None:
                m_ref[rows, :] = m_new
            for c, t in zip(cols, tiles):
                e_ref[rows, c] = jnp.exp(t - m_new).astype(bf16)

    unit = lambda a, u: a[u * ur:(u + 1) * ur]

    def own_half(pv, u):
        heads = range(u * ur // QB, (u + 1) * ur // QB)
        return jnp.concatenate([pv[(r - heads[0]) * QB:(r - heads[0] + 1) * QB, (r % 2) * LANES:(r % 2 + 1) * LANES]
                                for r in heads], axis=0)

    def staggered(scores, probs, values):
        scores(0)
        for u in range(NSA_UNITS):
            if u + 1 < NSA_UNITS:
                scores(u + 1)
            probs(u)
            values(u)

    nwin = WINDOW + QB
    w0 = pl.multiple_of(jnp.maximum(qi - WINDOW // QB, 0) * QB, QB)
    dist_w = rel_tile(nwin, qi * QB - w0)
    bias_w = jnp.where((dist_w >= 0) & (dist_w < WINDOW), 0.0, NEG)
    o_w = [None] * NSA_UNITS

    def win_scores(u):
        s_w[u][...] = _dot_nt(unit(q4b, u), kwa[pl.ds(w0, nwin), :])

    def win_values(u):
        o_w[u] = own_half(_dot(e_w[u][...], vwa[pl.ds(w0, nwin), :]), u)

    staggered(win_scores, lambda u: softmax_rows(s_w[u], e_w[u], None, None, nwin, bias_w, False), win_values)

    d0 = pl.multiple_of(qi * QB, QB)
    bias_d = jnp.where(rel_tile(QB, 0) >= 0, 0.0, NEG)

    def diag_scores(u):
        s_m[u][:, :QB] = _dot_nt(unit(q4b, u), ksa[pl.ds(d0, QB), :])

    def diag_values(u):
        acc[u][...] = own_half(_dot(e_m[u][:, :QB], vsa[pl.ds(d0, QB), :]), u)

    staggered(diag_scores, lambda u: softmax_rows(s_m[u], e_m[u], m_h[u], None, QB, bias_d, False), diag_values)

    def sel_body(kc, _):
        rows = pl.ds(pl.multiple_of(kc * SEL_CHUNK, SEL_CHUNK), SEL_CHUNK)

        def scores(u):
            s_m[u][...] = _dot_nt(q_scr[u * ur:(u + 1) * ur, :], ksa[rows, :])

        def values(u):
            acc[u][...] = al_h[u][...] * acc[u][...] + own_half(_dot(e_m[u][...], vsa[rows, :]), u)

        staggered(scores, lambda u: softmax_rows(s_m[u], e_m[u], m_h[u], al_h[u], SEL_CHUNK, None, True), values)
        return 0

    per = SEL_CHUNK // QB
    lax.fori_loop(0, (qi + per - 1) // per, sel_body, 0)

    acc_s = jnp.concatenate([a[...] for a in acc], axis=0)
    acc_w = jnp.concatenate(o_w, axis=0)
    for pr in range(GQA_REP // 2):
        ev, od = 2 * pr, 2 * pr + 1
        out = gate_b[:, (3 * pr) * LANES:(3 * pr + 1) * LANES] * jnp.where(left, head(o_cmp4, ev), head(o_cmp4, od))
        for j, a in ((1, acc_s), (2, acc_w)):
            num = jnp.where(left, head(a, ev), head(a, od))
            den = pltpu.roll(jnp.where(left, head(a, od), head(a, ev)), HEAD_DIM, axis=1)
            out = out + gate_b[:, (3 * pr + j) * LANES:(3 * pr + j + 1) * LANES] / den * num
        o_ref[:, pr * LANES:(pr + 1) * LANES] = out


def _nsa(proj3, kcmp, vcmp, slopes_b, ovt, kaug, caug, gsel):
    B, S, _ = proj3.shape
    ncb = kcmp.shape[1]
    nrow = GQA_REP * QB
    ur = nrow // NSA_UNITS
    nwin = WINDOW + QB
    kv = lambda j: pl.BlockSpec((None, S, LANES), lambda b, g, qi, j=j: (b, 0, COL_KC + j))
    full = lambda a: pl.BlockSpec(a.shape, lambda b, g, qi: (0,) * a.ndim)
    return pl.pallas_call(
        functools.partial(_nsa_kernel, seq=S),
        grid=(B, N_KV_B, S // QB),
        in_specs=[pl.BlockSpec(memory_space=pltpu.SMEM),
                  pl.BlockSpec((None, QB, 2 * LANES), lambda b, g, qi: (b, qi, COL_QB + g)),
                  kv(2), kv(3), kv(4), kv(5),
                  pl.BlockSpec((None, QB, LANES), lambda b, g, qi: (b, qi, COL_GATE + g)),
                  pl.BlockSpec((None, ncb, LANES), lambda b, g, qi: (b, 0, 0)),
                  pl.BlockSpec((None, ncb, LANES), lambda b, g, qi: (b, 0, 0)),
                  full(ovt), full(kaug), full(caug), full(gsel)],
        out_specs=pl.BlockSpec((None, QB, 2 * LANES), lambda b, g, qi: (b, qi, g)),
        out_shape=jax.ShapeDtypeStruct((B, S, N_HEADS_B * HEAD_DIM), f32),
        scratch_shapes=([pltpu.VMEM((S, LANES), bf16), pltpu.VMEM((S, 2 * LANES), bf16)] * 2
                        + [pltpu.VMEM((ncb, LANES), bf16)] * 2
                        + [pltpu.VMEM((nrow, LANES), bf16), pltpu.VMEM((QB, LANES), f32)]
                        + [pltpu.VMEM((ur, nwin), f32)] * NSA_UNITS + [pltpu.VMEM((ur, nwin), bf16)] * NSA_UNITS
                        + [pltpu.VMEM((ur, SEL_CHUNK), f32)] * NSA_UNITS
                        + [pltpu.VMEM((ur, SEL_CHUNK), bf16)] * NSA_UNITS
                        + [pltpu.VMEM((ur, LANES), f32)] * (2 * NSA_UNITS)
                        + [pltpu.VMEM((ur, LANES), f32)] * NSA_UNITS),
        compiler_params=_params("parallel", "parallel", "arbitrary"),
        name="nsa_attn",
    )(slopes_b, proj3, proj3, proj3, proj3, proj3, proj3, kcmp, vcmp, ovt, kaug, caug, gsel)


def _out_proj_kernel(x_ref, oa_ref, ob_ref, wa_ref, wb_ref, o_ref):
    o_ref[...] = (x_ref[...] + _dot(oa_ref[...].astype(bf16), wa_ref[...])
                  + _dot(ob_ref[...].astype(bf16), wb_ref[...]))


def _out_proj(x2d, oa, ob, wa, wb):
    T = x2d.shape[0]
    tm = 512
    row = lambda w: pl.BlockSpec((tm, w), lambda i: (i, 0))
    full = lambda a: pl.BlockSpec(a.shape, lambda i: (0,) * a.ndim)
    return pl.pallas_call(
        _out_proj_kernel,
        grid=(T // tm,),
        in_specs=[row(D_MODEL), row(oa.shape[1]), row(ob.shape[1]), full(wa), full(wb)],
        out_specs=row(D_MODEL),
        out_shape=jax.ShapeDtypeStruct((T, D_MODEL), f32),
        compiler_params=_params("parallel"),
        name="out_proj",
    )(x2d, oa, ob, wa, wb)


def _router_kernel(x_ref, g_ref, w_ref, b_ref, h_ref, gate_ref):
    h = _rms(x_ref[...], g_ref[...])
    h_ref[...] = h.astype(bf16)
    logit = jnp.dot(h, w_ref[...], preferred_element_type=f32, precision=lax.Precision.HIGHEST) + b_ref[...]
    tm = logit.shape[0]
    lane = lax.broadcasted_iota(jnp.int32, (tm, LANES), 1)
    big = jnp.int32(LANES)
    is_g = lane < N_GROUPS
    gl = jnp.where(is_g, logit, NEG)
    gmax = jnp.max(gl, axis=1, keepdims=True)
    gsum = jnp.sum(jnp.where(is_g, jnp.exp(gl - gmax), 0.0), axis=1, keepdims=True)
    gsel = jnp.min(jnp.where(is_g & (gl == gmax), lane, big), axis=1, keepdims=True)
    gw = 1.0 / gsum
    e_lane = lane - N_GROUPS
    in_grp = (e_lane >= 0) & (e_lane < N_EXPERTS) & (e_lane // EXPERTS_PER_GROUP == gsel)
    el = jnp.where(in_grp, logit, NEG)
    t1 = jnp.max(el, axis=1, keepdims=True)
    i1 = jnp.min(jnp.where(in_grp & (el == t1), lane, big), axis=1, keepdims=True)
    el2 = jnp.where(lane == i1, NEG, el)
    t2 = jnp.max(el2, axis=1, keepdims=True)
    i2 = jnp.min(jnp.where(in_grp & (lane != i1) & (el2 == t2), lane, big), axis=1, keepdims=True)
    e2 = jnp.exp(t2 - t1)
    w1 = gw / (1.0 + e2)
    w2 = gw * e2 / (1.0 + e2)
    gate_ref[...] = jnp.where(lane == i1, w1, jnp.where(lane == i2, w2, 0.0))


def _router(x2d, g, w, b):
    T = x2d.shape[0]
    tm = 512
    return pl.pallas_call(
        _router_kernel,
        grid=(T // tm,),
        in_specs=[pl.BlockSpec((tm, D_MODEL), lambda i: (i, 0)),
                  pl.BlockSpec((1, D_MODEL), lambda i: (0, 0)),
                  pl.BlockSpec((D_MODEL, LANES), lambda i: (0, 0)),
                  pl.BlockSpec((1, LANES), lambda i: (0, 0))],
        out_specs=[pl.BlockSpec((tm, D_MODEL), lambda i: (i, 0)),
                   pl.BlockSpec((tm, LANES), lambda i: (i, 0))],
        out_shape=[jax.ShapeDtypeStruct((T, D_MODEL), bf16), jax.ShapeDtypeStruct((T, LANES), f32)],
        compiler_params=_params("parallel"),
        name="router",
    )(x2d, g, w, b)


def _moe_kernel(x_ref, h_ref, gate_ref, wg_ref, wu_ref, wd_ref, o_ref):
    e = pl.program_id(1)

    @pl.when(e == 0)
    def _init():
        o_ref[...] = x_ref[...]

    h = h_ref[...]
    lane = lax.broadcasted_iota(jnp.int32, gate_ref.shape, 1)
    gcol = jnp.sum(jnp.where(lane == e + N_GROUPS, gate_ref[...], 0.0), axis=1, keepdims=True)
    a = jax.nn.silu(_dot(h, wg_ref[...])) * _dot(h, wu_ref[...])
    o_ref[...] += _dot((a * gcol).astype(bf16), wd_ref[...])


def _moe(x2d, h, gates, wg, wu, wd, layer):
    T = x2d.shape[0]
    tm = 1024
    return pl.pallas_call(
        _moe_kernel,
        grid=(T // tm, N_EXPERTS),
        in_specs=[pl.BlockSpec((tm, D_MODEL), lambda i, e: (i, 0)),
                  pl.BlockSpec((tm, D_MODEL), lambda i, e: (i, 0)),
                  pl.BlockSpec((tm, LANES), lambda i, e: (i, 0)),
                  pl.BlockSpec((None, None, D_MODEL, D_EXPERT), lambda i, e: (layer, e, 0, 0)),
                  pl.BlockSpec((None, None, D_MODEL, D_EXPERT), lambda i, e: (layer, e, 0, 0)),
                  pl.BlockSpec((None, None, D_EXPERT, D_MODEL), lambda i, e: (layer, e, 0, 0))],
        out_specs=pl.BlockSpec((tm, D_MODEL), lambda i, e: (i, 0)),
        out_shape=jax.ShapeDtypeStruct((T, D_MODEL), f32),
        compiler_params=_params("parallel", "arbitrary"),
        name="moe_ffn",
    )(x2d, h, gates, wg, wu, wd)


def _ple_kernel(x_ref, p_ref, g_ref, wg_ref, wp_ref, fg_ref, o_ref, *, final):
    x = x_ref[...]
    gate = jax.nn.sigmoid(_dot(_rms(x, g_ref[...]).astype(bf16), wg_ref[...]))
    y = x + gate * _dot(p_ref[...].astype(bf16), wp_ref[...])
    o_ref[...] = _rms(y, fg_ref[...]) if final else y


def _ple(x2d, p3, g, wg, wp, fg, layer, final):
    T = x2d.shape[0]
    tm = 512
    full = lambda a: pl.BlockSpec(a.shape, lambda i: (0,) * a.ndim)
    return pl.pallas_call(
        functools.partial(_ple_kernel, final=final),
        grid=(T // tm,),
        in_specs=[pl.BlockSpec((tm, D_MODEL), lambda i: (i, 0)),
                  pl.BlockSpec((None, tm, PLE_DIM), lambda i: (layer, i, 0)),
                  full(g), full(wg), full(wp), full(fg)],
        out_specs=pl.BlockSpec((tm, D_MODEL), lambda i: (i, 0)),
        out_shape=jax.ShapeDtypeStruct((T, D_MODEL), f32),
        compiler_params=_params("parallel"),
        name="ple",
    )(x2d, p3, g, wg, wp, fg)


def _alibi_slopes():
    s = 2.0 ** (-8.0 * np.arange(1, N_HEADS_TOTAL + 1) / N_HEADS_TOTAL)
    assert np.all(np.log2(s[1::2]) == np.round(np.log2(s[1::2])))
    return jnp.asarray(s[0::2], f32), jnp.asarray(s[1::2], f32)


def _selection_constants(seq):
    ncb = seq // CMP_STRIDE
    nsel = seq // SEL_BLOCK
    n_cmp = (seq - CMP_BLOCK) // CMP_STRIDE + 1
    cs = np.arange(n_cmp) * CMP_STRIDE
    bs = np.arange(nsel) * SEL_BLOCK
    ov = np.clip(np.minimum(cs[:, None] + CMP_BLOCK, bs[None, :] + SEL_BLOCK)
                 - np.maximum(cs[:, None], bs[None, :]), 0, None) / CMP_BLOCK
    assert ncb == LANES and seq <= POS_SPLIT * 256
    ovt = np.zeros((LANES, ncb), np.float32)
    ovt[:nsel, :n_cmp] = ov.T
    pos = np.arange(seq)
    kaug = np.zeros((seq, LANES), np.float32)
    kaug[:, POS_LANE0:POS_LANE0 + 2] = 1.0
    kaug[:, POS_LANE0 + 2] = pos // POS_SPLIT
    kaug[:, POS_LANE0 + 3] = pos % POS_SPLIT
    kaug[pos, SEL_LANE0 + pos // SEL_BLOCK] = 1.0
    cend = np.arange(ncb) * CMP_STRIDE + CMP_BLOCK - 1
    caug = np.zeros((ncb, LANES), np.float32)
    caug[:, POS_LANE0:POS_LANE0 + 2] = 1.0
    caug[:, POS_LANE0 + 2] = cend // POS_SPLIT
    caug[:, POS_LANE0 + 3] = cend % POS_SPLIT
    gsel = np.zeros((3, LANES, GQA_REP // 2, 3, 2, HEAD_DIM), np.float32)
    for pr in range(GQA_REP // 2):
        for j in range(3):
            for hh in range(2):
                gsel[:, 3 * (2 * pr + hh) + j, pr, j, hh, :] = 1.0
    gsel = gsel.reshape(3 * LANES, (GQA_REP // 2) * 3 * LANES)
    return jnp.asarray(ovt, bf16), jnp.asarray(kaug), jnp.asarray(caug), jnp.asarray(gsel, bf16)


def _block_diag2(w):
    z = jnp.zeros_like(w)
    return jnp.concatenate([jnp.concatenate([w, z], axis=-1), jnp.concatenate([z, w], axis=-1)], axis=-2)


def _layout_w_in(w):
    gate = w[:, N_MAIN:]
    per = GQA_REP * 3
    blocks = [jnp.pad(gate[:, g * per:(g + 1) * per], ((0, 0), (0, LANES - per))) for g in range(N_KV_B)]
    return jnp.concatenate([w[:, :N_MAIN]] + blocks, axis=1).astype(bf16)


def kernel(x, p, attn_norm, w_in, w_out, w_cmp_k1, w_cmp_k2, w_cmp_v1, w_cmp_v2, cmp_pos, ffn_norm, w_route_group, b_route_group, w_route_expert, b_route_expert, w_expert_gate, w_expert_up, w_expert_down, ple_norm, w_ple_gate, w_ple_proj, final_norm):
    B, S, D = x.shape
    depth = w_in.shape[0]
    T = B * S
    slopes_a, slopes_b = _alibi_slopes()
    ovt, kaug, caug, gsel = _selection_constants(S)
    wg_all = w_expert_gate.astype(bf16)
    wu_all = w_expert_up.astype(bf16)
    wd_all = w_expert_down.astype(bf16)
    p3 = p.reshape(depth, T, PLE_DIM)
    row = lambda v: v.reshape(1, -1)
    n_route = N_GROUPS + N_EXPERTS

    x2d = x.reshape(T, D)
    for i in range(depth):
        proj3 = _in_proj(x2d, row(attn_norm[i]), _layout_w_in(w_in[i])).reshape(B, S, N_PROJ)
        oa = _dilated(proj3, slopes_a)
        w1 = lambda w: _block_diag2(w.reshape(CMP_BLOCK, HEAD_DIM, CMP_HIDDEN)).astype(bf16)
        pos_dup = jnp.concatenate([cmp_pos[i], cmp_pos[i]], axis=-1)
        kcmp, vcmp = _compress(proj3, pos_dup, w1(w_cmp_k1[i]), _block_diag2(w_cmp_k2[i]).astype(bf16),
                               w1(w_cmp_v1[i]), _block_diag2(w_cmp_v2[i]).astype(bf16))
        ob = _nsa(proj3, kcmp, vcmp, slopes_b, ovt, kaug, caug, gsel)
        wo = w_out[i].astype(bf16)
        x2d = _out_proj(x2d, oa.reshape(T, A_W), ob.reshape(T, -1), wo[:A_W], wo[A_W:])
        w_route = jnp.pad(jnp.concatenate([w_route_group[i], w_route_expert[i]], axis=1),
                          ((0, 0), (0, LANES - n_route)))
        b_route = jnp.pad(jnp.concatenate([b_route_group[i], b_route_expert[i]]), (0, LANES - n_route))
        h, gates = _router(x2d, row(ffn_norm[i]), w_route, row(b_route))
        x2d = _moe(x2d, h, gates, wg_all, wu_all, wd_all, i)
        x2d = _ple(x2d, p3, row(ple_norm[i]), w_ple_gate[i].astype(bf16), w_ple_proj[i].astype(bf16),
                   row(final_norm), i, i == depth - 1)
    return x2d.reshape(B, S, D)
```

```python
import functools

import numpy as np
import jax
import jax.numpy as jnp
from jax import lax
from jax.experimental import pallas as pl
from jax.experimental.pallas import tpu as pltpu

D_MODEL = 1024
PLE_DIM = 256
HEAD_DIM = 64
N_HEADS_A = 8
N_HEADS_B = 8
N_KV_B = 2
GQA_REP = N_HEADS_B // N_KV_B
N_HEADS_TOTAL = N_HEADS_A + N_HEADS_B
DILATED_PATTERNS = ((128, 1), (512, 4), (2048, 16))
CMP_BLOCK = 32
CMP_STRIDE = 16
CMP_HIDDEN = 256
SEL_BLOCK = 64
SEL_TOP = 16
N_LOCAL_BLOCKS = 2
WINDOW = 512
N_GROUPS = 4
EXPERTS_PER_GROUP = 4
N_EXPERTS = N_GROUPS * EXPERTS_PER_GROUP
D_EXPERT = 512
RMS_EPS = 1e-6
NEG = -1e30
FORCE_BONUS = 1e4
SCALE = HEAD_DIM ** -0.5

LANES = 128
QB = 128
SEL_CHUNK = 512
NSA_UNITS = 1
POS_LANE0 = HEAD_DIM
POS_SPLIT = 16
SEL_LANE0 = POS_LANE0 + 4
SEL_LANES = 32
MASK_BIG = -(2.0 ** 100)
MOE_TILE = 512
STAGE_W = D_MODEL + LANES
RANK_SPLIT = 128
DMA_BATCH = 256
A_W = N_HEADS_A * HEAD_DIM
N_MAIN = 3 * A_W + N_HEADS_B * HEAD_DIM + 6 * N_KV_B * HEAD_DIM
N_PROJ = N_MAIN + N_KV_B * LANES
COL_QB = (3 * A_W) // (2 * LANES)
COL_KC = (3 * A_W + N_HEADS_B * HEAD_DIM) // LANES
COL_GATE = N_MAIN // LANES
VMEM_LIMIT = 56 * 1024 * 1024

f32 = jnp.float32
bf16 = jnp.bfloat16


def _dot(a, b):
    return jnp.dot(a, b, preferred_element_type=f32)


def _dot_nt(a, b):
    return lax.dot_general(a, b, (((1,), (1,)), ((), ())), preferred_element_type=f32)


def _rms(x, g):
    return x * lax.rsqrt(jnp.mean(x * x, axis=-1, keepdims=True) + RMS_EPS) * g


def _params(*sem):
    return pltpu.CompilerParams(dimension_semantics=sem, vmem_limit_bytes=VMEM_LIMIT)


def _in_proj_kernel(x_ref, g_ref, w_ref, o_ref):
    h = _rms(x_ref[...], g_ref[...]).astype(bf16)
    for n0 in range(0, N_PROJ, 512):
        o_ref[:, n0:n0 + 512] = _dot(h, w_ref[:, n0:n0 + 512])


def _in_proj(x2d, g, w):
    T = x2d.shape[0]
    tm = 512
    return pl.pallas_call(
        _in_proj_kernel,
        grid=(T // tm,),
        in_specs=[pl.BlockSpec((tm, D_MODEL), lambda i: (i, 0)),
                  pl.BlockSpec((1, D_MODEL), lambda i: (0, 0)),
                  pl.BlockSpec((D_MODEL, N_PROJ), lambda i: (0, 0))],
        out_specs=pl.BlockSpec((tm, N_PROJ), lambda i: (i, 0)),
        out_shape=jax.ShapeDtypeStruct((T, N_PROJ), f32),
        compiler_params=_params("parallel"),
        name="in_proj",
    )(x2d, g, w)


def _dil_kernel(slope_ref, q_ref, k_ref, v_ref, o_ref, num_ref, m_ref, l_ref, *, seq):
    hp = pl.program_id(1)
    lane = lax.broadcasted_iota(jnp.int32, (QB, LANES), 1)
    left = lane < HEAD_DIM
    slopes = (slope_ref[2 * hp], slope_ref[2 * hp + 1])

    def rel_of(nk):
        i = lax.broadcasted_iota(jnp.int32, (QB, nk), 0)
        j = lax.broadcasted_iota(jnp.int32, (QB, nk), 1)
        return i - j + (nk - QB)

    def block(p, dil, row0, key0, nk):
        qc = q_ref[pl.ds(row0, QB, stride=dil), :] * SCALE
        kc = k_ref[pl.ds(key0, nk, stride=dil), :].astype(bf16)
        vc = v_ref[pl.ds(key0, nk, stride=dil), :].astype(bf16)
        rel = rel_of(nk)
        valid = (rel >= 0) & (rel <= QB)
        relf = rel.astype(f32)
        q2 = jnp.concatenate([jnp.where(left, qc, 0.0), jnp.where(left, 0.0, qc)], axis=0).astype(bf16)
        s2 = _dot_nt(q2, kc)
        es, ms, ls = [], [], []
        for hh in range(2):
            s = jnp.where(valid, s2[hh * QB:(hh + 1) * QB] - (slopes[hh] * float(dil)) * relf, NEG)
            mx = jnp.max(s, axis=1, keepdims=True)
            e = jnp.exp(s - mx)
            ms.append(mx)
            ls.append(jnp.sum(e, axis=1, keepdims=True))
            es.append(e.astype(bf16))
        num2 = _dot(jnp.concatenate(es, axis=0), vc)
        rows = pl.ds(row0, QB, stride=dil)
        num_ref[p, rows, :] = jnp.where(left, num2[:QB], num2[QB:])
        m_ref[p, rows, :] = jnp.where(left, ms[0], ms[1])
        l_ref[p, rows, :] = jnp.where(left, ls[0], ls[1])

    for p, (window, dil) in enumerate(DILATED_PATTERNS):
        assert window // dil == QB
        nblk = seq // dil // QB

        def per_class(r, _, p=p, dil=dil, nblk=nblk):
            block(p, dil, r, r, QB)

            def per_blk(a, _):
                block(p, dil, r + dil * QB * a, r + dil * QB * (a - 1), 2 * QB)
                return 0

            if nblk > 1:
                lax.fori_loop(1, nblk, per_blk, 0, unroll=3)
            return 0

        if dil == 1:
            per_class(0, 0)
        else:
            lax.fori_loop(0, dil, per_class, 0, unroll=4 if nblk == 1 else 1)

    ch = 256

    def combine(c, _):
        rows = pl.ds(pl.multiple_of(c * ch, ch), ch)
        ms = [m_ref[p, rows, :] for p in range(3)]
        big = jnp.maximum(jnp.maximum(ms[0], ms[1]), ms[2])
        num = jnp.zeros((ch, LANES), f32)
        den = jnp.zeros((ch, LANES), f32)
        for p in range(3):
            w = jnp.exp(ms[p] - big)
            num = num + w * num_ref[p, rows, :]
            den = den + w * l_ref[p, rows, :]
        o_ref[rows, :] = num / den
        return 0

    lax.fori_loop(0, seq // ch, combine, 0)


def _dilated(proj3, slopes_a):
    B, S, _ = proj3.shape
    npair = N_HEADS_A // 2
    blk = lambda off: pl.BlockSpec((None, S, LANES), lambda b, hp, off=off: (b, 0, off + hp))
    return pl.pallas_call(
        functools.partial(_dil_kernel, seq=S),
        grid=(B, npair),
        in_specs=[pl.BlockSpec(memory_space=pltpu.SMEM), blk(0), blk(npair), blk(2 * npair)],
        out_specs=pl.BlockSpec((None, S, LANES), lambda b, hp: (b, 0, hp)),
        out_shape=jax.ShapeDtypeStruct((B, S, A_W), f32),
        scratch_shapes=[pltpu.VMEM((3, S, LANES), f32)] * 3,
        compiler_params=_params("parallel", "parallel"),
        name="dilated_attn",
    )(slopes_a, proj3, proj3, proj3)


def _cmp_kernel(kc_ref, vc_ref, pos_ref, w1k_ref, w2k_ref, w1v_ref, w2v_ref, ko_ref, vo_ref, *, ncb):
    half = CMP_BLOCK // 2
    for x_ref, pi, w1_ref, w2_ref, o_ref in ((kc_ref, 0, w1k_ref, w2k_ref, ko_ref),
                                             (vc_ref, 1, w1v_ref, w2v_ref, vo_ref)):
        lo = jnp.zeros((ncb, 2 * CMP_HIDDEN), f32)
        hi = jnp.zeros((ncb, 2 * CMP_HIDDEN), f32)
        for r in range(half):
            x = x_ref[pl.ds(r, ncb, stride=CMP_STRIDE), :]
            lo = lo + _dot((x + pos_ref[pi, r:r + 1, :]).astype(bf16), w1_ref[r])
            hi = hi + _dot((x + pos_ref[pi, r + half:r + half + 1, :]).astype(bf16), w1_ref[r + half])
        h1 = lo + pltpu.roll(hi, ncb - 1, axis=0)
        o_ref[...] = _dot(jax.nn.gelu(h1).astype(bf16), w2_ref[...])


def _compress(proj3, pos_dup, w1k, w2k, w1v, w2v):
    B, S, _ = proj3.shape
    ncb = S // CMP_STRIDE
    full = lambda a: pl.BlockSpec(a.shape, lambda b: (0,) * a.ndim)
    out = jax.ShapeDtypeStruct((B, ncb, LANES), f32)
    return pl.pallas_call(
        functools.partial(_cmp_kernel, ncb=ncb),
        grid=(B,),
        in_specs=[pl.BlockSpec((None, S, LANES), lambda b: (b, 0, COL_KC)),
                  pl.BlockSpec((None, S, LANES), lambda b: (b, 0, COL_KC + 1)),
                  full(pos_dup), full(w1k), full(w2k), full(w1v), full(w2v)],
        out_specs=[pl.BlockSpec((None, ncb, LANES), lambda b: (b, 0, 0))] * 2,
        out_shape=[out, out],
        compiler_params=_params("parallel"),
        name="nsa_compress",
    )(proj3, proj3, pos_dup, w1k, w2k, w1v, w2v)


def _nsa_kernel(slope_ref, q_ref, ks_ref, vs_ref, kw_ref, vw_ref, gl_ref, kcmp_ref, vcmp_ref,
                ovt_ref, kaug_ref, caug_ref, gsel_ref, o_ref,
                ksa, vsa, kwa, vwa, kca, vca, q_scr, sel_scr, *per_head, seq):
    g = pl.program_id(1)
    qi = pl.program_id(2)
    nsel = seq // SEL_BLOCK
    n_top = min(SEL_TOP, nsel)
    assert nsel <= SEL_LANES and N_LOCAL_BLOCKS * SEL_BLOCK >= QB and n_top > N_LOCAL_BLOCKS
    slab = 32
    ur = GQA_REP * QB // NSA_UNITS
    s_w, e_w, s_m, e_m, m_h, al_h, acc = [per_head[i * NSA_UNITS:(i + 1) * NSA_UNITS] for i in range(7)]

    def group_lanes(x):
        return jnp.where(g == 0, x, pltpu.roll(x, HEAD_DIM, axis=1))

    @pl.when(qi == 0)
    def _prep():
        ch = 256
        lane_c = lax.broadcasted_iota(jnp.int32, (ch, LANES), 1)
        is_k = lane_c < HEAD_DIM
        for c in range(seq // ch):
            sl = slice(c * ch, (c + 1) * ch)
            aug = kaug_ref[sl, :]
            ksa[sl, :] = jnp.where(is_k, group_lanes(ks_ref[sl, :]), aug).astype(bf16)
            kwa[sl, :] = jnp.where(is_k, group_lanes(kw_ref[sl, :]),
                                   jnp.where(lane_c < SEL_LANE0, aug, 0.0)).astype(bf16)
            for src, dst in ((vs_ref, vsa), (vw_ref, vwa)):
                v = group_lanes(src[sl, :])
                dst[sl, :LANES] = jnp.where(is_k, v, 1.0).astype(bf16)
                dst[sl, LANES:] = jnp.where(is_k, 1.0, pltpu.roll(v, HEAD_DIM, axis=1)).astype(bf16)
        lane_k = lax.broadcasted_iota(jnp.int32, kca.shape, 1)
        kca[...] = jnp.where(lane_k < HEAD_DIM, group_lanes(kcmp_ref[...]), caug_ref[...]).astype(bf16)
        vc = group_lanes(vcmp_ref[...])
        vca[...] = jnp.where(lane_k < HEAD_DIM, vc, pltpu.roll(vc, HEAD_DIM, axis=1)).astype(bf16)

    lane = lax.broadcasted_iota(jnp.int32, (QB, LANES), 1)
    left = lane < HEAD_DIM
    ii = lax.broadcasted_iota(jnp.int32, (QB, LANES), 0)
    t_row = qi * QB + ii
    t_hi = (t_row // POS_SPLIT).astype(f32)
    t_lo = (t_row % POS_SPLIT).astype(f32)
    slopes = [slope_ref[g * GQA_REP + r] for r in range(GQA_REP)]
    head = lambda a, r: a[r * QB:(r + 1) * QB]

    def q_head(r):
        x = q_ref[:, (r // 2) * LANES:(r // 2 + 1) * LANES] * SCALE
        if r % 2:
            x = pltpu.roll(x, HEAD_DIM, axis=1)
        m = slopes[r]
        pos = jnp.where(lane == POS_LANE0, (-POS_SPLIT * m) * t_hi,
                        jnp.where(lane == POS_LANE0 + 1, (-m) * t_lo,
                                  jnp.where(lane == POS_LANE0 + 2, POS_SPLIT * m,
                                            jnp.where(lane == POS_LANE0 + 3, m, 0.0))))
        return jnp.where(left, x, pos)

    q4 = jnp.concatenate([q_head(r) for r in range(GQA_REP)], axis=0)
    q4b = q4.astype(bf16)

    cmp_end = (lane * CMP_STRIDE + (CMP_BLOCK - 1))
    valid_c = t_row >= cmp_end
    s4 = _dot_nt(q4b, kca[...])
    ps = []
    p_sum = jnp.zeros((QB, LANES), f32)
    row_bcast = lambda col: jnp.broadcast_to(col, (QB, LANES))
    for r in range(GQA_REP):
        s = jnp.where(valid_c, head(s4, r), NEG)
        e = jnp.exp(s - row_bcast(jnp.max(s, axis=1, keepdims=True)))
        p = jnp.where(valid_c, e, 0.0) / row_bcast(jnp.sum(e, axis=1, keepdims=True))
        p_sum = p_sum + p
        ps.append(p.astype(bf16))
    o_cmp4 = _dot(jnp.concatenate(ps, axis=0), vca[...])

    back = t_row // SEL_BLOCK - lane
    valid_s = (back >= 0) & (lane < nsel)

    @pl.when(2 * qi + 2 <= n_top)
    def _all_valid():
        sel_scr[...] = jnp.where(valid_s, 1.0, 0.0)

    @pl.when(2 * qi + 2 > n_top)
    def _top_k():
        p_hi = p_sum.astype(bf16)
        p_lo = (p_sum - p_hi.astype(f32)).astype(bf16)
        imp_t = (_dot_nt(ovt_ref[...], p_hi) + _dot_nt(ovt_ref[...], p_lo))[:SEL_LANES]
        blk = lax.broadcasted_iota(jnp.int32, (SEL_LANES, QB), 0)
        tq = qi * QB + lax.broadcasted_iota(jnp.int32, (SEL_LANES, QB), 1)
        back_t = tq // SEL_BLOCK - blk
        valid_t = (back_t >= 0) & (blk < nsel)
        forced = (blk == 0) | (valid_t & (back_t < N_LOCAL_BLOCKS))
        score = jnp.where(valid_t, imp_t + jnp.where(forced, FORCE_BONUS, 0.0), NEG)
        score = jnp.where(blk < nsel, score, 2.0 * NEG)
        rank = jnp.zeros((SEL_LANES, QB), jnp.int32)
        for n in range(nsel):
            row = score[n:n + 1, :]
            ahead = (row > score) | ((row == score) & (blk > n))
            rank = rank + ahead.astype(jnp.int32)
        sel_t = jnp.where((rank < n_top) & valid_t, 1.0, 0.0)
        sel_t = jnp.concatenate([sel_t, jnp.zeros((LANES - SEL_LANES, QB), f32)], axis=0)
        sel_scr[...] = sel_t.T

    sel_bias = jnp.where((sel_scr[...] > 0.5) & (lane < 2 * qi), 0.0, MASK_BIG)
    sel_bias = pltpu.roll(sel_bias, SEL_LANE0, axis=1)
    in_sel = (lane >= SEL_LANE0) & (lane < SEL_LANE0 + SEL_LANES)
    for r in range(GQA_REP):
        q_scr[r * QB:(r + 1) * QB, :] = jnp.where(in_sel, sel_bias, head(q4, r)).astype(bf16)

    sig = jax.nn.sigmoid(gl_ref[...])
    g_hi = sig.astype(bf16)
    g_mid = (sig - g_hi.astype(f32)).astype(bf16)
    g_lo = (sig - g_hi.astype(f32) - g_mid.astype(f32)).astype(bf16)
    gate_b = _dot(jnp.concatenate([g_hi, g_mid, g_lo], axis=1), gsel_ref[...])

    def rel_tile(nk, offset):
        i = lax.broadcasted_iota(jnp.int32, (QB, nk), 0)
        j = lax.broadcasted_iota(jnp.int32, (QB, nk), 1)
        return (i - j + offset).astype(f32)

    def softmax_rows(s_ref, e_ref, m_ref, al_ref, nk, bias, running):
        for sl in range(ur // slab):
            rows = slice(sl * slab, (sl + 1) * slab)
            cols = [slice(j * LANES, (j + 1) * LANES) for j in range(nk // LANES)]
            i0 = (sl * slab) % QB
            tiles = [s_ref[rows, c] if bias is None else s_ref[rows, c] + bias[i0:i0 + slab, c] for c in cols]
            mx = tiles[0]
            for t in tiles[1:]:
                mx = jnp.maximum(mx, t)
            m_new = jnp.broadcast_to(jnp.max(mx, axis=1, keepdims=True), (slab, LANES))
            if running:
                m_old = m_ref[rows, :]
                m_new = jnp.maximum(m_old, m_new)
                al_ref[rows, :] = jnp.exp(m_old - m_new)
            if m_ref is not None:
                m_ref[rows, :] = m_new
            for c, t in zip(cols, tiles):
                e_ref[rows, c] = jnp.exp(t - m_new).astype(bf16)

    unit = lambda a, u: a[u * ur:(u + 1) * ur]

    def own_half(pv, u):
        heads = range(u * ur // QB, (u + 1) * ur // QB)
        return jnp.concatenate([pv[(r - heads[0]) * QB:(r - heads[0] + 1) * QB, (r % 2) * LANES:(r % 2 + 1) * LANES]
                                for r in heads], axis=0)

    def staggered(scores, probs, values):
        scores(0)
        for u in range(NSA_UNITS):
            if u + 1 < NSA_UNITS:
                scores(u + 1)
            probs(u)
            values(u)

    nwin = WINDOW + QB
    w0 = pl.multiple_of(jnp.maximum(qi - WINDOW // QB, 0) * QB, QB)
    dist_w = rel_tile(nwin, qi * QB - w0)
    bias_w = jnp.where((dist_w >= 0) & (dist_w < WINDOW), 0.0, NEG)
    o_w = [None] * NSA_UNITS

    def win_scores(u):
        s_w[u][...] = _dot_nt(unit(q4b, u), kwa[pl.ds(w0, nwin), :])

    def win_values(u):
        o_w[u] = own_half(_dot(e_w[u][...], vwa[pl.ds(w0, nwin), :]), u)

    staggered(win_scores, lambda u: softmax_rows(s_w[u], e_w[u], None, None, nwin, bias_w, False), win_values)

    d0 = pl.multiple_of(qi * QB, QB)
    bias_d = jnp.where(rel_tile(QB, 0) >= 0, 0.0, NEG)

    def diag_scores(u):
        s_m[u][:, :QB] = _dot_nt(unit(q4b, u), ksa[pl.ds(d0, QB), :])

    def diag_values(u):
        acc[u][...] = own_half(_dot(e_m[u][:, :QB], vsa[pl.ds(d0, QB), :]), u)

    staggered(diag_scores, lambda u: softmax_rows(s_m[u], e_m[u], m_h[u], None, QB, bias_d, False), diag_values)

    def sel_body(kc, _):
        rows = pl.ds(pl.multiple_of(kc * SEL_CHUNK, SEL_CHUNK), SEL_CHUNK)

        def scores(u):
            s_m[u][...] = _dot_nt(q_scr[u * ur:(u + 1) * ur, :], ksa[rows, :])

        def values(u):
            acc[u][...] = al_h[u][...] * acc[u][...] + own_half(_dot(e_m[u][...], vsa[rows, :]), u)

        staggered(scores, lambda u: softmax_rows(s_m[u], e_m[u], m_h[u], al_h[u], SEL_CHUNK, None, True), values)
        return 0

    per = SEL_CHUNK // QB
    lax.fori_loop(0, (qi + per - 1) // per, sel_body, 0)

    acc_s = jnp.concatenate([a[...] for a in acc], axis=0)
    acc_w = jnp.concatenate(o_w, axis=0)
    for pr in range(GQA_REP // 2):
        ev, od = 2 * pr, 2 * pr + 1
        out = gate_b[:, (3 * pr) * LANES:(3 * pr + 1) * LANES] * jnp.where(left, head(o_cmp4, ev), head(o_cmp4, od))
        for j, a in ((1, acc_s), (2, acc_w)):
            num = jnp.where(left, head(a, ev), head(a, od))
            den = pltpu.roll(jnp.where(left, head(a, od), head(a, ev)), HEAD_DIM, axis=1)
            out = out + gate_b[:, (3 * pr + j) * LANES:(3 * pr + j + 1) * LANES] / den * num
        o_ref[:, pr * LANES:(pr + 1) * LANES] = out


def _nsa(proj3, kcmp, vcmp, slopes_b, ovt, kaug, caug, gsel):
    B, S, _ = proj3.shape
    ncb = kcmp.shape[1]
    nrow = GQA_REP * QB
    ur = nrow // NSA_UNITS
    nwin = WINDOW + QB
    kv = lambda j: pl.BlockSpec((None, S, LANES), lambda b, g, qi, j=j: (b, 0, COL_KC + j))
    full = lambda a: pl.BlockSpec(a.shape, lambda b, g, qi: (0,) * a.ndim)
    return pl.pallas_call(
        functools.partial(_nsa_kernel, seq=S),
        grid=(B, N_KV_B, S // QB),
        in_specs=[pl.BlockSpec(memory_space=pltpu.SMEM),
                  pl.BlockSpec((None, QB, 2 * LANES), lambda b, g, qi: (b, qi, COL_QB + g)),
                  kv(2), kv(3), kv(4), kv(5),
                  pl.BlockSpec((None, QB, LANES), lambda b, g, qi: (b, qi, COL_GATE + g)),
                  pl.BlockSpec((None, ncb, LANES), lambda b, g, qi: (b, 0, 0)),
                  pl.BlockSpec((None, ncb, LANES), lambda b, g, qi: (b, 0, 0)),
                  full(ovt), full(kaug), full(caug), full(gsel)],
        out_specs=pl.BlockSpec((None, QB, 2 * LANES), lambda b, g, qi: (b, qi, g)),
        out_shape=jax.ShapeDtypeStruct((B, S, N_HEADS_B * HEAD_DIM), f32),
        scratch_shapes=([pltpu.VMEM((S, LANES), bf16), pltpu.VMEM((S, 2 * LANES), bf16)] * 2
                        + [pltpu.VMEM((ncb, LANES), bf16)] * 2
                        + [pltpu.VMEM((nrow, LANES), bf16), pltpu.VMEM((QB, LANES), f32)]
                        + [pltpu.VMEM((ur, nwin), f32)] * NSA_UNITS + [pltpu.VMEM((ur, nwin), bf16)] * NSA_UNITS
                        + [pltpu.VMEM((ur, SEL_CHUNK), f32)] * NSA_UNITS
                        + [pltpu.VMEM((ur, SEL_CHUNK), bf16)] * NSA_UNITS
                        + [pltpu.VMEM((ur, LANES), f32)] * (2 * NSA_UNITS)
                        + [pltpu.VMEM((ur, LANES), f32)] * NSA_UNITS),
        compiler_params=_params("parallel", "parallel", "arbitrary"),
        name="nsa_attn",
    )(slopes_b, proj3, proj3, proj3, proj3, proj3, proj3, kcmp, vcmp, ovt, kaug, caug, gsel)


def _out_proj_kernel(x_ref, oa_ref, ob_ref, wa_ref, wb_ref, o_ref):
    o_ref[...] = (x_ref[...] + _dot(oa_ref[...].astype(bf16), wa_ref[...])
                  + _dot(ob_ref[...].astype(bf16), wb_ref[...]))


def _out_proj(x2d, oa, ob, wa, wb):
    T = x2d.shape[0]
    tm = 512
    row = lambda w: pl.BlockSpec((tm, w), lambda i: (i, 0))
    full = lambda a: pl.BlockSpec(a.shape, lambda i: (0,) * a.ndim)
    return pl.pallas_call(
        _out_proj_kernel,
        grid=(T // tm,),
        in_specs=[row(D_MODEL), row(oa.shape[1]), row(ob.shape[1]), full(wa), full(wb)],
        out_specs=row(D_MODEL),
        out_shape=jax.ShapeDtypeStruct((T, D_MODEL), f32),
        compiler_params=_params("parallel"),
        name="out_proj",
    )(x2d, oa, ob, wa, wb)


def _router_kernel(x_ref, g_ref, w_ref, b_ref, tri_ref, stage_ref, meta_ref, cnt_ref, cnt_scr):
    @pl.when(pl.program_id(0) == 0)
    def _zero():
        cnt_scr[...] = jnp.zeros_like(cnt_scr)

    h = _rms(x_ref[...], g_ref[...])
    stage_ref[:, :D_MODEL] = h
    logit = jnp.dot(h, w_ref[...], preferred_element_type=f32, precision=lax.Precision.HIGHEST) + b_ref[...]
    tm = logit.shape[0]
    lane = lax.broadcasted_iota(jnp.int32, (tm, LANES), 1)
    big = jnp.int32(LANES)
    is_g = lane < N_GROUPS
    gl = jnp.where(is_g, logit, NEG)
    gmax = jnp.max(gl, axis=1, keepdims=True)
    gsum = jnp.sum(jnp.where(is_g, jnp.exp(gl - gmax), 0.0), axis=1, keepdims=True)
    gsel = jnp.min(jnp.where(is_g & (gl == gmax), lane, big), axis=1, keepdims=True)
    gw = 1.0 / gsum
    e_lane = lane - N_GROUPS
    in_grp = (e_lane >= 0) & (e_lane < N_EXPERTS) & (e_lane // EXPERTS_PER_GROUP == gsel)
    el = jnp.where(in_grp, logit, NEG)
    t1 = jnp.max(el, axis=1, keepdims=True)
    i1 = jnp.min(jnp.where(in_grp & (el == t1), lane, big), axis=1, keepdims=True)
    el2 = jnp.where(lane == i1, NEG, el)
    t2 = jnp.max(el2, axis=1, keepdims=True)
    i2 = jnp.min(jnp.where(in_grp & (lane != i1) & (el2 == t2), lane, big), axis=1, keepdims=True)
    e2 = jnp.exp(t2 - t1)
    w1 = gw / (1.0 + e2)
    w2 = gw * e2 / (1.0 + e2)
    stage_ref[:, D_MODEL:] = jnp.where(lane == i1, w1, jnp.where(lane == i2, w2, 0.0))

    onehot = jnp.where(is_g & (lane == gsel), 1.0, 0.0)
    before = _dot(tri_ref[...], onehot.astype(bf16)) + cnt_scr[...]
    rank = jnp.sum(onehot * before, axis=1, keepdims=True)
    cnt_scr[...] = before[tm - 1:tm, :] + onehot[tm - 1:tm, :]
    cnt_ref[...] = cnt_scr[...]
    rank_hi = jnp.floor(rank * (1.0 / RANK_SPLIT))
    cols = jnp.where(lane == 0, gsel.astype(f32), jnp.where(lane == 1, rank_hi,
                                                            jnp.where(lane == 2, rank - RANK_SPLIT * rank_hi, 0.0)))
    pick = (lax.broadcasted_iota(jnp.int32, (8, LANES), 0) == lax.broadcasted_iota(jnp.int32, (8, LANES), 1))
    meta_ref[...] = _dot_nt(jnp.where(pick, 1.0, 0.0).astype(bf16), cols.astype(bf16))


def _router(x2d, g, w, b):
    T = x2d.shape[0]
    tm = MOE_TILE
    assert T // RANK_SPLIT <= 256
    tri = jnp.asarray(np.tril(np.ones((tm, tm), np.float32), -1), bf16)
    return pl.pallas_call(
        _router_kernel,
        grid=(T // tm,),
        in_specs=[pl.BlockSpec((tm, D_MODEL), lambda i: (i, 0)),
                  pl.BlockSpec((1, D_MODEL), lambda i: (0, 0)),
                  pl.BlockSpec((D_MODEL, LANES), lambda i: (0, 0)),
                  pl.BlockSpec((1, LANES), lambda i: (0, 0)),
                  pl.BlockSpec((tm, tm), lambda i: (0, 0))],
        out_specs=[pl.BlockSpec((tm, STAGE_W), lambda i: (i, 0)),
                   pl.BlockSpec((None, 8, tm), lambda i: (i, 0, 0)),
                   pl.BlockSpec((1, LANES), lambda i: (0, 0))],
        out_shape=[jax.ShapeDtypeStruct((T, STAGE_W), f32),
                   jax.ShapeDtypeStruct((T // tm, 8, tm), f32),
                   jax.ShapeDtypeStruct((1, LANES), f32)],
        scratch_shapes=[pltpu.VMEM((1, LANES), f32)],
        compiler_params=_params("arbitrary"),
        name="router",
    )(x2d, g, w, b, tri)


def _row_copies(n, src_row, dst_row, src_ref, dst_ref, sems):
    nb = n // DMA_BATCH

    def wait(b):
        pltpu.make_async_copy(src_ref.at[pl.ds(0, DMA_BATCH)], dst_ref.at[pl.ds(0, DMA_BATCH)],
                              sems.at[b % 2]).wait()

    def batch(b, _):
        def one(k, _):
            t = b * DMA_BATCH + k
            pltpu.make_async_copy(src_ref.at[pl.ds(src_row(t), 1)], dst_ref.at[pl.ds(dst_row(t), 1)],
                                  sems.at[b % 2]).start()
            return 0

        lax.fori_loop(0, DMA_BATCH, one, 0, unroll=8)

        @pl.when(b > 0)
        def _():
            wait(b - 1)
        return 0

    lax.fori_loop(0, nb, batch, 0)
    wait(nb - 1)


def _dispatch_kernel(pos_ref, pad_ref, stage_ref, zero_ref, xs_ref, sems):
    n = stage_ref.shape[0]
    _row_copies(n, lambda t: t, lambda t: pos_ref[t], stage_ref, xs_ref, sems)
    npad = xs_ref.shape[0] - n
    for r in range(N_GROUPS + 1):
        def fill(row, _):
            pltpu.make_async_copy(zero_ref.at[pl.ds(0, 1)], xs_ref.at[pl.ds(row, 1)], sems.at[2]).start()
            return 0

        lax.fori_loop(pad_ref[2 * r], pad_ref[2 * r + 1], fill, 0)
    pltpu.make_async_copy(stage_ref.at[pl.ds(0, npad)], xs_ref.at[pl.ds(0, npad)], sems.at[2]).wait()


def _dispatch(stage, pos, pad, n_rows):
    any_spec = pl.BlockSpec(memory_space=pl.ANY)
    zero = jnp.zeros((8, stage.shape[1]), f32)
    return pl.pallas_call(
        _dispatch_kernel,
        grid_spec=pltpu.PrefetchScalarGridSpec(
            num_scalar_prefetch=2, grid=(1,), in_specs=[any_spec, any_spec], out_specs=any_spec,
            scratch_shapes=[pltpu.SemaphoreType.DMA((3,))]),
        out_shape=jax.ShapeDtypeStruct((n_rows, stage.shape[1]), f32),
        compiler_params=pltpu.CompilerParams(dimension_semantics=("arbitrary",), has_side_effects=True),
        name="moe_dispatch",
    )(pos, pad, stage, zero)


def _combine_kernel(pos_ref, ys_ref, y_ref, sems):
    _row_copies(y_ref.shape[0], lambda t: pos_ref[t], lambda t: t, ys_ref, y_ref, sems)


def _combine(ys, pos):
    any_spec = pl.BlockSpec(memory_space=pl.ANY)
    return pl.pallas_call(
        _combine_kernel,
        grid_spec=pltpu.PrefetchScalarGridSpec(
            num_scalar_prefetch=1, grid=(1,), in_specs=[any_spec], out_specs=any_spec,
            scratch_shapes=[pltpu.SemaphoreType.DMA((2,))]),
        out_shape=jax.ShapeDtypeStruct((pos.shape[0], ys.shape[1]), f32),
        compiler_params=pltpu.CompilerParams(dimension_semantics=("arbitrary",), has_side_effects=True),
        name="moe_combine",
    )(pos, ys)


def _moe_kernel(tg_ref, xs_ref, wg_ref, wu_ref, wd_ref, o_ref):
    grp = tg_ref[pl.program_id(0)]
    x = xs_ref[:, :D_MODEL].astype(bf16)
    gates = xs_ref[:, D_MODEL:]
    lane = lax.broadcasted_iota(jnp.int32, gates.shape, 1)
    y = jnp.zeros(o_ref.shape, f32)
    for e in range(EXPERTS_PER_GROUP):
        gcol = jnp.sum(jnp.where(lane == N_GROUPS + EXPERTS_PER_GROUP * grp + e, gates, 0.0), axis=1, keepdims=True)
        a = jax.nn.silu(_dot(x, wg_ref[e])) * _dot(x, wu_ref[e])
        y = y + _dot((a * gcol).astype(bf16), wd_ref[e])
    o_ref[...] = y


def _moe(xs, tile_group, wg, wu, wd, layer):
    tm = MOE_TILE
    n_rows = xs.shape[0]
    w_spec = lambda k, n: pl.BlockSpec((None, None, EXPERTS_PER_GROUP, k, n), lambda j, tg: (layer, tg[j], 0, 0, 0))
    grouped = lambda w: w.reshape(w.shape[0], N_GROUPS, EXPERTS_PER_GROUP, *w.shape[2:])
    return pl.pallas_call(
        _moe_kernel,
        grid_spec=pltpu.PrefetchScalarGridSpec(
            num_scalar_prefetch=1, grid=(n_rows // tm,),
            in_specs=[pl.BlockSpec((tm, STAGE_W), lambda j, tg: (j, 0)),
                      w_spec(D_MODEL, D_EXPERT), w_spec(D_MODEL, D_EXPERT), w_spec(D_EXPERT, D_MODEL)],
            out_specs=pl.BlockSpec((tm, D_MODEL), lambda j, tg: (j, 0))),
        out_shape=jax.ShapeDtypeStruct((n_rows, D_MODEL), f32),
        compiler_params=_params("arbitrary"),
        name="moe_ffn",
    )(tile_group, xs, grouped(wg), grouped(wu), grouped(wd))


def _ple_kernel(x_ref, y_ref, p_ref, g_ref, wg_ref, wp_ref, fg_ref, o_ref, *, final):
    x = x_ref[...] + y_ref[...]
    gate = jax.nn.sigmoid(_dot(_rms(x, g_ref[...]).astype(bf16), wg_ref[...]))
    y = x + gate * _dot(p_ref[...].astype(bf16), wp_ref[...])
    o_ref[...] = _rms(y, fg_ref[...]) if final else y


def _ple(x2d, y2d, p3, g, wg, wp, fg, layer, final):
    T = x2d.shape[0]
    tm = 512
    full = lambda a: pl.BlockSpec(a.shape, lambda i: (0,) * a.ndim)
    return pl.pallas_call(
        functools.partial(_ple_kernel, final=final),
        grid=(T // tm,),
        in_specs=[pl.BlockSpec((tm, D_MODEL), lambda i: (i, 0)),
                  pl.BlockSpec((tm, D_MODEL), lambda i: (i, 0)),
                  pl.BlockSpec((None, tm, PLE_DIM), lambda i: (layer, i, 0)),
                  full(g), full(wg), full(wp), full(fg)],
        out_specs=pl.BlockSpec((tm, D_MODEL), lambda i: (i, 0)),
        out_shape=jax.ShapeDtypeStruct((T, D_MODEL), f32),
        compiler_params=_params("parallel"),
        name="ple",
    )(x2d, y2d, p3, g, wg, wp, fg)


def _alibi_slopes():
    s = 2.0 ** (-8.0 * np.arange(1, N_HEADS_TOTAL + 1) / N_HEADS_TOTAL)
    assert np.all(np.log2(s[1::2]) == np.round(np.log2(s[1::2])))
    return jnp.asarray(s[0::2], f32), jnp.asarray(s[1::2], f32)


def _selection_constants(seq):
    ncb = seq // CMP_STRIDE
    nsel = seq // SEL_BLOCK
    n_cmp = (seq - CMP_BLOCK) // CMP_STRIDE + 1
    cs = np.arange(n_cmp) * CMP_STRIDE
    bs = np.arange(nsel) * SEL_BLOCK
    ov = np.clip(np.minimum(cs[:, None] + CMP_BLOCK, bs[None, :] + SEL_BLOCK)
                 - np.maximum(cs[:, None], bs[None, :]), 0, None) / CMP_BLOCK
    assert ncb == LANES and seq <= POS_SPLIT * 256
    ovt = np.zeros((LANES, ncb), np.float32)
    ovt[:nsel, :n_cmp] = ov.T
    pos = np.arange(seq)
    kaug = np.zeros((seq, LANES), np.float32)
    kaug[:, POS_LANE0:POS_LANE0 + 2] = 1.0
    kaug[:, POS_LANE0 + 2] = pos // POS_SPLIT
    kaug[:, POS_LANE0 + 3] = pos % POS_SPLIT
    kaug[pos, SEL_LANE0 + pos // SEL_BLOCK] = 1.0
    cend = np.arange(ncb) * CMP_STRIDE + CMP_BLOCK - 1
    caug = np.zeros((ncb, LANES), np.float32)
    caug[:, POS_LANE0:POS_LANE0 + 2] = 1.0
    caug[:, POS_LANE0 + 2] = cend // POS_SPLIT
    caug[:, POS_LANE0 + 3] = cend % POS_SPLIT
    gsel = np.zeros((3, LANES, GQA_REP // 2, 3, 2, HEAD_DIM), np.float32)
    for pr in range(GQA_REP // 2):
        for j in range(3):
            for hh in range(2):
                gsel[:, 3 * (2 * pr + hh) + j, pr, j, hh, :] = 1.0
    gsel = gsel.reshape(3 * LANES, (GQA_REP // 2) * 3 * LANES)
    return jnp.asarray(ovt, bf16), jnp.asarray(kaug), jnp.asarray(caug), jnp.asarray(gsel, bf16)


def _dispatch_plan(meta, counts, n_tokens):
    i32 = jnp.int32
    n_rows = n_tokens + N_GROUPS * MOE_TILE
    group = meta[:, 0, :].reshape(n_tokens).astype(i32)
    rank = (meta[:, 1, :] * RANK_SPLIT + meta[:, 2, :]).reshape(n_tokens).astype(i32)
    cnt = counts[0, :N_GROUPS].astype(i32)
    padded = (cnt + MOE_TILE - 1) // MOE_TILE * MOE_TILE
    end = jnp.cumsum(padded)
    off = end - padded
    pos = off[group] + rank
    pad = jnp.stack([jnp.concatenate([off + cnt, end[-1:]]),
                     jnp.concatenate([end, jnp.full((1,), n_rows, i32)])], axis=1).reshape(-1)
    tile_start = jnp.arange(n_rows // MOE_TILE, dtype=i32) * MOE_TILE
    tile_group = jnp.minimum(jnp.sum(tile_start[:, None] >= end[None, :], axis=1), N_GROUPS - 1).astype(i32)
    return pos, pad.astype(i32), tile_group, n_rows


def _block_diag2(w):
    z = jnp.zeros_like(w)
    return jnp.concatenate([jnp.concatenate([w, z], axis=-1), jnp.concatenate([z, w], axis=-1)], axis=-2)


def _layout_w_in(w):
    gate = w[:, N_MAIN:]
    per = GQA_REP * 3
    blocks = [jnp.pad(gate[:, g * per:(g + 1) * per], ((0, 0), (0, LANES - per))) for g in range(N_KV_B)]
    return jnp.concatenate([w[:, :N_MAIN]] + blocks, axis=1).astype(bf16)


def kernel(x, p, attn_norm, w_in, w_out, w_cmp_k1, w_cmp_k2, w_cmp_v1, w_cmp_v2, cmp_pos, ffn_norm, w_route_group, b_route_group, w_route_expert, b_route_expert, w_expert_gate, w_expert_up, w_expert_down, ple_norm, w_ple_gate, w_ple_proj, final_norm):
    B, S, D = x.shape
    depth = w_in.shape[0]
    T = B * S
    slopes_a, slopes_b = _alibi_slopes()
    ovt, kaug, caug, gsel = _selection_constants(S)
    wg_all = w_expert_gate.astype(bf16)
    wu_all = w_expert_up.astype(bf16)
    wd_all = w_expert_down.astype(bf16)
    p3 = p.reshape(depth, T, PLE_DIM)
    row = lambda v: v.reshape(1, -1)
    n_route = N_GROUPS + N_EXPERTS

    x2d = x.reshape(T, D)
    for i in range(depth):
        proj3 = _in_proj(x2d, row(attn_norm[i]), _layout_w_in(w_in[i])).reshape(B, S, N_PROJ)
        oa = _dilated(proj3, slopes_a)
        w1 = lambda w: _block_diag2(w.reshape(CMP_BLOCK, HEAD_DIM, CMP_HIDDEN)).astype(bf16)
        pos_dup = jnp.concatenate([cmp_pos[i], cmp_pos[i]], axis=-1)
        kcmp, vcmp = _compress(proj3, pos_dup, w1(w_cmp_k1[i]), _block_diag2(w_cmp_k2[i]).astype(bf16),
                               w1(w_cmp_v1[i]), _block_diag2(w_cmp_v2[i]).astype(bf16))
        ob = _nsa(proj3, kcmp, vcmp, slopes_b, ovt, kaug, caug, gsel)
        wo = w_out[i].astype(bf16)
        x2d = _out_proj(x2d, oa.reshape(T, A_W), ob.reshape(T, -1), wo[:A_W], wo[A_W:])
        w_route = jnp.pad(jnp.concatenate([w_route_group[i], w_route_expert[i]], axis=1),
                          ((0, 0), (0, LANES - n_route)))
        b_route = jnp.pad(jnp.concatenate([b_route_group[i], b_route_expert[i]]), (0, LANES - n_route))
        stage, meta, counts = _router(x2d, row(ffn_norm[i]), w_route, row(b_route))
        pos, pad, tile_group, n_rows = _dispatch_plan(meta, counts, T)
        ys = _moe(_dispatch(stage, pos, pad, n_rows), tile_group, wg_all, wu_all, wd_all, i)
        x2d = _ple(x2d, _combine(ys, pos), p3, row(ple_norm[i]), w_ple_gate[i].astype(bf16),
                   w_ple_proj[i].astype(bf16), row(final_norm), i, i == depth - 1)
    return x2d.reshape(B, S, D)
```

```python
import functools

import numpy as np
import jax
import jax.numpy as jnp
from jax import lax
from jax.experimental import pallas as pl
from jax.experimental.pallas import tpu as pltpu

D_MODEL = 1024
PLE_DIM = 256
HEAD_DIM = 64
N_HEADS_A = 8
N_HEADS_B = 8
N_KV_B = 2
GQA_REP = N_HEADS_B // N_KV_B
N_HEADS_TOTAL = N_HEADS_A + N_HEADS_B
DILATED_PATTERNS = ((128, 1), (512, 4), (2048, 16))
CMP_BLOCK = 32
CMP_STRIDE = 16
CMP_HIDDEN = 256
SEL_BLOCK = 64
SEL_TOP = 16
N_LOCAL_BLOCKS = 2
WINDOW = 512
N_GROUPS = 4
EXPERTS_PER_GROUP = 4
N_EXPERTS = N_GROUPS * EXPERTS_PER_GROUP
D_EXPERT = 512
RMS_EPS = 1e-6
NEG = -1e30
FORCE_BONUS = 1e4
SCALE = HEAD_DIM ** -0.5

LANES = 128
QB = 128
SEL_CHUNK = 512
NSA_UNITS = 1
POS_LANE0 = HEAD_DIM
POS_SPLIT = 16
SEL_LANE0 = POS_LANE0 + 4
SEL_LANES = 32
MASK_BIG = -(2.0 ** 100)
MOE_TILE = 512
TOKEN_ROWS = D_MODEL // LANES
RANK_SPLIT = 128
DMA_BATCH = 256
A_W = N_HEADS_A * HEAD_DIM
N_MAIN = 3 * A_W + N_HEADS_B * HEAD_DIM + 6 * N_KV_B * HEAD_DIM
N_PROJ = N_MAIN + N_KV_B * LANES
COL_QB = (3 * A_W) // (2 * LANES)
COL_KC = (3 * A_W + N_HEADS_B * HEAD_DIM) // LANES
COL_GATE = N_MAIN // LANES
VMEM_LIMIT = 56 * 1024 * 1024

f32 = jnp.float32
bf16 = jnp.bfloat16


def _dot(a, b):
    return jnp.dot(a, b, preferred_element_type=f32)


def _dot_nt(a, b):
    return lax.dot_general(a, b, (((1,), (1,)), ((), ())), preferred_element_type=f32)


def _rms(x, g):
    return x * lax.rsqrt(jnp.mean(x * x, axis=-1, keepdims=True) + RMS_EPS) * g


def _params(*sem):
    return pltpu.CompilerParams(dimension_semantics=sem, vmem_limit_bytes=VMEM_LIMIT)


def _in_proj_kernel(x_ref, g_ref, w_ref, o_ref):
    h = _rms(x_ref[...], g_ref[...]).astype(bf16)
    for n0 in range(0, N_PROJ, 512):
        o_ref[:, n0:n0 + 512] = _dot(h, w_ref[:, n0:n0 + 512])


def _in_proj(x2d, g, w):
    T = x2d.shape[0]
    tm = 512
    return pl.pallas_call(
        _in_proj_kernel,
        grid=(T // tm,),
        in_specs=[pl.BlockSpec((tm, D_MODEL), lambda i: (i, 0)),
                  pl.BlockSpec((1, D_MODEL), lambda i: (0, 0)),
                  pl.BlockSpec((D_MODEL, N_PROJ), lambda i: (0, 0))],
        out_specs=pl.BlockSpec((tm, N_PROJ), lambda i: (i, 0)),
        out_shape=jax.ShapeDtypeStruct((T, N_PROJ), f32),
        compiler_params=_params("parallel"),
        name="in_proj",
    )(x2d, g, w)


def _dil_kernel(slope_ref, q_ref, k_ref, v_ref, o_ref, num_ref, m_ref, l_ref, *, seq):
    hp = pl.program_id(1)
    lane = lax.broadcasted_iota(jnp.int32, (QB, LANES), 1)
    left = lane < HEAD_DIM
    slopes = (slope_ref[2 * hp], slope_ref[2 * hp + 1])

    def rel_of(nk):
        i = lax.broadcasted_iota(jnp.int32, (QB, nk), 0)
        j = lax.broadcasted_iota(jnp.int32, (QB, nk), 1)
        return i - j + (nk - QB)

    def block(p, dil, row0, key0, nk):
        qc = q_ref[pl.ds(row0, QB, stride=dil), :] * SCALE
        kc = k_ref[pl.ds(key0, nk, stride=dil), :].astype(bf16)
        vc = v_ref[pl.ds(key0, nk, stride=dil), :].astype(bf16)
        rel = rel_of(nk)
        valid = (rel >= 0) & (rel <= QB)
        relf = rel.astype(f32)
        q2 = jnp.concatenate([jnp.where(left, qc, 0.0), jnp.where(left, 0.0, qc)], axis=0).astype(bf16)
        s2 = _dot_nt(q2, kc)
        es, ms, ls = [], [], []
        for hh in range(2):
            s = jnp.where(valid, s2[hh * QB:(hh + 1) * QB] - (slopes[hh] * float(dil)) * relf, NEG)
            mx = jnp.max(s, axis=1, keepdims=True)
            e = jnp.exp(s - mx)
            ms.append(mx)
            ls.append(jnp.sum(e, axis=1, keepdims=True))
            es.append(e.astype(bf16))
        num2 = _dot(jnp.concatenate(es, axis=0), vc)
        rows = pl.ds(row0, QB, stride=dil)
        num_ref[p, rows, :] = jnp.where(left, num2[:QB], num2[QB:])
        m_ref[p, rows, :] = jnp.where(left, ms[0], ms[1])
        l_ref[p, rows, :] = jnp.where(left, ls[0], ls[1])

    for p, (window, dil) in enumerate(DILATED_PATTERNS):
        assert window // dil == QB
        nblk = seq // dil // QB

        def per_class(r, _, p=p, dil=dil, nblk=nblk):
            block(p, dil, r, r, QB)

            def per_blk(a, _):
                block(p, dil, r + dil * QB * a, r + dil * QB * (a - 1), 2 * QB)
                return 0

            if nblk > 1:
                lax.fori_loop(1, nblk, per_blk, 0, unroll=3)
            return 0

        if dil == 1:
            per_class(0, 0)
        else:
            lax.fori_loop(0, dil, per_class, 0, unroll=4 if nblk == 1 else 1)

    ch = 256

    def combine(c, _):
        rows = pl.ds(pl.multiple_of(c * ch, ch), ch)
        ms = [m_ref[p, rows, :] for p in range(3)]
        big = jnp.maximum(jnp.maximum(ms[0], ms[1]), ms[2])
        num = jnp.zeros((ch, LANES), f32)
        den = jnp.zeros((ch, LANES), f32)
        for p in range(3):
            w = jnp.exp(ms[p] - big)
            num = num + w * num_ref[p, rows, :]
            den = den + w * l_ref[p, rows, :]
        o_ref[rows, :] = num / den
        return 0

    lax.fori_loop(0, seq // ch, combine, 0)


def _dilated(proj3, slopes_a):
    B, S, _ = proj3.shape
    npair = N_HEADS_A // 2
    blk = lambda off: pl.BlockSpec((None, S, LANES), lambda b, hp, off=off: (b, 0, off + hp))
    return pl.pallas_call(
        functools.partial(_dil_kernel, seq=S),
        grid=(B, npair),
        in_specs=[pl.BlockSpec(memory_space=pltpu.SMEM), blk(0), blk(npair), blk(2 * npair)],
        out_specs=pl.BlockSpec((None, S, LANES), lambda b, hp: (b, 0, hp)),
        out_shape=jax.ShapeDtypeStruct((B, S, A_W), f32),
        scratch_shapes=[pltpu.VMEM((3, S, LANES), f32)] * 3,
        compiler_params=_params("parallel", "parallel"),
        name="dilated_attn",
    )(slopes_a, proj3, proj3, proj3)


def _cmp_kernel(kc_ref, vc_ref, pos_ref, w1k_ref, w2k_ref, w1v_ref, w2v_ref, ko_ref, vo_ref, *, ncb):
    half = CMP_BLOCK // 2
    for x_ref, pi, w1_ref, w2_ref, o_ref in ((kc_ref, 0, w1k_ref, w2k_ref, ko_ref),
                                             (vc_ref, 1, w1v_ref, w2v_ref, vo_ref)):
        lo = jnp.zeros((ncb, 2 * CMP_HIDDEN), f32)
        hi = jnp.zeros((ncb, 2 * CMP_HIDDEN), f32)
        for r in range(half):
            x = x_ref[pl.ds(r, ncb, stride=CMP_STRIDE), :]
            lo = lo + _dot((x + pos_ref[pi, r:r + 1, :]).astype(bf16), w1_ref[r])
            hi = hi + _dot((x + pos_ref[pi, r + half:r + half + 1, :]).astype(bf16), w1_ref[r + half])
        h1 = lo + pltpu.roll(hi, ncb - 1, axis=0)
        o_ref[...] = _dot(jax.nn.gelu(h1).astype(bf16), w2_ref[...])


def _compress(proj3, pos_dup, w1k, w2k, w1v, w2v):
    B, S, _ = proj3.shape
    ncb = S // CMP_STRIDE
    full = lambda a: pl.BlockSpec(a.shape, lambda b: (0,) * a.ndim)
    out = jax.ShapeDtypeStruct((B, ncb, LANES), f32)
    return pl.pallas_call(
        functools.partial(_cmp_kernel, ncb=ncb),
        grid=(B,),
        in_specs=[pl.BlockSpec((None, S, LANES), lambda b: (b, 0, COL_KC)),
                  pl.BlockSpec((None, S, LANES), lambda b: (b, 0, COL_KC + 1)),
                  full(pos_dup), full(w1k), full(w2k), full(w1v), full(w2v)],
        out_specs=[pl.BlockSpec((None, ncb, LANES), lambda b: (b, 0, 0))] * 2,
        out_shape=[out, out],
        compiler_params=_params("parallel"),
        name="nsa_compress",
    )(proj3, proj3, pos_dup, w1k, w2k, w1v, w2v)


def _nsa_kernel(slope_ref, q_ref, ks_ref, vs_ref, kw_ref, vw_ref, gl_ref, kcmp_ref, vcmp_ref,
                ovt_ref, kaug_ref, caug_ref, gsel_ref, o_ref,
                ksa, vsa, kwa, vwa, kca, vca, q_scr, sel_scr, *per_head, seq):
    g = pl.program_id(1)
    qi = pl.program_id(2)
    nsel = seq // SEL_BLOCK
    n_top = min(SEL_TOP, nsel)
    assert nsel <= SEL_LANES and N_LOCAL_BLOCKS * SEL_BLOCK >= QB and n_top > N_LOCAL_BLOCKS
    slab = 32
    ur = GQA_REP * QB // NSA_UNITS
    s_w, e_w, s_m, e_m, m_h, al_h, acc = [per_head[i * NSA_UNITS:(i + 1) * NSA_UNITS] for i in range(7)]

    def group_lanes(x):
        return jnp.where(g == 0, x, pltpu.roll(x, HEAD_DIM, axis=1))

    @pl.when(qi == 0)
    def _prep():
        ch = 256
        lane_c = lax.broadcasted_iota(jnp.int32, (ch, LANES), 1)
        is_k = lane_c < HEAD_DIM
        for c in range(seq // ch):
            sl = slice(c * ch, (c + 1) * ch)
            aug = kaug_ref[sl, :]
            ksa[sl, :] = jnp.where(is_k, group_lanes(ks_ref[sl, :]), aug).astype(bf16)
            kwa[sl, :] = jnp.where(is_k, group_lanes(kw_ref[sl, :]),
                                   jnp.where(lane_c < SEL_LANE0, aug, 0.0)).astype(bf16)
            for src, dst in ((vs_ref, vsa), (vw_ref, vwa)):
                v = group_lanes(src[sl, :])
                dst[sl, :LANES] = jnp.where(is_k, v, 1.0).astype(bf16)
                dst[sl, LANES:] = jnp.where(is_k, 1.0, pltpu.roll(v, HEAD_DIM, axis=1)).astype(bf16)
        lane_k = lax.broadcasted_iota(jnp.int32, kca.shape, 1)
        kca[...] = jnp.where(lane_k < HEAD_DIM, group_lanes(kcmp_ref[...]), caug_ref[...]).astype(bf16)
        vc = group_lanes(vcmp_ref[...])
        vca[...] = jnp.where(lane_k < HEAD_DIM, vc, pltpu.roll(vc, HEAD_DIM, axis=1)).astype(bf16)

    lane = lax.broadcasted_iota(jnp.int32, (QB, LANES), 1)
    left = lane < HEAD_DIM
    ii = lax.broadcasted_iota(jnp.int32, (QB, LANES), 0)
    t_row = qi * QB + ii
    t_hi = (t_row // POS_SPLIT).astype(f32)
    t_lo = (t_row % POS_SPLIT).astype(f32)
    slopes = [slope_ref[g * GQA_REP + r] for r in range(GQA_REP)]
    head = lambda a, r: a[r * QB:(r + 1) * QB]

    def q_head(r):
        x = q_ref[:, (r // 2) * LANES:(r // 2 + 1) * LANES] * SCALE
        if r % 2:
            x = pltpu.roll(x, HEAD_DIM, axis=1)
        m = slopes[r]
        pos = jnp.where(lane == POS_LANE0, (-POS_SPLIT * m) * t_hi,
                        jnp.where(lane == POS_LANE0 + 1, (-m) * t_lo,
                                  jnp.where(lane == POS_LANE0 + 2, POS_SPLIT * m,
                                            jnp.where(lane == POS_LANE0 + 3, m, 0.0))))
        return jnp.where(left, x, pos)

    q4 = jnp.concatenate([q_head(r) for r in range(GQA_REP)], axis=0)
    q4b = q4.astype(bf16)

    cmp_end = (lane * CMP_STRIDE + (CMP_BLOCK - 1))
    valid_c = t_row >= cmp_end
    s4 = _dot_nt(q4b, kca[...])
    ps = []
    p_sum = jnp.zeros((QB, LANES), f32)
    row_bcast = lambda col: jnp.broadcast_to(col, (QB, LANES))
    for r in range(GQA_REP):
        s = jnp.where(valid_c, head(s4, r), NEG)
        e = jnp.exp(s - row_bcast(jnp.max(s, axis=1, keepdims=True)))
        p = jnp.where(valid_c, e, 0.0) / row_bcast(jnp.sum(e, axis=1, keepdims=True))
        p_sum = p_sum + p
        ps.append(p.astype(bf16))
    o_cmp4 = _dot(jnp.concatenate(ps, axis=0), vca[...])

    back = t_row // SEL_BLOCK - lane
    valid_s = (back >= 0) & (lane < nsel)

    @pl.when(2 * qi + 2 <= n_top)
    def _all_valid():
        sel_scr[...] = jnp.where(valid_s, 1.0, 0.0)

    @pl.when(2 * qi + 2 > n_top)
    def _top_k():
        p_hi = p_sum.astype(bf16)
        p_lo = (p_sum - p_hi.astype(f32)).astype(bf16)
        imp_t = (_dot_nt(ovt_ref[...], p_hi) + _dot_nt(ovt_ref[...], p_lo))[:SEL_LANES]
        blk = lax.broadcasted_iota(jnp.int32, (SEL_LANES, QB), 0)
        tq = qi * QB + lax.broadcasted_iota(jnp.int32, (SEL_LANES, QB), 1)
        back_t = tq // SEL_BLOCK - blk
        valid_t = (back_t >= 0) & (blk < nsel)
        forced = (blk == 0) | (valid_t & (back_t < N_LOCAL_BLOCKS))
        score = jnp.where(valid_t, imp_t + jnp.where(forced, FORCE_BONUS, 0.0), NEG)
        score = jnp.where(blk < nsel, score, 2.0 * NEG)
        rank = jnp.zeros((SEL_LANES, QB), jnp.int32)
        for n in range(nsel):
            row = score[n:n + 1, :]
            ahead = (row > score) | ((row == score) & (blk > n))
            rank = rank + ahead.astype(jnp.int32)
        sel_t = jnp.where((rank < n_top) & valid_t, 1.0, 0.0)
        sel_t = jnp.concatenate([sel_t, jnp.zeros((LANES - SEL_LANES, QB), f32)], axis=0)
        sel_scr[...] = sel_t.T

    sel_bias = jnp.where((sel_scr[...] > 0.5) & (lane < 2 * qi), 0.0, MASK_BIG)
    sel_bias = pltpu.roll(sel_bias, SEL_LANE0, axis=1)
    in_sel = (lane >= SEL_LANE0) & (lane < SEL_LANE0 + SEL_LANES)
    for r in range(GQA_REP):
        q_scr[r * QB:(r + 1) * QB, :] = jnp.where(in_sel, sel_bias, head(q4, r)).astype(bf16)

    sig = jax.nn.sigmoid(gl_ref[...])
    g_hi = sig.astype(bf16)
    g_mid = (sig - g_hi.astype(f32)).astype(bf16)
    g_lo = (sig - g_hi.astype(f32) - g_mid.astype(f32)).astype(bf16)
    gate_b = _dot(jnp.concatenate([g_hi, g_mid, g_lo], axis=1), gsel_ref[...])

    def rel_tile(nk, offset):
        i = lax.broadcasted_iota(jnp.int32, (QB, nk), 0)
        j = lax.broadcasted_iota(jnp.int32, (QB, nk), 1)
        return (i - j + offset).astype(f32)

    def softmax_rows(s_ref, e_ref, m_ref, al_ref, nk, bias, running):
        for sl in range(ur // slab):
            rows = slice(sl * slab, (sl + 1) * slab)
            cols = [slice(j * LANES, (j + 1) * LANES) for j in range(nk // LANES)]
            i0 = (sl * slab) % QB
            tiles = [s_ref[rows, c] if bias is None else s_ref[rows, c] + bias[i0:i0 + slab, c] for c in cols]
            mx = tiles[0]
            for t in tiles[1:]:
                mx = jnp.maximum(mx, t)
            m_new = jnp.broadcast_to(jnp.max(mx, axis=1, keepdims=True), (slab, LANES))
            if running:
                m_old = m_ref[rows, :]
                m_new = jnp.maximum(m_old, m_new)
                al_ref[rows, :] = jnp.exp(m_old - m_new)
            if m_ref is not None:
                m_ref[rows, :] = m_new
            for c, t in zip(cols, tiles):
                e_ref[rows, c] = jnp.exp(t - m_new).astype(bf16)

    unit = lambda a, u: a[u * ur:(u + 1) * ur]

    def own_half(pv, u):
        heads = range(u * ur // QB, (u + 1) * ur // QB)
        return jnp.concatenate([pv[(r - heads[0]) * QB:(r - heads[0] + 1) * QB, (r % 2) * LANES:(r % 2 + 1) * LANES]
                                for r in heads], axis=0)

    def staggered(scores, probs, values):
        scores(0)
        for u in range(NSA_UNITS):
            if u + 1 < NSA_UNITS:
                scores(u + 1)
            probs(u)
            values(u)

    nwin = WINDOW + QB
    w0 = pl.multiple_of(jnp.maximum(qi - WINDOW // QB, 0) * QB, QB)
    dist_w = rel_tile(nwin, qi * QB - w0)
    bias_w = jnp.where((dist_w >= 0) & (dist_w < WINDOW), 0.0, NEG)
    o_w = [None] * NSA_UNITS

    def win_scores(u):
        s_w[u][...] = _dot_nt(unit(q4b, u), kwa[pl.ds(w0, nwin), :])

    def win_values(u):
        o_w[u] = own_half(_dot(e_w[u][...], vwa[pl.ds(w0, nwin), :]), u)

    staggered(win_scores, lambda u: softmax_rows(s_w[u], e_w[u], None, None, nwin, bias_w, False), win_values)

    d0 = pl.multiple_of(qi * QB, QB)
    bias_d = jnp.where(rel_tile(QB, 0) >= 0, 0.0, NEG)

    def diag_scores(u):
        s_m[u][:, :QB] = _dot_nt(unit(q4b, u), ksa[pl.ds(d0, QB), :])

    def diag_values(u):
        acc[u][...] = own_half(_dot(e_m[u][:, :QB], vsa[pl.ds(d0, QB), :]), u)

    staggered(diag_scores, lambda u: softmax_rows(s_m[u], e_m[u], m_h[u], None, QB, bias_d, False), diag_values)

    def sel_body(kc, _):
        rows = pl.ds(pl.multiple_of(kc * SEL_CHUNK, SEL_CHUNK), SEL_CHUNK)

        def scores(u):
            s_m[u][...] = _dot_nt(q_scr[u * ur:(u + 1) * ur, :], ksa[rows, :])

        def values(u):
            acc[u][...] = al_h[u][...] * acc[u][...] + own_half(_dot(e_m[u][...], vsa[rows, :]), u)

        staggered(scores, lambda u: softmax_rows(s_m[u], e_m[u], m_h[u], al_h[u], SEL_CHUNK, None, True), values)
        return 0

    per = SEL_CHUNK // QB
    lax.fori_loop(0, (qi + per - 1) // per, sel_body, 0)

    acc_s = jnp.concatenate([a[...] for a in acc], axis=0)
    acc_w = jnp.concatenate(o_w, axis=0)
    for pr in range(GQA_REP // 2):
        ev, od = 2 * pr, 2 * pr + 1
        out = gate_b[:, (3 * pr) * LANES:(3 * pr + 1) * LANES] * jnp.where(left, head(o_cmp4, ev), head(o_cmp4, od))
        for j, a in ((1, acc_s), (2, acc_w)):
            num = jnp.where(left, head(a, ev), head(a, od))
            den = pltpu.roll(jnp.where(left, head(a, od), head(a, ev)), HEAD_DIM, axis=1)
            out = out + gate_b[:, (3 * pr + j) * LANES:(3 * pr + j + 1) * LANES] / den * num
        o_ref[:, pr * LANES:(pr + 1) * LANES] = out


def _nsa(proj3, kcmp, vcmp, slopes_b, ovt, kaug, caug, gsel):
    B, S, _ = proj3.shape
    ncb = kcmp.shape[1]
    nrow = GQA_REP * QB
    ur = nrow // NSA_UNITS
    nwin = WINDOW + QB
    kv = lambda j: pl.BlockSpec((None, S, LANES), lambda b, g, qi, j=j: (b, 0, COL_KC + j))
    full = lambda a: pl.BlockSpec(a.shape, lambda b, g, qi: (0,) * a.ndim)
    return pl.pallas_call(
        functools.partial(_nsa_kernel, seq=S),
        grid=(B, N_KV_B, S // QB),
        in_specs=[pl.BlockSpec(memory_space=pltpu.SMEM),
                  pl.BlockSpec((None, QB, 2 * LANES), lambda b, g, qi: (b, qi, COL_QB + g)),
                  kv(2), kv(3), kv(4), kv(5),
                  pl.BlockSpec((None, QB, LANES), lambda b, g, qi: (b, qi, COL_GATE + g)),
                  pl.BlockSpec((None, ncb, LANES), lambda b, g, qi: (b, 0, 0)),
                  pl.BlockSpec((None, ncb, LANES), lambda b, g, qi: (b, 0, 0)),
                  full(ovt), full(kaug), full(caug), full(gsel)],
        out_specs=pl.BlockSpec((None, QB, 2 * LANES), lambda b, g, qi: (b, qi, g)),
        out_shape=jax.ShapeDtypeStruct((B, S, N_HEADS_B * HEAD_DIM), f32),
        scratch_shapes=([pltpu.VMEM((S, LANES), bf16), pltpu.VMEM((S, 2 * LANES), bf16)] * 2
                        + [pltpu.VMEM((ncb, LANES), bf16)] * 2
                        + [pltpu.VMEM((nrow, LANES), bf16), pltpu.VMEM((QB, LANES), f32)]
                        + [pltpu.VMEM((ur, nwin), f32)] * NSA_UNITS + [pltpu.VMEM((ur, nwin), bf16)] * NSA_UNITS
                        + [pltpu.VMEM((ur, SEL_CHUNK), f32)] * NSA_UNITS
                        + [pltpu.VMEM((ur, SEL_CHUNK), bf16)] * NSA_UNITS
                        + [pltpu.VMEM((ur, LANES), f32)] * (2 * NSA_UNITS)
                        + [pltpu.VMEM((ur, LANES), f32)] * NSA_UNITS),
        compiler_params=_params("parallel", "parallel", "arbitrary"),
        name="nsa_attn",
    )(slopes_b, proj3, proj3, proj3, proj3, proj3, proj3, kcmp, vcmp, ovt, kaug, caug, gsel)


def _out_proj_kernel(x_ref, oa_ref, ob_ref, wa_ref, wb_ref, o_ref):
    o_ref[...] = (x_ref[...] + _dot(oa_ref[...].astype(bf16), wa_ref[...])
                  + _dot(ob_ref[...].astype(bf16), wb_ref[...]))


def _out_proj(x2d, oa, ob, wa, wb):
    T = x2d.shape[0]
    tm = 512
    row = lambda w: pl.BlockSpec((tm, w), lambda i: (i, 0))
    full = lambda a: pl.BlockSpec(a.shape, lambda i: (0,) * a.ndim)
    return pl.pallas_call(
        _out_proj_kernel,
        grid=(T // tm,),
        in_specs=[row(D_MODEL), row(oa.shape[1]), row(ob.shape[1]), full(wa), full(wb)],
        out_specs=row(D_MODEL),
        out_shape=jax.ShapeDtypeStruct((T, D_MODEL), f32),
        compiler_params=_params("parallel"),
        name="out_proj",
    )(x2d, oa, ob, wa, wb)


def _route(h, w_ref, b_ref):
    logit = jnp.dot(h, w_ref[...], preferred_element_type=f32, precision=lax.Precision.HIGHEST) + b_ref[...]
    tm = logit.shape[0]
    lane = lax.broadcasted_iota(jnp.int32, (tm, LANES), 1)
    big = jnp.int32(LANES)
    is_g = lane < N_GROUPS
    gl = jnp.where(is_g, logit, NEG)
    gmax = jnp.max(gl, axis=1, keepdims=True)
    gsum = jnp.sum(jnp.where(is_g, jnp.exp(gl - gmax), 0.0), axis=1, keepdims=True)
    gsel = jnp.min(jnp.where(is_g & (gl == gmax), lane, big), axis=1, keepdims=True)
    gw = 1.0 / gsum
    e_lane = lane - N_GROUPS
    in_grp = (e_lane >= 0) & (e_lane < N_EXPERTS) & (e_lane // EXPERTS_PER_GROUP == gsel)
    el = jnp.where(in_grp, logit, NEG)
    t1 = jnp.max(el, axis=1, keepdims=True)
    i1 = jnp.min(jnp.where(in_grp & (el == t1), lane, big), axis=1, keepdims=True)
    el2 = jnp.where(lane == i1, NEG, el)
    t2 = jnp.max(el2, axis=1, keepdims=True)
    i2 = jnp.min(jnp.where(in_grp & (lane != i1) & (el2 == t2), lane, big), axis=1, keepdims=True)
    e2 = jnp.exp(t2 - t1)
    w1 = gw / (1.0 + e2)
    w2 = gw * e2 / (1.0 + e2)
    return jnp.where(lane == i1, w1, jnp.where(lane == i2, w2, 0.0)), gsel


def _to_token_tiles(ref, x):
    for s in range(D_MODEL // LANES):
        ref[pl.ds(s, x.shape[0], stride=D_MODEL // LANES), :] = x[:, s * LANES:(s + 1) * LANES]


def _from_token_tiles(ref):
    n = D_MODEL // LANES
    return jnp.concatenate([ref[pl.ds(s, ref.shape[0] // n, stride=n), :] for s in range(n)], axis=1)


def _router_kernel(x_ref, g_ref, w_ref, b_ref, tri_ref, stage_ref, meta_ref, cnt_ref, cnt_scr):
    @pl.when(pl.program_id(0) == 0)
    def _zero():
        cnt_scr[...] = jnp.zeros_like(cnt_scr)

    h = _rms(x_ref[...], g_ref[...])
    _to_token_tiles(stage_ref, h)
    _, gsel = _route(h, w_ref, b_ref)
    tm = h.shape[0]
    lane = lax.broadcasted_iota(jnp.int32, (tm, LANES), 1)
    is_g = lane < N_GROUPS

    onehot = jnp.where(is_g & (lane == gsel), 1.0, 0.0)
    before = _dot(tri_ref[...], onehot.astype(bf16)) + cnt_scr[...]
    rank = jnp.sum(onehot * before, axis=1, keepdims=True)
    cnt_scr[...] = before[tm - 1:tm, :] + onehot[tm - 1:tm, :]
    cnt_ref[...] = cnt_scr[...]
    rank_hi = jnp.floor(rank * (1.0 / RANK_SPLIT))
    cols = jnp.where(lane == 0, gsel.astype(f32), jnp.where(lane == 1, rank_hi,
                                                            jnp.where(lane == 2, rank - RANK_SPLIT * rank_hi, 0.0)))
    pick = (lax.broadcasted_iota(jnp.int32, (8, LANES), 0) == lax.broadcasted_iota(jnp.int32, (8, LANES), 1))
    meta_ref[...] = _dot_nt(jnp.where(pick, 1.0, 0.0).astype(bf16), cols.astype(bf16))


def _router(x2d, g, w, b):
    T = x2d.shape[0]
    tm = MOE_TILE
    assert T // RANK_SPLIT <= 256
    tri = jnp.asarray(np.tril(np.ones((tm, tm), np.float32), -1), bf16)
    return pl.pallas_call(
        _router_kernel,
        grid=(T // tm,),
        in_specs=[pl.BlockSpec((tm, D_MODEL), lambda i: (i, 0)),
                  pl.BlockSpec((1, D_MODEL), lambda i: (0, 0)),
                  pl.BlockSpec((D_MODEL, LANES), lambda i: (0, 0)),
                  pl.BlockSpec((1, LANES), lambda i: (0, 0)),
                  pl.BlockSpec((tm, tm), lambda i: (0, 0))],
        out_specs=[pl.BlockSpec((tm * TOKEN_ROWS, LANES), lambda i: (i, 0)),
                   pl.BlockSpec((None, 8, tm), lambda i: (i, 0, 0)),
                   pl.BlockSpec((1, LANES), lambda i: (0, 0))],
        out_shape=[jax.ShapeDtypeStruct((T * TOKEN_ROWS, LANES), f32),
                   jax.ShapeDtypeStruct((T // tm, 8, tm), f32),
                   jax.ShapeDtypeStruct((1, LANES), f32)],
        scratch_shapes=[pltpu.VMEM((1, LANES), f32)],
        compiler_params=_params("arbitrary"),
        name="router",
    )(x2d, g, w, b, tri)


def _token_rows(t, n=1):
    return pl.ds(pl.multiple_of(t * TOKEN_ROWS, TOKEN_ROWS), n * TOKEN_ROWS)


def _row_copies(n, src_tok, dst_tok, src_ref, dst_ref, sems):
    nb = n // DMA_BATCH

    def wait(b):
        pltpu.make_async_copy(src_ref.at[_token_rows(0, DMA_BATCH)], dst_ref.at[_token_rows(0, DMA_BATCH)],
                              sems.at[b % 2]).wait()

    def batch(b, _):
        def one(k, _):
            t = b * DMA_BATCH + k
            pltpu.make_async_copy(src_ref.at[_token_rows(src_tok(t))], dst_ref.at[_token_rows(dst_tok(t))],
                                  sems.at[b % 2]).start()
            return 0

        lax.fori_loop(0, DMA_BATCH, one, 0, unroll=8)

        @pl.when(b > 0)
        def _():
            wait(b - 1)
        return 0

    lax.fori_loop(0, nb, batch, 0)
    wait(nb - 1)


def _dispatch_kernel(pos_ref, pad_ref, stage_ref, zero_ref, xs_ref, sems):
    n = stage_ref.shape[0] // TOKEN_ROWS
    _row_copies(n, lambda t: t, lambda t: pos_ref[t], stage_ref, xs_ref, sems)
    npad = xs_ref.shape[0] // TOKEN_ROWS - n
    for r in range(N_GROUPS + 1):
        def fill(slot, _):
            pltpu.make_async_copy(zero_ref, xs_ref.at[_token_rows(slot)], sems.at[2]).start()
            return 0

        lax.fori_loop(pad_ref[2 * r], pad_ref[2 * r + 1], fill, 0)
    pltpu.make_async_copy(stage_ref.at[_token_rows(0, npad)], xs_ref.at[_token_rows(0, npad)], sems.at[2]).wait()


def _dispatch(stage, pos, pad, n_slots):
    any_spec = pl.BlockSpec(memory_space=pl.ANY)
    zero = jnp.zeros((TOKEN_ROWS, LANES), f32)
    return pl.pallas_call(
        _dispatch_kernel,
        grid_spec=pltpu.PrefetchScalarGridSpec(
            num_scalar_prefetch=2, grid=(1,), in_specs=[any_spec, any_spec], out_specs=any_spec,
            scratch_shapes=[pltpu.SemaphoreType.DMA((3,))]),
        out_shape=jax.ShapeDtypeStruct((n_slots * TOKEN_ROWS, LANES), f32),
        compiler_params=pltpu.CompilerParams(dimension_semantics=("arbitrary",), has_side_effects=True),
        name="moe_dispatch",
    )(pos, pad, stage, zero)


def _combine_kernel(pos_ref, ys_ref, y_ref, sems):
    _row_copies(y_ref.shape[0] // TOKEN_ROWS, lambda t: pos_ref[t], lambda t: t, ys_ref, y_ref, sems)


def _combine(ys, pos):
    any_spec = pl.BlockSpec(memory_space=pl.ANY)
    return pl.pallas_call(
        _combine_kernel,
        grid_spec=pltpu.PrefetchScalarGridSpec(
            num_scalar_prefetch=1, grid=(1,), in_specs=[any_spec], out_specs=any_spec,
            scratch_shapes=[pltpu.SemaphoreType.DMA((2,))]),
        out_shape=jax.ShapeDtypeStruct((pos.shape[0] * TOKEN_ROWS, LANES), f32),
        compiler_params=pltpu.CompilerParams(dimension_semantics=("arbitrary",), has_side_effects=True),
        name="moe_combine",
    )(pos, ys)


def _moe_kernel(tg_ref, xs_ref, wr_ref, br_ref, wg_ref, wu_ref, wd_ref, o_ref):
    grp = tg_ref[pl.program_id(0)]
    h = _from_token_tiles(xs_ref)
    gates, _ = _route(h, wr_ref, br_ref)
    x = h.astype(bf16)
    lane = lax.broadcasted_iota(jnp.int32, gates.shape, 1)
    y = jnp.zeros(h.shape, f32)
    for e in range(EXPERTS_PER_GROUP):
        gcol = jnp.sum(jnp.where(lane == N_GROUPS + EXPERTS_PER_GROUP * grp + e, gates, 0.0), axis=1, keepdims=True)
        a = jax.nn.silu(_dot(x, wg_ref[e])) * _dot(x, wu_ref[e])
        y = y + _dot((a * gcol).astype(bf16), wd_ref[e])
    _to_token_tiles(o_ref, y)


def _moe(xs, tile_group, w_route, b_route, wg, wu, wd, layer):
    tm = MOE_TILE
    n_slots = xs.shape[0] // TOKEN_ROWS
    w_spec = lambda k, n: pl.BlockSpec((None, None, EXPERTS_PER_GROUP, k, n), lambda j, tg: (layer, tg[j], 0, 0, 0))
    grouped = lambda w: w.reshape(w.shape[0], N_GROUPS, EXPERTS_PER_GROUP, *w.shape[2:])
    tiles = pl.BlockSpec((tm * TOKEN_ROWS, LANES), lambda j, tg: (j, 0))
    return pl.pallas_call(
        _moe_kernel,
        grid_spec=pltpu.PrefetchScalarGridSpec(
            num_scalar_prefetch=1, grid=(n_slots // tm,),
            in_specs=[tiles,
                      pl.BlockSpec((D_MODEL, LANES), lambda j, tg: (0, 0)),
                      pl.BlockSpec((1, LANES), lambda j, tg: (0, 0)),
                      w_spec(D_MODEL, D_EXPERT), w_spec(D_MODEL, D_EXPERT), w_spec(D_EXPERT, D_MODEL)],
            out_specs=tiles),
        out_shape=jax.ShapeDtypeStruct(xs.shape, f32),
        compiler_params=_params("arbitrary"),
        name="moe_ffn",
    )(tile_group, xs, w_route, b_route, grouped(wg), grouped(wu), grouped(wd))


def _ple_kernel(x_ref, y_ref, p_ref, g_ref, wg_ref, wp_ref, fg_ref, o_ref, *, final):
    x = x_ref[...] + _from_token_tiles(y_ref)
    gate = jax.nn.sigmoid(_dot(_rms(x, g_ref[...]).astype(bf16), wg_ref[...]))
    y = x + gate * _dot(p_ref[...].astype(bf16), wp_ref[...])
    o_ref[...] = _rms(y, fg_ref[...]) if final else y


def _ple(x2d, y2d, p3, g, wg, wp, fg, layer, final):
    T = x2d.shape[0]
    tm = 512
    full = lambda a: pl.BlockSpec(a.shape, lambda i: (0,) * a.ndim)
    return pl.pallas_call(
        functools.partial(_ple_kernel, final=final),
        grid=(T // tm,),
        in_specs=[pl.BlockSpec((tm, D_MODEL), lambda i: (i, 0)),
                  pl.BlockSpec((tm * TOKEN_ROWS, LANES), lambda i: (i, 0)),
                  pl.BlockSpec((None, tm, PLE_DIM), lambda i: (layer, i, 0)),
                  full(g), full(wg), full(wp), full(fg)],
        out_specs=pl.BlockSpec((tm, D_MODEL), lambda i: (i, 0)),
        out_shape=jax.ShapeDtypeStruct((T, D_MODEL), f32),
        compiler_params=_params("parallel"),
        name="ple",
    )(x2d, y2d, p3, g, wg, wp, fg)


def _alibi_slopes():
    s = 2.0 ** (-8.0 * np.arange(1, N_HEADS_TOTAL + 1) / N_HEADS_TOTAL)
    assert np.all(np.log2(s[1::2]) == np.round(np.log2(s[1::2])))
    return jnp.asarray(s[0::2], f32), jnp.asarray(s[1::2], f32)


def _selection_constants(seq):
    ncb = seq // CMP_STRIDE
    nsel = seq // SEL_BLOCK
    n_cmp = (seq - CMP_BLOCK) // CMP_STRIDE + 1
    cs = np.arange(n_cmp) * CMP_STRIDE
    bs = np.arange(nsel) * SEL_BLOCK
    ov = np.clip(np.minimum(cs[:, None] + CMP_BLOCK, bs[None, :] + SEL_BLOCK)
                 - np.maximum(cs[:, None], bs[None, :]), 0, None) / CMP_BLOCK
    assert ncb == LANES and seq <= POS_SPLIT * 256
    ovt = np.zeros((LANES, ncb), np.float32)
    ovt[:nsel, :n_cmp] = ov.T
    pos = np.arange(seq)
    kaug = np.zeros((seq, LANES), np.float32)
    kaug[:, POS_LANE0:POS_LANE0 + 2] = 1.0
    kaug[:, POS_LANE0 + 2] = pos // POS_SPLIT
    kaug[:, POS_LANE0 + 3] = pos % POS_SPLIT
    kaug[pos, SEL_LANE0 + pos // SEL_BLOCK] = 1.0
    cend = np.arange(ncb) * CMP_STRIDE + CMP_BLOCK - 1
    caug = np.zeros((ncb, LANES), np.float32)
    caug[:, POS_LANE0:POS_LANE0 + 2] = 1.0
    caug[:, POS_LANE0 + 2] = cend // POS_SPLIT
    caug[:, POS_LANE0 + 3] = cend % POS_SPLIT
    gsel = np.zeros((3, LANES, GQA_REP // 2, 3, 2, HEAD_DIM), np.float32)
    for pr in range(GQA_REP // 2):
        for j in range(3):
            for hh in range(2):
                gsel[:, 3 * (2 * pr + hh) + j, pr, j, hh, :] = 1.0
    gsel = gsel.reshape(3 * LANES, (GQA_REP // 2) * 3 * LANES)
    return jnp.asarray(ovt, bf16), jnp.asarray(kaug), jnp.asarray(caug), jnp.asarray(gsel, bf16)


def _dispatch_plan(meta, counts, n_tokens):
    i32 = jnp.int32
    n_rows = n_tokens + N_GROUPS * MOE_TILE
    group = meta[:, 0, :].reshape(n_tokens).astype(i32)
    rank = (meta[:, 1, :] * RANK_SPLIT + meta[:, 2, :]).reshape(n_tokens).astype(i32)
    cnt = counts[0, :N_GROUPS].astype(i32)
    padded = (cnt + MOE_TILE - 1) // MOE_TILE * MOE_TILE
    end = jnp.cumsum(padded)
    off = end - padded
    pos = off[group] + rank
    pad = jnp.stack([jnp.concatenate([off + cnt, end[-1:]]),
                     jnp.concatenate([end, jnp.full((1,), n_rows, i32)])], axis=1).reshape(-1)
    tile_start = jnp.arange(n_rows // MOE_TILE, dtype=i32) * MOE_TILE
    tile_group = jnp.minimum(jnp.sum(tile_start[:, None] >= end[None, :], axis=1), N_GROUPS - 1).astype(i32)
    return pos, pad.astype(i32), tile_group, n_rows


def _block_diag2(w):
    z = jnp.zeros_like(w)
    return jnp.concatenate([jnp.concatenate([w, z], axis=-1), jnp.concatenate([z, w], axis=-1)], axis=-2)


def _layout_w_in(w):
    gate = w[:, N_MAIN:]
    per = GQA_REP * 3
    blocks = [jnp.pad(gate[:, g * per:(g + 1) * per], ((0, 0), (0, LANES - per))) for g in range(N_KV_B)]
    return jnp.concatenate([w[:, :N_MAIN]] + blocks, axis=1).astype(bf16)


def kernel(x, p, attn_norm, w_in, w_out, w_cmp_k1, w_cmp_k2, w_cmp_v1, w_cmp_v2, cmp_pos, ffn_norm, w_route_group, b_route_group, w_route_expert, b_route_expert, w_expert_gate, w_expert_up, w_expert_down, ple_norm, w_ple_gate, w_ple_proj, final_norm):
    B, S, D = x.shape
    depth = w_in.shape[0]
    T = B * S
    slopes_a, slopes_b = _alibi_slopes()
    ovt, kaug, caug, gsel = _selection_constants(S)
    wg_all = w_expert_gate.astype(bf16)
    wu_all = w_expert_up.astype(bf16)
    wd_all = w_expert_down.astype(bf16)
    p3 = p.reshape(depth, T, PLE_DIM)
    row = lambda v: v.reshape(1, -1)
    n_route = N_GROUPS + N_EXPERTS

    x2d = x.reshape(T, D)
    for i in range(depth):
        proj3 = _in_proj(x2d, row(attn_norm[i]), _layout_w_in(w_in[i])).reshape(B, S, N_PROJ)
        oa = _dilated(proj3, slopes_a)
        w1 = lambda w: _block_diag2(w.reshape(CMP_BLOCK, HEAD_DIM, CMP_HIDDEN)).astype(bf16)
        pos_dup = jnp.concatenate([cmp_pos[i], cmp_pos[i]], axis=-1)
        kcmp, vcmp = _compress(proj3, pos_dup, w1(w_cmp_k1[i]), _block_diag2(w_cmp_k2[i]).astype(bf16),
                               w1(w_cmp_v1[i]), _block_diag2(w_cmp_v2[i]).astype(bf16))
        ob = _nsa(proj3, kcmp, vcmp, slopes_b, ovt, kaug, caug, gsel)
        wo = w_out[i].astype(bf16)
        x2d = _out_proj(x2d, oa.reshape(T, A_W), ob.reshape(T, -1), wo[:A_W], wo[A_W:])
        w_route = jnp.pad(jnp.concatenate([w_route_group[i], w_route_expert[i]], axis=1),
                          ((0, 0), (0, LANES - n_route)))
        b_route = jnp.pad(jnp.concatenate([b_route_group[i], b_route_expert[i]]), (0, LANES - n_route))
        stage, meta, counts = _router(x2d, row(ffn_norm[i]), w_route, row(b_route))
        pos, pad, tile_group, n_slots = _dispatch_plan(meta, counts, T)
        ys = _moe(_dispatch(stage, pos, pad, n_slots), tile_group, w_route, row(b_route), wg_all, wu_all, wd_all, i)
        x2d = _ple(x2d, _combine(ys, pos), p3, row(ple_norm[i]), w_ple_gate[i].astype(bf16),
                   w_ple_proj[i].astype(bf16), row(final_norm), i, i == depth - 1)
    return x2d.reshape(B, S, D)
```

```python
import functools

import numpy as np
import jax
import jax.numpy as jnp
from jax import lax
from jax.experimental import pallas as pl
from jax.experimental.pallas import tpu as pltpu

D_MODEL = 1024
PLE_DIM = 256
HEAD_DIM = 64
N_HEADS_A = 8
N_HEADS_B = 8
N_KV_B = 2
GQA_REP = N_HEADS_B // N_KV_B
N_HEADS_TOTAL = N_HEADS_A + N_HEADS_B
DILATED_PATTERNS = ((128, 1), (512, 4), (2048, 16))
CMP_BLOCK = 32
CMP_STRIDE = 16
CMP_HIDDEN = 256
SEL_BLOCK = 64
SEL_TOP = 16
N_LOCAL_BLOCKS = 2
WINDOW = 512
N_GROUPS = 4
EXPERTS_PER_GROUP = 4
N_EXPERTS = N_GROUPS * EXPERTS_PER_GROUP
D_EXPERT = 512
RMS_EPS = 1e-6
NEG = -1e30
FORCE_BONUS = 1e4
SCALE = HEAD_DIM ** -0.5

LANES = 128
QB = 128
SEL_CHUNK = 512
NSA_UNITS = 1
POS_LANE0 = HEAD_DIM
POS_SPLIT = 16
SEL_LANE0 = POS_LANE0 + 4
SEL_LANES = 32
MASK_BIG = -(2.0 ** 100)
MOE_TILE = 512
TOKEN_ROWS = D_MODEL // LANES
RANK_SPLIT = 128
A_W = N_HEADS_A * HEAD_DIM
N_MAIN = 3 * A_W + N_HEADS_B * HEAD_DIM + 6 * N_KV_B * HEAD_DIM
N_PROJ = N_MAIN + N_KV_B * LANES
COL_QB = (3 * A_W) // (2 * LANES)
COL_KC = (3 * A_W + N_HEADS_B * HEAD_DIM) // LANES
COL_GATE = N_MAIN // LANES
VMEM_LIMIT = 56 * 1024 * 1024

f32 = jnp.float32
bf16 = jnp.bfloat16


def _dot(a, b):
    return jnp.dot(a, b, preferred_element_type=f32)


def _dot_nt(a, b):
    return lax.dot_general(a, b, (((1,), (1,)), ((), ())), preferred_element_type=f32)


def _rms(x, g):
    return x * lax.rsqrt(jnp.mean(x * x, axis=-1, keepdims=True) + RMS_EPS) * g


def _params(*sem):
    return pltpu.CompilerParams(dimension_semantics=sem, vmem_limit_bytes=VMEM_LIMIT)


def _in_proj_kernel(x_ref, g_ref, w_ref, o_ref):
    h = _rms(x_ref[...], g_ref[...]).astype(bf16)
    for n0 in range(0, N_PROJ, 512):
        o_ref[:, n0:n0 + 512] = _dot(h, w_ref[:, n0:n0 + 512])


def _in_proj(x2d, g, w):
    T = x2d.shape[0]
    tm = 512
    return pl.pallas_call(
        _in_proj_kernel,
        grid=(T // tm,),
        in_specs=[pl.BlockSpec((tm, D_MODEL), lambda i: (i, 0)),
                  pl.BlockSpec((1, D_MODEL), lambda i: (0, 0)),
                  pl.BlockSpec((D_MODEL, N_PROJ), lambda i: (0, 0))],
        out_specs=pl.BlockSpec((tm, N_PROJ), lambda i: (i, 0)),
        out_shape=jax.ShapeDtypeStruct((T, N_PROJ), f32),
        compiler_params=_params("parallel"),
        name="in_proj",
    )(x2d, g, w)


def _dil_kernel(slope_ref, q_ref, k_ref, v_ref, o_ref, num_ref, m_ref, l_ref, *, seq):
    hp = pl.program_id(1)
    lane = lax.broadcasted_iota(jnp.int32, (QB, LANES), 1)
    left = lane < HEAD_DIM
    slopes = (slope_ref[2 * hp], slope_ref[2 * hp + 1])

    def rel_of(nk):
        i = lax.broadcasted_iota(jnp.int32, (QB, nk), 0)
        j = lax.broadcasted_iota(jnp.int32, (QB, nk), 1)
        return i - j + (nk - QB)

    def block(p, dil, row0, key0, nk):
        qc = q_ref[pl.ds(row0, QB, stride=dil), :] * SCALE
        kc = k_ref[pl.ds(key0, nk, stride=dil), :].astype(bf16)
        vc = v_ref[pl.ds(key0, nk, stride=dil), :].astype(bf16)
        rel = rel_of(nk)
        valid = (rel >= 0) & (rel <= QB)
        relf = rel.astype(f32)
        q2 = jnp.concatenate([jnp.where(left, qc, 0.0), jnp.where(left, 0.0, qc)], axis=0).astype(bf16)
        s2 = _dot_nt(q2, kc)
        es, ms, ls = [], [], []
        for hh in range(2):
            s = jnp.where(valid, s2[hh * QB:(hh + 1) * QB] - (slopes[hh] * float(dil)) * relf, NEG)
            mx = jnp.max(s, axis=1, keepdims=True)
            e = jnp.exp(s - mx)
            ms.append(mx)
            ls.append(jnp.sum(e, axis=1, keepdims=True))
            es.append(e.astype(bf16))
        num2 = _dot(jnp.concatenate(es, axis=0), vc)
        rows = pl.ds(row0, QB, stride=dil)
        num_ref[p, rows, :] = jnp.where(left, num2[:QB], num2[QB:])
        m_ref[p, rows, :] = jnp.where(left, ms[0], ms[1])
        l_ref[p, rows, :] = jnp.where(left, ls[0], ls[1])

    for p, (window, dil) in enumerate(DILATED_PATTERNS):
        assert window // dil == QB
        nblk = seq // dil // QB

        def per_class(r, _, p=p, dil=dil, nblk=nblk):
            block(p, dil, r, r, QB)

            def per_blk(a, _):
                block(p, dil, r + dil * QB * a, r + dil * QB * (a - 1), 2 * QB)
                return 0

            if nblk > 1:
                lax.fori_loop(1, nblk, per_blk, 0, unroll=3)
            return 0

        if dil == 1:
            per_class(0, 0)
        else:
            lax.fori_loop(0, dil, per_class, 0, unroll=4 if nblk == 1 else 1)

    ch = 256

    def combine(c, _):
        rows = pl.ds(pl.multiple_of(c * ch, ch), ch)
        ms = [m_ref[p, rows, :] for p in range(3)]
        big = jnp.maximum(jnp.maximum(ms[0], ms[1]), ms[2])
        num = jnp.zeros((ch, LANES), f32)
        den = jnp.zeros((ch, LANES), f32)
        for p in range(3):
            w = jnp.exp(ms[p] - big)
            num = num + w * num_ref[p, rows, :]
            den = den + w * l_ref[p, rows, :]
        o_ref[rows, :] = num / den
        return 0

    lax.fori_loop(0, seq // ch, combine, 0)


def _dilated(proj3, slopes_a):
    B, S, _ = proj3.shape
    npair = N_HEADS_A // 2
    blk = lambda off: pl.BlockSpec((None, S, LANES), lambda b, hp, off=off: (b, 0, off + hp))
    return pl.pallas_call(
        functools.partial(_dil_kernel, seq=S),
        grid=(B, npair),
        in_specs=[pl.BlockSpec(memory_space=pltpu.SMEM), blk(0), blk(npair), blk(2 * npair)],
        out_specs=pl.BlockSpec((None, S, LANES), lambda b, hp: (b, 0, hp)),
        out_shape=jax.ShapeDtypeStruct((B, S, A_W), f32),
        scratch_shapes=[pltpu.VMEM((3, S, LANES), f32)] * 3,
        compiler_params=_params("parallel", "parallel"),
        name="dilated_attn",
    )(slopes_a, proj3, proj3, proj3)


def _cmp_kernel(kc_ref, vc_ref, pos_ref, w1k_ref, w2k_ref, w1v_ref, w2v_ref, ko_ref, vo_ref, *, ncb):
    half = CMP_BLOCK // 2
    for x_ref, pi, w1_ref, w2_ref, o_ref in ((kc_ref, 0, w1k_ref, w2k_ref, ko_ref),
                                             (vc_ref, 1, w1v_ref, w2v_ref, vo_ref)):
        lo = jnp.zeros((ncb, 2 * CMP_HIDDEN), f32)
        hi = jnp.zeros((ncb, 2 * CMP_HIDDEN), f32)
        for r in range(half):
            x = x_ref[pl.ds(r, ncb, stride=CMP_STRIDE), :]
            lo = lo + _dot((x + pos_ref[pi, r:r + 1, :]).astype(bf16), w1_ref[r])
            hi = hi + _dot((x + pos_ref[pi, r + half:r + half + 1, :]).astype(bf16), w1_ref[r + half])
        h1 = lo + pltpu.roll(hi, ncb - 1, axis=0)
        o_ref[...] = _dot(jax.nn.gelu(h1).astype(bf16), w2_ref[...])


def _compress(proj3, pos_dup, w1k, w2k, w1v, w2v):
    B, S, _ = proj3.shape
    ncb = S // CMP_STRIDE
    full = lambda a: pl.BlockSpec(a.shape, lambda b: (0,) * a.ndim)
    out = jax.ShapeDtypeStruct((B, ncb, LANES), f32)
    return pl.pallas_call(
        functools.partial(_cmp_kernel, ncb=ncb),
        grid=(B,),
        in_specs=[pl.BlockSpec((None, S, LANES), lambda b: (b, 0, COL_KC)),
                  pl.BlockSpec((None, S, LANES), lambda b: (b, 0, COL_KC + 1)),
                  full(pos_dup), full(w1k), full(w2k), full(w1v), full(w2v)],
        out_specs=[pl.BlockSpec((None, ncb, LANES), lambda b: (b, 0, 0))] * 2,
        out_shape=[out, out],
        compiler_params=_params("parallel"),
        name="nsa_compress",
    )(proj3, proj3, pos_dup, w1k, w2k, w1v, w2v)


def _nsa_kernel(slope_ref, q_ref, ks_ref, vs_ref, kw_ref, vw_ref, gl_ref, kcmp_ref, vcmp_ref,
                ovt_ref, kaug_ref, caug_ref, gsel_ref, o_ref,
                ksa, vsa, kwa, vwa, kca, vca, q_scr, sel_scr, *per_head, seq):
    g = pl.program_id(1)
    qi = pl.program_id(2)
    nsel = seq // SEL_BLOCK
    n_top = min(SEL_TOP, nsel)
    assert nsel <= SEL_LANES and N_LOCAL_BLOCKS * SEL_BLOCK >= QB and n_top > N_LOCAL_BLOCKS
    slab = 32
    ur = GQA_REP * QB // NSA_UNITS
    s_w, e_w, s_m, e_m, m_h, al_h, acc = [per_head[i * NSA_UNITS:(i + 1) * NSA_UNITS] for i in range(7)]

    def group_lanes(x):
        return jnp.where(g == 0, x, pltpu.roll(x, HEAD_DIM, axis=1))

    @pl.when(qi == 0)
    def _prep():
        ch = 256
        lane_c = lax.broadcasted_iota(jnp.int32, (ch, LANES), 1)
        is_k = lane_c < HEAD_DIM
        for c in range(seq // ch):
            sl = slice(c * ch, (c + 1) * ch)
            aug = kaug_ref[sl, :]
            ksa[sl, :] = jnp.where(is_k, group_lanes(ks_ref[sl, :]), aug).astype(bf16)
            kwa[sl, :] = jnp.where(is_k, group_lanes(kw_ref[sl, :]),
                                   jnp.where(lane_c < SEL_LANE0, aug, 0.0)).astype(bf16)
            for src, dst in ((vs_ref, vsa), (vw_ref, vwa)):
                v = group_lanes(src[sl, :])
                dst[sl, :LANES] = jnp.where(is_k, v, 1.0).astype(bf16)
                dst[sl, LANES:] = jnp.where(is_k, 1.0, pltpu.roll(v, HEAD_DIM, axis=1)).astype(bf16)
        lane_k = lax.broadcasted_iota(jnp.int32, kca.shape, 1)
        kca[...] = jnp.where(lane_k < HEAD_DIM, group_lanes(kcmp_ref[...]), caug_ref[...]).astype(bf16)
        vc = group_lanes(vcmp_ref[...])
        vca[...] = jnp.where(lane_k < HEAD_DIM, vc, pltpu.roll(vc, HEAD_DIM, axis=1)).astype(bf16)

    lane = lax.broadcasted_iota(jnp.int32, (QB, LANES), 1)
    left = lane < HEAD_DIM
    ii = lax.broadcasted_iota(jnp.int32, (QB, LANES), 0)
    t_row = qi * QB + ii
    t_hi = (t_row // POS_SPLIT).astype(f32)
    t_lo = (t_row % POS_SPLIT).astype(f32)
    slopes = [slope_ref[g * GQA_REP + r] for r in range(GQA_REP)]
    head = lambda a, r: a[r * QB:(r + 1) * QB]

    def q_head(r):
        x = q_ref[:, (r // 2) * LANES:(r // 2 + 1) * LANES] * SCALE
        if r % 2:
            x = pltpu.roll(x, HEAD_DIM, axis=1)
        m = slopes[r]
        pos = jnp.where(lane == POS_LANE0, (-POS_SPLIT * m) * t_hi,
                        jnp.where(lane == POS_LANE0 + 1, (-m) * t_lo,
                                  jnp.where(lane == POS_LANE0 + 2, POS_SPLIT * m,
                                            jnp.where(lane == POS_LANE0 + 3, m, 0.0))))
        return jnp.where(left, x, pos)

    q4 = jnp.concatenate([q_head(r) for r in range(GQA_REP)], axis=0)
    q4b = q4.astype(bf16)

    cmp_end = (lane * CMP_STRIDE + (CMP_BLOCK - 1))
    valid_c = t_row >= cmp_end
    s4 = _dot_nt(q4b, kca[...])
    ps = []
    p_sum = jnp.zeros((QB, LANES), f32)
    row_bcast = lambda col: jnp.broadcast_to(col, (QB, LANES))
    for r in range(GQA_REP):
        s = jnp.where(valid_c, head(s4, r), NEG)
        e = jnp.exp(s - row_bcast(jnp.max(s, axis=1, keepdims=True)))
        p = jnp.where(valid_c, e, 0.0) / row_bcast(jnp.sum(e, axis=1, keepdims=True))
        p_sum = p_sum + p
        ps.append(p.astype(bf16))
    o_cmp4 = _dot(jnp.concatenate(ps, axis=0), vca[...])

    back = t_row // SEL_BLOCK - lane
    valid_s = (back >= 0) & (lane < nsel)

    @pl.when(2 * qi + 2 <= n_top)
    def _all_valid():
        sel_scr[...] = jnp.where(valid_s, 1.0, 0.0)

    @pl.when(2 * qi + 2 > n_top)
    def _top_k():
        p_hi = p_sum.astype(bf16)
        p_lo = (p_sum - p_hi.astype(f32)).astype(bf16)
        imp_t = (_dot_nt(ovt_ref[...], p_hi) + _dot_nt(ovt_ref[...], p_lo))[:SEL_LANES]
        blk = lax.broadcasted_iota(jnp.int32, (SEL_LANES, QB), 0)
        tq = qi * QB + lax.broadcasted_iota(jnp.int32, (SEL_LANES, QB), 1)
        back_t = tq // SEL_BLOCK - blk
        valid_t = (back_t >= 0) & (blk < nsel)
        forced = (blk == 0) | (valid_t & (back_t < N_LOCAL_BLOCKS))
        score = jnp.where(valid_t, imp_t + jnp.where(forced, FORCE_BONUS, 0.0), NEG)
        score = jnp.where(blk < nsel, score, 2.0 * NEG)
        rank = jnp.zeros((SEL_LANES, QB), jnp.int32)
        for n in range(nsel):
            row = score[n:n + 1, :]
            ahead = (row > score) | ((row == score) & (blk > n))
            rank = rank + ahead.astype(jnp.int32)
        sel_t = jnp.where((rank < n_top) & valid_t, 1.0, 0.0)
        sel_t = jnp.concatenate([sel_t, jnp.zeros((LANES - SEL_LANES, QB), f32)], axis=0)
        sel_scr[...] = sel_t.T

    sel_bias = jnp.where((sel_scr[...] > 0.5) & (lane < 2 * qi), 0.0, MASK_BIG)
    sel_bias = pltpu.roll(sel_bias, SEL_LANE0, axis=1)
    in_sel = (lane >= SEL_LANE0) & (lane < SEL_LANE0 + SEL_LANES)
    for r in range(GQA_REP):
        q_scr[r * QB:(r + 1) * QB, :] = jnp.where(in_sel, sel_bias, head(q4, r)).astype(bf16)

    sig = jax.nn.sigmoid(gl_ref[...])
    g_hi = sig.astype(bf16)
    g_mid = (sig - g_hi.astype(f32)).astype(bf16)
    g_lo = (sig - g_hi.astype(f32) - g_mid.astype(f32)).astype(bf16)
    gate_b = _dot(jnp.concatenate([g_hi, g_mid, g_lo], axis=1), gsel_ref[...])

    def rel_tile(nk, offset):
        i = lax.broadcasted_iota(jnp.int32, (QB, nk), 0)
        j = lax.broadcasted_iota(jnp.int32, (QB, nk), 1)
        return (i - j + offset).astype(f32)

    def softmax_rows(s_ref, e_ref, m_ref, al_ref, nk, bias, running):
        for sl in range(ur // slab):
            rows = slice(sl * slab, (sl + 1) * slab)
            cols = [slice(j * LANES, (j + 1) * LANES) for j in range(nk // LANES)]
            i0 = (sl * slab) % QB
            tiles = [s_ref[rows, c] if bias is None else s_ref[rows, c] + bias[i0:i0 + slab, c] for c in cols]
            mx = tiles[0]
            for t in tiles[1:]:
                mx = jnp.maximum(mx, t)
            m_new = jnp.broadcast_to(jnp.max(mx, axis=1, keepdims=True), (slab, LANES))
            if running:
                m_old = m_ref[rows, :]
                m_new = jnp.maximum(m_old, m_new)
                al_ref[rows, :] = jnp.exp(m_old - m_new)
            if m_ref is not None:
                m_ref[rows, :] = m_new
            for c, t in zip(cols, tiles):
                e_ref[rows, c] = jnp.exp(t - m_new).astype(bf16)

    unit = lambda a, u: a[u * ur:(u + 1) * ur]

    def own_half(pv, u):
        heads = range(u * ur // QB, (u + 1) * ur // QB)
        return jnp.concatenate([pv[(r - heads[0]) * QB:(r - heads[0] + 1) * QB, (r % 2) * LANES:(r % 2 + 1) * LANES]
                                for r in heads], axis=0)

    def staggered(scores, probs, values):
        scores(0)
        for u in range(NSA_UNITS):
            if u + 1 < NSA_UNITS:
                scores(u + 1)
            probs(u)
            values(u)

    nwin = WINDOW + QB
    w0 = pl.multiple_of(jnp.maximum(qi - WINDOW // QB, 0) * QB, QB)
    dist_w = rel_tile(nwin, qi * QB - w0)
    bias_w = jnp.where((dist_w >= 0) & (dist_w < WINDOW), 0.0, NEG)
    o_w = [None] * NSA_UNITS

    def win_scores(u):
        s_w[u][...] = _dot_nt(unit(q4b, u), kwa[pl.ds(w0, nwin), :])

    def win_values(u):
        o_w[u] = own_half(_dot(e_w[u][...], vwa[pl.ds(w0, nwin), :]), u)

    staggered(win_scores, lambda u: softmax_rows(s_w[u], e_w[u], None, None, nwin, bias_w, False), win_values)

    d0 = pl.multiple_of(qi * QB, QB)
    bias_d = jnp.where(rel_tile(QB, 0) >= 0, 0.0, NEG)

    def diag_scores(u):
        s_m[u][:, :QB] = _dot_nt(unit(q4b, u), ksa[pl.ds(d0, QB), :])

    def diag_values(u):
        acc[u][...] = own_half(_dot(e_m[u][:, :QB], vsa[pl.ds(d0, QB), :]), u)

    staggered(diag_scores, lambda u: softmax_rows(s_m[u], e_m[u], m_h[u], None, QB, bias_d, False), diag_values)

    def sel_body(kc, _):
        rows = pl.ds(pl.multiple_of(kc * SEL_CHUNK, SEL_CHUNK), SEL_CHUNK)

        def scores(u):
            s_m[u][...] = _dot_nt(q_scr[u * ur:(u + 1) * ur, :], ksa[rows, :])

        def values(u):
            acc[u][...] = al_h[u][...] * acc[u][...] + own_half(_dot(e_m[u][...], vsa[rows, :]), u)

        staggered(scores, lambda u: softmax_rows(s_m[u], e_m[u], m_h[u], al_h[u], SEL_CHUNK, None, True), values)
        return 0

    per = SEL_CHUNK // QB
    lax.fori_loop(0, (qi + per - 1) // per, sel_body, 0)

    acc_s = jnp.concatenate([a[...] for a in acc], axis=0)
    acc_w = jnp.concatenate(o_w, axis=0)
    for pr in range(GQA_REP // 2):
        ev, od = 2 * pr, 2 * pr + 1
        out = gate_b[:, (3 * pr) * LANES:(3 * pr + 1) * LANES] * jnp.where(left, head(o_cmp4, ev), head(o_cmp4, od))
        for j, a in ((1, acc_s), (2, acc_w)):
            num = jnp.where(left, head(a, ev), head(a, od))
            den = pltpu.roll(jnp.where(left, head(a, od), head(a, ev)), HEAD_DIM, axis=1)
            out = out + gate_b[:, (3 * pr + j) * LANES:(3 * pr + j + 1) * LANES] / den * num
        o_ref[:, pr * LANES:(pr + 1) * LANES] = out


def _nsa(proj3, kcmp, vcmp, slopes_b, ovt, kaug, caug, gsel):
    B, S, _ = proj3.shape
    ncb = kcmp.shape[1]
    nrow = GQA_REP * QB
    ur = nrow // NSA_UNITS
    nwin = WINDOW + QB
    kv = lambda j: pl.BlockSpec((None, S, LANES), lambda b, g, qi, j=j: (b, 0, COL_KC + j))
    full = lambda a: pl.BlockSpec(a.shape, lambda b, g, qi: (0,) * a.ndim)
    return pl.pallas_call(
        functools.partial(_nsa_kernel, seq=S),
        grid=(B, N_KV_B, S // QB),
        in_specs=[pl.BlockSpec(memory_space=pltpu.SMEM),
                  pl.BlockSpec((None, QB, 2 * LANES), lambda b, g, qi: (b, qi, COL_QB + g)),
                  kv(2), kv(3), kv(4), kv(5),
                  pl.BlockSpec((None, QB, LANES), lambda b, g, qi: (b, qi, COL_GATE + g)),
                  pl.BlockSpec((None, ncb, LANES), lambda b, g, qi: (b, 0, 0)),
                  pl.BlockSpec((None, ncb, LANES), lambda b, g, qi: (b, 0, 0)),
                  full(ovt), full(kaug), full(caug), full(gsel)],
        out_specs=pl.BlockSpec((None, QB, 2 * LANES), lambda b, g, qi: (b, qi, g)),
        out_shape=jax.ShapeDtypeStruct((B, S, N_HEADS_B * HEAD_DIM), f32),
        scratch_shapes=([pltpu.VMEM((S, LANES), bf16), pltpu.VMEM((S, 2 * LANES), bf16)] * 2
                        + [pltpu.VMEM((ncb, LANES), bf16)] * 2
                        + [pltpu.VMEM((nrow, LANES), bf16), pltpu.VMEM((QB, LANES), f32)]
                        + [pltpu.VMEM((ur, nwin), f32)] * NSA_UNITS + [pltpu.VMEM((ur, nwin), bf16)] * NSA_UNITS
                        + [pltpu.VMEM((ur, SEL_CHUNK), f32)] * NSA_UNITS
                        + [pltpu.VMEM((ur, SEL_CHUNK), bf16)] * NSA_UNITS
                        + [pltpu.VMEM((ur, LANES), f32)] * (2 * NSA_UNITS)
                        + [pltpu.VMEM((ur, LANES), f32)] * NSA_UNITS),
        compiler_params=_params("parallel", "parallel", "arbitrary"),
        name="nsa_attn",
    )(slopes_b, proj3, proj3, proj3, proj3, proj3, proj3, kcmp, vcmp, ovt, kaug, caug, gsel)


def _out_proj_kernel(x_ref, oa_ref, ob_ref, wa_ref, wb_ref, o_ref):
    o_ref[...] = (x_ref[...] + _dot(oa_ref[...].astype(bf16), wa_ref[...])
                  + _dot(ob_ref[...].astype(bf16), wb_ref[...]))


def _out_proj(x2d, oa, ob, wa, wb):
    T = x2d.shape[0]
    tm = 512
    row = lambda w: pl.BlockSpec((tm, w), lambda i: (i, 0))
    full = lambda a: pl.BlockSpec(a.shape, lambda i: (0,) * a.ndim)
    return pl.pallas_call(
        _out_proj_kernel,
        grid=(T // tm,),
        in_specs=[row(D_MODEL), row(oa.shape[1]), row(ob.shape[1]), full(wa), full(wb)],
        out_specs=row(D_MODEL),
        out_shape=jax.ShapeDtypeStruct((T, D_MODEL), f32),
        compiler_params=_params("parallel"),
        name="out_proj",
    )(x2d, oa, ob, wa, wb)


def _route(h, w_ref, b_ref):
    logit = jnp.dot(h, w_ref[...], preferred_element_type=f32, precision=lax.Precision.HIGHEST) + b_ref[...]
    tm = logit.shape[0]
    lane = lax.broadcasted_iota(jnp.int32, (tm, LANES), 1)
    big = jnp.int32(LANES)
    is_g = lane < N_GROUPS
    gl = jnp.where(is_g, logit, NEG)
    gmax = jnp.max(gl, axis=1, keepdims=True)
    gsum = jnp.sum(jnp.where(is_g, jnp.exp(gl - gmax), 0.0), axis=1, keepdims=True)
    gsel = jnp.min(jnp.where(is_g & (gl == gmax), lane, big), axis=1, keepdims=True)
    gw = 1.0 / gsum
    e_lane = lane - N_GROUPS
    in_grp = (e_lane >= 0) & (e_lane < N_EXPERTS) & (e_lane // EXPERTS_PER_GROUP == gsel)
    el = jnp.where(in_grp, logit, NEG)
    t1 = jnp.max(el, axis=1, keepdims=True)
    i1 = jnp.min(jnp.where(in_grp & (el == t1), lane, big), axis=1, keepdims=True)
    el2 = jnp.where(lane == i1, NEG, el)
    t2 = jnp.max(el2, axis=1, keepdims=True)
    i2 = jnp.min(jnp.where(in_grp & (lane != i1) & (el2 == t2), lane, big), axis=1, keepdims=True)
    e2 = jnp.exp(t2 - t1)
    w1 = gw / (1.0 + e2)
    w2 = gw * e2 / (1.0 + e2)
    return jnp.where(lane == i1, w1, jnp.where(lane == i2, w2, 0.0)), gsel


def _to_token_tiles(ref, x):
    for s in range(D_MODEL // LANES):
        ref[pl.ds(s, x.shape[0], stride=D_MODEL // LANES), :] = x[:, s * LANES:(s + 1) * LANES]


def _from_token_tiles(ref):
    n = D_MODEL // LANES
    return jnp.concatenate([ref[pl.ds(s, ref.shape[0] // n, stride=n), :] for s in range(n)], axis=1)


def _router_kernel(x_ref, g_ref, w_ref, b_ref, tri_ref, stage_ref, meta_ref, cnt_ref, cnt_scr):
    @pl.when(pl.program_id(0) == 0)
    def _zero():
        cnt_scr[...] = jnp.zeros_like(cnt_scr)

    h = _rms(x_ref[...], g_ref[...])
    _to_token_tiles(stage_ref, h)
    _, gsel = _route(h, w_ref, b_ref)
    tm = h.shape[0]
    lane = lax.broadcasted_iota(jnp.int32, (tm, LANES), 1)
    is_g = lane < N_GROUPS

    onehot = jnp.where(is_g & (lane == gsel), 1.0, 0.0)
    before = _dot(tri_ref[...], onehot.astype(bf16)) + cnt_scr[...]
    rank = jnp.sum(onehot * before, axis=1, keepdims=True)
    cnt_scr[...] = before[tm - 1:tm, :] + onehot[tm - 1:tm, :]
    cnt_ref[...] = cnt_scr[...]
    rank_hi = jnp.floor(rank * (1.0 / RANK_SPLIT))
    cols = jnp.where(lane == 0, gsel.astype(f32), jnp.where(lane == 1, rank_hi,
                                                            jnp.where(lane == 2, rank - RANK_SPLIT * rank_hi, 0.0)))
    pick = (lax.broadcasted_iota(jnp.int32, (8, LANES), 0) == lax.broadcasted_iota(jnp.int32, (8, LANES), 1))
    meta_ref[...] = _dot_nt(jnp.where(pick, 1.0, 0.0).astype(bf16), cols.astype(bf16))


def _router(x2d, g, w, b):
    T = x2d.shape[0]
    tm = MOE_TILE
    assert T // RANK_SPLIT <= 256
    tri = jnp.asarray(np.tril(np.ones((tm, tm), np.float32), -1), bf16)
    return pl.pallas_call(
        _router_kernel,
        grid=(T // tm,),
        in_specs=[pl.BlockSpec((tm, D_MODEL), lambda i: (i, 0)),
                  pl.BlockSpec((1, D_MODEL), lambda i: (0, 0)),
                  pl.BlockSpec((D_MODEL, LANES), lambda i: (0, 0)),
                  pl.BlockSpec((1, LANES), lambda i: (0, 0)),
                  pl.BlockSpec((tm, tm), lambda i: (0, 0))],
        out_specs=[pl.BlockSpec((tm * TOKEN_ROWS, LANES), lambda i: (i, 0)),
                   pl.BlockSpec((None, 8, tm), lambda i: (i, 0, 0)),
                   pl.BlockSpec((1, LANES), lambda i: (0, 0))],
        out_shape=[jax.ShapeDtypeStruct((T * TOKEN_ROWS, LANES), f32),
                   jax.ShapeDtypeStruct((T // tm, 8, tm), f32),
                   jax.ShapeDtypeStruct((1, LANES), f32)],
        scratch_shapes=[pltpu.VMEM((1, LANES), f32)],
        compiler_params=_params("arbitrary"),
        name="router",
    )(x2d, g, w, b, tri)


def _token_rows(t, n=1):
    return pl.ds(pl.multiple_of(t * TOKEN_ROWS, TOKEN_ROWS), n * TOKEN_ROWS)


def _gathered_tile(idx_ref, src_ref, buf_ref, sems, tm):
    i = pl.program_id(0)
    n = pl.num_programs(0)

    def issue(step):
        slot = step % 2

        def one(k, _):
            pltpu.make_async_copy(src_ref.at[_token_rows(idx_ref[step * tm + k])],
                                  buf_ref.at[slot, _token_rows(k)], sems.at[slot]).start()
            return 0

        lax.fori_loop(0, tm, one, 0, unroll=8)

    @pl.when(i == 0)
    def _first():
        issue(i)

    @pl.when(i + 1 < n)
    def _next():
        issue(i + 1)

    slot = i % 2
    pltpu.make_async_copy(src_ref.at[_token_rows(0, tm)], buf_ref.at[slot], sems.at[slot]).wait()
    return slot


def _moe_kernel(tg_ref, src_ref, stage_ref, wr_ref, br_ref, wg_ref, wu_ref, wd_ref, o_ref, buf_ref, sems):
    grp = tg_ref[pl.program_id(0)]
    slot = _gathered_tile(src_ref, stage_ref, buf_ref, sems, MOE_TILE)
    h = _from_token_tiles(buf_ref.at[slot])
    gates, _ = _route(h, wr_ref, br_ref)
    x = h.astype(bf16)
    lane = lax.broadcasted_iota(jnp.int32, gates.shape, 1)
    y = jnp.zeros(h.shape, f32)
    for e in range(EXPERTS_PER_GROUP):
        gcol = jnp.sum(jnp.where(lane == N_GROUPS + EXPERTS_PER_GROUP * grp + e, gates, 0.0), axis=1, keepdims=True)
        a = jax.nn.silu(_dot(x, wg_ref[e])) * _dot(x, wu_ref[e])
        y = y + _dot((a * gcol).astype(bf16), wd_ref[e])
    _to_token_tiles(o_ref, y)


def _moe(stage, tile_group, slot_token, w_route, b_route, wg, wu, wd, layer):
    tm = MOE_TILE
    n_slots = slot_token.shape[0]
    w_spec = lambda k, n: pl.BlockSpec((None, None, EXPERTS_PER_GROUP, k, n),
                                       lambda j, tg, src: (layer, tg[j], 0, 0, 0))
    grouped = lambda w: w.reshape(w.shape[0], N_GROUPS, EXPERTS_PER_GROUP, *w.shape[2:])
    return pl.pallas_call(
        _moe_kernel,
        grid_spec=pltpu.PrefetchScalarGridSpec(
            num_scalar_prefetch=2, grid=(n_slots // tm,),
            in_specs=[pl.BlockSpec(memory_space=pl.ANY),
                      pl.BlockSpec((D_MODEL, LANES), lambda j, tg, src: (0, 0)),
                      pl.BlockSpec((1, LANES), lambda j, tg, src: (0, 0)),
                      w_spec(D_MODEL, D_EXPERT), w_spec(D_MODEL, D_EXPERT), w_spec(D_EXPERT, D_MODEL)],
            out_specs=pl.BlockSpec((tm * TOKEN_ROWS, LANES), lambda j, tg, src: (j, 0)),
            scratch_shapes=[pltpu.VMEM((2, tm * TOKEN_ROWS, LANES), f32), pltpu.SemaphoreType.DMA((2,))]),
        out_shape=jax.ShapeDtypeStruct((n_slots * TOKEN_ROWS, LANES), f32),
        compiler_params=_params("arbitrary"),
        name="moe_ffn",
    )(tile_group, slot_token, stage, w_route, b_route, grouped(wg), grouped(wu), grouped(wd))


def _ple_kernel(pos_ref, x_ref, ys_ref, p_ref, g_ref, wg_ref, wp_ref, fg_ref, o_ref, buf_ref, sems, *, final):
    slot = _gathered_tile(pos_ref, ys_ref, buf_ref, sems, x_ref.shape[0])
    x = x_ref[...] + _from_token_tiles(buf_ref.at[slot])
    gate = jax.nn.sigmoid(_dot(_rms(x, g_ref[...]).astype(bf16), wg_ref[...]))
    y = x + gate * _dot(p_ref[...].astype(bf16), wp_ref[...])
    o_ref[...] = _rms(y, fg_ref[...]) if final else y


def _ple(x2d, ys, pos, p3, g, wg, wp, fg, layer, final):
    T = x2d.shape[0]
    tm = 512
    full = lambda a: pl.BlockSpec(a.shape, lambda i, pos: (0,) * a.ndim)
    return pl.pallas_call(
        functools.partial(_ple_kernel, final=final),
        grid_spec=pltpu.PrefetchScalarGridSpec(
            num_scalar_prefetch=1, grid=(T // tm,),
            in_specs=[pl.BlockSpec((tm, D_MODEL), lambda i, pos: (i, 0)),
                      pl.BlockSpec(memory_space=pl.ANY),
                      pl.BlockSpec((None, tm, PLE_DIM), lambda i, pos: (layer, i, 0)),
                      full(g), full(wg), full(wp), full(fg)],
            out_specs=pl.BlockSpec((tm, D_MODEL), lambda i, pos: (i, 0)),
            scratch_shapes=[pltpu.VMEM((2, tm * TOKEN_ROWS, LANES), f32), pltpu.SemaphoreType.DMA((2,))]),
        out_shape=jax.ShapeDtypeStruct((T, D_MODEL), f32),
        compiler_params=_params("arbitrary"),
        name="ple",
    )(pos, x2d, ys, p3, g, wg, wp, fg)


def _alibi_slopes():
    s = 2.0 ** (-8.0 * np.arange(1, N_HEADS_TOTAL + 1) / N_HEADS_TOTAL)
    assert np.all(np.log2(s[1::2]) == np.round(np.log2(s[1::2])))
    return jnp.asarray(s[0::2], f32), jnp.asarray(s[1::2], f32)


def _selection_constants(seq):
    ncb = seq // CMP_STRIDE
    nsel = seq // SEL_BLOCK
    n_cmp = (seq - CMP_BLOCK) // CMP_STRIDE + 1
    cs = np.arange(n_cmp) * CMP_STRIDE
    bs = np.arange(nsel) * SEL_BLOCK
    ov = np.clip(np.minimum(cs[:, None] + CMP_BLOCK, bs[None, :] + SEL_BLOCK)
                 - np.maximum(cs[:, None], bs[None, :]), 0, None) / CMP_BLOCK
    assert ncb == LANES and seq <= POS_SPLIT * 256
    ovt = np.zeros((LANES, ncb), np.float32)
    ovt[:nsel, :n_cmp] = ov.T
    pos = np.arange(seq)
    kaug = np.zeros((seq, LANES), np.float32)
    kaug[:, POS_LANE0:POS_LANE0 + 2] = 1.0
    kaug[:, POS_LANE0 + 2] = pos // POS_SPLIT
    kaug[:, POS_LANE0 + 3] = pos % POS_SPLIT
    kaug[pos, SEL_LANE0 + pos // SEL_BLOCK] = 1.0
    cend = np.arange(ncb) * CMP_STRIDE + CMP_BLOCK - 1
    caug = np.zeros((ncb, LANES), np.float32)
    caug[:, POS_LANE0:POS_LANE0 + 2] = 1.0
    caug[:, POS_LANE0 + 2] = cend // POS_SPLIT
    caug[:, POS_LANE0 + 3] = cend % POS_SPLIT
    gsel = np.zeros((3, LANES, GQA_REP // 2, 3, 2, HEAD_DIM), np.float32)
    for pr in range(GQA_REP // 2):
        for j in range(3):
            for hh in range(2):
                gsel[:, 3 * (2 * pr + hh) + j, pr, j, hh, :] = 1.0
    gsel = gsel.reshape(3 * LANES, (GQA_REP // 2) * 3 * LANES)
    return jnp.asarray(ovt, bf16), jnp.asarray(kaug), jnp.asarray(caug), jnp.asarray(gsel, bf16)


def _dispatch_plan(meta, counts, n_tokens):
    i32 = jnp.int32
    n_slots = n_tokens + N_GROUPS * MOE_TILE
    group = meta[:, 0, :].reshape(n_tokens).astype(i32)
    rank = (meta[:, 1, :] * RANK_SPLIT + meta[:, 2, :]).reshape(n_tokens).astype(i32)
    cnt = counts[0, :N_GROUPS].astype(i32)
    padded = (cnt + MOE_TILE - 1) // MOE_TILE * MOE_TILE
    end = jnp.cumsum(padded)
    pos = (end - padded)[group] + rank
    slot_token = jnp.zeros((n_slots,), i32).at[pos].set(jnp.arange(n_tokens, dtype=i32))
    tile_start = jnp.arange(n_slots // MOE_TILE, dtype=i32) * MOE_TILE
    tile_group = jnp.minimum(jnp.sum(tile_start[:, None] >= end[None, :], axis=1), N_GROUPS - 1).astype(i32)
    return pos, slot_token, tile_group


def _block_diag2(w):
    z = jnp.zeros_like(w)
    return jnp.concatenate([jnp.concatenate([w, z], axis=-1), jnp.concatenate([z, w], axis=-1)], axis=-2)


def _layout_w_in(w):
    gate = w[:, N_MAIN:]
    per = GQA_REP * 3
    blocks = [jnp.pad(gate[:, g * per:(g + 1) * per], ((0, 0), (0, LANES - per))) for g in range(N_KV_B)]
    return jnp.concatenate([w[:, :N_MAIN]] + blocks, axis=1).astype(bf16)


def kernel(x, p, attn_norm, w_in, w_out, w_cmp_k1, w_cmp_k2, w_cmp_v1, w_cmp_v2, cmp_pos, ffn_norm, w_route_group, b_route_group, w_route_expert, b_route_expert, w_expert_gate, w_expert_up, w_expert_down, ple_norm, w_ple_gate, w_ple_proj, final_norm):
    B, S, D = x.shape
    depth = w_in.shape[0]
    T = B * S
    slopes_a, slopes_b = _alibi_slopes()
    ovt, kaug, caug, gsel = _selection_constants(S)
    wg_all = w_expert_gate.astype(bf16)
    wu_all = w_expert_up.astype(bf16)
    wd_all = w_expert_down.astype(bf16)
    p3 = p.reshape(depth, T, PLE_DIM)
    row = lambda v: v.reshape(1, -1)
    n_route = N_GROUPS + N_EXPERTS

    x2d = x.reshape(T, D)
    for i in range(depth):
        proj3 = _in_proj(x2d, row(attn_norm[i]), _layout_w_in(w_in[i])).reshape(B, S, N_PROJ)
        oa = _dilated(proj3, slopes_a)
        w1 = lambda w: _block_diag2(w.reshape(CMP_BLOCK, HEAD_DIM, CMP_HIDDEN)).astype(bf16)
        pos_dup = jnp.concatenate([cmp_pos[i], cmp_pos[i]], axis=-1)
        kcmp, vcmp = _compress(proj3, pos_dup, w1(w_cmp_k1[i]), _block_diag2(w_cmp_k2[i]).astype(bf16),
                               w1(w_cmp_v1[i]), _block_diag2(w_cmp_v2[i]).astype(bf16))
        ob = _nsa(proj3, kcmp, vcmp, slopes_b, ovt, kaug, caug, gsel)
        wo = w_out[i].astype(bf16)
        x2d = _out_proj(x2d, oa.reshape(T, A_W), ob.reshape(T, -1), wo[:A_W], wo[A_W:])
        w_route = jnp.pad(jnp.concatenate([w_route_group[i], w_route_expert[i]], axis=1),
                          ((0, 0), (0, LANES - n_route)))
        b_route = jnp.pad(jnp.concatenate([b_route_group[i], b_route_expert[i]]), (0, LANES - n_route))
        stage, meta, counts = _router(x2d, row(ffn_norm[i]), w_route, row(b_route))
        pos, slot_token, tile_group = _dispatch_plan(meta, counts, T)
        ys = _moe(stage, tile_group, slot_token, w_route, row(b_route), wg_all, wu_all, wd_all, i)
        x2d = _ple(x2d, ys, pos, p3, row(ple_norm[i]), w_ple_gate[i].astype(bf16),
                   w_ple_proj[i].astype(bf16), row(final_norm), i, i == depth - 1)
    return x2d.reshape(B, S, D)
```

```python
import functools

import numpy as np
import jax
import jax.numpy as jnp
from jax import lax
from jax.experimental import pallas as pl
from jax.experimental.pallas import tpu as pltpu

D_MODEL = 1024
PLE_DIM = 256
HEAD_DIM = 64
N_HEADS_A = 8
N_HEADS_B = 8
N_KV_B = 2
GQA_REP = N_HEADS_B // N_KV_B
N_HEADS_TOTAL = N_HEADS_A + N_HEADS_B
DILATED_PATTERNS = ((128, 1), (512, 4), (2048, 16))
CMP_BLOCK = 32
CMP_STRIDE = 16
CMP_HIDDEN = 256
SEL_BLOCK = 64
SEL_TOP = 16
N_LOCAL_BLOCKS = 2
WINDOW = 512
N_GROUPS = 4
EXPERTS_PER_GROUP = 4
N_EXPERTS = N_GROUPS * EXPERTS_PER_GROUP
D_EXPERT = 512
RMS_EPS = 1e-6
NEG = -1e30
FORCE_BONUS = 1e4
SCALE = HEAD_DIM ** -0.5

LANES = 128
QB = 128
SEL_CHUNK = 512
NSA_UNITS = 1
POS_LANE0 = HEAD_DIM
POS_SPLIT = 16
SEL_LANE0 = POS_LANE0 + 4
SEL_LANES = 32
MASK_BIG = -(2.0 ** 100)
MOE_TILE = 512
TOKEN_ROWS = D_MODEL // LANES
RANK_SPLIT = 128
A_W = N_HEADS_A * HEAD_DIM
N_MAIN = 3 * A_W + N_HEADS_B * HEAD_DIM + 6 * N_KV_B * HEAD_DIM
N_PROJ = N_MAIN + N_KV_B * LANES
COL_QB = (3 * A_W) // (2 * LANES)
COL_KC = (3 * A_W + N_HEADS_B * HEAD_DIM) // LANES
COL_GATE = N_MAIN // LANES
VMEM_LIMIT = 56 * 1024 * 1024

f32 = jnp.float32
bf16 = jnp.bfloat16


def _dot(a, b):
    return jnp.dot(a, b, preferred_element_type=f32)


def _dot_nt(a, b):
    return lax.dot_general(a, b, (((1,), (1,)), ((), ())), preferred_element_type=f32)


def _rms(x, g):
    return x * lax.rsqrt(jnp.mean(x * x, axis=-1, keepdims=True) + RMS_EPS) * g


def _params(*sem):
    return pltpu.CompilerParams(dimension_semantics=sem, vmem_limit_bytes=VMEM_LIMIT)


def _in_proj_kernel(x_ref, g_ref, w_ref, o_ref):
    h = _rms(x_ref[...], g_ref[...]).astype(bf16)
    for n0 in range(0, N_PROJ, 512):
        o_ref[:, n0:n0 + 512] = _dot(h, w_ref[:, n0:n0 + 512])


def _in_proj(x2d, g, w):
    T = x2d.shape[0]
    tm = 512
    return pl.pallas_call(
        _in_proj_kernel,
        grid=(T // tm,),
        in_specs=[pl.BlockSpec((tm, D_MODEL), lambda i: (i, 0)),
                  pl.BlockSpec((1, D_MODEL), lambda i: (0, 0)),
                  pl.BlockSpec((D_MODEL, N_PROJ), lambda i: (0, 0))],
        out_specs=pl.BlockSpec((tm, N_PROJ), lambda i: (i, 0)),
        out_shape=jax.ShapeDtypeStruct((T, N_PROJ), f32),
        compiler_params=_params("parallel"),
        name="in_proj",
    )(x2d, g, w)


def _dil_kernel(slope_ref, q_ref, k_ref, v_ref, o_ref, num_ref, m_ref, l_ref, *, seq):
    hp = pl.program_id(1)
    lane = lax.broadcasted_iota(jnp.int32, (QB, LANES), 1)
    left = lane < HEAD_DIM
    slopes = (slope_ref[2 * hp], slope_ref[2 * hp + 1])

    def rel_of(nk):
        i = lax.broadcasted_iota(jnp.int32, (QB, nk), 0)
        j = lax.broadcasted_iota(jnp.int32, (QB, nk), 1)
        return i - j + (nk - QB)

    def block(p, dil, row0, key0, nk):
        qc = q_ref[pl.ds(row0, QB, stride=dil), :] * SCALE
        kc = k_ref[pl.ds(key0, nk, stride=dil), :].astype(bf16)
        vc = v_ref[pl.ds(key0, nk, stride=dil), :].astype(bf16)
        rel = rel_of(nk)
        valid = (rel >= 0) & (rel <= QB)
        relf = rel.astype(f32)
        q2 = jnp.concatenate([jnp.where(left, qc, 0.0), jnp.where(left, 0.0, qc)], axis=0).astype(bf16)
        s2 = _dot_nt(q2, kc)
        es, ms, ls = [], [], []
        for hh in range(2):
            s = jnp.where(valid, s2[hh * QB:(hh + 1) * QB] - (slopes[hh] * float(dil)) * relf, NEG)
            mx = jnp.max(s, axis=1, keepdims=True)
            e = jnp.exp(s - mx)
            ms.append(mx)
            ls.append(jnp.sum(e, axis=1, keepdims=True))
            es.append(e.astype(bf16))
        num2 = _dot(jnp.concatenate(es, axis=0), vc)
        rows = pl.ds(row0, QB, stride=dil)
        num_ref[p, rows, :] = jnp.where(left, num2[:QB], num2[QB:])
        m_ref[p, rows, :] = jnp.where(left, ms[0], ms[1])
        l_ref[p, rows, :] = jnp.where(left, ls[0], ls[1])

    for p, (window, dil) in enumerate(DILATED_PATTERNS):
        assert window // dil == QB
        nblk = seq // dil // QB

        def per_class(r, _, p=p, dil=dil, nblk=nblk):
            block(p, dil, r, r, QB)

            def per_blk(a, _):
                block(p, dil, r + dil * QB * a, r + dil * QB * (a - 1), 2 * QB)
                return 0

            if nblk > 1:
                lax.fori_loop(1, nblk, per_blk, 0, unroll=3)
            return 0

        if dil == 1:
            per_class(0, 0)
        else:
            lax.fori_loop(0, dil, per_class, 0, unroll=4 if nblk == 1 else 1)

    ch = 256

    def combine(c, _):
        rows = pl.ds(pl.multiple_of(c * ch, ch), ch)
        ms = [m_ref[p, rows, :] for p in range(3)]
        big = jnp.maximum(jnp.maximum(ms[0], ms[1]), ms[2])
        num = jnp.zeros((ch, LANES), f32)
        den = jnp.zeros((ch, LANES), f32)
        for p in range(3):
            w = jnp.exp(ms[p] - big)
            num = num + w * num_ref[p, rows, :]
            den = den + w * l_ref[p, rows, :]
        o_ref[rows, :] = num / den
        return 0

    lax.fori_loop(0, seq // ch, combine, 0)


def _dilated(proj3, slopes_a):
    B, S, _ = proj3.shape
    npair = N_HEADS_A // 2
    blk = lambda off: pl.BlockSpec((None, S, LANES), lambda b, hp, off=off: (b, 0, off + hp))
    return pl.pallas_call(
        functools.partial(_dil_kernel, seq=S),
        grid=(B, npair),
        in_specs=[pl.BlockSpec(memory_space=pltpu.SMEM), blk(0), blk(npair), blk(2 * npair)],
        out_specs=pl.BlockSpec((None, S, LANES), lambda b, hp: (b, 0, hp)),
        out_shape=jax.ShapeDtypeStruct((B, S, A_W), f32),
        scratch_shapes=[pltpu.VMEM((3, S, LANES), f32)] * 3,
        compiler_params=_params("parallel", "parallel"),
        name="dilated_attn",
    )(slopes_a, proj3, proj3, proj3)


def _cmp_kernel(kc_ref, vc_ref, pos_ref, w1k_ref, w2k_ref, w1v_ref, w2v_ref, ko_ref, vo_ref, *, ncb):
    half = CMP_BLOCK // 2
    for x_ref, pi, w1_ref, w2_ref, o_ref in ((kc_ref, 0, w1k_ref, w2k_ref, ko_ref),
                                             (vc_ref, 1, w1v_ref, w2v_ref, vo_ref)):
        lo = jnp.zeros((ncb, 2 * CMP_HIDDEN), f32)
        hi = jnp.zeros((ncb, 2 * CMP_HIDDEN), f32)
        for r in range(half):
            x = x_ref[pl.ds(r, ncb, stride=CMP_STRIDE), :]
            lo = lo + _dot((x + pos_ref[pi, r:r + 1, :]).astype(bf16), w1_ref[r])
            hi = hi + _dot((x + pos_ref[pi, r + half:r + half + 1, :]).astype(bf16), w1_ref[r + half])
        h1 = lo + pltpu.roll(hi, ncb - 1, axis=0)
        o_ref[...] = _dot(jax.nn.gelu(h1).astype(bf16), w2_ref[...])


def _compress(proj3, pos_dup, w1k, w2k, w1v, w2v):
    B, S, _ = proj3.shape
    ncb = S // CMP_STRIDE
    full = lambda a: pl.BlockSpec(a.shape, lambda b: (0,) * a.ndim)
    out = jax.ShapeDtypeStruct((B, ncb, LANES), f32)
    return pl.pallas_call(
        functools.partial(_cmp_kernel, ncb=ncb),
        grid=(B,),
        in_specs=[pl.BlockSpec((None, S, LANES), lambda b: (b, 0, COL_KC)),
                  pl.BlockSpec((None, S, LANES), lambda b: (b, 0, COL_KC + 1)),
                  full(pos_dup), full(w1k), full(w2k), full(w1v), full(w2v)],
        out_specs=[pl.BlockSpec((None, ncb, LANES), lambda b: (b, 0, 0))] * 2,
        out_shape=[out, out],
        compiler_params=_params("parallel"),
        name="nsa_compress",
    )(proj3, proj3, pos_dup, w1k, w2k, w1v, w2v)


def _nsa_kernel(slope_ref, q_ref, ks_ref, vs_ref, kw_ref, vw_ref, gl_ref, kcmp_ref, vcmp_ref,
                ovt_ref, kaug_ref, caug_ref, gsel_ref, o_ref,
                ksa, vsa, kwa, vwa, kca, vca, q_scr, sel_scr, *per_head, seq):
    g = pl.program_id(1)
    qi = pl.program_id(2)
    nsel = seq // SEL_BLOCK
    n_top = min(SEL_TOP, nsel)
    assert nsel <= SEL_LANES and N_LOCAL_BLOCKS * SEL_BLOCK >= QB and n_top > N_LOCAL_BLOCKS
    slab = 32
    ur = GQA_REP * QB // NSA_UNITS
    s_w, e_w, s_m, e_m, m_h, al_h, acc = [per_head[i * NSA_UNITS:(i + 1) * NSA_UNITS] for i in range(7)]

    def group_lanes(x):
        return jnp.where(g == 0, x, pltpu.roll(x, HEAD_DIM, axis=1))

    @pl.when(qi == 0)
    def _prep():
        ch = 256
        lane_c = lax.broadcasted_iota(jnp.int32, (ch, LANES), 1)
        is_k = lane_c < HEAD_DIM
        for c in range(seq // ch):
            sl = slice(c * ch, (c + 1) * ch)
            aug = kaug_ref[sl, :]
            ksa[sl, :] = jnp.where(is_k, group_lanes(ks_ref[sl, :]), aug).astype(bf16)
            kwa[sl, :] = jnp.where(is_k, group_lanes(kw_ref[sl, :]),
                                   jnp.where(lane_c < SEL_LANE0, aug, 0.0)).astype(bf16)
            for src, dst in ((vs_ref, vsa), (vw_ref, vwa)):
                v = group_lanes(src[sl, :])
                dst[sl, :LANES] = jnp.where(is_k, v, 1.0).astype(bf16)
                dst[sl, LANES:] = jnp.where(is_k, 1.0, pltpu.roll(v, HEAD_DIM, axis=1)).astype(bf16)
        lane_k = lax.broadcasted_iota(jnp.int32, kca.shape, 1)
        kca[...] = jnp.where(lane_k < HEAD_DIM, group_lanes(kcmp_ref[...]), caug_ref[...]).astype(bf16)
        vc = group_lanes(vcmp_ref[...])
        vca[...] = jnp.where(lane_k < HEAD_DIM, vc, pltpu.roll(vc, HEAD_DIM, axis=1)).astype(bf16)

    lane = lax.broadcasted_iota(jnp.int32, (QB, LANES), 1)
    left = lane < HEAD_DIM
    ii = lax.broadcasted_iota(jnp.int32, (QB, LANES), 0)
    t_row = qi * QB + ii
    t_hi = (t_row // POS_SPLIT).astype(f32)
    t_lo = (t_row % POS_SPLIT).astype(f32)
    slopes = [slope_ref[g * GQA_REP + r] for r in range(GQA_REP)]
    head = lambda a, r: a[r * QB:(r + 1) * QB]

    def q_head(r):
        x = q_ref[:, (r // 2) * LANES:(r // 2 + 1) * LANES] * SCALE
        if r % 2:
            x = pltpu.roll(x, HEAD_DIM, axis=1)
        m = slopes[r]
        pos = jnp.where(lane == POS_LANE0, (-POS_SPLIT * m) * t_hi,
                        jnp.where(lane == POS_LANE0 + 1, (-m) * t_lo,
                                  jnp.where(lane == POS_LANE0 + 2, POS_SPLIT * m,
                                            jnp.where(lane == POS_LANE0 + 3, m, 0.0))))
        return jnp.where(left, x, pos)

    q4 = jnp.concatenate([q_head(r) for r in range(GQA_REP)], axis=0)
    q4b = q4.astype(bf16)

    cmp_end = (lane * CMP_STRIDE + (CMP_BLOCK - 1))
    valid_c = t_row >= cmp_end
    s4 = _dot_nt(q4b, kca[...])
    ps = []
    p_sum = jnp.zeros((QB, LANES), f32)
    row_bcast = lambda col: jnp.broadcast_to(col, (QB, LANES))
    for r in range(GQA_REP):
        s = jnp.where(valid_c, head(s4, r), NEG)
        e = jnp.exp(s - row_bcast(jnp.max(s, axis=1, keepdims=True)))
        p = jnp.where(valid_c, e, 0.0) / row_bcast(jnp.sum(e, axis=1, keepdims=True))
        p_sum = p_sum + p
        ps.append(p.astype(bf16))
    o_cmp4 = _dot(jnp.concatenate(ps, axis=0), vca[...])

    back = t_row // SEL_BLOCK - lane
    valid_s = (back >= 0) & (lane < nsel)

    @pl.when(2 * qi + 2 <= n_top)
    def _all_valid():
        sel_scr[...] = jnp.where(valid_s, 1.0, 0.0)

    @pl.when(2 * qi + 2 > n_top)
    def _top_k():
        p_hi = p_sum.astype(bf16)
        p_lo = (p_sum - p_hi.astype(f32)).astype(bf16)
        imp_t = (_dot_nt(ovt_ref[...], p_hi) + _dot_nt(ovt_ref[...], p_lo))[:SEL_LANES]
        blk = lax.broadcasted_iota(jnp.int32, (SEL_LANES, QB), 0)
        tq = qi * QB + lax.broadcasted_iota(jnp.int32, (SEL_LANES, QB), 1)
        back_t = tq // SEL_BLOCK - blk
        valid_t = (back_t >= 0) & (blk < nsel)
        forced = (blk == 0) | (valid_t & (back_t < N_LOCAL_BLOCKS))
        score = jnp.where(valid_t, imp_t + jnp.where(forced, FORCE_BONUS, 0.0), NEG)
        score = jnp.where(blk < nsel, score, 2.0 * NEG)
        rank = jnp.zeros((SEL_LANES, QB), jnp.int32)
        for n in range(nsel):
            row = score[n:n + 1, :]
            ahead = (row > score) | ((row == score) & (blk > n))
            rank = rank + ahead.astype(jnp.int32)
        sel_t = jnp.where((rank < n_top) & valid_t, 1.0, 0.0)
        sel_t = jnp.concatenate([sel_t, jnp.zeros((LANES - SEL_LANES, QB), f32)], axis=0)
        sel_scr[...] = sel_t.T

    sel_bias = jnp.where((sel_scr[...] > 0.5) & (lane < 2 * qi), 0.0, MASK_BIG)
    sel_bias = pltpu.roll(sel_bias, SEL_LANE0, axis=1)
    in_sel = (lane >= SEL_LANE0) & (lane < SEL_LANE0 + SEL_LANES)
    for r in range(GQA_REP):
        q_scr[r * QB:(r + 1) * QB, :] = jnp.where(in_sel, sel_bias, head(q4, r)).astype(bf16)

    sig = jax.nn.sigmoid(gl_ref[...])
    g_hi = sig.astype(bf16)
    g_mid = (sig - g_hi.astype(f32)).astype(bf16)
    g_lo = (sig - g_hi.astype(f32) - g_mid.astype(f32)).astype(bf16)
    gate_b = _dot(jnp.concatenate([g_hi, g_mid, g_lo], axis=1), gsel_ref[...])

    def rel_tile(nk, offset):
        i = lax.broadcasted_iota(jnp.int32, (QB, nk), 0)
        j = lax.broadcasted_iota(jnp.int32, (QB, nk), 1)
        return (i - j + offset).astype(f32)

    def softmax_rows(s_ref, e_ref, m_ref, al_ref, nk, bias, running):
        for sl in range(ur // slab):
            rows = slice(sl * slab, (sl + 1) * slab)
            cols = [slice(j * LANES, (j + 1) * LANES) for j in range(nk // LANES)]
            i0 = (sl * slab) % QB
            tiles = [s_ref[rows, c] if bias is None else s_ref[rows, c] + bias[i0:i0 + slab, c] for c in cols]
            mx = tiles[0]
            for t in tiles[1:]:
                mx = jnp.maximum(mx, t)
            m_new = jnp.broadcast_to(jnp.max(mx, axis=1, keepdims=True), (slab, LANES))
            if running:
                m_old = m_ref[rows, :]
                m_new = jnp.maximum(m_old, m_new)
                al_ref[rows, :] = jnp.exp(m_old - m_new)
            if m_ref is not None:
                m_ref[rows, :] = m_new
            for c, t in zip(cols, tiles):
                e_ref[rows, c] = jnp.exp(t - m_new).astype(bf16)

    unit = lambda a, u: a[u * ur:(u + 1) * ur]

    def own_half(pv, u):
        heads = range(u * ur // QB, (u + 1) * ur // QB)
        return jnp.concatenate([pv[(r - heads[0]) * QB:(r - heads[0] + 1) * QB, (r % 2) * LANES:(r % 2 + 1) * LANES]
                                for r in heads], axis=0)

    def staggered(scores, probs, values):
        scores(0)
        for u in range(NSA_UNITS):
            if u + 1 < NSA_UNITS:
                scores(u + 1)
            probs(u)
            values(u)

    nwin = WINDOW + QB
    w0 = pl.multiple_of(jnp.maximum(qi - WINDOW // QB, 0) * QB, QB)
    dist_w = rel_tile(nwin, qi * QB - w0)
    bias_w = jnp.where((dist_w >= 0) & (dist_w < WINDOW), 0.0, NEG)
    o_w = [None] * NSA_UNITS

    def win_scores(u):
        s_w[u][...] = _dot_nt(unit(q4b, u), kwa[pl.ds(w0, nwin), :])

    def win_values(u):
        o_w[u] = own_half(_dot(e_w[u][...], vwa[pl.ds(w0, nwin), :]), u)

    staggered(win_scores, lambda u: softmax_rows(s_w[u], e_w[u], None, None, nwin, bias_w, False), win_values)

    d0 = pl.multiple_of(qi * QB, QB)
    bias_d = jnp.where(rel_tile(QB, 0) >= 0, 0.0, NEG)

    def diag_scores(u):
        s_m[u][:, :QB] = _dot_nt(unit(q4b, u), ksa[pl.ds(d0, QB), :])

    def diag_values(u):
        acc[u][...] = own_half(_dot(e_m[u][:, :QB], vsa[pl.ds(d0, QB), :]), u)

    staggered(diag_scores, lambda u: softmax_rows(s_m[u], e_m[u], m_h[u], None, QB, bias_d, False), diag_values)

    def sel_body(kc, _):
        rows = pl.ds(pl.multiple_of(kc * SEL_CHUNK, SEL_CHUNK), SEL_CHUNK)

        def scores(u):
            s_m[u][...] = _dot_nt(q_scr[u * ur:(u + 1) * ur, :], ksa[rows, :])

        def values(u):
            acc[u][...] = al_h[u][...] * acc[u][...] + own_half(_dot(e_m[u][...], vsa[rows, :]), u)

        staggered(scores, lambda u: softmax_rows(s_m[u], e_m[u], m_h[u], al_h[u], SEL_CHUNK, None, True), values)
        return 0

    per = SEL_CHUNK // QB
    lax.fori_loop(0, (qi + per - 1) // per, sel_body, 0)

    acc_s = jnp.concatenate([a[...] for a in acc], axis=0)
    acc_w = jnp.concatenate(o_w, axis=0)
    for pr in range(GQA_REP // 2):
        ev, od = 2 * pr, 2 * pr + 1
        out = gate_b[:, (3 * pr) * LANES:(3 * pr + 1) * LANES] * jnp.where(left, head(o_cmp4, ev), head(o_cmp4, od))
        for j, a in ((1, acc_s), (2, acc_w)):
            num = jnp.where(left, head(a, ev), head(a, od))
            den = pltpu.roll(jnp.where(left, head(a, od), head(a, ev)), HEAD_DIM, axis=1)
            out = out + gate_b[:, (3 * pr + j) * LANES:(3 * pr + j + 1) * LANES] / den * num
        o_ref[:, pr * LANES:(pr + 1) * LANES] = out


def _nsa(proj3, kcmp, vcmp, slopes_b, ovt, kaug, caug, gsel):
    B, S, _ = proj3.shape
    ncb = kcmp.shape[1]
    nrow = GQA_REP * QB
    ur = nrow // NSA_UNITS
    nwin = WINDOW + QB
    kv = lambda j: pl.BlockSpec((None, S, LANES), lambda b, g, qi, j=j: (b, 0, COL_KC + j))
    full = lambda a: pl.BlockSpec(a.shape, lambda b, g, qi: (0,) * a.ndim)
    return pl.pallas_call(
        functools.partial(_nsa_kernel, seq=S),
        grid=(B, N_KV_B, S // QB),
        in_specs=[pl.BlockSpec(memory_space=pltpu.SMEM),
                  pl.BlockSpec((None, QB, 2 * LANES), lambda b, g, qi: (b, qi, COL_QB + g)),
                  kv(2), kv(3), kv(4), kv(5),
                  pl.BlockSpec((None, QB, LANES), lambda b, g, qi: (b, qi, COL_GATE + g)),
                  pl.BlockSpec((None, ncb, LANES), lambda b, g, qi: (b, 0, 0)),
                  pl.BlockSpec((None, ncb, LANES), lambda b, g, qi: (b, 0, 0)),
                  full(ovt), full(kaug), full(caug), full(gsel)],
        out_specs=pl.BlockSpec((None, QB, 2 * LANES), lambda b, g, qi: (b, qi, g)),
        out_shape=jax.ShapeDtypeStruct((B, S, N_HEADS_B * HEAD_DIM), f32),
        scratch_shapes=([pltpu.VMEM((S, LANES), bf16), pltpu.VMEM((S, 2 * LANES), bf16)] * 2
                        + [pltpu.VMEM((ncb, LANES), bf16)] * 2
                        + [pltpu.VMEM((nrow, LANES), bf16), pltpu.VMEM((QB, LANES), f32)]
                        + [pltpu.VMEM((ur, nwin), f32)] * NSA_UNITS + [pltpu.VMEM((ur, nwin), bf16)] * NSA_UNITS
                        + [pltpu.VMEM((ur, SEL_CHUNK), f32)] * NSA_UNITS
                        + [pltpu.VMEM((ur, SEL_CHUNK), bf16)] * NSA_UNITS
                        + [pltpu.VMEM((ur, LANES), f32)] * (2 * NSA_UNITS)
                        + [pltpu.VMEM((ur, LANES), f32)] * NSA_UNITS),
        compiler_params=_params("parallel", "parallel", "arbitrary"),
        name="nsa_attn",
    )(slopes_b, proj3, proj3, proj3, proj3, proj3, proj3, kcmp, vcmp, ovt, kaug, caug, gsel)


def _out_proj_kernel(x_ref, oa_ref, ob_ref, wa_ref, wb_ref, o_ref):
    o_ref[...] = (x_ref[...] + _dot(oa_ref[...].astype(bf16), wa_ref[...])
                  + _dot(ob_ref[...].astype(bf16), wb_ref[...]))


def _out_proj(x2d, oa, ob, wa, wb):
    T = x2d.shape[0]
    tm = 512
    row = lambda w: pl.BlockSpec((tm, w), lambda i: (i, 0))
    full = lambda a: pl.BlockSpec(a.shape, lambda i: (0,) * a.ndim)
    return pl.pallas_call(
        _out_proj_kernel,
        grid=(T // tm,),
        in_specs=[row(D_MODEL), row(oa.shape[1]), row(ob.shape[1]), full(wa), full(wb)],
        out_specs=row(D_MODEL),
        out_shape=jax.ShapeDtypeStruct((T, D_MODEL), f32),
        compiler_params=_params("parallel"),
        name="out_proj",
    )(x2d, oa, ob, wa, wb)


def _route(h, w_ref, b_ref):
    logit = jnp.dot(h, w_ref[...], preferred_element_type=f32, precision=lax.Precision.HIGHEST) + b_ref[...]
    tm = logit.shape[0]
    lane = lax.broadcasted_iota(jnp.int32, (tm, LANES), 1)
    big = jnp.int32(LANES)
    is_g = lane < N_GROUPS
    gl = jnp.where(is_g, logit, NEG)
    gmax = jnp.max(gl, axis=1, keepdims=True)
    gsum = jnp.sum(jnp.where(is_g, jnp.exp(gl - gmax), 0.0), axis=1, keepdims=True)
    gsel = jnp.min(jnp.where(is_g & (gl == gmax), lane, big), axis=1, keepdims=True)
    gw = 1.0 / gsum
    e_lane = lane - N_GROUPS
    in_grp = (e_lane >= 0) & (e_lane < N_EXPERTS) & (e_lane // EXPERTS_PER_GROUP == gsel)
    el = jnp.where(in_grp, logit, NEG)
    t1 = jnp.max(el, axis=1, keepdims=True)
    i1 = jnp.min(jnp.where(in_grp & (el == t1), lane, big), axis=1, keepdims=True)
    el2 = jnp.where(lane == i1, NEG, el)
    t2 = jnp.max(el2, axis=1, keepdims=True)
    i2 = jnp.min(jnp.where(in_grp & (lane != i1) & (el2 == t2), lane, big), axis=1, keepdims=True)
    e2 = jnp.exp(t2 - t1)
    w1 = gw / (1.0 + e2)
    w2 = gw * e2 / (1.0 + e2)
    return jnp.where(lane == i1, w1, jnp.where(lane == i2, w2, 0.0)), gsel


def _to_token_tiles(ref, x):
    for s in range(D_MODEL // LANES):
        ref[pl.ds(s, x.shape[0], stride=D_MODEL // LANES), :] = x[:, s * LANES:(s + 1) * LANES]


def _from_token_tiles(ref):
    n = D_MODEL // LANES
    return jnp.concatenate([ref[pl.ds(s, ref.shape[0] // n, stride=n), :] for s in range(n)], axis=1)


def _router_kernel(x_ref, g_ref, w_ref, b_ref, tri_ref, stage_ref, meta_ref, cnt_ref, cnt_scr):
    @pl.when(pl.program_id(0) == 0)
    def _zero():
        cnt_scr[...] = jnp.zeros_like(cnt_scr)

    h = _rms(x_ref[...], g_ref[...])
    _to_token_tiles(stage_ref, h)
    _, gsel = _route(h, w_ref, b_ref)
    tm = h.shape[0]
    lane = lax.broadcasted_iota(jnp.int32, (tm, LANES), 1)
    is_g = lane < N_GROUPS

    onehot = jnp.where(is_g & (lane == gsel), 1.0, 0.0)
    before = _dot(tri_ref[...], onehot.astype(bf16)) + cnt_scr[...]
    rank = jnp.sum(onehot * before, axis=1, keepdims=True)
    cnt_scr[...] = before[tm - 1:tm, :] + onehot[tm - 1:tm, :]
    cnt_ref[...] = cnt_scr[...]
    rank_hi = jnp.floor(rank * (1.0 / RANK_SPLIT))
    cols = jnp.where(lane == 0, gsel.astype(f32), jnp.where(lane == 1, rank_hi,
                                                            jnp.where(lane == 2, rank - RANK_SPLIT * rank_hi, 0.0)))
    pick = (lax.broadcasted_iota(jnp.int32, (8, LANES), 0) == lax.broadcasted_iota(jnp.int32, (8, LANES), 1))
    meta_ref[...] = _dot_nt(jnp.where(pick, 1.0, 0.0).astype(bf16), cols.astype(bf16))


def _router(x2d, g, w, b):
    T = x2d.shape[0]
    tm = MOE_TILE
    assert T // RANK_SPLIT <= 256
    tri = jnp.asarray(np.tril(np.ones((tm, tm), np.float32), -1), bf16)
    return pl.pallas_call(
        _router_kernel,
        grid=(T // tm,),
        in_specs=[pl.BlockSpec((tm, D_MODEL), lambda i: (i, 0)),
                  pl.BlockSpec((1, D_MODEL), lambda i: (0, 0)),
                  pl.BlockSpec((D_MODEL, LANES), lambda i: (0, 0)),
                  pl.BlockSpec((1, LANES), lambda i: (0, 0)),
                  pl.BlockSpec((tm, tm), lambda i: (0, 0))],
        out_specs=[pl.BlockSpec((tm * TOKEN_ROWS, LANES), lambda i: (i, 0)),
                   pl.BlockSpec((None, 8, tm), lambda i: (i, 0, 0)),
                   pl.BlockSpec((1, LANES), lambda i: (0, 0))],
        out_shape=[jax.ShapeDtypeStruct((T * TOKEN_ROWS, LANES), f32),
                   jax.ShapeDtypeStruct((T // tm, 8, tm), f32),
                   jax.ShapeDtypeStruct((1, LANES), f32)],
        scratch_shapes=[pltpu.VMEM((1, LANES), f32)],
        compiler_params=_params("arbitrary"),
        name="router",
    )(x2d, g, w, b, tri)


def _token_rows(t, n=1):
    return pl.ds(pl.multiple_of(t * TOKEN_ROWS, TOKEN_ROWS), n * TOKEN_ROWS)


class _TokenGather:
    def __init__(self, idx_ref, src_ref, buf_ref, sems, tm):
        self.idx, self.src, self.buf, self.sems, self.tm = idx_ref, src_ref, buf_ref, sems, tm
        self.i, self.n = pl.program_id(0), pl.num_programs(0)

    def _start(self, step, k):
        slot = step % 2
        pltpu.make_async_copy(self.src.at[_token_rows(self.idx[step * self.tm + k])],
                              self.buf.at[slot, _token_rows(k)], self.sems.at[slot]).start()

    def _wait(self, slot):
        pltpu.make_async_copy(self.src.at[_token_rows(0, self.tm)], self.buf.at[slot], self.sems.at[slot]).wait()

    def current(self):
        @pl.when(self.i == 0)
        def _first():
            def one(k, _):
                self._start(self.i, k)
                return 0

            lax.fori_loop(0, self.tm, one, 0, unroll=8)

        self._wait(self.i % 2)
        return self.buf.at[self.i % 2]

    def prefetch_next(self):
        nxt = jnp.where(self.i + 1 < self.n, self.i + 1, 0)
        for k in range(self.tm):
            pltpu.make_async_copy(self.src.at[_token_rows(self.idx[nxt * self.tm + k])],
                                  self.buf.at[(self.i + 1) % 2, _token_rows(k)],
                                  self.sems.at[(self.i + 1) % 2]).start()

    def finish(self):
        @pl.when(self.i + 1 == self.n)
        def _drain():
            self._wait((self.i + 1) % 2)


def _moe_kernel(tg_ref, src_ref, stage_ref, wr_ref, br_ref, wg_ref, wu_ref, wd_ref, o_ref, buf_ref, sems):
    grp = tg_ref[pl.program_id(0)]
    gather = _TokenGather(src_ref, stage_ref, buf_ref, sems, MOE_TILE)
    h = _from_token_tiles(gather.current())
    gather.prefetch_next()
    gates, _ = _route(h, wr_ref, br_ref)
    x = h.astype(bf16)
    lane = lax.broadcasted_iota(jnp.int32, gates.shape, 1)
    y = jnp.zeros(h.shape, f32)
    for e in range(EXPERTS_PER_GROUP):
        gcol = jnp.sum(jnp.where(lane == N_GROUPS + EXPERTS_PER_GROUP * grp + e, gates, 0.0), axis=1, keepdims=True)
        a = jax.nn.silu(_dot(x, wg_ref[e])) * _dot(x, wu_ref[e])
        y = y + _dot((a * gcol).astype(bf16), wd_ref[e])
    _to_token_tiles(o_ref, y)
    gather.finish()


def _moe(stage, tile_group, slot_token, w_route, b_route, wg, wu, wd, layer):
    tm = MOE_TILE
    n_slots = slot_token.shape[0]
    w_spec = lambda k, n: pl.BlockSpec((None, None, EXPERTS_PER_GROUP, k, n),
                                       lambda j, tg, src: (layer, tg[j], 0, 0, 0))
    grouped = lambda w: w.reshape(w.shape[0], N_GROUPS, EXPERTS_PER_GROUP, *w.shape[2:])
    return pl.pallas_call(
        _moe_kernel,
        grid_spec=pltpu.PrefetchScalarGridSpec(
            num_scalar_prefetch=2, grid=(n_slots // tm,),
            in_specs=[pl.BlockSpec(memory_space=pl.ANY),
                      pl.BlockSpec((D_MODEL, LANES), lambda j, tg, src: (0, 0)),
                      pl.BlockSpec((1, LANES), lambda j, tg, src: (0, 0)),
                      w_spec(D_MODEL, D_EXPERT), w_spec(D_MODEL, D_EXPERT), w_spec(D_EXPERT, D_MODEL)],
            out_specs=pl.BlockSpec((tm * TOKEN_ROWS, LANES), lambda j, tg, src: (j, 0)),
            scratch_shapes=[pltpu.VMEM((2, tm * TOKEN_ROWS, LANES), f32), pltpu.SemaphoreType.DMA((2,))]),
        out_shape=jax.ShapeDtypeStruct((n_slots * TOKEN_ROWS, LANES), f32),
        compiler_params=_params("arbitrary"),
        name="moe_ffn",
    )(tile_group, slot_token, stage, w_route, b_route, grouped(wg), grouped(wu), grouped(wd))


def _ple_kernel(pos_ref, x_ref, ys_ref, p_ref, g_ref, wg_ref, wp_ref, fg_ref, o_ref, buf_ref, sems, *, final):
    gather = _TokenGather(pos_ref, ys_ref, buf_ref, sems, x_ref.shape[0])
    x = x_ref[...] + _from_token_tiles(gather.current())
    gather.prefetch_next()
    gate = jax.nn.sigmoid(_dot(_rms(x, g_ref[...]).astype(bf16), wg_ref[...]))
    y = x + gate * _dot(p_ref[...].astype(bf16), wp_ref[...])
    o_ref[...] = _rms(y, fg_ref[...]) if final else y
    gather.finish()


def _ple(x2d, ys, pos, p3, g, wg, wp, fg, layer, final):
    T = x2d.shape[0]
    tm = 512
    full = lambda a: pl.BlockSpec(a.shape, lambda i, pos: (0,) * a.ndim)
    return pl.pallas_call(
        functools.partial(_ple_kernel, final=final),
        grid_spec=pltpu.PrefetchScalarGridSpec(
            num_scalar_prefetch=1, grid=(T // tm,),
            in_specs=[pl.BlockSpec((tm, D_MODEL), lambda i, pos: (i, 0)),
                      pl.BlockSpec(memory_space=pl.ANY),
                      pl.BlockSpec((None, tm, PLE_DIM), lambda i, pos: (layer, i, 0)),
                      full(g), full(wg), full(wp), full(fg)],
            out_specs=pl.BlockSpec((tm, D_MODEL), lambda i, pos: (i, 0)),
            scratch_shapes=[pltpu.VMEM((2, tm * TOKEN_ROWS, LANES), f32), pltpu.SemaphoreType.DMA((2,))]),
        out_shape=jax.ShapeDtypeStruct((T, D_MODEL), f32),
        compiler_params=_params("arbitrary"),
        name="ple",
    )(pos, x2d, ys, p3, g, wg, wp, fg)


def _alibi_slopes():
    s = 2.0 ** (-8.0 * np.arange(1, N_HEADS_TOTAL + 1) / N_HEADS_TOTAL)
    assert np.all(np.log2(s[1::2]) == np.round(np.log2(s[1::2])))
    return jnp.asarray(s[0::2], f32), jnp.asarray(s[1::2], f32)


def _selection_constants(seq):
    ncb = seq // CMP_STRIDE
    nsel = seq // SEL_BLOCK
    n_cmp = (seq - CMP_BLOCK) // CMP_STRIDE + 1
    cs = np.arange(n_cmp) * CMP_STRIDE
    bs = np.arange(nsel) * SEL_BLOCK
    ov = np.clip(np.minimum(cs[:, None] + CMP_BLOCK, bs[None, :] + SEL_BLOCK)
                 - np.maximum(cs[:, None], bs[None, :]), 0, None) / CMP_BLOCK
    assert ncb == LANES and seq <= POS_SPLIT * 256
    ovt = np.zeros((LANES, ncb), np.float32)
    ovt[:nsel, :n_cmp] = ov.T
    pos = np.arange(seq)
    kaug = np.zeros((seq, LANES), np.float32)
    kaug[:, POS_LANE0:POS_LANE0 + 2] = 1.0
    kaug[:, POS_LANE0 + 2] = pos // POS_SPLIT
    kaug[:, POS_LANE0 + 3] = pos % POS_SPLIT
    kaug[pos, SEL_LANE0 + pos // SEL_BLOCK] = 1.0
    cend = np.arange(ncb) * CMP_STRIDE + CMP_BLOCK - 1
    caug = np.zeros((ncb, LANES), np.float32)
    caug[:, POS_LANE0:POS_LANE0 + 2] = 1.0
    caug[:, POS_LANE0 + 2] = cend // POS_SPLIT
    caug[:, POS_LANE0 + 3] = cend % POS_SPLIT
    gsel = np.zeros((3, LANES, GQA_REP // 2, 3, 2, HEAD_DIM), np.float32)
    for pr in range(GQA_REP // 2):
        for j in range(3):
            for hh in range(2):
                gsel[:, 3 * (2 * pr + hh) + j, pr, j, hh, :] = 1.0
    gsel = gsel.reshape(3 * LANES, (GQA_REP // 2) * 3 * LANES)
    return jnp.asarray(ovt, bf16), jnp.asarray(kaug), jnp.asarray(caug), jnp.asarray(gsel, bf16)


def _dispatch_plan(meta, counts, n_tokens):
    i32 = jnp.int32
    n_slots = n_tokens + N_GROUPS * MOE_TILE
    group = meta[:, 0, :].reshape(n_tokens).astype(i32)
    rank = (meta[:, 1, :] * RANK_SPLIT + meta[:, 2, :]).reshape(n_tokens).astype(i32)
    cnt = counts[0, :N_GROUPS].astype(i32)
    padded = (cnt + MOE_TILE - 1) // MOE_TILE * MOE_TILE
    end = jnp.cumsum(padded)
    pos = (end - padded)[group] + rank
    slot_token = jnp.zeros((n_slots,), i32).at[pos].set(jnp.arange(n_tokens, dtype=i32), unique_indices=True)
    tile_start = jnp.arange(n_slots // MOE_TILE, dtype=i32) * MOE_TILE
    tile_group = jnp.minimum(jnp.sum(tile_start[:, None] >= end[None, :], axis=1), N_GROUPS - 1).astype(i32)
    return pos, slot_token, tile_group


def _block_diag2(w):
    z = jnp.zeros_like(w)
    return jnp.concatenate([jnp.concatenate([w, z], axis=-1), jnp.concatenate([z, w], axis=-1)], axis=-2)


def _layout_w_in(w):
    gate = w[:, N_MAIN:]
    per = GQA_REP * 3
    blocks = [jnp.pad(gate[:, g * per:(g + 1) * per], ((0, 0), (0, LANES - per))) for g in range(N_KV_B)]
    return jnp.concatenate([w[:, :N_MAIN]] + blocks, axis=1).astype(bf16)


def kernel(x, p, attn_norm, w_in, w_out, w_cmp_k1, w_cmp_k2, w_cmp_v1, w_cmp_v2, cmp_pos, ffn_norm, w_route_group, b_route_group, w_route_expert, b_route_expert, w_expert_gate, w_expert_up, w_expert_down, ple_norm, w_ple_gate, w_ple_proj, final_norm):
    B, S, D = x.shape
    depth = w_in.shape[0]
    T = B * S
    slopes_a, slopes_b = _alibi_slopes()
    ovt, kaug, caug, gsel = _selection_constants(S)
    wg_all = w_expert_gate.astype(bf16)
    wu_all = w_expert_up.astype(bf16)
    wd_all = w_expert_down.astype(bf16)
    p3 = p.reshape(depth, T, PLE_DIM)
    row = lambda v: v.reshape(1, -1)
    n_route = N_GROUPS + N_EXPERTS

    x2d = x.reshape(T, D)
    for i in range(depth):
        proj3 = _in_proj(x2d, row(attn_norm[i]), _layout_w_in(w_in[i])).reshape(B, S, N_PROJ)
        oa = _dilated(proj3, slopes_a)
        w1 = lambda w: _block_diag2(w.reshape(CMP_BLOCK, HEAD_DIM, CMP_HIDDEN)).astype(bf16)
        pos_dup = jnp.concatenate([cmp_pos[i], cmp_pos[i]], axis=-1)
        kcmp, vcmp = _compress(proj3, pos_dup, w1(w_cmp_k1[i]), _block_diag2(w_cmp_k2[i]).astype(bf16),
                               w1(w_cmp_v1[i]), _block_diag2(w_cmp_v2[i]).astype(bf16))
        ob = _nsa(proj3, kcmp, vcmp, slopes_b, ovt, kaug, caug, gsel)
        wo = w_out[i].astype(bf16)
        x2d = _out_proj(x2d, oa.reshape(T, A_W), ob.reshape(T, -1), wo[:A_W], wo[A_W:])
        w_route = jnp.pad(jnp.concatenate([w_route_group[i], w_route_expert[i]], axis=1),
                          ((0, 0), (0, LANES - n_route)))
        b_route = jnp.pad(jnp.concatenate([b_route_group[i], b_route_expert[i]]), (0, LANES - n_route))
        stage, meta, counts = _router(x2d, row(ffn_norm[i]), w_route, row(b_route))
        pos, slot_token, tile_group = _dispatch_plan(meta, counts, T)
        ys = _moe(stage, tile_group, slot_token, w_route, row(b_route), wg_all, wu_all, wd_all, i)
        x2d = _ple(x2d, ys, pos, p3, row(ple_norm[i]), w_ple_gate[i].astype(bf16),
                   w_ple_proj[i].astype(bf16), row(final_norm), i, i == depth - 1)
    return x2d.reshape(B, S, D)
```

```python
import functools

import numpy as np
import jax
import jax.numpy as jnp
from jax import lax
from jax.experimental import pallas as pl
from jax.experimental.pallas import tpu as pltpu

D_MODEL = 1024
PLE_DIM = 256
HEAD_DIM = 64
N_HEADS_A = 8
N_HEADS_B = 8
N_KV_B = 2
GQA_REP = N_HEADS_B // N_KV_B
N_HEADS_TOTAL = N_HEADS_A + N_HEADS_B
DILATED_PATTERNS = ((128, 1), (512, 4), (2048, 16))
CMP_BLOCK = 32
CMP_STRIDE = 16
CMP_HIDDEN = 256
SEL_BLOCK = 64
SEL_TOP = 16
N_LOCAL_BLOCKS = 2
WINDOW = 512
N_GROUPS = 4
EXPERTS_PER_GROUP = 4
N_EXPERTS = N_GROUPS * EXPERTS_PER_GROUP
D_EXPERT = 512
RMS_EPS = 1e-6
NEG = -1e30
FORCE_BONUS = 1e4
SCALE = HEAD_DIM ** -0.5

LANES = 128
QB = 128
SEL_CHUNK = 512
NSA_UNITS = 1
DIL_INFLIGHT = 3
POS_LANE0 = HEAD_DIM
POS_SPLIT = 16
SEL_LANE0 = POS_LANE0 + 4
SEL_LANES = 32
MASK_BIG = -(2.0 ** 100)
MOE_TILE = 512
TOKEN_ROWS = D_MODEL // LANES
RANK_SPLIT = 128
A_W = N_HEADS_A * HEAD_DIM
N_MAIN = 3 * A_W + N_HEADS_B * HEAD_DIM + 6 * N_KV_B * HEAD_DIM
N_PROJ = N_MAIN + N_KV_B * LANES
COL_QB = (3 * A_W) // (2 * LANES)
COL_KC = (3 * A_W + N_HEADS_B * HEAD_DIM) // LANES
COL_GATE = N_MAIN // LANES
VMEM_LIMIT = 56 * 1024 * 1024

f32 = jnp.float32
bf16 = jnp.bfloat16


def _dot(a, b):
    return jnp.dot(a, b, preferred_element_type=f32)


def _dot_nt(a, b):
    return lax.dot_general(a, b, (((1,), (1,)), ((), ())), preferred_element_type=f32)


def _rms(x, g):
    return x * lax.rsqrt(jnp.mean(x * x, axis=-1, keepdims=True) + RMS_EPS) * g


def _params(*sem):
    return pltpu.CompilerParams(dimension_semantics=sem, vmem_limit_bytes=VMEM_LIMIT)


def _in_proj_kernel(x_ref, g_ref, w_ref, o_ref):
    h = _rms(x_ref[...], g_ref[...]).astype(bf16)
    for n0 in range(0, N_PROJ, 512):
        o_ref[:, n0:n0 + 512] = _dot(h, w_ref[:, n0:n0 + 512])


def _in_proj(x2d, g, w):
    T = x2d.shape[0]
    tm = 512
    return pl.pallas_call(
        _in_proj_kernel,
        grid=(T // tm,),
        in_specs=[pl.BlockSpec((tm, D_MODEL), lambda i: (i, 0)),
                  pl.BlockSpec((1, D_MODEL), lambda i: (0, 0)),
                  pl.BlockSpec((D_MODEL, N_PROJ), lambda i: (0, 0))],
        out_specs=pl.BlockSpec((tm, N_PROJ), lambda i: (i, 0)),
        out_shape=jax.ShapeDtypeStruct((T, N_PROJ), f32),
        compiler_params=_params("parallel"),
        name="in_proj",
    )(x2d, g, w)


def _dil_kernel(slope_ref, q_ref, k_ref, v_ref, o_ref, out_ref, lse_ref, bias_scr, *bufs, seq):
    hp = pl.program_id(1)
    lane = lax.broadcasted_iota(jnp.int32, (QB, LANES), 1)
    left = lane < HEAD_DIM
    slab = 32
    s_bufs, e_bufs = bufs[:DIL_INFLIGHT], bufs[DIL_INFLIGHT:]

    i = lax.broadcasted_iota(jnp.int32, (QB, 2 * QB), 0)
    j = lax.broadcasted_iota(jnp.int32, (QB, 2 * QB), 1)
    rel = i - j + QB
    valid = (rel >= 0) & (rel <= QB)
    relf = rel.astype(f32)
    for p, (window, dil) in enumerate(DILATED_PATTERNS):
        assert window // dil == QB
        for hh in range(2):
            bias_scr[2 * p + hh] = jnp.where(valid, (-float(dil) * slope_ref[2 * hp + hh]) * relf, NEG)

    ones = jnp.ones((2 * QB, LANES), bf16)

    def scores(u, blk):
        p, dil, row0, key0, nk = blk
        qc = q_ref[pl.ds(row0, QB, stride=dil), :] * SCALE
        q2 = jnp.concatenate([jnp.where(left, qc, 0.0), jnp.where(left, 0.0, qc)], axis=0).astype(bf16)
        s_bufs[u][:, :nk] = _dot_nt(q2, k_ref[pl.ds(key0, nk, stride=dil), :].astype(bf16))

    def probs(u, blk):
        p, dil, row0, key0, nk = blk
        c0 = 2 * QB - nk
        ms = []
        for hh in range(2):
            parts = []
            for sl in range(QB // slab):
                rows = slice(hh * QB + sl * slab, hh * QB + (sl + 1) * slab)
                tiles = [s_bufs[u][rows, c:c + LANES]
                         + bias_scr[2 * p + hh, sl * slab:(sl + 1) * slab, c0 + c:c0 + c + LANES]
                         for c in range(0, nk, LANES)]
                mx = tiles[0]
                for t in tiles[1:]:
                    mx = jnp.maximum(mx, t)
                mx = jnp.broadcast_to(jnp.max(mx, axis=1, keepdims=True), (slab, LANES))
                for c, t in zip(range(0, nk, LANES), tiles):
                    e_bufs[u][rows, c:c + LANES] = jnp.exp(t - mx).astype(bf16)
                parts.append(mx)
            ms.append(jnp.concatenate(parts, axis=0))
        return jnp.where(left, ms[0], ms[1])

    def values(u, blk, row_max):
        p, dil, row0, key0, nk = blk
        v2 = jnp.concatenate([v_ref[pl.ds(key0, nk, stride=dil), :].astype(bf16), ones[:nk]], axis=1)
        res = _dot(e_bufs[u][:, :nk], v2)
        rows = pl.ds(row0, QB, stride=dil)
        den = jnp.where(left, res[:QB, LANES:], res[QB:, LANES:])
        out_ref[p, rows, :] = jnp.where(left, res[:QB, :LANES], res[QB:, :LANES]) / den
        lse_ref[p, rows, :] = row_max + jnp.log(den)

    def run(blocks):
        ahead = DIL_INFLIGHT - 1
        for n in range(min(ahead, len(blocks))):
            scores(n % DIL_INFLIGHT, blocks[n])
        for n, blk in enumerate(blocks):
            if n + ahead < len(blocks):
                scores((n + ahead) % DIL_INFLIGHT, blocks[n + ahead])
            values(n % DIL_INFLIGHT, blk, probs(n % DIL_INFLIGHT, blk))

    def first(p, dil, r):
        return (p, dil, r, r, QB)

    def later(p, dil, r, a):
        return (p, dil, r + dil * QB * a, r + dil * QB * (a - 1), 2 * QB)

    for p, (window, dil) in enumerate(DILATED_PATTERNS):
        nblk = seq // dil // QB
        if dil == 1:
            run([first(p, dil, 0)])

            def trip(t, _, p=p, dil=dil):
                run([later(p, dil, 0, 1 + 3 * t + n) for n in range(3)])
                return 0

            assert (nblk - 1) % 3 == 0
            lax.fori_loop(0, (nblk - 1) // 3, trip, 0)
        elif nblk > 1:
            def per_class(r, _, p=p, dil=dil, nblk=nblk):
                run([first(p, dil, r)] + [later(p, dil, r, a) for a in range(1, nblk)])
                return 0

            lax.fori_loop(0, dil, per_class, 0)
        else:
            def some_classes(t, _, p=p, dil=dil):
                run([first(p, dil, 8 * t + n) for n in range(8)])
                return 0

            lax.fori_loop(0, dil // 8, some_classes, 0)

    ch = 256

    def combine(c, _):
        rows = pl.ds(pl.multiple_of(c * ch, ch), ch)
        lses = [lse_ref[p, rows, :] for p in range(len(DILATED_PATTERNS))]
        big = functools.reduce(jnp.maximum, lses)
        num = jnp.zeros((ch, LANES), f32)
        den = jnp.zeros((ch, LANES), f32)
        for p, lse in enumerate(lses):
            w = jnp.exp(lse - big)
            num = num + w * out_ref[p, rows, :]
            den = den + w
        o_ref[rows, :] = num / den
        return 0

    lax.fori_loop(0, seq // ch, combine, 0)


def _dilated(proj3, slopes_a):
    B, S, _ = proj3.shape
    npair = N_HEADS_A // 2
    blk = lambda off: pl.BlockSpec((None, S, LANES), lambda b, hp, off=off: (b, 0, off + hp))
    return pl.pallas_call(
        functools.partial(_dil_kernel, seq=S),
        grid=(B, npair),
        in_specs=[pl.BlockSpec(memory_space=pltpu.SMEM), blk(0), blk(npair), blk(2 * npair)],
        out_specs=pl.BlockSpec((None, S, LANES), lambda b, hp: (b, 0, hp)),
        out_shape=jax.ShapeDtypeStruct((B, S, A_W), f32),
        scratch_shapes=([pltpu.VMEM((len(DILATED_PATTERNS), S, LANES), f32)] * 2
                        + [pltpu.VMEM((2 * len(DILATED_PATTERNS), QB, 2 * QB), f32)]
                        + [pltpu.VMEM((2 * QB, 2 * QB), f32)] * DIL_INFLIGHT
                        + [pltpu.VMEM((2 * QB, 2 * QB), bf16)] * DIL_INFLIGHT),
        compiler_params=_params("parallel", "parallel"),
        name="dilated_attn",
    )(slopes_a, proj3, proj3, proj3)


def _cmp_kernel(kc_ref, vc_ref, pos_ref, w1k_ref, w2k_ref, w1v_ref, w2v_ref, ko_ref, vo_ref, *, ncb):
    half = CMP_BLOCK // 2
    for x_ref, pi, w1_ref, w2_ref, o_ref in ((kc_ref, 0, w1k_ref, w2k_ref, ko_ref),
                                             (vc_ref, 1, w1v_ref, w2v_ref, vo_ref)):
        lo = jnp.zeros((ncb, 2 * CMP_HIDDEN), f32)
        hi = jnp.zeros((ncb, 2 * CMP_HIDDEN), f32)
        for r in range(half):
            x = x_ref[pl.ds(r, ncb, stride=CMP_STRIDE), :]
            lo = lo + _dot((x + pos_ref[pi, r:r + 1, :]).astype(bf16), w1_ref[r])
            hi = hi + _dot((x + pos_ref[pi, r + half:r + half + 1, :]).astype(bf16), w1_ref[r + half])
        h1 = lo + pltpu.roll(hi, ncb - 1, axis=0)
        o_ref[...] = _dot(jax.nn.gelu(h1).astype(bf16), w2_ref[...])


def _compress(proj3, pos_dup, w1k, w2k, w1v, w2v):
    B, S, _ = proj3.shape
    ncb = S // CMP_STRIDE
    full = lambda a: pl.BlockSpec(a.shape, lambda b: (0,) * a.ndim)
    out = jax.ShapeDtypeStruct((B, ncb, LANES), f32)
    return pl.pallas_call(
        functools.partial(_cmp_kernel, ncb=ncb),
        grid=(B,),
        in_specs=[pl.BlockSpec((None, S, LANES), lambda b: (b, 0, COL_KC)),
                  pl.BlockSpec((None, S, LANES), lambda b: (b, 0, COL_KC + 1)),
                  full(pos_dup), full(w1k), full(w2k), full(w1v), full(w2v)],
        out_specs=[pl.BlockSpec((None, ncb, LANES), lambda b: (b, 0, 0))] * 2,
        out_shape=[out, out],
        compiler_params=_params("parallel"),
        name="nsa_compress",
    )(proj3, proj3, pos_dup, w1k, w2k, w1v, w2v)


def _nsa_kernel(slope_ref, q_ref, ks_ref, vs_ref, kw_ref, vw_ref, gl_ref, kcmp_ref, vcmp_ref,
                ovt_ref, kaug_ref, caug_ref, gsel_ref, o_ref,
                ksa, vsa, kwa, vwa, kca, vca, q_scr, sel_scr, *per_head, seq):
    g = pl.program_id(1)
    qi = pl.program_id(2)
    nsel = seq // SEL_BLOCK
    n_top = min(SEL_TOP, nsel)
    assert nsel <= SEL_LANES and N_LOCAL_BLOCKS * SEL_BLOCK >= QB and n_top > N_LOCAL_BLOCKS
    slab = 32
    ur = GQA_REP * QB // NSA_UNITS
    s_w, e_w, s_m, e_m, m_h, al_h, acc = [per_head[i * NSA_UNITS:(i + 1) * NSA_UNITS] for i in range(7)]

    def group_lanes(x):
        return jnp.where(g == 0, x, pltpu.roll(x, HEAD_DIM, axis=1))

    @pl.when(qi == 0)
    def _prep():
        ch = 256
        lane_c = lax.broadcasted_iota(jnp.int32, (ch, LANES), 1)
        is_k = lane_c < HEAD_DIM
        for c in range(seq // ch):
            sl = slice(c * ch, (c + 1) * ch)
            aug = kaug_ref[sl, :]
            ksa[sl, :] = jnp.where(is_k, group_lanes(ks_ref[sl, :]), aug).astype(bf16)
            kwa[sl, :] = jnp.where(is_k, group_lanes(kw_ref[sl, :]),
                                   jnp.where(lane_c < SEL_LANE0, aug, 0.0)).astype(bf16)
            for src, dst in ((vs_ref, vsa), (vw_ref, vwa)):
                v = group_lanes(src[sl, :])
                dst[sl, :LANES] = jnp.where(is_k, v, 1.0).astype(bf16)
                dst[sl, LANES:] = jnp.where(is_k, 1.0, pltpu.roll(v, HEAD_DIM, axis=1)).astype(bf16)
        lane_k = lax.broadcasted_iota(jnp.int32, kca.shape, 1)
        kca[...] = jnp.where(lane_k < HEAD_DIM, group_lanes(kcmp_ref[...]), caug_ref[...]).astype(bf16)
        vc = group_lanes(vcmp_ref[...])
        vca[...] = jnp.where(lane_k < HEAD_DIM, vc, pltpu.roll(vc, HEAD_DIM, axis=1)).astype(bf16)

    lane = lax.broadcasted_iota(jnp.int32, (QB, LANES), 1)
    left = lane < HEAD_DIM
    ii = lax.broadcasted_iota(jnp.int32, (QB, LANES), 0)
    t_row = qi * QB + ii
    t_hi = (t_row // POS_SPLIT).astype(f32)
    t_lo = (t_row % POS_SPLIT).astype(f32)
    slopes = [slope_ref[g * GQA_REP + r] for r in range(GQA_REP)]
    head = lambda a, r: a[r * QB:(r + 1) * QB]

    def q_head(r):
        x = q_ref[:, (r // 2) * LANES:(r // 2 + 1) * LANES] * SCALE
        if r % 2:
            x = pltpu.roll(x, HEAD_DIM, axis=1)
        m = slopes[r]
        pos = jnp.where(lane == POS_LANE0, (-POS_SPLIT * m) * t_hi,
                        jnp.where(lane == POS_LANE0 + 1, (-m) * t_lo,
                                  jnp.where(lane == POS_LANE0 + 2, POS_SPLIT * m,
                                            jnp.where(lane == POS_LANE0 + 3, m, 0.0))))
        return jnp.where(left, x, pos)

    q4 = jnp.concatenate([q_head(r) for r in range(GQA_REP)], axis=0)
    q4b = q4.astype(bf16)

    cmp_end = (lane * CMP_STRIDE + (CMP_BLOCK - 1))
    valid_c = t_row >= cmp_end
    s4 = _dot_nt(q4b, kca[...])
    ps = []
    p_sum = jnp.zeros((QB, LANES), f32)
    row_bcast = lambda col: jnp.broadcast_to(col, (QB, LANES))
    for r in range(GQA_REP):
        s = jnp.where(valid_c, head(s4, r), NEG)
        e = jnp.exp(s - row_bcast(jnp.max(s, axis=1, keepdims=True)))
        p = jnp.where(valid_c, e, 0.0) / row_bcast(jnp.sum(e, axis=1, keepdims=True))
        p_sum = p_sum + p
        ps.append(p.astype(bf16))
    o_cmp4 = _dot(jnp.concatenate(ps, axis=0), vca[...])

    back = t_row // SEL_BLOCK - lane
    valid_s = (back >= 0) & (lane < nsel)

    @pl.when(2 * qi + 2 <= n_top)
    def _all_valid():
        sel_scr[...] = jnp.where(valid_s, 1.0, 0.0)

    @pl.when(2 * qi + 2 > n_top)
    def _top_k():
        p_hi = p_sum.astype(bf16)
        p_lo = (p_sum - p_hi.astype(f32)).astype(bf16)
        imp_t = (_dot_nt(ovt_ref[...], p_hi) + _dot_nt(ovt_ref[...], p_lo))[:SEL_LANES]
        blk = lax.broadcasted_iota(jnp.int32, (SEL_LANES, QB), 0)
        tq = qi * QB + lax.broadcasted_iota(jnp.int32, (SEL_LANES, QB), 1)
        back_t = tq // SEL_BLOCK - blk
        valid_t = (back_t >= 0) & (blk < nsel)
        forced = (blk == 0) | (valid_t & (back_t < N_LOCAL_BLOCKS))
        score = jnp.where(valid_t, imp_t + jnp.where(forced, FORCE_BONUS, 0.0), NEG)
        score = jnp.where(blk < nsel, score, 2.0 * NEG)
        rank = jnp.zeros((SEL_LANES, QB), jnp.int32)
        for n in range(nsel):
            row = score[n:n + 1, :]
            ahead = (row > score) | ((row == score) & (blk > n))
            rank = rank + ahead.astype(jnp.int32)
        sel_t = jnp.where((rank < n_top) & valid_t, 1.0, 0.0)
        sel_t = jnp.concatenate([sel_t, jnp.zeros((LANES - SEL_LANES, QB), f32)], axis=0)
        sel_scr[...] = sel_t.T

    sel_bias = jnp.where((sel_scr[...] > 0.5) & (lane < 2 * qi), 0.0, MASK_BIG)
    sel_bias = pltpu.roll(sel_bias, SEL_LANE0, axis=1)
    in_sel = (lane >= SEL_LANE0) & (lane < SEL_LANE0 + SEL_LANES)
    for r in range(GQA_REP):
        q_scr[r * QB:(r + 1) * QB, :] = jnp.where(in_sel, sel_bias, head(q4, r)).astype(bf16)

    sig = jax.nn.sigmoid(gl_ref[...])
    g_hi = sig.astype(bf16)
    g_mid = (sig - g_hi.astype(f32)).astype(bf16)
    g_lo = (sig - g_hi.astype(f32) - g_mid.astype(f32)).astype(bf16)
    gate_b = _dot(jnp.concatenate([g_hi, g_mid, g_lo], axis=1), gsel_ref[...])

    def rel_tile(nk, offset):
        i = lax.broadcasted_iota(jnp.int32, (QB, nk), 0)
        j = lax.broadcasted_iota(jnp.int32, (QB, nk), 1)
        return (i - j + offset).astype(f32)

    def softmax_rows(s_ref, e_ref, m_ref, al_ref, nk, bias, running):
        for sl in range(ur // slab):
            rows = slice(sl * slab, (sl + 1) * slab)
            cols = [slice(j * LANES, (j + 1) * LANES) for j in range(nk // LANES)]
            i0 = (sl * slab) % QB
            tiles = [s_ref[rows, c] if bias is None else s_ref[rows, c] + bias[i0:i0 + slab, c] for c in cols]
            mx = tiles[0]
            for t in tiles[1:]:
                mx = jnp.maximum(mx, t)
            m_new = jnp.broadcast_to(jnp.max(mx, axis=1, keepdims=True), (slab, LANES))
            if running:
                m_old = m_ref[rows, :]
                m_new = jnp.maximum(m_old, m_new)
                al_ref[rows, :] = jnp.exp(m_old - m_new)
            if m_ref is not None:
                m_ref[rows, :] = m_new
            for c, t in zip(cols, tiles):
                e_ref[rows, c] = jnp.exp(t - m_new).astype(bf16)

    unit = lambda a, u: a[u * ur:(u + 1) * ur]

    def own_half(pv, u):
        heads = range(u * ur // QB, (u + 1) * ur // QB)
        return jnp.concatenate([pv[(r - heads[0]) * QB:(r - heads[0] + 1) * QB, (r % 2) * LANES:(r % 2 + 1) * LANES]
                                for r in heads], axis=0)

    def staggered(scores, probs, values):
        scores(0)
        for u in range(NSA_UNITS):
            if u + 1 < NSA_UNITS:
                scores(u + 1)
            probs(u)
            values(u)

    nwin = WINDOW + QB
    w0 = pl.multiple_of(jnp.maximum(qi - WINDOW // QB, 0) * QB, QB)
    dist_w = rel_tile(nwin, qi * QB - w0)
    bias_w = jnp.where((dist_w >= 0) & (dist_w < WINDOW), 0.0, NEG)
    o_w = [None] * NSA_UNITS

    def win_scores(u):
        s_w[u][...] = _dot_nt(unit(q4b, u), kwa[pl.ds(w0, nwin), :])

    def win_values(u):
        o_w[u] = own_half(_dot(e_w[u][...], vwa[pl.ds(w0, nwin), :]), u)

    staggered(win_scores, lambda u: softmax_rows(s_w[u], e_w[u], None, None, nwin, bias_w, False), win_values)

    d0 = pl.multiple_of(qi * QB, QB)
    bias_d = jnp.where(rel_tile(QB, 0) >= 0, 0.0, NEG)

    def diag_scores(u):
        s_m[u][:, :QB] = _dot_nt(unit(q4b, u), ksa[pl.ds(d0, QB), :])

    def diag_values(u):
        acc[u][...] = own_half(_dot(e_m[u][:, :QB], vsa[pl.ds(d0, QB), :]), u)

    staggered(diag_scores, lambda u: softmax_rows(s_m[u], e_m[u], m_h[u], None, QB, bias_d, False), diag_values)

    def sel_body(kc, _):
        rows = pl.ds(pl.multiple_of(kc * SEL_CHUNK, SEL_CHUNK), SEL_CHUNK)

        def scores(u):
            s_m[u][...] = _dot_nt(q_scr[u * ur:(u + 1) * ur, :], ksa[rows, :])

        def values(u):
            acc[u][...] = al_h[u][...] * acc[u][...] + own_half(_dot(e_m[u][...], vsa[rows, :]), u)

        staggered(scores, lambda u: softmax_rows(s_m[u], e_m[u], m_h[u], al_h[u], SEL_CHUNK, None, True), values)
        return 0

    per = SEL_CHUNK // QB
    lax.fori_loop(0, (qi + per - 1) // per, sel_body, 0)

    acc_s = jnp.concatenate([a[...] for a in acc], axis=0)
    acc_w = jnp.concatenate(o_w, axis=0)
    for pr in range(GQA_REP // 2):
        ev, od = 2 * pr, 2 * pr + 1
        out = gate_b[:, (3 * pr) * LANES:(3 * pr + 1) * LANES] * jnp.where(left, head(o_cmp4, ev), head(o_cmp4, od))
        for j, a in ((1, acc_s), (2, acc_w)):
            num = jnp.where(left, head(a, ev), head(a, od))
            den = pltpu.roll(jnp.where(left, head(a, od), head(a, ev)), HEAD_DIM, axis=1)
            out = out + gate_b[:, (3 * pr + j) * LANES:(3 * pr + j + 1) * LANES] / den * num
        o_ref[:, pr * LANES:(pr + 1) * LANES] = out


def _nsa(proj3, kcmp, vcmp, slopes_b, ovt, kaug, caug, gsel):
    B, S, _ = proj3.shape
    ncb = kcmp.shape[1]
    nrow = GQA_REP * QB
    ur = nrow // NSA_UNITS
    nwin = WINDOW + QB
    kv = lambda j: pl.BlockSpec((None, S, LANES), lambda b, g, qi, j=j: (b, 0, COL_KC + j))
    full = lambda a: pl.BlockSpec(a.shape, lambda b, g, qi: (0,) * a.ndim)
    return pl.pallas_call(
        functools.partial(_nsa_kernel, seq=S),
        grid=(B, N_KV_B, S // QB),
        in_specs=[pl.BlockSpec(memory_space=pltpu.SMEM),
                  pl.BlockSpec((None, QB, 2 * LANES), lambda b, g, qi: (b, qi, COL_QB + g)),
                  kv(2), kv(3), kv(4), kv(5),
                  pl.BlockSpec((None, QB, LANES), lambda b, g, qi: (b, qi, COL_GATE + g)),
                  pl.BlockSpec((None, ncb, LANES), lambda b, g, qi: (b, 0, 0)),
                  pl.BlockSpec((None, ncb, LANES), lambda b, g, qi: (b, 0, 0)),
                  full(ovt), full(kaug), full(caug), full(gsel)],
        out_specs=pl.BlockSpec((None, QB, 2 * LANES), lambda b, g, qi: (b, qi, g)),
        out_shape=jax.ShapeDtypeStruct((B, S, N_HEADS_B * HEAD_DIM), f32),
        scratch_shapes=([pltpu.VMEM((S, LANES), bf16), pltpu.VMEM((S, 2 * LANES), bf16)] * 2
                        + [pltpu.VMEM((ncb, LANES), bf16)] * 2
                        + [pltpu.VMEM((nrow, LANES), bf16), pltpu.VMEM((QB, LANES), f32)]
                        + [pltpu.VMEM((ur, nwin), f32)] * NSA_UNITS + [pltpu.VMEM((ur, nwin), bf16)] * NSA_UNITS
                        + [pltpu.VMEM((ur, SEL_CHUNK), f32)] * NSA_UNITS
                        + [pltpu.VMEM((ur, SEL_CHUNK), bf16)] * NSA_UNITS
                        + [pltpu.VMEM((ur, LANES), f32)] * (2 * NSA_UNITS)
                        + [pltpu.VMEM((ur, LANES), f32)] * NSA_UNITS),
        compiler_params=_params("parallel", "parallel", "arbitrary"),
        name="nsa_attn",
    )(slopes_b, proj3, proj3, proj3, proj3, proj3, proj3, kcmp, vcmp, ovt, kaug, caug, gsel)


def _out_proj_kernel(x_ref, oa_ref, ob_ref, wa_ref, wb_ref, o_ref):
    o_ref[...] = (x_ref[...] + _dot(oa_ref[...].astype(bf16), wa_ref[...])
                  + _dot(ob_ref[...].astype(bf16), wb_ref[...]))


def _out_proj(x2d, oa, ob, wa, wb):
    T = x2d.shape[0]
    tm = 512
    row = lambda w: pl.BlockSpec((tm, w), lambda i: (i, 0))
    full = lambda a: pl.BlockSpec(a.shape, lambda i: (0,) * a.ndim)
    return pl.pallas_call(
        _out_proj_kernel,
        grid=(T // tm,),
        in_specs=[row(D_MODEL), row(oa.shape[1]), row(ob.shape[1]), full(wa), full(wb)],
        out_specs=row(D_MODEL),
        out_shape=jax.ShapeDtypeStruct((T, D_MODEL), f32),
        compiler_params=_params("parallel"),
        name="out_proj",
    )(x2d, oa, ob, wa, wb)


def _route(h, w_ref, b_ref):
    w = w_ref[...]
    h_hi, w_hi = h.astype(bf16), w.astype(bf16)
    h_lo, w_lo = (h - h_hi.astype(f32)).astype(bf16), (w - w_hi.astype(f32)).astype(bf16)
    logit = _dot(h_hi, w_hi) + (_dot(h_hi, w_lo) + _dot(h_lo, w_hi)) + b_ref[...]
    tm = logit.shape[0]
    lane = lax.broadcasted_iota(jnp.int32, (tm, LANES), 1)
    big = jnp.int32(LANES)
    is_g = lane < N_GROUPS
    gl = jnp.where(is_g, logit, NEG)
    gmax = jnp.max(gl, axis=1, keepdims=True)
    gsum = jnp.sum(jnp.where(is_g, jnp.exp(gl - gmax), 0.0), axis=1, keepdims=True)
    gsel = jnp.min(jnp.where(is_g & (gl == gmax), lane, big), axis=1, keepdims=True)
    gw = 1.0 / gsum
    e_lane = lane - N_GROUPS
    in_grp = (e_lane >= 0) & (e_lane < N_EXPERTS) & (e_lane // EXPERTS_PER_GROUP == gsel)
    el = jnp.where(in_grp, logit, NEG)
    t1 = jnp.max(el, axis=1, keepdims=True)
    i1 = jnp.min(jnp.where(in_grp & (el == t1), lane, big), axis=1, keepdims=True)
    el2 = jnp.where(lane == i1, NEG, el)
    t2 = jnp.max(el2, axis=1, keepdims=True)
    i2 = jnp.min(jnp.where(in_grp & (lane != i1) & (el2 == t2), lane, big), axis=1, keepdims=True)
    e2 = jnp.exp(t2 - t1)
    w1 = gw / (1.0 + e2)
    w2 = gw * e2 / (1.0 + e2)
    return jnp.where(lane == i1, w1, jnp.where(lane == i2, w2, 0.0)), gsel


def _to_token_tiles(ref, x):
    for s in range(D_MODEL // LANES):
        ref[pl.ds(s, x.shape[0], stride=D_MODEL // LANES), :] = x[:, s * LANES:(s + 1) * LANES]


def _from_token_tiles(ref):
    n = D_MODEL // LANES
    return jnp.concatenate([ref[pl.ds(s, ref.shape[0] // n, stride=n), :] for s in range(n)], axis=1)


def _router_kernel(x_ref, g_ref, w_ref, b_ref, tri_ref, stage_ref, meta_ref, cnt_ref, cnt_scr):
    @pl.when(pl.program_id(0) == 0)
    def _zero():
        cnt_scr[...] = jnp.zeros_like(cnt_scr)

    h = _rms(x_ref[...], g_ref[...])
    _to_token_tiles(stage_ref, h)
    _, gsel = _route(h, w_ref, b_ref)
    tm = h.shape[0]
    lane = lax.broadcasted_iota(jnp.int32, (tm, LANES), 1)
    is_g = lane < N_GROUPS

    onehot = jnp.where(is_g & (lane == gsel), 1.0, 0.0)
    before = _dot(tri_ref[...], onehot.astype(bf16)) + cnt_scr[...]
    rank = jnp.sum(onehot * before, axis=1, keepdims=True)
    cnt_scr[...] = before[tm - 1:tm, :] + onehot[tm - 1:tm, :]
    cnt_ref[...] = cnt_scr[...]
    rank_hi = jnp.floor(rank * (1.0 / RANK_SPLIT))
    cols = jnp.where(lane == 0, gsel.astype(f32), jnp.where(lane == 1, rank_hi,
                                                            jnp.where(lane == 2, rank - RANK_SPLIT * rank_hi, 0.0)))
    pick = (lax.broadcasted_iota(jnp.int32, (8, LANES), 0) == lax.broadcasted_iota(jnp.int32, (8, LANES), 1))
    meta_ref[...] = _dot_nt(jnp.where(pick, 1.0, 0.0).astype(bf16), cols.astype(bf16))


def _router(x2d, g, w, b):
    T = x2d.shape[0]
    tm = MOE_TILE
    assert T // RANK_SPLIT <= 256
    tri = jnp.asarray(np.tril(np.ones((tm, tm), np.float32), -1), bf16)
    return pl.pallas_call(
        _router_kernel,
        grid=(T // tm,),
        in_specs=[pl.BlockSpec((tm, D_MODEL), lambda i: (i, 0)),
                  pl.BlockSpec((1, D_MODEL), lambda i: (0, 0)),
                  pl.BlockSpec((D_MODEL, LANES), lambda i: (0, 0)),
                  pl.BlockSpec((1, LANES), lambda i: (0, 0)),
                  pl.BlockSpec((tm, tm), lambda i: (0, 0))],
        out_specs=[pl.BlockSpec((tm * TOKEN_ROWS, LANES), lambda i: (i, 0)),
                   pl.BlockSpec((None, 8, tm), lambda i: (i, 0, 0)),
                   pl.BlockSpec((1, LANES), lambda i: (0, 0))],
        out_shape=[jax.ShapeDtypeStruct((T * TOKEN_ROWS, LANES), f32),
                   jax.ShapeDtypeStruct((T // tm, 8, tm), f32),
                   jax.ShapeDtypeStruct((1, LANES), f32)],
        scratch_shapes=[pltpu.VMEM((1, LANES), f32)],
        compiler_params=_params("arbitrary"),
        name="router",
    )(x2d, g, w, b, tri)


def _token_rows(t, n=1):
    return pl.ds(pl.multiple_of(t * TOKEN_ROWS, TOKEN_ROWS), n * TOKEN_ROWS)


def _gathered_tile(idx_ref, src_ref, buf_ref, sems, tm):
    i = pl.program_id(0)
    n = pl.num_programs(0)

    def issue(step):
        slot = step % 2

        def one(k, _):
            pltpu.make_async_copy(src_ref.at[_token_rows(idx_ref[step * tm + k])],
                                  buf_ref.at[slot, _token_rows(k)], sems.at[slot]).start()
            return 0

        lax.fori_loop(0, tm, one, 0, unroll=8)

    @pl.when(i == 0)
    def _first():
        issue(i)

    @pl.when(i + 1 < n)
    def _next():
        issue(i + 1)

    slot = i % 2
    pltpu.make_async_copy(src_ref.at[_token_rows(0, tm)], buf_ref.at[slot], sems.at[slot]).wait()
    return slot


def _moe_kernel(tg_ref, src_ref, stage_ref, wr_ref, br_ref, wg_ref, wu_ref, wd_ref, o_ref, buf_ref, sems):
    grp = tg_ref[pl.program_id(0)]
    slot = _gathered_tile(src_ref, stage_ref, buf_ref, sems, MOE_TILE)
    h = _from_token_tiles(buf_ref.at[slot])
    gates, _ = _route(h, wr_ref, br_ref)
    x = h.astype(bf16)
    lane = lax.broadcasted_iota(jnp.int32, gates.shape, 1)
    y = jnp.zeros(h.shape, f32)
    for e in range(EXPERTS_PER_GROUP):
        gcol = jnp.sum(jnp.where(lane == N_GROUPS + EXPERTS_PER_GROUP * grp + e, gates, 0.0), axis=1, keepdims=True)
        a = jax.nn.silu(_dot(x, wg_ref[e])) * _dot(x, wu_ref[e])
        y = y + _dot((a * gcol).astype(bf16), wd_ref[e])
    _to_token_tiles(o_ref, y)


def _moe(stage, tile_group, slot_token, w_route, b_route, wg, wu, wd, layer):
    tm = MOE_TILE
    n_slots = slot_token.shape[0]
    w_spec = lambda k, n: pl.BlockSpec((None, None, EXPERTS_PER_GROUP, k, n),
                                       lambda j, tg, src: (layer, tg[j], 0, 0, 0))
    grouped = lambda w: w.reshape(w.shape[0], N_GROUPS, EXPERTS_PER_GROUP, *w.shape[2:])
    return pl.pallas_call(
        _moe_kernel,
        grid_spec=pltpu.PrefetchScalarGridSpec(
            num_scalar_prefetch=2, grid=(n_slots // tm,),
            in_specs=[pl.BlockSpec(memory_space=pl.ANY),
                      pl.BlockSpec((D_MODEL, LANES), lambda j, tg, src: (0, 0)),
                      pl.BlockSpec((1, LANES), lambda j, tg, src: (0, 0)),
                      w_spec(D_MODEL, D_EXPERT), w_spec(D_MODEL, D_EXPERT), w_spec(D_EXPERT, D_MODEL)],
            out_specs=pl.BlockSpec((tm * TOKEN_ROWS, LANES), lambda j, tg, src: (j, 0)),
            scratch_shapes=[pltpu.VMEM((2, tm * TOKEN_ROWS, LANES), f32), pltpu.SemaphoreType.DMA((2,))]),
        out_shape=jax.ShapeDtypeStruct((n_slots * TOKEN_ROWS, LANES), f32),
        compiler_params=_params("arbitrary"),
        name="moe_ffn",
    )(tile_group, slot_token, stage, w_route, b_route, grouped(wg), grouped(wu), grouped(wd))


def _ple_kernel(pos_ref, x_ref, ys_ref, p_ref, g_ref, wg_ref, wp_ref, fg_ref, o_ref, buf_ref, sems, *, final):
    slot = _gathered_tile(pos_ref, ys_ref, buf_ref, sems, x_ref.shape[0])
    x = x_ref[...] + _from_token_tiles(buf_ref.at[slot])
    gate = jax.nn.sigmoid(_dot(_rms(x, g_ref[...]).astype(bf16), wg_ref[...]))
    y = x + gate * _dot(p_ref[...].astype(bf16), wp_ref[...])
    o_ref[...] = _rms(y, fg_ref[...]) if final else y


def _ple(x2d, ys, pos, p3, g, wg, wp, fg, layer, final):
    T = x2d.shape[0]
    tm = 512
    full = lambda a: pl.BlockSpec(a.shape, lambda i, pos: (0,) * a.ndim)
    return pl.pallas_call(
        functools.partial(_ple_kernel, final=final),
        grid_spec=pltpu.PrefetchScalarGridSpec(
            num_scalar_prefetch=1, grid=(T // tm,),
            in_specs=[pl.BlockSpec((tm, D_MODEL), lambda i, pos: (i, 0)),
                      pl.BlockSpec(memory_space=pl.ANY),
                      pl.BlockSpec((None, tm, PLE_DIM), lambda i, pos: (layer, i, 0)),
                      full(g), full(wg), full(wp), full(fg)],
            out_specs=pl.BlockSpec((tm, D_MODEL), lambda i, pos: (i, 0)),
            scratch_shapes=[pltpu.VMEM((2, tm * TOKEN_ROWS, LANES), f32), pltpu.SemaphoreType.DMA((2,))]),
        out_shape=jax.ShapeDtypeStruct((T, D_MODEL), f32),
        compiler_params=_params("arbitrary"),
        name="ple",
    )(pos, x2d, ys, p3, g, wg, wp, fg)


def _alibi_slopes():
    s = 2.0 ** (-8.0 * np.arange(1, N_HEADS_TOTAL + 1) / N_HEADS_TOTAL)
    assert np.all(np.log2(s[1::2]) == np.round(np.log2(s[1::2])))
    return jnp.asarray(s[0::2], f32), jnp.asarray(s[1::2], f32)


def _selection_constants(seq):
    ncb = seq // CMP_STRIDE
    nsel = seq // SEL_BLOCK
    n_cmp = (seq - CMP_BLOCK) // CMP_STRIDE + 1
    cs = np.arange(n_cmp) * CMP_STRIDE
    bs = np.arange(nsel) * SEL_BLOCK
    ov = np.clip(np.minimum(cs[:, None] + CMP_BLOCK, bs[None, :] + SEL_BLOCK)
                 - np.maximum(cs[:, None], bs[None, :]), 0, None) / CMP_BLOCK
    assert ncb == LANES and seq <= POS_SPLIT * 256
    ovt = np.zeros((LANES, ncb), np.float32)
    ovt[:nsel, :n_cmp] = ov.T
    pos = np.arange(seq)
    kaug = np.zeros((seq, LANES), np.float32)
    kaug[:, POS_LANE0:POS_LANE0 + 2] = 1.0
    kaug[:, POS_LANE0 + 2] = pos // POS_SPLIT
    kaug[:, POS_LANE0 + 3] = pos % POS_SPLIT
    kaug[pos, SEL_LANE0 + pos // SEL_BLOCK] = 1.0
    cend = np.arange(ncb) * CMP_STRIDE + CMP_BLOCK - 1
    caug = np.zeros((ncb, LANES), np.float32)
    caug[:, POS_LANE0:POS_LANE0 + 2] = 1.0
    caug[:, POS_LANE0 + 2] = cend // POS_SPLIT
    caug[:, POS_LANE0 + 3] = cend % POS_SPLIT
    gsel = np.zeros((3, LANES, GQA_REP // 2, 3, 2, HEAD_DIM), np.float32)
    for pr in range(GQA_REP // 2):
        for j in range(3):
            for hh in range(2):
                gsel[:, 3 * (2 * pr + hh) + j, pr, j, hh, :] = 1.0
    gsel = gsel.reshape(3 * LANES, (GQA_REP // 2) * 3 * LANES)
    return jnp.asarray(ovt, bf16), jnp.asarray(kaug), jnp.asarray(caug), jnp.asarray(gsel, bf16)


def _dispatch_plan(meta, counts, n_tokens):
    i32 = jnp.int32
    n_slots = n_tokens + N_GROUPS * MOE_TILE
    group = meta[:, 0, :].reshape(n_tokens).astype(i32)
    rank = (meta[:, 1, :] * RANK_SPLIT + meta[:, 2, :]).reshape(n_tokens).astype(i32)
    cnt = counts[0, :N_GROUPS].astype(i32)
    padded = (cnt + MOE_TILE - 1) // MOE_TILE * MOE_TILE
    end = jnp.cumsum(padded)
    pos = (end - padded)[group] + rank
    slot_token = jnp.zeros((n_slots,), i32).at[pos].set(jnp.arange(n_tokens, dtype=i32))
    tile_start = jnp.arange(n_slots // MOE_TILE, dtype=i32) * MOE_TILE
    tile_group = jnp.minimum(jnp.sum(tile_start[:, None] >= end[None, :], axis=1), N_GROUPS - 1).astype(i32)
    return pos, slot_token, tile_group


def _block_diag2(w):
    z = jnp.zeros_like(w)
    return jnp.concatenate([jnp.concatenate([w, z], axis=-1), jnp.concatenate([z, w], axis=-1)], axis=-2)


def _layout_w_in(w):
    gate = w[:, N_MAIN:]
    per = GQA_REP * 3
    blocks = [jnp.pad(gate[:, g * per:(g + 1) * per], ((0, 0), (0, LANES - per))) for g in range(N_KV_B)]
    return jnp.concatenate([w[:, :N_MAIN]] + blocks, axis=1).astype(bf16)


def kernel(x, p, attn_norm, w_in, w_out, w_cmp_k1, w_cmp_k2, w_cmp_v1, w_cmp_v2, cmp_pos, ffn_norm, w_route_group, b_route_group, w_route_expert, b_route_expert, w_expert_gate, w_expert_up, w_expert_down, ple_norm, w_ple_gate, w_ple_proj, final_norm):
    B, S, D = x.shape
    depth = w_in.shape[0]
    T = B * S
    slopes_a, slopes_b = _alibi_slopes()
    ovt, kaug, caug, gsel = _selection_constants(S)
    wg_all = w_expert_gate.astype(bf16)
    wu_all = w_expert_up.astype(bf16)
    wd_all = w_expert_down.astype(bf16)
    p3 = p.reshape(depth, T, PLE_DIM)
    row = lambda v: v.reshape(1, -1)
    n_route = N_GROUPS + N_EXPERTS

    x2d = x.reshape(T, D)
    for i in range(depth):
        proj3 = _in_proj(x2d, row(attn_norm[i]), _layout_w_in(w_in[i])).reshape(B, S, N_PROJ)
        oa = _dilated(proj3, slopes_a)
        w1 = lambda w: _block_diag2(w.reshape(CMP_BLOCK, HEAD_DIM, CMP_HIDDEN)).astype(bf16)
        pos_dup = jnp.concatenate([cmp_pos[i], cmp_pos[i]], axis=-1)
        kcmp, vcmp = _compress(proj3, pos_dup, w1(w_cmp_k1[i]), _block_diag2(w_cmp_k2[i]).astype(bf16),
                               w1(w_cmp_v1[i]), _block_diag2(w_cmp_v2[i]).astype(bf16))
        ob = _nsa(proj3, kcmp, vcmp, slopes_b, ovt, kaug, caug, gsel)
        wo = w_out[i].astype(bf16)
        x2d = _out_proj(x2d, oa.reshape(T, A_W), ob.reshape(T, -1), wo[:A_W], wo[A_W:])
        w_route = jnp.pad(jnp.concatenate([w_route_group[i], w_route_expert[i]], axis=1),
                          ((0, 0), (0, LANES - n_route)))
        b_route = jnp.pad(jnp.concatenate([b_route_group[i], b_route_expert[i]]), (0, LANES - n_route))
        stage, meta, counts = _router(x2d, row(ffn_norm[i]), w_route, row(b_route))
        pos, slot_token, tile_group = _dispatch_plan(meta, counts, T)
        ys = _moe(stage, tile_group, slot_token, w_route, row(b_route), wg_all, wu_all, wd_all, i)
        x2d = _ple(x2d, ys, pos, p3, row(ple_norm[i]), w_ple_gate[i].astype(bf16),
                   w_ple_proj[i].astype(bf16), row(final_norm), i, i == depth - 1)
    return x2d.reshape(B, S, D)
```

```python
import functools

import numpy as np
import jax
import jax.numpy as jnp
from jax import lax
from jax.experimental import pallas as pl
from jax.experimental.pallas import tpu as pltpu

D_MODEL = 1024
PLE_DIM = 256
HEAD_DIM = 64
N_HEADS_A = 8
N_HEADS_B = 8
N_KV_B = 2
GQA_REP = N_HEADS_B // N_KV_B
N_HEADS_TOTAL = N_HEADS_A + N_HEADS_B
DILATED_PATTERNS = ((128, 1), (512, 4), (2048, 16))
CMP_BLOCK = 32
CMP_STRIDE = 16
CMP_HIDDEN = 256
SEL_BLOCK = 64
SEL_TOP = 16
N_LOCAL_BLOCKS = 2
WINDOW = 512
N_GROUPS = 4
EXPERTS_PER_GROUP = 4
N_EXPERTS = N_GROUPS * EXPERTS_PER_GROUP
D_EXPERT = 512
RMS_EPS = 1e-6
NEG = -1e30
FORCE_BONUS = 1e4
SCALE = HEAD_DIM ** -0.5

LANES = 128
QB = 128
SEL_CHUNK = 512
NSA_UNITS = 2
DIL_INFLIGHT = 3
POS_LANE0 = HEAD_DIM
POS_SPLIT = 16
SEL_LANE0 = POS_LANE0 + 4
SEL_LANES = 32
MASK_BIG = -(2.0 ** 100)
MOE_TILE = 512
TOKEN_ROWS = D_MODEL // LANES
RANK_SPLIT = 128
A_W = N_HEADS_A * HEAD_DIM
N_MAIN = 3 * A_W + N_HEADS_B * HEAD_DIM + 6 * N_KV_B * HEAD_DIM
N_PROJ = N_MAIN + N_KV_B * LANES
COL_QB = (3 * A_W) // (2 * LANES)
COL_KC = (3 * A_W + N_HEADS_B * HEAD_DIM) // LANES
COL_GATE = N_MAIN // LANES
VMEM_LIMIT = 56 * 1024 * 1024

f32 = jnp.float32
bf16 = jnp.bfloat16


def _dot(a, b):
    return jnp.dot(a, b, preferred_element_type=f32)


def _dot_nt(a, b):
    return lax.dot_general(a, b, (((1,), (1,)), ((), ())), preferred_element_type=f32)


def _rms(x, g):
    return x * lax.rsqrt(jnp.mean(x * x, axis=-1, keepdims=True) + RMS_EPS) * g


def _params(*sem):
    return pltpu.CompilerParams(dimension_semantics=sem, vmem_limit_bytes=VMEM_LIMIT)


def _in_proj_kernel(x_ref, g_ref, w_ref, o_ref):
    h = _rms(x_ref[...], g_ref[...]).astype(bf16)
    for n0 in range(0, N_PROJ, 512):
        o_ref[:, n0:n0 + 512] = _dot(h, w_ref[:, n0:n0 + 512])


def _in_proj(x2d, g, w):
    T = x2d.shape[0]
    tm = 512
    return pl.pallas_call(
        _in_proj_kernel,
        grid=(T // tm,),
        in_specs=[pl.BlockSpec((tm, D_MODEL), lambda i: (i, 0)),
                  pl.BlockSpec((1, D_MODEL), lambda i: (0, 0)),
                  pl.BlockSpec((D_MODEL, N_PROJ), lambda i: (0, 0))],
        out_specs=pl.BlockSpec((tm, N_PROJ), lambda i: (i, 0)),
        out_shape=jax.ShapeDtypeStruct((T, N_PROJ), f32),
        compiler_params=_params("parallel"),
        name="in_proj",
    )(x2d, g, w)


def _dil_kernel(slope_ref, q_ref, k_ref, v_ref, o_ref, out_ref, lse_ref, bias_scr, *bufs, seq):
    hp = pl.program_id(1)
    lane = lax.broadcasted_iota(jnp.int32, (QB, LANES), 1)
    left = lane < HEAD_DIM
    slab = 32
    s_bufs, e_bufs = bufs[:DIL_INFLIGHT], bufs[DIL_INFLIGHT:]

    i = lax.broadcasted_iota(jnp.int32, (QB, 2 * QB), 0)
    j = lax.broadcasted_iota(jnp.int32, (QB, 2 * QB), 1)
    rel = i - j + QB
    valid = (rel >= 0) & (rel <= QB)
    relf = rel.astype(f32)
    for p, (window, dil) in enumerate(DILATED_PATTERNS):
        assert window // dil == QB
        for hh in range(2):
            bias_scr[2 * p + hh] = jnp.where(valid, (-float(dil) * slope_ref[2 * hp + hh]) * relf, NEG)

    ones = jnp.ones((2 * QB, LANES), bf16)

    def scores(u, blk):
        p, dil, row0, key0, nk = blk
        qc = q_ref[pl.ds(row0, QB, stride=dil), :] * SCALE
        q2 = jnp.concatenate([jnp.where(left, qc, 0.0), jnp.where(left, 0.0, qc)], axis=0).astype(bf16)
        s_bufs[u][:, :nk] = _dot_nt(q2, k_ref[pl.ds(key0, nk, stride=dil), :].astype(bf16))

    def probs(u, blk):
        p, dil, row0, key0, nk = blk
        c0 = 2 * QB - nk
        ms = []
        for hh in range(2):
            parts = []
            for sl in range(QB // slab):
                rows = slice(hh * QB + sl * slab, hh * QB + (sl + 1) * slab)
                tiles = [s_bufs[u][rows, c:c + LANES]
                         + bias_scr[2 * p + hh, sl * slab:(sl + 1) * slab, c0 + c:c0 + c + LANES]
                         for c in range(0, nk, LANES)]
                mx = tiles[0]
                for t in tiles[1:]:
                    mx = jnp.maximum(mx, t)
                mx = jnp.broadcast_to(jnp.max(mx, axis=1, keepdims=True), (slab, LANES))
                for c, t in zip(range(0, nk, LANES), tiles):
                    e_bufs[u][rows, c:c + LANES] = jnp.exp(t - mx).astype(bf16)
                parts.append(mx)
            ms.append(jnp.concatenate(parts, axis=0))
        return jnp.where(left, ms[0], ms[1])

    def values(u, blk, row_max):
        p, dil, row0, key0, nk = blk
        v2 = jnp.concatenate([v_ref[pl.ds(key0, nk, stride=dil), :].astype(bf16), ones[:nk]], axis=1)
        res = _dot(e_bufs[u][:, :nk], v2)
        rows = pl.ds(row0, QB, stride=dil)
        den = jnp.where(left, res[:QB, LANES:], res[QB:, LANES:])
        out_ref[p, rows, :] = jnp.where(left, res[:QB, :LANES], res[QB:, :LANES]) / den
        lse_ref[p, rows, :] = row_max + jnp.log(den)

    def run(blocks):
        ahead = DIL_INFLIGHT - 1
        for n in range(min(ahead, len(blocks))):
            scores(n % DIL_INFLIGHT, blocks[n])
        for n, blk in enumerate(blocks):
            if n + ahead < len(blocks):
                scores((n + ahead) % DIL_INFLIGHT, blocks[n + ahead])
            values(n % DIL_INFLIGHT, blk, probs(n % DIL_INFLIGHT, blk))

    def first(p, dil, r):
        return (p, dil, r, r, QB)

    def later(p, dil, r, a):
        return (p, dil, r + dil * QB * a, r + dil * QB * (a - 1), 2 * QB)

    for p, (window, dil) in enumerate(DILATED_PATTERNS):
        nblk = seq // dil // QB
        if dil == 1:
            run([first(p, dil, 0)])

            def trip(t, _, p=p, dil=dil):
                run([later(p, dil, 0, 1 + 3 * t + n) for n in range(3)])
                return 0

            assert (nblk - 1) % 3 == 0
            lax.fori_loop(0, (nblk - 1) // 3, trip, 0)
        elif nblk > 1:
            def per_class(r, _, p=p, dil=dil, nblk=nblk):
                run([first(p, dil, r)] + [later(p, dil, r, a) for a in range(1, nblk)])
                return 0

            lax.fori_loop(0, dil, per_class, 0)
        else:
            def some_classes(t, _, p=p, dil=dil):
                run([first(p, dil, 8 * t + n) for n in range(8)])
                return 0

            lax.fori_loop(0, dil // 8, some_classes, 0)

    ch = 256

    def combine(c, _):
        rows = pl.ds(pl.multiple_of(c * ch, ch), ch)
        lses = [lse_ref[p, rows, :] for p in range(len(DILATED_PATTERNS))]
        big = functools.reduce(jnp.maximum, lses)
        num = jnp.zeros((ch, LANES), f32)
        den = jnp.zeros((ch, LANES), f32)
        for p, lse in enumerate(lses):
            w = jnp.exp(lse - big)
            num = num + w * out_ref[p, rows, :]
            den = den + w
        o_ref[rows, :] = num / den
        return 0

    lax.fori_loop(0, seq // ch, combine, 0)


def _dilated(proj3, slopes_a):
    B, S, _ = proj3.shape
    npair = N_HEADS_A // 2
    blk = lambda off: pl.BlockSpec((None, S, LANES), lambda b, hp, off=off: (b, 0, off + hp))
    return pl.pallas_call(
        functools.partial(_dil_kernel, seq=S),
        grid=(B, npair),
        in_specs=[pl.BlockSpec(memory_space=pltpu.SMEM), blk(0), blk(npair), blk(2 * npair)],
        out_specs=pl.BlockSpec((None, S, LANES), lambda b, hp: (b, 0, hp)),
        out_shape=jax.ShapeDtypeStruct((B, S, A_W), f32),
        scratch_shapes=([pltpu.VMEM((len(DILATED_PATTERNS), S, LANES), f32)] * 2
                        + [pltpu.VMEM((2 * len(DILATED_PATTERNS), QB, 2 * QB), f32)]
                        + [pltpu.VMEM((2 * QB, 2 * QB), f32)] * DIL_INFLIGHT
                        + [pltpu.VMEM((2 * QB, 2 * QB), bf16)] * DIL_INFLIGHT),
        compiler_params=_params("parallel", "parallel"),
        name="dilated_attn",
    )(slopes_a, proj3, proj3, proj3)


def _cmp_kernel(kc_ref, vc_ref, pos_ref, w1k_ref, w2k_ref, w1v_ref, w2v_ref, ko_ref, vo_ref, *, ncb):
    half = CMP_BLOCK // 2
    for x_ref, pi, w1_ref, w2_ref, o_ref in ((kc_ref, 0, w1k_ref, w2k_ref, ko_ref),
                                             (vc_ref, 1, w1v_ref, w2v_ref, vo_ref)):
        lo = jnp.zeros((ncb, 2 * CMP_HIDDEN), f32)
        hi = jnp.zeros((ncb, 2 * CMP_HIDDEN), f32)
        for r in range(half):
            x = x_ref[pl.ds(r, ncb, stride=CMP_STRIDE), :]
            lo = lo + _dot((x + pos_ref[pi, r:r + 1, :]).astype(bf16), w1_ref[r])
            hi = hi + _dot((x + pos_ref[pi, r + half:r + half + 1, :]).astype(bf16), w1_ref[r + half])
        h1 = lo + pltpu.roll(hi, ncb - 1, axis=0)
        o_ref[...] = _dot(jax.nn.gelu(h1).astype(bf16), w2_ref[...])


def _compress(proj3, pos_dup, w1k, w2k, w1v, w2v):
    B, S, _ = proj3.shape
    ncb = S // CMP_STRIDE
    full = lambda a: pl.BlockSpec(a.shape, lambda b: (0,) * a.ndim)
    out = jax.ShapeDtypeStruct((B, ncb, LANES), f32)
    return pl.pallas_call(
        functools.partial(_cmp_kernel, ncb=ncb),
        grid=(B,),
        in_specs=[pl.BlockSpec((None, S, LANES), lambda b: (b, 0, COL_KC)),
                  pl.BlockSpec((None, S, LANES), lambda b: (b, 0, COL_KC + 1)),
                  full(pos_dup), full(w1k), full(w2k), full(w1v), full(w2v)],
        out_specs=[pl.BlockSpec((None, ncb, LANES), lambda b: (b, 0, 0))] * 2,
        out_shape=[out, out],
        compiler_params=_params("parallel"),
        name="nsa_compress",
    )(proj3, proj3, pos_dup, w1k, w2k, w1v, w2v)


def _nsa_kernel(slope_ref, q_ref, ks_ref, vs_ref, kw_ref, vw_ref, gl_ref, kcmp_ref, vcmp_ref,
                ovt_ref, kaug_ref, caug_ref, gsel_ref, o_ref,
                ksa, vsa, kwa, vwa, kca, vca, q_scr, sel_scr, s_m, *per_unit, seq):
    g = pl.program_id(1)
    qi = pl.program_id(2)
    nsel = seq // SEL_BLOCK
    n_top = min(SEL_TOP, nsel)
    assert nsel <= SEL_LANES and N_LOCAL_BLOCKS * SEL_BLOCK >= QB and n_top > N_LOCAL_BLOCKS
    slab = 32
    ur = GQA_REP * QB // NSA_UNITS
    s_w, e_w, s_d, e_d, e_m, m_h, al_h, acc = [per_unit[i * NSA_UNITS:(i + 1) * NSA_UNITS] for i in range(8)]

    def group_lanes(x):
        return jnp.where(g == 0, x, pltpu.roll(x, HEAD_DIM, axis=1))

    @pl.when(qi == 0)
    def _prep():
        ch = 256
        lane_c = lax.broadcasted_iota(jnp.int32, (ch, LANES), 1)
        is_k = lane_c < HEAD_DIM
        for c in range(seq // ch):
            sl = slice(c * ch, (c + 1) * ch)
            aug = kaug_ref[sl, :]
            ksa[sl, :] = jnp.where(is_k, group_lanes(ks_ref[sl, :]), aug).astype(bf16)
            kwa[sl, :] = jnp.where(is_k, group_lanes(kw_ref[sl, :]),
                                   jnp.where(lane_c < SEL_LANE0, aug, 0.0)).astype(bf16)
            for src, dst in ((vs_ref, vsa), (vw_ref, vwa)):
                v = group_lanes(src[sl, :])
                dst[sl, :LANES] = jnp.where(is_k, v, 1.0).astype(bf16)
                dst[sl, LANES:] = jnp.where(is_k, 1.0, pltpu.roll(v, HEAD_DIM, axis=1)).astype(bf16)
        lane_k = lax.broadcasted_iota(jnp.int32, kca.shape, 1)
        kca[...] = jnp.where(lane_k < HEAD_DIM, group_lanes(kcmp_ref[...]), caug_ref[...]).astype(bf16)
        vc = group_lanes(vcmp_ref[...])
        vca[...] = jnp.where(lane_k < HEAD_DIM, vc, pltpu.roll(vc, HEAD_DIM, axis=1)).astype(bf16)

    lane = lax.broadcasted_iota(jnp.int32, (QB, LANES), 1)
    left = lane < HEAD_DIM
    ii = lax.broadcasted_iota(jnp.int32, (QB, LANES), 0)
    t_row = qi * QB + ii
    t_hi = (t_row // POS_SPLIT).astype(f32)
    t_lo = (t_row % POS_SPLIT).astype(f32)
    slopes = [slope_ref[g * GQA_REP + r] for r in range(GQA_REP)]
    head = lambda a, r: a[r * QB:(r + 1) * QB]

    def q_head(r):
        x = q_ref[:, (r // 2) * LANES:(r // 2 + 1) * LANES] * SCALE
        if r % 2:
            x = pltpu.roll(x, HEAD_DIM, axis=1)
        m = slopes[r]
        pos = jnp.where(lane == POS_LANE0, (-POS_SPLIT * m) * t_hi,
                        jnp.where(lane == POS_LANE0 + 1, (-m) * t_lo,
                                  jnp.where(lane == POS_LANE0 + 2, POS_SPLIT * m,
                                            jnp.where(lane == POS_LANE0 + 3, m, 0.0))))
        return jnp.where(left, x, pos)

    q4 = jnp.concatenate([q_head(r) for r in range(GQA_REP)], axis=0)
    q4b = q4.astype(bf16)

    def rel_tile(nk, offset):
        i = lax.broadcasted_iota(jnp.int32, (QB, nk), 0)
        j = lax.broadcasted_iota(jnp.int32, (QB, nk), 1)
        return (i - j + offset).astype(f32)

    def softmax_rows(s_ref, e_ref, m_ref, al_ref, nk, bias, running):
        for sl in range(ur // slab):
            rows = slice(sl * slab, (sl + 1) * slab)
            cols = [slice(j * LANES, (j + 1) * LANES) for j in range(nk // LANES)]
            i0 = (sl * slab) % QB
            tiles = [s_ref[rows, c] if bias is None else s_ref[rows, c] + bias[i0:i0 + slab, c] for c in cols]
            mx = tiles[0]
            for t in tiles[1:]:
                mx = jnp.maximum(mx, t)
            m_new = jnp.broadcast_to(jnp.max(mx, axis=1, keepdims=True), (slab, LANES))
            if running:
                m_old = m_ref[rows, :]
                m_new = jnp.maximum(m_old, m_new)
                al_ref[rows, :] = jnp.exp(m_old - m_new)
            if m_ref is not None:
                m_ref[rows, :] = m_new
            for c, t in zip(cols, tiles):
                e_ref[rows, c] = jnp.exp(t - m_new).astype(bf16)

    unit = lambda a, u: a[u * ur:(u + 1) * ur]

    def own_half(pv, u):
        heads = range(u * ur // QB, (u + 1) * ur // QB)
        return jnp.concatenate([pv[(r - heads[0]) * QB:(r - heads[0] + 1) * QB, (r % 2) * LANES:(r % 2 + 1) * LANES]
                                for r in heads], axis=0)

    s4 = _dot_nt(q4b, kca[...])

    sig = jax.nn.sigmoid(gl_ref[...])
    g_hi = sig.astype(bf16)
    g_mid = (sig - g_hi.astype(f32)).astype(bf16)
    g_lo = (sig - g_hi.astype(f32) - g_mid.astype(f32)).astype(bf16)
    gate_b = _dot(jnp.concatenate([g_hi, g_mid, g_lo], axis=1), gsel_ref[...])

    nwin = WINDOW + QB
    w0 = pl.multiple_of(jnp.maximum(qi - WINDOW // QB, 0) * QB, QB)
    dist_w = rel_tile(nwin, qi * QB - w0)
    bias_w = jnp.where((dist_w >= 0) & (dist_w < WINDOW), 0.0, NEG)
    for u in range(NSA_UNITS):
        s_w[u][...] = _dot_nt(unit(q4b, u), kwa[pl.ds(w0, nwin), :])

    d0 = pl.multiple_of(qi * QB, QB)
    bias_d = jnp.where(rel_tile(QB, 0) >= 0, 0.0, NEG)
    for u in range(NSA_UNITS):
        s_d[u][...] = _dot_nt(unit(q4b, u), ksa[pl.ds(d0, QB), :])

    cmp_end = (lane * CMP_STRIDE + (CMP_BLOCK - 1))
    valid_c = t_row >= cmp_end
    ps = []
    p_sum = jnp.zeros((QB, LANES), f32)
    row_bcast = lambda col: jnp.broadcast_to(col, (QB, LANES))
    for r in range(GQA_REP):
        s = jnp.where(valid_c, head(s4, r), NEG)
        e = jnp.exp(s - row_bcast(jnp.max(s, axis=1, keepdims=True)))
        p = jnp.where(valid_c, e, 0.0) / row_bcast(jnp.sum(e, axis=1, keepdims=True))
        p_sum = p_sum + p
        ps.append(p.astype(bf16))
    o_cmp4 = _dot(jnp.concatenate(ps, axis=0), vca[...])

    o_w = []
    for u in range(NSA_UNITS):
        softmax_rows(s_w[u], e_w[u], None, None, nwin, bias_w, False)
        o_w.append(own_half(_dot(e_w[u][...], vwa[pl.ds(w0, nwin), :]), u))
    for u in range(NSA_UNITS):
        softmax_rows(s_d[u], e_d[u], m_h[u], None, QB, bias_d, False)
        acc[u][...] = own_half(_dot(e_d[u][...], vsa[pl.ds(d0, QB), :]), u)

    back = t_row // SEL_BLOCK - lane
    valid_s = (back >= 0) & (lane < nsel)

    @pl.when(2 * qi + 2 <= n_top)
    def _all_valid():
        sel_scr[...] = jnp.where(valid_s, 1.0, 0.0)

    @pl.when(2 * qi + 2 > n_top)
    def _top_k():
        p_hi = p_sum.astype(bf16)
        p_lo = (p_sum - p_hi.astype(f32)).astype(bf16)
        imp_t = (_dot_nt(ovt_ref[...], p_hi) + _dot_nt(ovt_ref[...], p_lo))[:SEL_LANES]
        blk = lax.broadcasted_iota(jnp.int32, (SEL_LANES, QB), 0)
        tq = qi * QB + lax.broadcasted_iota(jnp.int32, (SEL_LANES, QB), 1)
        back_t = tq // SEL_BLOCK - blk
        valid_t = (back_t >= 0) & (blk < nsel)
        forced = (blk == 0) | (valid_t & (back_t < N_LOCAL_BLOCKS))
        score = jnp.where(valid_t, imp_t + jnp.where(forced, FORCE_BONUS, 0.0), NEG)
        score = jnp.where(blk < nsel, score, 2.0 * NEG)
        rank = jnp.zeros((SEL_LANES, QB), jnp.int32)
        for n in range(nsel):
            row = score[n:n + 1, :]
            ahead = (row > score) | ((row == score) & (blk > n))
            rank = rank + ahead.astype(jnp.int32)
        sel_t = jnp.where((rank < n_top) & valid_t, 1.0, 0.0)
        sel_t = jnp.concatenate([sel_t, jnp.zeros((LANES - SEL_LANES, QB), f32)], axis=0)
        sel_scr[...] = sel_t.T

    sel_bias = jnp.where((sel_scr[...] > 0.5) & (lane < 2 * qi), 0.0, MASK_BIG)
    sel_bias = pltpu.roll(sel_bias, SEL_LANE0, axis=1)
    in_sel = (lane >= SEL_LANE0) & (lane < SEL_LANE0 + SEL_LANES)
    for r in range(GQA_REP):
        q_scr[r * QB:(r + 1) * QB, :] = jnp.where(in_sel, sel_bias, head(q4, r)).astype(bf16)

    per = SEL_CHUNK // QB
    n_chunks = (qi + per - 1) // per

    def sel_scores(kc, half):
        rows = pl.ds(pl.multiple_of(kc * SEL_CHUNK, SEL_CHUNK), SEL_CHUNK)
        s_m[half] = _dot_nt(q_scr[...], ksa[rows, :])

    sel_scores(0, 0)

    def sel_chunk(kc, half):
        sel_scores(jnp.minimum(kc + 1, n_chunks - 1), 1 - half)
        rows = pl.ds(pl.multiple_of(kc * SEL_CHUNK, SEL_CHUNK), SEL_CHUNK)
        for u in range(NSA_UNITS):
            softmax_rows(s_m.at[half, pl.ds(u * ur, ur)], e_m[u], m_h[u], al_h[u], SEL_CHUNK, None, True)
            acc[u][...] = al_h[u][...] * acc[u][...] + own_half(_dot(e_m[u][...], vsa[rows, :]), u)

    def sel_body(t, _):
        sel_chunk(2 * t, 0)

        @pl.when(2 * t + 1 < n_chunks)
        def _odd():
            sel_chunk(2 * t + 1, 1)
        return 0

    lax.fori_loop(0, (n_chunks + 1) // 2, sel_body, 0)

    acc_s = jnp.concatenate([a[...] for a in acc], axis=0)
    acc_w = jnp.concatenate(o_w, axis=0)
    for pr in range(GQA_REP // 2):
        ev, od = 2 * pr, 2 * pr + 1
        out = gate_b[:, (3 * pr) * LANES:(3 * pr + 1) * LANES] * jnp.where(left, head(o_cmp4, ev), head(o_cmp4, od))
        for j, a in ((1, acc_s), (2, acc_w)):
            num = jnp.where(left, head(a, ev), head(a, od))
            den = pltpu.roll(jnp.where(left, head(a, od), head(a, ev)), HEAD_DIM, axis=1)
            out = out + gate_b[:, (3 * pr + j) * LANES:(3 * pr + j + 1) * LANES] / den * num
        o_ref[:, pr * LANES:(pr + 1) * LANES] = out


def _nsa(proj3, kcmp, vcmp, slopes_b, ovt, kaug, caug, gsel):
    B, S, _ = proj3.shape
    ncb = kcmp.shape[1]
    nrow = GQA_REP * QB
    ur = nrow // NSA_UNITS
    nwin = WINDOW + QB
    kv = lambda j: pl.BlockSpec((None, S, LANES), lambda b, g, qi, j=j: (b, 0, COL_KC + j))
    full = lambda a: pl.BlockSpec(a.shape, lambda b, g, qi: (0,) * a.ndim)
    return pl.pallas_call(
        functools.partial(_nsa_kernel, seq=S),
        grid=(B, N_KV_B, S // QB),
        in_specs=[pl.BlockSpec(memory_space=pltpu.SMEM),
                  pl.BlockSpec((None, QB, 2 * LANES), lambda b, g, qi: (b, qi, COL_QB + g)),
                  kv(2), kv(3), kv(4), kv(5),
                  pl.BlockSpec((None, QB, LANES), lambda b, g, qi: (b, qi, COL_GATE + g)),
                  pl.BlockSpec((None, ncb, LANES), lambda b, g, qi: (b, 0, 0)),
                  pl.BlockSpec((None, ncb, LANES), lambda b, g, qi: (b, 0, 0)),
                  full(ovt), full(kaug), full(caug), full(gsel)],
        out_specs=pl.BlockSpec((None, QB, 2 * LANES), lambda b, g, qi: (b, qi, g)),
        out_shape=jax.ShapeDtypeStruct((B, S, N_HEADS_B * HEAD_DIM), f32),
        scratch_shapes=([pltpu.VMEM((S, LANES), bf16), pltpu.VMEM((S, 2 * LANES), bf16)] * 2
                        + [pltpu.VMEM((ncb, LANES), bf16)] * 2
                        + [pltpu.VMEM((nrow, LANES), bf16), pltpu.VMEM((QB, LANES), f32),
                           pltpu.VMEM((2, nrow, SEL_CHUNK), f32)]
                        + [pltpu.VMEM((ur, nwin), f32)] * NSA_UNITS + [pltpu.VMEM((ur, nwin), bf16)] * NSA_UNITS
                        + [pltpu.VMEM((ur, QB), f32)] * NSA_UNITS + [pltpu.VMEM((ur, QB), bf16)] * NSA_UNITS
                        + [pltpu.VMEM((ur, SEL_CHUNK), bf16)] * NSA_UNITS
                        + [pltpu.VMEM((ur, LANES), f32)] * (2 * NSA_UNITS)
                        + [pltpu.VMEM((ur, LANES), f32)] * NSA_UNITS),
        compiler_params=_params("parallel", "parallel", "arbitrary"),
        name="nsa_attn",
    )(slopes_b, proj3, proj3, proj3, proj3, proj3, proj3, kcmp, vcmp, ovt, kaug, caug, gsel)


def _out_proj_kernel(x_ref, oa_ref, ob_ref, wa_ref, wb_ref, o_ref):
    o_ref[...] = (x_ref[...] + _dot(oa_ref[...].astype(bf16), wa_ref[...])
                  + _dot(ob_ref[...].astype(bf16), wb_ref[...]))


def _out_proj(x2d, oa, ob, wa, wb):
    T = x2d.shape[0]
    tm = 512
    row = lambda w: pl.BlockSpec((tm, w), lambda i: (i, 0))
    full = lambda a: pl.BlockSpec(a.shape, lambda i: (0,) * a.ndim)
    return pl.pallas_call(
        _out_proj_kernel,
        grid=(T // tm,),
        in_specs=[row(D_MODEL), row(oa.shape[1]), row(ob.shape[1]), full(wa), full(wb)],
        out_specs=row(D_MODEL),
        out_shape=jax.ShapeDtypeStruct((T, D_MODEL), f32),
        compiler_params=_params("parallel"),
        name="out_proj",
    )(x2d, oa, ob, wa, wb)


def _route(h, w_ref, b_ref):
    w = w_ref[...]
    h_hi, w_hi = h.astype(bf16), w.astype(bf16)
    h_lo, w_lo = (h - h_hi.astype(f32)).astype(bf16), (w - w_hi.astype(f32)).astype(bf16)
    logit = _dot(h_hi, w_hi) + (_dot(h_hi, w_lo) + _dot(h_lo, w_hi)) + b_ref[...]
    tm = logit.shape[0]
    lane = lax.broadcasted_iota(jnp.int32, (tm, LANES), 1)
    big = jnp.int32(LANES)
    is_g = lane < N_GROUPS
    gl = jnp.where(is_g, logit, NEG)
    gmax = jnp.max(gl, axis=1, keepdims=True)
    gsum = jnp.sum(jnp.where(is_g, jnp.exp(gl - gmax), 0.0), axis=1, keepdims=True)
    gsel = jnp.min(jnp.where(is_g & (gl == gmax), lane, big), axis=1, keepdims=True)
    gw = 1.0 / gsum
    e_lane = lane - N_GROUPS
    in_grp = (e_lane >= 0) & (e_lane < N_EXPERTS) & (e_lane // EXPERTS_PER_GROUP == gsel)
    el = jnp.where(in_grp, logit, NEG)
    t1 = jnp.max(el, axis=1, keepdims=True)
    i1 = jnp.min(jnp.where(in_grp & (el == t1), lane, big), axis=1, keepdims=True)
    el2 = jnp.where(lane == i1, NEG, el)
    t2 = jnp.max(el2, axis=1, keepdims=True)
    i2 = jnp.min(jnp.where(in_grp & (lane != i1) & (el2 == t2), lane, big), axis=1, keepdims=True)
    e2 = jnp.exp(t2 - t1)
    w1 = gw / (1.0 + e2)
    w2 = gw * e2 / (1.0 + e2)
    return jnp.where(lane == i1, w1, jnp.where(lane == i2, w2, 0.0)), gsel


def _to_token_tiles(ref, x):
    for s in range(D_MODEL // LANES):
        ref[pl.ds(s, x.shape[0], stride=D_MODEL // LANES), :] = x[:, s * LANES:(s + 1) * LANES]


def _from_token_tiles(ref):
    n = D_MODEL // LANES
    return jnp.concatenate([ref[pl.ds(s, ref.shape[0] // n, stride=n), :] for s in range(n)], axis=1)


def _router_kernel(x_ref, g_ref, w_ref, b_ref, tri_ref, stage_ref, meta_ref, cnt_ref, cnt_scr):
    @pl.when(pl.program_id(0) == 0)
    def _zero():
        cnt_scr[...] = jnp.zeros_like(cnt_scr)

    h = _rms(x_ref[...], g_ref[...])
    _to_token_tiles(stage_ref, h)
    _, gsel = _route(h, w_ref, b_ref)
    tm = h.shape[0]
    lane = lax.broadcasted_iota(jnp.int32, (tm, LANES), 1)
    is_g = lane < N_GROUPS

    onehot = jnp.where(is_g & (lane == gsel), 1.0, 0.0)
    before = _dot(tri_ref[...], onehot.astype(bf16)) + cnt_scr[...]
    rank = jnp.sum(onehot * before, axis=1, keepdims=True)
    cnt_scr[...] = before[tm - 1:tm, :] + onehot[tm - 1:tm, :]
    cnt_ref[...] = cnt_scr[...]
    rank_hi = jnp.floor(rank * (1.0 / RANK_SPLIT))
    cols = jnp.where(lane == 0, gsel.astype(f32), jnp.where(lane == 1, rank_hi,
                                                            jnp.where(lane == 2, rank - RANK_SPLIT * rank_hi, 0.0)))
    pick = (lax.broadcasted_iota(jnp.int32, (8, LANES), 0) == lax.broadcasted_iota(jnp.int32, (8, LANES), 1))
    meta_ref[...] = _dot_nt(jnp.where(pick, 1.0, 0.0).astype(bf16), cols.astype(bf16))


def _router(x2d, g, w, b):
    T = x2d.shape[0]
    tm = MOE_TILE
    assert T // RANK_SPLIT <= 256
    tri = jnp.asarray(np.tril(np.ones((tm, tm), np.float32), -1), bf16)
    return pl.pallas_call(
        _router_kernel,
        grid=(T // tm,),
        in_specs=[pl.BlockSpec((tm, D_MODEL), lambda i: (i, 0)),
                  pl.BlockSpec((1, D_MODEL), lambda i: (0, 0)),
                  pl.BlockSpec((D_MODEL, LANES), lambda i: (0, 0)),
                  pl.BlockSpec((1, LANES), lambda i: (0, 0)),
                  pl.BlockSpec((tm, tm), lambda i: (0, 0))],
        out_specs=[pl.BlockSpec((tm * TOKEN_ROWS, LANES), lambda i: (i, 0)),
                   pl.BlockSpec((None, 8, tm), lambda i: (i, 0, 0)),
                   pl.BlockSpec((1, LANES), lambda i: (0, 0))],
        out_shape=[jax.ShapeDtypeStruct((T * TOKEN_ROWS, LANES), f32),
                   jax.ShapeDtypeStruct((T // tm, 8, tm), f32),
                   jax.ShapeDtypeStruct((1, LANES), f32)],
        scratch_shapes=[pltpu.VMEM((1, LANES), f32)],
        compiler_params=_params("arbitrary"),
        name="router",
    )(x2d, g, w, b, tri)


def _token_rows(t, n=1):
    return pl.ds(pl.multiple_of(t * TOKEN_ROWS, TOKEN_ROWS), n * TOKEN_ROWS)


def _gathered_tile(idx_ref, src_ref, buf_ref, sems, tm):
    i = pl.program_id(0)
    n = pl.num_programs(0)

    def issue(step):
        slot = step % 2

        def one(k, _):
            pltpu.make_async_copy(src_ref.at[_token_rows(idx_ref[step * tm + k])],
                                  buf_ref.at[slot, _token_rows(k)], sems.at[slot]).start()
            return 0

        lax.fori_loop(0, tm, one, 0, unroll=8)

    @pl.when(i == 0)
    def _first():
        issue(i)

    @pl.when(i + 1 < n)
    def _next():
        issue(i + 1)

    slot = i % 2
    pltpu.make_async_copy(src_ref.at[_token_rows(0, tm)], buf_ref.at[slot], sems.at[slot]).wait()
    return slot


def _moe_kernel(tg_ref, src_ref, stage_ref, wr_ref, br_ref, wg_ref, wu_ref, wd_ref, o_ref, buf_ref, sems):
    grp = tg_ref[pl.program_id(0)]
    slot = _gathered_tile(src_ref, stage_ref, buf_ref, sems, MOE_TILE)
    h = _from_token_tiles(buf_ref.at[slot])
    gates, _ = _route(h, wr_ref, br_ref)
    x = h.astype(bf16)
    lane = lax.broadcasted_iota(jnp.int32, gates.shape, 1)
    y = jnp.zeros(h.shape, f32)
    for e in range(EXPERTS_PER_GROUP):
        gcol = jnp.sum(jnp.where(lane == N_GROUPS + EXPERTS_PER_GROUP * grp + e, gates, 0.0), axis=1, keepdims=True)
        a = jax.nn.silu(_dot(x, wg_ref[e])) * _dot(x, wu_ref[e])
        y = y + _dot((a * gcol).astype(bf16), wd_ref[e])
    _to_token_tiles(o_ref, y)


def _moe(stage, tile_group, slot_token, w_route, b_route, wg, wu, wd, layer):
    tm = MOE_TILE
    n_slots = slot_token.shape[0]
    w_spec = lambda k, n: pl.BlockSpec((None, None, EXPERTS_PER_GROUP, k, n),
                                       lambda j, tg, src: (layer, tg[j], 0, 0, 0))
    grouped = lambda w: w.reshape(w.shape[0], N_GROUPS, EXPERTS_PER_GROUP, *w.shape[2:])
    return pl.pallas_call(
        _moe_kernel,
        grid_spec=pltpu.PrefetchScalarGridSpec(
            num_scalar_prefetch=2, grid=(n_slots // tm,),
            in_specs=[pl.BlockSpec(memory_space=pl.ANY),
                      pl.BlockSpec((D_MODEL, LANES), lambda j, tg, src: (0, 0)),
                      pl.BlockSpec((1, LANES), lambda j, tg, src: (0, 0)),
                      w_spec(D_MODEL, D_EXPERT), w_spec(D_MODEL, D_EXPERT), w_spec(D_EXPERT, D_MODEL)],
            out_specs=pl.BlockSpec((tm * TOKEN_ROWS, LANES), lambda j, tg, src: (j, 0)),
            scratch_shapes=[pltpu.VMEM((2, tm * TOKEN_ROWS, LANES), f32), pltpu.SemaphoreType.DMA((2,))]),
        out_shape=jax.ShapeDtypeStruct((n_slots * TOKEN_ROWS, LANES), f32),
        compiler_params=_params("arbitrary"),
        name="moe_ffn",
    )(tile_group, slot_token, stage, w_route, b_route, grouped(wg), grouped(wu), grouped(wd))


def _ple_kernel(pos_ref, x_ref, ys_ref, p_ref, g_ref, wg_ref, wp_ref, fg_ref, o_ref, buf_ref, sems, *, final):
    slot = _gathered_tile(pos_ref, ys_ref, buf_ref, sems, x_ref.shape[0])
    x = x_ref[...] + _from_token_tiles(buf_ref.at[slot])
    gate = jax.nn.sigmoid(_dot(_rms(x, g_ref[...]).astype(bf16), wg_ref[...]))
    y = x + gate * _dot(p_ref[...].astype(bf16), wp_ref[...])
    o_ref[...] = _rms(y, fg_ref[...]) if final else y


def _ple(x2d, ys, pos, p3, g, wg, wp, fg, layer, final):
    T = x2d.shape[0]
    tm = 512
    full = lambda a: pl.BlockSpec(a.shape, lambda i, pos: (0,) * a.ndim)
    return pl.pallas_call(
        functools.partial(_ple_kernel, final=final),
        grid_spec=pltpu.PrefetchScalarGridSpec(
            num_scalar_prefetch=1, grid=(T // tm,),
            in_specs=[pl.BlockSpec((tm, D_MODEL), lambda i, pos: (i, 0)),
                      pl.BlockSpec(memory_space=pl.ANY),
                      pl.BlockSpec((None, tm, PLE_DIM), lambda i, pos: (layer, i, 0)),
                      full(g), full(wg), full(wp), full(fg)],
            out_specs=pl.BlockSpec((tm, D_MODEL), lambda i, pos: (i, 0)),
            scratch_shapes=[pltpu.VMEM((2, tm * TOKEN_ROWS, LANES), f32), pltpu.SemaphoreType.DMA((2,))]),
        out_shape=jax.ShapeDtypeStruct((T, D_MODEL), f32),
        compiler_params=_params("arbitrary"),
        name="ple",
    )(pos, x2d, ys, p3, g, wg, wp, fg)


def _alibi_slopes():
    s = 2.0 ** (-8.0 * np.arange(1, N_HEADS_TOTAL + 1) / N_HEADS_TOTAL)
    assert np.all(np.log2(s[1::2]) == np.round(np.log2(s[1::2])))
    return jnp.asarray(s[0::2], f32), jnp.asarray(s[1::2], f32)


def _selection_constants(seq):
    ncb = seq // CMP_STRIDE
    nsel = seq // SEL_BLOCK
    n_cmp = (seq - CMP_BLOCK) // CMP_STRIDE + 1
    cs = np.arange(n_cmp) * CMP_STRIDE
    bs = np.arange(nsel) * SEL_BLOCK
    ov = np.clip(np.minimum(cs[:, None] + CMP_BLOCK, bs[None, :] + SEL_BLOCK)
                 - np.maximum(cs[:, None], bs[None, :]), 0, None) / CMP_BLOCK
    assert ncb == LANES and seq <= POS_SPLIT * 256
    ovt = np.zeros((LANES, ncb), np.float32)
    ovt[:nsel, :n_cmp] = ov.T
    pos = np.arange(seq)
    kaug = np.zeros((seq, LANES), np.float32)
    kaug[:, POS_LANE0:POS_LANE0 + 2] = 1.0
    kaug[:, POS_LANE0 + 2] = pos // POS_SPLIT
    kaug[:, POS_LANE0 + 3] = pos % POS_SPLIT
    kaug[pos, SEL_LANE0 + pos // SEL_BLOCK] = 1.0
    cend = np.arange(ncb) * CMP_STRIDE + CMP_BLOCK - 1
    caug = np.zeros((ncb, LANES), np.float32)
    caug[:, POS_LANE0:POS_LANE0 + 2] = 1.0
    caug[:, POS_LANE0 + 2] = cend // POS_SPLIT
    caug[:, POS_LANE0 + 3] = cend % POS_SPLIT
    gsel = np.zeros((3, LANES, GQA_REP // 2, 3, 2, HEAD_DIM), np.float32)
    for pr in range(GQA_REP // 2):
        for j in range(3):
            for hh in range(2):
                gsel[:, 3 * (2 * pr + hh) + j, pr, j, hh, :] = 1.0
    gsel = gsel.reshape(3 * LANES, (GQA_REP // 2) * 3 * LANES)
    return jnp.asarray(ovt, bf16), jnp.asarray(kaug), jnp.asarray(caug), jnp.asarray(gsel, bf16)


def _dispatch_plan(meta, counts, n_tokens):
    i32 = jnp.int32
    n_slots = n_tokens + N_GROUPS * MOE_TILE
    group = meta[:, 0, :].reshape(n_tokens).astype(i32)
    rank = (meta[:, 1, :] * RANK_SPLIT + meta[:, 2, :]).reshape(n_tokens).astype(i32)
    cnt = counts[0, :N_GROUPS].astype(i32)
    padded = (cnt + MOE_TILE - 1) // MOE_TILE * MOE_TILE
    end = jnp.cumsum(padded)
    pos = (end - padded)[group] + rank
    slot_token = jnp.zeros((n_slots,), i32).at[pos].set(jnp.arange(n_tokens, dtype=i32))
    tile_start = jnp.arange(n_slots // MOE_TILE, dtype=i32) * MOE_TILE
    tile_group = jnp.minimum(jnp.sum(tile_start[:, None] >= end[None, :], axis=1), N_GROUPS - 1).astype(i32)
    return pos, slot_token, tile_group


def _block_diag2(w):
    z = jnp.zeros_like(w)
    return jnp.concatenate([jnp.concatenate([w, z], axis=-1), jnp.concatenate([z, w], axis=-1)], axis=-2)


def _layout_w_in(w):
    gate = w[:, N_MAIN:]
    per = GQA_REP * 3
    blocks = [jnp.pad(gate[:, g * per:(g + 1) * per], ((0, 0), (0, LANES - per))) for g in range(N_KV_B)]
    return jnp.concatenate([w[:, :N_MAIN]] + blocks, axis=1).astype(bf16)


def kernel(x, p, attn_norm, w_in, w_out, w_cmp_k1, w_cmp_k2, w_cmp_v1, w_cmp_v2, cmp_pos, ffn_norm, w_route_group, b_route_group, w_route_expert, b_route_expert, w_expert_gate, w_expert_up, w_expert_down, ple_norm, w_ple_gate, w_ple_proj, final_norm):
    B, S, D = x.shape
    depth = w_in.shape[0]
    T = B * S
    slopes_a, slopes_b = _alibi_slopes()
    ovt, kaug, caug, gsel = _selection_constants(S)
    wg_all = w_expert_gate.astype(bf16)
    wu_all = w_expert_up.astype(bf16)
    wd_all = w_expert_down.astype(bf16)
    p3 = p.reshape(depth, T, PLE_DIM)
    row = lambda v: v.reshape(1, -1)
    n_route = N_GROUPS + N_EXPERTS

    x2d = x.reshape(T, D)
    for i in range(depth):
        proj3 = _in_proj(x2d, row(attn_norm[i]), _layout_w_in(w_in[i])).reshape(B, S, N_PROJ)
        oa = _dilated(proj3, slopes_a)
        w1 = lambda w: _block_diag2(w.reshape(CMP_BLOCK, HEAD_DIM, CMP_HIDDEN)).astype(bf16)
        pos_dup = jnp.concatenate([cmp_pos[i], cmp_pos[i]], axis=-1)
        kcmp, vcmp = _compress(proj3, pos_dup, w1(w_cmp_k1[i]), _block_diag2(w_cmp_k2[i]).astype(bf16),
                               w1(w_cmp_v1[i]), _block_diag2(w_cmp_v2[i]).astype(bf16))
        ob = _nsa(proj3, kcmp, vcmp, slopes_b, ovt, kaug, caug, gsel)
        wo = w_out[i].astype(bf16)
        x2d = _out_proj(x2d, oa.reshape(T, A_W), ob.reshape(T, -1), wo[:A_W], wo[A_W:])
        w_route = jnp.pad(jnp.concatenate([w_route_group[i], w_route_expert[i]], axis=1),
                          ((0, 0), (0, LANES - n_route)))
        b_route = jnp.pad(jnp.concatenate([b_route_group[i], b_route_expert[i]]), (0, LANES - n_route))
        stage, meta, counts = _router(x2d, row(ffn_norm[i]), w_route, row(b_route))
        pos, slot_token, tile_group = _dispatch_plan(meta, counts, T)
        ys = _moe(stage, tile_group, slot_token, w_route, row(b_route), wg_all, wu_all, wd_all, i)
        x2d = _ple(x2d, ys, pos, p3, row(ple_norm[i]), w_ple_gate[i].astype(bf16),
                   w_ple_proj[i].astype(bf16), row(final_norm), i, i == depth - 1)
    return x2d.reshape(B, S, D)
```

```python
import functools

import numpy as np
import jax
import jax.numpy as jnp
from jax import lax
from jax.experimental import pallas as pl
from jax.experimental.pallas import tpu as pltpu

D_MODEL = 1024
PLE_DIM = 256
HEAD_DIM = 64
N_HEADS_A = 8
N_HEADS_B = 8
N_KV_B = 2
GQA_REP = N_HEADS_B // N_KV_B
N_HEADS_TOTAL = N_HEADS_A + N_HEADS_B
DILATED_PATTERNS = ((128, 1), (512, 4), (2048, 16))
CMP_BLOCK = 32
CMP_STRIDE = 16
CMP_HIDDEN = 256
SEL_BLOCK = 64
SEL_TOP = 16
N_LOCAL_BLOCKS = 2
WINDOW = 512
N_GROUPS = 4
EXPERTS_PER_GROUP = 4
N_EXPERTS = N_GROUPS * EXPERTS_PER_GROUP
D_EXPERT = 512
RMS_EPS = 1e-6
NEG = -1e30
FORCE_BONUS = 1e4
SCALE = HEAD_DIM ** -0.5

LANES = 128
QB = 128
SEL_CHUNK = 512
NSA_UNITS = 2
DIL_INFLIGHT = 3
POS_LANE0 = HEAD_DIM
POS_SPLIT = 16
SEL_LANE0 = POS_LANE0 + 4
SEL_LANES = 32
MASK_BIG = -(2.0 ** 100)
MOE_TILE = 512
TOKEN_ROWS = D_MODEL // LANES
RANK_SPLIT = 128
A_W = N_HEADS_A * HEAD_DIM
N_MAIN = 3 * A_W + N_HEADS_B * HEAD_DIM + 6 * N_KV_B * HEAD_DIM
N_PROJ = N_MAIN + N_KV_B * LANES
COL_QB = (3 * A_W) // (2 * LANES)
COL_KC = (3 * A_W + N_HEADS_B * HEAD_DIM) // LANES
COL_GATE = N_MAIN // LANES
VMEM_LIMIT = 56 * 1024 * 1024

f32 = jnp.float32
bf16 = jnp.bfloat16


def _dot(a, b):
    return jnp.dot(a, b, preferred_element_type=f32)


def _dot_nt(a, b):
    return lax.dot_general(a, b, (((1,), (1,)), ((), ())), preferred_element_type=f32)


def _rms(x, g):
    return x * lax.rsqrt(jnp.mean(x * x, axis=-1, keepdims=True) + RMS_EPS) * g


def _params(*sem):
    return pltpu.CompilerParams(dimension_semantics=sem, vmem_limit_bytes=VMEM_LIMIT)


def _in_proj_kernel(x_ref, g_ref, w_ref, o_ref):
    h = _rms(x_ref[...], g_ref[...]).astype(bf16)
    for n0 in range(0, N_PROJ, 512):
        o_ref[:, n0:n0 + 512] = _dot(h, w_ref[:, n0:n0 + 512])


def _in_proj(x2d, g, w):
    T = x2d.shape[0]
    tm = 512
    return pl.pallas_call(
        _in_proj_kernel,
        grid=(T // tm,),
        in_specs=[pl.BlockSpec((tm, D_MODEL), lambda i: (i, 0)),
                  pl.BlockSpec((1, D_MODEL), lambda i: (0, 0)),
                  pl.BlockSpec((D_MODEL, N_PROJ), lambda i: (0, 0))],
        out_specs=pl.BlockSpec((tm, N_PROJ), lambda i: (i, 0)),
        out_shape=jax.ShapeDtypeStruct((T, N_PROJ), f32),
        compiler_params=_params("parallel"),
        name="in_proj",
    )(x2d, g, w)


def _dil_kernel(slope_ref, q_ref, k_ref, v_ref, o_ref, out_ref, lse_ref, bias_scr, *bufs, seq):
    hp = pl.program_id(1)
    lane = lax.broadcasted_iota(jnp.int32, (QB, LANES), 1)
    left = lane < HEAD_DIM
    slab = 32
    s_bufs, e_bufs = bufs[:DIL_INFLIGHT], bufs[DIL_INFLIGHT:]

    i = lax.broadcasted_iota(jnp.int32, (QB, 2 * QB), 0)
    j = lax.broadcasted_iota(jnp.int32, (QB, 2 * QB), 1)
    rel = i - j + QB
    valid = (rel >= 0) & (rel <= QB)
    relf = rel.astype(f32)
    for p, (window, dil) in enumerate(DILATED_PATTERNS):
        assert window // dil == QB
        for hh in range(2):
            bias_scr[2 * p + hh] = jnp.where(valid, (-float(dil) * slope_ref[2 * hp + hh]) * relf, NEG)

    ones = jnp.ones((2 * QB, LANES), bf16)

    def scores(u, blk):
        p, dil, row0, key0, nk = blk
        qc = q_ref[pl.ds(row0, QB, stride=dil), :] * SCALE
        q2 = jnp.concatenate([jnp.where(left, qc, 0.0), jnp.where(left, 0.0, qc)], axis=0).astype(bf16)
        s_bufs[u][:, :nk] = _dot_nt(q2, k_ref[pl.ds(key0, nk, stride=dil), :].astype(bf16))

    def probs(u, blk):
        p, dil, row0, key0, nk = blk
        c0 = 2 * QB - nk
        ms = []
        for hh in range(2):
            parts = []
            for sl in range(QB // slab):
                rows = slice(hh * QB + sl * slab, hh * QB + (sl + 1) * slab)
                tiles = [s_bufs[u][rows, c:c + LANES]
                         + bias_scr[2 * p + hh, sl * slab:(sl + 1) * slab, c0 + c:c0 + c + LANES]
                         for c in range(0, nk, LANES)]
                mx = tiles[0]
                for t in tiles[1:]:
                    mx = jnp.maximum(mx, t)
                mx = jnp.broadcast_to(jnp.max(mx, axis=1, keepdims=True), (slab, LANES))
                for c, t in zip(range(0, nk, LANES), tiles):
                    e_bufs[u][rows, c:c + LANES] = jnp.exp(t - mx).astype(bf16)
                parts.append(mx)
            ms.append(jnp.concatenate(parts, axis=0))
        return jnp.where(left, ms[0], ms[1])

    def values(u, blk, row_max):
        p, dil, row0, key0, nk = blk
        v2 = jnp.concatenate([v_ref[pl.ds(key0, nk, stride=dil), :].astype(bf16), ones[:nk]], axis=1)
        res = _dot(e_bufs[u][:, :nk], v2)
        rows = pl.ds(row0, QB, stride=dil)
        den = jnp.where(left, res[:QB, LANES:], res[QB:, LANES:])
        out_ref[p, rows, :] = jnp.where(left, res[:QB, :LANES], res[QB:, :LANES]) / den
        lse_ref[p, rows, :] = row_max + jnp.log(den)

    def run(blocks):
        ahead = DIL_INFLIGHT - 1
        for n in range(min(ahead, len(blocks))):
            scores(n % DIL_INFLIGHT, blocks[n])
        for n, blk in enumerate(blocks):
            if n + ahead < len(blocks):
                scores((n + ahead) % DIL_INFLIGHT, blocks[n + ahead])
            values(n % DIL_INFLIGHT, blk, probs(n % DIL_INFLIGHT, blk))

    def first(p, dil, r):
        return (p, dil, r, r, QB)

    def later(p, dil, r, a):
        return (p, dil, r + dil * QB * a, r + dil * QB * (a - 1), 2 * QB)

    for p, (window, dil) in enumerate(DILATED_PATTERNS):
        nblk = seq // dil // QB
        if dil == 1:
            run([first(p, dil, 0)])

            def trip(t, _, p=p, dil=dil):
                run([later(p, dil, 0, 1 + 3 * t + n) for n in range(3)])
                return 0

            assert (nblk - 1) % 3 == 0
            lax.fori_loop(0, (nblk - 1) // 3, trip, 0)
        elif nblk > 1:
            def per_class(r, _, p=p, dil=dil, nblk=nblk):
                run([first(p, dil, r)] + [later(p, dil, r, a) for a in range(1, nblk)])
                return 0

            lax.fori_loop(0, dil, per_class, 0)
        else:
            def some_classes(t, _, p=p, dil=dil):
                run([first(p, dil, 8 * t + n) for n in range(8)])
                return 0

            lax.fori_loop(0, dil // 8, some_classes, 0)

    ch = 256

    def combine(c, _):
        rows = pl.ds(pl.multiple_of(c * ch, ch), ch)
        lses = [lse_ref[p, rows, :] for p in range(len(DILATED_PATTERNS))]
        big = functools.reduce(jnp.maximum, lses)
        num = jnp.zeros((ch, LANES), f32)
        den = jnp.zeros((ch, LANES), f32)
        for p, lse in enumerate(lses):
            w = jnp.exp(lse - big)
            num = num + w * out_ref[p, rows, :]
            den = den + w
        o_ref[rows, :] = (num / den).astype(o_ref.dtype)
        return 0

    lax.fori_loop(0, seq // ch, combine, 0)


def _dilated(proj3, slopes_a):
    B, S, _ = proj3.shape
    npair = N_HEADS_A // 2
    blk = lambda off: pl.BlockSpec((None, S, LANES), lambda b, hp, off=off: (b, 0, off + hp))
    return pl.pallas_call(
        functools.partial(_dil_kernel, seq=S),
        grid=(B, npair),
        in_specs=[pl.BlockSpec(memory_space=pltpu.SMEM), blk(0), blk(npair), blk(2 * npair)],
        out_specs=pl.BlockSpec((None, S, LANES), lambda b, hp: (b, 0, hp)),
        out_shape=jax.ShapeDtypeStruct((B, S, A_W), bf16),
        scratch_shapes=([pltpu.VMEM((len(DILATED_PATTERNS), S, LANES), f32)] * 2
                        + [pltpu.VMEM((2 * len(DILATED_PATTERNS), QB, 2 * QB), f32)]
                        + [pltpu.VMEM((2 * QB, 2 * QB), f32)] * DIL_INFLIGHT
                        + [pltpu.VMEM((2 * QB, 2 * QB), bf16)] * DIL_INFLIGHT),
        compiler_params=_params("parallel", "parallel"),
        name="dilated_attn",
    )(slopes_a, proj3, proj3, proj3)


def _cmp_kernel(kc_ref, vc_ref, pos_ref, w1k_ref, w2k_ref, w1v_ref, w2v_ref, ko_ref, vo_ref, *, ncb):
    half = CMP_BLOCK // 2
    for x_ref, pi, w1_ref, w2_ref, o_ref in ((kc_ref, 0, w1k_ref, w2k_ref, ko_ref),
                                             (vc_ref, 1, w1v_ref, w2v_ref, vo_ref)):
        lo = jnp.zeros((ncb, 2 * CMP_HIDDEN), f32)
        hi = jnp.zeros((ncb, 2 * CMP_HIDDEN), f32)
        for r in range(half):
            x = x_ref[pl.ds(r, ncb, stride=CMP_STRIDE), :]
            lo = lo + _dot((x + pos_ref[pi, r:r + 1, :]).astype(bf16), w1_ref[r])
            hi = hi + _dot((x + pos_ref[pi, r + half:r + half + 1, :]).astype(bf16), w1_ref[r + half])
        h1 = lo + pltpu.roll(hi, ncb - 1, axis=0)
        o_ref[...] = _dot(jax.nn.gelu(h1).astype(bf16), w2_ref[...])


def _compress(proj3, pos_dup, w1k, w2k, w1v, w2v):
    B, S, _ = proj3.shape
    ncb = S // CMP_STRIDE
    full = lambda a: pl.BlockSpec(a.shape, lambda b: (0,) * a.ndim)
    out = jax.ShapeDtypeStruct((B, ncb, LANES), f32)
    return pl.pallas_call(
        functools.partial(_cmp_kernel, ncb=ncb),
        grid=(B,),
        in_specs=[pl.BlockSpec((None, S, LANES), lambda b: (b, 0, COL_KC)),
                  pl.BlockSpec((None, S, LANES), lambda b: (b, 0, COL_KC + 1)),
                  full(pos_dup), full(w1k), full(w2k), full(w1v), full(w2v)],
        out_specs=[pl.BlockSpec((None, ncb, LANES), lambda b: (b, 0, 0))] * 2,
        out_shape=[out, out],
        compiler_params=_params("parallel"),
        name="nsa_compress",
    )(proj3, proj3, pos_dup, w1k, w2k, w1v, w2v)


def _nsa_kernel(slope_ref, q_ref, ks_ref, vs_ref, kw_ref, vw_ref, gl_ref, kcmp_ref, vcmp_ref,
                ovt_ref, kaug_ref, caug_ref, gsel_ref, o_ref,
                ksa, vsa, kwa, vwa, kca, vca, q_scr, sel_scr, s_m, *per_unit, seq):
    g = pl.program_id(1)
    qi = pl.program_id(2)
    nsel = seq // SEL_BLOCK
    n_top = min(SEL_TOP, nsel)
    assert nsel <= SEL_LANES and N_LOCAL_BLOCKS * SEL_BLOCK >= QB and n_top > N_LOCAL_BLOCKS
    slab = 32
    ur = GQA_REP * QB // NSA_UNITS
    s_w, e_w, s_d, e_d, e_m, m_h, al_h, acc = [per_unit[i * NSA_UNITS:(i + 1) * NSA_UNITS] for i in range(8)]

    def group_lanes(x):
        return jnp.where(g == 0, x, pltpu.roll(x, HEAD_DIM, axis=1))

    @pl.when(qi == 0)
    def _prep():
        ch = 256
        lane_c = lax.broadcasted_iota(jnp.int32, (ch, LANES), 1)
        is_k = lane_c < HEAD_DIM
        for c in range(seq // ch):
            sl = slice(c * ch, (c + 1) * ch)
            aug = kaug_ref[sl, :]
            ksa[sl, :] = jnp.where(is_k, group_lanes(ks_ref[sl, :]), aug).astype(bf16)
            kwa[sl, :] = jnp.where(is_k, group_lanes(kw_ref[sl, :]),
                                   jnp.where(lane_c < SEL_LANE0, aug, 0.0)).astype(bf16)
            for src, dst in ((vs_ref, vsa), (vw_ref, vwa)):
                v = group_lanes(src[sl, :])
                dst[sl, :LANES] = jnp.where(is_k, v, 1.0).astype(bf16)
                dst[sl, LANES:] = jnp.where(is_k, 1.0, pltpu.roll(v, HEAD_DIM, axis=1)).astype(bf16)
        lane_k = lax.broadcasted_iota(jnp.int32, kca.shape, 1)
        kca[...] = jnp.where(lane_k < HEAD_DIM, group_lanes(kcmp_ref[...]), caug_ref[...]).astype(bf16)
        vc = group_lanes(vcmp_ref[...])
        vca[...] = jnp.where(lane_k < HEAD_DIM, vc, pltpu.roll(vc, HEAD_DIM, axis=1)).astype(bf16)

    lane = lax.broadcasted_iota(jnp.int32, (QB, LANES), 1)
    left = lane < HEAD_DIM
    ii = lax.broadcasted_iota(jnp.int32, (QB, LANES), 0)
    t_row = qi * QB + ii
    t_hi = (t_row // POS_SPLIT).astype(f32)
    t_lo = (t_row % POS_SPLIT).astype(f32)
    slopes = [slope_ref[g * GQA_REP + r] for r in range(GQA_REP)]
    head = lambda a, r: a[r * QB:(r + 1) * QB]

    def q_head(r):
        x = q_ref[:, (r // 2) * LANES:(r // 2 + 1) * LANES] * SCALE
        if r % 2:
            x = pltpu.roll(x, HEAD_DIM, axis=1)
        m = slopes[r]
        pos = jnp.where(lane == POS_LANE0, (-POS_SPLIT * m) * t_hi,
                        jnp.where(lane == POS_LANE0 + 1, (-m) * t_lo,
                                  jnp.where(lane == POS_LANE0 + 2, POS_SPLIT * m,
                                            jnp.where(lane == POS_LANE0 + 3, m, 0.0))))
        return jnp.where(left, x, pos)

    q4 = jnp.concatenate([q_head(r) for r in range(GQA_REP)], axis=0)
    q4b = q4.astype(bf16)

    def rel_tile(nk, offset):
        i = lax.broadcasted_iota(jnp.int32, (QB, nk), 0)
        j = lax.broadcasted_iota(jnp.int32, (QB, nk), 1)
        return (i - j + offset).astype(f32)

    def softmax_rows(s_ref, e_ref, m_ref, al_ref, nk, bias, running):
        for sl in range(ur // slab):
            rows = slice(sl * slab, (sl + 1) * slab)
            cols = [slice(j * LANES, (j + 1) * LANES) for j in range(nk // LANES)]
            i0 = (sl * slab) % QB
            tiles = [s_ref[rows, c] if bias is None else s_ref[rows, c] + bias[i0:i0 + slab, c] for c in cols]
            mx = tiles[0]
            for t in tiles[1:]:
                mx = jnp.maximum(mx, t)
            m_new = jnp.broadcast_to(jnp.max(mx, axis=1, keepdims=True), (slab, LANES))
            if running:
                m_old = m_ref[rows, :]
                m_new = jnp.maximum(m_old, m_new)
                al_ref[rows, :] = jnp.exp(m_old - m_new)
            if m_ref is not None:
                m_ref[rows, :] = m_new
            for c, t in zip(cols, tiles):
                e_ref[rows, c] = jnp.exp(t - m_new).astype(bf16)

    unit = lambda a, u: a[u * ur:(u + 1) * ur]

    def own_half(pv, u):
        heads = range(u * ur // QB, (u + 1) * ur // QB)
        return jnp.concatenate([pv[(r - heads[0]) * QB:(r - heads[0] + 1) * QB, (r % 2) * LANES:(r % 2 + 1) * LANES]
                                for r in heads], axis=0)

    s4 = _dot_nt(q4b, kca[...])

    sig = jax.nn.sigmoid(gl_ref[...])
    g_hi = sig.astype(bf16)
    g_mid = (sig - g_hi.astype(f32)).astype(bf16)
    g_lo = (sig - g_hi.astype(f32) - g_mid.astype(f32)).astype(bf16)
    gate_b = _dot(jnp.concatenate([g_hi, g_mid, g_lo], axis=1), gsel_ref[...])

    nwin = WINDOW + QB
    w0 = pl.multiple_of(jnp.maximum(qi - WINDOW // QB, 0) * QB, QB)
    dist_w = rel_tile(nwin, qi * QB - w0)
    bias_w = jnp.where((dist_w >= 0) & (dist_w < WINDOW), 0.0, NEG)
    for u in range(NSA_UNITS):
        s_w[u][...] = _dot_nt(unit(q4b, u), kwa[pl.ds(w0, nwin), :])

    d0 = pl.multiple_of(qi * QB, QB)
    bias_d = jnp.where(rel_tile(QB, 0) >= 0, 0.0, NEG)
    for u in range(NSA_UNITS):
        s_d[u][...] = _dot_nt(unit(q4b, u), ksa[pl.ds(d0, QB), :])

    cmp_end = (lane * CMP_STRIDE + (CMP_BLOCK - 1))
    valid_c = t_row >= cmp_end
    ps = []
    p_sum = jnp.zeros((QB, LANES), f32)
    row_bcast = lambda col: jnp.broadcast_to(col, (QB, LANES))
    for r in range(GQA_REP):
        s = jnp.where(valid_c, head(s4, r), NEG)
        e = jnp.exp(s - row_bcast(jnp.max(s, axis=1, keepdims=True)))
        p = jnp.where(valid_c, e, 0.0) / row_bcast(jnp.sum(e, axis=1, keepdims=True))
        p_sum = p_sum + p
        ps.append(p.astype(bf16))
    o_cmp4 = _dot(jnp.concatenate(ps, axis=0), vca[...])

    o_w = []
    for u in range(NSA_UNITS):
        softmax_rows(s_w[u], e_w[u], None, None, nwin, bias_w, False)
        o_w.append(own_half(_dot(e_w[u][...], vwa[pl.ds(w0, nwin), :]), u))
    for u in range(NSA_UNITS):
        softmax_rows(s_d[u], e_d[u], m_h[u], None, QB, bias_d, False)
        acc[u][...] = own_half(_dot(e_d[u][...], vsa[pl.ds(d0, QB), :]), u)

    back = t_row // SEL_BLOCK - lane
    valid_s = (back >= 0) & (lane < nsel)

    @pl.when(2 * qi + 2 <= n_top)
    def _all_valid():
        sel_scr[...] = jnp.where(valid_s, 1.0, 0.0)

    @pl.when(2 * qi + 2 > n_top)
    def _top_k():
        p_hi = p_sum.astype(bf16)
        p_lo = (p_sum - p_hi.astype(f32)).astype(bf16)
        imp_t = (_dot_nt(ovt_ref[...], p_hi) + _dot_nt(ovt_ref[...], p_lo))[:SEL_LANES]
        blk = lax.broadcasted_iota(jnp.int32, (SEL_LANES, QB), 0)
        tq = qi * QB + lax.broadcasted_iota(jnp.int32, (SEL_LANES, QB), 1)
        back_t = tq // SEL_BLOCK - blk
        valid_t = (back_t >= 0) & (blk < nsel)
        forced = (blk == 0) | (valid_t & (back_t < N_LOCAL_BLOCKS))
        score = jnp.where(valid_t, imp_t + jnp.where(forced, FORCE_BONUS, 0.0), NEG)
        score = jnp.where(blk < nsel, score, 2.0 * NEG)
        rank = jnp.zeros((SEL_LANES, QB), jnp.int32)
        for n in range(nsel):
            row = score[n:n + 1, :]
            ahead = (row > score) | ((row == score) & (blk > n))
            rank = rank + ahead.astype(jnp.int32)
        sel_t = jnp.where((rank < n_top) & valid_t, 1.0, 0.0)
        sel_t = jnp.concatenate([sel_t, jnp.zeros((LANES - SEL_LANES, QB), f32)], axis=0)
        sel_scr[...] = sel_t.T

    sel_bias = jnp.where((sel_scr[...] > 0.5) & (lane < 2 * qi), 0.0, MASK_BIG)
    sel_bias = pltpu.roll(sel_bias, SEL_LANE0, axis=1)
    in_sel = (lane >= SEL_LANE0) & (lane < SEL_LANE0 + SEL_LANES)
    for r in range(GQA_REP):
        q_scr[r * QB:(r + 1) * QB, :] = jnp.where(in_sel, sel_bias, head(q4, r)).astype(bf16)

    per = SEL_CHUNK // QB
    n_chunks = (qi + per - 1) // per

    def sel_scores(kc, half):
        rows = pl.ds(pl.multiple_of(kc * SEL_CHUNK, SEL_CHUNK), SEL_CHUNK)
        s_m[half] = _dot_nt(q_scr[...], ksa[rows, :])

    sel_scores(0, 0)

    def sel_chunk(kc, half):
        sel_scores(jnp.minimum(kc + 1, n_chunks - 1), 1 - half)
        rows = pl.ds(pl.multiple_of(kc * SEL_CHUNK, SEL_CHUNK), SEL_CHUNK)
        for u in range(NSA_UNITS):
            softmax_rows(s_m.at[half, pl.ds(u * ur, ur)], e_m[u], m_h[u], al_h[u], SEL_CHUNK, None, True)
            acc[u][...] = al_h[u][...] * acc[u][...] + own_half(_dot(e_m[u][...], vsa[rows, :]), u)

    def sel_body(t, _):
        sel_chunk(2 * t, 0)

        @pl.when(2 * t + 1 < n_chunks)
        def _odd():
            sel_chunk(2 * t + 1, 1)
        return 0

    lax.fori_loop(0, (n_chunks + 1) // 2, sel_body, 0)

    acc_s = jnp.concatenate([a[...] for a in acc], axis=0)
    acc_w = jnp.concatenate(o_w, axis=0)
    for pr in range(GQA_REP // 2):
        ev, od = 2 * pr, 2 * pr + 1
        out = gate_b[:, (3 * pr) * LANES:(3 * pr + 1) * LANES] * jnp.where(left, head(o_cmp4, ev), head(o_cmp4, od))
        for j, a in ((1, acc_s), (2, acc_w)):
            num = jnp.where(left, head(a, ev), head(a, od))
            den = pltpu.roll(jnp.where(left, head(a, od), head(a, ev)), HEAD_DIM, axis=1)
            out = out + gate_b[:, (3 * pr + j) * LANES:(3 * pr + j + 1) * LANES] / den * num
        o_ref[:, pr * LANES:(pr + 1) * LANES] = out.astype(o_ref.dtype)


def _nsa(proj3, kcmp, vcmp, slopes_b, ovt, kaug, caug, gsel):
    B, S, _ = proj3.shape
    ncb = kcmp.shape[1]
    nrow = GQA_REP * QB
    ur = nrow // NSA_UNITS
    nwin = WINDOW + QB
    kv = lambda j: pl.BlockSpec((None, S, LANES), lambda b, g, qi, j=j: (b, 0, COL_KC + j))
    full = lambda a: pl.BlockSpec(a.shape, lambda b, g, qi: (0,) * a.ndim)
    return pl.pallas_call(
        functools.partial(_nsa_kernel, seq=S),
        grid=(B, N_KV_B, S // QB),
        in_specs=[pl.BlockSpec(memory_space=pltpu.SMEM),
                  pl.BlockSpec((None, QB, 2 * LANES), lambda b, g, qi: (b, qi, COL_QB + g)),
                  kv(2), kv(3), kv(4), kv(5),
                  pl.BlockSpec((None, QB, LANES), lambda b, g, qi: (b, qi, COL_GATE + g)),
                  pl.BlockSpec((None, ncb, LANES), lambda b, g, qi: (b, 0, 0)),
                  pl.BlockSpec((None, ncb, LANES), lambda b, g, qi: (b, 0, 0)),
                  full(ovt), full(kaug), full(caug), full(gsel)],
        out_specs=pl.BlockSpec((None, QB, 2 * LANES), lambda b, g, qi: (b, qi, g)),
        out_shape=jax.ShapeDtypeStruct((B, S, N_HEADS_B * HEAD_DIM), bf16),
        scratch_shapes=([pltpu.VMEM((S, LANES), bf16), pltpu.VMEM((S, 2 * LANES), bf16)] * 2
                        + [pltpu.VMEM((ncb, LANES), bf16)] * 2
                        + [pltpu.VMEM((nrow, LANES), bf16), pltpu.VMEM((QB, LANES), f32),
                           pltpu.VMEM((2, nrow, SEL_CHUNK), f32)]
                        + [pltpu.VMEM((ur, nwin), f32)] * NSA_UNITS + [pltpu.VMEM((ur, nwin), bf16)] * NSA_UNITS
                        + [pltpu.VMEM((ur, QB), f32)] * NSA_UNITS + [pltpu.VMEM((ur, QB), bf16)] * NSA_UNITS
                        + [pltpu.VMEM((ur, SEL_CHUNK), bf16)] * NSA_UNITS
                        + [pltpu.VMEM((ur, LANES), f32)] * (2 * NSA_UNITS)
                        + [pltpu.VMEM((ur, LANES), f32)] * NSA_UNITS),
        compiler_params=_params("parallel", "parallel", "arbitrary"),
        name="nsa_attn",
    )(slopes_b, proj3, proj3, proj3, proj3, proj3, proj3, kcmp, vcmp, ovt, kaug, caug, gsel)


def _route(h, w_ref, b_ref):
    w = w_ref[...]
    h_hi, w_hi = h.astype(bf16), w.astype(bf16)
    h_lo, w_lo = (h - h_hi.astype(f32)).astype(bf16), (w - w_hi.astype(f32)).astype(bf16)
    logit = _dot(h_hi, w_hi) + (_dot(h_hi, w_lo) + _dot(h_lo, w_hi)) + b_ref[...]
    tm = logit.shape[0]
    lane = lax.broadcasted_iota(jnp.int32, (tm, LANES), 1)
    big = jnp.int32(LANES)
    is_g = lane < N_GROUPS
    gl = jnp.where(is_g, logit, NEG)
    gmax = jnp.max(gl, axis=1, keepdims=True)
    gsum = jnp.sum(jnp.where(is_g, jnp.exp(gl - gmax), 0.0), axis=1, keepdims=True)
    gsel = jnp.min(jnp.where(is_g & (gl == gmax), lane, big), axis=1, keepdims=True)
    gw = 1.0 / gsum
    e_lane = lane - N_GROUPS
    in_grp = (e_lane >= 0) & (e_lane < N_EXPERTS) & (e_lane // EXPERTS_PER_GROUP == gsel)
    el = jnp.where(in_grp, logit, NEG)
    t1 = jnp.max(el, axis=1, keepdims=True)
    i1 = jnp.min(jnp.where(in_grp & (el == t1), lane, big), axis=1, keepdims=True)
    el2 = jnp.where(lane == i1, NEG, el)
    t2 = jnp.max(el2, axis=1, keepdims=True)
    i2 = jnp.min(jnp.where(in_grp & (lane != i1) & (el2 == t2), lane, big), axis=1, keepdims=True)
    e2 = jnp.exp(t2 - t1)
    w1 = gw / (1.0 + e2)
    w2 = gw * e2 / (1.0 + e2)
    return jnp.where(lane == i1, w1, jnp.where(lane == i2, w2, 0.0)), gsel


def _to_token_tiles(ref, x):
    for s in range(D_MODEL // LANES):
        ref[pl.ds(s, x.shape[0], stride=D_MODEL // LANES), :] = x[:, s * LANES:(s + 1) * LANES]


def _from_token_tiles(ref):
    n = D_MODEL // LANES
    return jnp.concatenate([ref[pl.ds(s, ref.shape[0] // n, stride=n), :] for s in range(n)], axis=1)


def _out_route_kernel(x_ref, oa_ref, ob_ref, wa_ref, wb_ref, g_ref, w_ref, b_ref, tri_ref,
                      x1_ref, stage_ref, meta_ref, cnt_ref, cnt_scr):
    @pl.when(pl.program_id(0) == 0)
    def _zero():
        cnt_scr[...] = jnp.zeros_like(cnt_scr)

    x1 = x_ref[...] + _dot(oa_ref[...], wa_ref[...]) + _dot(ob_ref[...], wb_ref[...])
    x1_ref[...] = x1
    h = _rms(x1, g_ref[...])
    _to_token_tiles(stage_ref, h)
    _, gsel = _route(h, w_ref, b_ref)
    tm = h.shape[0]
    lane = lax.broadcasted_iota(jnp.int32, (tm, LANES), 1)
    is_g = lane < N_GROUPS

    onehot = jnp.where(is_g & (lane == gsel), 1.0, 0.0)
    before = _dot(tri_ref[...], onehot.astype(bf16)) + cnt_scr[...]
    rank = jnp.sum(onehot * before, axis=1, keepdims=True)
    cnt_scr[...] = before[tm - 1:tm, :] + onehot[tm - 1:tm, :]
    cnt_ref[...] = cnt_scr[...]
    rank_hi = jnp.floor(rank * (1.0 / RANK_SPLIT))
    cols = jnp.where(lane == 0, gsel.astype(f32), jnp.where(lane == 1, rank_hi,
                                                            jnp.where(lane == 2, rank - RANK_SPLIT * rank_hi, 0.0)))
    pick = (lax.broadcasted_iota(jnp.int32, (8, LANES), 0) == lax.broadcasted_iota(jnp.int32, (8, LANES), 1))
    meta_ref[...] = _dot_nt(jnp.where(pick, 1.0, 0.0).astype(bf16), cols.astype(bf16))


def _out_route(x2d, oa, ob, wa, wb, g, w, b):
    T = x2d.shape[0]
    tm = MOE_TILE
    assert T // RANK_SPLIT <= 256
    tri = jnp.asarray(np.tril(np.ones((tm, tm), np.float32), -1), bf16)
    row = lambda n: pl.BlockSpec((tm, n), lambda i: (i, 0))
    full = lambda a: pl.BlockSpec(a.shape, lambda i: (0,) * a.ndim)
    return pl.pallas_call(
        _out_route_kernel,
        grid=(T // tm,),
        in_specs=[row(D_MODEL), row(oa.shape[1]), row(ob.shape[1]), full(wa), full(wb),
                  full(g), full(w), full(b), full(tri)],
        out_specs=[row(D_MODEL),
                   pl.BlockSpec((tm * TOKEN_ROWS, LANES), lambda i: (i, 0)),
                   pl.BlockSpec((None, 8, tm), lambda i: (i, 0, 0)),
                   pl.BlockSpec((1, LANES), lambda i: (0, 0))],
        out_shape=[jax.ShapeDtypeStruct((T, D_MODEL), f32),
                   jax.ShapeDtypeStruct((T * TOKEN_ROWS, LANES), f32),
                   jax.ShapeDtypeStruct((T // tm, 8, tm), f32),
                   jax.ShapeDtypeStruct((1, LANES), f32)],
        scratch_shapes=[pltpu.VMEM((1, LANES), f32)],
        compiler_params=_params("arbitrary"),
        name="out_proj_router",
    )(x2d, oa, ob, wa, wb, g, w, b, tri)


def _token_rows(t, n=1):
    return pl.ds(pl.multiple_of(t * TOKEN_ROWS, TOKEN_ROWS), n * TOKEN_ROWS)


def _gathered_tile(idx_ref, src_ref, buf_ref, sems, tm):
    i = pl.program_id(0)
    n = pl.num_programs(0)

    def issue(step):
        slot = step % 2

        def one(k, _):
            pltpu.make_async_copy(src_ref.at[_token_rows(idx_ref[step * tm + k])],
                                  buf_ref.at[slot, _token_rows(k)], sems.at[slot]).start()
            return 0

        lax.fori_loop(0, tm, one, 0, unroll=8)

    @pl.when(i == 0)
    def _first():
        issue(i)

    @pl.when(i + 1 < n)
    def _next():
        issue(i + 1)

    slot = i % 2
    pltpu.make_async_copy(src_ref.at[_token_rows(0, tm)], buf_ref.at[slot], sems.at[slot]).wait()
    return slot


def _moe_kernel(tg_ref, src_ref, stage_ref, wr_ref, br_ref, wg_ref, wu_ref, wd_ref, o_ref, buf_ref, sems):
    grp = tg_ref[pl.program_id(0)]
    slot = _gathered_tile(src_ref, stage_ref, buf_ref, sems, MOE_TILE)
    h = _from_token_tiles(buf_ref.at[slot])
    gates, _ = _route(h, wr_ref, br_ref)
    x = h.astype(bf16)
    lane = lax.broadcasted_iota(jnp.int32, gates.shape, 1)
    y = jnp.zeros(h.shape, f32)
    for e in range(EXPERTS_PER_GROUP):
        gcol = jnp.sum(jnp.where(lane == N_GROUPS + EXPERTS_PER_GROUP * grp + e, gates, 0.0), axis=1, keepdims=True)
        a = jax.nn.silu(_dot(x, wg_ref[e])) * _dot(x, wu_ref[e])
        y = y + _dot((a * gcol).astype(bf16), wd_ref[e])
    _to_token_tiles(o_ref, y)


def _moe(stage, tile_group, slot_token, w_route, b_route, wg, wu, wd, layer):
    tm = MOE_TILE
    n_slots = slot_token.shape[0]
    w_spec = lambda k, n: pl.BlockSpec((None, None, EXPERTS_PER_GROUP, k, n),
                                       lambda j, tg, src: (layer, tg[j], 0, 0, 0))
    grouped = lambda w: w.reshape(w.shape[0], N_GROUPS, EXPERTS_PER_GROUP, *w.shape[2:])
    return pl.pallas_call(
        _moe_kernel,
        grid_spec=pltpu.PrefetchScalarGridSpec(
            num_scalar_prefetch=2, grid=(n_slots // tm,),
            in_specs=[pl.BlockSpec(memory_space=pl.ANY),
                      pl.BlockSpec((D_MODEL, LANES), lambda j, tg, src: (0, 0)),
                      pl.BlockSpec((1, LANES), lambda j, tg, src: (0, 0)),
                      w_spec(D_MODEL, D_EXPERT), w_spec(D_MODEL, D_EXPERT), w_spec(D_EXPERT, D_MODEL)],
            out_specs=pl.BlockSpec((tm * TOKEN_ROWS, LANES), lambda j, tg, src: (j, 0)),
            scratch_shapes=[pltpu.VMEM((2, tm * TOKEN_ROWS, LANES), f32), pltpu.SemaphoreType.DMA((2,))]),
        out_shape=jax.ShapeDtypeStruct((n_slots * TOKEN_ROWS, LANES), f32),
        compiler_params=_params("arbitrary"),
        name="moe_ffn",
    )(tile_group, slot_token, stage, w_route, b_route, grouped(wg), grouped(wu), grouped(wd))


def _ple_kernel(pos_ref, x_ref, ys_ref, p_ref, g_ref, wg_ref, wp_ref, fg_ref, o_ref, buf_ref, sems, *, final):
    slot = _gathered_tile(pos_ref, ys_ref, buf_ref, sems, x_ref.shape[0])
    x = x_ref[...] + _from_token_tiles(buf_ref.at[slot])
    gate = jax.nn.sigmoid(_dot(_rms(x, g_ref[...]).astype(bf16), wg_ref[...]))
    y = x + gate * _dot(p_ref[...].astype(bf16), wp_ref[...])
    o_ref[...] = _rms(y, fg_ref[...]) if final else y


def _ple(x2d, ys, pos, p3, g, wg, wp, fg, layer, final):
    T = x2d.shape[0]
    tm = 512
    full = lambda a: pl.BlockSpec(a.shape, lambda i, pos: (0,) * a.ndim)
    return pl.pallas_call(
        functools.partial(_ple_kernel, final=final),
        grid_spec=pltpu.PrefetchScalarGridSpec(
            num_scalar_prefetch=1, grid=(T // tm,),
            in_specs=[pl.BlockSpec((tm, D_MODEL), lambda i, pos: (i, 0)),
                      pl.BlockSpec(memory_space=pl.ANY),
                      pl.BlockSpec((None, tm, PLE_DIM), lambda i, pos: (layer, i, 0)),
                      full(g), full(wg), full(wp), full(fg)],
            out_specs=pl.BlockSpec((tm, D_MODEL), lambda i, pos: (i, 0)),
            scratch_shapes=[pltpu.VMEM((2, tm * TOKEN_ROWS, LANES), f32), pltpu.SemaphoreType.DMA((2,))]),
        out_shape=jax.ShapeDtypeStruct((T, D_MODEL), f32),
        compiler_params=_params("arbitrary"),
        name="ple",
    )(pos, x2d, ys, p3, g, wg, wp, fg)


def _alibi_slopes():
    s = 2.0 ** (-8.0 * np.arange(1, N_HEADS_TOTAL + 1) / N_HEADS_TOTAL)
    assert np.all(np.log2(s[1::2]) == np.round(np.log2(s[1::2])))
    return jnp.asarray(s[0::2], f32), jnp.asarray(s[1::2], f32)


def _selection_constants(seq):
    ncb = seq // CMP_STRIDE
    nsel = seq // SEL_BLOCK
    n_cmp = (seq - CMP_BLOCK) // CMP_STRIDE + 1
    cs = np.arange(n_cmp) * CMP_STRIDE
    bs = np.arange(nsel) * SEL_BLOCK
    ov = np.clip(np.minimum(cs[:, None] + CMP_BLOCK, bs[None, :] + SEL_BLOCK)
                 - np.maximum(cs[:, None], bs[None, :]), 0, None) / CMP_BLOCK
    assert ncb == LANES and seq <= POS_SPLIT * 256
    ovt = np.zeros((LANES, ncb), np.float32)
    ovt[:nsel, :n_cmp] = ov.T
    pos = np.arange(seq)
    kaug = np.zeros((seq, LANES), np.float32)
    kaug[:, POS_LANE0:POS_LANE0 + 2] = 1.0
    kaug[:, POS_LANE0 + 2] = pos // POS_SPLIT
    kaug[:, POS_LANE0 + 3] = pos % POS_SPLIT
    kaug[pos, SEL_LANE0 + pos // SEL_BLOCK] = 1.0
    cend = np.arange(ncb) * CMP_STRIDE + CMP_BLOCK - 1
    caug = np.zeros((ncb, LANES), np.float32)
    caug[:, POS_LANE0:POS_LANE0 + 2] = 1.0
    caug[:, POS_LANE0 + 2] = cend // POS_SPLIT
    caug[:, POS_LANE0 + 3] = cend % POS_SPLIT
    gsel = np.zeros((3, LANES, GQA_REP // 2, 3, 2, HEAD_DIM), np.float32)
    for pr in range(GQA_REP // 2):
        for j in range(3):
            for hh in range(2):
                gsel[:, 3 * (2 * pr + hh) + j, pr, j, hh, :] = 1.0
    gsel = gsel.reshape(3 * LANES, (GQA_REP // 2) * 3 * LANES)
    return jnp.asarray(ovt, bf16), jnp.asarray(kaug), jnp.asarray(caug), jnp.asarray(gsel, bf16)


def _dispatch_plan(meta, counts, n_tokens):
    i32 = jnp.int32
    n_slots = n_tokens + N_GROUPS * MOE_TILE
    group = meta[:, 0, :].reshape(n_tokens).astype(i32)
    rank = (meta[:, 1, :] * RANK_SPLIT + meta[:, 2, :]).reshape(n_tokens).astype(i32)
    cnt = counts[0, :N_GROUPS].astype(i32)
    padded = (cnt + MOE_TILE - 1) // MOE_TILE * MOE_TILE
    end = jnp.cumsum(padded)
    pos = (end - padded)[group] + rank
    slot_token = jnp.zeros((n_slots,), i32).at[pos].set(jnp.arange(n_tokens, dtype=i32))
    tile_start = jnp.arange(n_slots // MOE_TILE, dtype=i32) * MOE_TILE
    tile_group = jnp.minimum(jnp.sum(tile_start[:, None] >= end[None, :], axis=1), N_GROUPS - 1).astype(i32)
    return pos, slot_token, tile_group


def _block_diag2(w):
    z = jnp.zeros_like(w)
    return jnp.concatenate([jnp.concatenate([w, z], axis=-1), jnp.concatenate([z, w], axis=-1)], axis=-2)


def _layout_w_in(w):
    gate = w[:, N_MAIN:]
    per = GQA_REP * 3
    blocks = [jnp.pad(gate[:, g * per:(g + 1) * per], ((0, 0), (0, LANES - per))) for g in range(N_KV_B)]
    return jnp.concatenate([w[:, :N_MAIN]] + blocks, axis=1).astype(bf16)


def kernel(x, p, attn_norm, w_in, w_out, w_cmp_k1, w_cmp_k2, w_cmp_v1, w_cmp_v2, cmp_pos, ffn_norm, w_route_group, b_route_group, w_route_expert, b_route_expert, w_expert_gate, w_expert_up, w_expert_down, ple_norm, w_ple_gate, w_ple_proj, final_norm):
    B, S, D = x.shape
    depth = w_in.shape[0]
    T = B * S
    slopes_a, slopes_b = _alibi_slopes()
    ovt, kaug, caug, gsel = _selection_constants(S)
    wg_all = w_expert_gate.astype(bf16)
    wu_all = w_expert_up.astype(bf16)
    wd_all = w_expert_down.astype(bf16)
    p3 = p.reshape(depth, T, PLE_DIM)
    row = lambda v: v.reshape(1, -1)
    n_route = N_GROUPS + N_EXPERTS

    x2d = x.reshape(T, D)
    for i in range(depth):
        proj3 = _in_proj(x2d, row(attn_norm[i]), _layout_w_in(w_in[i])).reshape(B, S, N_PROJ)
        oa = _dilated(proj3, slopes_a)
        w1 = lambda w: _block_diag2(w.reshape(CMP_BLOCK, HEAD_DIM, CMP_HIDDEN)).astype(bf16)
        pos_dup = jnp.concatenate([cmp_pos[i], cmp_pos[i]], axis=-1)
        kcmp, vcmp = _compress(proj3, pos_dup, w1(w_cmp_k1[i]), _block_diag2(w_cmp_k2[i]).astype(bf16),
                               w1(w_cmp_v1[i]), _block_diag2(w_cmp_v2[i]).astype(bf16))
        ob = _nsa(proj3, kcmp, vcmp, slopes_b, ovt, kaug, caug, gsel)
        wo = w_out[i].astype(bf16)
        w_route = jnp.pad(jnp.concatenate([w_route_group[i], w_route_expert[i]], axis=1),
                          ((0, 0), (0, LANES - n_route)))
        b_route = jnp.pad(jnp.concatenate([b_route_group[i], b_route_expert[i]]), (0, LANES - n_route))
        x2d, stage, meta, counts = _out_route(x2d, oa.reshape(T, A_W), ob.reshape(T, -1), wo[:A_W], wo[A_W:],
                                              row(ffn_norm[i]), w_route, row(b_route))
        pos, slot_token, tile_group = _dispatch_plan(meta, counts, T)
        ys = _moe(stage, tile_group, slot_token, w_route, row(b_route), wg_all, wu_all, wd_all, i)
        x2d = _ple(x2d, ys, pos, p3, row(ple_norm[i]), w_ple_gate[i].astype(bf16),
                   w_ple_proj[i].astype(bf16), row(final_norm), i, i == depth - 1)
    return x2d.reshape(B, S, D)
```

```python
import functools

import numpy as np
import jax
import jax.numpy as jnp
from jax import lax
from jax.experimental import pallas as pl
from jax.experimental.pallas import tpu as pltpu

D_MODEL = 1024
PLE_DIM = 256
HEAD_DIM = 64
N_HEADS_A = 8
N_HEADS_B = 8
N_KV_B = 2
GQA_REP = N_HEADS_B // N_KV_B
N_HEADS_TOTAL = N_HEADS_A + N_HEADS_B
DILATED_PATTERNS = ((128, 1), (512, 4), (2048, 16))
CMP_BLOCK = 32
CMP_STRIDE = 16
CMP_HIDDEN = 256
SEL_BLOCK = 64
SEL_TOP = 16
N_LOCAL_BLOCKS = 2
WINDOW = 512
N_GROUPS = 4
EXPERTS_PER_GROUP = 4
N_EXPERTS = N_GROUPS * EXPERTS_PER_GROUP
D_EXPERT = 512
RMS_EPS = 1e-6
NEG = -1e30
FORCE_BONUS = 1e4
SCALE = HEAD_DIM ** -0.5

LANES = 128
QB = 128
SEL_CHUNK = 512
NSA_UNITS = 2
DIL_INFLIGHT = 3
POS_LANE0 = HEAD_DIM
POS_SPLIT = 16
SEL_LANE0 = POS_LANE0 + 4
SEL_LANES = 32
MASK_BIG = -(2.0 ** 100)
MOE_TILE = 512
TOKEN_ROWS = 1
TOKEN_W = D_MODEL // TOKEN_ROWS
RANK_SPLIT = 128
A_W = N_HEADS_A * HEAD_DIM
N_MAIN = 3 * A_W + N_HEADS_B * HEAD_DIM + 6 * N_KV_B * HEAD_DIM
N_PROJ = N_MAIN + N_KV_B * LANES
COL_QB = (3 * A_W) // (2 * LANES)
COL_KC = (3 * A_W + N_HEADS_B * HEAD_DIM) // LANES
COL_GATE = N_MAIN // LANES
VMEM_LIMIT = 56 * 1024 * 1024

f32 = jnp.float32
bf16 = jnp.bfloat16


def _dot(a, b):
    return jnp.dot(a, b, preferred_element_type=f32)


def _dot_nt(a, b):
    return lax.dot_general(a, b, (((1,), (1,)), ((), ())), preferred_element_type=f32)


def _rms(x, g):
    return x * lax.rsqrt(jnp.mean(x * x, axis=-1, keepdims=True) + RMS_EPS) * g


def _params(*sem):
    return pltpu.CompilerParams(dimension_semantics=sem, vmem_limit_bytes=VMEM_LIMIT)


def _in_proj_kernel(x_ref, g_ref, w_ref, o_ref):
    h = _rms(x_ref[...], g_ref[...]).astype(bf16)
    for n0 in range(0, N_PROJ, 512):
        o_ref[:, n0:n0 + 512] = _dot(h, w_ref[:, n0:n0 + 512])


def _in_proj(x2d, g, w):
    T = x2d.shape[0]
    tm = 512
    return pl.pallas_call(
        _in_proj_kernel,
        grid=(T // tm,),
        in_specs=[pl.BlockSpec((tm, D_MODEL), lambda i: (i, 0)),
                  pl.BlockSpec((1, D_MODEL), lambda i: (0, 0)),
                  pl.BlockSpec((D_MODEL, N_PROJ), lambda i: (0, 0))],
        out_specs=pl.BlockSpec((tm, N_PROJ), lambda i: (i, 0)),
        out_shape=jax.ShapeDtypeStruct((T, N_PROJ), f32),
        compiler_params=_params("parallel"),
        name="in_proj",
    )(x2d, g, w)


def _dil_kernel(slope_ref, q_ref, k_ref, v_ref, o_ref, out_ref, lse_ref, bias_scr, *bufs, seq):
    hp = pl.program_id(1)
    lane = lax.broadcasted_iota(jnp.int32, (QB, LANES), 1)
    left = lane < HEAD_DIM
    slab = 32
    s_bufs, e_bufs = bufs[:DIL_INFLIGHT], bufs[DIL_INFLIGHT:]

    i = lax.broadcasted_iota(jnp.int32, (QB, 2 * QB), 0)
    j = lax.broadcasted_iota(jnp.int32, (QB, 2 * QB), 1)
    rel = i - j + QB
    valid = (rel >= 0) & (rel <= QB)
    relf = rel.astype(f32)
    for p, (window, dil) in enumerate(DILATED_PATTERNS):
        assert window // dil == QB
        for hh in range(2):
            bias_scr[2 * p + hh] = jnp.where(valid, (-float(dil) * slope_ref[2 * hp + hh]) * relf, NEG)

    ones = jnp.ones((2 * QB, LANES), bf16)

    def scores(u, blk):
        p, dil, row0, key0, nk = blk
        qc = q_ref[pl.ds(row0, QB, stride=dil), :] * SCALE
        q2 = jnp.concatenate([jnp.where(left, qc, 0.0), jnp.where(left, 0.0, qc)], axis=0).astype(bf16)
        s_bufs[u][:, :nk] = _dot_nt(q2, k_ref[pl.ds(key0, nk, stride=dil), :].astype(bf16))

    def probs(u, blk):
        p, dil, row0, key0, nk = blk
        c0 = 2 * QB - nk
        ms = []
        for hh in range(2):
            parts = []
            for sl in range(QB // slab):
                rows = slice(hh * QB + sl * slab, hh * QB + (sl + 1) * slab)
                tiles = [s_bufs[u][rows, c:c + LANES]
                         + bias_scr[2 * p + hh, sl * slab:(sl + 1) * slab, c0 + c:c0 + c + LANES]
                         for c in range(0, nk, LANES)]
                mx = tiles[0]
                for t in tiles[1:]:
                    mx = jnp.maximum(mx, t)
                mx = jnp.broadcast_to(jnp.max(mx, axis=1, keepdims=True), (slab, LANES))
                for c, t in zip(range(0, nk, LANES), tiles):
                    e_bufs[u][rows, c:c + LANES] = jnp.exp(t - mx).astype(bf16)
                parts.append(mx)
            ms.append(jnp.concatenate(parts, axis=0))
        return jnp.where(left, ms[0], ms[1])

    def values(u, blk, row_max):
        p, dil, row0, key0, nk = blk
        v2 = jnp.concatenate([v_ref[pl.ds(key0, nk, stride=dil), :].astype(bf16), ones[:nk]], axis=1)
        res = _dot(e_bufs[u][:, :nk], v2)
        rows = pl.ds(row0, QB, stride=dil)
        den = jnp.where(left, res[:QB, LANES:], res[QB:, LANES:])
        out_ref[p, rows, :] = jnp.where(left, res[:QB, :LANES], res[QB:, :LANES]) / den
        lse_ref[p, rows, :] = row_max + jnp.log(den)

    def run(blocks):
        ahead = DIL_INFLIGHT - 1
        for n in range(min(ahead, len(blocks))):
            scores(n % DIL_INFLIGHT, blocks[n])
        for n, blk in enumerate(blocks):
            if n + ahead < len(blocks):
                scores((n + ahead) % DIL_INFLIGHT, blocks[n + ahead])
            values(n % DIL_INFLIGHT, blk, probs(n % DIL_INFLIGHT, blk))

    def first(p, dil, r):
        return (p, dil, r, r, QB)

    def later(p, dil, r, a):
        return (p, dil, r + dil * QB * a, r + dil * QB * (a - 1), 2 * QB)

    for p, (window, dil) in enumerate(DILATED_PATTERNS):
        nblk = seq // dil // QB
        if dil == 1:
            run([first(p, dil, 0)])

            def trip(t, _, p=p, dil=dil):
                run([later(p, dil, 0, 1 + 3 * t + n) for n in range(3)])
                return 0

            assert (nblk - 1) % 3 == 0
            lax.fori_loop(0, (nblk - 1) // 3, trip, 0)
        elif nblk > 1:
            def per_class(r, _, p=p, dil=dil, nblk=nblk):
                run([first(p, dil, r)] + [later(p, dil, r, a) for a in range(1, nblk)])
                return 0

            lax.fori_loop(0, dil, per_class, 0)
        else:
            def some_classes(t, _, p=p, dil=dil):
                run([first(p, dil, 8 * t + n) for n in range(8)])
                return 0

            lax.fori_loop(0, dil // 8, some_classes, 0)

    ch = 256

    def combine(c, _):
        rows = pl.ds(pl.multiple_of(c * ch, ch), ch)
        lses = [lse_ref[p, rows, :] for p in range(len(DILATED_PATTERNS))]
        big = functools.reduce(jnp.maximum, lses)
        num = jnp.zeros((ch, LANES), f32)
        den = jnp.zeros((ch, LANES), f32)
        for p, lse in enumerate(lses):
            w = jnp.exp(lse - big)
            num = num + w * out_ref[p, rows, :]
            den = den + w
        o_ref[rows, :] = (num / den).astype(o_ref.dtype)
        return 0

    lax.fori_loop(0, seq // ch, combine, 0)


def _dilated(proj3, slopes_a):
    B, S, _ = proj3.shape
    npair = N_HEADS_A // 2
    blk = lambda off: pl.BlockSpec((None, S, LANES), lambda b, hp, off=off: (b, 0, off + hp))
    return pl.pallas_call(
        functools.partial(_dil_kernel, seq=S),
        grid=(B, npair),
        in_specs=[pl.BlockSpec(memory_space=pltpu.SMEM), blk(0), blk(npair), blk(2 * npair)],
        out_specs=pl.BlockSpec((None, S, LANES), lambda b, hp: (b, 0, hp)),
        out_shape=jax.ShapeDtypeStruct((B, S, A_W), bf16),
        scratch_shapes=([pltpu.VMEM((len(DILATED_PATTERNS), S, LANES), f32)] * 2
                        + [pltpu.VMEM((2 * len(DILATED_PATTERNS), QB, 2 * QB), f32)]
                        + [pltpu.VMEM((2 * QB, 2 * QB), f32)] * DIL_INFLIGHT
                        + [pltpu.VMEM((2 * QB, 2 * QB), bf16)] * DIL_INFLIGHT),
        compiler_params=_params("parallel", "parallel"),
        name="dilated_attn",
    )(slopes_a, proj3, proj3, proj3)


def _cmp_kernel(kc_ref, vc_ref, pos_ref, w1k_ref, w2k_ref, w1v_ref, w2v_ref, ko_ref, vo_ref, *, ncb):
    half = CMP_BLOCK // 2
    for x_ref, pi, w1_ref, w2_ref, o_ref in ((kc_ref, 0, w1k_ref, w2k_ref, ko_ref),
                                             (vc_ref, 1, w1v_ref, w2v_ref, vo_ref)):
        lo = jnp.zeros((ncb, 2 * CMP_HIDDEN), f32)
        hi = jnp.zeros((ncb, 2 * CMP_HIDDEN), f32)
        for r in range(half):
            x = x_ref[pl.ds(r, ncb, stride=CMP_STRIDE), :]
            lo = lo + _dot((x + pos_ref[pi, r:r + 1, :]).astype(bf16), w1_ref[r])
            hi = hi + _dot((x + pos_ref[pi, r + half:r + half + 1, :]).astype(bf16), w1_ref[r + half])
        h1 = lo + pltpu.roll(hi, ncb - 1, axis=0)
        o_ref[...] = _dot(jax.nn.gelu(h1).astype(bf16), w2_ref[...])


def _compress(proj3, pos_dup, w1k, w2k, w1v, w2v):
    B, S, _ = proj3.shape
    ncb = S // CMP_STRIDE
    full = lambda a: pl.BlockSpec(a.shape, lambda b: (0,) * a.ndim)
    out = jax.ShapeDtypeStruct((B, ncb, LANES), f32)
    return pl.pallas_call(
        functools.partial(_cmp_kernel, ncb=ncb),
        grid=(B,),
        in_specs=[pl.BlockSpec((None, S, LANES), lambda b: (b, 0, COL_KC)),
                  pl.BlockSpec((None, S, LANES), lambda b: (b, 0, COL_KC + 1)),
                  full(pos_dup), full(w1k), full(w2k), full(w1v), full(w2v)],
        out_specs=[pl.BlockSpec((None, ncb, LANES), lambda b: (b, 0, 0))] * 2,
        out_shape=[out, out],
        compiler_params=_params("parallel"),
        name="nsa_compress",
    )(proj3, proj3, pos_dup, w1k, w2k, w1v, w2v)


def _nsa_kernel(slope_ref, q_ref, ks_ref, vs_ref, kw_ref, vw_ref, gl_ref, kcmp_ref, vcmp_ref,
                ovt_ref, kaug_ref, caug_ref, gsel_ref, o_ref,
                ksa, vsa, kwa, vwa, kca, vca, q_scr, sel_scr, s_m, *per_unit, seq):
    g = pl.program_id(1)
    qi = pl.program_id(2)
    nsel = seq // SEL_BLOCK
    n_top = min(SEL_TOP, nsel)
    assert nsel <= SEL_LANES and N_LOCAL_BLOCKS * SEL_BLOCK >= QB and n_top > N_LOCAL_BLOCKS
    slab = 32
    ur = GQA_REP * QB // NSA_UNITS
    s_w, e_w, s_d, e_d, e_m, m_h, al_h, acc = [per_unit[i * NSA_UNITS:(i + 1) * NSA_UNITS] for i in range(8)]

    def group_lanes(x):
        return jnp.where(g == 0, x, pltpu.roll(x, HEAD_DIM, axis=1))

    @pl.when(qi == 0)
    def _prep():
        ch = 256
        lane_c = lax.broadcasted_iota(jnp.int32, (ch, LANES), 1)
        is_k = lane_c < HEAD_DIM
        for c in range(seq // ch):
            sl = slice(c * ch, (c + 1) * ch)
            aug = kaug_ref[sl, :]
            ksa[sl, :] = jnp.where(is_k, group_lanes(ks_ref[sl, :]), aug).astype(bf16)
            kwa[sl, :] = jnp.where(is_k, group_lanes(kw_ref[sl, :]),
                                   jnp.where(lane_c < SEL_LANE0, aug, 0.0)).astype(bf16)
            for src, dst in ((vs_ref, vsa), (vw_ref, vwa)):
                v = group_lanes(src[sl, :])
                dst[sl, :LANES] = jnp.where(is_k, v, 1.0).astype(bf16)
                dst[sl, LANES:] = jnp.where(is_k, 1.0, pltpu.roll(v, HEAD_DIM, axis=1)).astype(bf16)
        lane_k = lax.broadcasted_iota(jnp.int32, kca.shape, 1)
        kca[...] = jnp.where(lane_k < HEAD_DIM, group_lanes(kcmp_ref[...]), caug_ref[...]).astype(bf16)
        vc = group_lanes(vcmp_ref[...])
        vca[...] = jnp.where(lane_k < HEAD_DIM, vc, pltpu.roll(vc, HEAD_DIM, axis=1)).astype(bf16)

    lane = lax.broadcasted_iota(jnp.int32, (QB, LANES), 1)
    left = lane < HEAD_DIM
    ii = lax.broadcasted_iota(jnp.int32, (QB, LANES), 0)
    t_row = qi * QB + ii
    t_hi = (t_row // POS_SPLIT).astype(f32)
    t_lo = (t_row % POS_SPLIT).astype(f32)
    slopes = [slope_ref[g * GQA_REP + r] for r in range(GQA_REP)]
    head = lambda a, r: a[r * QB:(r + 1) * QB]

    def q_head(r):
        x = q_ref[:, (r // 2) * LANES:(r // 2 + 1) * LANES] * SCALE
        if r % 2:
            x = pltpu.roll(x, HEAD_DIM, axis=1)
        m = slopes[r]
        pos = jnp.where(lane == POS_LANE0, (-POS_SPLIT * m) * t_hi,
                        jnp.where(lane == POS_LANE0 + 1, (-m) * t_lo,
                                  jnp.where(lane == POS_LANE0 + 2, POS_SPLIT * m,
                                            jnp.where(lane == POS_LANE0 + 3, m, 0.0))))
        return jnp.where(left, x, pos)

    q4 = jnp.concatenate([q_head(r) for r in range(GQA_REP)], axis=0)
    q4b = q4.astype(bf16)

    def rel_tile(nk, offset):
        i = lax.broadcasted_iota(jnp.int32, (QB, nk), 0)
        j = lax.broadcasted_iota(jnp.int32, (QB, nk), 1)
        return (i - j + offset).astype(f32)

    def softmax_rows(s_ref, e_ref, m_ref, al_ref, nk, bias, running):
        for sl in range(ur // slab):
            rows = slice(sl * slab, (sl + 1) * slab)
            cols = [slice(j * LANES, (j + 1) * LANES) for j in range(nk // LANES)]
            i0 = (sl * slab) % QB
            tiles = [s_ref[rows, c] if bias is None else s_ref[rows, c] + bias[i0:i0 + slab, c] for c in cols]
            mx = tiles[0]
            for t in tiles[1:]:
                mx = jnp.maximum(mx, t)
            m_new = jnp.broadcast_to(jnp.max(mx, axis=1, keepdims=True), (slab, LANES))
            if running:
                m_old = m_ref[rows, :]
                m_new = jnp.maximum(m_old, m_new)
                al_ref[rows, :] = jnp.exp(m_old - m_new)
            if m_ref is not None:
                m_ref[rows, :] = m_new
            for c, t in zip(cols, tiles):
                e_ref[rows, c] = jnp.exp(t - m_new).astype(bf16)

    unit = lambda a, u: a[u * ur:(u + 1) * ur]

    def own_half(pv, u):
        heads = range(u * ur // QB, (u + 1) * ur // QB)
        return jnp.concatenate([pv[(r - heads[0]) * QB:(r - heads[0] + 1) * QB, (r % 2) * LANES:(r % 2 + 1) * LANES]
                                for r in heads], axis=0)

    s4 = _dot_nt(q4b, kca[...])

    sig = jax.nn.sigmoid(gl_ref[...])
    g_hi = sig.astype(bf16)
    g_mid = (sig - g_hi.astype(f32)).astype(bf16)
    g_lo = (sig - g_hi.astype(f32) - g_mid.astype(f32)).astype(bf16)
    gate_b = _dot(jnp.concatenate([g_hi, g_mid, g_lo], axis=1), gsel_ref[...])

    nwin = WINDOW + QB
    w0 = pl.multiple_of(jnp.maximum(qi - WINDOW // QB, 0) * QB, QB)
    dist_w = rel_tile(nwin, qi * QB - w0)
    bias_w = jnp.where((dist_w >= 0) & (dist_w < WINDOW), 0.0, NEG)
    for u in range(NSA_UNITS):
        s_w[u][...] = _dot_nt(unit(q4b, u), kwa[pl.ds(w0, nwin), :])

    d0 = pl.multiple_of(qi * QB, QB)
    bias_d = jnp.where(rel_tile(QB, 0) >= 0, 0.0, NEG)
    for u in range(NSA_UNITS):
        s_d[u][...] = _dot_nt(unit(q4b, u), ksa[pl.ds(d0, QB), :])

    cmp_end = (lane * CMP_STRIDE + (CMP_BLOCK - 1))
    valid_c = t_row >= cmp_end
    ps = []
    p_sum = jnp.zeros((QB, LANES), f32)
    row_bcast = lambda col: jnp.broadcast_to(col, (QB, LANES))
    for r in range(GQA_REP):
        s = jnp.where(valid_c, head(s4, r), NEG)
        e = jnp.exp(s - row_bcast(jnp.max(s, axis=1, keepdims=True)))
        p = jnp.where(valid_c, e, 0.0) / row_bcast(jnp.sum(e, axis=1, keepdims=True))
        p_sum = p_sum + p
        ps.append(p.astype(bf16))
    o_cmp4 = _dot(jnp.concatenate(ps, axis=0), vca[...])

    o_w = []
    for u in range(NSA_UNITS):
        softmax_rows(s_w[u], e_w[u], None, None, nwin, bias_w, False)
        o_w.append(own_half(_dot(e_w[u][...], vwa[pl.ds(w0, nwin), :]), u))
    for u in range(NSA_UNITS):
        softmax_rows(s_d[u], e_d[u], m_h[u], None, QB, bias_d, False)
        acc[u][...] = own_half(_dot(e_d[u][...], vsa[pl.ds(d0, QB), :]), u)

    back = t_row // SEL_BLOCK - lane
    valid_s = (back >= 0) & (lane < nsel)

    @pl.when(2 * qi + 2 <= n_top)
    def _all_valid():
        sel_scr[...] = jnp.where(valid_s, 1.0, 0.0)

    @pl.when(2 * qi + 2 > n_top)
    def _top_k():
        p_hi = p_sum.astype(bf16)
        p_lo = (p_sum - p_hi.astype(f32)).astype(bf16)
        imp_t = (_dot_nt(ovt_ref[...], p_hi) + _dot_nt(ovt_ref[...], p_lo))[:SEL_LANES]
        blk = lax.broadcasted_iota(jnp.int32, (SEL_LANES, QB), 0)
        tq = qi * QB + lax.broadcasted_iota(jnp.int32, (SEL_LANES, QB), 1)
        back_t = tq // SEL_BLOCK - blk
        valid_t = (back_t >= 0) & (blk < nsel)
        forced = (blk == 0) | (valid_t & (back_t < N_LOCAL_BLOCKS))
        score = jnp.where(valid_t, imp_t + jnp.where(forced, FORCE_BONUS, 0.0), NEG)
        score = jnp.where(blk < nsel, score, 2.0 * NEG)
        rank = jnp.zeros((SEL_LANES, QB), jnp.int32)
        for n in range(nsel):
            row = score[n:n + 1, :]
            ahead = (row > score) | ((row == score) & (blk > n))
            rank = rank + ahead.astype(jnp.int32)
        sel_t = jnp.where((rank < n_top) & valid_t, 1.0, 0.0)
        sel_t = jnp.concatenate([sel_t, jnp.zeros((LANES - SEL_LANES, QB), f32)], axis=0)
        sel_scr[...] = sel_t.T

    sel_bias = jnp.where((sel_scr[...] > 0.5) & (lane < 2 * qi), 0.0, MASK_BIG)
    sel_bias = pltpu.roll(sel_bias, SEL_LANE0, axis=1)
    in_sel = (lane >= SEL_LANE0) & (lane < SEL_LANE0 + SEL_LANES)
    for r in range(GQA_REP):
        q_scr[r * QB:(r + 1) * QB, :] = jnp.where(in_sel, sel_bias, head(q4, r)).astype(bf16)

    per = SEL_CHUNK // QB
    n_chunks = (qi + per - 1) // per

    def sel_scores(kc, half):
        rows = pl.ds(pl.multiple_of(kc * SEL_CHUNK, SEL_CHUNK), SEL_CHUNK)
        s_m[half] = _dot_nt(q_scr[...], ksa[rows, :])

    sel_scores(0, 0)

    def sel_chunk(kc, half):
        sel_scores(jnp.minimum(kc + 1, n_chunks - 1), 1 - half)
        rows = pl.ds(pl.multiple_of(kc * SEL_CHUNK, SEL_CHUNK), SEL_CHUNK)
        for u in range(NSA_UNITS):
            softmax_rows(s_m.at[half, pl.ds(u * ur, ur)], e_m[u], m_h[u], al_h[u], SEL_CHUNK, None, True)
            acc[u][...] = al_h[u][...] * acc[u][...] + own_half(_dot(e_m[u][...], vsa[rows, :]), u)

    def sel_body(t, _):
        sel_chunk(2 * t, 0)

        @pl.when(2 * t + 1 < n_chunks)
        def _odd():
            sel_chunk(2 * t + 1, 1)
        return 0

    lax.fori_loop(0, (n_chunks + 1) // 2, sel_body, 0)

    acc_s = jnp.concatenate([a[...] for a in acc], axis=0)
    acc_w = jnp.concatenate(o_w, axis=0)
    for pr in range(GQA_REP // 2):
        ev, od = 2 * pr, 2 * pr + 1
        out = gate_b[:, (3 * pr) * LANES:(3 * pr + 1) * LANES] * jnp.where(left, head(o_cmp4, ev), head(o_cmp4, od))
        for j, a in ((1, acc_s), (2, acc_w)):
            num = jnp.where(left, head(a, ev), head(a, od))
            den = pltpu.roll(jnp.where(left, head(a, od), head(a, ev)), HEAD_DIM, axis=1)
            out = out + gate_b[:, (3 * pr + j) * LANES:(3 * pr + j + 1) * LANES] / den * num
        o_ref[:, pr * LANES:(pr + 1) * LANES] = out.astype(o_ref.dtype)


def _nsa(proj3, kcmp, vcmp, slopes_b, ovt, kaug, caug, gsel):
    B, S, _ = proj3.shape
    ncb = kcmp.shape[1]
    nrow = GQA_REP * QB
    ur = nrow // NSA_UNITS
    nwin = WINDOW + QB
    kv = lambda j: pl.BlockSpec((None, S, LANES), lambda b, g, qi, j=j: (b, 0, COL_KC + j))
    full = lambda a: pl.BlockSpec(a.shape, lambda b, g, qi: (0,) * a.ndim)
    return pl.pallas_call(
        functools.partial(_nsa_kernel, seq=S),
        grid=(B, N_KV_B, S // QB),
        in_specs=[pl.BlockSpec(memory_space=pltpu.SMEM),
                  pl.BlockSpec((None, QB, 2 * LANES), lambda b, g, qi: (b, qi, COL_QB + g)),
                  kv(2), kv(3), kv(4), kv(5),
                  pl.BlockSpec((None, QB, LANES), lambda b, g, qi: (b, qi, COL_GATE + g)),
                  pl.BlockSpec((None, ncb, LANES), lambda b, g, qi: (b, 0, 0)),
                  pl.BlockSpec((None, ncb, LANES), lambda b, g, qi: (b, 0, 0)),
                  full(ovt), full(kaug), full(caug), full(gsel)],
        out_specs=pl.BlockSpec((None, QB, 2 * LANES), lambda b, g, qi: (b, qi, g)),
        out_shape=jax.ShapeDtypeStruct((B, S, N_HEADS_B * HEAD_DIM), bf16),
        scratch_shapes=([pltpu.VMEM((S, LANES), bf16), pltpu.VMEM((S, 2 * LANES), bf16)] * 2
                        + [pltpu.VMEM((ncb, LANES), bf16)] * 2
                        + [pltpu.VMEM((nrow, LANES), bf16), pltpu.VMEM((QB, LANES), f32),
                           pltpu.VMEM((2, nrow, SEL_CHUNK), f32)]
                        + [pltpu.VMEM((ur, nwin), f32)] * NSA_UNITS + [pltpu.VMEM((ur, nwin), bf16)] * NSA_UNITS
                        + [pltpu.VMEM((ur, QB), f32)] * NSA_UNITS + [pltpu.VMEM((ur, QB), bf16)] * NSA_UNITS
                        + [pltpu.VMEM((ur, SEL_CHUNK), bf16)] * NSA_UNITS
                        + [pltpu.VMEM((ur, LANES), f32)] * (2 * NSA_UNITS)
                        + [pltpu.VMEM((ur, LANES), f32)] * NSA_UNITS),
        compiler_params=_params("parallel", "parallel", "arbitrary"),
        name="nsa_attn",
    )(slopes_b, proj3, proj3, proj3, proj3, proj3, proj3, kcmp, vcmp, ovt, kaug, caug, gsel)


def _route(h, w_ref, b_ref):
    w = w_ref[...]
    h_hi, w_hi = h.astype(bf16), w.astype(bf16)
    h_lo, w_lo = (h - h_hi.astype(f32)).astype(bf16), (w - w_hi.astype(f32)).astype(bf16)
    logit = _dot(h_hi, w_hi) + (_dot(h_hi, w_lo) + _dot(h_lo, w_hi)) + b_ref[...]
    tm = logit.shape[0]
    lane = lax.broadcasted_iota(jnp.int32, (tm, LANES), 1)
    big = jnp.int32(LANES)
    is_g = lane < N_GROUPS
    gl = jnp.where(is_g, logit, NEG)
    gmax = jnp.max(gl, axis=1, keepdims=True)
    gsum = jnp.sum(jnp.where(is_g, jnp.exp(gl - gmax), 0.0), axis=1, keepdims=True)
    gsel = jnp.min(jnp.where(is_g & (gl == gmax), lane, big), axis=1, keepdims=True)
    gw = 1.0 / gsum
    e_lane = lane - N_GROUPS
    in_grp = (e_lane >= 0) & (e_lane < N_EXPERTS) & (e_lane // EXPERTS_PER_GROUP == gsel)
    el = jnp.where(in_grp, logit, NEG)
    t1 = jnp.max(el, axis=1, keepdims=True)
    i1 = jnp.min(jnp.where(in_grp & (el == t1), lane, big), axis=1, keepdims=True)
    el2 = jnp.where(lane == i1, NEG, el)
    t2 = jnp.max(el2, axis=1, keepdims=True)
    i2 = jnp.min(jnp.where(in_grp & (lane != i1) & (el2 == t2), lane, big), axis=1, keepdims=True)
    e2 = jnp.exp(t2 - t1)
    w1 = gw / (1.0 + e2)
    w2 = gw * e2 / (1.0 + e2)
    return jnp.where(lane == i1, w1, jnp.where(lane == i2, w2, 0.0)), gsel


def _to_token_tiles(ref, x):
    if TOKEN_ROWS == 1:
        ref[...] = x
        return
    for s in range(TOKEN_ROWS):
        ref[pl.ds(s, x.shape[0], stride=TOKEN_ROWS), :] = x[:, s * TOKEN_W:(s + 1) * TOKEN_W]


def _from_token_tiles(ref):
    if TOKEN_ROWS == 1:
        return ref[...]
    n = TOKEN_ROWS
    return jnp.concatenate([ref[pl.ds(s, ref.shape[0] // n, stride=n), :] for s in range(n)], axis=1)


def _out_route_kernel(x_ref, oa_ref, ob_ref, wa_ref, wb_ref, g_ref, w_ref, b_ref, tri_ref,
                      x1_ref, stage_ref, meta_ref, cnt_ref, cnt_scr):
    @pl.when(pl.program_id(0) == 0)
    def _zero():
        cnt_scr[...] = jnp.zeros_like(cnt_scr)

    x1 = x_ref[...] + _dot(oa_ref[...], wa_ref[...]) + _dot(ob_ref[...], wb_ref[...])
    x1_ref[...] = x1
    h = _rms(x1, g_ref[...])
    _to_token_tiles(stage_ref, h)
    _, gsel = _route(h, w_ref, b_ref)
    tm = h.shape[0]
    lane = lax.broadcasted_iota(jnp.int32, (tm, LANES), 1)
    is_g = lane < N_GROUPS

    onehot = jnp.where(is_g & (lane == gsel), 1.0, 0.0)
    before = _dot(tri_ref[...], onehot.astype(bf16)) + cnt_scr[...]
    rank = jnp.sum(onehot * before, axis=1, keepdims=True)
    cnt_scr[...] = before[tm - 1:tm, :] + onehot[tm - 1:tm, :]
    cnt_ref[...] = cnt_scr[...]
    rank_hi = jnp.floor(rank * (1.0 / RANK_SPLIT))
    cols = jnp.where(lane == 0, gsel.astype(f32), jnp.where(lane == 1, rank_hi,
                                                            jnp.where(lane == 2, rank - RANK_SPLIT * rank_hi, 0.0)))
    pick = (lax.broadcasted_iota(jnp.int32, (8, LANES), 0) == lax.broadcasted_iota(jnp.int32, (8, LANES), 1))
    meta_ref[...] = _dot_nt(jnp.where(pick, 1.0, 0.0).astype(bf16), cols.astype(bf16))


def _out_route(x2d, oa, ob, wa, wb, g, w, b):
    T = x2d.shape[0]
    tm = MOE_TILE
    assert T // RANK_SPLIT <= 256
    tri = jnp.asarray(np.tril(np.ones((tm, tm), np.float32), -1), bf16)
    row = lambda n: pl.BlockSpec((tm, n), lambda i: (i, 0))
    full = lambda a: pl.BlockSpec(a.shape, lambda i: (0,) * a.ndim)
    return pl.pallas_call(
        _out_route_kernel,
        grid=(T // tm,),
        in_specs=[row(D_MODEL), row(oa.shape[1]), row(ob.shape[1]), full(wa), full(wb),
                  full(g), full(w), full(b), full(tri)],
        out_specs=[row(D_MODEL),
                   pl.BlockSpec((tm * TOKEN_ROWS, TOKEN_W), lambda i: (i, 0)),
                   pl.BlockSpec((None, 8, tm), lambda i: (i, 0, 0)),
                   pl.BlockSpec((1, LANES), lambda i: (0, 0))],
        out_shape=[jax.ShapeDtypeStruct((T, D_MODEL), f32),
                   jax.ShapeDtypeStruct((T * TOKEN_ROWS, TOKEN_W), f32),
                   jax.ShapeDtypeStruct((T // tm, 8, tm), f32),
                   jax.ShapeDtypeStruct((1, LANES), f32)],
        scratch_shapes=[pltpu.VMEM((1, LANES), f32)],
        compiler_params=_params("arbitrary"),
        name="out_proj_router",
    )(x2d, oa, ob, wa, wb, g, w, b, tri)


def _token_rows(t, n=1):
    return pl.ds(pl.multiple_of(t * TOKEN_ROWS, TOKEN_ROWS), n * TOKEN_ROWS)


def _gathered_tile(idx_ref, src_ref, buf_ref, sems, tm):
    i = pl.program_id(0)
    n = pl.num_programs(0)

    def issue(step):
        slot = step % 2

        def one(k, _):
            pltpu.make_async_copy(src_ref.at[_token_rows(idx_ref[step * tm + k])],
                                  buf_ref.at[slot, _token_rows(k)], sems.at[slot]).start()
            return 0

        lax.fori_loop(0, tm, one, 0, unroll=8)

    @pl.when(i == 0)
    def _first():
        issue(i)

    @pl.when(i + 1 < n)
    def _next():
        issue(i + 1)

    slot = i % 2
    pltpu.make_async_copy(src_ref.at[_token_rows(0, tm)], buf_ref.at[slot], sems.at[slot]).wait()
    return slot


def _moe_kernel(tg_ref, src_ref, stage_ref, wr_ref, br_ref, wg_ref, wu_ref, wd_ref, o_ref, buf_ref, sems):
    grp = tg_ref[pl.program_id(0)]
    slot = _gathered_tile(src_ref, stage_ref, buf_ref, sems, MOE_TILE)
    h = _from_token_tiles(buf_ref.at[slot])
    gates, _ = _route(h, wr_ref, br_ref)
    x = h.astype(bf16)
    lane = lax.broadcasted_iota(jnp.int32, gates.shape, 1)
    y = jnp.zeros(h.shape, f32)
    for e in range(EXPERTS_PER_GROUP):
        gcol = jnp.sum(jnp.where(lane == N_GROUPS + EXPERTS_PER_GROUP * grp + e, gates, 0.0), axis=1, keepdims=True)
        a = jax.nn.silu(_dot(x, wg_ref[e])) * _dot(x, wu_ref[e])
        y = y + _dot((a * gcol).astype(bf16), wd_ref[e])
    _to_token_tiles(o_ref, y)


def _moe(stage, tile_group, slot_token, w_route, b_route, wg, wu, wd, layer):
    tm = MOE_TILE
    n_slots = slot_token.shape[0]
    w_spec = lambda k, n: pl.BlockSpec((None, None, EXPERTS_PER_GROUP, k, n),
                                       lambda j, tg, src: (layer, tg[j], 0, 0, 0))
    grouped = lambda w: w.reshape(w.shape[0], N_GROUPS, EXPERTS_PER_GROUP, *w.shape[2:])
    return pl.pallas_call(
        _moe_kernel,
        grid_spec=pltpu.PrefetchScalarGridSpec(
            num_scalar_prefetch=2, grid=(n_slots // tm,),
            in_specs=[pl.BlockSpec(memory_space=pl.ANY),
                      pl.BlockSpec((D_MODEL, LANES), lambda j, tg, src: (0, 0)),
                      pl.BlockSpec((1, LANES), lambda j, tg, src: (0, 0)),
                      w_spec(D_MODEL, D_EXPERT), w_spec(D_MODEL, D_EXPERT), w_spec(D_EXPERT, D_MODEL)],
            out_specs=pl.BlockSpec((tm * TOKEN_ROWS, TOKEN_W), lambda j, tg, src: (j, 0)),
            scratch_shapes=[pltpu.VMEM((2, tm * TOKEN_ROWS, TOKEN_W), f32), pltpu.SemaphoreType.DMA((2,))]),
        out_shape=jax.ShapeDtypeStruct((n_slots * TOKEN_ROWS, TOKEN_W), f32),
        compiler_params=_params("arbitrary"),
        name="moe_ffn",
    )(tile_group, slot_token, stage, w_route, b_route, grouped(wg), grouped(wu), grouped(wd))


def _ple_kernel(pos_ref, x_ref, ys_ref, p_ref, g_ref, wg_ref, wp_ref, fg_ref, o_ref, buf_ref, sems, *, final):
    slot = _gathered_tile(pos_ref, ys_ref, buf_ref, sems, x_ref.shape[0])
    x = x_ref[...] + _from_token_tiles(buf_ref.at[slot])
    gate = jax.nn.sigmoid(_dot(_rms(x, g_ref[...]).astype(bf16), wg_ref[...]))
    y = x + gate * _dot(p_ref[...].astype(bf16), wp_ref[...])
    o_ref[...] = _rms(y, fg_ref[...]) if final else y


def _ple(x2d, ys, pos, p3, g, wg, wp, fg, layer, final):
    T = x2d.shape[0]
    tm = 512
    full = lambda a: pl.BlockSpec(a.shape, lambda i, pos: (0,) * a.ndim)
    return pl.pallas_call(
        functools.partial(_ple_kernel, final=final),
        grid_spec=pltpu.PrefetchScalarGridSpec(
            num_scalar_prefetch=1, grid=(T // tm,),
            in_specs=[pl.BlockSpec((tm, D_MODEL), lambda i, pos: (i, 0)),
                      pl.BlockSpec(memory_space=pl.ANY),
                      pl.BlockSpec((None, tm, PLE_DIM), lambda i, pos: (layer, i, 0)),
                      full(g), full(wg), full(wp), full(fg)],
            out_specs=pl.BlockSpec((tm, D_MODEL), lambda i, pos: (i, 0)),
            scratch_shapes=[pltpu.VMEM((2, tm * TOKEN_ROWS, TOKEN_W), f32), pltpu.SemaphoreType.DMA((2,))]),
        out_shape=jax.ShapeDtypeStruct((T, D_MODEL), f32),
        compiler_params=_params("arbitrary"),
        name="ple",
    )(pos, x2d, ys, p3, g, wg, wp, fg)


def _alibi_slopes():
    s = 2.0 ** (-8.0 * np.arange(1, N_HEADS_TOTAL + 1) / N_HEADS_TOTAL)
    assert np.all(np.log2(s[1::2]) == np.round(np.log2(s[1::2])))
    return jnp.asarray(s[0::2], f32), jnp.asarray(s[1::2], f32)


def _selection_constants(seq):
    ncb = seq // CMP_STRIDE
    nsel = seq // SEL_BLOCK
    n_cmp = (seq - CMP_BLOCK) // CMP_STRIDE + 1
    cs = np.arange(n_cmp) * CMP_STRIDE
    bs = np.arange(nsel) * SEL_BLOCK
    ov = np.clip(np.minimum(cs[:, None] + CMP_BLOCK, bs[None, :] + SEL_BLOCK)
                 - np.maximum(cs[:, None], bs[None, :]), 0, None) / CMP_BLOCK
    assert ncb == LANES and seq <= POS_SPLIT * 256
    ovt = np.zeros((LANES, ncb), np.float32)
    ovt[:nsel, :n_cmp] = ov.T
    pos = np.arange(seq)
    kaug = np.zeros((seq, LANES), np.float32)
    kaug[:, POS_LANE0:POS_LANE0 + 2] = 1.0
    kaug[:, POS_LANE0 + 2] = pos // POS_SPLIT
    kaug[:, POS_LANE0 + 3] = pos % POS_SPLIT
    kaug[pos, SEL_LANE0 + pos // SEL_BLOCK] = 1.0
    cend = np.arange(ncb) * CMP_STRIDE + CMP_BLOCK - 1
    caug = np.zeros((ncb, LANES), np.float32)
    caug[:, POS_LANE0:POS_LANE0 + 2] = 1.0
    caug[:, POS_LANE0 + 2] = cend // POS_SPLIT
    caug[:, POS_LANE0 + 3] = cend % POS_SPLIT
    gsel = np.zeros((3, LANES, GQA_REP // 2, 3, 2, HEAD_DIM), np.float32)
    for pr in range(GQA_REP // 2):
        for j in range(3):
            for hh in range(2):
                gsel[:, 3 * (2 * pr + hh) + j, pr, j, hh, :] = 1.0
    gsel = gsel.reshape(3 * LANES, (GQA_REP // 2) * 3 * LANES)
    return jnp.asarray(ovt, bf16), jnp.asarray(kaug), jnp.asarray(caug), jnp.asarray(gsel, bf16)


def _dispatch_plan(meta, counts, n_tokens):
    i32 = jnp.int32
    n_slots = n_tokens + N_GROUPS * MOE_TILE
    group = meta[:, 0, :].reshape(n_tokens).astype(i32)
    rank = (meta[:, 1, :] * RANK_SPLIT + meta[:, 2, :]).reshape(n_tokens).astype(i32)
    cnt = counts[0, :N_GROUPS].astype(i32)
    padded = (cnt + MOE_TILE - 1) // MOE_TILE * MOE_TILE
    end = jnp.cumsum(padded)
    pos = (end - padded)[group] + rank
    slot_token = jnp.zeros((n_slots,), i32).at[pos].set(jnp.arange(n_tokens, dtype=i32))
    tile_start = jnp.arange(n_slots // MOE_TILE, dtype=i32) * MOE_TILE
    tile_group = jnp.minimum(jnp.sum(tile_start[:, None] >= end[None, :], axis=1), N_GROUPS - 1).astype(i32)
    return pos, slot_token, tile_group


def _block_diag2(w):
    z = jnp.zeros_like(w)
    return jnp.concatenate([jnp.concatenate([w, z], axis=-1), jnp.concatenate([z, w], axis=-1)], axis=-2)


def _layout_w_in(w):
    gate = w[:, N_MAIN:]
    per = GQA_REP * 3
    blocks = [jnp.pad(gate[:, g * per:(g + 1) * per], ((0, 0), (0, LANES - per))) for g in range(N_KV_B)]
    return jnp.concatenate([w[:, :N_MAIN]] + blocks, axis=1).astype(bf16)


def kernel(x, p, attn_norm, w_in, w_out, w_cmp_k1, w_cmp_k2, w_cmp_v1, w_cmp_v2, cmp_pos, ffn_norm, w_route_group, b_route_group, w_route_expert, b_route_expert, w_expert_gate, w_expert_up, w_expert_down, ple_norm, w_ple_gate, w_ple_proj, final_norm):
    B, S, D = x.shape
    depth = w_in.shape[0]
    T = B * S
    slopes_a, slopes_b = _alibi_slopes()
    ovt, kaug, caug, gsel = _selection_constants(S)
    wg_all = w_expert_gate.astype(bf16)
    wu_all = w_expert_up.astype(bf16)
    wd_all = w_expert_down.astype(bf16)
    p3 = p.reshape(depth, T, PLE_DIM)
    row = lambda v: v.reshape(1, -1)
    n_route = N_GROUPS + N_EXPERTS

    x2d = x.reshape(T, D)
    for i in range(depth):
        proj3 = _in_proj(x2d, row(attn_norm[i]), _layout_w_in(w_in[i])).reshape(B, S, N_PROJ)
        oa = _dilated(proj3, slopes_a)
        w1 = lambda w: _block_diag2(w.reshape(CMP_BLOCK, HEAD_DIM, CMP_HIDDEN)).astype(bf16)
        pos_dup = jnp.concatenate([cmp_pos[i], cmp_pos[i]], axis=-1)
        kcmp, vcmp = _compress(proj3, pos_dup, w1(w_cmp_k1[i]), _block_diag2(w_cmp_k2[i]).astype(bf16),
                               w1(w_cmp_v1[i]), _block_diag2(w_cmp_v2[i]).astype(bf16))
        ob = _nsa(proj3, kcmp, vcmp, slopes_b, ovt, kaug, caug, gsel)
        wo = w_out[i].astype(bf16)
        w_route = jnp.pad(jnp.concatenate([w_route_group[i], w_route_expert[i]], axis=1),
                          ((0, 0), (0, LANES - n_route)))
        b_route = jnp.pad(jnp.concatenate([b_route_group[i], b_route_expert[i]]), (0, LANES - n_route))
        x2d, stage, meta, counts = _out_route(x2d, oa.reshape(T, A_W), ob.reshape(T, -1), wo[:A_W], wo[A_W:],
                                              row(ffn_norm[i]), w_route, row(b_route))
        pos, slot_token, tile_group = _dispatch_plan(meta, counts, T)
        ys = _moe(stage, tile_group, slot_token, w_route, row(b_route), wg_all, wu_all, wd_all, i)
        x2d = _ple(x2d, ys, pos, p3, row(ple_norm[i]), w_ple_gate[i].astype(bf16),
                   w_ple_proj[i].astype(bf16), row(final_norm), i, i == depth - 1)
    return x2d.reshape(B, S, D)
```

```python
import functools

import numpy as np
import jax
import jax.numpy as jnp
from jax import lax
from jax.experimental import pallas as pl
from jax.experimental.pallas import tpu as pltpu

D_MODEL = 1024
PLE_DIM = 256
HEAD_DIM = 64
N_HEADS_A = 8
N_HEADS_B = 8
N_KV_B = 2
GQA_REP = N_HEADS_B // N_KV_B
N_HEADS_TOTAL = N_HEADS_A + N_HEADS_B
DILATED_PATTERNS = ((128, 1), (512, 4), (2048, 16))
CMP_BLOCK = 32
CMP_STRIDE = 16
CMP_HIDDEN = 256
SEL_BLOCK = 64
SEL_TOP = 16
N_LOCAL_BLOCKS = 2
WINDOW = 512
N_GROUPS = 4
EXPERTS_PER_GROUP = 4
N_EXPERTS = N_GROUPS * EXPERTS_PER_GROUP
D_EXPERT = 512
RMS_EPS = 1e-6
NEG = -1e30
FORCE_BONUS = 1e4
SCALE = HEAD_DIM ** -0.5

LANES = 128
QB = 128
SEL_CHUNK = 512
NSA_UNITS = 2
DIL_INFLIGHT = 3
POS_LANE0 = HEAD_DIM
POS_SPLIT = 16
SEL_LANE0 = POS_LANE0 + 4
SEL_LANES = 32
MASK_BIG = -(2.0 ** 100)
MOE_TILE = 512
TOKEN_ROWS = D_MODEL // LANES
RANK_SPLIT = 128
A_W = N_HEADS_A * HEAD_DIM
N_MAIN = 3 * A_W + N_HEADS_B * HEAD_DIM + 6 * N_KV_B * HEAD_DIM
N_PROJ = N_MAIN + N_KV_B * LANES
COL_QB = (3 * A_W) // (2 * LANES)
COL_KC = (3 * A_W + N_HEADS_B * HEAD_DIM) // LANES
COL_GATE = N_MAIN // LANES
VMEM_LIMIT = 56 * 1024 * 1024

f32 = jnp.float32
bf16 = jnp.bfloat16


def _dot(a, b):
    return jnp.dot(a, b, preferred_element_type=f32)


def _dot_nt(a, b):
    return lax.dot_general(a, b, (((1,), (1,)), ((), ())), preferred_element_type=f32)


def _rms(x, g):
    return x * lax.rsqrt(jnp.mean(x * x, axis=-1, keepdims=True) + RMS_EPS) * g


def _params(*sem):
    return pltpu.CompilerParams(dimension_semantics=sem, vmem_limit_bytes=VMEM_LIMIT)


def _in_proj_kernel(x_ref, g_ref, w_ref, o_ref):
    h = _rms(x_ref[...], g_ref[...]).astype(bf16)
    for n0 in range(0, N_PROJ, 512):
        o_ref[:, n0:n0 + 512] = _dot(h, w_ref[:, n0:n0 + 512])


def _in_proj(x2d, g, w):
    T = x2d.shape[0]
    tm = 512
    return pl.pallas_call(
        _in_proj_kernel,
        grid=(T // tm,),
        in_specs=[pl.BlockSpec((tm, D_MODEL), lambda i: (i, 0)),
                  pl.BlockSpec((1, D_MODEL), lambda i: (0, 0)),
                  pl.BlockSpec((D_MODEL, N_PROJ), lambda i: (0, 0))],
        out_specs=pl.BlockSpec((tm, N_PROJ), lambda i: (i, 0)),
        out_shape=jax.ShapeDtypeStruct((T, N_PROJ), f32),
        compiler_params=_params("parallel"),
        name="in_proj",
    )(x2d, g, w)


def _dil_kernel(slope_ref, q_ref, k_ref, v_ref, o_ref, out_ref, lse_ref, bias_scr, *bufs, seq):
    hp = pl.program_id(1)
    lane = lax.broadcasted_iota(jnp.int32, (QB, LANES), 1)
    left = lane < HEAD_DIM
    slab = 32
    s_bufs, e_bufs = bufs[:DIL_INFLIGHT], bufs[DIL_INFLIGHT:]

    i = lax.broadcasted_iota(jnp.int32, (QB, 2 * QB), 0)
    j = lax.broadcasted_iota(jnp.int32, (QB, 2 * QB), 1)
    rel = i - j + QB
    valid = (rel >= 0) & (rel <= QB)
    relf = rel.astype(f32)
    for p, (window, dil) in enumerate(DILATED_PATTERNS):
        assert window // dil == QB
        for hh in range(2):
            bias_scr[2 * p + hh] = jnp.where(valid, (-float(dil) * slope_ref[2 * hp + hh]) * relf, NEG)

    ones = jnp.ones((2 * QB, LANES), bf16)

    def scores(u, blk):
        p, dil, row0, key0, nk = blk
        qc = q_ref[pl.ds(row0, QB, stride=dil), :] * SCALE
        q2 = jnp.concatenate([jnp.where(left, qc, 0.0), jnp.where(left, 0.0, qc)], axis=0).astype(bf16)
        s_bufs[u][:, :nk] = _dot_nt(q2, k_ref[pl.ds(key0, nk, stride=dil), :].astype(bf16))

    def probs(u, blk):
        p, dil, row0, key0, nk = blk
        c0 = 2 * QB - nk
        ms = []
        for hh in range(2):
            parts = []
            for sl in range(QB // slab):
                rows = slice(hh * QB + sl * slab, hh * QB + (sl + 1) * slab)
                tiles = [s_bufs[u][rows, c:c + LANES]
                         + bias_scr[2 * p + hh, sl * slab:(sl + 1) * slab, c0 + c:c0 + c + LANES]
                         for c in range(0, nk, LANES)]
                mx = tiles[0]
                for t in tiles[1:]:
                    mx = jnp.maximum(mx, t)
                mx = jnp.broadcast_to(jnp.max(mx, axis=1, keepdims=True), (slab, LANES))
                for c, t in zip(range(0, nk, LANES), tiles):
                    e_bufs[u][rows, c:c + LANES] = jnp.exp(t - mx).astype(bf16)
                parts.append(mx)
            ms.append(jnp.concatenate(parts, axis=0))
        return jnp.where(left, ms[0], ms[1])

    def values(u, blk, row_max):
        p, dil, row0, key0, nk = blk
        v2 = jnp.concatenate([v_ref[pl.ds(key0, nk, stride=dil), :].astype(bf16), ones[:nk]], axis=1)
        res = _dot(e_bufs[u][:, :nk], v2)
        rows = pl.ds(row0, QB, stride=dil)
        den = jnp.where(left, res[:QB, LANES:], res[QB:, LANES:])
        out_ref[p, rows, :] = jnp.where(left, res[:QB, :LANES], res[QB:, :LANES]) / den
        lse_ref[p, rows, :] = row_max + jnp.log(den)

    def run(blocks):
        ahead = DIL_INFLIGHT - 1
        for n in range(min(ahead, len(blocks))):
            scores(n % DIL_INFLIGHT, blocks[n])
        for n, blk in enumerate(blocks):
            if n + ahead < len(blocks):
                scores((n + ahead) % DIL_INFLIGHT, blocks[n + ahead])
            values(n % DIL_INFLIGHT, blk, probs(n % DIL_INFLIGHT, blk))

    def first(p, dil, r):
        return (p, dil, r, r, QB)

    def later(p, dil, r, a):
        return (p, dil, r + dil * QB * a, r + dil * QB * (a - 1), 2 * QB)

    for p, (window, dil) in enumerate(DILATED_PATTERNS):
        nblk = seq // dil // QB
        if dil == 1:
            run([first(p, dil, 0)])

            def trip(t, _, p=p, dil=dil):
                run([later(p, dil, 0, 1 + 3 * t + n) for n in range(3)])
                return 0

            assert (nblk - 1) % 3 == 0
            lax.fori_loop(0, (nblk - 1) // 3, trip, 0)
        elif nblk > 1:
            def per_class(r, _, p=p, dil=dil, nblk=nblk):
                run([first(p, dil, r)] + [later(p, dil, r, a) for a in range(1, nblk)])
                return 0

            lax.fori_loop(0, dil, per_class, 0)
        else:
            def some_classes(t, _, p=p, dil=dil):
                run([first(p, dil, 8 * t + n) for n in range(8)])
                return 0

            lax.fori_loop(0, dil // 8, some_classes, 0)

    ch = 256

    def combine(c, _):
        rows = pl.ds(pl.multiple_of(c * ch, ch), ch)
        lses = [lse_ref[p, rows, :] for p in range(len(DILATED_PATTERNS))]
        big = functools.reduce(jnp.maximum, lses)
        num = jnp.zeros((ch, LANES), f32)
        den = jnp.zeros((ch, LANES), f32)
        for p, lse in enumerate(lses):
            w = jnp.exp(lse - big)
            num = num + w * out_ref[p, rows, :]
            den = den + w
        o_ref[rows, :] = (num / den).astype(o_ref.dtype)
        return 0

    lax.fori_loop(0, seq // ch, combine, 0)


def _dilated(proj3, slopes_a):
    B, S, _ = proj3.shape
    npair = N_HEADS_A // 2
    blk = lambda off: pl.BlockSpec((None, S, LANES), lambda b, hp, off=off: (b, 0, off + hp))
    return pl.pallas_call(
        functools.partial(_dil_kernel, seq=S),
        grid=(B, npair),
        in_specs=[pl.BlockSpec(memory_space=pltpu.SMEM), blk(0), blk(npair), blk(2 * npair)],
        out_specs=pl.BlockSpec((None, S, LANES), lambda b, hp: (b, 0, hp)),
        out_shape=jax.ShapeDtypeStruct((B, S, A_W), bf16),
        scratch_shapes=([pltpu.VMEM((len(DILATED_PATTERNS), S, LANES), f32)] * 2
                        + [pltpu.VMEM((2 * len(DILATED_PATTERNS), QB, 2 * QB), f32)]
                        + [pltpu.VMEM((2 * QB, 2 * QB), f32)] * DIL_INFLIGHT
                        + [pltpu.VMEM((2 * QB, 2 * QB), bf16)] * DIL_INFLIGHT),
        compiler_params=_params("parallel", "parallel"),
        name="dilated_attn",
    )(slopes_a, proj3, proj3, proj3)


def _cmp_kernel(kc_ref, vc_ref, pos_ref, w1k_ref, w2k_ref, w1v_ref, w2v_ref, ko_ref, vo_ref, *, ncb):
    half = CMP_BLOCK // 2
    for x_ref, pi, w1_ref, w2_ref, o_ref in ((kc_ref, 0, w1k_ref, w2k_ref, ko_ref),
                                             (vc_ref, 1, w1v_ref, w2v_ref, vo_ref)):
        lo = jnp.zeros((ncb, 2 * CMP_HIDDEN), f32)
        hi = jnp.zeros((ncb, 2 * CMP_HIDDEN), f32)
        for r in range(half):
            x = x_ref[pl.ds(r, ncb, stride=CMP_STRIDE), :]
            lo = lo + _dot((x + pos_ref[pi, r:r + 1, :]).astype(bf16), w1_ref[r])
            hi = hi + _dot((x + pos_ref[pi, r + half:r + half + 1, :]).astype(bf16), w1_ref[r + half])
        h1 = lo + pltpu.roll(hi, ncb - 1, axis=0)
        o_ref[...] = _dot(jax.nn.gelu(h1).astype(bf16), w2_ref[...])


def _compress(proj3, pos_dup, w1k, w2k, w1v, w2v):
    B, S, _ = proj3.shape
    ncb = S // CMP_STRIDE
    full = lambda a: pl.BlockSpec(a.shape, lambda b: (0,) * a.ndim)
    out = jax.ShapeDtypeStruct((B, ncb, LANES), f32)
    return pl.pallas_call(
        functools.partial(_cmp_kernel, ncb=ncb),
        grid=(B,),
        in_specs=[pl.BlockSpec((None, S, LANES), lambda b: (b, 0, COL_KC)),
                  pl.BlockSpec((None, S, LANES), lambda b: (b, 0, COL_KC + 1)),
                  full(pos_dup), full(w1k), full(w2k), full(w1v), full(w2v)],
        out_specs=[pl.BlockSpec((None, ncb, LANES), lambda b: (b, 0, 0))] * 2,
        out_shape=[out, out],
        compiler_params=_params("parallel"),
        name="nsa_compress",
    )(proj3, proj3, pos_dup, w1k, w2k, w1v, w2v)


def _nsa_kernel(slope_ref, q_ref, ks_ref, vs_ref, kw_ref, vw_ref, gl_ref, kcmp_ref, vcmp_ref,
                ovt_ref, kaug_ref, caug_ref, gsel_ref, o_ref,
                ksa, vsa, kwa, vwa, kca, vca, q_scr, sel_scr, s_m, *per_unit, seq):
    g = pl.program_id(1)
    qi = pl.program_id(2)
    nsel = seq // SEL_BLOCK
    n_top = min(SEL_TOP, nsel)
    assert nsel <= SEL_LANES and N_LOCAL_BLOCKS * SEL_BLOCK >= QB and n_top > N_LOCAL_BLOCKS
    slab = 32
    ur = GQA_REP * QB // NSA_UNITS
    s_w, e_w, s_d, e_d, e_m, m_h, al_h, acc = [per_unit[i * NSA_UNITS:(i + 1) * NSA_UNITS] for i in range(8)]

    def group_lanes(x):
        return jnp.where(g == 0, x, pltpu.roll(x, HEAD_DIM, axis=1))

    @pl.when(qi == 0)
    def _prep():
        ch = 256
        lane_c = lax.broadcasted_iota(jnp.int32, (ch, LANES), 1)
        is_k = lane_c < HEAD_DIM
        for c in range(seq // ch):
            sl = slice(c * ch, (c + 1) * ch)
            aug = kaug_ref[sl, :]
            ksa[sl, :] = jnp.where(is_k, group_lanes(ks_ref[sl, :]), aug).astype(bf16)
            kwa[sl, :] = jnp.where(is_k, group_lanes(kw_ref[sl, :]),
                                   jnp.where(lane_c < SEL_LANE0, aug, 0.0)).astype(bf16)
            for src, dst in ((vs_ref, vsa), (vw_ref, vwa)):
                v = group_lanes(src[sl, :])
                dst[sl, :LANES] = jnp.where(is_k, v, 1.0).astype(bf16)
                dst[sl, LANES:] = jnp.where(is_k, 1.0, pltpu.roll(v, HEAD_DIM, axis=1)).astype(bf16)
        lane_k = lax.broadcasted_iota(jnp.int32, kca.shape, 1)
        kca[...] = jnp.where(lane_k < HEAD_DIM, group_lanes(kcmp_ref[...]), caug_ref[...]).astype(bf16)
        vc = group_lanes(vcmp_ref[...])
        vca[...] = jnp.where(lane_k < HEAD_DIM, vc, pltpu.roll(vc, HEAD_DIM, axis=1)).astype(bf16)

    lane = lax.broadcasted_iota(jnp.int32, (QB, LANES), 1)
    left = lane < HEAD_DIM
    ii = lax.broadcasted_iota(jnp.int32, (QB, LANES), 0)
    t_row = qi * QB + ii
    t_hi = (t_row // POS_SPLIT).astype(f32)
    t_lo = (t_row % POS_SPLIT).astype(f32)
    slopes = [slope_ref[g * GQA_REP + r] for r in range(GQA_REP)]
    head = lambda a, r: a[r * QB:(r + 1) * QB]

    def q_head(r):
        x = q_ref[:, (r // 2) * LANES:(r // 2 + 1) * LANES] * SCALE
        if r % 2:
            x = pltpu.roll(x, HEAD_DIM, axis=1)
        m = slopes[r]
        pos = jnp.where(lane == POS_LANE0, (-POS_SPLIT * m) * t_hi,
                        jnp.where(lane == POS_LANE0 + 1, (-m) * t_lo,
                                  jnp.where(lane == POS_LANE0 + 2, POS_SPLIT * m,
                                            jnp.where(lane == POS_LANE0 + 3, m, 0.0))))
        return jnp.where(left, x, pos)

    q4 = jnp.concatenate([q_head(r) for r in range(GQA_REP)], axis=0)
    q4b = q4.astype(bf16)

    def rel_tile(nk, offset):
        i = lax.broadcasted_iota(jnp.int32, (QB, nk), 0)
        j = lax.broadcasted_iota(jnp.int32, (QB, nk), 1)
        return (i - j + offset).astype(f32)

    def softmax_rows(s_ref, e_ref, m_ref, al_ref, nk, bias, running):
        for sl in range(ur // slab):
            rows = slice(sl * slab, (sl + 1) * slab)
            cols = [slice(j * LANES, (j + 1) * LANES) for j in range(nk // LANES)]
            i0 = (sl * slab) % QB
            tiles = [s_ref[rows, c] if bias is None else s_ref[rows, c] + bias[i0:i0 + slab, c] for c in cols]
            mx = tiles[0]
            for t in tiles[1:]:
                mx = jnp.maximum(mx, t)
            m_new = jnp.broadcast_to(jnp.max(mx, axis=1, keepdims=True), (slab, LANES))
            if running:
                m_old = m_ref[rows, :]
                m_new = jnp.maximum(m_old, m_new)
                al_ref[rows, :] = jnp.exp(m_old - m_new)
            if m_ref is not None:
                m_ref[rows, :] = m_new
            for c, t in zip(cols, tiles):
                e_ref[rows, c] = jnp.exp(t - m_new).astype(bf16)

    unit = lambda a, u: a[u * ur:(u + 1) * ur]

    def own_half(pv, u):
        heads = range(u * ur // QB, (u + 1) * ur // QB)
        return jnp.concatenate([pv[(r - heads[0]) * QB:(r - heads[0] + 1) * QB, (r % 2) * LANES:(r % 2 + 1) * LANES]
                                for r in heads], axis=0)

    s4 = _dot_nt(q4b, kca[...])

    sig = jax.nn.sigmoid(gl_ref[...])
    g_hi = sig.astype(bf16)
    g_lo = (sig - g_hi.astype(f32)).astype(bf16)
    gate_b = _dot(jnp.concatenate([g_hi, g_lo], axis=1), gsel_ref[...])

    nwin = WINDOW + QB
    w0 = pl.multiple_of(jnp.maximum(qi - WINDOW // QB, 0) * QB, QB)
    dist_w = rel_tile(nwin, qi * QB - w0)
    bias_w = jnp.where((dist_w >= 0) & (dist_w < WINDOW), 0.0, NEG)
    for u in range(NSA_UNITS):
        s_w[u][...] = _dot_nt(unit(q4b, u), kwa[pl.ds(w0, nwin), :])

    d0 = pl.multiple_of(qi * QB, QB)
    bias_d = jnp.where(rel_tile(QB, 0) >= 0, 0.0, NEG)
    for u in range(NSA_UNITS):
        s_d[u][...] = _dot_nt(unit(q4b, u), ksa[pl.ds(d0, QB), :])
    s_m[0] = _dot_nt(q4b, ksa[:SEL_CHUNK, :])

    cmp_end = (lane * CMP_STRIDE + (CMP_BLOCK - 1))
    valid_c = t_row >= cmp_end
    ps = []
    p_sum = jnp.zeros((QB, LANES), f32)
    row_bcast = lambda col: jnp.broadcast_to(col, (QB, LANES))
    for r in range(GQA_REP):
        s = jnp.where(valid_c, head(s4, r), NEG)
        e = jnp.exp(s - row_bcast(jnp.max(s, axis=1, keepdims=True)))
        p = jnp.where(valid_c, e, 0.0) / row_bcast(jnp.sum(e, axis=1, keepdims=True))
        p_sum = p_sum + p
        ps.append(p.astype(bf16))
    o_cmp4 = _dot(jnp.concatenate(ps, axis=0), vca[...])

    o_w = []
    for u in range(NSA_UNITS):
        softmax_rows(s_w[u], e_w[u], None, None, nwin, bias_w, False)
        o_w.append(own_half(_dot(e_w[u][...], vwa[pl.ds(w0, nwin), :]), u))
    for u in range(NSA_UNITS):
        softmax_rows(s_d[u], e_d[u], m_h[u], None, QB, bias_d, False)
        acc[u][...] = own_half(_dot(e_d[u][...], vsa[pl.ds(d0, QB), :]), u)

    back = t_row // SEL_BLOCK - lane
    valid_s = (back >= 0) & (lane < nsel)

    @pl.when(2 * qi + 2 <= n_top)
    def _all_valid():
        sel_scr[...] = jnp.where(valid_s, 1.0, 0.0)

    @pl.when(2 * qi + 2 > n_top)
    def _top_k():
        p_hi = p_sum.astype(bf16)
        p_lo = (p_sum - p_hi.astype(f32)).astype(bf16)
        imp_t = (_dot_nt(ovt_ref[...], p_hi) + _dot_nt(ovt_ref[...], p_lo))[:SEL_LANES]
        blk = lax.broadcasted_iota(jnp.int32, (SEL_LANES, QB), 0)
        tq = qi * QB + lax.broadcasted_iota(jnp.int32, (SEL_LANES, QB), 1)
        back_t = tq // SEL_BLOCK - blk
        valid_t = (back_t >= 0) & (blk < nsel)
        forced = (blk == 0) | (valid_t & (back_t < N_LOCAL_BLOCKS))
        score = jnp.where(valid_t, imp_t + jnp.where(forced, FORCE_BONUS, 0.0), NEG)
        score = jnp.where(blk < nsel, score, 2.0 * NEG)
        rank = jnp.zeros((SEL_LANES, QB), jnp.int32)
        for n in range(nsel):
            row = score[n:n + 1, :]
            ahead = (row > score) | ((row == score) & (blk > n))
            rank = rank + ahead.astype(jnp.int32)
        sel_t = jnp.where((rank < n_top) & valid_t, 1.0, 0.0)
        sel_t = jnp.concatenate([sel_t, jnp.zeros((LANES - SEL_LANES, QB), f32)], axis=0)
        sel_scr[...] = sel_t.T

    sel_bias = jnp.where((sel_scr[...] > 0.5) & (lane < 2 * qi), 0.0, MASK_BIG)
    sel_bias = pltpu.roll(sel_bias, SEL_LANE0, axis=1)
    in_sel = (lane >= SEL_LANE0) & (lane < SEL_LANE0 + SEL_LANES)
    sel_only = jnp.where(in_sel, sel_bias, 0.0).astype(bf16)
    for r in range(GQA_REP):
        q_scr[r * QB:(r + 1) * QB, :] = jnp.where(in_sel, sel_bias, head(q4, r)).astype(bf16)

    per = SEL_CHUNK // QB
    n_chunks = (qi + per - 1) // per

    def sel_chunk(kc, half, bias=None):
        nxt = jnp.minimum(kc + 1, n_chunks - 1)
        s_m[1 - half] = _dot_nt(q_scr[...], ksa[pl.ds(pl.multiple_of(nxt * SEL_CHUNK, SEL_CHUNK), SEL_CHUNK), :])
        rows = pl.ds(pl.multiple_of(kc * SEL_CHUNK, SEL_CHUNK), SEL_CHUNK)
        for u in range(NSA_UNITS):
            softmax_rows(s_m.at[half, pl.ds(u * ur, ur)], e_m[u], m_h[u], al_h[u], SEL_CHUNK, bias, True)
            acc[u][...] = al_h[u][...] * acc[u][...] + own_half(_dot(e_m[u][...], vsa[rows, :]), u)

    @pl.when(n_chunks > 0)
    def _chunk0():
        sel_chunk(0, 0, _dot_nt(sel_only, ksa[:SEL_CHUNK, :]))

    def sel_body(t, _):
        @pl.when(2 * t + 1 < n_chunks)
        def _odd():
            sel_chunk(2 * t + 1, 1)

        @pl.when(2 * t + 2 < n_chunks)
        def _even():
            sel_chunk(2 * t + 2, 0)
        return 0

    lax.fori_loop(0, n_chunks // 2, sel_body, 0)

    acc_s = jnp.concatenate([a[...] for a in acc], axis=0)
    acc_w = jnp.concatenate(o_w, axis=0)
    for pr in range(GQA_REP // 2):
        ev, od = 2 * pr, 2 * pr + 1
        out = gate_b[:, (3 * pr) * LANES:(3 * pr + 1) * LANES] * jnp.where(left, head(o_cmp4, ev), head(o_cmp4, od))
        for j, a in ((1, acc_s), (2, acc_w)):
            num = jnp.where(left, head(a, ev), head(a, od))
            den = pltpu.roll(jnp.where(left, head(a, od), head(a, ev)), HEAD_DIM, axis=1)
            out = out + gate_b[:, (3 * pr + j) * LANES:(3 * pr + j + 1) * LANES] / den * num
        o_ref[:, pr * LANES:(pr + 1) * LANES] = out.astype(o_ref.dtype)


def _nsa(proj3, kcmp, vcmp, slopes_b, ovt, kaug, caug, gsel):
    B, S, _ = proj3.shape
    ncb = kcmp.shape[1]
    nrow = GQA_REP * QB
    ur = nrow // NSA_UNITS
    nwin = WINDOW + QB
    kv = lambda j: pl.BlockSpec((None, S, LANES), lambda b, g, qi, j=j: (b, 0, COL_KC + j))
    full = lambda a: pl.BlockSpec(a.shape, lambda b, g, qi: (0,) * a.ndim)
    return pl.pallas_call(
        functools.partial(_nsa_kernel, seq=S),
        grid=(B, N_KV_B, S // QB),
        in_specs=[pl.BlockSpec(memory_space=pltpu.SMEM),
                  pl.BlockSpec((None, QB, 2 * LANES), lambda b, g, qi: (b, qi, COL_QB + g)),
                  kv(2), kv(3), kv(4), kv(5),
                  pl.BlockSpec((None, QB, LANES), lambda b, g, qi: (b, qi, COL_GATE + g)),
                  pl.BlockSpec((None, ncb, LANES), lambda b, g, qi: (b, 0, 0)),
                  pl.BlockSpec((None, ncb, LANES), lambda b, g, qi: (b, 0, 0)),
                  full(ovt), full(kaug), full(caug), full(gsel)],
        out_specs=pl.BlockSpec((None, QB, 2 * LANES), lambda b, g, qi: (b, qi, g)),
        out_shape=jax.ShapeDtypeStruct((B, S, N_HEADS_B * HEAD_DIM), bf16),
        scratch_shapes=([pltpu.VMEM((S, LANES), bf16), pltpu.VMEM((S, 2 * LANES), bf16)] * 2
                        + [pltpu.VMEM((ncb, LANES), bf16)] * 2
                        + [pltpu.VMEM((nrow, LANES), bf16), pltpu.VMEM((QB, LANES), f32),
                           pltpu.VMEM((2, nrow, SEL_CHUNK), f32)]
                        + [pltpu.VMEM((ur, nwin), f32)] * NSA_UNITS + [pltpu.VMEM((ur, nwin), bf16)] * NSA_UNITS
                        + [pltpu.VMEM((ur, QB), f32)] * NSA_UNITS + [pltpu.VMEM((ur, QB), bf16)] * NSA_UNITS
                        + [pltpu.VMEM((ur, SEL_CHUNK), bf16)] * NSA_UNITS
                        + [pltpu.VMEM((ur, LANES), f32)] * (2 * NSA_UNITS)
                        + [pltpu.VMEM((ur, LANES), f32)] * NSA_UNITS),
        compiler_params=_params("parallel", "parallel", "arbitrary"),
        name="nsa_attn",
    )(slopes_b, proj3, proj3, proj3, proj3, proj3, proj3, kcmp, vcmp, ovt, kaug, caug, gsel)


def _route(h, w_ref, b_ref):
    w = w_ref[...]
    h_hi, w_hi = h.astype(bf16), w.astype(bf16)
    h_lo, w_lo = (h - h_hi.astype(f32)).astype(bf16), (w - w_hi.astype(f32)).astype(bf16)
    logit = _dot(h_hi, w_hi) + (_dot(h_hi, w_lo) + _dot(h_lo, w_hi)) + b_ref[...]
    tm = logit.shape[0]
    lane = lax.broadcasted_iota(jnp.int32, (tm, LANES), 1)
    big = jnp.int32(LANES)
    is_g = lane < N_GROUPS
    gl = jnp.where(is_g, logit, NEG)
    gmax = jnp.max(gl, axis=1, keepdims=True)
    gsum = jnp.sum(jnp.where(is_g, jnp.exp(gl - gmax), 0.0), axis=1, keepdims=True)
    gsel = jnp.min(jnp.where(is_g & (gl == gmax), lane, big), axis=1, keepdims=True)
    gw = 1.0 / gsum
    e_lane = lane - N_GROUPS
    in_grp = (e_lane >= 0) & (e_lane < N_EXPERTS) & (e_lane // EXPERTS_PER_GROUP == gsel)
    el = jnp.where(in_grp, logit, NEG)
    t1 = jnp.max(el, axis=1, keepdims=True)
    i1 = jnp.min(jnp.where(in_grp & (el == t1), lane, big), axis=1, keepdims=True)
    el2 = jnp.where(lane == i1, NEG, el)
    t2 = jnp.max(el2, axis=1, keepdims=True)
    i2 = jnp.min(jnp.where(in_grp & (lane != i1) & (el2 == t2), lane, big), axis=1, keepdims=True)
    e2 = jnp.exp(t2 - t1)
    w1 = gw / (1.0 + e2)
    w2 = gw * e2 / (1.0 + e2)
    return jnp.where(lane == i1, w1, jnp.where(lane == i2, w2, 0.0)), gsel


def _to_token_tiles(ref, x):
    for s in range(D_MODEL // LANES):
        ref[pl.ds(s, x.shape[0], stride=D_MODEL // LANES), :] = x[:, s * LANES:(s + 1) * LANES]


def _from_token_tiles(ref):
    n = D_MODEL // LANES
    return jnp.concatenate([ref[pl.ds(s, ref.shape[0] // n, stride=n), :] for s in range(n)], axis=1)


def _out_route_kernel(x_ref, oa_ref, ob_ref, wa_ref, wb_ref, g_ref, w_ref, b_ref, tri_ref,
                      x1_ref, stage_ref, meta_ref, cnt_ref, cnt_scr):
    @pl.when(pl.program_id(0) == 0)
    def _zero():
        cnt_scr[...] = jnp.zeros_like(cnt_scr)

    x1 = x_ref[...] + _dot(oa_ref[...], wa_ref[...]) + _dot(ob_ref[...], wb_ref[...])
    x1_ref[...] = x1
    h = _rms(x1, g_ref[...])
    _to_token_tiles(stage_ref, h)
    _, gsel = _route(h, w_ref, b_ref)
    tm = h.shape[0]
    lane = lax.broadcasted_iota(jnp.int32, (tm, LANES), 1)
    is_g = lane < N_GROUPS

    onehot = jnp.where(is_g & (lane == gsel), 1.0, 0.0)
    before = _dot(tri_ref[...], onehot.astype(bf16)) + cnt_scr[...]
    rank = jnp.sum(onehot * before, axis=1, keepdims=True)
    cnt_scr[...] = before[tm - 1:tm, :] + onehot[tm - 1:tm, :]
    cnt_ref[...] = cnt_scr[...]
    rank_hi = jnp.floor(rank * (1.0 / RANK_SPLIT))
    cols = jnp.where(lane == 0, gsel.astype(f32), jnp.where(lane == 1, rank_hi,
                                                            jnp.where(lane == 2, rank - RANK_SPLIT * rank_hi, 0.0)))
    pick = (lax.broadcasted_iota(jnp.int32, (8, LANES), 0) == lax.broadcasted_iota(jnp.int32, (8, LANES), 1))
    meta_ref[...] = _dot_nt(jnp.where(pick, 1.0, 0.0).astype(bf16), cols.astype(bf16))


def _out_route(x2d, oa, ob, wa, wb, g, w, b):
    T = x2d.shape[0]
    tm = MOE_TILE
    assert T // RANK_SPLIT <= 256
    tri = jnp.asarray(np.tril(np.ones((tm, tm), np.float32), -1), bf16)
    row = lambda n: pl.BlockSpec((tm, n), lambda i: (i, 0))
    full = lambda a: pl.BlockSpec(a.shape, lambda i: (0,) * a.ndim)
    return pl.pallas_call(
        _out_route_kernel,
        grid=(T // tm,),
        in_specs=[row(D_MODEL), row(oa.shape[1]), row(ob.shape[1]), full(wa), full(wb),
                  full(g), full(w), full(b), full(tri)],
        out_specs=[row(D_MODEL),
                   pl.BlockSpec((tm * TOKEN_ROWS, LANES), lambda i: (i, 0)),
                   pl.BlockSpec((None, 8, tm), lambda i: (i, 0, 0)),
                   pl.BlockSpec((1, LANES), lambda i: (0, 0))],
        out_shape=[jax.ShapeDtypeStruct((T, D_MODEL), f32),
                   jax.ShapeDtypeStruct((T * TOKEN_ROWS, LANES), f32),
                   jax.ShapeDtypeStruct((T // tm, 8, tm), f32),
                   jax.ShapeDtypeStruct((1, LANES), f32)],
        scratch_shapes=[pltpu.VMEM((1, LANES), f32)],
        compiler_params=_params("arbitrary"),
        name="out_proj_router",
    )(x2d, oa, ob, wa, wb, g, w, b, tri)


def _token_rows(t, n=1):
    return pl.ds(pl.multiple_of(t * TOKEN_ROWS, TOKEN_ROWS), n * TOKEN_ROWS)


def _gathered_tile(idx_ref, src_ref, buf_ref, sems, tm):
    i = pl.program_id(0)
    n = pl.num_programs(0)

    def issue(step):
        slot = step % 2

        def one(k, _):
            pltpu.make_async_copy(src_ref.at[_token_rows(idx_ref[step * tm + k])],
                                  buf_ref.at[slot, _token_rows(k)], sems.at[slot]).start()
            return 0

        lax.fori_loop(0, tm, one, 0, unroll=8)

    @pl.when(i == 0)
    def _first():
        issue(i)

    @pl.when(i + 1 < n)
    def _next():
        issue(i + 1)

    slot = i % 2
    pltpu.make_async_copy(src_ref.at[_token_rows(0, tm)], buf_ref.at[slot], sems.at[slot]).wait()
    return slot


def _moe_kernel(tg_ref, src_ref, stage_ref, wr_ref, br_ref, wg_ref, wu_ref, wd_ref, o_ref, buf_ref, sems):
    grp = tg_ref[pl.program_id(0)]
    slot = _gathered_tile(src_ref, stage_ref, buf_ref, sems, MOE_TILE)

    @pl.when(grp < N_GROUPS)
    def _experts():
        h = _from_token_tiles(buf_ref.at[slot])
        gates, _ = _route(h, wr_ref, br_ref)
        x = h.astype(bf16)
        lane = lax.broadcasted_iota(jnp.int32, gates.shape, 1)
        y = jnp.zeros(h.shape, f32)
        for e in range(EXPERTS_PER_GROUP):
            gcol = jnp.sum(jnp.where(lane == N_GROUPS + EXPERTS_PER_GROUP * grp + e, gates, 0.0),
                           axis=1, keepdims=True)
            a = jax.nn.silu(_dot(x, wg_ref[e])) * _dot(x, wu_ref[e])
            y = y + _dot((a * gcol).astype(bf16), wd_ref[e])
        _to_token_tiles(o_ref, y)

    @pl.when(grp >= N_GROUPS)
    def _unused():
        o_ref[...] = jnp.zeros_like(o_ref)


def _moe(stage, tile_group, slot_token, w_route, b_route, wg, wu, wd, layer):
    tm = MOE_TILE
    n_slots = slot_token.shape[0]
    w_spec = lambda k, n: pl.BlockSpec((None, None, EXPERTS_PER_GROUP, k, n),
                                       lambda j, tg, src: (layer, jnp.minimum(tg[j], N_GROUPS - 1), 0, 0, 0))
    grouped = lambda w: w.reshape(w.shape[0], N_GROUPS, EXPERTS_PER_GROUP, *w.shape[2:])
    return pl.pallas_call(
        _moe_kernel,
        grid_spec=pltpu.PrefetchScalarGridSpec(
            num_scalar_prefetch=2, grid=(n_slots // tm,),
            in_specs=[pl.BlockSpec(memory_space=pl.ANY),
                      pl.BlockSpec((D_MODEL, LANES), lambda j, tg, src: (0, 0)),
                      pl.BlockSpec((1, LANES), lambda j, tg, src: (0, 0)),
                      w_spec(D_MODEL, D_EXPERT), w_spec(D_MODEL, D_EXPERT), w_spec(D_EXPERT, D_MODEL)],
            out_specs=pl.BlockSpec((tm * TOKEN_ROWS, LANES), lambda j, tg, src: (j, 0)),
            scratch_shapes=[pltpu.VMEM((2, tm * TOKEN_ROWS, LANES), f32), pltpu.SemaphoreType.DMA((2,))]),
        out_shape=jax.ShapeDtypeStruct((n_slots * TOKEN_ROWS, LANES), f32),
        compiler_params=_params("arbitrary"),
        name="moe_ffn",
    )(tile_group, slot_token, stage, w_route, b_route, grouped(wg), grouped(wu), grouped(wd))


def _ple_kernel(pos_ref, x_ref, ys_ref, p_ref, g_ref, wg_ref, wp_ref, fg_ref, o_ref, buf_ref, sems, *, final):
    slot = _gathered_tile(pos_ref, ys_ref, buf_ref, sems, x_ref.shape[0])
    x = x_ref[...] + _from_token_tiles(buf_ref.at[slot])
    gate = jax.nn.sigmoid(_dot(_rms(x, g_ref[...]).astype(bf16), wg_ref[...]))
    y = x + gate * _dot(p_ref[...].astype(bf16), wp_ref[...])
    o_ref[...] = _rms(y, fg_ref[...]) if final else y


def _ple(x2d, ys, pos, p3, g, wg, wp, fg, layer, final):
    T = x2d.shape[0]
    tm = 512
    full = lambda a: pl.BlockSpec(a.shape, lambda i, pos: (0,) * a.ndim)
    return pl.pallas_call(
        functools.partial(_ple_kernel, final=final),
        grid_spec=pltpu.PrefetchScalarGridSpec(
            num_scalar_prefetch=1, grid=(T // tm,),
            in_specs=[pl.BlockSpec((tm, D_MODEL), lambda i, pos: (i, 0)),
                      pl.BlockSpec(memory_space=pl.ANY),
                      pl.BlockSpec((None, tm, PLE_DIM), lambda i, pos: (layer, i, 0)),
                      full(g), full(wg), full(wp), full(fg)],
            out_specs=pl.BlockSpec((tm, D_MODEL), lambda i, pos: (i, 0)),
            scratch_shapes=[pltpu.VMEM((2, tm * TOKEN_ROWS, LANES), f32), pltpu.SemaphoreType.DMA((2,))]),
        out_shape=jax.ShapeDtypeStruct((T, D_MODEL), f32),
        compiler_params=_params("arbitrary"),
        name="ple",
    )(pos, x2d, ys, p3, g, wg, wp, fg)


def _alibi_slopes():
    s = 2.0 ** (-8.0 * np.arange(1, N_HEADS_TOTAL + 1) / N_HEADS_TOTAL)
    assert np.all(np.log2(s[1::2]) == np.round(np.log2(s[1::2])))
    return jnp.asarray(s[0::2], f32), jnp.asarray(s[1::2], f32)


def _selection_constants(seq):
    ncb = seq // CMP_STRIDE
    nsel = seq // SEL_BLOCK
    n_cmp = (seq - CMP_BLOCK) // CMP_STRIDE + 1
    cs = np.arange(n_cmp) * CMP_STRIDE
    bs = np.arange(nsel) * SEL_BLOCK
    ov = np.clip(np.minimum(cs[:, None] + CMP_BLOCK, bs[None, :] + SEL_BLOCK)
                 - np.maximum(cs[:, None], bs[None, :]), 0, None) / CMP_BLOCK
    assert ncb == LANES and seq <= POS_SPLIT * 256
    ovt = np.zeros((LANES, ncb), np.float32)
    ovt[:nsel, :n_cmp] = ov.T
    pos = np.arange(seq)
    kaug = np.zeros((seq, LANES), np.float32)
    kaug[:, POS_LANE0:POS_LANE0 + 2] = 1.0
    kaug[:, POS_LANE0 + 2] = pos // POS_SPLIT
    kaug[:, POS_LANE0 + 3] = pos % POS_SPLIT
    kaug[pos, SEL_LANE0 + pos // SEL_BLOCK] = 1.0
    cend = np.arange(ncb) * CMP_STRIDE + CMP_BLOCK - 1
    caug = np.zeros((ncb, LANES), np.float32)
    caug[:, POS_LANE0:POS_LANE0 + 2] = 1.0
    caug[:, POS_LANE0 + 2] = cend // POS_SPLIT
    caug[:, POS_LANE0 + 3] = cend % POS_SPLIT
    gsel = np.zeros((2, LANES, GQA_REP // 2, 3, 2, HEAD_DIM), np.float32)
    for pr in range(GQA_REP // 2):
        for j in range(3):
            for hh in range(2):
                gsel[:, 3 * (2 * pr + hh) + j, pr, j, hh, :] = 1.0
    gsel = gsel.reshape(2 * LANES, (GQA_REP // 2) * 3 * LANES)
    return jnp.asarray(ovt, bf16), jnp.asarray(kaug), jnp.asarray(caug), jnp.asarray(gsel, bf16)


def _dispatch_plan(meta, counts, n_tokens):
    i32 = jnp.int32
    n_slots = n_tokens + N_GROUPS * MOE_TILE
    group = meta[:, 0, :].reshape(n_tokens).astype(i32)
    rank = (meta[:, 1, :] * RANK_SPLIT + meta[:, 2, :]).reshape(n_tokens).astype(i32)
    cnt = counts[0, :N_GROUPS].astype(i32)
    padded = (cnt + MOE_TILE - 1) // MOE_TILE * MOE_TILE
    end = jnp.cumsum(padded)
    pos = (end - padded)[group] + rank
    slot_token = jnp.zeros((n_slots,), i32).at[pos].set(jnp.arange(n_tokens, dtype=i32))
    tile_start = jnp.arange(n_slots // MOE_TILE, dtype=i32) * MOE_TILE
    tile_group = jnp.sum(tile_start[:, None] >= end[None, :], axis=1).astype(i32)
    return pos, slot_token, tile_group


def _block_diag2(w):
    z = jnp.zeros_like(w)
    return jnp.concatenate([jnp.concatenate([w, z], axis=-1), jnp.concatenate([z, w], axis=-1)], axis=-2)


def _layout_w_in(w):
    gate = w[:, N_MAIN:]
    per = GQA_REP * 3
    blocks = [jnp.pad(gate[:, g * per:(g + 1) * per], ((0, 0), (0, LANES - per))) for g in range(N_KV_B)]
    return jnp.concatenate([w[:, :N_MAIN]] + blocks, axis=1).astype(bf16)


def kernel(x, p, attn_norm, w_in, w_out, w_cmp_k1, w_cmp_k2, w_cmp_v1, w_cmp_v2, cmp_pos, ffn_norm, w_route_group, b_route_group, w_route_expert, b_route_expert, w_expert_gate, w_expert_up, w_expert_down, ple_norm, w_ple_gate, w_ple_proj, final_norm):
    B, S, D = x.shape
    depth = w_in.shape[0]
    T = B * S
    slopes_a, slopes_b = _alibi_slopes()
    ovt, kaug, caug, gsel = _selection_constants(S)
    wg_all = w_expert_gate.astype(bf16)
    wu_all = w_expert_up.astype(bf16)
    wd_all = w_expert_down.astype(bf16)
    p3 = p.reshape(depth, T, PLE_DIM)
    row = lambda v: v.reshape(1, -1)
    n_route = N_GROUPS + N_EXPERTS

    x2d = x.reshape(T, D)
    for i in range(depth):
        proj3 = _in_proj(x2d, row(attn_norm[i]), _layout_w_in(w_in[i])).reshape(B, S, N_PROJ)
        oa = _dilated(proj3, slopes_a)
        w1 = lambda w: _block_diag2(w.reshape(CMP_BLOCK, HEAD_DIM, CMP_HIDDEN)).astype(bf16)
        pos_dup = jnp.concatenate([cmp_pos[i], cmp_pos[i]], axis=-1)
        kcmp, vcmp = _compress(proj3, pos_dup, w1(w_cmp_k1[i]), _block_diag2(w_cmp_k2[i]).astype(bf16),
                               w1(w_cmp_v1[i]), _block_diag2(w_cmp_v2[i]).astype(bf16))
        ob = _nsa(proj3, kcmp, vcmp, slopes_b, ovt, kaug, caug, gsel)
        wo = w_out[i].astype(bf16)
        w_route = jnp.pad(jnp.concatenate([w_route_group[i], w_route_expert[i]], axis=1),
                          ((0, 0), (0, LANES - n_route)))
        b_route = jnp.pad(jnp.concatenate([b_route_group[i], b_route_expert[i]]), (0, LANES - n_route))
        x2d, stage, meta, counts = _out_route(x2d, oa.reshape(T, A_W), ob.reshape(T, -1), wo[:A_W], wo[A_W:],
                                              row(ffn_norm[i]), w_route, row(b_route))
        pos, slot_token, tile_group = _dispatch_plan(meta, counts, T)
        ys = _moe(stage, tile_group, slot_token, w_route, row(b_route), wg_all, wu_all, wd_all, i)
        x2d = _ple(x2d, ys, pos, p3, row(ple_norm[i]), w_ple_gate[i].astype(bf16),
                   w_ple_proj[i].astype(bf16), row(final_norm), i, i == depth - 1)
    return x2d.reshape(B, S, D)
```

```python
import functools

import numpy as np
import jax
import jax.numpy as jnp
from jax import lax
from jax.experimental import pallas as pl
from jax.experimental.pallas import tpu as pltpu

D_MODEL = 1024
PLE_DIM = 256
HEAD_DIM = 64
N_HEADS_A = 8
N_HEADS_B = 8
N_KV_B = 2
GQA_REP = N_HEADS_B // N_KV_B
N_HEADS_TOTAL = N_HEADS_A + N_HEADS_B
DILATED_PATTERNS = ((128, 1), (512, 4), (2048, 16))
CMP_BLOCK = 32
CMP_STRIDE = 16
CMP_HIDDEN = 256
SEL_BLOCK = 64
SEL_TOP = 16
N_LOCAL_BLOCKS = 2
WINDOW = 512
N_GROUPS = 4
EXPERTS_PER_GROUP = 4
N_EXPERTS = N_GROUPS * EXPERTS_PER_GROUP
D_EXPERT = 512
RMS_EPS = 1e-6
NEG = -1e30
FORCE_BONUS = 1e4
SCALE = HEAD_DIM ** -0.5

LANES = 128
QB = 128
SEL_CHUNK = 512
NSA_UNITS = 2
DIL_INFLIGHT = 3
DIL_TRIP = 16
POS_LANE0 = HEAD_DIM
POS_SPLIT = 16
SEL_LANE0 = POS_LANE0 + 4
SEL_LANES = 32
MASK_BIG = -(2.0 ** 100)
MOE_TILE = 512
TOKEN_ROWS = D_MODEL // LANES
RANK_SPLIT = 128
A_W = N_HEADS_A * HEAD_DIM
N_MAIN = 3 * A_W + N_HEADS_B * HEAD_DIM + 6 * N_KV_B * HEAD_DIM
N_PROJ = N_MAIN + N_KV_B * LANES
COL_QB = (3 * A_W) // (2 * LANES)
COL_KC = (3 * A_W + N_HEADS_B * HEAD_DIM) // LANES
COL_GATE = N_MAIN // LANES
VMEM_LIMIT = 56 * 1024 * 1024

f32 = jnp.float32
bf16 = jnp.bfloat16


def _dot(a, b):
    return jnp.dot(a, b, preferred_element_type=f32)


def _dot_nt(a, b):
    return lax.dot_general(a, b, (((1,), (1,)), ((), ())), preferred_element_type=f32)


def _rms(x, g):
    return x * lax.rsqrt(jnp.mean(x * x, axis=-1, keepdims=True) + RMS_EPS) * g


def _params(*sem):
    return pltpu.CompilerParams(dimension_semantics=sem, vmem_limit_bytes=VMEM_LIMIT)


def _in_proj_kernel(x_ref, g_ref, w_ref, o_ref):
    h = _rms(x_ref[...], g_ref[...]).astype(bf16)
    for n0 in range(0, N_PROJ, 512):
        o_ref[:, n0:n0 + 512] = _dot(h, w_ref[:, n0:n0 + 512])


def _in_proj(x2d, g, w):
    T = x2d.shape[0]
    tm = 512
    return pl.pallas_call(
        _in_proj_kernel,
        grid=(T // tm,),
        in_specs=[pl.BlockSpec((tm, D_MODEL), lambda i: (i, 0)),
                  pl.BlockSpec((1, D_MODEL), lambda i: (0, 0)),
                  pl.BlockSpec((D_MODEL, N_PROJ), lambda i: (0, 0))],
        out_specs=pl.BlockSpec((tm, N_PROJ), lambda i: (i, 0)),
        out_shape=jax.ShapeDtypeStruct((T, N_PROJ), f32),
        compiler_params=_params("parallel"),
        name="in_proj",
    )(x2d, g, w)


def _dil_kernel(slope_ref, q_ref, k_ref, v_ref, o_ref, out_ref, lse_ref, bias_scr, *bufs, seq):
    hp = pl.program_id(1)
    lane = lax.broadcasted_iota(jnp.int32, (QB, LANES), 1)
    left = lane < HEAD_DIM
    slab = 32
    s_bufs, e_bufs = bufs[:DIL_INFLIGHT], bufs[DIL_INFLIGHT:]

    i = lax.broadcasted_iota(jnp.int32, (QB, 2 * QB), 0)
    j = lax.broadcasted_iota(jnp.int32, (QB, 2 * QB), 1)
    rel = i - j + QB
    valid = (rel >= 0) & (rel <= QB)
    relf = rel.astype(f32)
    for p, (window, dil) in enumerate(DILATED_PATTERNS):
        assert window // dil == QB
        for hh in range(2):
            bias_scr[2 * p + hh] = jnp.where(valid, (-float(dil) * slope_ref[2 * hp + hh]) * relf, NEG)

    ones = jnp.ones((2 * QB, LANES), bf16)

    def scores(u, blk):
        p, dil, row0, key0, nk = blk
        qc = q_ref[pl.ds(row0, QB, stride=dil), :] * SCALE
        q2 = jnp.concatenate([jnp.where(left, qc, 0.0), jnp.where(left, 0.0, qc)], axis=0).astype(bf16)
        s_bufs[u][:, :nk] = _dot_nt(q2, k_ref[pl.ds(key0, nk, stride=dil), :].astype(bf16))

    def probs(u, blk):
        p, dil, row0, key0, nk = blk
        c0 = 2 * QB - nk
        ms = []
        for hh in range(2):
            parts = []
            for sl in range(QB // slab):
                rows = slice(hh * QB + sl * slab, hh * QB + (sl + 1) * slab)
                tiles = [s_bufs[u][rows, c:c + LANES]
                         + bias_scr[2 * p + hh, sl * slab:(sl + 1) * slab, c0 + c:c0 + c + LANES]
                         for c in range(0, nk, LANES)]
                mx = tiles[0]
                for t in tiles[1:]:
                    mx = jnp.maximum(mx, t)
                mx = jnp.broadcast_to(jnp.max(mx, axis=1, keepdims=True), (slab, LANES))
                for c, t in zip(range(0, nk, LANES), tiles):
                    e_bufs[u][rows, c:c + LANES] = jnp.exp(t - mx).astype(bf16)
                parts.append(mx)
            ms.append(jnp.concatenate(parts, axis=0))
        return jnp.where(left, ms[0], ms[1])

    def values(u, blk, row_max):
        p, dil, row0, key0, nk = blk
        v2 = jnp.concatenate([v_ref[pl.ds(key0, nk, stride=dil), :].astype(bf16), ones[:nk]], axis=1)
        res = _dot(e_bufs[u][:, :nk], v2)
        rows = pl.ds(row0, QB, stride=dil)
        den = jnp.where(left, res[:QB, LANES:], res[QB:, LANES:])
        out_ref[p, rows, :] = jnp.where(left, res[:QB, :LANES], res[QB:, :LANES]) / den
        lse_ref[p, rows, :] = row_max + jnp.log(den)

    def run(blocks):
        ahead = DIL_INFLIGHT - 1
        for n in range(min(ahead, len(blocks))):
            scores(n % DIL_INFLIGHT, blocks[n])
        for n, blk in enumerate(blocks):
            if n + ahead < len(blocks):
                scores((n + ahead) % DIL_INFLIGHT, blocks[n + ahead])
            values(n % DIL_INFLIGHT, blk, probs(n % DIL_INFLIGHT, blk))

    def first(p, dil, r):
        return (p, dil, r, r, QB)

    def later(p, dil, r, a):
        return (p, dil, r + dil * QB * a, r + dil * QB * (a - 1), 2 * QB)

    for p, (window, dil) in enumerate(DILATED_PATTERNS):
        nblk = seq // dil // QB
        blocks = [first(p, dil, r) if a == 0 else later(p, dil, r, a) for r in range(dil) for a in range(nblk)]
        for b0 in range(0, len(blocks), DIL_TRIP):
            run(blocks[b0:b0 + DIL_TRIP])

    ch = 256

    def combine(c, _):
        rows = pl.ds(pl.multiple_of(c * ch, ch), ch)
        lses = [lse_ref[p, rows, :] for p in range(len(DILATED_PATTERNS))]
        big = functools.reduce(jnp.maximum, lses)
        num = jnp.zeros((ch, LANES), f32)
        den = jnp.zeros((ch, LANES), f32)
        for p, lse in enumerate(lses):
            w = jnp.exp(lse - big)
            num = num + w * out_ref[p, rows, :]
            den = den + w
        o_ref[rows, :] = (num / den).astype(o_ref.dtype)
        return 0

    lax.fori_loop(0, seq // ch, combine, 0)


def _dilated(proj3, slopes_a):
    B, S, _ = proj3.shape
    npair = N_HEADS_A // 2
    blk = lambda off: pl.BlockSpec((None, S, LANES), lambda b, hp, off=off: (b, 0, off + hp))
    return pl.pallas_call(
        functools.partial(_dil_kernel, seq=S),
        grid=(B, npair),
        in_specs=[pl.BlockSpec(memory_space=pltpu.SMEM), blk(0), blk(npair), blk(2 * npair)],
        out_specs=pl.BlockSpec((None, S, LANES), lambda b, hp: (b, 0, hp)),
        out_shape=jax.ShapeDtypeStruct((B, S, A_W), bf16),
        scratch_shapes=([pltpu.VMEM((len(DILATED_PATTERNS), S, LANES), f32)] * 2
                        + [pltpu.VMEM((2 * len(DILATED_PATTERNS), QB, 2 * QB), f32)]
                        + [pltpu.VMEM((2 * QB, 2 * QB), f32)] * DIL_INFLIGHT
                        + [pltpu.VMEM((2 * QB, 2 * QB), bf16)] * DIL_INFLIGHT),
        compiler_params=_params("parallel", "parallel"),
        name="dilated_attn",
    )(slopes_a, proj3, proj3, proj3)


def _cmp_kernel(kc_ref, vc_ref, pos_ref, w1k_ref, w2k_ref, w1v_ref, w2v_ref, ko_ref, vo_ref, *, ncb):
    half = CMP_BLOCK // 2
    for x_ref, pi, w1_ref, w2_ref, o_ref in ((kc_ref, 0, w1k_ref, w2k_ref, ko_ref),
                                             (vc_ref, 1, w1v_ref, w2v_ref, vo_ref)):
        lo = jnp.zeros((ncb, 2 * CMP_HIDDEN), f32)
        hi = jnp.zeros((ncb, 2 * CMP_HIDDEN), f32)
        for r in range(half):
            x = x_ref[pl.ds(r, ncb, stride=CMP_STRIDE), :]
            lo = lo + _dot((x + pos_ref[pi, r:r + 1, :]).astype(bf16), w1_ref[r])
            hi = hi + _dot((x + pos_ref[pi, r + half:r + half + 1, :]).astype(bf16), w1_ref[r + half])
        h1 = lo + pltpu.roll(hi, ncb - 1, axis=0)
        o_ref[...] = _dot(jax.nn.gelu(h1).astype(bf16), w2_ref[...])


def _compress(proj3, pos_dup, w1k, w2k, w1v, w2v):
    B, S, _ = proj3.shape
    ncb = S // CMP_STRIDE
    full = lambda a: pl.BlockSpec(a.shape, lambda b: (0,) * a.ndim)
    out = jax.ShapeDtypeStruct((B, ncb, LANES), f32)
    return pl.pallas_call(
        functools.partial(_cmp_kernel, ncb=ncb),
        grid=(B,),
        in_specs=[pl.BlockSpec((None, S, LANES), lambda b: (b, 0, COL_KC)),
                  pl.BlockSpec((None, S, LANES), lambda b: (b, 0, COL_KC + 1)),
                  full(pos_dup), full(w1k), full(w2k), full(w1v), full(w2v)],
        out_specs=[pl.BlockSpec((None, ncb, LANES), lambda b: (b, 0, 0))] * 2,
        out_shape=[out, out],
        compiler_params=_params("parallel"),
        name="nsa_compress",
    )(proj3, proj3, pos_dup, w1k, w2k, w1v, w2v)


def _nsa_kernel(slope_ref, q_ref, ks_ref, vs_ref, kw_ref, vw_ref, gl_ref, kcmp_ref, vcmp_ref,
                ovt_ref, kaug_ref, caug_ref, gsel_ref, o_ref,
                ksa, vsa, kwa, vwa, kca, vca, q_scr, sel_scr, s_m, *per_unit, seq):
    g = pl.program_id(1)
    qi = pl.program_id(2)
    nsel = seq // SEL_BLOCK
    n_top = min(SEL_TOP, nsel)
    assert nsel <= SEL_LANES and N_LOCAL_BLOCKS * SEL_BLOCK >= QB and n_top > N_LOCAL_BLOCKS
    slab = 32
    ur = GQA_REP * QB // NSA_UNITS
    s_w, e_w, s_d, e_d, e_m, m_h, al_h, acc = [per_unit[i * NSA_UNITS:(i + 1) * NSA_UNITS] for i in range(8)]

    def group_lanes(x):
        return jnp.where(g == 0, x, pltpu.roll(x, HEAD_DIM, axis=1))

    @pl.when(qi == 0)
    def _prep():
        ch = 256
        lane_c = lax.broadcasted_iota(jnp.int32, (ch, LANES), 1)
        is_k = lane_c < HEAD_DIM
        for c in range(seq // ch):
            sl = slice(c * ch, (c + 1) * ch)
            aug = kaug_ref[sl, :]
            ksa[sl, :] = jnp.where(is_k, group_lanes(ks_ref[sl, :]), aug).astype(bf16)
            kwa[sl, :] = jnp.where(is_k, group_lanes(kw_ref[sl, :]),
                                   jnp.where(lane_c < SEL_LANE0, aug, 0.0)).astype(bf16)
            for src, dst in ((vs_ref, vsa), (vw_ref, vwa)):
                v = group_lanes(src[sl, :])
                dst[sl, :LANES] = jnp.where(is_k, v, 1.0).astype(bf16)
                dst[sl, LANES:] = jnp.where(is_k, 1.0, pltpu.roll(v, HEAD_DIM, axis=1)).astype(bf16)
        lane_k = lax.broadcasted_iota(jnp.int32, kca.shape, 1)
        kca[...] = jnp.where(lane_k < HEAD_DIM, group_lanes(kcmp_ref[...]), caug_ref[...]).astype(bf16)
        vc = group_lanes(vcmp_ref[...])
        vca[...] = jnp.where(lane_k < HEAD_DIM, vc, pltpu.roll(vc, HEAD_DIM, axis=1)).astype(bf16)

    lane = lax.broadcasted_iota(jnp.int32, (QB, LANES), 1)
    left = lane < HEAD_DIM
    ii = lax.broadcasted_iota(jnp.int32, (QB, LANES), 0)
    t_row = qi * QB + ii
    t_hi = (t_row // POS_SPLIT).astype(f32)
    t_lo = (t_row % POS_SPLIT).astype(f32)
    slopes = [slope_ref[g * GQA_REP + r] for r in range(GQA_REP)]
    head = lambda a, r: a[r * QB:(r + 1) * QB]

    def q_head(r):
        x = q_ref[:, (r // 2) * LANES:(r // 2 + 1) * LANES] * SCALE
        if r % 2:
            x = pltpu.roll(x, HEAD_DIM, axis=1)
        m = slopes[r]
        pos = jnp.where(lane == POS_LANE0, (-POS_SPLIT * m) * t_hi,
                        jnp.where(lane == POS_LANE0 + 1, (-m) * t_lo,
                                  jnp.where(lane == POS_LANE0 + 2, POS_SPLIT * m,
                                            jnp.where(lane == POS_LANE0 + 3, m, 0.0))))
        return jnp.where(left, x, pos)

    q4 = jnp.concatenate([q_head(r) for r in range(GQA_REP)], axis=0)
    q4b = q4.astype(bf16)

    def rel_tile(nk, offset):
        i = lax.broadcasted_iota(jnp.int32, (QB, nk), 0)
        j = lax.broadcasted_iota(jnp.int32, (QB, nk), 1)
        return (i - j + offset).astype(f32)

    def softmax_rows(s_ref, e_ref, m_ref, al_ref, nk, bias, running):
        for sl in range(ur // slab):
            rows = slice(sl * slab, (sl + 1) * slab)
            cols = [slice(j * LANES, (j + 1) * LANES) for j in range(nk // LANES)]
            i0 = (sl * slab) % QB
            tiles = [s_ref[rows, c] if bias is None else s_ref[rows, c] + bias[i0:i0 + slab, c] for c in cols]
            mx = tiles[0]
            for t in tiles[1:]:
                mx = jnp.maximum(mx, t)
            m_new = jnp.broadcast_to(jnp.max(mx, axis=1, keepdims=True), (slab, LANES))
            if running:
                m_old = m_ref[rows, :]
                m_new = jnp.maximum(m_old, m_new)
                al_ref[rows, :] = jnp.exp(m_old - m_new)
            if m_ref is not None:
                m_ref[rows, :] = m_new
            for c, t in zip(cols, tiles):
                e_ref[rows, c] = jnp.exp(t - m_new).astype(bf16)

    unit = lambda a, u: a[u * ur:(u + 1) * ur]

    def own_half(pv, u):
        heads = range(u * ur // QB, (u + 1) * ur // QB)
        return jnp.concatenate([pv[(r - heads[0]) * QB:(r - heads[0] + 1) * QB, (r % 2) * LANES:(r % 2 + 1) * LANES]
                                for r in heads], axis=0)

    s4 = _dot_nt(q4b, kca[...])

    sig = jax.nn.sigmoid(gl_ref[...])
    g_hi = sig.astype(bf16)
    g_lo = (sig - g_hi.astype(f32)).astype(bf16)
    gate_b = _dot(jnp.concatenate([g_hi, g_lo], axis=1), gsel_ref[...])

    nwin = WINDOW + QB
    w0 = pl.multiple_of(jnp.maximum(qi - WINDOW // QB, 0) * QB, QB)
    dist_w = rel_tile(nwin, qi * QB - w0)
    bias_w = jnp.where((dist_w >= 0) & (dist_w < WINDOW), 0.0, NEG)
    for u in range(NSA_UNITS):
        s_w[u][...] = _dot_nt(unit(q4b, u), kwa[pl.ds(w0, nwin), :])

    d0 = pl.multiple_of(qi * QB, QB)
    bias_d = jnp.where(rel_tile(QB, 0) >= 0, 0.0, NEG)
    for u in range(NSA_UNITS):
        s_d[u][...] = _dot_nt(unit(q4b, u), ksa[pl.ds(d0, QB), :])
    s_m[0] = _dot_nt(q4b, ksa[:SEL_CHUNK, :])

    cmp_end = (lane * CMP_STRIDE + (CMP_BLOCK - 1))
    valid_c = t_row >= cmp_end
    ps = []
    p_sum = jnp.zeros((QB, LANES), f32)
    row_bcast = lambda col: jnp.broadcast_to(col, (QB, LANES))
    for r in range(GQA_REP):
        s = jnp.where(valid_c, head(s4, r), NEG)
        e = jnp.exp(s - row_bcast(jnp.max(s, axis=1, keepdims=True)))
        p = jnp.where(valid_c, e, 0.0) / row_bcast(jnp.sum(e, axis=1, keepdims=True))
        p_sum = p_sum + p
        ps.append(p.astype(bf16))
    o_cmp4 = _dot(jnp.concatenate(ps, axis=0), vca[...])

    o_w = []
    for u in range(NSA_UNITS):
        softmax_rows(s_w[u], e_w[u], None, None, nwin, bias_w, False)
        o_w.append(own_half(_dot(e_w[u][...], vwa[pl.ds(w0, nwin), :]), u))
    for u in range(NSA_UNITS):
        softmax_rows(s_d[u], e_d[u], m_h[u], None, QB, bias_d, False)
        acc[u][...] = own_half(_dot(e_d[u][...], vsa[pl.ds(d0, QB), :]), u)

    back = t_row // SEL_BLOCK - lane
    valid_s = (back >= 0) & (lane < nsel)

    @pl.when(2 * qi + 2 <= n_top)
    def _all_valid():
        sel_scr[...] = jnp.where(valid_s, 1.0, 0.0)

    @pl.when(2 * qi + 2 > n_top)
    def _top_k():
        p_hi = p_sum.astype(bf16)
        p_lo = (p_sum - p_hi.astype(f32)).astype(bf16)
        imp_t = (_dot_nt(ovt_ref[...], p_hi) + _dot_nt(ovt_ref[...], p_lo))[:SEL_LANES]
        blk = lax.broadcasted_iota(jnp.int32, (SEL_LANES, QB), 0)
        tq = qi * QB + lax.broadcasted_iota(jnp.int32, (SEL_LANES, QB), 1)
        back_t = tq // SEL_BLOCK - blk
        valid_t = (back_t >= 0) & (blk < nsel)
        forced = (blk == 0) | (valid_t & (back_t < N_LOCAL_BLOCKS))
        score = jnp.where(valid_t, imp_t + jnp.where(forced, FORCE_BONUS, 0.0), NEG)
        score = jnp.where(blk < nsel, score, 2.0 * NEG)
        rank = jnp.zeros((SEL_LANES, QB), jnp.int32)
        for n in range(nsel):
            row = score[n:n + 1, :]
            ahead = (row > score) | ((row == score) & (blk > n))
            rank = rank + ahead.astype(jnp.int32)
        sel_t = jnp.where((rank < n_top) & valid_t, 1.0, 0.0)
        sel_t = jnp.concatenate([sel_t, jnp.zeros((LANES - SEL_LANES, QB), f32)], axis=0)
        sel_scr[...] = sel_t.T

    sel_bias = jnp.where((sel_scr[...] > 0.5) & (lane < 2 * qi), 0.0, MASK_BIG)
    sel_bias = pltpu.roll(sel_bias, SEL_LANE0, axis=1)
    in_sel = (lane >= SEL_LANE0) & (lane < SEL_LANE0 + SEL_LANES)
    sel_only = jnp.where(in_sel, sel_bias, 0.0).astype(bf16)
    for r in range(GQA_REP):
        q_scr[r * QB:(r + 1) * QB, :] = jnp.where(in_sel, sel_bias, head(q4, r)).astype(bf16)

    per = SEL_CHUNK // QB
    n_chunks = (qi + per - 1) // per

    def sel_chunk(kc, half, bias=None):
        nxt = jnp.minimum(kc + 1, n_chunks - 1)
        s_m[1 - half] = _dot_nt(q_scr[...], ksa[pl.ds(pl.multiple_of(nxt * SEL_CHUNK, SEL_CHUNK), SEL_CHUNK), :])
        rows = pl.ds(pl.multiple_of(kc * SEL_CHUNK, SEL_CHUNK), SEL_CHUNK)
        for u in range(NSA_UNITS):
            softmax_rows(s_m.at[half, pl.ds(u * ur, ur)], e_m[u], m_h[u], al_h[u], SEL_CHUNK, bias, True)
            acc[u][...] = al_h[u][...] * acc[u][...] + own_half(_dot(e_m[u][...], vsa[rows, :]), u)

    @pl.when(n_chunks > 0)
    def _chunk0():
        sel_chunk(0, 0, _dot_nt(sel_only, ksa[:SEL_CHUNK, :]))

    def sel_body(t, _):
        @pl.when(2 * t + 1 < n_chunks)
        def _odd():
            sel_chunk(2 * t + 1, 1)

        @pl.when(2 * t + 2 < n_chunks)
        def _even():
            sel_chunk(2 * t + 2, 0)
        return 0

    lax.fori_loop(0, n_chunks // 2, sel_body, 0)

    acc_s = jnp.concatenate([a[...] for a in acc], axis=0)
    acc_w = jnp.concatenate(o_w, axis=0)
    for pr in range(GQA_REP // 2):
        ev, od = 2 * pr, 2 * pr + 1
        out = gate_b[:, (3 * pr) * LANES:(3 * pr + 1) * LANES] * jnp.where(left, head(o_cmp4, ev), head(o_cmp4, od))
        for j, a in ((1, acc_s), (2, acc_w)):
            num = jnp.where(left, head(a, ev), head(a, od))
            den = pltpu.roll(jnp.where(left, head(a, od), head(a, ev)), HEAD_DIM, axis=1)
            out = out + gate_b[:, (3 * pr + j) * LANES:(3 * pr + j + 1) * LANES] / den * num
        o_ref[:, pr * LANES:(pr + 1) * LANES] = out.astype(o_ref.dtype)


def _nsa(proj3, kcmp, vcmp, slopes_b, ovt, kaug, caug, gsel):
    B, S, _ = proj3.shape
    ncb = kcmp.shape[1]
    nrow = GQA_REP * QB
    ur = nrow // NSA_UNITS
    nwin = WINDOW + QB
    kv = lambda j: pl.BlockSpec((None, S, LANES), lambda b, g, qi, j=j: (b, 0, COL_KC + j))
    full = lambda a: pl.BlockSpec(a.shape, lambda b, g, qi: (0,) * a.ndim)
    return pl.pallas_call(
        functools.partial(_nsa_kernel, seq=S),
        grid=(B, N_KV_B, S // QB),
        in_specs=[pl.BlockSpec(memory_space=pltpu.SMEM),
                  pl.BlockSpec((None, QB, 2 * LANES), lambda b, g, qi: (b, qi, COL_QB + g)),
                  kv(2), kv(3), kv(4), kv(5),
                  pl.BlockSpec((None, QB, LANES), lambda b, g, qi: (b, qi, COL_GATE + g)),
                  pl.BlockSpec((None, ncb, LANES), lambda b, g, qi: (b, 0, 0)),
                  pl.BlockSpec((None, ncb, LANES), lambda b, g, qi: (b, 0, 0)),
                  full(ovt), full(kaug), full(caug), full(gsel)],
        out_specs=pl.BlockSpec((None, QB, 2 * LANES), lambda b, g, qi: (b, qi, g)),
        out_shape=jax.ShapeDtypeStruct((B, S, N_HEADS_B * HEAD_DIM), bf16),
        scratch_shapes=([pltpu.VMEM((S, LANES), bf16), pltpu.VMEM((S, 2 * LANES), bf16)] * 2
                        + [pltpu.VMEM((ncb, LANES), bf16)] * 2
                        + [pltpu.VMEM((nrow, LANES), bf16), pltpu.VMEM((QB, LANES), f32),
                           pltpu.VMEM((2, nrow, SEL_CHUNK), f32)]
                        + [pltpu.VMEM((ur, nwin), f32)] * NSA_UNITS + [pltpu.VMEM((ur, nwin), bf16)] * NSA_UNITS
                        + [pltpu.VMEM((ur, QB), f32)] * NSA_UNITS + [pltpu.VMEM((ur, QB), bf16)] * NSA_UNITS
                        + [pltpu.VMEM((ur, SEL_CHUNK), bf16)] * NSA_UNITS
                        + [pltpu.VMEM((ur, LANES), f32)] * (2 * NSA_UNITS)
                        + [pltpu.VMEM((ur, LANES), f32)] * NSA_UNITS),
        compiler_params=_params("parallel", "parallel", "arbitrary"),
        name="nsa_attn",
    )(slopes_b, proj3, proj3, proj3, proj3, proj3, proj3, kcmp, vcmp, ovt, kaug, caug, gsel)


def _route(h, w_ref, b_ref):
    w = w_ref[...]
    h_hi, w_hi = h.astype(bf16), w.astype(bf16)
    h_lo, w_lo = (h - h_hi.astype(f32)).astype(bf16), (w - w_hi.astype(f32)).astype(bf16)
    logit = _dot(h_hi, w_hi) + (_dot(h_hi, w_lo) + _dot(h_lo, w_hi)) + b_ref[...]
    tm = logit.shape[0]
    lane = lax.broadcasted_iota(jnp.int32, (tm, LANES), 1)
    big = jnp.int32(LANES)
    is_g = lane < N_GROUPS
    gl = jnp.where(is_g, logit, NEG)
    gmax = jnp.max(gl, axis=1, keepdims=True)
    gsum = jnp.sum(jnp.where(is_g, jnp.exp(gl - gmax), 0.0), axis=1, keepdims=True)
    gsel = jnp.min(jnp.where(is_g & (gl == gmax), lane, big), axis=1, keepdims=True)
    gw = 1.0 / gsum
    e_lane = lane - N_GROUPS
    in_grp = (e_lane >= 0) & (e_lane < N_EXPERTS) & (e_lane // EXPERTS_PER_GROUP == gsel)
    el = jnp.where(in_grp, logit, NEG)
    t1 = jnp.max(el, axis=1, keepdims=True)
    i1 = jnp.min(jnp.where(in_grp & (el == t1), lane, big), axis=1, keepdims=True)
    el2 = jnp.where(lane == i1, NEG, el)
    t2 = jnp.max(el2, axis=1, keepdims=True)
    i2 = jnp.min(jnp.where(in_grp & (lane != i1) & (el2 == t2), lane, big), axis=1, keepdims=True)
    e2 = jnp.exp(t2 - t1)
    w1 = gw / (1.0 + e2)
    w2 = gw * e2 / (1.0 + e2)
    return jnp.where(lane == i1, w1, jnp.where(lane == i2, w2, 0.0)), gsel


def _to_token_tiles(ref, x):
    for s in range(D_MODEL // LANES):
        ref[pl.ds(s, x.shape[0], stride=D_MODEL // LANES), :] = x[:, s * LANES:(s + 1) * LANES]


def _from_token_tiles(ref):
    n = D_MODEL // LANES
    return jnp.concatenate([ref[pl.ds(s, ref.shape[0] // n, stride=n), :] for s in range(n)], axis=1)


def _out_route_kernel(x_ref, oa_ref, ob_ref, wa_ref, wb_ref, g_ref, w_ref, b_ref, tri_ref,
                      x1_ref, stage_ref, meta_ref, cnt_ref, cnt_scr):
    @pl.when(pl.program_id(0) == 0)
    def _zero():
        cnt_scr[...] = jnp.zeros_like(cnt_scr)

    x1 = x_ref[...] + _dot(oa_ref[...], wa_ref[...]) + _dot(ob_ref[...], wb_ref[...])
    x1_ref[...] = x1
    h = _rms(x1, g_ref[...])
    _to_token_tiles(stage_ref, h)
    _, gsel = _route(h, w_ref, b_ref)
    tm = h.shape[0]
    lane = lax.broadcasted_iota(jnp.int32, (tm, LANES), 1)
    is_g = lane < N_GROUPS

    onehot = jnp.where(is_g & (lane == gsel), 1.0, 0.0)
    before = _dot(tri_ref[...], onehot.astype(bf16)) + cnt_scr[...]
    rank = jnp.sum(onehot * before, axis=1, keepdims=True)
    cnt_scr[...] = before[tm - 1:tm, :] + onehot[tm - 1:tm, :]
    cnt_ref[...] = cnt_scr[...]
    rank_hi = jnp.floor(rank * (1.0 / RANK_SPLIT))
    cols = jnp.where(lane == 0, gsel.astype(f32), jnp.where(lane == 1, rank_hi,
                                                            jnp.where(lane == 2, rank - RANK_SPLIT * rank_hi, 0.0)))
    pick = (lax.broadcasted_iota(jnp.int32, (8, LANES), 0) == lax.broadcasted_iota(jnp.int32, (8, LANES), 1))
    meta_ref[...] = _dot_nt(jnp.where(pick, 1.0, 0.0).astype(bf16), cols.astype(bf16))


def _out_route(x2d, oa, ob, wa, wb, g, w, b):
    T = x2d.shape[0]
    tm = MOE_TILE
    assert T // RANK_SPLIT <= 256
    tri = jnp.asarray(np.tril(np.ones((tm, tm), np.float32), -1), bf16)
    row = lambda n: pl.BlockSpec((tm, n), lambda i: (i, 0))
    full = lambda a: pl.BlockSpec(a.shape, lambda i: (0,) * a.ndim)
    return pl.pallas_call(
        _out_route_kernel,
        grid=(T // tm,),
        in_specs=[row(D_MODEL), row(oa.shape[1]), row(ob.shape[1]), full(wa), full(wb),
                  full(g), full(w), full(b), full(tri)],
        out_specs=[row(D_MODEL),
                   pl.BlockSpec((tm * TOKEN_ROWS, LANES), lambda i: (i, 0)),
                   pl.BlockSpec((None, 8, tm), lambda i: (i, 0, 0)),
                   pl.BlockSpec((1, LANES), lambda i: (0, 0))],
        out_shape=[jax.ShapeDtypeStruct((T, D_MODEL), f32),
                   jax.ShapeDtypeStruct((T * TOKEN_ROWS, LANES), f32),
                   jax.ShapeDtypeStruct((T // tm, 8, tm), f32),
                   jax.ShapeDtypeStruct((1, LANES), f32)],
        scratch_shapes=[pltpu.VMEM((1, LANES), f32)],
        compiler_params=_params("arbitrary"),
        name="out_proj_router",
    )(x2d, oa, ob, wa, wb, g, w, b, tri)


def _token_rows(t, n=1):
    return pl.ds(pl.multiple_of(t * TOKEN_ROWS, TOKEN_ROWS), n * TOKEN_ROWS)


def _gathered_tile(idx_ref, src_ref, buf_ref, sems, tm):
    i = pl.program_id(0)
    n = pl.num_programs(0)

    def issue(step):
        slot = step % 2

        def one(k, _):
            pltpu.make_async_copy(src_ref.at[_token_rows(idx_ref[step * tm + k])],
                                  buf_ref.at[slot, _token_rows(k)], sems.at[slot]).start()
            return 0

        lax.fori_loop(0, tm, one, 0, unroll=8)

    @pl.when(i == 0)
    def _first():
        issue(i)

    @pl.when(i + 1 < n)
    def _next():
        issue(i + 1)

    slot = i % 2
    pltpu.make_async_copy(src_ref.at[_token_rows(0, tm)], buf_ref.at[slot], sems.at[slot]).wait()
    return slot


def _moe_kernel(tg_ref, src_ref, stage_ref, wr_ref, br_ref, wg_ref, wu_ref, wd_ref, o_ref, buf_ref, sems):
    grp = tg_ref[pl.program_id(0)]
    slot = _gathered_tile(src_ref, stage_ref, buf_ref, sems, MOE_TILE)

    @pl.when(grp < N_GROUPS)
    def _experts():
        h = _from_token_tiles(buf_ref.at[slot])
        gates, _ = _route(h, wr_ref, br_ref)
        x = h.astype(bf16)
        lane = lax.broadcasted_iota(jnp.int32, gates.shape, 1)
        y = jnp.zeros(h.shape, f32)
        for e in range(EXPERTS_PER_GROUP):
            gcol = jnp.sum(jnp.where(lane == N_GROUPS + EXPERTS_PER_GROUP * grp + e, gates, 0.0),
                           axis=1, keepdims=True)
            a = jax.nn.silu(_dot(x, wg_ref[e])) * _dot(x, wu_ref[e])
            y = y + _dot((a * gcol).astype(bf16), wd_ref[e])
        _to_token_tiles(o_ref, y)

    @pl.when(grp >= N_GROUPS)
    def _unused():
        o_ref[...] = jnp.zeros_like(o_ref)


def _moe(stage, tile_group, slot_token, w_route, b_route, wg, wu, wd, layer):
    tm = MOE_TILE
    n_slots = slot_token.shape[0]
    w_spec = lambda k, n: pl.BlockSpec((None, None, EXPERTS_PER_GROUP, k, n),
                                       lambda j, tg, src: (layer, jnp.minimum(tg[j], N_GROUPS - 1), 0, 0, 0))
    grouped = lambda w: w.reshape(w.shape[0], N_GROUPS, EXPERTS_PER_GROUP, *w.shape[2:])
    return pl.pallas_call(
        _moe_kernel,
        grid_spec=pltpu.PrefetchScalarGridSpec(
            num_scalar_prefetch=2, grid=(n_slots // tm,),
            in_specs=[pl.BlockSpec(memory_space=pl.ANY),
                      pl.BlockSpec((D_MODEL, LANES), lambda j, tg, src: (0, 0)),
                      pl.BlockSpec((1, LANES), lambda j, tg, src: (0, 0)),
                      w_spec(D_MODEL, D_EXPERT), w_spec(D_MODEL, D_EXPERT), w_spec(D_EXPERT, D_MODEL)],
            out_specs=pl.BlockSpec((tm * TOKEN_ROWS, LANES), lambda j, tg, src: (j, 0)),
            scratch_shapes=[pltpu.VMEM((2, tm * TOKEN_ROWS, LANES), f32), pltpu.SemaphoreType.DMA((2,))]),
        out_shape=jax.ShapeDtypeStruct((n_slots * TOKEN_ROWS, LANES), f32),
        compiler_params=_params("arbitrary"),
        name="moe_ffn",
    )(tile_group, slot_token, stage, w_route, b_route, grouped(wg), grouped(wu), grouped(wd))


def _ple_kernel(pos_ref, x_ref, ys_ref, p_ref, g_ref, wg_ref, wp_ref, fg_ref, o_ref, buf_ref, sems, *, final):
    slot = _gathered_tile(pos_ref, ys_ref, buf_ref, sems, x_ref.shape[0])
    x = x_ref[...] + _from_token_tiles(buf_ref.at[slot])
    gate = jax.nn.sigmoid(_dot(_rms(x, g_ref[...]).astype(bf16), wg_ref[...]))
    y = x + gate * _dot(p_ref[...].astype(bf16), wp_ref[...])
    o_ref[...] = _rms(y, fg_ref[...]) if final else y


def _ple(x2d, ys, pos, p3, g, wg, wp, fg, layer, final):
    T = x2d.shape[0]
    tm = 512
    full = lambda a: pl.BlockSpec(a.shape, lambda i, pos: (0,) * a.ndim)
    return pl.pallas_call(
        functools.partial(_ple_kernel, final=final),
        grid_spec=pltpu.PrefetchScalarGridSpec(
            num_scalar_prefetch=1, grid=(T // tm,),
            in_specs=[pl.BlockSpec((tm, D_MODEL), lambda i, pos: (i, 0)),
                      pl.BlockSpec(memory_space=pl.ANY),
                      pl.BlockSpec((None, tm, PLE_DIM), lambda i, pos: (layer, i, 0)),
                      full(g), full(wg), full(wp), full(fg)],
            out_specs=pl.BlockSpec((tm, D_MODEL), lambda i, pos: (i, 0)),
            scratch_shapes=[pltpu.VMEM((2, tm * TOKEN_ROWS, LANES), f32), pltpu.SemaphoreType.DMA((2,))]),
        out_shape=jax.ShapeDtypeStruct((T, D_MODEL), f32),
        compiler_params=_params("arbitrary"),
        name="ple",
    )(pos, x2d, ys, p3, g, wg, wp, fg)


def _alibi_slopes():
    s = 2.0 ** (-8.0 * np.arange(1, N_HEADS_TOTAL + 1) / N_HEADS_TOTAL)
    assert np.all(np.log2(s[1::2]) == np.round(np.log2(s[1::2])))
    return jnp.asarray(s[0::2], f32), jnp.asarray(s[1::2], f32)


def _selection_constants(seq):
    ncb = seq // CMP_STRIDE
    nsel = seq // SEL_BLOCK
    n_cmp = (seq - CMP_BLOCK) // CMP_STRIDE + 1
    cs = np.arange(n_cmp) * CMP_STRIDE
    bs = np.arange(nsel) * SEL_BLOCK
    ov = np.clip(np.minimum(cs[:, None] + CMP_BLOCK, bs[None, :] + SEL_BLOCK)
                 - np.maximum(cs[:, None], bs[None, :]), 0, None) / CMP_BLOCK
    assert ncb == LANES and seq <= POS_SPLIT * 256
    ovt = np.zeros((LANES, ncb), np.float32)
    ovt[:nsel, :n_cmp] = ov.T
    pos = np.arange(seq)
    kaug = np.zeros((seq, LANES), np.float32)
    kaug[:, POS_LANE0:POS_LANE0 + 2] = 1.0
    kaug[:, POS_LANE0 + 2] = pos // POS_SPLIT
    kaug[:, POS_LANE0 + 3] = pos % POS_SPLIT
    kaug[pos, SEL_LANE0 + pos // SEL_BLOCK] = 1.0
    cend = np.arange(ncb) * CMP_STRIDE + CMP_BLOCK - 1
    caug = np.zeros((ncb, LANES), np.float32)
    caug[:, POS_LANE0:POS_LANE0 + 2] = 1.0
    caug[:, POS_LANE0 + 2] = cend // POS_SPLIT
    caug[:, POS_LANE0 + 3] = cend % POS_SPLIT
    gsel = np.zeros((2, LANES, GQA_REP // 2, 3, 2, HEAD_DIM), np.float32)
    for pr in range(GQA_REP // 2):
        for j in range(3):
            for hh in range(2):
                gsel[:, 3 * (2 * pr + hh) + j, pr, j, hh, :] = 1.0
    gsel = gsel.reshape(2 * LANES, (GQA_REP // 2) * 3 * LANES)
    return jnp.asarray(ovt, bf16), jnp.asarray(kaug), jnp.asarray(caug), jnp.asarray(gsel, bf16)


def _dispatch_plan(meta, counts, n_tokens):
    i32 = jnp.int32
    n_slots = n_tokens + N_GROUPS * MOE_TILE
    group = meta[:, 0, :].reshape(n_tokens).astype(i32)
    rank = (meta[:, 1, :] * RANK_SPLIT + meta[:, 2, :]).reshape(n_tokens).astype(i32)
    cnt = counts[0, :N_GROUPS].astype(i32)
    padded = (cnt + MOE_TILE - 1) // MOE_TILE * MOE_TILE
    end = jnp.cumsum(padded)
    pos = (end - padded)[group] + rank
    slot_token = jnp.zeros((n_slots,), i32).at[pos].set(jnp.arange(n_tokens, dtype=i32))
    tile_start = jnp.arange(n_slots // MOE_TILE, dtype=i32) * MOE_TILE
    tile_group = jnp.sum(tile_start[:, None] >= end[None, :], axis=1).astype(i32)
    return pos, slot_token, tile_group


def _block_diag2(w):
    z = jnp.zeros_like(w)
    return jnp.concatenate([jnp.concatenate([w, z], axis=-1), jnp.concatenate([z, w], axis=-1)], axis=-2)


def _layout_w_in(w):
    gate = w[:, N_MAIN:]
    per = GQA_REP * 3
    blocks = [jnp.pad(gate[:, g * per:(g + 1) * per], ((0, 0), (0, LANES - per))) for g in range(N_KV_B)]
    return jnp.concatenate([w[:, :N_MAIN]] + blocks, axis=1).astype(bf16)


def kernel(x, p, attn_norm, w_in, w_out, w_cmp_k1, w_cmp_k2, w_cmp_v1, w_cmp_v2, cmp_pos, ffn_norm, w_route_group, b_route_group, w_route_expert, b_route_expert, w_expert_gate, w_expert_up, w_expert_down, ple_norm, w_ple_gate, w_ple_proj, final_norm):
    B, S, D = x.shape
    depth = w_in.shape[0]
    T = B * S
    slopes_a, slopes_b = _alibi_slopes()
    ovt, kaug, caug, gsel = _selection_constants(S)
    wg_all = w_expert_gate.astype(bf16)
    wu_all = w_expert_up.astype(bf16)
    wd_all = w_expert_down.astype(bf16)
    p3 = p.reshape(depth, T, PLE_DIM)
    row = lambda v: v.reshape(1, -1)
    n_route = N_GROUPS + N_EXPERTS

    x2d = x.reshape(T, D)
    for i in range(depth):
        proj3 = _in_proj(x2d, row(attn_norm[i]), _layout_w_in(w_in[i])).reshape(B, S, N_PROJ)
        oa = _dilated(proj3, slopes_a)
        w1 = lambda w: _block_diag2(w.reshape(CMP_BLOCK, HEAD_DIM, CMP_HIDDEN)).astype(bf16)
        pos_dup = jnp.concatenate([cmp_pos[i], cmp_pos[i]], axis=-1)
        kcmp, vcmp = _compress(proj3, pos_dup, w1(w_cmp_k1[i]), _block_diag2(w_cmp_k2[i]).astype(bf16),
                               w1(w_cmp_v1[i]), _block_diag2(w_cmp_v2[i]).astype(bf16))
        ob = _nsa(proj3, kcmp, vcmp, slopes_b, ovt, kaug, caug, gsel)
        wo = w_out[i].astype(bf16)
        w_route = jnp.pad(jnp.concatenate([w_route_group[i], w_route_expert[i]], axis=1),
                          ((0, 0), (0, LANES - n_route)))
        b_route = jnp.pad(jnp.concatenate([b_route_group[i], b_route_expert[i]]), (0, LANES - n_route))
        x2d, stage, meta, counts = _out_route(x2d, oa.reshape(T, A_W), ob.reshape(T, -1), wo[:A_W], wo[A_W:],
                                              row(ffn_norm[i]), w_route, row(b_route))
        pos, slot_token, tile_group = _dispatch_plan(meta, counts, T)
        ys = _moe(stage, tile_group, slot_token, w_route, row(b_route), wg_all, wu_all, wd_all, i)
        x2d = _ple(x2d, ys, pos, p3, row(ple_norm[i]), w_ple_gate[i].astype(bf16),
                   w_ple_proj[i].astype(bf16), row(final_norm), i, i == depth - 1)
    return x2d.reshape(B, S, D)
```

```python
import functools

import numpy as np
import jax
import jax.numpy as jnp
from jax import lax
from jax.experimental import pallas as pl
from jax.experimental.pallas import tpu as pltpu

D_MODEL = 1024
PLE_DIM = 256
HEAD_DIM = 64
N_HEADS_A = 8
N_HEADS_B = 8
N_KV_B = 2
GQA_REP = N_HEADS_B // N_KV_B
N_HEADS_TOTAL = N_HEADS_A + N_HEADS_B
DILATED_PATTERNS = ((128, 1), (512, 4), (2048, 16))
CMP_BLOCK = 32
CMP_STRIDE = 16
CMP_HIDDEN = 256
SEL_BLOCK = 64
SEL_TOP = 16
N_LOCAL_BLOCKS = 2
WINDOW = 512
N_GROUPS = 4
EXPERTS_PER_GROUP = 4
N_EXPERTS = N_GROUPS * EXPERTS_PER_GROUP
D_EXPERT = 512
RMS_EPS = 1e-6
NEG = -1e30
FORCE_BONUS = 1e4
SCALE = HEAD_DIM ** -0.5

LANES = 128
QB = 128
SEL_CHUNK = 512
NSA_UNITS = 2
DIL_INFLIGHT = 3
DIL_TRIP = 16
POS_LANE0 = HEAD_DIM
POS_SPLIT = 16
SEL_LANE0 = POS_LANE0 + 4
SEL_LANES = 32
MASK_BIG = -(2.0 ** 100)
MOE_TILE = 512
TOKEN_ROWS = D_MODEL // LANES
RANK_SPLIT = 128
A_W = N_HEADS_A * HEAD_DIM
N_MAIN = 3 * A_W + N_HEADS_B * HEAD_DIM + 6 * N_KV_B * HEAD_DIM
N_PROJ = N_MAIN + N_KV_B * LANES
COL_QB = (3 * A_W) // (2 * LANES)
COL_KC = (3 * A_W + N_HEADS_B * HEAD_DIM) // LANES
COL_GATE = N_MAIN // LANES
VMEM_LIMIT = 56 * 1024 * 1024

f32 = jnp.float32
bf16 = jnp.bfloat16


def _dot(a, b):
    return jnp.dot(a, b, preferred_element_type=f32)


def _dot_nt(a, b):
    return lax.dot_general(a, b, (((1,), (1,)), ((), ())), preferred_element_type=f32)


def _rms(x, g):
    return x * lax.rsqrt(jnp.mean(x * x, axis=-1, keepdims=True) + RMS_EPS) * g


def _params(*sem):
    return pltpu.CompilerParams(dimension_semantics=sem, vmem_limit_bytes=VMEM_LIMIT)


def _in_proj_kernel(x_ref, g_ref, w_ref, o_ref):
    h = _rms(x_ref[...], g_ref[...]).astype(bf16)
    for n0 in range(0, N_PROJ, 512):
        o_ref[:, n0:n0 + 512] = _dot(h, w_ref[:, n0:n0 + 512])


def _in_proj(x2d, g, w):
    T = x2d.shape[0]
    tm = 512
    return pl.pallas_call(
        _in_proj_kernel,
        grid=(T // tm,),
        in_specs=[pl.BlockSpec((tm, D_MODEL), lambda i: (i, 0)),
                  pl.BlockSpec((1, D_MODEL), lambda i: (0, 0)),
                  pl.BlockSpec((D_MODEL, N_PROJ), lambda i: (0, 0))],
        out_specs=pl.BlockSpec((tm, N_PROJ), lambda i: (i, 0)),
        out_shape=jax.ShapeDtypeStruct((T, N_PROJ), f32),
        compiler_params=_params("parallel"),
        name="in_proj",
    )(x2d, g, w)


def _dil_kernel(slope_ref, q_ref, k_ref, v_ref, o_ref, out_ref, lse_ref, bias_scr, *bufs, seq):
    hp = pl.program_id(1)
    lane = lax.broadcasted_iota(jnp.int32, (QB, LANES), 1)
    left = lane < HEAD_DIM
    slab = 32
    s_bufs, e_bufs = bufs[:DIL_INFLIGHT], bufs[DIL_INFLIGHT:]

    i = lax.broadcasted_iota(jnp.int32, (QB, 2 * QB), 0)
    j = lax.broadcasted_iota(jnp.int32, (QB, 2 * QB), 1)
    rel = i - j + QB
    valid = (rel >= 0) & (rel <= QB)
    relf = rel.astype(f32)
    for p, (window, dil) in enumerate(DILATED_PATTERNS):
        assert window // dil == QB
        for hh in range(2):
            bias_scr[2 * p + hh] = jnp.where(valid, (-float(dil) * slope_ref[2 * hp + hh]) * relf, NEG)

    ones = jnp.ones((2 * QB, LANES), bf16)

    def scores(u, blk):
        p, dil, row0, key0, nk = blk
        qc = q_ref[pl.ds(row0, QB, stride=dil), :] * SCALE
        q2 = jnp.concatenate([jnp.where(left, qc, 0.0), jnp.where(left, 0.0, qc)], axis=0).astype(bf16)
        s_bufs[u][:, :nk] = _dot_nt(q2, k_ref[pl.ds(key0, nk, stride=dil), :].astype(bf16))

    def probs(u, blk):
        p, dil, row0, key0, nk = blk
        c0 = 2 * QB - nk
        ms = []
        for hh in range(2):
            parts = []
            for sl in range(QB // slab):
                rows = slice(hh * QB + sl * slab, hh * QB + (sl + 1) * slab)
                tiles = [s_bufs[u][rows, c:c + LANES]
                         + bias_scr[2 * p + hh, sl * slab:(sl + 1) * slab, c0 + c:c0 + c + LANES]
                         for c in range(0, nk, LANES)]
                mx = tiles[0]
                for t in tiles[1:]:
                    mx = jnp.maximum(mx, t)
                mx = jnp.broadcast_to(jnp.max(mx, axis=1, keepdims=True), (slab, LANES))
                for c, t in zip(range(0, nk, LANES), tiles):
                    e_bufs[u][rows, c:c + LANES] = jnp.exp(t - mx).astype(bf16)
                parts.append(mx)
            ms.append(jnp.concatenate(parts, axis=0))
        return jnp.where(left, ms[0], ms[1])

    def values(u, blk, row_max):
        p, dil, row0, key0, nk = blk
        v2 = jnp.concatenate([v_ref[pl.ds(key0, nk, stride=dil), :].astype(bf16), ones[:nk]], axis=1)
        res = _dot(e_bufs[u][:, :nk], v2)
        rows = pl.ds(row0, QB, stride=dil)
        den = jnp.where(left, res[:QB, LANES:], res[QB:, LANES:])
        out_ref[p, rows, :] = jnp.where(left, res[:QB, :LANES], res[QB:, :LANES]) / den
        lse_ref[p, rows, :] = row_max + jnp.log(den)

    def run(blocks):
        ahead = DIL_INFLIGHT - 1
        for n in range(min(ahead, len(blocks))):
            scores(n % DIL_INFLIGHT, blocks[n])
        for n, blk in enumerate(blocks):
            if n + ahead < len(blocks):
                scores((n + ahead) % DIL_INFLIGHT, blocks[n + ahead])
            values(n % DIL_INFLIGHT, blk, probs(n % DIL_INFLIGHT, blk))

    def first(p, dil, r):
        return (p, dil, r, r, QB)

    def later(p, dil, r, a):
        return (p, dil, r + dil * QB * a, r + dil * QB * (a - 1), 2 * QB)

    for p, (window, dil) in enumerate(DILATED_PATTERNS):
        nblk = seq // dil // QB
        blocks = [first(p, dil, r) if a == 0 else later(p, dil, r, a) for r in range(dil) for a in range(nblk)]
        for b0 in range(0, len(blocks), DIL_TRIP):
            run(blocks[b0:b0 + DIL_TRIP])

    ch = 256

    def combine(c, _):
        rows = pl.ds(pl.multiple_of(c * ch, ch), ch)
        lses = [lse_ref[p, rows, :] for p in range(len(DILATED_PATTERNS))]
        big = functools.reduce(jnp.maximum, lses)
        num = jnp.zeros((ch, LANES), f32)
        den = jnp.zeros((ch, LANES), f32)
        for p, lse in enumerate(lses):
            w = jnp.exp(lse - big)
            num = num + w * out_ref[p, rows, :]
            den = den + w
        o_ref[rows, :] = (num / den).astype(o_ref.dtype)
        return 0

    lax.fori_loop(0, seq // ch, combine, 0)


def _dilated(proj3, slopes_a):
    B, S, _ = proj3.shape
    npair = N_HEADS_A // 2
    blk = lambda off: pl.BlockSpec((None, S, LANES), lambda b, hp, off=off: (b, 0, off + hp))
    return pl.pallas_call(
        functools.partial(_dil_kernel, seq=S),
        grid=(B, npair),
        in_specs=[pl.BlockSpec(memory_space=pltpu.SMEM), blk(0), blk(npair), blk(2 * npair)],
        out_specs=pl.BlockSpec((None, S, LANES), lambda b, hp: (b, 0, hp)),
        out_shape=jax.ShapeDtypeStruct((B, S, A_W), bf16),
        scratch_shapes=([pltpu.VMEM((len(DILATED_PATTERNS), S, LANES), f32)] * 2
                        + [pltpu.VMEM((2 * len(DILATED_PATTERNS), QB, 2 * QB), f32)]
                        + [pltpu.VMEM((2 * QB, 2 * QB), f32)] * DIL_INFLIGHT
                        + [pltpu.VMEM((2 * QB, 2 * QB), bf16)] * DIL_INFLIGHT),
        compiler_params=_params("parallel", "parallel"),
        name="dilated_attn",
    )(slopes_a, proj3, proj3, proj3)


def _cmp_kernel(kc_ref, vc_ref, pos_ref, w1k_ref, w2k_ref, w1v_ref, w2v_ref, ko_ref, vo_ref, *, ncb):
    half = CMP_BLOCK // 2
    for x_ref, pi, w1_ref, w2_ref, o_ref in ((kc_ref, 0, w1k_ref, w2k_ref, ko_ref),
                                             (vc_ref, 1, w1v_ref, w2v_ref, vo_ref)):
        lo = jnp.zeros((ncb, 2 * CMP_HIDDEN), f32)
        hi = jnp.zeros((ncb, 2 * CMP_HIDDEN), f32)
        for r in range(half):
            x = x_ref[pl.ds(r, ncb, stride=CMP_STRIDE), :]
            lo = lo + _dot((x + pos_ref[pi, r:r + 1, :]).astype(bf16), w1_ref[r])
            hi = hi + _dot((x + pos_ref[pi, r + half:r + half + 1, :]).astype(bf16), w1_ref[r + half])
        h1 = lo + pltpu.roll(hi, ncb - 1, axis=0)
        o_ref[...] = _dot(jax.nn.gelu(h1).astype(bf16), w2_ref[...])


def _compress(proj3, pos_dup, w1k, w2k, w1v, w2v):
    B, S, _ = proj3.shape
    ncb = S // CMP_STRIDE
    full = lambda a: pl.BlockSpec(a.shape, lambda b: (0,) * a.ndim)
    out = jax.ShapeDtypeStruct((B, ncb, LANES), f32)
    return pl.pallas_call(
        functools.partial(_cmp_kernel, ncb=ncb),
        grid=(B,),
        in_specs=[pl.BlockSpec((None, S, LANES), lambda b: (b, 0, COL_KC)),
                  pl.BlockSpec((None, S, LANES), lambda b: (b, 0, COL_KC + 1)),
                  full(pos_dup), full(w1k), full(w2k), full(w1v), full(w2v)],
        out_specs=[pl.BlockSpec((None, ncb, LANES), lambda b: (b, 0, 0))] * 2,
        out_shape=[out, out],
        compiler_params=_params("parallel"),
        name="nsa_compress",
    )(proj3, proj3, pos_dup, w1k, w2k, w1v, w2v)


def _nsa_kernel(slope_ref, q_ref, ks_ref, vs_ref, kw_ref, vw_ref, gl_ref, kcmp_ref, vcmp_ref,
                ovt_ref, kaug_ref, caug_ref, gsel_ref, o_ref,
                ksa, vsa, kwa, vwa, kca, vca, q_scr, sel_scr, s_m, *per_unit, seq):
    g = pl.program_id(1)
    qi = pl.program_id(2)
    nsel = seq // SEL_BLOCK
    n_top = min(SEL_TOP, nsel)
    assert nsel <= SEL_LANES and N_LOCAL_BLOCKS * SEL_BLOCK >= QB and n_top > N_LOCAL_BLOCKS
    slab = 32
    ur = GQA_REP * QB // NSA_UNITS
    s_w, e_w, s_d, e_d, e_m, m_h, al_h, acc = [per_unit[i * NSA_UNITS:(i + 1) * NSA_UNITS] for i in range(8)]

    def group_lanes(x):
        return jnp.where(g == 0, x, pltpu.roll(x, HEAD_DIM, axis=1))

    @pl.when(qi == 0)
    def _prep():
        ch = 256
        lane_c = lax.broadcasted_iota(jnp.int32, (ch, LANES), 1)
        is_k = lane_c < HEAD_DIM
        for c in range(seq // ch):
            sl = slice(c * ch, (c + 1) * ch)
            aug = kaug_ref[sl, :]
            ksa[sl, :] = jnp.where(is_k, group_lanes(ks_ref[sl, :]), aug).astype(bf16)
            kwa[sl, :] = jnp.where(is_k, group_lanes(kw_ref[sl, :]),
                                   jnp.where(lane_c < SEL_LANE0, aug, 0.0)).astype(bf16)
            for src, dst in ((vs_ref, vsa), (vw_ref, vwa)):
                v = group_lanes(src[sl, :])
                dst[sl, :LANES] = jnp.where(is_k, v, 1.0).astype(bf16)
                dst[sl, LANES:] = jnp.where(is_k, 1.0, pltpu.roll(v, HEAD_DIM, axis=1)).astype(bf16)
        lane_k = lax.broadcasted_iota(jnp.int32, kca.shape, 1)
        kca[...] = jnp.where(lane_k < HEAD_DIM, group_lanes(kcmp_ref[...]), caug_ref[...]).astype(bf16)
        vc = group_lanes(vcmp_ref[...])
        vca[...] = jnp.where(lane_k < HEAD_DIM, vc, pltpu.roll(vc, HEAD_DIM, axis=1)).astype(bf16)

    lane = lax.broadcasted_iota(jnp.int32, (QB, LANES), 1)
    left = lane < HEAD_DIM
    ii = lax.broadcasted_iota(jnp.int32, (QB, LANES), 0)
    t_row = qi * QB + ii
    t_hi = (t_row // POS_SPLIT).astype(f32)
    t_lo = (t_row % POS_SPLIT).astype(f32)
    slopes = [slope_ref[g * GQA_REP + r] for r in range(GQA_REP)]
    head = lambda a, r: a[r * QB:(r + 1) * QB]

    def q_head(r):
        x = q_ref[:, (r // 2) * LANES:(r // 2 + 1) * LANES] * SCALE
        if r % 2:
            x = pltpu.roll(x, HEAD_DIM, axis=1)
        m = slopes[r]
        pos = jnp.where(lane == POS_LANE0, (-POS_SPLIT * m) * t_hi,
                        jnp.where(lane == POS_LANE0 + 1, (-m) * t_lo,
                                  jnp.where(lane == POS_LANE0 + 2, POS_SPLIT * m,
                                            jnp.where(lane == POS_LANE0 + 3, m, 0.0))))
        return jnp.where(left, x, pos)

    q4 = jnp.concatenate([q_head(r) for r in range(GQA_REP)], axis=0)
    q4b = q4.astype(bf16)

    def rel_tile(nk, offset):
        i = lax.broadcasted_iota(jnp.int32, (QB, nk), 0)
        j = lax.broadcasted_iota(jnp.int32, (QB, nk), 1)
        return (i - j + offset).astype(f32)

    def softmax_rows(s_ref, e_ref, m_ref, al_ref, nk, bias, running):
        for sl in range(ur // slab):
            rows = slice(sl * slab, (sl + 1) * slab)
            cols = [slice(j * LANES, (j + 1) * LANES) for j in range(nk // LANES)]
            i0 = (sl * slab) % QB
            tiles = [s_ref[rows, c] if bias is None else s_ref[rows, c] + bias[i0:i0 + slab, c] for c in cols]
            mx = tiles[0]
            for t in tiles[1:]:
                mx = jnp.maximum(mx, t)
            m_new = jnp.broadcast_to(jnp.max(mx, axis=1, keepdims=True), (slab, LANES))
            if running:
                m_old = m_ref[rows, :]
                m_new = jnp.maximum(m_old, m_new)
                al_ref[rows, :] = jnp.exp(m_old - m_new)
            if m_ref is not None:
                m_ref[rows, :] = m_new
            for c, t in zip(cols, tiles):
                e_ref[rows, c] = jnp.exp(t - m_new).astype(bf16)

    unit = lambda a, u: a[u * ur:(u + 1) * ur]

    def own_half(pv, u):
        heads = range(u * ur // QB, (u + 1) * ur // QB)
        return jnp.concatenate([pv[(r - heads[0]) * QB:(r - heads[0] + 1) * QB, (r % 2) * LANES:(r % 2 + 1) * LANES]
                                for r in heads], axis=0)

    s4 = _dot_nt(q4b, kca[...])

    sig = jax.nn.sigmoid(gl_ref[...])
    g_hi = sig.astype(bf16)
    g_lo = (sig - g_hi.astype(f32)).astype(bf16)
    gate_b = _dot(jnp.concatenate([g_hi, g_lo], axis=1), gsel_ref[...])

    nwin = WINDOW + QB
    w0 = pl.multiple_of(jnp.maximum(qi - WINDOW // QB, 0) * QB, QB)
    dist_w = rel_tile(nwin, qi * QB - w0)
    bias_w = jnp.where((dist_w >= 0) & (dist_w < WINDOW), 0.0, NEG)
    for u in range(NSA_UNITS):
        s_w[u][...] = _dot_nt(unit(q4b, u), kwa[pl.ds(w0, nwin), :])

    d0 = pl.multiple_of(qi * QB, QB)
    bias_d = jnp.where(rel_tile(QB, 0) >= 0, 0.0, NEG)
    for u in range(NSA_UNITS):
        s_d[u][...] = _dot_nt(unit(q4b, u), ksa[pl.ds(d0, QB), :])
    s_m[0] = _dot_nt(q4b, ksa[:SEL_CHUNK, :])

    cmp_end = (lane * CMP_STRIDE + (CMP_BLOCK - 1))
    valid_c = t_row >= cmp_end
    ps = []
    p_sum = jnp.zeros((QB, LANES), f32)
    row_bcast = lambda col: jnp.broadcast_to(col, (QB, LANES))
    for r in range(GQA_REP):
        s = jnp.where(valid_c, head(s4, r), NEG)
        e = jnp.exp(s - row_bcast(jnp.max(s, axis=1, keepdims=True)))
        p = jnp.where(valid_c, e, 0.0) / row_bcast(jnp.sum(e, axis=1, keepdims=True))
        p_sum = p_sum + p
        ps.append(p.astype(bf16))
    o_cmp4 = _dot(jnp.concatenate(ps, axis=0), vca[...])

    o_w = []
    for u in range(NSA_UNITS):
        softmax_rows(s_w[u], e_w[u], None, None, nwin, bias_w, False)
        o_w.append(own_half(_dot(e_w[u][...], vwa[pl.ds(w0, nwin), :]), u))
    for u in range(NSA_UNITS):
        softmax_rows(s_d[u], e_d[u], m_h[u], None, QB, bias_d, False)
        acc[u][...] = own_half(_dot(e_d[u][...], vsa[pl.ds(d0, QB), :]), u)

    back = t_row // SEL_BLOCK - lane
    valid_s = (back >= 0) & (lane < nsel)

    @pl.when(2 * qi + 2 <= n_top)
    def _all_valid():
        sel_scr[...] = jnp.where(valid_s, 1.0, 0.0)

    @pl.when(2 * qi + 2 > n_top)
    def _top_k():
        p_hi = p_sum.astype(bf16)
        p_lo = (p_sum - p_hi.astype(f32)).astype(bf16)
        imp_t = (_dot_nt(ovt_ref[...], p_hi) + _dot_nt(ovt_ref[...], p_lo))[:SEL_LANES]
        blk = lax.broadcasted_iota(jnp.int32, (SEL_LANES, QB), 0)
        tq = qi * QB + lax.broadcasted_iota(jnp.int32, (SEL_LANES, QB), 1)
        back_t = tq // SEL_BLOCK - blk
        valid_t = (back_t >= 0) & (blk < nsel)
        forced = (blk == 0) | (valid_t & (back_t < N_LOCAL_BLOCKS))
        score = jnp.where(valid_t, imp_t + jnp.where(forced, FORCE_BONUS, 0.0), NEG)
        score = jnp.where(blk < nsel, score, 2.0 * NEG)
        rank = jnp.zeros((SEL_LANES, QB), jnp.int32)
        for n in range(nsel):
            row = score[n:n + 1, :]
            ahead = (row > score) | ((row == score) & (blk > n))
            rank = rank + ahead.astype(jnp.int32)
        sel_t = jnp.where((rank < n_top) & valid_t, 1.0, 0.0)
        sel_t = jnp.concatenate([sel_t, jnp.zeros((LANES - SEL_LANES, QB), f32)], axis=0)
        sel_scr[...] = sel_t.T

    sel_bias = jnp.where((sel_scr[...] > 0.5) & (lane < 2 * qi), 0.0, MASK_BIG)
    sel_bias = pltpu.roll(sel_bias, SEL_LANE0, axis=1)
    in_sel = (lane >= SEL_LANE0) & (lane < SEL_LANE0 + SEL_LANES)
    sel_only = jnp.where(in_sel, sel_bias, 0.0).astype(bf16)
    for r in range(GQA_REP):
        q_scr[r * QB:(r + 1) * QB, :] = jnp.where(in_sel, sel_bias, head(q4, r)).astype(bf16)

    per = SEL_CHUNK // QB
    n_chunks = (qi + per - 1) // per

    bias0 = _dot_nt(sel_only, ksa[:SEL_CHUNK, :])

    def sel_chunks(n):
        for kc in range(n):
            half = kc % 2
            if kc + 1 < n:
                s_m[1 - half] = _dot_nt(q_scr[...], ksa[(kc + 1) * SEL_CHUNK:(kc + 2) * SEL_CHUNK, :])
            for u in range(NSA_UNITS):
                softmax_rows(s_m.at[half, pl.ds(u * ur, ur)], e_m[u], m_h[u], al_h[u], SEL_CHUNK,
                             bias0 if kc == 0 else None, True)
                acc[u][...] = (al_h[u][...] * acc[u][...]
                               + own_half(_dot(e_m[u][...], vsa[kc * SEL_CHUNK:(kc + 1) * SEL_CHUNK, :]), u))

    for n in range(1, (seq // QB - 1 + per - 1) // per + 1):
        pl.when(n_chunks == n)(functools.partial(sel_chunks, n))

    acc_s = jnp.concatenate([a[...] for a in acc], axis=0)
    acc_w = jnp.concatenate(o_w, axis=0)
    for pr in range(GQA_REP // 2):
        ev, od = 2 * pr, 2 * pr + 1
        out = gate_b[:, (3 * pr) * LANES:(3 * pr + 1) * LANES] * jnp.where(left, head(o_cmp4, ev), head(o_cmp4, od))
        for j, a in ((1, acc_s), (2, acc_w)):
            num = jnp.where(left, head(a, ev), head(a, od))
            den = pltpu.roll(jnp.where(left, head(a, od), head(a, ev)), HEAD_DIM, axis=1)
            out = out + gate_b[:, (3 * pr + j) * LANES:(3 * pr + j + 1) * LANES] / den * num
        o_ref[:, pr * LANES:(pr + 1) * LANES] = out.astype(o_ref.dtype)


def _nsa(proj3, kcmp, vcmp, slopes_b, ovt, kaug, caug, gsel):
    B, S, _ = proj3.shape
    ncb = kcmp.shape[1]
    nrow = GQA_REP * QB
    ur = nrow // NSA_UNITS
    nwin = WINDOW + QB
    kv = lambda j: pl.BlockSpec((None, S, LANES), lambda b, g, qi, j=j: (b, 0, COL_KC + j))
    full = lambda a: pl.BlockSpec(a.shape, lambda b, g, qi: (0,) * a.ndim)
    return pl.pallas_call(
        functools.partial(_nsa_kernel, seq=S),
        grid=(B, N_KV_B, S // QB),
        in_specs=[pl.BlockSpec(memory_space=pltpu.SMEM),
                  pl.BlockSpec((None, QB, 2 * LANES), lambda b, g, qi: (b, qi, COL_QB + g)),
                  kv(2), kv(3), kv(4), kv(5),
                  pl.BlockSpec((None, QB, LANES), lambda b, g, qi: (b, qi, COL_GATE + g)),
                  pl.BlockSpec((None, ncb, LANES), lambda b, g, qi: (b, 0, 0)),
                  pl.BlockSpec((None, ncb, LANES), lambda b, g, qi: (b, 0, 0)),
                  full(ovt), full(kaug), full(caug), full(gsel)],
        out_specs=pl.BlockSpec((None, QB, 2 * LANES), lambda b, g, qi: (b, qi, g)),
        out_shape=jax.ShapeDtypeStruct((B, S, N_HEADS_B * HEAD_DIM), bf16),
        scratch_shapes=([pltpu.VMEM((S, LANES), bf16), pltpu.VMEM((S, 2 * LANES), bf16)] * 2
                        + [pltpu.VMEM((ncb, LANES), bf16)] * 2
                        + [pltpu.VMEM((nrow, LANES), bf16), pltpu.VMEM((QB, LANES), f32),
                           pltpu.VMEM((2, nrow, SEL_CHUNK), f32)]
                        + [pltpu.VMEM((ur, nwin), f32)] * NSA_UNITS + [pltpu.VMEM((ur, nwin), bf16)] * NSA_UNITS
                        + [pltpu.VMEM((ur, QB), f32)] * NSA_UNITS + [pltpu.VMEM((ur, QB), bf16)] * NSA_UNITS
                        + [pltpu.VMEM((ur, SEL_CHUNK), bf16)] * NSA_UNITS
                        + [pltpu.VMEM((ur, LANES), f32)] * (2 * NSA_UNITS)
                        + [pltpu.VMEM((ur, LANES), f32)] * NSA_UNITS),
        compiler_params=_params("parallel", "parallel", "arbitrary"),
        name="nsa_attn",
    )(slopes_b, proj3, proj3, proj3, proj3, proj3, proj3, kcmp, vcmp, ovt, kaug, caug, gsel)


def _route(h, w_ref, b_ref):
    w = w_ref[...]
    h_hi, w_hi = h.astype(bf16), w.astype(bf16)
    h_lo, w_lo = (h - h_hi.astype(f32)).astype(bf16), (w - w_hi.astype(f32)).astype(bf16)
    logit = _dot(h_hi, w_hi) + (_dot(h_hi, w_lo) + _dot(h_lo, w_hi)) + b_ref[...]
    tm = logit.shape[0]
    lane = lax.broadcasted_iota(jnp.int32, (tm, LANES), 1)
    big = jnp.int32(LANES)
    is_g = lane < N_GROUPS
    gl = jnp.where(is_g, logit, NEG)
    gmax = jnp.max(gl, axis=1, keepdims=True)
    gsum = jnp.sum(jnp.where(is_g, jnp.exp(gl - gmax), 0.0), axis=1, keepdims=True)
    gsel = jnp.min(jnp.where(is_g & (gl == gmax), lane, big), axis=1, keepdims=True)
    gw = 1.0 / gsum
    e_lane = lane - N_GROUPS
    in_grp = (e_lane >= 0) & (e_lane < N_EXPERTS) & (e_lane // EXPERTS_PER_GROUP == gsel)
    el = jnp.where(in_grp, logit, NEG)
    t1 = jnp.max(el, axis=1, keepdims=True)
    i1 = jnp.min(jnp.where(in_grp & (el == t1), lane, big), axis=1, keepdims=True)
    el2 = jnp.where(lane == i1, NEG, el)
    t2 = jnp.max(el2, axis=1, keepdims=True)
    i2 = jnp.min(jnp.where(in_grp & (lane != i1) & (el2 == t2), lane, big), axis=1, keepdims=True)
    e2 = jnp.exp(t2 - t1)
    w1 = gw / (1.0 + e2)
    w2 = gw * e2 / (1.0 + e2)
    return jnp.where(lane == i1, w1, jnp.where(lane == i2, w2, 0.0)), gsel


def _to_token_tiles(ref, x):
    for s in range(D_MODEL // LANES):
        ref[pl.ds(s, x.shape[0], stride=D_MODEL // LANES), :] = x[:, s * LANES:(s + 1) * LANES]


def _from_token_tiles(ref):
    n = D_MODEL // LANES
    return jnp.concatenate([ref[pl.ds(s, ref.shape[0] // n, stride=n), :] for s in range(n)], axis=1)


def _out_route_kernel(x_ref, oa_ref, ob_ref, wa_ref, wb_ref, g_ref, w_ref, b_ref, tri_ref,
                      x1_ref, stage_ref, meta_ref, cnt_ref, cnt_scr):
    @pl.when(pl.program_id(0) == 0)
    def _zero():
        cnt_scr[...] = jnp.zeros_like(cnt_scr)

    x1 = x_ref[...] + _dot(oa_ref[...], wa_ref[...]) + _dot(ob_ref[...], wb_ref[...])
    x1_ref[...] = x1
    h = _rms(x1, g_ref[...])
    _to_token_tiles(stage_ref, h)
    _, gsel = _route(h, w_ref, b_ref)
    tm = h.shape[0]
    lane = lax.broadcasted_iota(jnp.int32, (tm, LANES), 1)
    is_g = lane < N_GROUPS

    onehot = jnp.where(is_g & (lane == gsel), 1.0, 0.0)
    before = _dot(tri_ref[...], onehot.astype(bf16)) + cnt_scr[...]
    rank = jnp.sum(onehot * before, axis=1, keepdims=True)
    cnt_scr[...] = before[tm - 1:tm, :] + onehot[tm - 1:tm, :]
    cnt_ref[...] = cnt_scr[...]
    rank_hi = jnp.floor(rank * (1.0 / RANK_SPLIT))
    cols = jnp.where(lane == 0, gsel.astype(f32), jnp.where(lane == 1, rank_hi,
                                                            jnp.where(lane == 2, rank - RANK_SPLIT * rank_hi, 0.0)))
    pick = (lax.broadcasted_iota(jnp.int32, (8, LANES), 0) == lax.broadcasted_iota(jnp.int32, (8, LANES), 1))
    meta_ref[...] = _dot_nt(jnp.where(pick, 1.0, 0.0).astype(bf16), cols.astype(bf16))


def _out_route(x2d, oa, ob, wa, wb, g, w, b):
    T = x2d.shape[0]
    tm = MOE_TILE
    assert T // RANK_SPLIT <= 256
    tri = jnp.asarray(np.tril(np.ones((tm, tm), np.float32), -1), bf16)
    row = lambda n: pl.BlockSpec((tm, n), lambda i: (i, 0))
    full = lambda a: pl.BlockSpec(a.shape, lambda i: (0,) * a.ndim)
    return pl.pallas_call(
        _out_route_kernel,
        grid=(T // tm,),
        in_specs=[row(D_MODEL), row(oa.shape[1]), row(ob.shape[1]), full(wa), full(wb),
                  full(g), full(w), full(b), full(tri)],
        out_specs=[row(D_MODEL),
                   pl.BlockSpec((tm * TOKEN_ROWS, LANES), lambda i: (i, 0)),
                   pl.BlockSpec((None, 8, tm), lambda i: (i, 0, 0)),
                   pl.BlockSpec((1, LANES), lambda i: (0, 0))],
        out_shape=[jax.ShapeDtypeStruct((T, D_MODEL), f32),
                   jax.ShapeDtypeStruct((T * TOKEN_ROWS, LANES), f32),
                   jax.ShapeDtypeStruct((T // tm, 8, tm), f32),
                   jax.ShapeDtypeStruct((1, LANES), f32)],
        scratch_shapes=[pltpu.VMEM((1, LANES), f32)],
        compiler_params=_params("arbitrary"),
        name="out_proj_router",
    )(x2d, oa, ob, wa, wb, g, w, b, tri)


def _token_rows(t, n=1):
    return pl.ds(pl.multiple_of(t * TOKEN_ROWS, TOKEN_ROWS), n * TOKEN_ROWS)


def _gathered_tile(idx_ref, src_ref, buf_ref, sems, tm):
    i = pl.program_id(0)
    n = pl.num_programs(0)

    def issue(step):
        slot = step % 2

        def one(k, _):
            pltpu.make_async_copy(src_ref.at[_token_rows(idx_ref[step * tm + k])],
                                  buf_ref.at[slot, _token_rows(k)], sems.at[slot]).start()
            return 0

        lax.fori_loop(0, tm, one, 0, unroll=8)

    @pl.when(i == 0)
    def _first():
        issue(i)

    @pl.when(i + 1 < n)
    def _next():
        issue(i + 1)

    slot = i % 2
    pltpu.make_async_copy(src_ref.at[_token_rows(0, tm)], buf_ref.at[slot], sems.at[slot]).wait()
    return slot


def _moe_kernel(tg_ref, src_ref, stage_ref, wr_ref, br_ref, wg_ref, wu_ref, wd_ref, o_ref, buf_ref, sems):
    grp = tg_ref[pl.program_id(0)]
    slot = _gathered_tile(src_ref, stage_ref, buf_ref, sems, MOE_TILE)

    @pl.when(grp < N_GROUPS)
    def _experts():
        h = _from_token_tiles(buf_ref.at[slot])
        gates, _ = _route(h, wr_ref, br_ref)
        x = h.astype(bf16)
        lane = lax.broadcasted_iota(jnp.int32, gates.shape, 1)
        y = jnp.zeros(h.shape, f32)
        for e in range(EXPERTS_PER_GROUP):
            gcol = jnp.sum(jnp.where(lane == N_GROUPS + EXPERTS_PER_GROUP * grp + e, gates, 0.0),
                           axis=1, keepdims=True)
            a = jax.nn.silu(_dot(x, wg_ref[e])) * _dot(x, wu_ref[e])
            y = y + _dot((a * gcol).astype(bf16), wd_ref[e])
        _to_token_tiles(o_ref, y)

    @pl.when(grp >= N_GROUPS)
    def _unused():
        o_ref[...] = jnp.zeros_like(o_ref)


def _moe(stage, tile_group, slot_token, w_route, b_route, wg, wu, wd, layer):
    tm = MOE_TILE
    n_slots = slot_token.shape[0]
    w_spec = lambda k, n: pl.BlockSpec((None, None, EXPERTS_PER_GROUP, k, n),
                                       lambda j, tg, src: (layer, jnp.minimum(tg[j], N_GROUPS - 1), 0, 0, 0))
    grouped = lambda w: w.reshape(w.shape[0], N_GROUPS, EXPERTS_PER_GROUP, *w.shape[2:])
    return pl.pallas_call(
        _moe_kernel,
        grid_spec=pltpu.PrefetchScalarGridSpec(
            num_scalar_prefetch=2, grid=(n_slots // tm,),
            in_specs=[pl.BlockSpec(memory_space=pl.ANY),
                      pl.BlockSpec((D_MODEL, LANES), lambda j, tg, src: (0, 0)),
                      pl.BlockSpec((1, LANES), lambda j, tg, src: (0, 0)),
                      w_spec(D_MODEL, D_EXPERT), w_spec(D_MODEL, D_EXPERT), w_spec(D_EXPERT, D_MODEL)],
            out_specs=pl.BlockSpec((tm * TOKEN_ROWS, LANES), lambda j, tg, src: (j, 0)),
            scratch_shapes=[pltpu.VMEM((2, tm * TOKEN_ROWS, LANES), f32), pltpu.SemaphoreType.DMA((2,))]),
        out_shape=jax.ShapeDtypeStruct((n_slots * TOKEN_ROWS, LANES), f32),
        compiler_params=_params("arbitrary"),
        name="moe_ffn",
    )(tile_group, slot_token, stage, w_route, b_route, grouped(wg), grouped(wu), grouped(wd))


def _ple_kernel(pos_ref, x_ref, ys_ref, p_ref, g_ref, wg_ref, wp_ref, fg_ref, o_ref, buf_ref, sems, *, final):
    slot = _gathered_tile(pos_ref, ys_ref, buf_ref, sems, x_ref.shape[0])
    x = x_ref[...] + _from_token_tiles(buf_ref.at[slot])
    gate = jax.nn.sigmoid(_dot(_rms(x, g_ref[...]).astype(bf16), wg_ref[...]))
    y = x + gate * _dot(p_ref[...].astype(bf16), wp_ref[...])
    o_ref[...] = _rms(y, fg_ref[...]) if final else y


def _ple(x2d, ys, pos, p3, g, wg, wp, fg, layer, final):
    T = x2d.shape[0]
    tm = 512
    full = lambda a: pl.BlockSpec(a.shape, lambda i, pos: (0,) * a.ndim)
    return pl.pallas_call(
        functools.partial(_ple_kernel, final=final),
        grid_spec=pltpu.PrefetchScalarGridSpec(
            num_scalar_prefetch=1, grid=(T // tm,),
            in_specs=[pl.BlockSpec((tm, D_MODEL), lambda i, pos: (i, 0)),
                      pl.BlockSpec(memory_space=pl.ANY),
                      pl.BlockSpec((None, tm, PLE_DIM), lambda i, pos: (layer, i, 0)),
                      full(g), full(wg), full(wp), full(fg)],
            out_specs=pl.BlockSpec((tm, D_MODEL), lambda i, pos: (i, 0)),
            scratch_shapes=[pltpu.VMEM((2, tm * TOKEN_ROWS, LANES), f32), pltpu.SemaphoreType.DMA((2,))]),
        out_shape=jax.ShapeDtypeStruct((T, D_MODEL), f32),
        compiler_params=_params("arbitrary"),
        name="ple",
    )(pos, x2d, ys, p3, g, wg, wp, fg)


def _alibi_slopes():
    s = 2.0 ** (-8.0 * np.arange(1, N_HEADS_TOTAL + 1) / N_HEADS_TOTAL)
    assert np.all(np.log2(s[1::2]) == np.round(np.log2(s[1::2])))
    return jnp.asarray(s[0::2], f32), jnp.asarray(s[1::2], f32)


def _selection_constants(seq):
    ncb = seq // CMP_STRIDE
    nsel = seq // SEL_BLOCK
    n_cmp = (seq - CMP_BLOCK) // CMP_STRIDE + 1
    cs = np.arange(n_cmp) * CMP_STRIDE
    bs = np.arange(nsel) * SEL_BLOCK
    ov = np.clip(np.minimum(cs[:, None] + CMP_BLOCK, bs[None, :] + SEL_BLOCK)
                 - np.maximum(cs[:, None], bs[None, :]), 0, None) / CMP_BLOCK
    assert ncb == LANES and seq <= POS_SPLIT * 256
    ovt = np.zeros((LANES, ncb), np.float32)
    ovt[:nsel, :n_cmp] = ov.T
    pos = np.arange(seq)
    kaug = np.zeros((seq, LANES), np.float32)
    kaug[:, POS_LANE0:POS_LANE0 + 2] = 1.0
    kaug[:, POS_LANE0 + 2] = pos // POS_SPLIT
    kaug[:, POS_LANE0 + 3] = pos % POS_SPLIT
    kaug[pos, SEL_LANE0 + pos // SEL_BLOCK] = 1.0
    cend = np.arange(ncb) * CMP_STRIDE + CMP_BLOCK - 1
    caug = np.zeros((ncb, LANES), np.float32)
    caug[:, POS_LANE0:POS_LANE0 + 2] = 1.0
    caug[:, POS_LANE0 + 2] = cend // POS_SPLIT
    caug[:, POS_LANE0 + 3] = cend % POS_SPLIT
    gsel = np.zeros((2, LANES, GQA_REP // 2, 3, 2, HEAD_DIM), np.float32)
    for pr in range(GQA_REP // 2):
        for j in range(3):
            for hh in range(2):
                gsel[:, 3 * (2 * pr + hh) + j, pr, j, hh, :] = 1.0
    gsel = gsel.reshape(2 * LANES, (GQA_REP // 2) * 3 * LANES)
    return jnp.asarray(ovt, bf16), jnp.asarray(kaug), jnp.asarray(caug), jnp.asarray(gsel, bf16)


def _dispatch_plan(meta, counts, n_tokens):
    i32 = jnp.int32
    n_slots = n_tokens + N_GROUPS * MOE_TILE
    group = meta[:, 0, :].reshape(n_tokens).astype(i32)
    rank = (meta[:, 1, :] * RANK_SPLIT + meta[:, 2, :]).reshape(n_tokens).astype(i32)
    cnt = counts[0, :N_GROUPS].astype(i32)
    padded = (cnt + MOE_TILE - 1) // MOE_TILE * MOE_TILE
    end = jnp.cumsum(padded)
    pos = (end - padded)[group] + rank
    slot_token = jnp.zeros((n_slots,), i32).at[pos].set(jnp.arange(n_tokens, dtype=i32))
    tile_start = jnp.arange(n_slots // MOE_TILE, dtype=i32) * MOE_TILE
    tile_group = jnp.sum(tile_start[:, None] >= end[None, :], axis=1).astype(i32)
    return pos, slot_token, tile_group


def _block_diag2(w):
    z = jnp.zeros_like(w)
    return jnp.concatenate([jnp.concatenate([w, z], axis=-1), jnp.concatenate([z, w], axis=-1)], axis=-2)


def _layout_w_in(w):
    gate = w[:, N_MAIN:]
    per = GQA_REP * 3
    blocks = [jnp.pad(gate[:, g * per:(g + 1) * per], ((0, 0), (0, LANES - per))) for g in range(N_KV_B)]
    return jnp.concatenate([w[:, :N_MAIN]] + blocks, axis=1).astype(bf16)


def kernel(x, p, attn_norm, w_in, w_out, w_cmp_k1, w_cmp_k2, w_cmp_v1, w_cmp_v2, cmp_pos, ffn_norm, w_route_group, b_route_group, w_route_expert, b_route_expert, w_expert_gate, w_expert_up, w_expert_down, ple_norm, w_ple_gate, w_ple_proj, final_norm):
    B, S, D = x.shape
    depth = w_in.shape[0]
    T = B * S
    slopes_a, slopes_b = _alibi_slopes()
    ovt, kaug, caug, gsel = _selection_constants(S)
    wg_all = w_expert_gate.astype(bf16)
    wu_all = w_expert_up.astype(bf16)
    wd_all = w_expert_down.astype(bf16)
    p3 = p.reshape(depth, T, PLE_DIM)
    row = lambda v: v.reshape(1, -1)
    n_route = N_GROUPS + N_EXPERTS

    x2d = x.reshape(T, D)
    for i in range(depth):
        proj3 = _in_proj(x2d, row(attn_norm[i]), _layout_w_in(w_in[i])).reshape(B, S, N_PROJ)
        oa = _dilated(proj3, slopes_a)
        w1 = lambda w: _block_diag2(w.reshape(CMP_BLOCK, HEAD_DIM, CMP_HIDDEN)).astype(bf16)
        pos_dup = jnp.concatenate([cmp_pos[i], cmp_pos[i]], axis=-1)
        kcmp, vcmp = _compress(proj3, pos_dup, w1(w_cmp_k1[i]), _block_diag2(w_cmp_k2[i]).astype(bf16),
                               w1(w_cmp_v1[i]), _block_diag2(w_cmp_v2[i]).astype(bf16))
        ob = _nsa(proj3, kcmp, vcmp, slopes_b, ovt, kaug, caug, gsel)
        wo = w_out[i].astype(bf16)
        w_route = jnp.pad(jnp.concatenate([w_route_group[i], w_route_expert[i]], axis=1),
                          ((0, 0), (0, LANES - n_route)))
        b_route = jnp.pad(jnp.concatenate([b_route_group[i], b_route_expert[i]]), (0, LANES - n_route))
        x2d, stage, meta, counts = _out_route(x2d, oa.reshape(T, A_W), ob.reshape(T, -1), wo[:A_W], wo[A_W:],
                                              row(ffn_norm[i]), w_route, row(b_route))
        pos, slot_token, tile_group = _dispatch_plan(meta, counts, T)
        ys = _moe(stage, tile_group, slot_token, w_route, row(b_route), wg_all, wu_all, wd_all, i)
        x2d = _ple(x2d, ys, pos, p3, row(ple_norm[i]), w_ple_gate[i].astype(bf16),
                   w_ple_proj[i].astype(bf16), row(final_norm), i, i == depth - 1)
    return x2d.reshape(B, S, D)
```

```python
import functools

import numpy as np
import jax
import jax.numpy as jnp
from jax import lax
from jax.experimental import pallas as pl
from jax.experimental.pallas import tpu as pltpu

D_MODEL = 1024
PLE_DIM = 256
HEAD_DIM = 64
N_HEADS_A = 8
N_HEADS_B = 8
N_KV_B = 2
GQA_REP = N_HEADS_B // N_KV_B
N_HEADS_TOTAL = N_HEADS_A + N_HEADS_B
DILATED_PATTERNS = ((128, 1), (512, 4), (2048, 16))
CMP_BLOCK = 32
CMP_STRIDE = 16
CMP_HIDDEN = 256
SEL_BLOCK = 64
SEL_TOP = 16
N_LOCAL_BLOCKS = 2
WINDOW = 512
N_GROUPS = 4
EXPERTS_PER_GROUP = 4
N_EXPERTS = N_GROUPS * EXPERTS_PER_GROUP
D_EXPERT = 512
RMS_EPS = 1e-6
NEG = -1e30
FORCE_BONUS = 1e4
SCALE = HEAD_DIM ** -0.5

LANES = 128
QB = 128
SEL_CHUNK = 512
NSA_UNITS = 2
DIL_INFLIGHT = 3
DIL_TRIP = 16
POS_LANE0 = HEAD_DIM
POS_SPLIT = 16
SEL_LANE0 = POS_LANE0 + 4
SEL_LANES = 32
MASK_BIG = -(2.0 ** 100)
MOE_TILE = 512
TOKEN_ROWS = D_MODEL // LANES
RANK_SPLIT = 128
A_W = N_HEADS_A * HEAD_DIM
N_MAIN = 3 * A_W + N_HEADS_B * HEAD_DIM + 6 * N_KV_B * HEAD_DIM
N_PROJ = N_MAIN + N_KV_B * LANES
COL_QB = (3 * A_W) // (2 * LANES)
COL_KC = (3 * A_W + N_HEADS_B * HEAD_DIM) // LANES
COL_GATE = N_MAIN // LANES
VMEM_LIMIT = 56 * 1024 * 1024

f32 = jnp.float32
bf16 = jnp.bfloat16


def _dot(a, b):
    return jnp.dot(a, b, preferred_element_type=f32)


def _dot_nt(a, b):
    return lax.dot_general(a, b, (((1,), (1,)), ((), ())), preferred_element_type=f32)


def _rms(x, g):
    return x * lax.rsqrt(jnp.mean(x * x, axis=-1, keepdims=True) + RMS_EPS) * g


def _params(*sem):
    return pltpu.CompilerParams(dimension_semantics=sem, vmem_limit_bytes=VMEM_LIMIT)


def _in_proj_kernel(x_ref, g_ref, w_ref, o_ref):
    h = _rms(x_ref[...], g_ref[...]).astype(bf16)
    for n0 in range(0, N_PROJ, 512):
        o_ref[:, n0:n0 + 512] = _dot(h, w_ref[:, n0:n0 + 512])


def _in_proj(x2d, g, w):
    T = x2d.shape[0]
    tm = 512
    return pl.pallas_call(
        _in_proj_kernel,
        grid=(T // tm,),
        in_specs=[pl.BlockSpec((tm, D_MODEL), lambda i: (i, 0)),
                  pl.BlockSpec((1, D_MODEL), lambda i: (0, 0)),
                  pl.BlockSpec((D_MODEL, N_PROJ), lambda i: (0, 0))],
        out_specs=pl.BlockSpec((tm, N_PROJ), lambda i: (i, 0)),
        out_shape=jax.ShapeDtypeStruct((T, N_PROJ), f32),
        compiler_params=_params("parallel"),
        name="in_proj",
    )(x2d, g, w)


def _dil_kernel(slope_ref, q_ref, k_ref, v_ref, o_ref, out_ref, lse_ref, bias_scr, *bufs, seq):
    hp = pl.program_id(1)
    lane = lax.broadcasted_iota(jnp.int32, (QB, LANES), 1)
    left = lane < HEAD_DIM
    slab = 32
    s_bufs, e_bufs = bufs[:DIL_INFLIGHT], bufs[DIL_INFLIGHT:]

    i = lax.broadcasted_iota(jnp.int32, (QB, 2 * QB), 0)
    j = lax.broadcasted_iota(jnp.int32, (QB, 2 * QB), 1)
    rel = i - j + QB
    valid = (rel >= 0) & (rel <= QB)
    relf = rel.astype(f32)
    for p, (window, dil) in enumerate(DILATED_PATTERNS):
        assert window // dil == QB
        for hh in range(2):
            bias_scr[2 * p + hh] = jnp.where(valid, (-float(dil) * slope_ref[2 * hp + hh]) * relf, NEG)

    ones = jnp.ones((2 * QB, LANES), bf16)

    def scores(u, blk):
        p, dil, row0, key0, nk = blk
        qc = q_ref[pl.ds(row0, QB, stride=dil), :] * SCALE
        q2 = jnp.concatenate([jnp.where(left, qc, 0.0), jnp.where(left, 0.0, qc)], axis=0).astype(bf16)
        s_bufs[u][:, :nk] = _dot_nt(q2, k_ref[pl.ds(key0, nk, stride=dil), :].astype(bf16))

    def probs(u, blk):
        p, dil, row0, key0, nk = blk
        c0 = 2 * QB - nk
        ms = []
        for hh in range(2):
            parts = []
            for sl in range(QB // slab):
                rows = slice(hh * QB + sl * slab, hh * QB + (sl + 1) * slab)
                tiles = [s_bufs[u][rows, c:c + LANES]
                         + bias_scr[2 * p + hh, sl * slab:(sl + 1) * slab, c0 + c:c0 + c + LANES]
                         for c in range(0, nk, LANES)]
                mx = tiles[0]
                for t in tiles[1:]:
                    mx = jnp.maximum(mx, t)
                mx = jnp.broadcast_to(jnp.max(mx, axis=1, keepdims=True), (slab, LANES))
                for c, t in zip(range(0, nk, LANES), tiles):
                    e_bufs[u][rows, c:c + LANES] = jnp.exp(t - mx).astype(bf16)
                parts.append(mx)
            ms.append(jnp.concatenate(parts, axis=0))
        return jnp.where(left, ms[0], ms[1])

    def values(u, blk, row_max):
        p, dil, row0, key0, nk = blk
        v2 = jnp.concatenate([v_ref[pl.ds(key0, nk, stride=dil), :].astype(bf16), ones[:nk]], axis=1)
        res = _dot(e_bufs[u][:, :nk], v2)
        rows = pl.ds(row0, QB, stride=dil)
        den = jnp.where(left, res[:QB, LANES:], res[QB:, LANES:])
        out_ref[p, rows, :] = jnp.where(left, res[:QB, :LANES], res[QB:, :LANES]) / den
        lse_ref[p, rows, :] = row_max + jnp.log(den)

    def run(blocks):
        ahead = DIL_INFLIGHT - 1
        for n in range(min(ahead, len(blocks))):
            scores(n % DIL_INFLIGHT, blocks[n])
        for n, blk in enumerate(blocks):
            if n + ahead < len(blocks):
                scores((n + ahead) % DIL_INFLIGHT, blocks[n + ahead])
            values(n % DIL_INFLIGHT, blk, probs(n % DIL_INFLIGHT, blk))

    def first(p, dil, r):
        return (p, dil, r, r, QB)

    def later(p, dil, r, a):
        return (p, dil, r + dil * QB * a, r + dil * QB * (a - 1), 2 * QB)

    for p, (window, dil) in enumerate(DILATED_PATTERNS):
        nblk = seq // dil // QB
        blocks = [first(p, dil, r) if a == 0 else later(p, dil, r, a) for r in range(dil) for a in range(nblk)]
        for b0 in range(0, len(blocks), DIL_TRIP):
            run(blocks[b0:b0 + DIL_TRIP])

    ch = 256

    def combine(c, _):
        rows = pl.ds(pl.multiple_of(c * ch, ch), ch)
        lses = [lse_ref[p, rows, :] for p in range(len(DILATED_PATTERNS))]
        big = functools.reduce(jnp.maximum, lses)
        num = jnp.zeros((ch, LANES), f32)
        den = jnp.zeros((ch, LANES), f32)
        for p, lse in enumerate(lses):
            w = jnp.exp(lse - big)
            num = num + w * out_ref[p, rows, :]
            den = den + w
        o_ref[rows, :] = (num / den).astype(o_ref.dtype)
        return 0

    lax.fori_loop(0, seq // ch, combine, 0)


def _dilated(proj3, slopes_a):
    B, S, _ = proj3.shape
    npair = N_HEADS_A // 2
    blk = lambda off: pl.BlockSpec((None, S, LANES), lambda b, hp, off=off: (b, 0, off + hp))
    return pl.pallas_call(
        functools.partial(_dil_kernel, seq=S),
        grid=(B, npair),
        in_specs=[pl.BlockSpec(memory_space=pltpu.SMEM), blk(0), blk(npair), blk(2 * npair)],
        out_specs=pl.BlockSpec((None, S, LANES), lambda b, hp: (b, 0, hp)),
        out_shape=jax.ShapeDtypeStruct((B, S, A_W), bf16),
        scratch_shapes=([pltpu.VMEM((len(DILATED_PATTERNS), S, LANES), f32)] * 2
                        + [pltpu.VMEM((2 * len(DILATED_PATTERNS), QB, 2 * QB), f32)]
                        + [pltpu.VMEM((2 * QB, 2 * QB), f32)] * DIL_INFLIGHT
                        + [pltpu.VMEM((2 * QB, 2 * QB), bf16)] * DIL_INFLIGHT),
        compiler_params=_params("parallel", "parallel"),
        name="dilated_attn",
    )(slopes_a, proj3, proj3, proj3)


def _cmp_kernel(kc_ref, vc_ref, pos_ref, w1k_ref, w2k_ref, w1v_ref, w2v_ref, ko_ref, vo_ref, *, ncb):
    half = CMP_BLOCK // 2
    for x_ref, pi, w1_ref, w2_ref, o_ref in ((kc_ref, 0, w1k_ref, w2k_ref, ko_ref),
                                             (vc_ref, 1, w1v_ref, w2v_ref, vo_ref)):
        lo = jnp.zeros((ncb, 2 * CMP_HIDDEN), f32)
        hi = jnp.zeros((ncb, 2 * CMP_HIDDEN), f32)
        for r in range(half):
            x = x_ref[pl.ds(r, ncb, stride=CMP_STRIDE), :]
            lo = lo + _dot((x + pos_ref[pi, r:r + 1, :]).astype(bf16), w1_ref[r])
            hi = hi + _dot((x + pos_ref[pi, r + half:r + half + 1, :]).astype(bf16), w1_ref[r + half])
        h1 = lo + pltpu.roll(hi, ncb - 1, axis=0)
        o_ref[...] = _dot(jax.nn.gelu(h1).astype(bf16), w2_ref[...])


def _compress(proj3, pos_dup, w1k, w2k, w1v, w2v):
    B, S, _ = proj3.shape
    ncb = S // CMP_STRIDE
    full = lambda a: pl.BlockSpec(a.shape, lambda b: (0,) * a.ndim)
    out = jax.ShapeDtypeStruct((B, ncb, LANES), f32)
    return pl.pallas_call(
        functools.partial(_cmp_kernel, ncb=ncb),
        grid=(B,),
        in_specs=[pl.BlockSpec((None, S, LANES), lambda b: (b, 0, COL_KC)),
                  pl.BlockSpec((None, S, LANES), lambda b: (b, 0, COL_KC + 1)),
                  full(pos_dup), full(w1k), full(w2k), full(w1v), full(w2v)],
        out_specs=[pl.BlockSpec((None, ncb, LANES), lambda b: (b, 0, 0))] * 2,
        out_shape=[out, out],
        compiler_params=_params("parallel"),
        name="nsa_compress",
    )(proj3, proj3, pos_dup, w1k, w2k, w1v, w2v)


def _nsa_kernel(slope_ref, q_ref, ks_ref, vs_ref, kw_ref, vw_ref, gl_ref, kcmp_ref, vcmp_ref,
                ovt_ref, kaug_ref, caug_ref, gsel_ref, o_ref,
                ksa, vsa, kwa, vwa, kca, vca, q_scr, sel_scr, s_m, *per_unit, seq):
    g = pl.program_id(1)
    qi = pl.program_id(2)
    nsel = seq // SEL_BLOCK
    n_top = min(SEL_TOP, nsel)
    assert nsel <= SEL_LANES and N_LOCAL_BLOCKS * SEL_BLOCK >= QB and n_top > N_LOCAL_BLOCKS
    slab = 32
    ur = GQA_REP * QB // NSA_UNITS
    s_w, e_w, s_d, e_d, e_m, m_h, al_h, acc = [per_unit[i * NSA_UNITS:(i + 1) * NSA_UNITS] for i in range(8)]

    def group_lanes(x):
        return jnp.where(g == 0, x, pltpu.roll(x, HEAD_DIM, axis=1))

    @pl.when(qi == 0)
    def _prep():
        ch = 256
        lane_c = lax.broadcasted_iota(jnp.int32, (ch, LANES), 1)
        is_k = lane_c < HEAD_DIM
        for c in range(seq // ch):
            sl = slice(c * ch, (c + 1) * ch)
            aug = kaug_ref[sl, :]
            ksa[sl, :] = jnp.where(is_k, group_lanes(ks_ref[sl, :]), aug).astype(bf16)
            kwa[sl, :] = jnp.where(is_k, group_lanes(kw_ref[sl, :]),
                                   jnp.where(lane_c < SEL_LANE0, aug, 0.0)).astype(bf16)
            for src, dst in ((vs_ref, vsa), (vw_ref, vwa)):
                v = group_lanes(src[sl, :])
                dst[sl, :LANES] = jnp.where(is_k, v, 1.0).astype(bf16)
                dst[sl, LANES:] = jnp.where(is_k, 1.0, pltpu.roll(v, HEAD_DIM, axis=1)).astype(bf16)
        lane_k = lax.broadcasted_iota(jnp.int32, kca.shape, 1)
        kca[...] = jnp.where(lane_k < HEAD_DIM, group_lanes(kcmp_ref[...]), caug_ref[...]).astype(bf16)
        vc = group_lanes(vcmp_ref[...])
        vca[...] = jnp.where(lane_k < HEAD_DIM, vc, pltpu.roll(vc, HEAD_DIM, axis=1)).astype(bf16)

    lane = lax.broadcasted_iota(jnp.int32, (QB, LANES), 1)
    left = lane < HEAD_DIM
    ii = lax.broadcasted_iota(jnp.int32, (QB, LANES), 0)
    t_row = qi * QB + ii
    t_hi = (t_row // POS_SPLIT).astype(f32)
    t_lo = (t_row % POS_SPLIT).astype(f32)
    slopes = [slope_ref[g * GQA_REP + r] for r in range(GQA_REP)]
    head = lambda a, r: a[r * QB:(r + 1) * QB]

    def q_head(r):
        x = q_ref[:, (r // 2) * LANES:(r // 2 + 1) * LANES] * SCALE
        if r % 2:
            x = pltpu.roll(x, HEAD_DIM, axis=1)
        m = slopes[r]
        pos = jnp.where(lane == POS_LANE0, (-POS_SPLIT * m) * t_hi,
                        jnp.where(lane == POS_LANE0 + 1, (-m) * t_lo,
                                  jnp.where(lane == POS_LANE0 + 2, POS_SPLIT * m,
                                            jnp.where(lane == POS_LANE0 + 3, m, 0.0))))
        return jnp.where(left, x, pos)

    q4 = jnp.concatenate([q_head(r) for r in range(GQA_REP)], axis=0)
    q4b = q4.astype(bf16)

    def rel_tile(nk, offset):
        i = lax.broadcasted_iota(jnp.int32, (QB, nk), 0)
        j = lax.broadcasted_iota(jnp.int32, (QB, nk), 1)
        return (i - j + offset).astype(f32)

    def softmax_rows(s_ref, e_ref, m_ref, al_ref, nk, bias, running):
        for sl in range(ur // slab):
            rows = slice(sl * slab, (sl + 1) * slab)
            cols = [slice(j * LANES, (j + 1) * LANES) for j in range(nk // LANES)]
            i0 = (sl * slab) % QB
            tiles = [s_ref[rows, c] if bias is None else s_ref[rows, c] + bias[i0:i0 + slab, c] for c in cols]
            mx = tiles[0]
            for t in tiles[1:]:
                mx = jnp.maximum(mx, t)
            m_new = jnp.broadcast_to(jnp.max(mx, axis=1, keepdims=True), (slab, LANES))
            if running:
                m_old = m_ref[rows, :]
                m_new = jnp.maximum(m_old, m_new)
                al_ref[rows, :] = jnp.exp(m_old - m_new)
            if m_ref is not None:
                m_ref[rows, :] = m_new
            for c, t in zip(cols, tiles):
                e_ref[rows, c] = jnp.exp(t - m_new).astype(bf16)

    unit = lambda a, u: a[u * ur:(u + 1) * ur]

    def own_half(pv, u):
        heads = range(u * ur // QB, (u + 1) * ur // QB)
        return jnp.concatenate([pv[(r - heads[0]) * QB:(r - heads[0] + 1) * QB, (r % 2) * LANES:(r % 2 + 1) * LANES]
                                for r in heads], axis=0)

    s4 = _dot_nt(q4b, kca[...])

    sig = jax.nn.sigmoid(gl_ref[...])
    g_hi = sig.astype(bf16)
    g_lo = (sig - g_hi.astype(f32)).astype(bf16)
    gate_b = _dot(jnp.concatenate([g_hi, g_lo], axis=1), gsel_ref[...])

    nwin = WINDOW + QB
    w0 = pl.multiple_of(jnp.maximum(qi - WINDOW // QB, 0) * QB, QB)
    dist_w = rel_tile(nwin, qi * QB - w0)
    bias_w = jnp.where((dist_w >= 0) & (dist_w < WINDOW), 0.0, NEG)
    for u in range(NSA_UNITS):
        s_w[u][...] = _dot_nt(unit(q4b, u), kwa[pl.ds(w0, nwin), :])

    d0 = pl.multiple_of(qi * QB, QB)
    bias_d = jnp.where(rel_tile(QB, 0) >= 0, 0.0, NEG)
    for u in range(NSA_UNITS):
        s_d[u][...] = _dot_nt(unit(q4b, u), ksa[pl.ds(d0, QB), :])
    s_m[0] = _dot_nt(q4b, ksa[:SEL_CHUNK, :])

    cmp_end = (lane * CMP_STRIDE + (CMP_BLOCK - 1))
    valid_c = t_row >= cmp_end
    ps = []
    p_sum = jnp.zeros((QB, LANES), f32)
    row_bcast = lambda col: jnp.broadcast_to(col, (QB, LANES))
    for r in range(GQA_REP):
        s = jnp.where(valid_c, head(s4, r), NEG)
        e = jnp.exp(s - row_bcast(jnp.max(s, axis=1, keepdims=True)))
        p = jnp.where(valid_c, e, 0.0) / row_bcast(jnp.sum(e, axis=1, keepdims=True))
        p_sum = p_sum + p
        ps.append(p.astype(bf16))
    o_cmp4 = _dot(jnp.concatenate(ps, axis=0), vca[...])

    o_w = []
    for u in range(NSA_UNITS):
        softmax_rows(s_w[u], e_w[u], None, None, nwin, bias_w, False)
        o_w.append(own_half(_dot(e_w[u][...], vwa[pl.ds(w0, nwin), :]), u))
    for u in range(NSA_UNITS):
        softmax_rows(s_d[u], e_d[u], m_h[u], None, QB, bias_d, False)
        acc[u][...] = own_half(_dot(e_d[u][...], vsa[pl.ds(d0, QB), :]), u)

    back = t_row // SEL_BLOCK - lane
    valid_s = (back >= 0) & (lane < nsel)

    @pl.when(2 * qi + 2 <= n_top)
    def _all_valid():
        sel_scr[...] = jnp.where(valid_s, 1.0, 0.0)

    @pl.when(2 * qi + 2 > n_top)
    def _top_k():
        p_hi = p_sum.astype(bf16)
        p_lo = (p_sum - p_hi.astype(f32)).astype(bf16)
        imp_t = (_dot_nt(ovt_ref[...], p_hi) + _dot_nt(ovt_ref[...], p_lo))[:SEL_LANES]
        blk = lax.broadcasted_iota(jnp.int32, (SEL_LANES, QB), 0)
        tq = qi * QB + lax.broadcasted_iota(jnp.int32, (SEL_LANES, QB), 1)
        back_t = tq // SEL_BLOCK - blk
        valid_t = (back_t >= 0) & (blk < nsel)
        forced = (blk == 0) | (valid_t & (back_t < N_LOCAL_BLOCKS))
        score = jnp.where(valid_t, imp_t + jnp.where(forced, FORCE_BONUS, 0.0), NEG)
        score = jnp.where(blk < nsel, score, 2.0 * NEG)
        rank = jnp.zeros((SEL_LANES, QB), jnp.int32)
        for n in range(nsel):
            row = score[n:n + 1, :]
            ahead = (row > score) | ((row == score) & (blk > n))
            rank = rank + ahead.astype(jnp.int32)
        sel_t = jnp.where((rank < n_top) & valid_t, 1.0, 0.0)
        sel_t = jnp.concatenate([sel_t, jnp.zeros((LANES - SEL_LANES, QB), f32)], axis=0)
        sel_scr[...] = sel_t.T

    per = SEL_CHUNK // QB
    n_chunks = (qi + per - 1) // per
    acc_w = jnp.concatenate(o_w, axis=0)

    def finish(n):
        if n > 0:
            sel_bias = jnp.where((sel_scr[...] > 0.5) & (lane < 2 * qi), 0.0, MASK_BIG)
            sel_bias = pltpu.roll(sel_bias, SEL_LANE0, axis=1)
            in_sel = (lane >= SEL_LANE0) & (lane < SEL_LANE0 + SEL_LANES)
            for r in range(GQA_REP):
                q_scr[r * QB:(r + 1) * QB, :] = jnp.where(in_sel, sel_bias, head(q4, r)).astype(bf16)
            bias0 = _dot_nt(jnp.where(in_sel, sel_bias, 0.0).astype(bf16), ksa[:SEL_CHUNK, :])
        for kc in range(n):
            half = kc % 2
            if kc + 1 < n:
                s_m[1 - half] = _dot_nt(q_scr[...], ksa[(kc + 1) * SEL_CHUNK:(kc + 2) * SEL_CHUNK, :])
            for u in range(NSA_UNITS):
                softmax_rows(s_m.at[half, pl.ds(u * ur, ur)], e_m[u], m_h[u], al_h[u], SEL_CHUNK,
                             bias0 if kc == 0 else None, True)
                acc[u][...] = (al_h[u][...] * acc[u][...]
                               + own_half(_dot(e_m[u][...], vsa[kc * SEL_CHUNK:(kc + 1) * SEL_CHUNK, :]), u))

        acc_s = jnp.concatenate([a[...] for a in acc], axis=0)
        for pr in range(GQA_REP // 2):
            ev, od = 2 * pr, 2 * pr + 1
            out = (gate_b[:, (3 * pr) * LANES:(3 * pr + 1) * LANES]
                   * jnp.where(left, head(o_cmp4, ev), head(o_cmp4, od)))
            for j, a in ((1, acc_s), (2, acc_w)):
                num = jnp.where(left, head(a, ev), head(a, od))
                den = pltpu.roll(jnp.where(left, head(a, od), head(a, ev)), HEAD_DIM, axis=1)
                out = out + gate_b[:, (3 * pr + j) * LANES:(3 * pr + j + 1) * LANES] / den * num
            o_ref[:, pr * LANES:(pr + 1) * LANES] = out.astype(o_ref.dtype)

    for n in range((seq // QB - 1 + per - 1) // per + 1):
        pl.when(n_chunks == n)(functools.partial(finish, n))


def _nsa(proj3, kcmp, vcmp, slopes_b, ovt, kaug, caug, gsel):
    B, S, _ = proj3.shape
    ncb = kcmp.shape[1]
    nrow = GQA_REP * QB
    ur = nrow // NSA_UNITS
    nwin = WINDOW + QB
    kv = lambda j: pl.BlockSpec((None, S, LANES), lambda b, g, qi, j=j: (b, 0, COL_KC + j))
    full = lambda a: pl.BlockSpec(a.shape, lambda b, g, qi: (0,) * a.ndim)
    return pl.pallas_call(
        functools.partial(_nsa_kernel, seq=S),
        grid=(B, N_KV_B, S // QB),
        in_specs=[pl.BlockSpec(memory_space=pltpu.SMEM),
                  pl.BlockSpec((None, QB, 2 * LANES), lambda b, g, qi: (b, qi, COL_QB + g)),
                  kv(2), kv(3), kv(4), kv(5),
                  pl.BlockSpec((None, QB, LANES), lambda b, g, qi: (b, qi, COL_GATE + g)),
                  pl.BlockSpec((None, ncb, LANES), lambda b, g, qi: (b, 0, 0)),
                  pl.BlockSpec((None, ncb, LANES), lambda b, g, qi: (b, 0, 0)),
                  full(ovt), full(kaug), full(caug), full(gsel)],
        out_specs=pl.BlockSpec((None, QB, 2 * LANES), lambda b, g, qi: (b, qi, g)),
        out_shape=jax.ShapeDtypeStruct((B, S, N_HEADS_B * HEAD_DIM), bf16),
        scratch_shapes=([pltpu.VMEM((S, LANES), bf16), pltpu.VMEM((S, 2 * LANES), bf16)] * 2
                        + [pltpu.VMEM((ncb, LANES), bf16)] * 2
                        + [pltpu.VMEM((nrow, LANES), bf16), pltpu.VMEM((QB, LANES), f32),
                           pltpu.VMEM((2, nrow, SEL_CHUNK), f32)]
                        + [pltpu.VMEM((ur, nwin), f32)] * NSA_UNITS + [pltpu.VMEM((ur, nwin), bf16)] * NSA_UNITS
                        + [pltpu.VMEM((ur, QB), f32)] * NSA_UNITS + [pltpu.VMEM((ur, QB), bf16)] * NSA_UNITS
                        + [pltpu.VMEM((ur, SEL_CHUNK), bf16)] * NSA_UNITS
                        + [pltpu.VMEM((ur, LANES), f32)] * (2 * NSA_UNITS)
                        + [pltpu.VMEM((ur, LANES), f32)] * NSA_UNITS),
        compiler_params=_params("parallel", "parallel", "arbitrary"),
        name="nsa_attn",
    )(slopes_b, proj3, proj3, proj3, proj3, proj3, proj3, kcmp, vcmp, ovt, kaug, caug, gsel)


def _route(h, w_ref, b_ref):
    w = w_ref[...]
    h_hi, w_hi = h.astype(bf16), w.astype(bf16)
    h_lo, w_lo = (h - h_hi.astype(f32)).astype(bf16), (w - w_hi.astype(f32)).astype(bf16)
    logit = _dot(h_hi, w_hi) + (_dot(h_hi, w_lo) + _dot(h_lo, w_hi)) + b_ref[...]
    tm = logit.shape[0]
    lane = lax.broadcasted_iota(jnp.int32, (tm, LANES), 1)
    big = jnp.int32(LANES)
    is_g = lane < N_GROUPS
    gl = jnp.where(is_g, logit, NEG)
    gmax = jnp.max(gl, axis=1, keepdims=True)
    gsum = jnp.sum(jnp.where(is_g, jnp.exp(gl - gmax), 0.0), axis=1, keepdims=True)
    gsel = jnp.min(jnp.where(is_g & (gl == gmax), lane, big), axis=1, keepdims=True)
    gw = 1.0 / gsum
    e_lane = lane - N_GROUPS
    in_grp = (e_lane >= 0) & (e_lane < N_EXPERTS) & (e_lane // EXPERTS_PER_GROUP == gsel)
    el = jnp.where(in_grp, logit, NEG)
    t1 = jnp.max(el, axis=1, keepdims=True)
    i1 = jnp.min(jnp.where(in_grp & (el == t1), lane, big), axis=1, keepdims=True)
    el2 = jnp.where(lane == i1, NEG, el)
    t2 = jnp.max(el2, axis=1, keepdims=True)
    i2 = jnp.min(jnp.where(in_grp & (lane != i1) & (el2 == t2), lane, big), axis=1, keepdims=True)
    e2 = jnp.exp(t2 - t1)
    w1 = gw / (1.0 + e2)
    w2 = gw * e2 / (1.0 + e2)
    return jnp.where(lane == i1, w1, jnp.where(lane == i2, w2, 0.0)), gsel


def _to_token_tiles(ref, x):
    for s in range(D_MODEL // LANES):
        ref[pl.ds(s, x.shape[0], stride=D_MODEL // LANES), :] = x[:, s * LANES:(s + 1) * LANES]


def _from_token_tiles(ref):
    n = D_MODEL // LANES
    return jnp.concatenate([ref[pl.ds(s, ref.shape[0] // n, stride=n), :] for s in range(n)], axis=1)


def _out_route_kernel(x_ref, oa_ref, ob_ref, wa_ref, wb_ref, g_ref, w_ref, b_ref, tri_ref,
                      x1_ref, stage_ref, meta_ref, cnt_ref, cnt_scr):
    @pl.when(pl.program_id(0) == 0)
    def _zero():
        cnt_scr[...] = jnp.zeros_like(cnt_scr)

    x1 = x_ref[...] + _dot(oa_ref[...], wa_ref[...]) + _dot(ob_ref[...], wb_ref[...])
    x1_ref[...] = x1
    h = _rms(x1, g_ref[...])
    _to_token_tiles(stage_ref, h)
    _, gsel = _route(h, w_ref, b_ref)
    tm = h.shape[0]
    lane = lax.broadcasted_iota(jnp.int32, (tm, LANES), 1)
    is_g = lane < N_GROUPS

    onehot = jnp.where(is_g & (lane == gsel), 1.0, 0.0)
    before = _dot(tri_ref[...], onehot.astype(bf16)) + cnt_scr[...]
    rank = jnp.sum(onehot * before, axis=1, keepdims=True)
    cnt_scr[...] = before[tm - 1:tm, :] + onehot[tm - 1:tm, :]
    cnt_ref[...] = cnt_scr[...]
    rank_hi = jnp.floor(rank * (1.0 / RANK_SPLIT))
    cols = jnp.where(lane == 0, gsel.astype(f32), jnp.where(lane == 1, rank_hi,
                                                            jnp.where(lane == 2, rank - RANK_SPLIT * rank_hi, 0.0)))
    pick = (lax.broadcasted_iota(jnp.int32, (8, LANES), 0) == lax.broadcasted_iota(jnp.int32, (8, LANES), 1))
    meta_ref[...] = _dot_nt(jnp.where(pick, 1.0, 0.0).astype(bf16), cols.astype(bf16))


def _out_route(x2d, oa, ob, wa, wb, g, w, b):
    T = x2d.shape[0]
    tm = MOE_TILE
    assert T // RANK_SPLIT <= 256
    tri = jnp.asarray(np.tril(np.ones((tm, tm), np.float32), -1), bf16)
    row = lambda n: pl.BlockSpec((tm, n), lambda i: (i, 0))
    full = lambda a: pl.BlockSpec(a.shape, lambda i: (0,) * a.ndim)
    return pl.pallas_call(
        _out_route_kernel,
        grid=(T // tm,),
        in_specs=[row(D_MODEL), row(oa.shape[1]), row(ob.shape[1]), full(wa), full(wb),
                  full(g), full(w), full(b), full(tri)],
        out_specs=[row(D_MODEL),
                   pl.BlockSpec((tm * TOKEN_ROWS, LANES), lambda i: (i, 0)),
                   pl.BlockSpec((None, 8, tm), lambda i: (i, 0, 0)),
                   pl.BlockSpec((1, LANES), lambda i: (0, 0))],
        out_shape=[jax.ShapeDtypeStruct((T, D_MODEL), f32),
                   jax.ShapeDtypeStruct((T * TOKEN_ROWS, LANES), f32),
                   jax.ShapeDtypeStruct((T // tm, 8, tm), f32),
                   jax.ShapeDtypeStruct((1, LANES), f32)],
        scratch_shapes=[pltpu.VMEM((1, LANES), f32)],
        compiler_params=_params("arbitrary"),
        name="out_proj_router",
    )(x2d, oa, ob, wa, wb, g, w, b, tri)


def _token_rows(t, n=1):
    return pl.ds(pl.multiple_of(t * TOKEN_ROWS, TOKEN_ROWS), n * TOKEN_ROWS)


def _gathered_tile(idx_ref, src_ref, buf_ref, sems, tm):
    i = pl.program_id(0)
    n = pl.num_programs(0)

    def issue(step):
        slot = step % 2

        def one(k, _):
            pltpu.make_async_copy(src_ref.at[_token_rows(idx_ref[step * tm + k])],
                                  buf_ref.at[slot, _token_rows(k)], sems.at[slot]).start()
            return 0

        lax.fori_loop(0, tm, one, 0, unroll=8)

    @pl.when(i == 0)
    def _first():
        issue(i)

    @pl.when(i + 1 < n)
    def _next():
        issue(i + 1)

    slot = i % 2
    pltpu.make_async_copy(src_ref.at[_token_rows(0, tm)], buf_ref.at[slot], sems.at[slot]).wait()
    return slot


def _moe_kernel(tg_ref, src_ref, stage_ref, wr_ref, br_ref, wg_ref, wu_ref, wd_ref, o_ref, buf_ref, sems):
    grp = tg_ref[pl.program_id(0)]
    slot = _gathered_tile(src_ref, stage_ref, buf_ref, sems, MOE_TILE)

    @pl.when(grp < N_GROUPS)
    def _experts():
        h = _from_token_tiles(buf_ref.at[slot])
        gates, _ = _route(h, wr_ref, br_ref)
        x = h.astype(bf16)
        lane = lax.broadcasted_iota(jnp.int32, gates.shape, 1)
        y = jnp.zeros(h.shape, f32)
        for e in range(EXPERTS_PER_GROUP):
            gcol = jnp.sum(jnp.where(lane == N_GROUPS + EXPERTS_PER_GROUP * grp + e, gates, 0.0),
                           axis=1, keepdims=True)
            a = jax.nn.silu(_dot(x, wg_ref[e])) * _dot(x, wu_ref[e])
            y = y + _dot((a * gcol).astype(bf16), wd_ref[e])
        _to_token_tiles(o_ref, y)

    @pl.when(grp >= N_GROUPS)
    def _unused():
        o_ref[...] = jnp.zeros_like(o_ref)


def _moe(stage, tile_group, slot_token, w_route, b_route, wg, wu, wd, layer):
    tm = MOE_TILE
    n_slots = slot_token.shape[0]
    w_spec = lambda k, n: pl.BlockSpec((None, None, EXPERTS_PER_GROUP, k, n),
                                       lambda j, tg, src: (layer, jnp.minimum(tg[j], N_GROUPS - 1), 0, 0, 0))
    grouped = lambda w: w.reshape(w.shape[0], N_GROUPS, EXPERTS_PER_GROUP, *w.shape[2:])
    return pl.pallas_call(
        _moe_kernel,
        grid_spec=pltpu.PrefetchScalarGridSpec(
            num_scalar_prefetch=2, grid=(n_slots // tm,),
            in_specs=[pl.BlockSpec(memory_space=pl.ANY),
                      pl.BlockSpec((D_MODEL, LANES), lambda j, tg, src: (0, 0)),
                      pl.BlockSpec((1, LANES), lambda j, tg, src: (0, 0)),
                      w_spec(D_MODEL, D_EXPERT), w_spec(D_MODEL, D_EXPERT), w_spec(D_EXPERT, D_MODEL)],
            out_specs=pl.BlockSpec((tm * TOKEN_ROWS, LANES), lambda j, tg, src: (j, 0)),
            scratch_shapes=[pltpu.VMEM((2, tm * TOKEN_ROWS, LANES), f32), pltpu.SemaphoreType.DMA((2,))]),
        out_shape=jax.ShapeDtypeStruct((n_slots * TOKEN_ROWS, LANES), f32),
        compiler_params=_params("arbitrary"),
        name="moe_ffn",
    )(tile_group, slot_token, stage, w_route, b_route, grouped(wg), grouped(wu), grouped(wd))


def _ple_kernel(pos_ref, x_ref, ys_ref, p_ref, g_ref, wg_ref, wp_ref, fg_ref, o_ref, buf_ref, sems, *, final):
    slot = _gathered_tile(pos_ref, ys_ref, buf_ref, sems, x_ref.shape[0])
    x = x_ref[...] + _from_token_tiles(buf_ref.at[slot])
    gate = jax.nn.sigmoid(_dot(_rms(x, g_ref[...]).astype(bf16), wg_ref[...]))
    y = x + gate * _dot(p_ref[...].astype(bf16), wp_ref[...])
    o_ref[...] = _rms(y, fg_ref[...]) if final else y


def _ple(x2d, ys, pos, p3, g, wg, wp, fg, layer, final):
    T = x2d.shape[0]
    tm = 512
    full = lambda a: pl.BlockSpec(a.shape, lambda i, pos: (0,) * a.ndim)
    return pl.pallas_call(
        functools.partial(_ple_kernel, final=final),
        grid_spec=pltpu.PrefetchScalarGridSpec(
            num_scalar_prefetch=1, grid=(T // tm,),
            in_specs=[pl.BlockSpec((tm, D_MODEL), lambda i, pos: (i, 0)),
                      pl.BlockSpec(memory_space=pl.ANY),
                      pl.BlockSpec((None, tm, PLE_DIM), lambda i, pos: (layer, i, 0)),
                      full(g), full(wg), full(wp), full(fg)],
            out_specs=pl.BlockSpec((tm, D_MODEL), lambda i, pos: (i, 0)),
            scratch_shapes=[pltpu.VMEM((2, tm * TOKEN_ROWS, LANES), f32), pltpu.SemaphoreType.DMA((2,))]),
        out_shape=jax.ShapeDtypeStruct((T, D_MODEL), f32),
        compiler_params=_params("arbitrary"),
        name="ple",
    )(pos, x2d, ys, p3, g, wg, wp, fg)


def _alibi_slopes():
    s = 2.0 ** (-8.0 * np.arange(1, N_HEADS_TOTAL + 1) / N_HEADS_TOTAL)
    assert np.all(np.log2(s[1::2]) == np.round(np.log2(s[1::2])))
    return jnp.asarray(s[0::2], f32), jnp.asarray(s[1::2], f32)


def _selection_constants(seq):
    ncb = seq // CMP_STRIDE
    nsel = seq // SEL_BLOCK
    n_cmp = (seq - CMP_BLOCK) // CMP_STRIDE + 1
    cs = np.arange(n_cmp) * CMP_STRIDE
    bs = np.arange(nsel) * SEL_BLOCK
    ov = np.clip(np.minimum(cs[:, None] + CMP_BLOCK, bs[None, :] + SEL_BLOCK)
                 - np.maximum(cs[:, None], bs[None, :]), 0, None) / CMP_BLOCK
    assert ncb == LANES and seq <= POS_SPLIT * 256
    ovt = np.zeros((LANES, ncb), np.float32)
    ovt[:nsel, :n_cmp] = ov.T
    pos = np.arange(seq)
    kaug = np.zeros((seq, LANES), np.float32)
    kaug[:, POS_LANE0:POS_LANE0 + 2] = 1.0
    kaug[:, POS_LANE0 + 2] = pos // POS_SPLIT
    kaug[:, POS_LANE0 + 3] = pos % POS_SPLIT
    kaug[pos, SEL_LANE0 + pos // SEL_BLOCK] = 1.0
    cend = np.arange(ncb) * CMP_STRIDE + CMP_BLOCK - 1
    caug = np.zeros((ncb, LANES), np.float32)
    caug[:, POS_LANE0:POS_LANE0 + 2] = 1.0
    caug[:, POS_LANE0 + 2] = cend // POS_SPLIT
    caug[:, POS_LANE0 + 3] = cend % POS_SPLIT
    gsel = np.zeros((2, LANES, GQA_REP // 2, 3, 2, HEAD_DIM), np.float32)
    for pr in range(GQA_REP // 2):
        for j in range(3):
            for hh in range(2):
                gsel[:, 3 * (2 * pr + hh) + j, pr, j, hh, :] = 1.0
    gsel = gsel.reshape(2 * LANES, (GQA_REP // 2) * 3 * LANES)
    return jnp.asarray(ovt, bf16), jnp.asarray(kaug), jnp.asarray(caug), jnp.asarray(gsel, bf16)


def _dispatch_plan(meta, counts, n_tokens):
    i32 = jnp.int32
    n_slots = n_tokens + N_GROUPS * MOE_TILE
    group = meta[:, 0, :].reshape(n_tokens).astype(i32)
    rank = (meta[:, 1, :] * RANK_SPLIT + meta[:, 2, :]).reshape(n_tokens).astype(i32)
    cnt = counts[0, :N_GROUPS].astype(i32)
    padded = (cnt + MOE_TILE - 1) // MOE_TILE * MOE_TILE
    end = jnp.cumsum(padded)
    pos = (end - padded)[group] + rank
    slot_token = jnp.zeros((n_slots,), i32).at[pos].set(jnp.arange(n_tokens, dtype=i32))
    tile_start = jnp.arange(n_slots // MOE_TILE, dtype=i32) * MOE_TILE
    tile_group = jnp.sum(tile_start[:, None] >= end[None, :], axis=1).astype(i32)
    return pos, slot_token, tile_group


def _block_diag2(w):
    z = jnp.zeros_like(w)
    return jnp.concatenate([jnp.concatenate([w, z], axis=-1), jnp.concatenate([z, w], axis=-1)], axis=-2)


def _layout_w_in(w):
    gate = w[:, N_MAIN:]
    per = GQA_REP * 3
    blocks = [jnp.pad(gate[:, g * per:(g + 1) * per], ((0, 0), (0, LANES - per))) for g in range(N_KV_B)]
    return jnp.concatenate([w[:, :N_MAIN]] + blocks, axis=1).astype(bf16)


def kernel(x, p, attn_norm, w_in, w_out, w_cmp_k1, w_cmp_k2, w_cmp_v1, w_cmp_v2, cmp_pos, ffn_norm, w_route_group, b_route_group, w_route_expert, b_route_expert, w_expert_gate, w_expert_up, w_expert_down, ple_norm, w_ple_gate, w_ple_proj, final_norm):
    B, S, D = x.shape
    depth = w_in.shape[0]
    T = B * S
    slopes_a, slopes_b = _alibi_slopes()
    ovt, kaug, caug, gsel = _selection_constants(S)
    wg_all = w_expert_gate.astype(bf16)
    wu_all = w_expert_up.astype(bf16)
    wd_all = w_expert_down.astype(bf16)
    p3 = p.reshape(depth, T, PLE_DIM)
    row = lambda v: v.reshape(1, -1)
    n_route = N_GROUPS + N_EXPERTS

    x2d = x.reshape(T, D)
    for i in range(depth):
        proj3 = _in_proj(x2d, row(attn_norm[i]), _layout_w_in(w_in[i])).reshape(B, S, N_PROJ)
        oa = _dilated(proj3, slopes_a)
        w1 = lambda w: _block_diag2(w.reshape(CMP_BLOCK, HEAD_DIM, CMP_HIDDEN)).astype(bf16)
        pos_dup = jnp.concatenate([cmp_pos[i], cmp_pos[i]], axis=-1)
        kcmp, vcmp = _compress(proj3, pos_dup, w1(w_cmp_k1[i]), _block_diag2(w_cmp_k2[i]).astype(bf16),
                               w1(w_cmp_v1[i]), _block_diag2(w_cmp_v2[i]).astype(bf16))
        ob = _nsa(proj3, kcmp, vcmp, slopes_b, ovt, kaug, caug, gsel)
        wo = w_out[i].astype(bf16)
        w_route = jnp.pad(jnp.concatenate([w_route_group[i], w_route_expert[i]], axis=1),
                          ((0, 0), (0, LANES - n_route)))
        b_route = jnp.pad(jnp.concatenate([b_route_group[i], b_route_expert[i]]), (0, LANES - n_route))
        x2d, stage, meta, counts = _out_route(x2d, oa.reshape(T, A_W), ob.reshape(T, -1), wo[:A_W], wo[A_W:],
                                              row(ffn_norm[i]), w_route, row(b_route))
        pos, slot_token, tile_group = _dispatch_plan(meta, counts, T)
        ys = _moe(stage, tile_group, slot_token, w_route, row(b_route), wg_all, wu_all, wd_all, i)
        x2d = _ple(x2d, ys, pos, p3, row(ple_norm[i]), w_ple_gate[i].astype(bf16),
                   w_ple_proj[i].astype(bf16), row(final_norm), i, i == depth - 1)
    return x2d.reshape(B, S, D)
```

```python
import functools

import numpy as np
import jax
import jax.numpy as jnp
from jax import lax
from jax.experimental import pallas as pl
from jax.experimental.pallas import tpu as pltpu

D_MODEL = 1024
PLE_DIM = 256
HEAD_DIM = 64
N_HEADS_A = 8
N_HEADS_B = 8
N_KV_B = 2
GQA_REP = N_HEADS_B // N_KV_B
N_HEADS_TOTAL = N_HEADS_A + N_HEADS_B
DILATED_PATTERNS = ((128, 1), (512, 4), (2048, 16))
CMP_BLOCK = 32
CMP_STRIDE = 16
CMP_HIDDEN = 256
SEL_BLOCK = 64
SEL_TOP = 16
N_LOCAL_BLOCKS = 2
WINDOW = 512
N_GROUPS = 4
EXPERTS_PER_GROUP = 4
N_EXPERTS = N_GROUPS * EXPERTS_PER_GROUP
D_EXPERT = 512
RMS_EPS = 1e-6
NEG = -1e30
FORCE_BONUS = 1e4
SCALE = HEAD_DIM ** -0.5

LANES = 128
QB = 128
SEL_CHUNK = 512
NSA_UNITS = 2
DIL_INFLIGHT = 3
DIL_TRIP = 16
POS_LANE0 = HEAD_DIM
POS_SPLIT = 16
SEL_LANE0 = POS_LANE0 + 4
SEL_LANES = 32
MASK_BIG = -(2.0 ** 100)
MOE_TILE = 512
TOKEN_ROWS = D_MODEL // LANES
RANK_SPLIT = 128
A_W = N_HEADS_A * HEAD_DIM
N_MAIN = 3 * A_W + N_HEADS_B * HEAD_DIM + 6 * N_KV_B * HEAD_DIM
N_PROJ = N_MAIN + N_KV_B * LANES
COL_QB = (3 * A_W) // (2 * LANES)
COL_KC = (3 * A_W + N_HEADS_B * HEAD_DIM) // LANES
COL_GATE = N_MAIN // LANES
VMEM_LIMIT = 56 * 1024 * 1024

f32 = jnp.float32
bf16 = jnp.bfloat16


def _dot(a, b):
    return jnp.dot(a, b, preferred_element_type=f32)


def _dot_nt(a, b):
    return lax.dot_general(a, b, (((1,), (1,)), ((), ())), preferred_element_type=f32)


def _rms(x, g):
    return x * lax.rsqrt(jnp.mean(x * x, axis=-1, keepdims=True) + RMS_EPS) * g


def _params(*sem):
    return pltpu.CompilerParams(dimension_semantics=sem, vmem_limit_bytes=VMEM_LIMIT)


def _in_proj_kernel(x_ref, g_ref, w_ref, o_ref):
    h = _rms(x_ref[...], g_ref[...]).astype(bf16)
    for n0 in range(0, N_PROJ, 512):
        o_ref[:, n0:n0 + 512] = _dot(h, w_ref[:, n0:n0 + 512])


def _in_proj(x2d, g, w):
    T = x2d.shape[0]
    tm = 512
    return pl.pallas_call(
        _in_proj_kernel,
        grid=(T // tm,),
        in_specs=[pl.BlockSpec((tm, D_MODEL), lambda i: (i, 0)),
                  pl.BlockSpec((1, D_MODEL), lambda i: (0, 0)),
                  pl.BlockSpec((D_MODEL, N_PROJ), lambda i: (0, 0))],
        out_specs=pl.BlockSpec((tm, N_PROJ), lambda i: (i, 0)),
        out_shape=jax.ShapeDtypeStruct((T, N_PROJ), f32),
        compiler_params=_params("parallel"),
        name="in_proj",
    )(x2d, g, w)


def _dil_kernel(slope_ref, q_ref, k_ref, v_ref, o_ref, out_ref, lse_ref, bias_scr, *bufs, seq):
    hp = pl.program_id(1)
    lane = lax.broadcasted_iota(jnp.int32, (QB, LANES), 1)
    left = lane < HEAD_DIM
    slab = 32
    s_bufs, e_bufs = bufs[:DIL_INFLIGHT], bufs[DIL_INFLIGHT:]

    i = lax.broadcasted_iota(jnp.int32, (QB, 2 * QB), 0)
    j = lax.broadcasted_iota(jnp.int32, (QB, 2 * QB), 1)
    rel = i - j + QB
    valid = (rel >= 0) & (rel <= QB)
    relf = rel.astype(f32)
    for p, (window, dil) in enumerate(DILATED_PATTERNS):
        assert window // dil == QB
        for hh in range(2):
            bias_scr[2 * p + hh] = jnp.where(valid, (-float(dil) * slope_ref[2 * hp + hh]) * relf, NEG)

    ones = jnp.ones((2 * QB, LANES), bf16)

    def scores(u, blk):
        p, dil, row0, key0, nk = blk
        qc = q_ref[pl.ds(row0, QB, stride=dil), :] * SCALE
        q2 = jnp.concatenate([jnp.where(left, qc, 0.0), jnp.where(left, 0.0, qc)], axis=0).astype(bf16)
        s_bufs[u][:, :nk] = _dot_nt(q2, k_ref[pl.ds(key0, nk, stride=dil), :].astype(bf16))

    def probs(u, blk):
        p, dil, row0, key0, nk = blk
        c0 = 2 * QB - nk
        ms = []
        for hh in range(2):
            parts = []
            for sl in range(QB // slab):
                rows = slice(hh * QB + sl * slab, hh * QB + (sl + 1) * slab)
                tiles = [s_bufs[u][rows, c:c + LANES]
                         + bias_scr[2 * p + hh, sl * slab:(sl + 1) * slab, c0 + c:c0 + c + LANES]
                         for c in range(0, nk, LANES)]
                mx = tiles[0]
                for t in tiles[1:]:
                    mx = jnp.maximum(mx, t)
                mx = jnp.broadcast_to(jnp.max(mx, axis=1, keepdims=True), (slab, LANES))
                for c, t in zip(range(0, nk, LANES), tiles):
                    e_bufs[u][rows, c:c + LANES] = jnp.exp(t - mx).astype(bf16)
                parts.append(mx)
            ms.append(jnp.concatenate(parts, axis=0))
        return jnp.where(left, ms[0], ms[1])

    def values(u, blk, row_max):
        p, dil, row0, key0, nk = blk
        v2 = jnp.concatenate([v_ref[pl.ds(key0, nk, stride=dil), :].astype(bf16), ones[:nk]], axis=1)
        res = _dot(e_bufs[u][:, :nk], v2)
        rows = pl.ds(row0, QB, stride=dil)
        den = jnp.where(left, res[:QB, LANES:], res[QB:, LANES:])
        out_ref[p, rows, :] = jnp.where(left, res[:QB, :LANES], res[QB:, :LANES]) / den
        lse_ref[p, rows, :] = row_max + jnp.log(den)

    def run(blocks):
        ahead = DIL_INFLIGHT - 1
        for n in range(min(ahead, len(blocks))):
            scores(n % DIL_INFLIGHT, blocks[n])
        for n, blk in enumerate(blocks):
            if n + ahead < len(blocks):
                scores((n + ahead) % DIL_INFLIGHT, blocks[n + ahead])
            values(n % DIL_INFLIGHT, blk, probs(n % DIL_INFLIGHT, blk))

    def first(p, dil, r):
        return (p, dil, r, r, QB)

    def later(p, dil, r, a):
        return (p, dil, r + dil * QB * a, r + dil * QB * (a - 1), 2 * QB)

    for p, (window, dil) in enumerate(DILATED_PATTERNS):
        nblk = seq // dil // QB
        blocks = [first(p, dil, r) if a == 0 else later(p, dil, r, a) for r in range(dil) for a in range(nblk)]
        for b0 in range(0, len(blocks), DIL_TRIP):
            run(blocks[b0:b0 + DIL_TRIP])

    ch = 256

    def combine(c, _):
        rows = pl.ds(pl.multiple_of(c * ch, ch), ch)
        lses = [lse_ref[p, rows, :] for p in range(len(DILATED_PATTERNS))]
        big = functools.reduce(jnp.maximum, lses)
        num = jnp.zeros((ch, LANES), f32)
        den = jnp.zeros((ch, LANES), f32)
        for p, lse in enumerate(lses):
            w = jnp.exp(lse - big)
            num = num + w * out_ref[p, rows, :]
            den = den + w
        o_ref[rows, :] = (num / den).astype(o_ref.dtype)
        return 0

    lax.fori_loop(0, seq // ch, combine, 0)


def _dilated(proj3, slopes_a):
    B, S, _ = proj3.shape
    npair = N_HEADS_A // 2
    blk = lambda off: pl.BlockSpec((None, S, LANES), lambda b, hp, off=off: (b, 0, off + hp))
    return pl.pallas_call(
        functools.partial(_dil_kernel, seq=S),
        grid=(B, npair),
        in_specs=[pl.BlockSpec(memory_space=pltpu.SMEM), blk(0), blk(npair), blk(2 * npair)],
        out_specs=pl.BlockSpec((None, S, LANES), lambda b, hp: (b, 0, hp)),
        out_shape=jax.ShapeDtypeStruct((B, S, A_W), bf16),
        scratch_shapes=([pltpu.VMEM((len(DILATED_PATTERNS), S, LANES), f32)] * 2
                        + [pltpu.VMEM((2 * len(DILATED_PATTERNS), QB, 2 * QB), f32)]
                        + [pltpu.VMEM((2 * QB, 2 * QB), f32)] * DIL_INFLIGHT
                        + [pltpu.VMEM((2 * QB, 2 * QB), bf16)] * DIL_INFLIGHT),
        compiler_params=_params("parallel", "parallel"),
        name="dilated_attn",
    )(slopes_a, proj3, proj3, proj3)


def _cmp_kernel(kc_ref, vc_ref, pos_ref, w1k_ref, w2k_ref, w1v_ref, w2v_ref, ko_ref, vo_ref, *, ncb):
    half = CMP_BLOCK // 2
    for x_ref, pi, w1_ref, w2_ref, o_ref in ((kc_ref, 0, w1k_ref, w2k_ref, ko_ref),
                                             (vc_ref, 1, w1v_ref, w2v_ref, vo_ref)):
        lo = jnp.zeros((ncb, 2 * CMP_HIDDEN), f32)
        hi = jnp.zeros((ncb, 2 * CMP_HIDDEN), f32)
        for r in range(half):
            x = x_ref[pl.ds(r, ncb, stride=CMP_STRIDE), :]
            lo = lo + _dot((x + pos_ref[pi, r:r + 1, :]).astype(bf16), w1_ref[r])
            hi = hi + _dot((x + pos_ref[pi, r + half:r + half + 1, :]).astype(bf16), w1_ref[r + half])
        h1 = lo + pltpu.roll(hi, ncb - 1, axis=0)
        o_ref[...] = _dot(jax.nn.gelu(h1).astype(bf16), w2_ref[...])


def _compress(proj3, pos_dup, w1k, w2k, w1v, w2v):
    B, S, _ = proj3.shape
    ncb = S // CMP_STRIDE
    full = lambda a: pl.BlockSpec(a.shape, lambda b: (0,) * a.ndim)
    out = jax.ShapeDtypeStruct((B, ncb, LANES), f32)
    return pl.pallas_call(
        functools.partial(_cmp_kernel, ncb=ncb),
        grid=(B,),
        in_specs=[pl.BlockSpec((None, S, LANES), lambda b: (b, 0, COL_KC)),
                  pl.BlockSpec((None, S, LANES), lambda b: (b, 0, COL_KC + 1)),
                  full(pos_dup), full(w1k), full(w2k), full(w1v), full(w2v)],
        out_specs=[pl.BlockSpec((None, ncb, LANES), lambda b: (b, 0, 0))] * 2,
        out_shape=[out, out],
        compiler_params=_params("parallel"),
        name="nsa_compress",
    )(proj3, proj3, pos_dup, w1k, w2k, w1v, w2v)


def _nsa_kernel(slope_ref, q_ref, ks_ref, vs_ref, kw_ref, vw_ref, gl_ref, kcmp_ref, vcmp_ref,
                ovt_ref, kaug_ref, caug_ref, gsel_ref, o_ref,
                ksa, vsa, kwa, vwa, kca, vca, q_scr, s_m, *per_unit, seq):
    g = pl.program_id(1)
    qi = pl.program_id(2)
    nsel = seq // SEL_BLOCK
    n_top = min(SEL_TOP, nsel)
    assert nsel <= SEL_LANES and N_LOCAL_BLOCKS * SEL_BLOCK >= QB and n_top > N_LOCAL_BLOCKS

    def group_lanes(x):
        return jnp.where(g == 0, x, pltpu.roll(x, HEAD_DIM, axis=1))

    @pl.when(qi == 0)
    def _prep():
        ch = 256
        lane_c = lax.broadcasted_iota(jnp.int32, (ch, LANES), 1)
        is_k = lane_c < HEAD_DIM
        for c in range(seq // ch):
            sl = slice(c * ch, (c + 1) * ch)
            aug = kaug_ref[sl, :]
            ksa[sl, :] = jnp.where(is_k, group_lanes(ks_ref[sl, :]), aug).astype(bf16)
            kwa[sl, :] = jnp.where(is_k, group_lanes(kw_ref[sl, :]),
                                   jnp.where(lane_c < SEL_LANE0, aug, 0.0)).astype(bf16)
            for src, dst in ((vs_ref, vsa), (vw_ref, vwa)):
                v = group_lanes(src[sl, :])
                dst[sl, :LANES] = jnp.where(is_k, v, 1.0).astype(bf16)
                dst[sl, LANES:] = jnp.where(is_k, 1.0, pltpu.roll(v, HEAD_DIM, axis=1)).astype(bf16)
        lane_k = lax.broadcasted_iota(jnp.int32, kca.shape, 1)
        kca[...] = jnp.where(lane_k < HEAD_DIM, group_lanes(kcmp_ref[...]), caug_ref[...]).astype(bf16)
        vc = group_lanes(vcmp_ref[...])
        vca[...] = jnp.where(lane_k < HEAD_DIM, vc, pltpu.roll(vc, HEAD_DIM, axis=1)).astype(bf16)

    per = SEL_CHUNK // QB
    n_chunks = (qi + per - 1) // per
    need_topk = 2 * qi + 2 > n_top
    refs = (slope_ref, q_ref, gl_ref, ovt_ref, gsel_ref, o_ref, ksa, vsa, kwa, vwa, kca, vca, q_scr, s_m, per_unit)
    for n in range((seq // QB - 1 + per - 1) // per + 1):
        qi_lo, qi_hi = max(per * (n - 1) + 1, 0), min(per * n, seq // QB - 1)
        for topk in sorted({2 * q + 2 > n_top for q in range(qi_lo, qi_hi + 1)}):
            pl.when((n_chunks == n) & (need_topk == topk))(
                functools.partial(_nsa_step, refs, seq=seq, n=n, topk=topk))


def _nsa_step(refs, *, seq, n, topk):
    slope_ref, q_ref, gl_ref, ovt_ref, gsel_ref, o_ref, ksa, vsa, kwa, vwa, kca, vca, q_scr, s_m, per_unit = refs
    g = pl.program_id(1)
    qi = pl.program_id(2)
    nsel = seq // SEL_BLOCK
    n_top = min(SEL_TOP, nsel)
    slab = 32
    ur = GQA_REP * QB // NSA_UNITS
    s_w, e_w, s_d, e_d, e_m, m_h, al_h, acc = [per_unit[i * NSA_UNITS:(i + 1) * NSA_UNITS] for i in range(8)]
    lane = lax.broadcasted_iota(jnp.int32, (QB, LANES), 1)
    left = lane < HEAD_DIM
    ii = lax.broadcasted_iota(jnp.int32, (QB, LANES), 0)
    t_row = qi * QB + ii
    t_hi = (t_row // POS_SPLIT).astype(f32)
    t_lo = (t_row % POS_SPLIT).astype(f32)
    slopes = [slope_ref[g * GQA_REP + r] for r in range(GQA_REP)]
    head = lambda a, r: a[r * QB:(r + 1) * QB]

    def q_head(r):
        x = q_ref[:, (r // 2) * LANES:(r // 2 + 1) * LANES] * SCALE
        if r % 2:
            x = pltpu.roll(x, HEAD_DIM, axis=1)
        m = slopes[r]
        pos = jnp.where(lane == POS_LANE0, (-POS_SPLIT * m) * t_hi,
                        jnp.where(lane == POS_LANE0 + 1, (-m) * t_lo,
                                  jnp.where(lane == POS_LANE0 + 2, POS_SPLIT * m,
                                            jnp.where(lane == POS_LANE0 + 3, m, 0.0))))
        return jnp.where(left, x, pos)

    q4 = jnp.concatenate([q_head(r) for r in range(GQA_REP)], axis=0)
    q4b = q4.astype(bf16)

    def rel_tile(nk, offset):
        i = lax.broadcasted_iota(jnp.int32, (QB, nk), 0)
        j = lax.broadcasted_iota(jnp.int32, (QB, nk), 1)
        return (i - j + offset).astype(f32)

    def softmax_rows(s_ref, e_ref, m_ref, al_ref, nk, bias, running):
        for sl in range(ur // slab):
            rows = slice(sl * slab, (sl + 1) * slab)
            cols = [slice(j * LANES, (j + 1) * LANES) for j in range(nk // LANES)]
            i0 = (sl * slab) % QB
            tiles = [s_ref[rows, c] if bias is None else s_ref[rows, c] + bias[i0:i0 + slab, c] for c in cols]
            mx = tiles[0]
            for t in tiles[1:]:
                mx = jnp.maximum(mx, t)
            m_new = jnp.broadcast_to(jnp.max(mx, axis=1, keepdims=True), (slab, LANES))
            if running:
                m_old = m_ref[rows, :]
                m_new = jnp.maximum(m_old, m_new)
                al_ref[rows, :] = jnp.exp(m_old - m_new)
            if m_ref is not None:
                m_ref[rows, :] = m_new
            for c, t in zip(cols, tiles):
                e_ref[rows, c] = jnp.exp(t - m_new).astype(bf16)

    unit = lambda a, u: a[u * ur:(u + 1) * ur]

    def own_half(pv, u):
        heads = range(u * ur // QB, (u + 1) * ur // QB)
        return jnp.concatenate([pv[(r - heads[0]) * QB:(r - heads[0] + 1) * QB, (r % 2) * LANES:(r % 2 + 1) * LANES]
                                for r in heads], axis=0)

    s4 = _dot_nt(q4b, kca[...])

    sig = jax.nn.sigmoid(gl_ref[...])
    g_hi = sig.astype(bf16)
    g_lo = (sig - g_hi.astype(f32)).astype(bf16)
    gate_b = _dot(jnp.concatenate([g_hi, g_lo], axis=1), gsel_ref[...])

    nwin = WINDOW + QB
    w0 = pl.multiple_of(jnp.maximum(qi - WINDOW // QB, 0) * QB, QB)
    dist_w = rel_tile(nwin, qi * QB - w0)
    bias_w = jnp.where((dist_w >= 0) & (dist_w < WINDOW), 0.0, NEG)
    for u in range(NSA_UNITS):
        s_w[u][...] = _dot_nt(unit(q4b, u), kwa[pl.ds(w0, nwin), :])

    d0 = pl.multiple_of(qi * QB, QB)
    bias_d = jnp.where(rel_tile(QB, 0) >= 0, 0.0, NEG)
    for u in range(NSA_UNITS):
        s_d[u][...] = _dot_nt(unit(q4b, u), ksa[pl.ds(d0, QB), :])
    s_m[0] = _dot_nt(q4b, ksa[:SEL_CHUNK, :])

    cmp_end = (lane * CMP_STRIDE + (CMP_BLOCK - 1))
    valid_c = t_row >= cmp_end
    ps = []
    p_sum = jnp.zeros((QB, LANES), f32)
    row_bcast = lambda col: jnp.broadcast_to(col, (QB, LANES))
    for r in range(GQA_REP):
        s = jnp.where(valid_c, head(s4, r), NEG)
        e = jnp.exp(s - row_bcast(jnp.max(s, axis=1, keepdims=True)))
        p = jnp.where(valid_c, e, 0.0) / row_bcast(jnp.sum(e, axis=1, keepdims=True))
        p_sum = p_sum + p
        ps.append(p.astype(bf16))
    o_cmp4 = _dot(jnp.concatenate(ps, axis=0), vca[...])

    o_w = []
    for u in range(NSA_UNITS):
        softmax_rows(s_w[u], e_w[u], None, None, nwin, bias_w, False)
        o_w.append(own_half(_dot(e_w[u][...], vwa[pl.ds(w0, nwin), :]), u))
    for u in range(NSA_UNITS):
        softmax_rows(s_d[u], e_d[u], m_h[u], None, QB, bias_d, False)
        acc[u][...] = own_half(_dot(e_d[u][...], vsa[pl.ds(d0, QB), :]), u)

    back = t_row // SEL_BLOCK - lane
    valid_s = (back >= 0) & (lane < nsel)

    def _all_valid():
        return jnp.where(valid_s, 1.0, 0.0)

    def _top_k():
        p_hi = p_sum.astype(bf16)
        p_lo = (p_sum - p_hi.astype(f32)).astype(bf16)
        imp_t = (_dot_nt(ovt_ref[...], p_hi) + _dot_nt(ovt_ref[...], p_lo))[:SEL_LANES]
        blk = lax.broadcasted_iota(jnp.int32, (SEL_LANES, QB), 0)
        tq = qi * QB + lax.broadcasted_iota(jnp.int32, (SEL_LANES, QB), 1)
        back_t = tq // SEL_BLOCK - blk
        valid_t = (back_t >= 0) & (blk < nsel)
        forced = (blk == 0) | (valid_t & (back_t < N_LOCAL_BLOCKS))
        score = jnp.where(valid_t, imp_t + jnp.where(forced, FORCE_BONUS, 0.0), NEG)
        score = jnp.where(blk < nsel, score, 2.0 * NEG)
        rank = jnp.zeros((SEL_LANES, QB), jnp.int32)
        for n in range(nsel):
            row = score[n:n + 1, :]
            ahead = (row > score) | ((row == score) & (blk > n))
            rank = rank + ahead.astype(jnp.int32)
        sel_t = jnp.where((rank < n_top) & valid_t, 1.0, 0.0)
        sel_t = jnp.concatenate([sel_t, jnp.zeros((LANES - SEL_LANES, QB), f32)], axis=0)
        return sel_t.T

    sel = _top_k() if topk else _all_valid()

    acc_w = jnp.concatenate(o_w, axis=0)

    def finish(n):
        if n > 0:
            sel_bias = jnp.where((sel > 0.5) & (lane < 2 * qi), 0.0, MASK_BIG)
            sel_bias = pltpu.roll(sel_bias, SEL_LANE0, axis=1)
            in_sel = (lane >= SEL_LANE0) & (lane < SEL_LANE0 + SEL_LANES)
            for r in range(GQA_REP):
                q_scr[r * QB:(r + 1) * QB, :] = jnp.where(in_sel, sel_bias, head(q4, r)).astype(bf16)
            bias0 = _dot_nt(jnp.where(in_sel, sel_bias, 0.0).astype(bf16), ksa[:SEL_CHUNK, :])
        for kc in range(n):
            half = kc % 2
            if kc + 1 < n:
                s_m[1 - half] = _dot_nt(q_scr[...], ksa[(kc + 1) * SEL_CHUNK:(kc + 2) * SEL_CHUNK, :])
            for u in range(NSA_UNITS):
                softmax_rows(s_m.at[half, pl.ds(u * ur, ur)], e_m[u], m_h[u], al_h[u], SEL_CHUNK,
                             bias0 if kc == 0 else None, True)
                acc[u][...] = (al_h[u][...] * acc[u][...]
                               + own_half(_dot(e_m[u][...], vsa[kc * SEL_CHUNK:(kc + 1) * SEL_CHUNK, :]), u))

        acc_s = jnp.concatenate([a[...] for a in acc], axis=0)
        for pr in range(GQA_REP // 2):
            ev, od = 2 * pr, 2 * pr + 1
            out = (gate_b[:, (3 * pr) * LANES:(3 * pr + 1) * LANES]
                   * jnp.where(left, head(o_cmp4, ev), head(o_cmp4, od)))
            for j, a in ((1, acc_s), (2, acc_w)):
                num = jnp.where(left, head(a, ev), head(a, od))
                den = pltpu.roll(jnp.where(left, head(a, od), head(a, ev)), HEAD_DIM, axis=1)
                out = out + gate_b[:, (3 * pr + j) * LANES:(3 * pr + j + 1) * LANES] / den * num
            o_ref[:, pr * LANES:(pr + 1) * LANES] = out.astype(o_ref.dtype)

    finish(n)


def _nsa(proj3, kcmp, vcmp, slopes_b, ovt, kaug, caug, gsel):
    B, S, _ = proj3.shape
    ncb = kcmp.shape[1]
    nrow = GQA_REP * QB
    ur = nrow // NSA_UNITS
    nwin = WINDOW + QB
    kv = lambda j: pl.BlockSpec((None, S, LANES), lambda b, g, qi, j=j: (b, 0, COL_KC + j))
    full = lambda a: pl.BlockSpec(a.shape, lambda b, g, qi: (0,) * a.ndim)
    return pl.pallas_call(
        functools.partial(_nsa_kernel, seq=S),
        grid=(B, N_KV_B, S // QB),
        in_specs=[pl.BlockSpec(memory_space=pltpu.SMEM),
                  pl.BlockSpec((None, QB, 2 * LANES), lambda b, g, qi: (b, qi, COL_QB + g)),
                  kv(2), kv(3), kv(4), kv(5),
                  pl.BlockSpec((None, QB, LANES), lambda b, g, qi: (b, qi, COL_GATE + g)),
                  pl.BlockSpec((None, ncb, LANES), lambda b, g, qi: (b, 0, 0)),
                  pl.BlockSpec((None, ncb, LANES), lambda b, g, qi: (b, 0, 0)),
                  full(ovt), full(kaug), full(caug), full(gsel)],
        out_specs=pl.BlockSpec((None, QB, 2 * LANES), lambda b, g, qi: (b, qi, g)),
        out_shape=jax.ShapeDtypeStruct((B, S, N_HEADS_B * HEAD_DIM), bf16),
        scratch_shapes=([pltpu.VMEM((S, LANES), bf16), pltpu.VMEM((S, 2 * LANES), bf16)] * 2
                        + [pltpu.VMEM((ncb, LANES), bf16)] * 2
                        + [pltpu.VMEM((nrow, LANES), bf16), pltpu.VMEM((2, nrow, SEL_CHUNK), f32)]
                        + [pltpu.VMEM((ur, nwin), f32)] * NSA_UNITS + [pltpu.VMEM((ur, nwin), bf16)] * NSA_UNITS
                        + [pltpu.VMEM((ur, QB), f32)] * NSA_UNITS + [pltpu.VMEM((ur, QB), bf16)] * NSA_UNITS
                        + [pltpu.VMEM((ur, SEL_CHUNK), bf16)] * NSA_UNITS
                        + [pltpu.VMEM((ur, LANES), f32)] * (2 * NSA_UNITS)
                        + [pltpu.VMEM((ur, LANES), f32)] * NSA_UNITS),
        compiler_params=_params("parallel", "parallel", "arbitrary"),
        name="nsa_attn",
    )(slopes_b, proj3, proj3, proj3, proj3, proj3, proj3, kcmp, vcmp, ovt, kaug, caug, gsel)


def _route(h, w_ref, b_ref):
    w = w_ref[...]
    h_hi, w_hi = h.astype(bf16), w.astype(bf16)
    h_lo, w_lo = (h - h_hi.astype(f32)).astype(bf16), (w - w_hi.astype(f32)).astype(bf16)
    logit = _dot(h_hi, w_hi) + (_dot(h_hi, w_lo) + _dot(h_lo, w_hi)) + b_ref[...]
    tm = logit.shape[0]
    lane = lax.broadcasted_iota(jnp.int32, (tm, LANES), 1)
    big = jnp.int32(LANES)
    is_g = lane < N_GROUPS
    gl = jnp.where(is_g, logit, NEG)
    gmax = jnp.max(gl, axis=1, keepdims=True)
    gsum = jnp.sum(jnp.where(is_g, jnp.exp(gl - gmax), 0.0), axis=1, keepdims=True)
    gsel = jnp.min(jnp.where(is_g & (gl == gmax), lane, big), axis=1, keepdims=True)
    gw = 1.0 / gsum
    e_lane = lane - N_GROUPS
    in_grp = (e_lane >= 0) & (e_lane < N_EXPERTS) & (e_lane // EXPERTS_PER_GROUP == gsel)
    el = jnp.where(in_grp, logit, NEG)
    t1 = jnp.max(el, axis=1, keepdims=True)
    i1 = jnp.min(jnp.where(in_grp & (el == t1), lane, big), axis=1, keepdims=True)
    el2 = jnp.where(lane == i1, NEG, el)
    t2 = jnp.max(el2, axis=1, keepdims=True)
    i2 = jnp.min(jnp.where(in_grp & (lane != i1) & (el2 == t2), lane, big), axis=1, keepdims=True)
    e2 = jnp.exp(t2 - t1)
    w1 = gw / (1.0 + e2)
    w2 = gw * e2 / (1.0 + e2)
    return jnp.where(lane == i1, w1, jnp.where(lane == i2, w2, 0.0)), gsel


def _to_token_tiles(ref, x):
    for s in range(D_MODEL // LANES):
        ref[pl.ds(s, x.shape[0], stride=D_MODEL // LANES), :] = x[:, s * LANES:(s + 1) * LANES]


def _from_token_tiles(ref):
    n = D_MODEL // LANES
    return jnp.concatenate([ref[pl.ds(s, ref.shape[0] // n, stride=n), :] for s in range(n)], axis=1)


def _out_route_kernel(x_ref, oa_ref, ob_ref, wa_ref, wb_ref, g_ref, w_ref, b_ref, tri_ref,
                      x1_ref, stage_ref, meta_ref, cnt_ref, cnt_scr):
    @pl.when(pl.program_id(0) == 0)
    def _zero():
        cnt_scr[...] = jnp.zeros_like(cnt_scr)

    x1 = x_ref[...] + _dot(oa_ref[...], wa_ref[...]) + _dot(ob_ref[...], wb_ref[...])
    x1_ref[...] = x1
    h = _rms(x1, g_ref[...])
    _to_token_tiles(stage_ref, h)
    _, gsel = _route(h, w_ref, b_ref)
    tm = h.shape[0]
    lane = lax.broadcasted_iota(jnp.int32, (tm, LANES), 1)
    is_g = lane < N_GROUPS

    onehot = jnp.where(is_g & (lane == gsel), 1.0, 0.0)
    before = _dot(tri_ref[...], onehot.astype(bf16)) + cnt_scr[...]
    rank = jnp.sum(onehot * before, axis=1, keepdims=True)
    cnt_scr[...] = before[tm - 1:tm, :] + onehot[tm - 1:tm, :]
    cnt_ref[...] = cnt_scr[...]
    rank_hi = jnp.floor(rank * (1.0 / RANK_SPLIT))
    cols = jnp.where(lane == 0, gsel.astype(f32), jnp.where(lane == 1, rank_hi,
                                                            jnp.where(lane == 2, rank - RANK_SPLIT * rank_hi, 0.0)))
    pick = (lax.broadcasted_iota(jnp.int32, (8, LANES), 0) == lax.broadcasted_iota(jnp.int32, (8, LANES), 1))
    meta_ref[...] = _dot_nt(jnp.where(pick, 1.0, 0.0).astype(bf16), cols.astype(bf16))


def _out_route(x2d, oa, ob, wa, wb, g, w, b):
    T = x2d.shape[0]
    tm = MOE_TILE
    assert T // RANK_SPLIT <= 256
    tri = jnp.asarray(np.tril(np.ones((tm, tm), np.float32), -1), bf16)
    row = lambda n: pl.BlockSpec((tm, n), lambda i: (i, 0))
    full = lambda a: pl.BlockSpec(a.shape, lambda i: (0,) * a.ndim)
    return pl.pallas_call(
        _out_route_kernel,
        grid=(T // tm,),
        in_specs=[row(D_MODEL), row(oa.shape[1]), row(ob.shape[1]), full(wa), full(wb),
                  full(g), full(w), full(b), full(tri)],
        out_specs=[row(D_MODEL),
                   pl.BlockSpec((tm * TOKEN_ROWS, LANES), lambda i: (i, 0)),
                   pl.BlockSpec((None, 8, tm), lambda i: (i, 0, 0)),
                   pl.BlockSpec((1, LANES), lambda i: (0, 0))],
        out_shape=[jax.ShapeDtypeStruct((T, D_MODEL), f32),
                   jax.ShapeDtypeStruct((T * TOKEN_ROWS, LANES), f32),
                   jax.ShapeDtypeStruct((T // tm, 8, tm), f32),
                   jax.ShapeDtypeStruct((1, LANES), f32)],
        scratch_shapes=[pltpu.VMEM((1, LANES), f32)],
        compiler_params=_params("arbitrary"),
        name="out_proj_router",
    )(x2d, oa, ob, wa, wb, g, w, b, tri)


def _token_rows(t, n=1):
    return pl.ds(pl.multiple_of(t * TOKEN_ROWS, TOKEN_ROWS), n * TOKEN_ROWS)


def _gathered_tile(idx_ref, src_ref, buf_ref, sems, tm):
    i = pl.program_id(0)
    n = pl.num_programs(0)

    def issue(step):
        slot = step % 2

        def one(k, _):
            pltpu.make_async_copy(src_ref.at[_token_rows(idx_ref[step * tm + k])],
                                  buf_ref.at[slot, _token_rows(k)], sems.at[slot]).start()
            return 0

        lax.fori_loop(0, tm, one, 0, unroll=8)

    @pl.when(i == 0)
    def _first():
        issue(i)

    @pl.when(i + 1 < n)
    def _next():
        issue(i + 1)

    slot = i % 2
    pltpu.make_async_copy(src_ref.at[_token_rows(0, tm)], buf_ref.at[slot], sems.at[slot]).wait()
    return slot


def _moe_kernel(tg_ref, src_ref, stage_ref, wr_ref, br_ref, wg_ref, wu_ref, wd_ref, o_ref, buf_ref, sems):
    grp = tg_ref[pl.program_id(0)]
    slot = _gathered_tile(src_ref, stage_ref, buf_ref, sems, MOE_TILE)

    @pl.when(grp < N_GROUPS)
    def _experts():
        h = _from_token_tiles(buf_ref.at[slot])
        gates, _ = _route(h, wr_ref, br_ref)
        x = h.astype(bf16)
        lane = lax.broadcasted_iota(jnp.int32, gates.shape, 1)
        y = jnp.zeros(h.shape, f32)
        for e in range(EXPERTS_PER_GROUP):
            gcol = jnp.sum(jnp.where(lane == N_GROUPS + EXPERTS_PER_GROUP * grp + e, gates, 0.0),
                           axis=1, keepdims=True)
            a = jax.nn.silu(_dot(x, wg_ref[e])) * _dot(x, wu_ref[e])
            y = y + _dot((a * gcol).astype(bf16), wd_ref[e])
        _to_token_tiles(o_ref, y)

    @pl.when(grp >= N_GROUPS)
    def _unused():
        o_ref[...] = jnp.zeros_like(o_ref)


def _moe(stage, tile_group, slot_token, w_route, b_route, wg, wu, wd, layer):
    tm = MOE_TILE
    n_slots = slot_token.shape[0]
    w_spec = lambda k, n: pl.BlockSpec((None, None, EXPERTS_PER_GROUP, k, n),
                                       lambda j, tg, src: (layer, jnp.minimum(tg[j], N_GROUPS - 1), 0, 0, 0))
    grouped = lambda w: w.reshape(w.shape[0], N_GROUPS, EXPERTS_PER_GROUP, *w.shape[2:])
    return pl.pallas_call(
        _moe_kernel,
        grid_spec=pltpu.PrefetchScalarGridSpec(
            num_scalar_prefetch=2, grid=(n_slots // tm,),
            in_specs=[pl.BlockSpec(memory_space=pl.ANY),
                      pl.BlockSpec((D_MODEL, LANES), lambda j, tg, src: (0, 0)),
                      pl.BlockSpec((1, LANES), lambda j, tg, src: (0, 0)),
                      w_spec(D_MODEL, D_EXPERT), w_spec(D_MODEL, D_EXPERT), w_spec(D_EXPERT, D_MODEL)],
            out_specs=pl.BlockSpec((tm * TOKEN_ROWS, LANES), lambda j, tg, src: (j, 0)),
            scratch_shapes=[pltpu.VMEM((2, tm * TOKEN_ROWS, LANES), f32), pltpu.SemaphoreType.DMA((2,))]),
        out_shape=jax.ShapeDtypeStruct((n_slots * TOKEN_ROWS, LANES), f32),
        compiler_params=_params("arbitrary"),
        name="moe_ffn",
    )(tile_group, slot_token, stage, w_route, b_route, grouped(wg), grouped(wu), grouped(wd))


def _ple_kernel(pos_ref, x_ref, ys_ref, p_ref, g_ref, wg_ref, wp_ref, fg_ref, o_ref, buf_ref, sems, *, final):
    slot = _gathered_tile(pos_ref, ys_ref, buf_ref, sems, x_ref.shape[0])
    x = x_ref[...] + _from_token_tiles(buf_ref.at[slot])
    gate = jax.nn.sigmoid(_dot(_rms(x, g_ref[...]).astype(bf16), wg_ref[...]))
    y = x + gate * _dot(p_ref[...].astype(bf16), wp_ref[...])
    o_ref[...] = _rms(y, fg_ref[...]) if final else y


def _ple(x2d, ys, pos, p3, g, wg, wp, fg, layer, final):
    T = x2d.shape[0]
    tm = 512
    full = lambda a: pl.BlockSpec(a.shape, lambda i, pos: (0,) * a.ndim)
    return pl.pallas_call(
        functools.partial(_ple_kernel, final=final),
        grid_spec=pltpu.PrefetchScalarGridSpec(
            num_scalar_prefetch=1, grid=(T // tm,),
            in_specs=[pl.BlockSpec((tm, D_MODEL), lambda i, pos: (i, 0)),
                      pl.BlockSpec(memory_space=pl.ANY),
                      pl.BlockSpec((None, tm, PLE_DIM), lambda i, pos: (layer, i, 0)),
                      full(g), full(wg), full(wp), full(fg)],
            out_specs=pl.BlockSpec((tm, D_MODEL), lambda i, pos: (i, 0)),
            scratch_shapes=[pltpu.VMEM((2, tm * TOKEN_ROWS, LANES), f32), pltpu.SemaphoreType.DMA((2,))]),
        out_shape=jax.ShapeDtypeStruct((T, D_MODEL), f32),
        compiler_params=_params("arbitrary"),
        name="ple",
    )(pos, x2d, ys, p3, g, wg, wp, fg)


def _alibi_slopes():
    s = 2.0 ** (-8.0 * np.arange(1, N_HEADS_TOTAL + 1) / N_HEADS_TOTAL)
    assert np.all(np.log2(s[1::2]) == np.round(np.log2(s[1::2])))
    return jnp.asarray(s[0::2], f32), jnp.asarray(s[1::2], f32)


def _selection_constants(seq):
    ncb = seq // CMP_STRIDE
    nsel = seq // SEL_BLOCK
    n_cmp = (seq - CMP_BLOCK) // CMP_STRIDE + 1
    cs = np.arange(n_cmp) * CMP_STRIDE
    bs = np.arange(nsel) * SEL_BLOCK
    ov = np.clip(np.minimum(cs[:, None] + CMP_BLOCK, bs[None, :] + SEL_BLOCK)
                 - np.maximum(cs[:, None], bs[None, :]), 0, None) / CMP_BLOCK
    assert ncb == LANES and seq <= POS_SPLIT * 256
    ovt = np.zeros((LANES, ncb), np.float32)
    ovt[:nsel, :n_cmp] = ov.T
    pos = np.arange(seq)
    kaug = np.zeros((seq, LANES), np.float32)
    kaug[:, POS_LANE0:POS_LANE0 + 2] = 1.0
    kaug[:, POS_LANE0 + 2] = pos // POS_SPLIT
    kaug[:, POS_LANE0 + 3] = pos % POS_SPLIT
    kaug[pos, SEL_LANE0 + pos // SEL_BLOCK] = 1.0
    cend = np.arange(ncb) * CMP_STRIDE + CMP_BLOCK - 1
    caug = np.zeros((ncb, LANES), np.float32)
    caug[:, POS_LANE0:POS_LANE0 + 2] = 1.0
    caug[:, POS_LANE0 + 2] = cend // POS_SPLIT
    caug[:, POS_LANE0 + 3] = cend % POS_SPLIT
    gsel = np.zeros((2, LANES, GQA_REP // 2, 3, 2, HEAD_DIM), np.float32)
    for pr in range(GQA_REP // 2):
        for j in range(3):
            for hh in range(2):
                gsel[:, 3 * (2 * pr + hh) + j, pr, j, hh, :] = 1.0
    gsel = gsel.reshape(2 * LANES, (GQA_REP // 2) * 3 * LANES)
    return jnp.asarray(ovt, bf16), jnp.asarray(kaug), jnp.asarray(caug), jnp.asarray(gsel, bf16)


def _dispatch_plan(meta, counts, n_tokens):
    i32 = jnp.int32
    n_slots = n_tokens + N_GROUPS * MOE_TILE
    group = meta[:, 0, :].reshape(n_tokens).astype(i32)
    rank = (meta[:, 1, :] * RANK_SPLIT + meta[:, 2, :]).reshape(n_tokens).astype(i32)
    cnt = counts[0, :N_GROUPS].astype(i32)
    padded = (cnt + MOE_TILE - 1) // MOE_TILE * MOE_TILE
    end = jnp.cumsum(padded)
    pos = (end - padded)[group] + rank
    slot_token = jnp.zeros((n_slots,), i32).at[pos].set(jnp.arange(n_tokens, dtype=i32))
    tile_start = jnp.arange(n_slots // MOE_TILE, dtype=i32) * MOE_TILE
    tile_group = jnp.sum(tile_start[:, None] >= end[None, :], axis=1).astype(i32)
    return pos, slot_token, tile_group


def _block_diag2(w):
    z = jnp.zeros_like(w)
    return jnp.concatenate([jnp.concatenate([w, z], axis=-1), jnp.concatenate([z, w], axis=-1)], axis=-2)


def _layout_w_in(w):
    gate = w[:, N_MAIN:]
    per = GQA_REP * 3
    blocks = [jnp.pad(gate[:, g * per:(g + 1) * per], ((0, 0), (0, LANES - per))) for g in range(N_KV_B)]
    return jnp.concatenate([w[:, :N_MAIN]] + blocks, axis=1).astype(bf16)


def kernel(x, p, attn_norm, w_in, w_out, w_cmp_k1, w_cmp_k2, w_cmp_v1, w_cmp_v2, cmp_pos, ffn_norm, w_route_group, b_route_group, w_route_expert, b_route_expert, w_expert_gate, w_expert_up, w_expert_down, ple_norm, w_ple_gate, w_ple_proj, final_norm):
    B, S, D = x.shape
    depth = w_in.shape[0]
    T = B * S
    slopes_a, slopes_b = _alibi_slopes()
    ovt, kaug, caug, gsel = _selection_constants(S)
    wg_all = w_expert_gate.astype(bf16)
    wu_all = w_expert_up.astype(bf16)
    wd_all = w_expert_down.astype(bf16)
    p3 = p.reshape(depth, T, PLE_DIM)
    row = lambda v: v.reshape(1, -1)
    n_route = N_GROUPS + N_EXPERTS

    x2d = x.reshape(T, D)
    for i in range(depth):
        proj3 = _in_proj(x2d, row(attn_norm[i]), _layout_w_in(w_in[i])).reshape(B, S, N_PROJ)
        oa = _dilated(proj3, slopes_a)
        w1 = lambda w: _block_diag2(w.reshape(CMP_BLOCK, HEAD_DIM, CMP_HIDDEN)).astype(bf16)
        pos_dup = jnp.concatenate([cmp_pos[i], cmp_pos[i]], axis=-1)
        kcmp, vcmp = _compress(proj3, pos_dup, w1(w_cmp_k1[i]), _block_diag2(w_cmp_k2[i]).astype(bf16),
                               w1(w_cmp_v1[i]), _block_diag2(w_cmp_v2[i]).astype(bf16))
        ob = _nsa(proj3, kcmp, vcmp, slopes_b, ovt, kaug, caug, gsel)
        wo = w_out[i].astype(bf16)
        w_route = jnp.pad(jnp.concatenate([w_route_group[i], w_route_expert[i]], axis=1),
                          ((0, 0), (0, LANES - n_route)))
        b_route = jnp.pad(jnp.concatenate([b_route_group[i], b_route_expert[i]]), (0, LANES - n_route))
        x2d, stage, meta, counts = _out_route(x2d, oa.reshape(T, A_W), ob.reshape(T, -1), wo[:A_W], wo[A_W:],
                                              row(ffn_norm[i]), w_route, row(b_route))
        pos, slot_token, tile_group = _dispatch_plan(meta, counts, T)
        ys = _moe(stage, tile_group, slot_token, w_route, row(b_route), wg_all, wu_all, wd_all, i)
        x2d = _ple(x2d, ys, pos, p3, row(ple_norm[i]), w_ple_gate[i].astype(bf16),
                   w_ple_proj[i].astype(bf16), row(final_norm), i, i == depth - 1)
    return x2d.reshape(B, S, D)
```

```python
import functools

import numpy as np
import jax
import jax.numpy as jnp
from jax import lax
from jax.experimental import pallas as pl
from jax.experimental.pallas import tpu as pltpu

D_MODEL = 1024
PLE_DIM = 256
HEAD_DIM = 64
N_HEADS_A = 8
N_HEADS_B = 8
N_KV_B = 2
GQA_REP = N_HEADS_B // N_KV_B
N_HEADS_TOTAL = N_HEADS_A + N_HEADS_B
DILATED_PATTERNS = ((128, 1), (512, 4), (2048, 16))
CMP_BLOCK = 32
CMP_STRIDE = 16
CMP_HIDDEN = 256
SEL_BLOCK = 64
SEL_TOP = 16
N_LOCAL_BLOCKS = 2
WINDOW = 512
N_GROUPS = 4
EXPERTS_PER_GROUP = 4
N_EXPERTS = N_GROUPS * EXPERTS_PER_GROUP
D_EXPERT = 512
RMS_EPS = 1e-6
NEG = -1e30
FORCE_BONUS = 1e4
SCALE = HEAD_DIM ** -0.5

LANES = 128
QB = 128
SEL_CHUNK = 512
NSA_UNITS = 2
DIL_INFLIGHT = 3
DIL_TRIP = 16
POS_LANE0 = HEAD_DIM
POS_SPLIT = 16
SEL_LANE0 = POS_LANE0 + 4
SEL_LANES = 32
MASK_BIG = -(2.0 ** 100)
MOE_TILE = 512
TOKEN_ROWS = D_MODEL // LANES
RANK_SPLIT = 128
A_W = N_HEADS_A * HEAD_DIM
N_MAIN = 3 * A_W + N_HEADS_B * HEAD_DIM + 6 * N_KV_B * HEAD_DIM
N_PROJ = N_MAIN + N_KV_B * LANES
COL_QB = (3 * A_W) // (2 * LANES)
COL_KC = (3 * A_W + N_HEADS_B * HEAD_DIM) // LANES
COL_GATE = N_MAIN // LANES
VMEM_LIMIT = 56 * 1024 * 1024

f32 = jnp.float32
bf16 = jnp.bfloat16


def _dot(a, b):
    return jnp.dot(a, b, preferred_element_type=f32)


def _dot_nt(a, b):
    return lax.dot_general(a, b, (((1,), (1,)), ((), ())), preferred_element_type=f32)


def _rms(x, g):
    return x * lax.rsqrt(jnp.mean(x * x, axis=-1, keepdims=True) + RMS_EPS) * g


def _params(*sem):
    return pltpu.CompilerParams(dimension_semantics=sem, vmem_limit_bytes=VMEM_LIMIT)


def _in_proj_kernel(x_ref, g_ref, w_ref, o_ref):
    h = _rms(x_ref[...], g_ref[...]).astype(bf16)
    for n0 in range(0, N_PROJ, 512):
        o_ref[:, n0:n0 + 512] = _dot(h, w_ref[:, n0:n0 + 512])


def _in_proj(x2d, g, w):
    T = x2d.shape[0]
    tm = 512
    return pl.pallas_call(
        _in_proj_kernel,
        grid=(T // tm,),
        in_specs=[pl.BlockSpec((tm, D_MODEL), lambda i: (i, 0)),
                  pl.BlockSpec((1, D_MODEL), lambda i: (0, 0)),
                  pl.BlockSpec((D_MODEL, N_PROJ), lambda i: (0, 0))],
        out_specs=pl.BlockSpec((tm, N_PROJ), lambda i: (i, 0)),
        out_shape=jax.ShapeDtypeStruct((T, N_PROJ), f32),
        compiler_params=_params("parallel"),
        name="in_proj",
    )(x2d, g, w)


def _dil_kernel(slope_ref, q_ref, k_ref, v_ref, o_ref, out_ref, lse_ref, bias_scr, *bufs, seq):
    hp = pl.program_id(1)
    lane = lax.broadcasted_iota(jnp.int32, (QB, LANES), 1)
    left = lane < HEAD_DIM
    slab = 32
    s_bufs, e_bufs = bufs[:DIL_INFLIGHT], bufs[DIL_INFLIGHT:]

    i = lax.broadcasted_iota(jnp.int32, (QB, 2 * QB), 0)
    j = lax.broadcasted_iota(jnp.int32, (QB, 2 * QB), 1)
    rel = i - j + QB
    valid = (rel >= 0) & (rel <= QB)
    relf = rel.astype(f32)
    for p, (window, dil) in enumerate(DILATED_PATTERNS):
        assert window // dil == QB
        for hh in range(2):
            bias_scr[2 * p + hh] = jnp.where(valid, (-float(dil) * slope_ref[2 * hp + hh]) * relf, NEG)

    ones = jnp.ones((2 * QB, LANES), bf16)

    def scores(u, blk):
        p, dil, row0, key0, nk = blk
        qc = q_ref[pl.ds(row0, QB, stride=dil), :] * SCALE
        q2 = jnp.concatenate([jnp.where(left, qc, 0.0), jnp.where(left, 0.0, qc)], axis=0).astype(bf16)
        s_bufs[u][:, :nk] = _dot_nt(q2, k_ref[pl.ds(key0, nk, stride=dil), :].astype(bf16))

    def probs(u, blk):
        p, dil, row0, key0, nk = blk
        c0 = 2 * QB - nk
        ms = []
        for hh in range(2):
            parts = []
            for sl in range(QB // slab):
                rows = slice(hh * QB + sl * slab, hh * QB + (sl + 1) * slab)
                tiles = [s_bufs[u][rows, c:c + LANES]
                         + bias_scr[2 * p + hh, sl * slab:(sl + 1) * slab, c0 + c:c0 + c + LANES]
                         for c in range(0, nk, LANES)]
                mx = tiles[0]
                for t in tiles[1:]:
                    mx = jnp.maximum(mx, t)
                mx = jnp.broadcast_to(jnp.max(mx, axis=1, keepdims=True), (slab, LANES))
                for c, t in zip(range(0, nk, LANES), tiles):
                    e_bufs[u][rows, c:c + LANES] = jnp.exp(t - mx).astype(bf16)
                parts.append(mx)
            ms.append(jnp.concatenate(parts, axis=0))
        return jnp.where(left, ms[0], ms[1])

    def values(u, blk, row_max):
        p, dil, row0, key0, nk = blk
        v2 = jnp.concatenate([v_ref[pl.ds(key0, nk, stride=dil), :].astype(bf16), ones[:nk]], axis=1)
        res = _dot(e_bufs[u][:, :nk], v2)
        rows = pl.ds(row0, QB, stride=dil)
        den = jnp.where(left, res[:QB, LANES:], res[QB:, LANES:])
        out_ref[p, rows, :] = jnp.where(left, res[:QB, :LANES], res[QB:, :LANES]) / den
        lse_ref[p, rows, :] = row_max + jnp.log(den)

    def run(blocks):
        ahead = DIL_INFLIGHT - 1
        for n in range(min(ahead, len(blocks))):
            scores(n % DIL_INFLIGHT, blocks[n])
        for n, blk in enumerate(blocks):
            if n + ahead < len(blocks):
                scores((n + ahead) % DIL_INFLIGHT, blocks[n + ahead])
            values(n % DIL_INFLIGHT, blk, probs(n % DIL_INFLIGHT, blk))

    def first(p, dil, r):
        return (p, dil, r, r, QB)

    def later(p, dil, r, a):
        return (p, dil, r + dil * QB * a, r + dil * QB * (a - 1), 2 * QB)

    for p, (window, dil) in enumerate(DILATED_PATTERNS):
        nblk = seq // dil // QB
        blocks = [first(p, dil, r) if a == 0 else later(p, dil, r, a) for r in range(dil) for a in range(nblk)]
        for b0 in range(0, len(blocks), DIL_TRIP):
            run(blocks[b0:b0 + DIL_TRIP])

    ch = 256

    for c in range(seq // ch):
        rows = slice(c * ch, (c + 1) * ch)
        lses = [lse_ref[p, rows, :] for p in range(len(DILATED_PATTERNS))]
        big = functools.reduce(jnp.maximum, lses)
        num = jnp.zeros((ch, LANES), f32)
        den = jnp.zeros((ch, LANES), f32)
        for p, lse in enumerate(lses):
            w = jnp.exp(lse - big)
            num = num + w * out_ref[p, rows, :]
            den = den + w
        o_ref[rows, :] = (num / den).astype(o_ref.dtype)


def _dilated(proj3, slopes_a):
    B, S, _ = proj3.shape
    npair = N_HEADS_A // 2
    blk = lambda off: pl.BlockSpec((None, S, LANES), lambda b, hp, off=off: (b, 0, off + hp))
    return pl.pallas_call(
        functools.partial(_dil_kernel, seq=S),
        grid=(B, npair),
        in_specs=[pl.BlockSpec(memory_space=pltpu.SMEM), blk(0), blk(npair), blk(2 * npair)],
        out_specs=pl.BlockSpec((None, S, LANES), lambda b, hp: (b, 0, hp)),
        out_shape=jax.ShapeDtypeStruct((B, S, A_W), bf16),
        scratch_shapes=([pltpu.VMEM((len(DILATED_PATTERNS), S, LANES), f32)] * 2
                        + [pltpu.VMEM((2 * len(DILATED_PATTERNS), QB, 2 * QB), f32)]
                        + [pltpu.VMEM((2 * QB, 2 * QB), f32)] * DIL_INFLIGHT
                        + [pltpu.VMEM((2 * QB, 2 * QB), bf16)] * DIL_INFLIGHT),
        compiler_params=_params("parallel", "parallel"),
        name="dilated_attn",
    )(slopes_a, proj3, proj3, proj3)


def _cmp_kernel(kc_ref, vc_ref, pos_ref, w1k_ref, w2k_ref, w1v_ref, w2v_ref, ko_ref, vo_ref, *, ncb):
    half = CMP_BLOCK // 2
    for x_ref, pi, w1_ref, w2_ref, o_ref in ((kc_ref, 0, w1k_ref, w2k_ref, ko_ref),
                                             (vc_ref, 1, w1v_ref, w2v_ref, vo_ref)):
        lo = jnp.zeros((ncb, 2 * CMP_HIDDEN), f32)
        hi = jnp.zeros((ncb, 2 * CMP_HIDDEN), f32)
        for r in range(half):
            x = x_ref[pl.ds(r, ncb, stride=CMP_STRIDE), :]
            lo = lo + _dot((x + pos_ref[pi, r:r + 1, :]).astype(bf16), w1_ref[r])
            hi = hi + _dot((x + pos_ref[pi, r + half:r + half + 1, :]).astype(bf16), w1_ref[r + half])
        h1 = lo + pltpu.roll(hi, ncb - 1, axis=0)
        o_ref[...] = _dot(jax.nn.gelu(h1).astype(bf16), w2_ref[...])


def _compress(proj3, pos_dup, w1k, w2k, w1v, w2v):
    B, S, _ = proj3.shape
    ncb = S // CMP_STRIDE
    full = lambda a: pl.BlockSpec(a.shape, lambda b: (0,) * a.ndim)
    out = jax.ShapeDtypeStruct((B, ncb, LANES), f32)
    return pl.pallas_call(
        functools.partial(_cmp_kernel, ncb=ncb),
        grid=(B,),
        in_specs=[pl.BlockSpec((None, S, LANES), lambda b: (b, 0, COL_KC)),
                  pl.BlockSpec((None, S, LANES), lambda b: (b, 0, COL_KC + 1)),
                  full(pos_dup), full(w1k), full(w2k), full(w1v), full(w2v)],
        out_specs=[pl.BlockSpec((None, ncb, LANES), lambda b: (b, 0, 0))] * 2,
        out_shape=[out, out],
        compiler_params=_params("parallel"),
        name="nsa_compress",
    )(proj3, proj3, pos_dup, w1k, w2k, w1v, w2v)


def _nsa_kernel(slope_ref, q_ref, ks_ref, vs_ref, kw_ref, vw_ref, gl_ref, kcmp_ref, vcmp_ref,
                ovt_ref, kaug_ref, caug_ref, gsel_ref, o_ref,
                ksa, vsa, kwa, vwa, kca, vca, q_scr, s_m, *per_unit, seq):
    g = pl.program_id(1)
    qi = pl.program_id(2)
    nsel = seq // SEL_BLOCK
    n_top = min(SEL_TOP, nsel)
    assert nsel <= SEL_LANES and N_LOCAL_BLOCKS * SEL_BLOCK >= QB and n_top > N_LOCAL_BLOCKS

    def group_lanes(x):
        return jnp.where(g == 0, x, pltpu.roll(x, HEAD_DIM, axis=1))

    @pl.when(qi == 0)
    def _prep():
        ch = 256
        lane_c = lax.broadcasted_iota(jnp.int32, (ch, LANES), 1)
        is_k = lane_c < HEAD_DIM
        for c in range(seq // ch):
            sl = slice(c * ch, (c + 1) * ch)
            aug = kaug_ref[sl, :]
            ksa[sl, :] = jnp.where(is_k, group_lanes(ks_ref[sl, :]), aug).astype(bf16)
            kwa[sl, :] = jnp.where(is_k, group_lanes(kw_ref[sl, :]),
                                   jnp.where(lane_c < SEL_LANE0, aug, 0.0)).astype(bf16)
            for src, dst in ((vs_ref, vsa), (vw_ref, vwa)):
                v = group_lanes(src[sl, :])
                dst[sl, :LANES] = jnp.where(is_k, v, 1.0).astype(bf16)
                dst[sl, LANES:] = jnp.where(is_k, 1.0, pltpu.roll(v, HEAD_DIM, axis=1)).astype(bf16)
        lane_k = lax.broadcasted_iota(jnp.int32, kca.shape, 1)
        kca[...] = jnp.where(lane_k < HEAD_DIM, group_lanes(kcmp_ref[...]), caug_ref[...]).astype(bf16)
        vc = group_lanes(vcmp_ref[...])
        vca[...] = jnp.where(lane_k < HEAD_DIM, vc, pltpu.roll(vc, HEAD_DIM, axis=1)).astype(bf16)

    per = SEL_CHUNK // QB
    n_chunks = (qi + per - 1) // per
    need_topk = 2 * qi + 2 > n_top
    refs = (slope_ref, q_ref, gl_ref, ovt_ref, gsel_ref, o_ref, ksa, vsa, kwa, vwa, kca, vca, q_scr, s_m, per_unit)
    for n in range((seq // QB - 1 + per - 1) // per + 1):
        qi_lo, qi_hi = max(per * (n - 1) + 1, 0), min(per * n, seq // QB - 1)
        for topk in sorted({2 * q + 2 > n_top for q in range(qi_lo, qi_hi + 1)}):
            pl.when((n_chunks == n) & (need_topk == topk))(
                functools.partial(_nsa_step, refs, seq=seq, n=n, topk=topk))


def _nsa_step(refs, *, seq, n, topk):
    slope_ref, q_ref, gl_ref, ovt_ref, gsel_ref, o_ref, ksa, vsa, kwa, vwa, kca, vca, q_scr, s_m, per_unit = refs
    g = pl.program_id(1)
    qi = pl.program_id(2)
    nsel = seq // SEL_BLOCK
    n_top = min(SEL_TOP, nsel)
    slab = 32
    ur = GQA_REP * QB // NSA_UNITS
    s_w, e_w, s_d, e_d, e_m, m_h, al_h, acc = [per_unit[i * NSA_UNITS:(i + 1) * NSA_UNITS] for i in range(8)]
    lane = lax.broadcasted_iota(jnp.int32, (QB, LANES), 1)
    left = lane < HEAD_DIM
    ii = lax.broadcasted_iota(jnp.int32, (QB, LANES), 0)
    t_row = qi * QB + ii
    t_hi = (t_row // POS_SPLIT).astype(f32)
    t_lo = (t_row % POS_SPLIT).astype(f32)
    slopes = [slope_ref[g * GQA_REP + r] for r in range(GQA_REP)]
    head = lambda a, r: a[r * QB:(r + 1) * QB]

    def q_head(r):
        x = q_ref[:, (r // 2) * LANES:(r // 2 + 1) * LANES] * SCALE
        if r % 2:
            x = pltpu.roll(x, HEAD_DIM, axis=1)
        m = slopes[r]
        pos = jnp.where(lane == POS_LANE0, (-POS_SPLIT * m) * t_hi,
                        jnp.where(lane == POS_LANE0 + 1, (-m) * t_lo,
                                  jnp.where(lane == POS_LANE0 + 2, POS_SPLIT * m,
                                            jnp.where(lane == POS_LANE0 + 3, m, 0.0))))
        return jnp.where(left, x, pos)

    q4 = jnp.concatenate([q_head(r) for r in range(GQA_REP)], axis=0)
    q4b = q4.astype(bf16)

    def rel_tile(nk, offset):
        i = lax.broadcasted_iota(jnp.int32, (QB, nk), 0)
        j = lax.broadcasted_iota(jnp.int32, (QB, nk), 1)
        return (i - j + offset).astype(f32)

    def softmax_rows(s_ref, e_ref, m_ref, al_ref, nk, bias, running):
        for sl in range(ur // slab):
            rows = slice(sl * slab, (sl + 1) * slab)
            cols = [slice(j * LANES, (j + 1) * LANES) for j in range(nk // LANES)]
            i0 = (sl * slab) % QB
            tiles = [s_ref[rows, c] if bias is None else s_ref[rows, c] + bias[i0:i0 + slab, c] for c in cols]
            mx = tiles[0]
            for t in tiles[1:]:
                mx = jnp.maximum(mx, t)
            m_new = jnp.broadcast_to(jnp.max(mx, axis=1, keepdims=True), (slab, LANES))
            if running:
                m_old = m_ref[rows, :]
                m_new = jnp.maximum(m_old, m_new)
                al_ref[rows, :] = jnp.exp(m_old - m_new)
            if m_ref is not None:
                m_ref[rows, :] = m_new
            for c, t in zip(cols, tiles):
                e_ref[rows, c] = jnp.exp(t - m_new).astype(bf16)

    unit = lambda a, u: a[u * ur:(u + 1) * ur]

    def own_half(pv, u):
        heads = range(u * ur // QB, (u + 1) * ur // QB)
        return jnp.concatenate([pv[(r - heads[0]) * QB:(r - heads[0] + 1) * QB, (r % 2) * LANES:(r % 2 + 1) * LANES]
                                for r in heads], axis=0)

    s4 = _dot_nt(q4b, kca[...])

    sig = jax.nn.sigmoid(gl_ref[...])
    g_hi = sig.astype(bf16)
    g_lo = (sig - g_hi.astype(f32)).astype(bf16)
    gate_b = _dot(jnp.concatenate([g_hi, g_lo], axis=1), gsel_ref[...])

    nwin = WINDOW + QB
    w0 = pl.multiple_of(jnp.maximum(qi - WINDOW // QB, 0) * QB, QB)
    dist_w = rel_tile(nwin, qi * QB - w0)
    bias_w = jnp.where((dist_w >= 0) & (dist_w < WINDOW), 0.0, NEG)
    for u in range(NSA_UNITS):
        s_w[u][...] = _dot_nt(unit(q4b, u), kwa[pl.ds(w0, nwin), :])

    d0 = pl.multiple_of(qi * QB, QB)
    bias_d = jnp.where(rel_tile(QB, 0) >= 0, 0.0, NEG)
    for u in range(NSA_UNITS):
        s_d[u][...] = _dot_nt(unit(q4b, u), ksa[pl.ds(d0, QB), :])
    s_m[0] = _dot_nt(q4b, ksa[:SEL_CHUNK, :])

    cmp_end = (lane * CMP_STRIDE + (CMP_BLOCK - 1))
    valid_c = t_row >= cmp_end
    ps = []
    p_sum = jnp.zeros((QB, LANES), f32)
    row_bcast = lambda col: jnp.broadcast_to(col, (QB, LANES))
    for r in range(GQA_REP):
        s = jnp.where(valid_c, head(s4, r), NEG)
        e = jnp.exp(s - row_bcast(jnp.max(s, axis=1, keepdims=True)))
        p = jnp.where(valid_c, e, 0.0) / row_bcast(jnp.sum(e, axis=1, keepdims=True))
        p_sum = p_sum + p
        ps.append(p.astype(bf16))
    o_cmp4 = _dot(jnp.concatenate(ps, axis=0), vca[...])

    o_w = []
    for u in range(NSA_UNITS):
        softmax_rows(s_w[u], e_w[u], None, None, nwin, bias_w, False)
        o_w.append(own_half(_dot(e_w[u][...], vwa[pl.ds(w0, nwin), :]), u))
    for u in range(NSA_UNITS):
        softmax_rows(s_d[u], e_d[u], m_h[u], None, QB, bias_d, False)
        acc[u][...] = own_half(_dot(e_d[u][...], vsa[pl.ds(d0, QB), :]), u)

    back = t_row // SEL_BLOCK - lane
    valid_s = (back >= 0) & (lane < nsel)

    def _all_valid():
        return jnp.where(valid_s, 1.0, 0.0)

    def _top_k():
        p_hi = p_sum.astype(bf16)
        p_lo = (p_sum - p_hi.astype(f32)).astype(bf16)
        imp_t = (_dot_nt(ovt_ref[...], p_hi) + _dot_nt(ovt_ref[...], p_lo))[:SEL_LANES]
        blk = lax.broadcasted_iota(jnp.int32, (SEL_LANES, QB), 0)
        tq = qi * QB + lax.broadcasted_iota(jnp.int32, (SEL_LANES, QB), 1)
        back_t = tq // SEL_BLOCK - blk
        valid_t = (back_t >= 0) & (blk < nsel)
        forced = (blk == 0) | (valid_t & (back_t < N_LOCAL_BLOCKS))
        score = jnp.where(valid_t, imp_t + jnp.where(forced, FORCE_BONUS, 0.0), NEG)
        score = jnp.where(blk < nsel, score, 2.0 * NEG)
        rank = jnp.zeros((SEL_LANES, QB), jnp.int32)
        for n in range(nsel):
            row = score[n:n + 1, :]
            ahead = (row > score) | ((row == score) & (blk > n))
            rank = rank + ahead.astype(jnp.int32)
        sel_t = jnp.where((rank < n_top) & valid_t, 1.0, 0.0)
        sel_t = jnp.concatenate([sel_t, jnp.zeros((LANES - SEL_LANES, QB), f32)], axis=0)
        return sel_t.T

    sel = _top_k() if topk else _all_valid()

    acc_w = jnp.concatenate(o_w, axis=0)

    def finish(n):
        if n > 0:
            sel_bias = jnp.where((sel > 0.5) & (lane < 2 * qi), 0.0, MASK_BIG)
            sel_bias = pltpu.roll(sel_bias, SEL_LANE0, axis=1)
            in_sel = (lane >= SEL_LANE0) & (lane < SEL_LANE0 + SEL_LANES)
            for r in range(GQA_REP):
                q_scr[r * QB:(r + 1) * QB, :] = jnp.where(in_sel, sel_bias, head(q4, r)).astype(bf16)
            bias0 = _dot_nt(jnp.where(in_sel, sel_bias, 0.0).astype(bf16), ksa[:SEL_CHUNK, :])
        for kc in range(n):
            half = kc % 2
            if kc + 1 < n:
                s_m[1 - half] = _dot_nt(q_scr[...], ksa[(kc + 1) * SEL_CHUNK:(kc + 2) * SEL_CHUNK, :])
            for u in range(NSA_UNITS):
                softmax_rows(s_m.at[half, pl.ds(u * ur, ur)], e_m[u], m_h[u], al_h[u], SEL_CHUNK,
                             bias0 if kc == 0 else None, True)
                acc[u][...] = (al_h[u][...] * acc[u][...]
                               + own_half(_dot(e_m[u][...], vsa[kc * SEL_CHUNK:(kc + 1) * SEL_CHUNK, :]), u))

        acc_s = jnp.concatenate([a[...] for a in acc], axis=0)
        for pr in range(GQA_REP // 2):
            ev, od = 2 * pr, 2 * pr + 1
            out = (gate_b[:, (3 * pr) * LANES:(3 * pr + 1) * LANES]
                   * jnp.where(left, head(o_cmp4, ev), head(o_cmp4, od)))
            for j, a in ((1, acc_s), (2, acc_w)):
                num = jnp.where(left, head(a, ev), head(a, od))
                den = pltpu.roll(jnp.where(left, head(a, od), head(a, ev)), HEAD_DIM, axis=1)
                out = out + gate_b[:, (3 * pr + j) * LANES:(3 * pr + j + 1) * LANES] / den * num
            o_ref[:, pr * LANES:(pr + 1) * LANES] = out.astype(o_ref.dtype)

    finish(n)


def _nsa(proj3, kcmp, vcmp, slopes_b, ovt, kaug, caug, gsel):
    B, S, _ = proj3.shape
    ncb = kcmp.shape[1]
    nrow = GQA_REP * QB
    ur = nrow // NSA_UNITS
    nwin = WINDOW + QB
    kv = lambda j: pl.BlockSpec((None, S, LANES), lambda b, g, qi, j=j: (b, 0, COL_KC + j))
    full = lambda a: pl.BlockSpec(a.shape, lambda b, g, qi: (0,) * a.ndim)
    return pl.pallas_call(
        functools.partial(_nsa_kernel, seq=S),
        grid=(B, N_KV_B, S // QB),
        in_specs=[pl.BlockSpec(memory_space=pltpu.SMEM),
                  pl.BlockSpec((None, QB, 2 * LANES), lambda b, g, qi: (b, qi, COL_QB + g)),
                  kv(2), kv(3), kv(4), kv(5),
                  pl.BlockSpec((None, QB, LANES), lambda b, g, qi: (b, qi, COL_GATE + g)),
                  pl.BlockSpec((None, ncb, LANES), lambda b, g, qi: (b, 0, 0)),
                  pl.BlockSpec((None, ncb, LANES), lambda b, g, qi: (b, 0, 0)),
                  full(ovt), full(kaug), full(caug), full(gsel)],
        out_specs=pl.BlockSpec((None, QB, 2 * LANES), lambda b, g, qi: (b, qi, g)),
        out_shape=jax.ShapeDtypeStruct((B, S, N_HEADS_B * HEAD_DIM), bf16),
        scratch_shapes=([pltpu.VMEM((S, LANES), bf16), pltpu.VMEM((S, 2 * LANES), bf16)] * 2
                        + [pltpu.VMEM((ncb, LANES), bf16)] * 2
                        + [pltpu.VMEM((nrow, LANES), bf16), pltpu.VMEM((2, nrow, SEL_CHUNK), f32)]
                        + [pltpu.VMEM((ur, nwin), f32)] * NSA_UNITS + [pltpu.VMEM((ur, nwin), bf16)] * NSA_UNITS
                        + [pltpu.VMEM((ur, QB), f32)] * NSA_UNITS + [pltpu.VMEM((ur, QB), bf16)] * NSA_UNITS
                        + [pltpu.VMEM((ur, SEL_CHUNK), bf16)] * NSA_UNITS
                        + [pltpu.VMEM((ur, LANES), f32)] * (2 * NSA_UNITS)
                        + [pltpu.VMEM((ur, LANES), f32)] * NSA_UNITS),
        compiler_params=_params("parallel", "parallel", "arbitrary"),
        name="nsa_attn",
    )(slopes_b, proj3, proj3, proj3, proj3, proj3, proj3, kcmp, vcmp, ovt, kaug, caug, gsel)


def _route(h, w_ref, b_ref):
    w = w_ref[...]
    h_hi, w_hi = h.astype(bf16), w.astype(bf16)
    h_lo, w_lo = (h - h_hi.astype(f32)).astype(bf16), (w - w_hi.astype(f32)).astype(bf16)
    logit = _dot(h_hi, w_hi) + (_dot(h_hi, w_lo) + _dot(h_lo, w_hi)) + b_ref[...]
    tm = logit.shape[0]
    lane = lax.broadcasted_iota(jnp.int32, (tm, LANES), 1)
    big = jnp.int32(LANES)
    is_g = lane < N_GROUPS
    gl = jnp.where(is_g, logit, NEG)
    gmax = jnp.max(gl, axis=1, keepdims=True)
    gsum = jnp.sum(jnp.where(is_g, jnp.exp(gl - gmax), 0.0), axis=1, keepdims=True)
    gsel = jnp.min(jnp.where(is_g & (gl == gmax), lane, big), axis=1, keepdims=True)
    gw = 1.0 / gsum
    e_lane = lane - N_GROUPS
    in_grp = (e_lane >= 0) & (e_lane < N_EXPERTS) & (e_lane // EXPERTS_PER_GROUP == gsel)
    el = jnp.where(in_grp, logit, NEG)
    t1 = jnp.max(el, axis=1, keepdims=True)
    i1 = jnp.min(jnp.where(in_grp & (el == t1), lane, big), axis=1, keepdims=True)
    el2 = jnp.where(lane == i1, NEG, el)
    t2 = jnp.max(el2, axis=1, keepdims=True)
    i2 = jnp.min(jnp.where(in_grp & (lane != i1) & (el2 == t2), lane, big), axis=1, keepdims=True)
    e2 = jnp.exp(t2 - t1)
    w1 = gw / (1.0 + e2)
    w2 = gw * e2 / (1.0 + e2)
    return jnp.where(lane == i1, w1, jnp.where(lane == i2, w2, 0.0)), gsel


def _to_token_tiles(ref, x):
    for s in range(D_MODEL // LANES):
        ref[pl.ds(s, x.shape[0], stride=D_MODEL // LANES), :] = x[:, s * LANES:(s + 1) * LANES]


def _from_token_tiles(ref):
    n = D_MODEL // LANES
    return jnp.concatenate([ref[pl.ds(s, ref.shape[0] // n, stride=n), :] for s in range(n)], axis=1)


def _out_route_kernel(x_ref, oa_ref, ob_ref, wa_ref, wb_ref, g_ref, w_ref, b_ref, tri_ref,
                      x1_ref, stage_ref, meta_ref, cnt_ref, cnt_scr):
    @pl.when(pl.program_id(0) == 0)
    def _zero():
        cnt_scr[...] = jnp.zeros_like(cnt_scr)

    x1 = x_ref[...] + _dot(oa_ref[...], wa_ref[...]) + _dot(ob_ref[...], wb_ref[...])
    x1_ref[...] = x1
    h = _rms(x1, g_ref[...])
    _to_token_tiles(stage_ref, h)
    _, gsel = _route(h, w_ref, b_ref)
    tm = h.shape[0]
    lane = lax.broadcasted_iota(jnp.int32, (tm, LANES), 1)
    is_g = lane < N_GROUPS

    onehot = jnp.where(is_g & (lane == gsel), 1.0, 0.0)
    before = _dot(tri_ref[...], onehot.astype(bf16)) + cnt_scr[...]
    rank = jnp.sum(onehot * before, axis=1, keepdims=True)
    cnt_scr[...] = before[tm - 1:tm, :] + onehot[tm - 1:tm, :]
    cnt_ref[...] = cnt_scr[...]
    rank_hi = jnp.floor(rank * (1.0 / RANK_SPLIT))
    cols = jnp.where(lane == 0, gsel.astype(f32), jnp.where(lane == 1, rank_hi,
                                                            jnp.where(lane == 2, rank - RANK_SPLIT * rank_hi, 0.0)))
    pick = (lax.broadcasted_iota(jnp.int32, (8, LANES), 0) == lax.broadcasted_iota(jnp.int32, (8, LANES), 1))
    meta_ref[...] = _dot_nt(jnp.where(pick, 1.0, 0.0).astype(bf16), cols.astype(bf16))


def _out_route(x2d, oa, ob, wa, wb, g, w, b):
    T = x2d.shape[0]
    tm = MOE_TILE
    assert T // RANK_SPLIT <= 256
    tri = jnp.asarray(np.tril(np.ones((tm, tm), np.float32), -1), bf16)
    row = lambda n: pl.BlockSpec((tm, n), lambda i: (i, 0))
    full = lambda a: pl.BlockSpec(a.shape, lambda i: (0,) * a.ndim)
    return pl.pallas_call(
        _out_route_kernel,
        grid=(T // tm,),
        in_specs=[row(D_MODEL), row(oa.shape[1]), row(ob.shape[1]), full(wa), full(wb),
                  full(g), full(w), full(b), full(tri)],
        out_specs=[row(D_MODEL),
                   pl.BlockSpec((tm * TOKEN_ROWS, LANES), lambda i: (i, 0)),
                   pl.BlockSpec((None, 8, tm), lambda i: (i, 0, 0)),
                   pl.BlockSpec((1, LANES), lambda i: (0, 0))],
        out_shape=[jax.ShapeDtypeStruct((T, D_MODEL), f32),
                   jax.ShapeDtypeStruct((T * TOKEN_ROWS, LANES), f32),
                   jax.ShapeDtypeStruct((T // tm, 8, tm), f32),
                   jax.ShapeDtypeStruct((1, LANES), f32)],
        scratch_shapes=[pltpu.VMEM((1, LANES), f32)],
        compiler_params=_params("arbitrary"),
        name="out_proj_router",
    )(x2d, oa, ob, wa, wb, g, w, b, tri)


def _token_rows(t, n=1):
    return pl.ds(pl.multiple_of(t * TOKEN_ROWS, TOKEN_ROWS), n * TOKEN_ROWS)


def _gathered_tile(idx_ref, src_ref, buf_ref, sems, tm):
    i = pl.program_id(0)
    n = pl.num_programs(0)

    def issue(step):
        slot = step % 2

        def one(k, _):
            pltpu.make_async_copy(src_ref.at[_token_rows(idx_ref[step * tm + k])],
                                  buf_ref.at[slot, _token_rows(k)], sems.at[slot]).start()
            return 0

        lax.fori_loop(0, tm, one, 0, unroll=8)

    @pl.when(i == 0)
    def _first():
        issue(i)

    @pl.when(i + 1 < n)
    def _next():
        issue(i + 1)

    slot = i % 2
    pltpu.make_async_copy(src_ref.at[_token_rows(0, tm)], buf_ref.at[slot], sems.at[slot]).wait()
    return slot


def _moe_kernel(tg_ref, src_ref, stage_ref, wr_ref, br_ref, wg_ref, wu_ref, wd_ref, o_ref, buf_ref, sems):
    grp = tg_ref[pl.program_id(0)]
    slot = _gathered_tile(src_ref, stage_ref, buf_ref, sems, MOE_TILE)

    @pl.when(grp < N_GROUPS)
    def _experts():
        h = _from_token_tiles(buf_ref.at[slot])
        gates, _ = _route(h, wr_ref, br_ref)
        x = h.astype(bf16)
        lane = lax.broadcasted_iota(jnp.int32, gates.shape, 1)
        y = jnp.zeros(h.shape, f32)
        for e in range(EXPERTS_PER_GROUP):
            gcol = jnp.sum(jnp.where(lane == N_GROUPS + EXPERTS_PER_GROUP * grp + e, gates, 0.0),
                           axis=1, keepdims=True)
            a = jax.nn.silu(_dot(x, wg_ref[e])) * _dot(x, wu_ref[e])
            y = y + _dot((a * gcol).astype(bf16), wd_ref[e])
        _to_token_tiles(o_ref, y)

    @pl.when(grp >= N_GROUPS)
    def _unused():
        o_ref[...] = jnp.zeros_like(o_ref)


def _moe(stage, tile_group, slot_token, w_route, b_route, wg, wu, wd, layer):
    tm = MOE_TILE
    n_slots = slot_token.shape[0]
    w_spec = lambda k, n: pl.BlockSpec((None, None, EXPERTS_PER_GROUP, k, n),
                                       lambda j, tg, src: (layer, jnp.minimum(tg[j], N_GROUPS - 1), 0, 0, 0))
    grouped = lambda w: w.reshape(w.shape[0], N_GROUPS, EXPERTS_PER_GROUP, *w.shape[2:])
    return pl.pallas_call(
        _moe_kernel,
        grid_spec=pltpu.PrefetchScalarGridSpec(
            num_scalar_prefetch=2, grid=(n_slots // tm,),
            in_specs=[pl.BlockSpec(memory_space=pl.ANY),
                      pl.BlockSpec((D_MODEL, LANES), lambda j, tg, src: (0, 0)),
                      pl.BlockSpec((1, LANES), lambda j, tg, src: (0, 0)),
                      w_spec(D_MODEL, D_EXPERT), w_spec(D_MODEL, D_EXPERT), w_spec(D_EXPERT, D_MODEL)],
            out_specs=pl.BlockSpec((tm * TOKEN_ROWS, LANES), lambda j, tg, src: (j, 0)),
            scratch_shapes=[pltpu.VMEM((2, tm * TOKEN_ROWS, LANES), f32), pltpu.SemaphoreType.DMA((2,))]),
        out_shape=jax.ShapeDtypeStruct((n_slots * TOKEN_ROWS, LANES), f32),
        compiler_params=_params("arbitrary"),
        name="moe_ffn",
    )(tile_group, slot_token, stage, w_route, b_route, grouped(wg), grouped(wu), grouped(wd))


def _ple_kernel(pos_ref, x_ref, ys_ref, p_ref, g_ref, wg_ref, wp_ref, fg_ref, o_ref, buf_ref, sems, *, final):
    slot = _gathered_tile(pos_ref, ys_ref, buf_ref, sems, x_ref.shape[0])
    x = x_ref[...] + _from_token_tiles(buf_ref.at[slot])
    gate = jax.nn.sigmoid(_dot(_rms(x, g_ref[...]).astype(bf16), wg_ref[...]))
    y = x + gate * _dot(p_ref[...].astype(bf16), wp_ref[...])
    o_ref[...] = _rms(y, fg_ref[...]) if final else y


def _ple(x2d, ys, pos, p3, g, wg, wp, fg, layer, final):
    T = x2d.shape[0]
    tm = 512
    full = lambda a: pl.BlockSpec(a.shape, lambda i, pos: (0,) * a.ndim)
    return pl.pallas_call(
        functools.partial(_ple_kernel, final=final),
        grid_spec=pltpu.PrefetchScalarGridSpec(
            num_scalar_prefetch=1, grid=(T // tm,),
            in_specs=[pl.BlockSpec((tm, D_MODEL), lambda i, pos: (i, 0)),
                      pl.BlockSpec(memory_space=pl.ANY),
                      pl.BlockSpec((None, tm, PLE_DIM), lambda i, pos: (layer, i, 0)),
                      full(g), full(wg), full(wp), full(fg)],
            out_specs=pl.BlockSpec((tm, D_MODEL), lambda i, pos: (i, 0)),
            scratch_shapes=[pltpu.VMEM((2, tm * TOKEN_ROWS, LANES), f32), pltpu.SemaphoreType.DMA((2,))]),
        out_shape=jax.ShapeDtypeStruct((T, D_MODEL), f32),
        compiler_params=_params("arbitrary"),
        name="ple",
    )(pos, x2d, ys, p3, g, wg, wp, fg)


def _alibi_slopes():
    s = 2.0 ** (-8.0 * np.arange(1, N_HEADS_TOTAL + 1) / N_HEADS_TOTAL)
    assert np.all(np.log2(s[1::2]) == np.round(np.log2(s[1::2])))
    return jnp.asarray(s[0::2], f32), jnp.asarray(s[1::2], f32)


def _selection_constants(seq):
    ncb = seq // CMP_STRIDE
    nsel = seq // SEL_BLOCK
    n_cmp = (seq - CMP_BLOCK) // CMP_STRIDE + 1
    cs = np.arange(n_cmp) * CMP_STRIDE
    bs = np.arange(nsel) * SEL_BLOCK
    ov = np.clip(np.minimum(cs[:, None] + CMP_BLOCK, bs[None, :] + SEL_BLOCK)
                 - np.maximum(cs[:, None], bs[None, :]), 0, None) / CMP_BLOCK
    assert ncb == LANES and seq <= POS_SPLIT * 256
    ovt = np.zeros((LANES, ncb), np.float32)
    ovt[:nsel, :n_cmp] = ov.T
    pos = np.arange(seq)
    kaug = np.zeros((seq, LANES), np.float32)
    kaug[:, POS_LANE0:POS_LANE0 + 2] = 1.0
    kaug[:, POS_LANE0 + 2] = pos // POS_SPLIT
    kaug[:, POS_LANE0 + 3] = pos % POS_SPLIT
    kaug[pos, SEL_LANE0 + pos // SEL_BLOCK] = 1.0
    cend = np.arange(ncb) * CMP_STRIDE + CMP_BLOCK - 1
    caug = np.zeros((ncb, LANES), np.float32)
    caug[:, POS_LANE0:POS_LANE0 + 2] = 1.0
    caug[:, POS_LANE0 + 2] = cend // POS_SPLIT
    caug[:, POS_LANE0 + 3] = cend % POS_SPLIT
    gsel = np.zeros((2, LANES, GQA_REP // 2, 3, 2, HEAD_DIM), np.float32)
    for pr in range(GQA_REP // 2):
        for j in range(3):
            for hh in range(2):
                gsel[:, 3 * (2 * pr + hh) + j, pr, j, hh, :] = 1.0
    gsel = gsel.reshape(2 * LANES, (GQA_REP // 2) * 3 * LANES)
    return jnp.asarray(ovt, bf16), jnp.asarray(kaug), jnp.asarray(caug), jnp.asarray(gsel, bf16)


def _dispatch_plan(meta, counts, n_tokens):
    i32 = jnp.int32
    n_slots = n_tokens + N_GROUPS * MOE_TILE
    group = meta[:, 0, :].reshape(n_tokens).astype(i32)
    rank = (meta[:, 1, :] * RANK_SPLIT + meta[:, 2, :]).reshape(n_tokens).astype(i32)
    cnt = counts[0, :N_GROUPS].astype(i32)
    padded = (cnt + MOE_TILE - 1) // MOE_TILE * MOE_TILE
    end = jnp.cumsum(padded)
    pos = (end - padded)[group] + rank
    slot_token = jnp.zeros((n_slots,), i32).at[pos].set(jnp.arange(n_tokens, dtype=i32))
    tile_start = jnp.arange(n_slots // MOE_TILE, dtype=i32) * MOE_TILE
    tile_group = jnp.sum(tile_start[:, None] >= end[None, :], axis=1).astype(i32)
    return pos, slot_token, tile_group


def _block_diag2(w):
    z = jnp.zeros_like(w)
    return jnp.concatenate([jnp.concatenate([w, z], axis=-1), jnp.concatenate([z, w], axis=-1)], axis=-2)


def _layout_w_in(w):
    gate = w[:, N_MAIN:]
    per = GQA_REP * 3
    blocks = [jnp.pad(gate[:, g * per:(g + 1) * per], ((0, 0), (0, LANES - per))) for g in range(N_KV_B)]
    return jnp.concatenate([w[:, :N_MAIN]] + blocks, axis=1).astype(bf16)


def kernel(x, p, attn_norm, w_in, w_out, w_cmp_k1, w_cmp_k2, w_cmp_v1, w_cmp_v2, cmp_pos, ffn_norm, w_route_group, b_route_group, w_route_expert, b_route_expert, w_expert_gate, w_expert_up, w_expert_down, ple_norm, w_ple_gate, w_ple_proj, final_norm):
    B, S, D = x.shape
    depth = w_in.shape[0]
    T = B * S
    slopes_a, slopes_b = _alibi_slopes()
    ovt, kaug, caug, gsel = _selection_constants(S)
    wg_all = w_expert_gate.astype(bf16)
    wu_all = w_expert_up.astype(bf16)
    wd_all = w_expert_down.astype(bf16)
    p3 = p.reshape(depth, T, PLE_DIM)
    row = lambda v: v.reshape(1, -1)
    n_route = N_GROUPS + N_EXPERTS

    x2d = x.reshape(T, D)
    for i in range(depth):
        proj3 = _in_proj(x2d, row(attn_norm[i]), _layout_w_in(w_in[i])).reshape(B, S, N_PROJ)
        oa = _dilated(proj3, slopes_a)
        w1 = lambda w: _block_diag2(w.reshape(CMP_BLOCK, HEAD_DIM, CMP_HIDDEN)).astype(bf16)
        pos_dup = jnp.concatenate([cmp_pos[i], cmp_pos[i]], axis=-1)
        kcmp, vcmp = _compress(proj3, pos_dup, w1(w_cmp_k1[i]), _block_diag2(w_cmp_k2[i]).astype(bf16),
                               w1(w_cmp_v1[i]), _block_diag2(w_cmp_v2[i]).astype(bf16))
        ob = _nsa(proj3, kcmp, vcmp, slopes_b, ovt, kaug, caug, gsel)
        wo = w_out[i].astype(bf16)
        w_route = jnp.pad(jnp.concatenate([w_route_group[i], w_route_expert[i]], axis=1),
                          ((0, 0), (0, LANES - n_route)))
        b_route = jnp.pad(jnp.concatenate([b_route_group[i], b_route_expert[i]]), (0, LANES - n_route))
        x2d, stage, meta, counts = _out_route(x2d, oa.reshape(T, A_W), ob.reshape(T, -1), wo[:A_W], wo[A_W:],
                                              row(ffn_norm[i]), w_route, row(b_route))
        pos, slot_token, tile_group = _dispatch_plan(meta, counts, T)
        ys = _moe(stage, tile_group, slot_token, w_route, row(b_route), wg_all, wu_all, wd_all, i)
        x2d = _ple(x2d, ys, pos, p3, row(ple_norm[i]), w_ple_gate[i].astype(bf16),
                   w_ple_proj[i].astype(bf16), row(final_norm), i, i == depth - 1)
    return x2d.reshape(B, S, D)
```

```python
import functools

import numpy as np
import jax
import jax.numpy as jnp
from jax import lax
from jax.experimental import pallas as pl
from jax.experimental.pallas import tpu as pltpu

D_MODEL = 1024
PLE_DIM = 256
HEAD_DIM = 64
N_HEADS_A = 8
N_HEADS_B = 8
N_KV_B = 2
GQA_REP = N_HEADS_B // N_KV_B
N_HEADS_TOTAL = N_HEADS_A + N_HEADS_B
DILATED_PATTERNS = ((128, 1), (512, 4), (2048, 16))
CMP_BLOCK = 32
CMP_STRIDE = 16
CMP_HIDDEN = 256
SEL_BLOCK = 64
SEL_TOP = 16
N_LOCAL_BLOCKS = 2
WINDOW = 512
N_GROUPS = 4
EXPERTS_PER_GROUP = 4
N_EXPERTS = N_GROUPS * EXPERTS_PER_GROUP
D_EXPERT = 512
RMS_EPS = 1e-6
NEG = -1e30
FORCE_BONUS = 1e4
SCALE = HEAD_DIM ** -0.5

LANES = 128
QB = 128
SEL_CHUNK = 512
NSA_UNITS = 2
DIL_INFLIGHT = 3
DIL_TRIP = 16
POS_LANE0 = HEAD_DIM
POS_SPLIT = 16
SEL_LANE0 = POS_LANE0 + 4
SEL_LANES = 32
MASK_BIG = -(2.0 ** 100)
MOE_TILE = 512
TOKEN_ROWS = D_MODEL // LANES
RANK_SPLIT = 128
A_W = N_HEADS_A * HEAD_DIM
N_MAIN = 3 * A_W + N_HEADS_B * HEAD_DIM + 6 * N_KV_B * HEAD_DIM
N_PROJ = N_MAIN + N_KV_B * LANES
COL_QB = (3 * A_W) // (2 * LANES)
COL_KC = (3 * A_W + N_HEADS_B * HEAD_DIM) // LANES
COL_GATE = N_MAIN // LANES
VMEM_LIMIT = 56 * 1024 * 1024

f32 = jnp.float32
bf16 = jnp.bfloat16


def _dot(a, b):
    return jnp.dot(a, b, preferred_element_type=f32)


def _dot_nt(a, b):
    return lax.dot_general(a, b, (((1,), (1,)), ((), ())), preferred_element_type=f32)


def _rms(x, g):
    return x * lax.rsqrt(jnp.mean(x * x, axis=-1, keepdims=True) + RMS_EPS) * g


def _params(*sem):
    return pltpu.CompilerParams(dimension_semantics=sem, vmem_limit_bytes=VMEM_LIMIT)


def _in_proj_kernel(x_ref, g_ref, w_ref, o_ref):
    h = _rms(x_ref[...], g_ref[...]).astype(bf16)
    for n0 in range(0, N_PROJ, 512):
        o_ref[:, n0:n0 + 512] = _dot(h, w_ref[:, n0:n0 + 512])


def _in_proj(x2d, g, w):
    T = x2d.shape[0]
    tm = 512
    return pl.pallas_call(
        _in_proj_kernel,
        grid=(T // tm,),
        in_specs=[pl.BlockSpec((tm, D_MODEL), lambda i: (i, 0)),
                  pl.BlockSpec((1, D_MODEL), lambda i: (0, 0)),
                  pl.BlockSpec((D_MODEL, N_PROJ), lambda i: (0, 0))],
        out_specs=pl.BlockSpec((tm, N_PROJ), lambda i: (i, 0)),
        out_shape=jax.ShapeDtypeStruct((T, N_PROJ), f32),
        compiler_params=_params("parallel"),
        name="in_proj",
    )(x2d, g, w)


def _dil_kernel(slope_ref, q_ref, k_ref, v_ref, o_ref, out_ref, lse_ref, bias_scr, *bufs, seq):
    hp = pl.program_id(1)
    lane = lax.broadcasted_iota(jnp.int32, (QB, LANES), 1)
    left = lane < HEAD_DIM
    slab = 32
    s_bufs, e_bufs = bufs[:DIL_INFLIGHT], bufs[DIL_INFLIGHT:]

    i = lax.broadcasted_iota(jnp.int32, (QB, 2 * QB), 0)
    j = lax.broadcasted_iota(jnp.int32, (QB, 2 * QB), 1)
    rel = i - j + QB
    valid = (rel >= 0) & (rel <= QB)
    relf = rel.astype(f32)
    for p, (window, dil) in enumerate(DILATED_PATTERNS):
        assert window // dil == QB
        for hh in range(2):
            bias_scr[2 * p + hh] = jnp.where(valid, (-float(dil) * slope_ref[2 * hp + hh]) * relf, NEG)

    ones = jnp.ones((2 * QB, LANES), bf16)

    def scores(u, blk):
        p, dil, row0, key0, nk = blk
        qc = q_ref[pl.ds(row0, QB, stride=dil), :] * SCALE
        q2 = jnp.concatenate([jnp.where(left, qc, 0.0), jnp.where(left, 0.0, qc)], axis=0).astype(bf16)
        s_bufs[u][:, :nk] = _dot_nt(q2, k_ref[pl.ds(key0, nk, stride=dil), :].astype(bf16))

    def probs(u, blk):
        p, dil, row0, key0, nk = blk
        c0 = 2 * QB - nk
        ms = []
        for hh in range(2):
            parts = []
            for sl in range(QB // slab):
                rows = slice(hh * QB + sl * slab, hh * QB + (sl + 1) * slab)
                tiles = [s_bufs[u][rows, c:c + LANES]
                         + bias_scr[2 * p + hh, sl * slab:(sl + 1) * slab, c0 + c:c0 + c + LANES]
                         for c in range(0, nk, LANES)]
                mx = tiles[0]
                for t in tiles[1:]:
                    mx = jnp.maximum(mx, t)
                mx = jnp.broadcast_to(jnp.max(mx, axis=1, keepdims=True), (slab, LANES))
                for c, t in zip(range(0, nk, LANES), tiles):
                    e_bufs[u][rows, c:c + LANES] = jnp.exp(t - mx).astype(bf16)
                parts.append(mx)
            ms.append(jnp.concatenate(parts, axis=0))
        return jnp.where(left, ms[0], ms[1])

    def values(u, blk, row_max):
        p, dil, row0, key0, nk = blk
        v2 = jnp.concatenate([v_ref[pl.ds(key0, nk, stride=dil), :].astype(bf16), ones[:nk]], axis=1)
        res = _dot(e_bufs[u][:, :nk], v2)
        rows = pl.ds(row0, QB, stride=dil)
        den = jnp.where(left, res[:QB, LANES:], res[QB:, LANES:])
        out_ref[p, rows, :] = jnp.where(left, res[:QB, :LANES], res[QB:, :LANES]) / den
        lse_ref[p, rows, :] = row_max + jnp.log(den)

    def run(blocks):
        ahead = DIL_INFLIGHT - 1
        for n in range(min(ahead, len(blocks))):
            scores(n % DIL_INFLIGHT, blocks[n])
        for n, blk in enumerate(blocks):
            if n + ahead < len(blocks):
                scores((n + ahead) % DIL_INFLIGHT, blocks[n + ahead])
            values(n % DIL_INFLIGHT, blk, probs(n % DIL_INFLIGHT, blk))

    def first(p, dil, r):
        return (p, dil, r, r, QB)

    def later(p, dil, r, a):
        return (p, dil, r + dil * QB * a, r + dil * QB * (a - 1), 2 * QB)

    for p, (window, dil) in enumerate(DILATED_PATTERNS):
        nblk = seq // dil // QB
        blocks = [first(p, dil, r) if a == 0 else later(p, dil, r, a) for r in range(dil) for a in range(nblk)]
        for b0 in range(0, len(blocks), DIL_TRIP):
            run(blocks[b0:b0 + DIL_TRIP])

    ch = 256

    for c in range(seq // ch):
        rows = slice(c * ch, (c + 1) * ch)
        lses = [lse_ref[p, rows, :] for p in range(len(DILATED_PATTERNS))]
        big = functools.reduce(jnp.maximum, lses)
        num = jnp.zeros((ch, LANES), f32)
        den = jnp.zeros((ch, LANES), f32)
        for p, lse in enumerate(lses):
            w = jnp.exp(lse - big)
            num = num + w * out_ref[p, rows, :]
            den = den + w
        o_ref[rows, :] = (num / den).astype(o_ref.dtype)


def _dilated(proj3, slopes_a):
    B, S, _ = proj3.shape
    npair = N_HEADS_A // 2
    blk = lambda off: pl.BlockSpec((None, S, LANES), lambda b, hp, off=off: (b, 0, off + hp))
    return pl.pallas_call(
        functools.partial(_dil_kernel, seq=S),
        grid=(B, npair),
        in_specs=[pl.BlockSpec(memory_space=pltpu.SMEM), blk(0), blk(npair), blk(2 * npair)],
        out_specs=pl.BlockSpec((None, S, LANES), lambda b, hp: (b, 0, hp)),
        out_shape=jax.ShapeDtypeStruct((B, S, A_W), bf16),
        scratch_shapes=([pltpu.VMEM((len(DILATED_PATTERNS), S, LANES), f32)] * 2
                        + [pltpu.VMEM((2 * len(DILATED_PATTERNS), QB, 2 * QB), f32)]
                        + [pltpu.VMEM((2 * QB, 2 * QB), f32)] * DIL_INFLIGHT
                        + [pltpu.VMEM((2 * QB, 2 * QB), bf16)] * DIL_INFLIGHT),
        compiler_params=_params("parallel", "parallel"),
        name="dilated_attn",
    )(slopes_a, proj3, proj3, proj3)


def _cmp_kernel(kc_ref, vc_ref, pos_ref, w1k_ref, w2k_ref, w1v_ref, w2v_ref, ko_ref, vo_ref, *, ncb):
    half = CMP_BLOCK // 2
    for x_ref, pi, w1_ref, w2_ref, o_ref in ((kc_ref, 0, w1k_ref, w2k_ref, ko_ref),
                                             (vc_ref, 1, w1v_ref, w2v_ref, vo_ref)):
        lo = jnp.zeros((ncb, 2 * CMP_HIDDEN), f32)
        hi = jnp.zeros((ncb, 2 * CMP_HIDDEN), f32)
        for r in range(half):
            x = x_ref[pl.ds(r, ncb, stride=CMP_STRIDE), :]
            lo = lo + _dot((x + pos_ref[pi, r:r + 1, :]).astype(bf16), w1_ref[r])
            hi = hi + _dot((x + pos_ref[pi, r + half:r + half + 1, :]).astype(bf16), w1_ref[r + half])
        h1 = lo + pltpu.roll(hi, ncb - 1, axis=0)
        o_ref[...] = _dot(jax.nn.gelu(h1).astype(bf16), w2_ref[...])


def _compress(proj3, pos_dup, w1k, w2k, w1v, w2v):
    B, S, _ = proj3.shape
    ncb = S // CMP_STRIDE
    full = lambda a: pl.BlockSpec(a.shape, lambda b: (0,) * a.ndim)
    out = jax.ShapeDtypeStruct((B, ncb, LANES), f32)
    return pl.pallas_call(
        functools.partial(_cmp_kernel, ncb=ncb),
        grid=(B,),
        in_specs=[pl.BlockSpec((None, S, LANES), lambda b: (b, 0, COL_KC)),
                  pl.BlockSpec((None, S, LANES), lambda b: (b, 0, COL_KC + 1)),
                  full(pos_dup), full(w1k), full(w2k), full(w1v), full(w2v)],
        out_specs=[pl.BlockSpec((None, ncb, LANES), lambda b: (b, 0, 0))] * 2,
        out_shape=[out, out],
        compiler_params=_params("parallel"),
        name="nsa_compress",
    )(proj3, proj3, pos_dup, w1k, w2k, w1v, w2v)


def _nsa_kernel(slope_ref, q_ref, ks_ref, vs_ref, kw_ref, vw_ref, gl_ref, kcmp_ref, vcmp_ref,
                ovt_ref, kaug_ref, caug_ref, gsel_ref, o_ref,
                ksa, vsa, kwa, vwa, kca, vca, q_scr, s_m, *per_unit, seq):
    g = pl.program_id(1)
    qi = pl.program_id(2)
    nsel = seq // SEL_BLOCK
    n_top = min(SEL_TOP, nsel)
    assert nsel <= SEL_LANES and N_LOCAL_BLOCKS * SEL_BLOCK >= QB and n_top > N_LOCAL_BLOCKS

    def group_lanes(x):
        return jnp.where(g == 0, x, pltpu.roll(x, HEAD_DIM, axis=1))

    @pl.when(qi == 0)
    def _prep():
        ch = 256
        lane_c = lax.broadcasted_iota(jnp.int32, (ch, LANES), 1)
        is_k = lane_c < HEAD_DIM
        for c in range(seq // ch):
            sl = slice(c * ch, (c + 1) * ch)
            aug = kaug_ref[sl, :]
            ksa[sl, :] = jnp.where(is_k, group_lanes(ks_ref[sl, :]), aug).astype(bf16)
            kwa[sl, :] = jnp.where(is_k, group_lanes(kw_ref[sl, :]),
                                   jnp.where(lane_c < SEL_LANE0, aug, 0.0)).astype(bf16)
            for src, dst in ((vs_ref, vsa), (vw_ref, vwa)):
                v = group_lanes(src[sl, :])
                dst[sl, :LANES] = jnp.where(is_k, v, 1.0).astype(bf16)
                dst[sl, LANES:] = jnp.where(is_k, 1.0, pltpu.roll(v, HEAD_DIM, axis=1)).astype(bf16)
        lane_k = lax.broadcasted_iota(jnp.int32, kca.shape, 1)
        kca[...] = jnp.where(lane_k < HEAD_DIM, group_lanes(kcmp_ref[...]), caug_ref[...]).astype(bf16)
        vc = group_lanes(vcmp_ref[...])
        vca[...] = jnp.where(lane_k < HEAD_DIM, vc, pltpu.roll(vc, HEAD_DIM, axis=1)).astype(bf16)

    per = SEL_CHUNK // QB
    n_chunks = (qi + per - 1) // per
    need_topk = 2 * qi + 2 > n_top
    refs = (slope_ref, q_ref, gl_ref, ovt_ref, gsel_ref, o_ref, ksa, vsa, kwa, vwa, kca, vca, q_scr, s_m, per_unit)
    for n in range((seq // QB - 1 + per - 1) // per + 1):
        qi_lo, qi_hi = max(per * (n - 1) + 1, 0), min(per * n, seq // QB - 1)
        for topk in sorted({2 * q + 2 > n_top for q in range(qi_lo, qi_hi + 1)}):
            pl.when((n_chunks == n) & (need_topk == topk))(
                functools.partial(_nsa_step, refs, seq=seq, n=n, topk=topk))


def _nsa_step(refs, *, seq, n, topk):
    slope_ref, q_ref, gl_ref, ovt_ref, gsel_ref, o_ref, ksa, vsa, kwa, vwa, kca, vca, q_scr, s_m, per_unit = refs
    g = pl.program_id(1)
    qi = pl.program_id(2)
    nsel = seq // SEL_BLOCK
    n_top = min(SEL_TOP, nsel)
    slab = 32
    ur = GQA_REP * QB // NSA_UNITS
    s_w, e_w, s_d, e_d, e_m, m_h, al_h, acc = [per_unit[i * NSA_UNITS:(i + 1) * NSA_UNITS] for i in range(8)]
    lane = lax.broadcasted_iota(jnp.int32, (QB, LANES), 1)
    left = lane < HEAD_DIM
    ii = lax.broadcasted_iota(jnp.int32, (QB, LANES), 0)
    t_row = qi * QB + ii
    t_hi = (t_row // POS_SPLIT).astype(f32)
    t_lo = (t_row % POS_SPLIT).astype(f32)
    slopes = [slope_ref[g * GQA_REP + r] for r in range(GQA_REP)]
    head = lambda a, r: a[r * QB:(r + 1) * QB]

    def q_head(r):
        x = q_ref[:, (r // 2) * LANES:(r // 2 + 1) * LANES] * SCALE
        if r % 2:
            x = pltpu.roll(x, HEAD_DIM, axis=1)
        m = slopes[r]
        pos = jnp.where(lane == POS_LANE0, (-POS_SPLIT * m) * t_hi,
                        jnp.where(lane == POS_LANE0 + 1, (-m) * t_lo,
                                  jnp.where(lane == POS_LANE0 + 2, POS_SPLIT * m,
                                            jnp.where(lane == POS_LANE0 + 3, m, 0.0))))
        return jnp.where(left, x, pos)

    q4 = jnp.concatenate([q_head(r) for r in range(GQA_REP)], axis=0)
    q4b = q4.astype(bf16)

    def rel_tile(nk, offset):
        i = lax.broadcasted_iota(jnp.int32, (QB, nk), 0)
        j = lax.broadcasted_iota(jnp.int32, (QB, nk), 1)
        return (i - j + offset).astype(f32)

    def softmax_rows(s_ref, e_ref, m_ref, al_ref, nk, bias, running):
        for sl in range(ur // slab):
            rows = slice(sl * slab, (sl + 1) * slab)
            cols = [slice(j * LANES, (j + 1) * LANES) for j in range(nk // LANES)]
            i0 = (sl * slab) % QB
            tiles = [s_ref[rows, c] if bias is None else s_ref[rows, c] + bias[i0:i0 + slab, c] for c in cols]
            mx = tiles[0]
            for t in tiles[1:]:
                mx = jnp.maximum(mx, t)
            m_new = jnp.broadcast_to(jnp.max(mx, axis=1, keepdims=True), (slab, LANES))
            if running:
                m_old = m_ref[rows, :]
                m_new = jnp.maximum(m_old, m_new)
                al_ref[rows, :] = jnp.exp(m_old - m_new)
            if m_ref is not None:
                m_ref[rows, :] = m_new
            for c, t in zip(cols, tiles):
                e_ref[rows, c] = jnp.exp(t - m_new).astype(bf16)

    unit = lambda a, u: a[u * ur:(u + 1) * ur]

    def own_half(pv, u):
        heads = range(u * ur // QB, (u + 1) * ur // QB)
        return jnp.concatenate([pv[(r - heads[0]) * QB:(r - heads[0] + 1) * QB, (r % 2) * LANES:(r % 2 + 1) * LANES]
                                for r in heads], axis=0)

    s4 = _dot_nt(q4b, kca[...])

    sig = jax.nn.sigmoid(gl_ref[...])
    g_hi = sig.astype(bf16)
    g_lo = (sig - g_hi.astype(f32)).astype(bf16)
    gate_b = _dot(jnp.concatenate([g_hi, g_lo], axis=1), gsel_ref[...])

    nwin = WINDOW + QB
    w0 = pl.multiple_of(jnp.maximum(qi - WINDOW // QB, 0) * QB, QB)
    dist_w = rel_tile(nwin, qi * QB - w0)
    bias_w = jnp.where((dist_w >= 0) & (dist_w < WINDOW), 0.0, NEG)
    for u in range(NSA_UNITS):
        s_w[u][...] = _dot_nt(unit(q4b, u), kwa[pl.ds(w0, nwin), :])

    d0 = pl.multiple_of(qi * QB, QB)
    bias_d = jnp.where(rel_tile(QB, 0) >= 0, 0.0, NEG)
    for u in range(NSA_UNITS):
        s_d[u][...] = _dot_nt(unit(q4b, u), ksa[pl.ds(d0, QB), :])
    s_m[0] = _dot_nt(q4b, ksa[:SEL_CHUNK, :])

    cmp_end = (lane * CMP_STRIDE + (CMP_BLOCK - 1))
    valid_c = t_row >= cmp_end
    ps = []
    p_sum = jnp.zeros((QB, LANES), f32)
    row_bcast = lambda col: jnp.broadcast_to(col, (QB, LANES))
    for r in range(GQA_REP):
        s = jnp.where(valid_c, head(s4, r), NEG)
        e = jnp.exp(s - row_bcast(jnp.max(s, axis=1, keepdims=True)))
        p = jnp.where(valid_c, e, 0.0) / row_bcast(jnp.sum(e, axis=1, keepdims=True))
        p_sum = p_sum + p
        ps.append(p.astype(bf16))
    o_cmp4 = _dot(jnp.concatenate(ps, axis=0), vca[...])

    o_w = []
    for u in range(NSA_UNITS):
        softmax_rows(s_w[u], e_w[u], None, None, nwin, bias_w, False)
        o_w.append(own_half(_dot(e_w[u][...], vwa[pl.ds(w0, nwin), :]), u))
    for u in range(NSA_UNITS):
        softmax_rows(s_d[u], e_d[u], m_h[u], None, QB, bias_d, False)
        acc[u][...] = own_half(_dot(e_d[u][...], vsa[pl.ds(d0, QB), :]), u)

    back = t_row // SEL_BLOCK - lane
    valid_s = (back >= 0) & (lane < nsel)

    def _all_valid():
        return jnp.where(valid_s, 1.0, 0.0)

    def _top_k():
        p_hi = p_sum.astype(bf16)
        p_lo = (p_sum - p_hi.astype(f32)).astype(bf16)
        imp_t = (_dot_nt(ovt_ref[...], p_hi) + _dot_nt(ovt_ref[...], p_lo))[:SEL_LANES]
        blk = lax.broadcasted_iota(jnp.int32, (SEL_LANES, QB), 0)
        tq = qi * QB + lax.broadcasted_iota(jnp.int32, (SEL_LANES, QB), 1)
        back_t = tq // SEL_BLOCK - blk
        valid_t = (back_t >= 0) & (blk < nsel)
        forced = (blk == 0) | (valid_t & (back_t < N_LOCAL_BLOCKS))
        score = jnp.where(valid_t, imp_t + jnp.where(forced, FORCE_BONUS, 0.0), NEG)
        score = jnp.where(blk < nsel, score, 2.0 * NEG)
        rank = jnp.zeros((SEL_LANES, QB), jnp.int32)
        for n in range(nsel):
            row = score[n:n + 1, :]
            ahead = (row > score) | ((row == score) & (blk > n))
            rank = rank + ahead.astype(jnp.int32)
        sel_t = jnp.where((rank < n_top) & valid_t, 1.0, 0.0)
        sel_t = jnp.concatenate([sel_t, jnp.zeros((LANES - SEL_LANES, QB), f32)], axis=0)
        return sel_t.T

    sel = _top_k() if topk else _all_valid()

    acc_w = jnp.concatenate(o_w, axis=0)

    def finish(n):
        if n > 0:
            sel_bias = jnp.where((sel > 0.5) & (lane < 2 * qi), 0.0, MASK_BIG)
            sel_bias = pltpu.roll(sel_bias, SEL_LANE0, axis=1)
            in_sel = (lane >= SEL_LANE0) & (lane < SEL_LANE0 + SEL_LANES)
            for r in range(GQA_REP):
                q_scr[r * QB:(r + 1) * QB, :] = jnp.where(in_sel, sel_bias, head(q4, r)).astype(bf16)
            bias0 = _dot_nt(jnp.where(in_sel, sel_bias, 0.0).astype(bf16), ksa[:SEL_CHUNK, :])
        for kc in range(n):
            half = kc % 2
            if kc + 1 < n:
                s_m[1 - half] = _dot_nt(q_scr[...], ksa[(kc + 1) * SEL_CHUNK:(kc + 2) * SEL_CHUNK, :])
            for u in range(NSA_UNITS):
                softmax_rows(s_m.at[half, pl.ds(u * ur, ur)], e_m[u], m_h[u], al_h[u], SEL_CHUNK,
                             bias0 if kc == 0 else None, True)
                acc[u][...] = (al_h[u][...] * acc[u][...]
                               + own_half(_dot(e_m[u][...], vsa[kc * SEL_CHUNK:(kc + 1) * SEL_CHUNK, :]), u))

        acc_s = jnp.concatenate([a[...] for a in acc], axis=0)
        for pr in range(GQA_REP // 2):
            ev, od = 2 * pr, 2 * pr + 1
            out = (gate_b[:, (3 * pr) * LANES:(3 * pr + 1) * LANES]
                   * jnp.where(left, head(o_cmp4, ev), head(o_cmp4, od)))
            for j, a in ((1, acc_s), (2, acc_w)):
                num = jnp.where(left, head(a, ev), head(a, od))
                den = pltpu.roll(jnp.where(left, head(a, od), head(a, ev)), HEAD_DIM, axis=1)
                out = out + gate_b[:, (3 * pr + j) * LANES:(3 * pr + j + 1) * LANES] / den * num
            o_ref[:, pr * LANES:(pr + 1) * LANES] = out.astype(o_ref.dtype)

    finish(n)


def _nsa(proj3, kcmp, vcmp, slopes_b, ovt, kaug, caug, gsel):
    B, S, _ = proj3.shape
    ncb = kcmp.shape[1]
    nrow = GQA_REP * QB
    ur = nrow // NSA_UNITS
    nwin = WINDOW + QB
    kv = lambda j: pl.BlockSpec((None, S, LANES), lambda b, g, qi, j=j: (b, 0, COL_KC + j))
    full = lambda a: pl.BlockSpec(a.shape, lambda b, g, qi: (0,) * a.ndim)
    return pl.pallas_call(
        functools.partial(_nsa_kernel, seq=S),
        grid=(B, N_KV_B, S // QB),
        in_specs=[pl.BlockSpec(memory_space=pltpu.SMEM),
                  pl.BlockSpec((None, QB, 2 * LANES), lambda b, g, qi: (b, qi, COL_QB + g)),
                  kv(2), kv(3), kv(4), kv(5),
                  pl.BlockSpec((None, QB, LANES), lambda b, g, qi: (b, qi, COL_GATE + g)),
                  pl.BlockSpec((None, ncb, LANES), lambda b, g, qi: (b, 0, 0)),
                  pl.BlockSpec((None, ncb, LANES), lambda b, g, qi: (b, 0, 0)),
                  full(ovt), full(kaug), full(caug), full(gsel)],
        out_specs=pl.BlockSpec((None, QB, 2 * LANES), lambda b, g, qi: (b, qi, g)),
        out_shape=jax.ShapeDtypeStruct((B, S, N_HEADS_B * HEAD_DIM), bf16),
        scratch_shapes=([pltpu.VMEM((S, LANES), bf16), pltpu.VMEM((S, 2 * LANES), bf16)] * 2
                        + [pltpu.VMEM((ncb, LANES), bf16)] * 2
                        + [pltpu.VMEM((nrow, LANES), bf16), pltpu.VMEM((2, nrow, SEL_CHUNK), f32)]
                        + [pltpu.VMEM((ur, nwin), f32)] * NSA_UNITS + [pltpu.VMEM((ur, nwin), bf16)] * NSA_UNITS
                        + [pltpu.VMEM((ur, QB), f32)] * NSA_UNITS + [pltpu.VMEM((ur, QB), bf16)] * NSA_UNITS
                        + [pltpu.VMEM((ur, SEL_CHUNK), bf16)] * NSA_UNITS
                        + [pltpu.VMEM((ur, LANES), f32)] * (2 * NSA_UNITS)
                        + [pltpu.VMEM((ur, LANES), f32)] * NSA_UNITS),
        compiler_params=_params("parallel", "parallel", "arbitrary"),
        name="nsa_attn",
    )(slopes_b, proj3, proj3, proj3, proj3, proj3, proj3, kcmp, vcmp, ovt, kaug, caug, gsel)


def _route(h, w_ref, b_ref):
    w = w_ref[...]
    h_hi, w_hi = h.astype(bf16), w.astype(bf16)
    h_lo, w_lo = (h - h_hi.astype(f32)).astype(bf16), (w - w_hi.astype(f32)).astype(bf16)
    logit = _dot(h_hi, w_hi) + (_dot(h_hi, w_lo) + _dot(h_lo, w_hi)) + b_ref[...]
    tm = logit.shape[0]
    lane = lax.broadcasted_iota(jnp.int32, (tm, LANES), 1)
    big = jnp.int32(LANES)
    is_g = lane < N_GROUPS
    gl = jnp.where(is_g, logit, NEG)
    gmax = jnp.max(gl, axis=1, keepdims=True)
    gsum = jnp.sum(jnp.where(is_g, jnp.exp(gl - gmax), 0.0), axis=1, keepdims=True)
    gsel = jnp.min(jnp.where(is_g & (gl == gmax), lane, big), axis=1, keepdims=True)
    gw = 1.0 / gsum
    e_lane = lane - N_GROUPS
    in_grp = (e_lane >= 0) & (e_lane < N_EXPERTS) & (e_lane // EXPERTS_PER_GROUP == gsel)
    el = jnp.where(in_grp, logit, NEG)
    t1 = jnp.max(el, axis=1, keepdims=True)
    i1 = jnp.min(jnp.where(in_grp & (el == t1), lane, big), axis=1, keepdims=True)
    el2 = jnp.where(lane == i1, NEG, el)
    t2 = jnp.max(el2, axis=1, keepdims=True)
    i2 = jnp.min(jnp.where(in_grp & (lane != i1) & (el2 == t2), lane, big), axis=1, keepdims=True)
    e2 = jnp.exp(t2 - t1)
    w1 = gw / (1.0 + e2)
    w2 = gw * e2 / (1.0 + e2)
    return jnp.where(lane == i1, w1, jnp.where(lane == i2, w2, 0.0)), gsel


def _to_token_tiles(ref, x):
    for s in range(D_MODEL // LANES):
        ref[pl.ds(s, x.shape[0], stride=D_MODEL // LANES), :] = x[:, s * LANES:(s + 1) * LANES]


def _from_token_tiles(ref):
    n = D_MODEL // LANES
    return jnp.concatenate([ref[pl.ds(s, ref.shape[0] // n, stride=n), :] for s in range(n)], axis=1)


def _out_route_kernel(x_ref, oa_ref, ob_ref, wa_ref, wb_ref, g_ref, w_ref, b_ref, tri_ref,
                      x1_ref, stage_ref, meta_ref, cnt_ref, cnt_scr):
    @pl.when(pl.program_id(0) == 0)
    def _zero():
        cnt_scr[...] = jnp.zeros_like(cnt_scr)

    x1 = x_ref[...] + _dot(oa_ref[...], wa_ref[...]) + _dot(ob_ref[...], wb_ref[...])
    x1_ref[...] = x1
    h = _rms(x1, g_ref[...])
    _to_token_tiles(stage_ref, h)
    _, gsel = _route(h, w_ref, b_ref)
    tm = h.shape[0]
    lane = lax.broadcasted_iota(jnp.int32, (tm, LANES), 1)
    is_g = lane < N_GROUPS

    onehot = jnp.where(is_g & (lane == gsel), 1.0, 0.0)
    before = _dot(tri_ref[...], onehot.astype(bf16)) + cnt_scr[...]
    rank = jnp.sum(onehot * before, axis=1, keepdims=True)
    cnt_scr[...] = before[tm - 1:tm, :] + onehot[tm - 1:tm, :]
    cnt_ref[...] = cnt_scr[...]
    rank_hi = jnp.floor(rank * (1.0 / RANK_SPLIT))
    cols = jnp.where(lane == 0, gsel.astype(f32), jnp.where(lane == 1, rank_hi,
                                                            jnp.where(lane == 2, rank - RANK_SPLIT * rank_hi, 0.0)))
    pick = (lax.broadcasted_iota(jnp.int32, (8, LANES), 0) == lax.broadcasted_iota(jnp.int32, (8, LANES), 1))
    meta_ref[...] = _dot_nt(jnp.where(pick, 1.0, 0.0).astype(bf16), cols.astype(bf16))


def _out_route(x2d, oa, ob, wa, wb, g, w, b):
    T = x2d.shape[0]
    tm = MOE_TILE
    assert T // RANK_SPLIT <= 256
    tri = jnp.asarray(np.tril(np.ones((tm, tm), np.float32), -1), bf16)
    row = lambda n: pl.BlockSpec((tm, n), lambda i: (i, 0))
    full = lambda a: pl.BlockSpec(a.shape, lambda i: (0,) * a.ndim)
    return pl.pallas_call(
        _out_route_kernel,
        grid=(T // tm,),
        in_specs=[row(D_MODEL), row(oa.shape[1]), row(ob.shape[1]), full(wa), full(wb),
                  full(g), full(w), full(b), full(tri)],
        out_specs=[row(D_MODEL),
                   pl.BlockSpec((tm * TOKEN_ROWS, LANES), lambda i: (i, 0)),
                   pl.BlockSpec((None, 8, tm), lambda i: (i, 0, 0)),
                   pl.BlockSpec((1, LANES), lambda i: (0, 0))],
        out_shape=[jax.ShapeDtypeStruct((T, D_MODEL), f32),
                   jax.ShapeDtypeStruct((T * TOKEN_ROWS, LANES), f32),
                   jax.ShapeDtypeStruct((T // tm, 8, tm), f32),
                   jax.ShapeDtypeStruct((1, LANES), f32)],
        scratch_shapes=[pltpu.VMEM((1, LANES), f32)],
        compiler_params=_params("arbitrary"),
        name="out_proj_router",
    )(x2d, oa, ob, wa, wb, g, w, b, tri)


def _token_rows(t, n=1):
    return pl.ds(pl.multiple_of(t * TOKEN_ROWS, TOKEN_ROWS), n * TOKEN_ROWS)


def _gathered_tile(idx_ref, src_ref, buf_ref, sems, tm):
    i = pl.program_id(0)
    n = pl.num_programs(0)

    def issue(step):
        slot = step % 2

        def one(k, _):
            pltpu.make_async_copy(src_ref.at[_token_rows(idx_ref[step * tm + k])],
                                  buf_ref.at[slot, _token_rows(k)], sems.at[slot]).start()
            return 0

        lax.fori_loop(0, tm, one, 0, unroll=8)

    @pl.when(i == 0)
    def _first():
        issue(i)

    @pl.when(i + 1 < n)
    def _next():
        issue(i + 1)

    slot = i % 2
    pltpu.make_async_copy(src_ref.at[_token_rows(0, tm)], buf_ref.at[slot], sems.at[slot]).wait()
    return slot


def _moe_kernel(tg_ref, src_ref, stage_ref, wr_ref, br_ref, wg_ref, wu_ref, wd_ref, o_ref, buf_ref, sems):
    grp = tg_ref[pl.program_id(0)]
    slot = _gathered_tile(src_ref, stage_ref, buf_ref, sems, MOE_TILE)

    @pl.when(grp < N_GROUPS)
    def _experts():
        h = _from_token_tiles(buf_ref.at[slot])
        gates, _ = _route(h, wr_ref, br_ref)
        x = h.astype(bf16)
        lane = lax.broadcasted_iota(jnp.int32, gates.shape, 1)
        y = jnp.zeros(h.shape, f32)
        for e in range(EXPERTS_PER_GROUP):
            gcol = jnp.sum(jnp.where(lane == N_GROUPS + EXPERTS_PER_GROUP * grp + e, gates, 0.0),
                           axis=1, keepdims=True)
            a = jax.nn.silu(_dot(x, wg_ref[e])) * _dot(x, wu_ref[e])
            y = y + _dot((a * gcol).astype(bf16), wd_ref[e])
        _to_token_tiles(o_ref, y)

    @pl.when(grp >= N_GROUPS)
    def _unused():
        o_ref[...] = jnp.zeros_like(o_ref)


def _moe(stage, tile_group, slot_token, w_route, b_route, wg, wu, wd, layer):
    tm = MOE_TILE
    n_slots = slot_token.shape[0]
    w_spec = lambda k, n: pl.BlockSpec((None, None, EXPERTS_PER_GROUP, k, n),
                                       lambda j, tg, src: (layer, jnp.minimum(tg[j], N_GROUPS - 1), 0, 0, 0))
    grouped = lambda w: w.reshape(w.shape[0], N_GROUPS, EXPERTS_PER_GROUP, *w.shape[2:])
    return pl.pallas_call(
        _moe_kernel,
        grid_spec=pltpu.PrefetchScalarGridSpec(
            num_scalar_prefetch=2, grid=(n_slots // tm,),
            in_specs=[pl.BlockSpec(memory_space=pl.ANY),
                      pl.BlockSpec((D_MODEL, LANES), lambda j, tg, src: (0, 0)),
                      pl.BlockSpec((1, LANES), lambda j, tg, src: (0, 0)),
                      w_spec(D_MODEL, D_EXPERT), w_spec(D_MODEL, D_EXPERT), w_spec(D_EXPERT, D_MODEL)],
            out_specs=pl.BlockSpec((tm * TOKEN_ROWS, LANES), lambda j, tg, src: (j, 0)),
            scratch_shapes=[pltpu.VMEM((2, tm * TOKEN_ROWS, LANES), f32), pltpu.SemaphoreType.DMA((2,))]),
        out_shape=jax.ShapeDtypeStruct((n_slots * TOKEN_ROWS, LANES), f32),
        compiler_params=_params("arbitrary"),
        name="moe_ffn",
    )(tile_group, slot_token, stage, w_route, b_route, grouped(wg), grouped(wu), grouped(wd))


def _ple_kernel(pos_ref, x_ref, ys_ref, p_ref, g_ref, wg_ref, wp_ref, fg_ref, o_ref, buf_ref, sems, *, final):
    slot = _gathered_tile(pos_ref, ys_ref, buf_ref, sems, x_ref.shape[0])
    x = x_ref[...] + _from_token_tiles(buf_ref.at[slot])
    gate = jax.nn.sigmoid(_dot(_rms(x, g_ref[...]).astype(bf16), wg_ref[...]))
    y = x + gate * _dot(p_ref[...].astype(bf16), wp_ref[...])
    o_ref[...] = _rms(y, fg_ref[...]) if final else y


def _ple(x2d, ys, pos, p3, g, wg, wp, fg, layer, final):
    T = x2d.shape[0]
    tm = 512
    full = lambda a: pl.BlockSpec(a.shape, lambda i, pos: (0,) * a.ndim)
    return pl.pallas_call(
        functools.partial(_ple_kernel, final=final),
        grid_spec=pltpu.PrefetchScalarGridSpec(
            num_scalar_prefetch=1, grid=(T // tm,),
            in_specs=[pl.BlockSpec((tm, D_MODEL), lambda i, pos: (i, 0)),
                      pl.BlockSpec(memory_space=pl.ANY),
                      pl.BlockSpec((None, tm, PLE_DIM), lambda i, pos: (layer, i, 0)),
                      full(g), full(wg), full(wp), full(fg)],
            out_specs=pl.BlockSpec((tm, D_MODEL), lambda i, pos: (i, 0)),
            scratch_shapes=[pltpu.VMEM((2, tm * TOKEN_ROWS, LANES), f32), pltpu.SemaphoreType.DMA((2,))]),
        out_shape=jax.ShapeDtypeStruct((T, D_MODEL), f32),
        compiler_params=_params("arbitrary"),
        name="ple",
    )(pos, x2d, ys, p3, g, wg, wp, fg)


def _alibi_slopes():
    s = 2.0 ** (-8.0 * np.arange(1, N_HEADS_TOTAL + 1) / N_HEADS_TOTAL)
    assert np.all(np.log2(s[1::2]) == np.round(np.log2(s[1::2])))
    return jnp.asarray(s[0::2], f32), jnp.asarray(s[1::2], f32)


def _selection_constants(seq):
    ncb = seq // CMP_STRIDE
    nsel = seq // SEL_BLOCK
    n_cmp = (seq - CMP_BLOCK) // CMP_STRIDE + 1
    cs = np.arange(n_cmp) * CMP_STRIDE
    bs = np.arange(nsel) * SEL_BLOCK
    ov = np.clip(np.minimum(cs[:, None] + CMP_BLOCK, bs[None, :] + SEL_BLOCK)
                 - np.maximum(cs[:, None], bs[None, :]), 0, None) / CMP_BLOCK
    assert ncb == LANES and seq <= POS_SPLIT * 256
    ovt = np.zeros((LANES, ncb), np.float32)
    ovt[:nsel, :n_cmp] = ov.T
    pos = np.arange(seq)
    kaug = np.zeros((seq, LANES), np.float32)
    kaug[:, POS_LANE0:POS_LANE0 + 2] = 1.0
    kaug[:, POS_LANE0 + 2] = pos // POS_SPLIT
    kaug[:, POS_LANE0 + 3] = pos % POS_SPLIT
    kaug[pos, SEL_LANE0 + pos // SEL_BLOCK] = 1.0
    cend = np.arange(ncb) * CMP_STRIDE + CMP_BLOCK - 1
    caug = np.zeros((ncb, LANES), np.float32)
    caug[:, POS_LANE0:POS_LANE0 + 2] = 1.0
    caug[:, POS_LANE0 + 2] = cend // POS_SPLIT
    caug[:, POS_LANE0 + 3] = cend % POS_SPLIT
    gsel = np.zeros((2, LANES, GQA_REP // 2, 3, 2, HEAD_DIM), np.float32)
    for pr in range(GQA_REP // 2):
        for j in range(3):
            for hh in range(2):
                gsel[:, 3 * (2 * pr + hh) + j, pr, j, hh, :] = 1.0
    gsel = gsel.reshape(2 * LANES, (GQA_REP // 2) * 3 * LANES)
    return jnp.asarray(ovt, bf16), jnp.asarray(kaug), jnp.asarray(caug), jnp.asarray(gsel, bf16)


def _invert_kernel(pos_ref, slot_ref):
    def clear(s, _):
        slot_ref[s] = 0
        return 0

    def place(t, _):
        slot_ref[pos_ref[t]] = t
        return 0

    lax.fori_loop(0, slot_ref.shape[0], clear, 0, unroll=8)
    lax.fori_loop(0, pos_ref.shape[0], place, 0, unroll=8)


def _invert_slots(pos, n_slots):
    smem = pl.BlockSpec(memory_space=pltpu.SMEM)
    return pl.pallas_call(
        _invert_kernel,
        in_specs=[smem],
        out_specs=smem,
        out_shape=jax.ShapeDtypeStruct((n_slots,), jnp.int32),
        name="moe_slot_index",
    )(pos)


def _dispatch_plan(meta, counts, n_tokens):
    i32 = jnp.int32
    n_slots = n_tokens + N_GROUPS * MOE_TILE
    group = meta[:, 0, :].reshape(n_tokens).astype(i32)
    rank = (meta[:, 1, :] * RANK_SPLIT + meta[:, 2, :]).reshape(n_tokens).astype(i32)
    cnt = counts[0, :N_GROUPS].astype(i32)
    padded = (cnt + MOE_TILE - 1) // MOE_TILE * MOE_TILE
    end = jnp.cumsum(padded)
    pos = (end - padded)[group] + rank
    slot_token = _invert_slots(pos, n_slots)
    tile_start = jnp.arange(n_slots // MOE_TILE, dtype=i32) * MOE_TILE
    tile_group = jnp.sum(tile_start[:, None] >= end[None, :], axis=1).astype(i32)
    return pos, slot_token, tile_group


def _block_diag2(w):
    z = jnp.zeros_like(w)
    return jnp.concatenate([jnp.concatenate([w, z], axis=-1), jnp.concatenate([z, w], axis=-1)], axis=-2)


def _layout_w_in(w):
    gate = w[:, N_MAIN:]
    per = GQA_REP * 3
    blocks = [jnp.pad(gate[:, g * per:(g + 1) * per], ((0, 0), (0, LANES - per))) for g in range(N_KV_B)]
    return jnp.concatenate([w[:, :N_MAIN]] + blocks, axis=1).astype(bf16)


def kernel(x, p, attn_norm, w_in, w_out, w_cmp_k1, w_cmp_k2, w_cmp_v1, w_cmp_v2, cmp_pos, ffn_norm, w_route_group, b_route_group, w_route_expert, b_route_expert, w_expert_gate, w_expert_up, w_expert_down, ple_norm, w_ple_gate, w_ple_proj, final_norm):
    B, S, D = x.shape
    depth = w_in.shape[0]
    T = B * S
    slopes_a, slopes_b = _alibi_slopes()
    ovt, kaug, caug, gsel = _selection_constants(S)
    wg_all = w_expert_gate.astype(bf16)
    wu_all = w_expert_up.astype(bf16)
    wd_all = w_expert_down.astype(bf16)
    p3 = p.reshape(depth, T, PLE_DIM)
    row = lambda v: v.reshape(1, -1)
    n_route = N_GROUPS + N_EXPERTS

    x2d = x.reshape(T, D)
    for i in range(depth):
        proj3 = _in_proj(x2d, row(attn_norm[i]), _layout_w_in(w_in[i])).reshape(B, S, N_PROJ)
        oa = _dilated(proj3, slopes_a)
        w1 = lambda w: _block_diag2(w.reshape(CMP_BLOCK, HEAD_DIM, CMP_HIDDEN)).astype(bf16)
        pos_dup = jnp.concatenate([cmp_pos[i], cmp_pos[i]], axis=-1)
        kcmp, vcmp = _compress(proj3, pos_dup, w1(w_cmp_k1[i]), _block_diag2(w_cmp_k2[i]).astype(bf16),
                               w1(w_cmp_v1[i]), _block_diag2(w_cmp_v2[i]).astype(bf16))
        ob = _nsa(proj3, kcmp, vcmp, slopes_b, ovt, kaug, caug, gsel)
        wo = w_out[i].astype(bf16)
        w_route = jnp.pad(jnp.concatenate([w_route_group[i], w_route_expert[i]], axis=1),
                          ((0, 0), (0, LANES - n_route)))
        b_route = jnp.pad(jnp.concatenate([b_route_group[i], b_route_expert[i]]), (0, LANES - n_route))
        x2d, stage, meta, counts = _out_route(x2d, oa.reshape(T, A_W), ob.reshape(T, -1), wo[:A_W], wo[A_W:],
                                              row(ffn_norm[i]), w_route, row(b_route))
        pos, slot_token, tile_group = _dispatch_plan(meta, counts, T)
        ys = _moe(stage, tile_group, slot_token, w_route, row(b_route), wg_all, wu_all, wd_all, i)
        x2d = _ple(x2d, ys, pos, p3, row(ple_norm[i]), w_ple_gate[i].astype(bf16),
                   w_ple_proj[i].astype(bf16), row(final_norm), i, i == depth - 1)
    return x2d.reshape(B, S, D)
```

```python
import functools

import numpy as np
import jax
import jax.numpy as jnp
from jax import lax
from jax.experimental import pallas as pl
from jax.experimental.pallas import tpu as pltpu

D_MODEL = 1024
PLE_DIM = 256
HEAD_DIM = 64
N_HEADS_A = 8
N_HEADS_B = 8
N_KV_B = 2
GQA_REP = N_HEADS_B // N_KV_B
N_HEADS_TOTAL = N_HEADS_A + N_HEADS_B
DILATED_PATTERNS = ((128, 1), (512, 4), (2048, 16))
CMP_BLOCK = 32
CMP_STRIDE = 16
CMP_HIDDEN = 256
SEL_BLOCK = 64
SEL_TOP = 16
N_LOCAL_BLOCKS = 2
WINDOW = 512
N_GROUPS = 4
EXPERTS_PER_GROUP = 4
N_EXPERTS = N_GROUPS * EXPERTS_PER_GROUP
D_EXPERT = 512
RMS_EPS = 1e-6
NEG = -1e30
FORCE_BONUS = 1e4
SCALE = HEAD_DIM ** -0.5

LANES = 128
QB = 128
SEL_CHUNK = 512
NSA_UNITS = 2
NSA_QBLOCKS = 2
DIL_INFLIGHT = 3
DIL_TRIP = 16
POS_LANE0 = HEAD_DIM
POS_SPLIT = 16
SEL_LANE0 = POS_LANE0 + 4
SEL_LANES = 32
MASK_BIG = -(2.0 ** 100)
MOE_TILE = 512
TOKEN_ROWS = D_MODEL // LANES
RANK_SPLIT = 128
A_W = N_HEADS_A * HEAD_DIM
N_MAIN = 3 * A_W + N_HEADS_B * HEAD_DIM + 6 * N_KV_B * HEAD_DIM
N_PROJ = N_MAIN + N_KV_B * LANES
COL_QB = (3 * A_W) // (2 * LANES)
COL_KC = (3 * A_W + N_HEADS_B * HEAD_DIM) // LANES
COL_GATE = N_MAIN // LANES
VMEM_LIMIT = 56 * 1024 * 1024

f32 = jnp.float32
bf16 = jnp.bfloat16


def _dot(a, b):
    return jnp.dot(a, b, preferred_element_type=f32)


def _dot_nt(a, b):
    return lax.dot_general(a, b, (((1,), (1,)), ((), ())), preferred_element_type=f32)


def _rms(x, g):
    return x * lax.rsqrt(jnp.mean(x * x, axis=-1, keepdims=True) + RMS_EPS) * g


def _params(*sem):
    return pltpu.CompilerParams(dimension_semantics=sem, vmem_limit_bytes=VMEM_LIMIT)


def _in_proj_kernel(x_ref, g_ref, w_ref, o_ref):
    h = _rms(x_ref[...], g_ref[...]).astype(bf16)
    for n0 in range(0, N_PROJ, 512):
        o_ref[:, n0:n0 + 512] = _dot(h, w_ref[:, n0:n0 + 512])


def _in_proj(x2d, g, w):
    T = x2d.shape[0]
    tm = 512
    return pl.pallas_call(
        _in_proj_kernel,
        grid=(T // tm,),
        in_specs=[pl.BlockSpec((tm, D_MODEL), lambda i: (i, 0)),
                  pl.BlockSpec((1, D_MODEL), lambda i: (0, 0)),
                  pl.BlockSpec((D_MODEL, N_PROJ), lambda i: (0, 0))],
        out_specs=pl.BlockSpec((tm, N_PROJ), lambda i: (i, 0)),
        out_shape=jax.ShapeDtypeStruct((T, N_PROJ), f32),
        compiler_params=_params("parallel"),
        name="in_proj",
    )(x2d, g, w)


def _dil_kernel(slope_ref, q_ref, k_ref, v_ref, o_ref, out_ref, lse_ref, bias_scr, *bufs, seq):
    hp = pl.program_id(1)
    lane = lax.broadcasted_iota(jnp.int32, (QB, LANES), 1)
    left = lane < HEAD_DIM
    slab = 32
    s_bufs, e_bufs = bufs[:DIL_INFLIGHT], bufs[DIL_INFLIGHT:]

    i = lax.broadcasted_iota(jnp.int32, (QB, 2 * QB), 0)
    j = lax.broadcasted_iota(jnp.int32, (QB, 2 * QB), 1)
    rel = i - j + QB
    valid = (rel >= 0) & (rel <= QB)
    relf = rel.astype(f32)
    for p, (window, dil) in enumerate(DILATED_PATTERNS):
        assert window // dil == QB
        for hh in range(2):
            bias_scr[2 * p + hh] = jnp.where(valid, (-float(dil) * slope_ref[2 * hp + hh]) * relf, NEG)

    ones = jnp.ones((2 * QB, LANES), bf16)

    def scores(u, blk):
        p, dil, row0, key0, nk = blk
        qc = q_ref[pl.ds(row0, QB, stride=dil), :] * SCALE
        q2 = jnp.concatenate([jnp.where(left, qc, 0.0), jnp.where(left, 0.0, qc)], axis=0).astype(bf16)
        s_bufs[u][:, :nk] = _dot_nt(q2, k_ref[pl.ds(key0, nk, stride=dil), :].astype(bf16))

    def probs(u, blk):
        p, dil, row0, key0, nk = blk
        c0 = 2 * QB - nk
        ms = []
        for hh in range(2):
            parts = []
            for sl in range(QB // slab):
                rows = slice(hh * QB + sl * slab, hh * QB + (sl + 1) * slab)
                tiles = [s_bufs[u][rows, c:c + LANES]
                         + bias_scr[2 * p + hh, sl * slab:(sl + 1) * slab, c0 + c:c0 + c + LANES]
                         for c in range(0, nk, LANES)]
                mx = tiles[0]
                for t in tiles[1:]:
                    mx = jnp.maximum(mx, t)
                mx = jnp.broadcast_to(jnp.max(mx, axis=1, keepdims=True), (slab, LANES))
                for c, t in zip(range(0, nk, LANES), tiles):
                    e_bufs[u][rows, c:c + LANES] = jnp.exp(t - mx).astype(bf16)
                parts.append(mx)
            ms.append(jnp.concatenate(parts, axis=0))
        return jnp.where(left, ms[0], ms[1])

    def values(u, blk, row_max):
        p, dil, row0, key0, nk = blk
        v2 = jnp.concatenate([v_ref[pl.ds(key0, nk, stride=dil), :].astype(bf16), ones[:nk]], axis=1)
        res = _dot(e_bufs[u][:, :nk], v2)
        rows = pl.ds(row0, QB, stride=dil)
        den = jnp.where(left, res[:QB, LANES:], res[QB:, LANES:])
        out_ref[p, rows, :] = jnp.where(left, res[:QB, :LANES], res[QB:, :LANES]) / den
        lse_ref[p, rows, :] = row_max + jnp.log(den)

    def run(blocks):
        ahead = DIL_INFLIGHT - 1
        for n in range(min(ahead, len(blocks))):
            scores(n % DIL_INFLIGHT, blocks[n])
        for n, blk in enumerate(blocks):
            if n + ahead < len(blocks):
                scores((n + ahead) % DIL_INFLIGHT, blocks[n + ahead])
            values(n % DIL_INFLIGHT, blk, probs(n % DIL_INFLIGHT, blk))

    def first(p, dil, r):
        return (p, dil, r, r, QB)

    def later(p, dil, r, a):
        return (p, dil, r + dil * QB * a, r + dil * QB * (a - 1), 2 * QB)

    for p, (window, dil) in enumerate(DILATED_PATTERNS):
        nblk = seq // dil // QB
        blocks = [first(p, dil, r) if a == 0 else later(p, dil, r, a) for r in range(dil) for a in range(nblk)]
        for b0 in range(0, len(blocks), DIL_TRIP):
            run(blocks[b0:b0 + DIL_TRIP])

    ch = 256

    for c in range(seq // ch):
        rows = slice(c * ch, (c + 1) * ch)
        lses = [lse_ref[p, rows, :] for p in range(len(DILATED_PATTERNS))]
        big = functools.reduce(jnp.maximum, lses)
        num = jnp.zeros((ch, LANES), f32)
        den = jnp.zeros((ch, LANES), f32)
        for p, lse in enumerate(lses):
            w = jnp.exp(lse - big)
            num = num + w * out_ref[p, rows, :]
            den = den + w
        o_ref[rows, :] = (num / den).astype(o_ref.dtype)


def _dilated(proj3, slopes_a):
    B, S, _ = proj3.shape
    npair = N_HEADS_A // 2
    blk = lambda off: pl.BlockSpec((None, S, LANES), lambda b, hp, off=off: (b, 0, off + hp))
    return pl.pallas_call(
        functools.partial(_dil_kernel, seq=S),
        grid=(B, npair),
        in_specs=[pl.BlockSpec(memory_space=pltpu.SMEM), blk(0), blk(npair), blk(2 * npair)],
        out_specs=pl.BlockSpec((None, S, LANES), lambda b, hp: (b, 0, hp)),
        out_shape=jax.ShapeDtypeStruct((B, S, A_W), bf16),
        scratch_shapes=([pltpu.VMEM((len(DILATED_PATTERNS), S, LANES), f32)] * 2
                        + [pltpu.VMEM((2 * len(DILATED_PATTERNS), QB, 2 * QB), f32)]
                        + [pltpu.VMEM((2 * QB, 2 * QB), f32)] * DIL_INFLIGHT
                        + [pltpu.VMEM((2 * QB, 2 * QB), bf16)] * DIL_INFLIGHT),
        compiler_params=_params("parallel", "parallel"),
        name="dilated_attn",
    )(slopes_a, proj3, proj3, proj3)


def _cmp_kernel(kc_ref, vc_ref, pos_ref, w1k_ref, w2k_ref, w1v_ref, w2v_ref, ko_ref, vo_ref, *, ncb):
    half = CMP_BLOCK // 2
    for x_ref, pi, w1_ref, w2_ref, o_ref in ((kc_ref, 0, w1k_ref, w2k_ref, ko_ref),
                                             (vc_ref, 1, w1v_ref, w2v_ref, vo_ref)):
        lo = jnp.zeros((ncb, 2 * CMP_HIDDEN), f32)
        hi = jnp.zeros((ncb, 2 * CMP_HIDDEN), f32)
        for r in range(half):
            x = x_ref[pl.ds(r, ncb, stride=CMP_STRIDE), :]
            lo = lo + _dot((x + pos_ref[pi, r:r + 1, :]).astype(bf16), w1_ref[r])
            hi = hi + _dot((x + pos_ref[pi, r + half:r + half + 1, :]).astype(bf16), w1_ref[r + half])
        h1 = lo + pltpu.roll(hi, ncb - 1, axis=0)
        o_ref[...] = _dot(jax.nn.gelu(h1).astype(bf16), w2_ref[...])


def _compress(proj3, pos_dup, w1k, w2k, w1v, w2v):
    B, S, _ = proj3.shape
    ncb = S // CMP_STRIDE
    full = lambda a: pl.BlockSpec(a.shape, lambda b: (0,) * a.ndim)
    out = jax.ShapeDtypeStruct((B, ncb, LANES), f32)
    return pl.pallas_call(
        functools.partial(_cmp_kernel, ncb=ncb),
        grid=(B,),
        in_specs=[pl.BlockSpec((None, S, LANES), lambda b: (b, 0, COL_KC)),
                  pl.BlockSpec((None, S, LANES), lambda b: (b, 0, COL_KC + 1)),
                  full(pos_dup), full(w1k), full(w2k), full(w1v), full(w2v)],
        out_specs=[pl.BlockSpec((None, ncb, LANES), lambda b: (b, 0, 0))] * 2,
        out_shape=[out, out],
        compiler_params=_params("parallel"),
        name="nsa_compress",
    )(proj3, proj3, pos_dup, w1k, w2k, w1v, w2v)


def _nsa_kernel(slope_ref, q_ref, ks_ref, vs_ref, kw_ref, vw_ref, gl_ref, kcmp_ref, vcmp_ref,
                ovt_ref, kaug_ref, caug_ref, gsel_ref, o_ref,
                ksa, vsa, kwa, vwa, kca, vca, q_scr, s_m, *per_unit, seq):
    g = pl.program_id(1)
    nsel = seq // SEL_BLOCK
    n_top = min(SEL_TOP, nsel)
    assert nsel <= SEL_LANES and N_LOCAL_BLOCKS * SEL_BLOCK >= QB and n_top > N_LOCAL_BLOCKS

    def group_lanes(x):
        return jnp.where(g == 0, x, pltpu.roll(x, HEAD_DIM, axis=1))

    @pl.when(pl.program_id(2) == 0)
    def _prep():
        ch = 256
        lane_c = lax.broadcasted_iota(jnp.int32, (ch, LANES), 1)
        is_k = lane_c < HEAD_DIM
        for c in range(seq // ch):
            sl = slice(c * ch, (c + 1) * ch)
            aug = kaug_ref[sl, :]
            ksa[sl, :] = jnp.where(is_k, group_lanes(ks_ref[sl, :]), aug).astype(bf16)
            kwa[sl, :] = jnp.where(is_k, group_lanes(kw_ref[sl, :]),
                                   jnp.where(lane_c < SEL_LANE0, aug, 0.0)).astype(bf16)
            for src, dst in ((vs_ref, vsa), (vw_ref, vwa)):
                v = group_lanes(src[sl, :])
                dst[sl, :LANES] = jnp.where(is_k, v, 1.0).astype(bf16)
                dst[sl, LANES:] = jnp.where(is_k, 1.0, pltpu.roll(v, HEAD_DIM, axis=1)).astype(bf16)
        lane_k = lax.broadcasted_iota(jnp.int32, kca.shape, 1)
        kca[...] = jnp.where(lane_k < HEAD_DIM, group_lanes(kcmp_ref[...]), caug_ref[...]).astype(bf16)
        vc = group_lanes(vcmp_ref[...])
        vca[...] = jnp.where(lane_k < HEAD_DIM, vc, pltpu.roll(vc, HEAD_DIM, axis=1)).astype(bf16)

    per = SEL_CHUNK // QB
    refs = (slope_ref, q_ref, gl_ref, ovt_ref, gsel_ref, o_ref, ksa, vsa, kwa, vwa, kca, vca, q_scr, s_m, per_unit)
    for h in range(NSA_QBLOCKS):
        qi = NSA_QBLOCKS * pl.program_id(2) + h
        n_chunks = (qi + per - 1) // per
        need_topk = 2 * qi + 2 > n_top
        for n in range((seq // QB - 1 + per - 1) // per + 1):
            blocks = [q for q in range(max(per * (n - 1) + 1, 0), min(per * n, seq // QB - 1) + 1)
                      if q % NSA_QBLOCKS == h]
            for topk in sorted({2 * q + 2 > n_top for q in blocks}):
                pl.when((n_chunks == n) & (need_topk == topk))(
                    functools.partial(_nsa_step, refs, qi, h * QB, seq=seq, n=n, topk=topk))


def _nsa_step(refs, qi, row0, *, seq, n, topk):
    slope_ref, q_ref, gl_ref, ovt_ref, gsel_ref, o_ref, ksa, vsa, kwa, vwa, kca, vca, q_scr, s_m, per_unit = refs
    g = pl.program_id(1)
    nsel = seq // SEL_BLOCK
    n_top = min(SEL_TOP, nsel)
    slab = 32
    ur = GQA_REP * QB // NSA_UNITS
    s_w, e_w, s_d, e_d, e_m, m_h, al_h, acc = [per_unit[i * NSA_UNITS:(i + 1) * NSA_UNITS] for i in range(8)]
    lane = lax.broadcasted_iota(jnp.int32, (QB, LANES), 1)
    left = lane < HEAD_DIM
    ii = lax.broadcasted_iota(jnp.int32, (QB, LANES), 0)
    t_row = qi * QB + ii
    t_hi = (t_row // POS_SPLIT).astype(f32)
    t_lo = (t_row % POS_SPLIT).astype(f32)
    slopes = [slope_ref[g * GQA_REP + r] for r in range(GQA_REP)]
    head = lambda a, r: a[r * QB:(r + 1) * QB]

    def q_head(r):
        x = q_ref[row0:row0 + QB, (r // 2) * LANES:(r // 2 + 1) * LANES] * SCALE
        if r % 2:
            x = pltpu.roll(x, HEAD_DIM, axis=1)
        m = slopes[r]
        pos = jnp.where(lane == POS_LANE0, (-POS_SPLIT * m) * t_hi,
                        jnp.where(lane == POS_LANE0 + 1, (-m) * t_lo,
                                  jnp.where(lane == POS_LANE0 + 2, POS_SPLIT * m,
                                            jnp.where(lane == POS_LANE0 + 3, m, 0.0))))
        return jnp.where(left, x, pos)

    q4 = jnp.concatenate([q_head(r) for r in range(GQA_REP)], axis=0)
    q4b = q4.astype(bf16)

    def rel_tile(nk, offset):
        i = lax.broadcasted_iota(jnp.int32, (QB, nk), 0)
        j = lax.broadcasted_iota(jnp.int32, (QB, nk), 1)
        return (i - j + offset).astype(f32)

    def softmax_rows(s_ref, e_ref, m_ref, al_ref, nk, bias, running):
        for sl in range(ur // slab):
            rows = slice(sl * slab, (sl + 1) * slab)
            cols = [slice(j * LANES, (j + 1) * LANES) for j in range(nk // LANES)]
            i0 = (sl * slab) % QB
            tiles = [s_ref[rows, c] if bias is None else s_ref[rows, c] + bias[i0:i0 + slab, c] for c in cols]
            mx = tiles[0]
            for t in tiles[1:]:
                mx = jnp.maximum(mx, t)
            m_new = jnp.broadcast_to(jnp.max(mx, axis=1, keepdims=True), (slab, LANES))
            if running:
                m_old = m_ref[rows, :]
                m_new = jnp.maximum(m_old, m_new)
                al_ref[rows, :] = jnp.exp(m_old - m_new)
            if m_ref is not None:
                m_ref[rows, :] = m_new
            for c, t in zip(cols, tiles):
                e_ref[rows, c] = jnp.exp(t - m_new).astype(bf16)

    unit = lambda a, u: a[u * ur:(u + 1) * ur]

    def own_half(pv, u):
        heads = range(u * ur // QB, (u + 1) * ur // QB)
        return jnp.concatenate([pv[(r - heads[0]) * QB:(r - heads[0] + 1) * QB, (r % 2) * LANES:(r % 2 + 1) * LANES]
                                for r in heads], axis=0)

    s4 = _dot_nt(q4b, kca[...])

    sig = jax.nn.sigmoid(gl_ref[row0:row0 + QB, :])
    g_hi = sig.astype(bf16)
    g_lo = (sig - g_hi.astype(f32)).astype(bf16)
    gate_b = _dot(jnp.concatenate([g_hi, g_lo], axis=1), gsel_ref[...])

    nwin = WINDOW + QB
    w0 = pl.multiple_of(jnp.maximum(qi - WINDOW // QB, 0) * QB, QB)
    dist_w = rel_tile(nwin, qi * QB - w0)
    bias_w = jnp.where((dist_w >= 0) & (dist_w < WINDOW), 0.0, NEG)
    for u in range(NSA_UNITS):
        s_w[u][...] = _dot_nt(unit(q4b, u), kwa[pl.ds(w0, nwin), :])

    d0 = pl.multiple_of(qi * QB, QB)
    bias_d = jnp.where(rel_tile(QB, 0) >= 0, 0.0, NEG)
    for u in range(NSA_UNITS):
        s_d[u][...] = _dot_nt(unit(q4b, u), ksa[pl.ds(d0, QB), :])
    s_m[0] = _dot_nt(q4b, ksa[:SEL_CHUNK, :])

    cmp_end = (lane * CMP_STRIDE + (CMP_BLOCK - 1))
    valid_c = t_row >= cmp_end
    ps = []
    p_sum = jnp.zeros((QB, LANES), f32)
    row_bcast = lambda col: jnp.broadcast_to(col, (QB, LANES))
    for r in range(GQA_REP):
        s = jnp.where(valid_c, head(s4, r), NEG)
        e = jnp.exp(s - row_bcast(jnp.max(s, axis=1, keepdims=True)))
        p = jnp.where(valid_c, e, 0.0) / row_bcast(jnp.sum(e, axis=1, keepdims=True))
        p_sum = p_sum + p
        ps.append(p.astype(bf16))
    o_cmp4 = _dot(jnp.concatenate(ps, axis=0), vca[...])

    o_w = []
    for u in range(NSA_UNITS):
        softmax_rows(s_w[u], e_w[u], None, None, nwin, bias_w, False)
        o_w.append(own_half(_dot(e_w[u][...], vwa[pl.ds(w0, nwin), :]), u))
    for u in range(NSA_UNITS):
        softmax_rows(s_d[u], e_d[u], m_h[u], None, QB, bias_d, False)
        acc[u][...] = own_half(_dot(e_d[u][...], vsa[pl.ds(d0, QB), :]), u)

    back = t_row // SEL_BLOCK - lane
    valid_s = (back >= 0) & (lane < nsel)

    def _all_valid():
        return jnp.where(valid_s, 1.0, 0.0)

    def _top_k():
        p_hi = p_sum.astype(bf16)
        p_lo = (p_sum - p_hi.astype(f32)).astype(bf16)
        imp_t = (_dot_nt(ovt_ref[...], p_hi) + _dot_nt(ovt_ref[...], p_lo))[:SEL_LANES]
        blk = lax.broadcasted_iota(jnp.int32, (SEL_LANES, QB), 0)
        tq = qi * QB + lax.broadcasted_iota(jnp.int32, (SEL_LANES, QB), 1)
        back_t = tq // SEL_BLOCK - blk
        valid_t = (back_t >= 0) & (blk < nsel)
        forced = (blk == 0) | (valid_t & (back_t < N_LOCAL_BLOCKS))
        score = jnp.where(valid_t, imp_t + jnp.where(forced, FORCE_BONUS, 0.0), NEG)
        score = jnp.where(blk < nsel, score, 2.0 * NEG)
        rank = jnp.zeros((SEL_LANES, QB), jnp.int32)
        for n in range(nsel):
            row = score[n:n + 1, :]
            ahead = (row > score) | ((row == score) & (blk > n))
            rank = rank + ahead.astype(jnp.int32)
        sel_t = jnp.where((rank < n_top) & valid_t, 1.0, 0.0)
        sel_t = jnp.concatenate([sel_t, jnp.zeros((LANES - SEL_LANES, QB), f32)], axis=0)
        return sel_t.T

    sel = _top_k() if topk else _all_valid()

    acc_w = jnp.concatenate(o_w, axis=0)

    def finish(n):
        if n > 0:
            sel_bias = jnp.where((sel > 0.5) & (lane < 2 * qi), 0.0, MASK_BIG)
            sel_bias = pltpu.roll(sel_bias, SEL_LANE0, axis=1)
            in_sel = (lane >= SEL_LANE0) & (lane < SEL_LANE0 + SEL_LANES)
            for r in range(GQA_REP):
                q_scr[r * QB:(r + 1) * QB, :] = jnp.where(in_sel, sel_bias, head(q4, r)).astype(bf16)
            bias0 = _dot_nt(jnp.where(in_sel, sel_bias, 0.0).astype(bf16), ksa[:SEL_CHUNK, :])
        for kc in range(n):
            half = kc % 2
            if kc + 1 < n:
                s_m[1 - half] = _dot_nt(q_scr[...], ksa[(kc + 1) * SEL_CHUNK:(kc + 2) * SEL_CHUNK, :])
            for u in range(NSA_UNITS):
                softmax_rows(s_m.at[half, pl.ds(u * ur, ur)], e_m[u], m_h[u], al_h[u], SEL_CHUNK,
                             bias0 if kc == 0 else None, True)
                acc[u][...] = (al_h[u][...] * acc[u][...]
                               + own_half(_dot(e_m[u][...], vsa[kc * SEL_CHUNK:(kc + 1) * SEL_CHUNK, :]), u))

        acc_s = jnp.concatenate([a[...] for a in acc], axis=0)
        for pr in range(GQA_REP // 2):
            ev, od = 2 * pr, 2 * pr + 1
            out = (gate_b[:, (3 * pr) * LANES:(3 * pr + 1) * LANES]
                   * jnp.where(left, head(o_cmp4, ev), head(o_cmp4, od)))
            for j, a in ((1, acc_s), (2, acc_w)):
                num = jnp.where(left, head(a, ev), head(a, od))
                den = pltpu.roll(jnp.where(left, head(a, od), head(a, ev)), HEAD_DIM, axis=1)
                out = out + gate_b[:, (3 * pr + j) * LANES:(3 * pr + j + 1) * LANES] / den * num
            o_ref[row0:row0 + QB, pr * LANES:(pr + 1) * LANES] = out.astype(o_ref.dtype)

    finish(n)


def _nsa(proj3, kcmp, vcmp, slopes_b, ovt, kaug, caug, gsel):
    B, S, _ = proj3.shape
    ncb = kcmp.shape[1]
    nrow = GQA_REP * QB
    ur = nrow // NSA_UNITS
    nwin = WINDOW + QB
    kv = lambda j: pl.BlockSpec((None, S, LANES), lambda b, g, qi, j=j: (b, 0, COL_KC + j))
    full = lambda a: pl.BlockSpec(a.shape, lambda b, g, qi: (0,) * a.ndim)
    return pl.pallas_call(
        functools.partial(_nsa_kernel, seq=S),
        grid=(B, N_KV_B, S // (NSA_QBLOCKS * QB)),
        in_specs=[pl.BlockSpec(memory_space=pltpu.SMEM),
                  pl.BlockSpec((None, NSA_QBLOCKS * QB, 2 * LANES), lambda b, g, qi: (b, qi, COL_QB + g)),
                  kv(2), kv(3), kv(4), kv(5),
                  pl.BlockSpec((None, NSA_QBLOCKS * QB, LANES), lambda b, g, qi: (b, qi, COL_GATE + g)),
                  pl.BlockSpec((None, ncb, LANES), lambda b, g, qi: (b, 0, 0)),
                  pl.BlockSpec((None, ncb, LANES), lambda b, g, qi: (b, 0, 0)),
                  full(ovt), full(kaug), full(caug), full(gsel)],
        out_specs=pl.BlockSpec((None, NSA_QBLOCKS * QB, 2 * LANES), lambda b, g, qi: (b, qi, g)),
        out_shape=jax.ShapeDtypeStruct((B, S, N_HEADS_B * HEAD_DIM), bf16),
        scratch_shapes=([pltpu.VMEM((S, LANES), bf16), pltpu.VMEM((S, 2 * LANES), bf16)] * 2
                        + [pltpu.VMEM((ncb, LANES), bf16)] * 2
                        + [pltpu.VMEM((nrow, LANES), bf16), pltpu.VMEM((2, nrow, SEL_CHUNK), f32)]
                        + [pltpu.VMEM((ur, nwin), f32)] * NSA_UNITS + [pltpu.VMEM((ur, nwin), bf16)] * NSA_UNITS
                        + [pltpu.VMEM((ur, QB), f32)] * NSA_UNITS + [pltpu.VMEM((ur, QB), bf16)] * NSA_UNITS
                        + [pltpu.VMEM((ur, SEL_CHUNK), bf16)] * NSA_UNITS
                        + [pltpu.VMEM((ur, LANES), f32)] * (2 * NSA_UNITS)
                        + [pltpu.VMEM((ur, LANES), f32)] * NSA_UNITS),
        compiler_params=_params("parallel", "parallel", "arbitrary"),
        name="nsa_attn",
    )(slopes_b, proj3, proj3, proj3, proj3, proj3, proj3, kcmp, vcmp, ovt, kaug, caug, gsel)


def _route(h, w_ref, b_ref):
    w = w_ref[...]
    h_hi, w_hi = h.astype(bf16), w.astype(bf16)
    h_lo, w_lo = (h - h_hi.astype(f32)).astype(bf16), (w - w_hi.astype(f32)).astype(bf16)
    logit = _dot(h_hi, w_hi) + (_dot(h_hi, w_lo) + _dot(h_lo, w_hi)) + b_ref[...]
    tm = logit.shape[0]
    lane = lax.broadcasted_iota(jnp.int32, (tm, LANES), 1)
    big = jnp.int32(LANES)
    is_g = lane < N_GROUPS
    gl = jnp.where(is_g, logit, NEG)
    gmax = jnp.max(gl, axis=1, keepdims=True)
    gsum = jnp.sum(jnp.where(is_g, jnp.exp(gl - gmax), 0.0), axis=1, keepdims=True)
    gsel = jnp.min(jnp.where(is_g & (gl == gmax), lane, big), axis=1, keepdims=True)
    gw = 1.0 / gsum
    e_lane = lane - N_GROUPS
    in_grp = (e_lane >= 0) & (e_lane < N_EXPERTS) & (e_lane // EXPERTS_PER_GROUP == gsel)
    el = jnp.where(in_grp, logit, NEG)
    t1 = jnp.max(el, axis=1, keepdims=True)
    i1 = jnp.min(jnp.where(in_grp & (el == t1), lane, big), axis=1, keepdims=True)
    el2 = jnp.where(lane == i1, NEG, el)
    t2 = jnp.max(el2, axis=1, keepdims=True)
    i2 = jnp.min(jnp.where(in_grp & (lane != i1) & (el2 == t2), lane, big), axis=1, keepdims=True)
    e2 = jnp.exp(t2 - t1)
    w1 = gw / (1.0 + e2)
    w2 = gw * e2 / (1.0 + e2)
    return jnp.where(lane == i1, w1, jnp.where(lane == i2, w2, 0.0)), gsel


def _to_token_tiles(ref, x):
    for s in range(D_MODEL // LANES):
        ref[pl.ds(s, x.shape[0], stride=D_MODEL // LANES), :] = x[:, s * LANES:(s + 1) * LANES]


def _from_token_tiles(ref):
    n = D_MODEL // LANES
    return jnp.concatenate([ref[pl.ds(s, ref.shape[0] // n, stride=n), :] for s in range(n)], axis=1)


def _out_route_kernel(x_ref, oa_ref, ob_ref, wa_ref, wb_ref, g_ref, w_ref, b_ref, tri_ref,
                      x1_ref, stage_ref, meta_ref, cnt_ref, cnt_scr):
    @pl.when(pl.program_id(0) == 0)
    def _zero():
        cnt_scr[...] = jnp.zeros_like(cnt_scr)

    x1 = x_ref[...] + _dot(oa_ref[...], wa_ref[...]) + _dot(ob_ref[...], wb_ref[...])
    x1_ref[...] = x1
    h = _rms(x1, g_ref[...])
    _to_token_tiles(stage_ref, h)
    _, gsel = _route(h, w_ref, b_ref)
    tm = h.shape[0]
    lane = lax.broadcasted_iota(jnp.int32, (tm, LANES), 1)
    is_g = lane < N_GROUPS

    onehot = jnp.where(is_g & (lane == gsel), 1.0, 0.0)
    before = _dot(tri_ref[...], onehot.astype(bf16)) + cnt_scr[...]
    rank = jnp.sum(onehot * before, axis=1, keepdims=True)
    cnt_scr[...] = before[tm - 1:tm, :] + onehot[tm - 1:tm, :]
    cnt_ref[...] = cnt_scr[...]
    rank_hi = jnp.floor(rank * (1.0 / RANK_SPLIT))
    cols = jnp.where(lane == 0, gsel.astype(f32), jnp.where(lane == 1, rank_hi,
                                                            jnp.where(lane == 2, rank - RANK_SPLIT * rank_hi, 0.0)))
    pick = (lax.broadcasted_iota(jnp.int32, (8, LANES), 0) == lax.broadcasted_iota(jnp.int32, (8, LANES), 1))
    meta_ref[...] = _dot_nt(jnp.where(pick, 1.0, 0.0).astype(bf16), cols.astype(bf16))


def _out_route(x2d, oa, ob, wa, wb, g, w, b):
    T = x2d.shape[0]
    tm = MOE_TILE
    assert T // RANK_SPLIT <= 256
    tri = jnp.asarray(np.tril(np.ones((tm, tm), np.float32), -1), bf16)
    row = lambda n: pl.BlockSpec((tm, n), lambda i: (i, 0))
    full = lambda a: pl.BlockSpec(a.shape, lambda i: (0,) * a.ndim)
    return pl.pallas_call(
        _out_route_kernel,
        grid=(T // tm,),
        in_specs=[row(D_MODEL), row(oa.shape[1]), row(ob.shape[1]), full(wa), full(wb),
                  full(g), full(w), full(b), full(tri)],
        out_specs=[row(D_MODEL),
                   pl.BlockSpec((tm * TOKEN_ROWS, LANES), lambda i: (i, 0)),
                   pl.BlockSpec((None, 8, tm), lambda i: (i, 0, 0)),
                   pl.BlockSpec((1, LANES), lambda i: (0, 0))],
        out_shape=[jax.ShapeDtypeStruct((T, D_MODEL), f32),
                   jax.ShapeDtypeStruct((T * TOKEN_ROWS, LANES), f32),
                   jax.ShapeDtypeStruct((T // tm, 8, tm), f32),
                   jax.ShapeDtypeStruct((1, LANES), f32)],
        scratch_shapes=[pltpu.VMEM((1, LANES), f32)],
        compiler_params=_params("arbitrary"),
        name="out_proj_router",
    )(x2d, oa, ob, wa, wb, g, w, b, tri)


def _token_rows(t, n=1):
    return pl.ds(pl.multiple_of(t * TOKEN_ROWS, TOKEN_ROWS), n * TOKEN_ROWS)


def _gathered_tile(idx_ref, src_ref, buf_ref, sems, tm):
    i = pl.program_id(0)
    n = pl.num_programs(0)

    def issue(step):
        slot = step % 2

        def one(k, _):
            pltpu.make_async_copy(src_ref.at[_token_rows(idx_ref[step * tm + k])],
                                  buf_ref.at[slot, _token_rows(k)], sems.at[slot]).start()
            return 0

        lax.fori_loop(0, tm, one, 0, unroll=8)

    @pl.when(i == 0)
    def _first():
        issue(i)

    @pl.when(i + 1 < n)
    def _next():
        issue(i + 1)

    slot = i % 2
    pltpu.make_async_copy(src_ref.at[_token_rows(0, tm)], buf_ref.at[slot], sems.at[slot]).wait()
    return slot


def _moe_kernel(tg_ref, src_ref, stage_ref, wr_ref, br_ref, wg_ref, wu_ref, wd_ref, o_ref, buf_ref, sems):
    grp = tg_ref[pl.program_id(0)]
    slot = _gathered_tile(src_ref, stage_ref, buf_ref, sems, MOE_TILE)

    @pl.when(grp < N_GROUPS)
    def _experts():
        h = _from_token_tiles(buf_ref.at[slot])
        gates, _ = _route(h, wr_ref, br_ref)
        x = h.astype(bf16)
        lane = lax.broadcasted_iota(jnp.int32, gates.shape, 1)
        y = jnp.zeros(h.shape, f32)
        for e in range(EXPERTS_PER_GROUP):
            gcol = jnp.sum(jnp.where(lane == N_GROUPS + EXPERTS_PER_GROUP * grp + e, gates, 0.0),
                           axis=1, keepdims=True)
            a = jax.nn.silu(_dot(x, wg_ref[e])) * _dot(x, wu_ref[e])
            y = y + _dot((a * gcol).astype(bf16), wd_ref[e])
        _to_token_tiles(o_ref, y)

    @pl.when(grp >= N_GROUPS)
    def _unused():
        o_ref[...] = jnp.zeros_like(o_ref)


def _moe(stage, tile_group, slot_token, w_route, b_route, wg, wu, wd, layer):
    tm = MOE_TILE
    n_slots = slot_token.shape[0]
    w_spec = lambda k, n: pl.BlockSpec((None, None, EXPERTS_PER_GROUP, k, n),
                                       lambda j, tg, src: (layer, jnp.minimum(tg[j], N_GROUPS - 1), 0, 0, 0))
    grouped = lambda w: w.reshape(w.shape[0], N_GROUPS, EXPERTS_PER_GROUP, *w.shape[2:])
    return pl.pallas_call(
        _moe_kernel,
        grid_spec=pltpu.PrefetchScalarGridSpec(
            num_scalar_prefetch=2, grid=(n_slots // tm,),
            in_specs=[pl.BlockSpec(memory_space=pl.ANY),
                      pl.BlockSpec((D_MODEL, LANES), lambda j, tg, src: (0, 0)),
                      pl.BlockSpec((1, LANES), lambda j, tg, src: (0, 0)),
                      w_spec(D_MODEL, D_EXPERT), w_spec(D_MODEL, D_EXPERT), w_spec(D_EXPERT, D_MODEL)],
            out_specs=pl.BlockSpec((tm * TOKEN_ROWS, LANES), lambda j, tg, src: (j, 0)),
            scratch_shapes=[pltpu.VMEM((2, tm * TOKEN_ROWS, LANES), f32), pltpu.SemaphoreType.DMA((2,))]),
        out_shape=jax.ShapeDtypeStruct((n_slots * TOKEN_ROWS, LANES), f32),
        compiler_params=_params("arbitrary"),
        name="moe_ffn",
    )(tile_group, slot_token, stage, w_route, b_route, grouped(wg), grouped(wu), grouped(wd))


def _ple_kernel(pos_ref, x_ref, ys_ref, p_ref, g_ref, wg_ref, wp_ref, fg_ref, o_ref, buf_ref, sems, *, final):
    slot = _gathered_tile(pos_ref, ys_ref, buf_ref, sems, x_ref.shape[0])
    x = x_ref[...] + _from_token_tiles(buf_ref.at[slot])
    gate = jax.nn.sigmoid(_dot(_rms(x, g_ref[...]).astype(bf16), wg_ref[...]))
    y = x + gate * _dot(p_ref[...].astype(bf16), wp_ref[...])
    o_ref[...] = _rms(y, fg_ref[...]) if final else y


def _ple(x2d, ys, pos, p3, g, wg, wp, fg, layer, final):
    T = x2d.shape[0]
    tm = 512
    full = lambda a: pl.BlockSpec(a.shape, lambda i, pos: (0,) * a.ndim)
    return pl.pallas_call(
        functools.partial(_ple_kernel, final=final),
        grid_spec=pltpu.PrefetchScalarGridSpec(
            num_scalar_prefetch=1, grid=(T // tm,),
            in_specs=[pl.BlockSpec((tm, D_MODEL), lambda i, pos: (i, 0)),
                      pl.BlockSpec(memory_space=pl.ANY),
                      pl.BlockSpec((None, tm, PLE_DIM), lambda i, pos: (layer, i, 0)),
                      full(g), full(wg), full(wp), full(fg)],
            out_specs=pl.BlockSpec((tm, D_MODEL), lambda i, pos: (i, 0)),
            scratch_shapes=[pltpu.VMEM((2, tm * TOKEN_ROWS, LANES), f32), pltpu.SemaphoreType.DMA((2,))]),
        out_shape=jax.ShapeDtypeStruct((T, D_MODEL), f32),
        compiler_params=_params("arbitrary"),
        name="ple",
    )(pos, x2d, ys, p3, g, wg, wp, fg)


def _alibi_slopes():
    s = 2.0 ** (-8.0 * np.arange(1, N_HEADS_TOTAL + 1) / N_HEADS_TOTAL)
    assert np.all(np.log2(s[1::2]) == np.round(np.log2(s[1::2])))
    return jnp.asarray(s[0::2], f32), jnp.asarray(s[1::2], f32)


def _selection_constants(seq):
    ncb = seq // CMP_STRIDE
    nsel = seq // SEL_BLOCK
    n_cmp = (seq - CMP_BLOCK) // CMP_STRIDE + 1
    cs = np.arange(n_cmp) * CMP_STRIDE
    bs = np.arange(nsel) * SEL_BLOCK
    ov = np.clip(np.minimum(cs[:, None] + CMP_BLOCK, bs[None, :] + SEL_BLOCK)
                 - np.maximum(cs[:, None], bs[None, :]), 0, None) / CMP_BLOCK
    assert ncb == LANES and seq <= POS_SPLIT * 256
    ovt = np.zeros((LANES, ncb), np.float32)
    ovt[:nsel, :n_cmp] = ov.T
    pos = np.arange(seq)
    kaug = np.zeros((seq, LANES), np.float32)
    kaug[:, POS_LANE0:POS_LANE0 + 2] = 1.0
    kaug[:, POS_LANE0 + 2] = pos // POS_SPLIT
    kaug[:, POS_LANE0 + 3] = pos % POS_SPLIT
    kaug[pos, SEL_LANE0 + pos // SEL_BLOCK] = 1.0
    cend = np.arange(ncb) * CMP_STRIDE + CMP_BLOCK - 1
    caug = np.zeros((ncb, LANES), np.float32)
    caug[:, POS_LANE0:POS_LANE0 + 2] = 1.0
    caug[:, POS_LANE0 + 2] = cend // POS_SPLIT
    caug[:, POS_LANE0 + 3] = cend % POS_SPLIT
    gsel = np.zeros((2, LANES, GQA_REP // 2, 3, 2, HEAD_DIM), np.float32)
    for pr in range(GQA_REP // 2):
        for j in range(3):
            for hh in range(2):
                gsel[:, 3 * (2 * pr + hh) + j, pr, j, hh, :] = 1.0
    gsel = gsel.reshape(2 * LANES, (GQA_REP // 2) * 3 * LANES)
    return jnp.asarray(ovt, bf16), jnp.asarray(kaug), jnp.asarray(caug), jnp.asarray(gsel, bf16)


def _invert_kernel(pos_ref, slot_ref):
    def clear(s, _):
        slot_ref[s] = 0
        return 0

    def place(t, _):
        slot_ref[pos_ref[t]] = t
        return 0

    lax.fori_loop(0, slot_ref.shape[0], clear, 0, unroll=8)
    lax.fori_loop(0, pos_ref.shape[0], place, 0, unroll=8)


def _invert_slots(pos, n_slots):
    smem = pl.BlockSpec(memory_space=pltpu.SMEM)
    return pl.pallas_call(
        _invert_kernel,
        in_specs=[smem],
        out_specs=smem,
        out_shape=jax.ShapeDtypeStruct((n_slots,), jnp.int32),
        name="moe_slot_index",
    )(pos)


def _dispatch_plan(meta, counts, n_tokens):
    i32 = jnp.int32
    n_slots = n_tokens + N_GROUPS * MOE_TILE
    group = meta[:, 0, :].reshape(n_tokens).astype(i32)
    rank = (meta[:, 1, :] * RANK_SPLIT + meta[:, 2, :]).reshape(n_tokens).astype(i32)
    cnt = counts[0, :N_GROUPS].astype(i32)
    padded = (cnt + MOE_TILE - 1) // MOE_TILE * MOE_TILE
    end = jnp.cumsum(padded)
    pos = (end - padded)[group] + rank
    slot_token = _invert_slots(pos, n_slots)
    tile_start = jnp.arange(n_slots // MOE_TILE, dtype=i32) * MOE_TILE
    tile_group = jnp.sum(tile_start[:, None] >= end[None, :], axis=1).astype(i32)
    return pos, slot_token, tile_group


def _block_diag2(w):
    z = jnp.zeros_like(w)
    return jnp.concatenate([jnp.concatenate([w, z], axis=-1), jnp.concatenate([z, w], axis=-1)], axis=-2)


def _layout_w_in(w):
    gate = w[:, N_MAIN:]
    per = GQA_REP * 3
    blocks = [jnp.pad(gate[:, g * per:(g + 1) * per], ((0, 0), (0, LANES - per))) for g in range(N_KV_B)]
    return jnp.concatenate([w[:, :N_MAIN]] + blocks, axis=1).astype(bf16)


def kernel(x, p, attn_norm, w_in, w_out, w_cmp_k1, w_cmp_k2, w_cmp_v1, w_cmp_v2, cmp_pos, ffn_norm, w_route_group, b_route_group, w_route_expert, b_route_expert, w_expert_gate, w_expert_up, w_expert_down, ple_norm, w_ple_gate, w_ple_proj, final_norm):
    B, S, D = x.shape
    depth = w_in.shape[0]
    T = B * S
    slopes_a, slopes_b = _alibi_slopes()
    ovt, kaug, caug, gsel = _selection_constants(S)
    wg_all = w_expert_gate.astype(bf16)
    wu_all = w_expert_up.astype(bf16)
    wd_all = w_expert_down.astype(bf16)
    p3 = p.reshape(depth, T, PLE_DIM)
    row = lambda v: v.reshape(1, -1)
    n_route = N_GROUPS + N_EXPERTS

    x2d = x.reshape(T, D)
    for i in range(depth):
        proj3 = _in_proj(x2d, row(attn_norm[i]), _layout_w_in(w_in[i])).reshape(B, S, N_PROJ)
        oa = _dilated(proj3, slopes_a)
        w1 = lambda w: _block_diag2(w.reshape(CMP_BLOCK, HEAD_DIM, CMP_HIDDEN)).astype(bf16)
        pos_dup = jnp.concatenate([cmp_pos[i], cmp_pos[i]], axis=-1)
        kcmp, vcmp = _compress(proj3, pos_dup, w1(w_cmp_k1[i]), _block_diag2(w_cmp_k2[i]).astype(bf16),
                               w1(w_cmp_v1[i]), _block_diag2(w_cmp_v2[i]).astype(bf16))
        ob = _nsa(proj3, kcmp, vcmp, slopes_b, ovt, kaug, caug, gsel)
        wo = w_out[i].astype(bf16)
        w_route = jnp.pad(jnp.concatenate([w_route_group[i], w_route_expert[i]], axis=1),
                          ((0, 0), (0, LANES - n_route)))
        b_route = jnp.pad(jnp.concatenate([b_route_group[i], b_route_expert[i]]), (0, LANES - n_route))
        x2d, stage, meta, counts = _out_route(x2d, oa.reshape(T, A_W), ob.reshape(T, -1), wo[:A_W], wo[A_W:],
                                              row(ffn_norm[i]), w_route, row(b_route))
        pos, slot_token, tile_group = _dispatch_plan(meta, counts, T)
        ys = _moe(stage, tile_group, slot_token, w_route, row(b_route), wg_all, wu_all, wd_all, i)
        x2d = _ple(x2d, ys, pos, p3, row(ple_norm[i]), w_ple_gate[i].astype(bf16),
                   w_ple_proj[i].astype(bf16), row(final_norm), i, i == depth - 1)
    return x2d.reshape(B, S, D)
```

```python
import functools

import numpy as np
import jax
import jax.numpy as jnp
from jax import lax
from jax.experimental import pallas as pl
from jax.experimental.pallas import tpu as pltpu

D_MODEL = 1024
PLE_DIM = 256
HEAD_DIM = 64
N_HEADS_A = 8
N_HEADS_B = 8
N_KV_B = 2
GQA_REP = N_HEADS_B // N_KV_B
N_HEADS_TOTAL = N_HEADS_A + N_HEADS_B
DILATED_PATTERNS = ((128, 1), (512, 4), (2048, 16))
CMP_BLOCK = 32
CMP_STRIDE = 16
CMP_HIDDEN = 256
SEL_BLOCK = 64
SEL_TOP = 16
N_LOCAL_BLOCKS = 2
WINDOW = 512
N_GROUPS = 4
EXPERTS_PER_GROUP = 4
N_EXPERTS = N_GROUPS * EXPERTS_PER_GROUP
D_EXPERT = 512
RMS_EPS = 1e-6
NEG = -1e30
FORCE_BONUS = 1e4
SCALE = HEAD_DIM ** -0.5

LANES = 128
QB = 128
SEL_CHUNK = 512
NSA_UNITS = 2
NSA_QBLOCKS = 4
DIL_INFLIGHT = 3
DIL_TRIP = 16
POS_LANE0 = HEAD_DIM
POS_SPLIT = 16
SEL_LANE0 = POS_LANE0 + 4
SEL_LANES = 32
MASK_BIG = -(2.0 ** 100)
MOE_TILE = 512
TOKEN_ROWS = D_MODEL // LANES
RANK_SPLIT = 128
A_W = N_HEADS_A * HEAD_DIM
N_MAIN = 3 * A_W + N_HEADS_B * HEAD_DIM + 6 * N_KV_B * HEAD_DIM
N_PROJ = N_MAIN + N_KV_B * LANES
COL_QB = (3 * A_W) // (2 * LANES)
COL_KC = (3 * A_W + N_HEADS_B * HEAD_DIM) // LANES
COL_GATE = N_MAIN // LANES
VMEM_LIMIT = 56 * 1024 * 1024

f32 = jnp.float32
bf16 = jnp.bfloat16


def _dot(a, b):
    return jnp.dot(a, b, preferred_element_type=f32)


def _dot_nt(a, b):
    return lax.dot_general(a, b, (((1,), (1,)), ((), ())), preferred_element_type=f32)


def _rms(x, g):
    return x * lax.rsqrt(jnp.mean(x * x, axis=-1, keepdims=True) + RMS_EPS) * g


def _params(*sem):
    return pltpu.CompilerParams(dimension_semantics=sem, vmem_limit_bytes=VMEM_LIMIT)


def _in_proj_kernel(x_ref, g_ref, w_ref, o_ref):
    h = _rms(x_ref[...], g_ref[...]).astype(bf16)
    for n0 in range(0, N_PROJ, 512):
        o_ref[:, n0:n0 + 512] = _dot(h, w_ref[:, n0:n0 + 512])


def _in_proj(x2d, g, w):
    T = x2d.shape[0]
    tm = 512
    return pl.pallas_call(
        _in_proj_kernel,
        grid=(T // tm,),
        in_specs=[pl.BlockSpec((tm, D_MODEL), lambda i: (i, 0)),
                  pl.BlockSpec((1, D_MODEL), lambda i: (0, 0)),
                  pl.BlockSpec((D_MODEL, N_PROJ), lambda i: (0, 0))],
        out_specs=pl.BlockSpec((tm, N_PROJ), lambda i: (i, 0)),
        out_shape=jax.ShapeDtypeStruct((T, N_PROJ), f32),
        compiler_params=_params("parallel"),
        name="in_proj",
    )(x2d, g, w)


def _dil_kernel(slope_ref, q_ref, k_ref, v_ref, o_ref, out_ref, lse_ref, bias_scr, *bufs, seq):
    hp = pl.program_id(1)
    lane = lax.broadcasted_iota(jnp.int32, (QB, LANES), 1)
    left = lane < HEAD_DIM
    slab = 32
    s_bufs, e_bufs = bufs[:DIL_INFLIGHT], bufs[DIL_INFLIGHT:]

    i = lax.broadcasted_iota(jnp.int32, (QB, 2 * QB), 0)
    j = lax.broadcasted_iota(jnp.int32, (QB, 2 * QB), 1)
    rel = i - j + QB
    valid = (rel >= 0) & (rel <= QB)
    relf = rel.astype(f32)
    for p, (window, dil) in enumerate(DILATED_PATTERNS):
        assert window // dil == QB
        for hh in range(2):
            bias_scr[2 * p + hh] = jnp.where(valid, (-float(dil) * slope_ref[2 * hp + hh]) * relf, NEG)

    ones = jnp.ones((2 * QB, LANES), bf16)

    def scores(u, blk):
        p, dil, row0, key0, nk = blk
        qc = q_ref[pl.ds(row0, QB, stride=dil), :] * SCALE
        q2 = jnp.concatenate([jnp.where(left, qc, 0.0), jnp.where(left, 0.0, qc)], axis=0).astype(bf16)
        s_bufs[u][:, :nk] = _dot_nt(q2, k_ref[pl.ds(key0, nk, stride=dil), :].astype(bf16))

    def probs(u, blk):
        p, dil, row0, key0, nk = blk
        c0 = 2 * QB - nk
        ms = []
        for hh in range(2):
            parts = []
            for sl in range(QB // slab):
                rows = slice(hh * QB + sl * slab, hh * QB + (sl + 1) * slab)
                tiles = [s_bufs[u][rows, c:c + LANES]
                         + bias_scr[2 * p + hh, sl * slab:(sl + 1) * slab, c0 + c:c0 + c + LANES]
                         for c in range(0, nk, LANES)]
                mx = tiles[0]
                for t in tiles[1:]:
                    mx = jnp.maximum(mx, t)
                mx = jnp.broadcast_to(jnp.max(mx, axis=1, keepdims=True), (slab, LANES))
                for c, t in zip(range(0, nk, LANES), tiles):
                    e_bufs[u][rows, c:c + LANES] = jnp.exp(t - mx).astype(bf16)
                parts.append(mx)
            ms.append(jnp.concatenate(parts, axis=0))
        return jnp.where(left, ms[0], ms[1])

    def values(u, blk, row_max):
        p, dil, row0, key0, nk = blk
        v2 = jnp.concatenate([v_ref[pl.ds(key0, nk, stride=dil), :].astype(bf16), ones[:nk]], axis=1)
        res = _dot(e_bufs[u][:, :nk], v2)
        rows = pl.ds(row0, QB, stride=dil)
        den = jnp.where(left, res[:QB, LANES:], res[QB:, LANES:])
        out_ref[p, rows, :] = jnp.where(left, res[:QB, :LANES], res[QB:, :LANES]) / den
        lse_ref[p, rows, :] = row_max + jnp.log(den)

    def run(blocks):
        ahead = DIL_INFLIGHT - 1
        for n in range(min(ahead, len(blocks))):
            scores(n % DIL_INFLIGHT, blocks[n])
        for n, blk in enumerate(blocks):
            if n + ahead < len(blocks):
                scores((n + ahead) % DIL_INFLIGHT, blocks[n + ahead])
            values(n % DIL_INFLIGHT, blk, probs(n % DIL_INFLIGHT, blk))

    def first(p, dil, r):
        return (p, dil, r, r, QB)

    def later(p, dil, r, a):
        return (p, dil, r + dil * QB * a, r + dil * QB * (a - 1), 2 * QB)

    for p, (window, dil) in enumerate(DILATED_PATTERNS):
        nblk = seq // dil // QB
        blocks = [first(p, dil, r) if a == 0 else later(p, dil, r, a) for r in range(dil) for a in range(nblk)]
        for b0 in range(0, len(blocks), DIL_TRIP):
            run(blocks[b0:b0 + DIL_TRIP])

    ch = 256

    for c in range(seq // ch):
        rows = slice(c * ch, (c + 1) * ch)
        lses = [lse_ref[p, rows, :] for p in range(len(DILATED_PATTERNS))]
        big = functools.reduce(jnp.maximum, lses)
        num = jnp.zeros((ch, LANES), f32)
        den = jnp.zeros((ch, LANES), f32)
        for p, lse in enumerate(lses):
            w = jnp.exp(lse - big)
            num = num + w * out_ref[p, rows, :]
            den = den + w
        o_ref[rows, :] = (num / den).astype(o_ref.dtype)


def _dilated(proj3, slopes_a):
    B, S, _ = proj3.shape
    npair = N_HEADS_A // 2
    blk = lambda off: pl.BlockSpec((None, S, LANES), lambda b, hp, off=off: (b, 0, off + hp))
    return pl.pallas_call(
        functools.partial(_dil_kernel, seq=S),
        grid=(B, npair),
        in_specs=[pl.BlockSpec(memory_space=pltpu.SMEM), blk(0), blk(npair), blk(2 * npair)],
        out_specs=pl.BlockSpec((None, S, LANES), lambda b, hp: (b, 0, hp)),
        out_shape=jax.ShapeDtypeStruct((B, S, A_W), bf16),
        scratch_shapes=([pltpu.VMEM((len(DILATED_PATTERNS), S, LANES), f32)] * 2
                        + [pltpu.VMEM((2 * len(DILATED_PATTERNS), QB, 2 * QB), f32)]
                        + [pltpu.VMEM((2 * QB, 2 * QB), f32)] * DIL_INFLIGHT
                        + [pltpu.VMEM((2 * QB, 2 * QB), bf16)] * DIL_INFLIGHT),
        compiler_params=_params("parallel", "parallel"),
        name="dilated_attn",
    )(slopes_a, proj3, proj3, proj3)


def _cmp_kernel(kc_ref, vc_ref, pos_ref, w1k_ref, w2k_ref, w1v_ref, w2v_ref, ko_ref, vo_ref, *, ncb):
    half = CMP_BLOCK // 2
    for x_ref, pi, w1_ref, w2_ref, o_ref in ((kc_ref, 0, w1k_ref, w2k_ref, ko_ref),
                                             (vc_ref, 1, w1v_ref, w2v_ref, vo_ref)):
        lo = jnp.zeros((ncb, 2 * CMP_HIDDEN), f32)
        hi = jnp.zeros((ncb, 2 * CMP_HIDDEN), f32)
        for r in range(half):
            x = x_ref[pl.ds(r, ncb, stride=CMP_STRIDE), :]
            lo = lo + _dot((x + pos_ref[pi, r:r + 1, :]).astype(bf16), w1_ref[r])
            hi = hi + _dot((x + pos_ref[pi, r + half:r + half + 1, :]).astype(bf16), w1_ref[r + half])
        h1 = lo + pltpu.roll(hi, ncb - 1, axis=0)
        o_ref[...] = _dot(jax.nn.gelu(h1).astype(bf16), w2_ref[...])


def _compress(proj3, pos_dup, w1k, w2k, w1v, w2v):
    B, S, _ = proj3.shape
    ncb = S // CMP_STRIDE
    full = lambda a: pl.BlockSpec(a.shape, lambda b: (0,) * a.ndim)
    out = jax.ShapeDtypeStruct((B, ncb, LANES), f32)
    return pl.pallas_call(
        functools.partial(_cmp_kernel, ncb=ncb),
        grid=(B,),
        in_specs=[pl.BlockSpec((None, S, LANES), lambda b: (b, 0, COL_KC)),
                  pl.BlockSpec((None, S, LANES), lambda b: (b, 0, COL_KC + 1)),
                  full(pos_dup), full(w1k), full(w2k), full(w1v), full(w2v)],
        out_specs=[pl.BlockSpec((None, ncb, LANES), lambda b: (b, 0, 0))] * 2,
        out_shape=[out, out],
        compiler_params=_params("parallel"),
        name="nsa_compress",
    )(proj3, proj3, pos_dup, w1k, w2k, w1v, w2v)


def _nsa_kernel(slope_ref, q_ref, ks_ref, vs_ref, kw_ref, vw_ref, gl_ref, kcmp_ref, vcmp_ref,
                ovt_ref, kaug_ref, caug_ref, gsel_ref, o_ref,
                ksa, vsa, kwa, vwa, kca, vca, q_scr, s_m, *per_unit, seq):
    g = pl.program_id(1)
    nsel = seq // SEL_BLOCK
    n_top = min(SEL_TOP, nsel)
    assert nsel <= SEL_LANES and N_LOCAL_BLOCKS * SEL_BLOCK >= QB and n_top > N_LOCAL_BLOCKS

    def group_lanes(x):
        return jnp.where(g == 0, x, pltpu.roll(x, HEAD_DIM, axis=1))

    @pl.when(pl.program_id(2) == 0)
    def _prep():
        ch = 256
        lane_c = lax.broadcasted_iota(jnp.int32, (ch, LANES), 1)
        is_k = lane_c < HEAD_DIM
        for c in range(seq // ch):
            sl = slice(c * ch, (c + 1) * ch)
            aug = kaug_ref[sl, :]
            ksa[sl, :] = jnp.where(is_k, group_lanes(ks_ref[sl, :]), aug).astype(bf16)
            kwa[sl, :] = jnp.where(is_k, group_lanes(kw_ref[sl, :]),
                                   jnp.where(lane_c < SEL_LANE0, aug, 0.0)).astype(bf16)
            for src, dst in ((vs_ref, vsa), (vw_ref, vwa)):
                v = group_lanes(src[sl, :])
                dst[sl, :LANES] = jnp.where(is_k, v, 1.0).astype(bf16)
                dst[sl, LANES:] = jnp.where(is_k, 1.0, pltpu.roll(v, HEAD_DIM, axis=1)).astype(bf16)
        lane_k = lax.broadcasted_iota(jnp.int32, kca.shape, 1)
        kca[...] = jnp.where(lane_k < HEAD_DIM, group_lanes(kcmp_ref[...]), caug_ref[...]).astype(bf16)
        vc = group_lanes(vcmp_ref[...])
        vca[...] = jnp.where(lane_k < HEAD_DIM, vc, pltpu.roll(vc, HEAD_DIM, axis=1)).astype(bf16)

    per = SEL_CHUNK // QB
    refs = (slope_ref, q_ref, gl_ref, ovt_ref, gsel_ref, o_ref, ksa, vsa, kwa, vwa, kca, vca, q_scr, s_m, per_unit)
    for h in range(NSA_QBLOCKS):
        qi = NSA_QBLOCKS * pl.program_id(2) + h
        n_chunks = (qi + per - 1) // per
        need_topk = 2 * qi + 2 > n_top
        for n in range((seq // QB - 1 + per - 1) // per + 1):
            blocks = [q for q in range(max(per * (n - 1) + 1, 0), min(per * n, seq // QB - 1) + 1)
                      if q % NSA_QBLOCKS == h]
            for topk in sorted({2 * q + 2 > n_top for q in blocks}):
                pl.when((n_chunks == n) & (need_topk == topk))(
                    functools.partial(_nsa_step, refs, qi, h * QB, seq=seq, n=n, topk=topk))


def _nsa_step(refs, qi, row0, *, seq, n, topk):
    slope_ref, q_ref, gl_ref, ovt_ref, gsel_ref, o_ref, ksa, vsa, kwa, vwa, kca, vca, q_scr, s_m, per_unit = refs
    g = pl.program_id(1)
    nsel = seq // SEL_BLOCK
    n_top = min(SEL_TOP, nsel)
    slab = 32
    ur = GQA_REP * QB // NSA_UNITS
    s_w, e_w, s_d, e_d, e_m, m_h, al_h, acc = [per_unit[i * NSA_UNITS:(i + 1) * NSA_UNITS] for i in range(8)]
    lane = lax.broadcasted_iota(jnp.int32, (QB, LANES), 1)
    left = lane < HEAD_DIM
    ii = lax.broadcasted_iota(jnp.int32, (QB, LANES), 0)
    t_row = qi * QB + ii
    t_hi = (t_row // POS_SPLIT).astype(f32)
    t_lo = (t_row % POS_SPLIT).astype(f32)
    slopes = [slope_ref[g * GQA_REP + r] for r in range(GQA_REP)]
    head = lambda a, r: a[r * QB:(r + 1) * QB]

    def q_head(r):
        x = q_ref[row0:row0 + QB, (r // 2) * LANES:(r // 2 + 1) * LANES] * SCALE
        if r % 2:
            x = pltpu.roll(x, HEAD_DIM, axis=1)
        m = slopes[r]
        pos = jnp.where(lane == POS_LANE0, (-POS_SPLIT * m) * t_hi,
                        jnp.where(lane == POS_LANE0 + 1, (-m) * t_lo,
                                  jnp.where(lane == POS_LANE0 + 2, POS_SPLIT * m,
                                            jnp.where(lane == POS_LANE0 + 3, m, 0.0))))
        return jnp.where(left, x, pos)

    q4 = jnp.concatenate([q_head(r) for r in range(GQA_REP)], axis=0)
    q4b = q4.astype(bf16)

    def rel_tile(nk, offset):
        i = lax.broadcasted_iota(jnp.int32, (QB, nk), 0)
        j = lax.broadcasted_iota(jnp.int32, (QB, nk), 1)
        return (i - j + offset).astype(f32)

    def softmax_rows(s_ref, e_ref, m_ref, al_ref, nk, bias, running):
        for sl in range(ur // slab):
            rows = slice(sl * slab, (sl + 1) * slab)
            cols = [slice(j * LANES, (j + 1) * LANES) for j in range(nk // LANES)]
            i0 = (sl * slab) % QB
            tiles = [s_ref[rows, c] if bias is None else s_ref[rows, c] + bias[i0:i0 + slab, c] for c in cols]
            mx = tiles[0]
            for t in tiles[1:]:
                mx = jnp.maximum(mx, t)
            m_new = jnp.broadcast_to(jnp.max(mx, axis=1, keepdims=True), (slab, LANES))
            if running:
                m_old = m_ref[rows, :]
                m_new = jnp.maximum(m_old, m_new)
                al_ref[rows, :] = jnp.exp(m_old - m_new)
            if m_ref is not None:
                m_ref[rows, :] = m_new
            for c, t in zip(cols, tiles):
                e_ref[rows, c] = jnp.exp(t - m_new).astype(bf16)

    unit = lambda a, u: a[u * ur:(u + 1) * ur]

    def own_half(pv, u):
        heads = range(u * ur // QB, (u + 1) * ur // QB)
        return jnp.concatenate([pv[(r - heads[0]) * QB:(r - heads[0] + 1) * QB, (r % 2) * LANES:(r % 2 + 1) * LANES]
                                for r in heads], axis=0)

    s4 = _dot_nt(q4b, kca[...])

    sig = jax.nn.sigmoid(gl_ref[row0:row0 + QB, :])
    g_hi = sig.astype(bf16)
    g_lo = (sig - g_hi.astype(f32)).astype(bf16)
    gate_b = _dot(jnp.concatenate([g_hi, g_lo], axis=1), gsel_ref[...])

    nwin = WINDOW + QB
    w0 = pl.multiple_of(jnp.maximum(qi - WINDOW // QB, 0) * QB, QB)
    dist_w = rel_tile(nwin, qi * QB - w0)
    bias_w = jnp.where((dist_w >= 0) & (dist_w < WINDOW), 0.0, NEG)
    for u in range(NSA_UNITS):
        s_w[u][...] = _dot_nt(unit(q4b, u), kwa[pl.ds(w0, nwin), :])

    d0 = pl.multiple_of(qi * QB, QB)
    bias_d = jnp.where(rel_tile(QB, 0) >= 0, 0.0, NEG)
    for u in range(NSA_UNITS):
        s_d[u][...] = _dot_nt(unit(q4b, u), ksa[pl.ds(d0, QB), :])
    s_m[0] = _dot_nt(q4b, ksa[:SEL_CHUNK, :])

    cmp_end = (lane * CMP_STRIDE + (CMP_BLOCK - 1))
    valid_c = t_row >= cmp_end
    ps = []
    p_sum = jnp.zeros((QB, LANES), f32)
    row_bcast = lambda col: jnp.broadcast_to(col, (QB, LANES))
    for r in range(GQA_REP):
        s = jnp.where(valid_c, head(s4, r), NEG)
        e = jnp.exp(s - row_bcast(jnp.max(s, axis=1, keepdims=True)))
        p = jnp.where(valid_c, e, 0.0) / row_bcast(jnp.sum(e, axis=1, keepdims=True))
        p_sum = p_sum + p
        ps.append(p.astype(bf16))
    o_cmp4 = _dot(jnp.concatenate(ps, axis=0), vca[...])

    o_w = []
    for u in range(NSA_UNITS):
        softmax_rows(s_w[u], e_w[u], None, None, nwin, bias_w, False)
        o_w.append(own_half(_dot(e_w[u][...], vwa[pl.ds(w0, nwin), :]), u))
    for u in range(NSA_UNITS):
        softmax_rows(s_d[u], e_d[u], m_h[u], None, QB, bias_d, False)
        acc[u][...] = own_half(_dot(e_d[u][...], vsa[pl.ds(d0, QB), :]), u)

    back = t_row // SEL_BLOCK - lane
    valid_s = (back >= 0) & (lane < nsel)

    def _all_valid():
        return jnp.where(valid_s, 1.0, 0.0)

    def _top_k():
        p_hi = p_sum.astype(bf16)
        p_lo = (p_sum - p_hi.astype(f32)).astype(bf16)
        imp_t = (_dot_nt(ovt_ref[...], p_hi) + _dot_nt(ovt_ref[...], p_lo))[:SEL_LANES]
        blk = lax.broadcasted_iota(jnp.int32, (SEL_LANES, QB), 0)
        tq = qi * QB + lax.broadcasted_iota(jnp.int32, (SEL_LANES, QB), 1)
        back_t = tq // SEL_BLOCK - blk
        valid_t = (back_t >= 0) & (blk < nsel)
        forced = (blk == 0) | (valid_t & (back_t < N_LOCAL_BLOCKS))
        score = jnp.where(valid_t, imp_t + jnp.where(forced, FORCE_BONUS, 0.0), NEG)
        score = jnp.where(blk < nsel, score, 2.0 * NEG)
        rank = jnp.zeros((SEL_LANES, QB), jnp.int32)
        for n in range(nsel):
            row = score[n:n + 1, :]
            ahead = (row > score) | ((row == score) & (blk > n))
            rank = rank + ahead.astype(jnp.int32)
        sel_t = jnp.where((rank < n_top) & valid_t, 1.0, 0.0)
        sel_t = jnp.concatenate([sel_t, jnp.zeros((LANES - SEL_LANES, QB), f32)], axis=0)
        return sel_t.T

    sel = _top_k() if topk else _all_valid()

    acc_w = jnp.concatenate(o_w, axis=0)

    def finish(n):
        if n > 0:
            sel_bias = jnp.where((sel > 0.5) & (lane < 2 * qi), 0.0, MASK_BIG)
            sel_bias = pltpu.roll(sel_bias, SEL_LANE0, axis=1)
            in_sel = (lane >= SEL_LANE0) & (lane < SEL_LANE0 + SEL_LANES)
            for r in range(GQA_REP):
                q_scr[r * QB:(r + 1) * QB, :] = jnp.where(in_sel, sel_bias, head(q4, r)).astype(bf16)
            bias0 = _dot_nt(jnp.where(in_sel, sel_bias, 0.0).astype(bf16), ksa[:SEL_CHUNK, :])
        for kc in range(n):
            half = kc % 2
            if kc + 1 < n:
                s_m[1 - half] = _dot_nt(q_scr[...], ksa[(kc + 1) * SEL_CHUNK:(kc + 2) * SEL_CHUNK, :])
            for u in range(NSA_UNITS):
                softmax_rows(s_m.at[half, pl.ds(u * ur, ur)], e_m[u], m_h[u], al_h[u], SEL_CHUNK,
                             bias0 if kc == 0 else None, True)
                acc[u][...] = (al_h[u][...] * acc[u][...]
                               + own_half(_dot(e_m[u][...], vsa[kc * SEL_CHUNK:(kc + 1) * SEL_CHUNK, :]), u))

        acc_s = jnp.concatenate([a[...] for a in acc], axis=0)
        for pr in range(GQA_REP // 2):
            ev, od = 2 * pr, 2 * pr + 1
            out = (gate_b[:, (3 * pr) * LANES:(3 * pr + 1) * LANES]
                   * jnp.where(left, head(o_cmp4, ev), head(o_cmp4, od)))
            for j, a in ((1, acc_s), (2, acc_w)):
                num = jnp.where(left, head(a, ev), head(a, od))
                den = pltpu.roll(jnp.where(left, head(a, od), head(a, ev)), HEAD_DIM, axis=1)
                out = out + gate_b[:, (3 * pr + j) * LANES:(3 * pr + j + 1) * LANES] / den * num
            o_ref[row0:row0 + QB, pr * LANES:(pr + 1) * LANES] = out.astype(o_ref.dtype)

    finish(n)


def _nsa(proj3, kcmp, vcmp, slopes_b, ovt, kaug, caug, gsel):
    B, S, _ = proj3.shape
    ncb = kcmp.shape[1]
    nrow = GQA_REP * QB
    ur = nrow // NSA_UNITS
    nwin = WINDOW + QB
    kv = lambda j: pl.BlockSpec((None, S, LANES), lambda b, g, qi, j=j: (b, 0, COL_KC + j))
    full = lambda a: pl.BlockSpec(a.shape, lambda b, g, qi: (0,) * a.ndim)
    return pl.pallas_call(
        functools.partial(_nsa_kernel, seq=S),
        grid=(B, N_KV_B, S // (NSA_QBLOCKS * QB)),
        in_specs=[pl.BlockSpec(memory_space=pltpu.SMEM),
                  pl.BlockSpec((None, NSA_QBLOCKS * QB, 2 * LANES), lambda b, g, qi: (b, qi, COL_QB + g)),
                  kv(2), kv(3), kv(4), kv(5),
                  pl.BlockSpec((None, NSA_QBLOCKS * QB, LANES), lambda b, g, qi: (b, qi, COL_GATE + g)),
                  pl.BlockSpec((None, ncb, LANES), lambda b, g, qi: (b, 0, 0)),
                  pl.BlockSpec((None, ncb, LANES), lambda b, g, qi: (b, 0, 0)),
                  full(ovt), full(kaug), full(caug), full(gsel)],
        out_specs=pl.BlockSpec((None, NSA_QBLOCKS * QB, 2 * LANES), lambda b, g, qi: (b, qi, g)),
        out_shape=jax.ShapeDtypeStruct((B, S, N_HEADS_B * HEAD_DIM), bf16),
        scratch_shapes=([pltpu.VMEM((S, LANES), bf16), pltpu.VMEM((S, 2 * LANES), bf16)] * 2
                        + [pltpu.VMEM((ncb, LANES), bf16)] * 2
                        + [pltpu.VMEM((nrow, LANES), bf16), pltpu.VMEM((2, nrow, SEL_CHUNK), f32)]
                        + [pltpu.VMEM((ur, nwin), f32)] * NSA_UNITS + [pltpu.VMEM((ur, nwin), bf16)] * NSA_UNITS
                        + [pltpu.VMEM((ur, QB), f32)] * NSA_UNITS + [pltpu.VMEM((ur, QB), bf16)] * NSA_UNITS
                        + [pltpu.VMEM((ur, SEL_CHUNK), bf16)] * NSA_UNITS
                        + [pltpu.VMEM((ur, LANES), f32)] * (2 * NSA_UNITS)
                        + [pltpu.VMEM((ur, LANES), f32)] * NSA_UNITS),
        compiler_params=_params("parallel", "parallel", "arbitrary"),
        name="nsa_attn",
    )(slopes_b, proj3, proj3, proj3, proj3, proj3, proj3, kcmp, vcmp, ovt, kaug, caug, gsel)


def _route(h, w_ref, b_ref):
    w = w_ref[...]
    h_hi, w_hi = h.astype(bf16), w.astype(bf16)
    h_lo, w_lo = (h - h_hi.astype(f32)).astype(bf16), (w - w_hi.astype(f32)).astype(bf16)
    logit = _dot(h_hi, w_hi) + (_dot(h_hi, w_lo) + _dot(h_lo, w_hi)) + b_ref[...]
    tm = logit.shape[0]
    lane = lax.broadcasted_iota(jnp.int32, (tm, LANES), 1)
    big = jnp.int32(LANES)
    is_g = lane < N_GROUPS
    gl = jnp.where(is_g, logit, NEG)
    gmax = jnp.max(gl, axis=1, keepdims=True)
    gsum = jnp.sum(jnp.where(is_g, jnp.exp(gl - gmax), 0.0), axis=1, keepdims=True)
    gsel = jnp.min(jnp.where(is_g & (gl == gmax), lane, big), axis=1, keepdims=True)
    gw = 1.0 / gsum
    e_lane = lane - N_GROUPS
    in_grp = (e_lane >= 0) & (e_lane < N_EXPERTS) & (e_lane // EXPERTS_PER_GROUP == gsel)
    el = jnp.where(in_grp, logit, NEG)
    t1 = jnp.max(el, axis=1, keepdims=True)
    i1 = jnp.min(jnp.where(in_grp & (el == t1), lane, big), axis=1, keepdims=True)
    el2 = jnp.where(lane == i1, NEG, el)
    t2 = jnp.max(el2, axis=1, keepdims=True)
    i2 = jnp.min(jnp.where(in_grp & (lane != i1) & (el2 == t2), lane, big), axis=1, keepdims=True)
    e2 = jnp.exp(t2 - t1)
    w1 = gw / (1.0 + e2)
    w2 = gw * e2 / (1.0 + e2)
    return jnp.where(lane == i1, w1, jnp.where(lane == i2, w2, 0.0)), gsel


def _to_token_tiles(ref, x):
    for s in range(D_MODEL // LANES):
        ref[pl.ds(s, x.shape[0], stride=D_MODEL // LANES), :] = x[:, s * LANES:(s + 1) * LANES]


def _from_token_tiles(ref):
    n = D_MODEL // LANES
    return jnp.concatenate([ref[pl.ds(s, ref.shape[0] // n, stride=n), :] for s in range(n)], axis=1)


def _out_route_kernel(x_ref, oa_ref, ob_ref, wa_ref, wb_ref, g_ref, w_ref, b_ref, tri_ref,
                      x1_ref, stage_ref, meta_ref, cnt_ref, cnt_scr):
    @pl.when(pl.program_id(0) == 0)
    def _zero():
        cnt_scr[...] = jnp.zeros_like(cnt_scr)

    x1 = x_ref[...] + _dot(oa_ref[...], wa_ref[...]) + _dot(ob_ref[...], wb_ref[...])
    x1_ref[...] = x1
    h = _rms(x1, g_ref[...])
    _to_token_tiles(stage_ref, h)
    _, gsel = _route(h, w_ref, b_ref)
    tm = h.shape[0]
    lane = lax.broadcasted_iota(jnp.int32, (tm, LANES), 1)
    is_g = lane < N_GROUPS

    onehot = jnp.where(is_g & (lane == gsel), 1.0, 0.0)
    before = _dot(tri_ref[...], onehot.astype(bf16)) + cnt_scr[...]
    rank = jnp.sum(onehot * before, axis=1, keepdims=True)
    cnt_scr[...] = before[tm - 1:tm, :] + onehot[tm - 1:tm, :]
    cnt_ref[...] = cnt_scr[...]
    rank_hi = jnp.floor(rank * (1.0 / RANK_SPLIT))
    cols = jnp.where(lane == 0, gsel.astype(f32), jnp.where(lane == 1, rank_hi,
                                                            jnp.where(lane == 2, rank - RANK_SPLIT * rank_hi, 0.0)))
    pick = (lax.broadcasted_iota(jnp.int32, (8, LANES), 0) == lax.broadcasted_iota(jnp.int32, (8, LANES), 1))
    meta_ref[...] = _dot_nt(jnp.where(pick, 1.0, 0.0).astype(bf16), cols.astype(bf16))


def _out_route(x2d, oa, ob, wa, wb, g, w, b):
    T = x2d.shape[0]
    tm = MOE_TILE
    assert T // RANK_SPLIT <= 256
    tri = jnp.asarray(np.tril(np.ones((tm, tm), np.float32), -1), bf16)
    row = lambda n: pl.BlockSpec((tm, n), lambda i: (i, 0))
    full = lambda a: pl.BlockSpec(a.shape, lambda i: (0,) * a.ndim)
    return pl.pallas_call(
        _out_route_kernel,
        grid=(T // tm,),
        in_specs=[row(D_MODEL), row(oa.shape[1]), row(ob.shape[1]), full(wa), full(wb),
                  full(g), full(w), full(b), full(tri)],
        out_specs=[row(D_MODEL),
                   pl.BlockSpec((tm * TOKEN_ROWS, LANES), lambda i: (i, 0)),
                   pl.BlockSpec((None, 8, tm), lambda i: (i, 0, 0)),
                   pl.BlockSpec((1, LANES), lambda i: (0, 0))],
        out_shape=[jax.ShapeDtypeStruct((T, D_MODEL), f32),
                   jax.ShapeDtypeStruct((T * TOKEN_ROWS, LANES), f32),
                   jax.ShapeDtypeStruct((T // tm, 8, tm), f32),
                   jax.ShapeDtypeStruct((1, LANES), f32)],
        scratch_shapes=[pltpu.VMEM((1, LANES), f32)],
        compiler_params=_params("arbitrary"),
        name="out_proj_router",
    )(x2d, oa, ob, wa, wb, g, w, b, tri)


def _token_rows(t, n=1):
    return pl.ds(pl.multiple_of(t * TOKEN_ROWS, TOKEN_ROWS), n * TOKEN_ROWS)


def _gathered_tile(idx_ref, src_ref, buf_ref, sems, tm):
    i = pl.program_id(0)
    n = pl.num_programs(0)

    def issue(step):
        slot = step % 2

        def one(k, _):
            pltpu.make_async_copy(src_ref.at[_token_rows(idx_ref[step * tm + k])],
                                  buf_ref.at[slot, _token_rows(k)], sems.at[slot]).start()
            return 0

        lax.fori_loop(0, tm, one, 0, unroll=8)

    @pl.when(i == 0)
    def _first():
        issue(i)

    @pl.when(i + 1 < n)
    def _next():
        issue(i + 1)

    slot = i % 2
    pltpu.make_async_copy(src_ref.at[_token_rows(0, tm)], buf_ref.at[slot], sems.at[slot]).wait()
    return slot


def _moe_kernel(tg_ref, src_ref, stage_ref, wr_ref, br_ref, wg_ref, wu_ref, wd_ref, o_ref, buf_ref, sems):
    grp = tg_ref[pl.program_id(0)]
    slot = _gathered_tile(src_ref, stage_ref, buf_ref, sems, MOE_TILE)

    @pl.when(grp < N_GROUPS)
    def _experts():
        h = _from_token_tiles(buf_ref.at[slot])
        gates, _ = _route(h, wr_ref, br_ref)
        x = h.astype(bf16)
        lane = lax.broadcasted_iota(jnp.int32, gates.shape, 1)
        y = jnp.zeros(h.shape, f32)
        for e in range(EXPERTS_PER_GROUP):
            gcol = jnp.sum(jnp.where(lane == N_GROUPS + EXPERTS_PER_GROUP * grp + e, gates, 0.0),
                           axis=1, keepdims=True)
            a = jax.nn.silu(_dot(x, wg_ref[e])) * _dot(x, wu_ref[e])
            y = y + _dot((a * gcol).astype(bf16), wd_ref[e])
        _to_token_tiles(o_ref, y)

    @pl.when(grp >= N_GROUPS)
    def _unused():
        o_ref[...] = jnp.zeros_like(o_ref)


def _moe(stage, tile_group, slot_token, w_route, b_route, wg, wu, wd, layer):
    tm = MOE_TILE
    n_slots = slot_token.shape[0]
    w_spec = lambda k, n: pl.BlockSpec((None, None, EXPERTS_PER_GROUP, k, n),
                                       lambda j, tg, src: (layer, jnp.minimum(tg[j], N_GROUPS - 1), 0, 0, 0))
    grouped = lambda w: w.reshape(w.shape[0], N_GROUPS, EXPERTS_PER_GROUP, *w.shape[2:])
    return pl.pallas_call(
        _moe_kernel,
        grid_spec=pltpu.PrefetchScalarGridSpec(
            num_scalar_prefetch=2, grid=(n_slots // tm,),
            in_specs=[pl.BlockSpec(memory_space=pl.ANY),
                      pl.BlockSpec((D_MODEL, LANES), lambda j, tg, src: (0, 0)),
                      pl.BlockSpec((1, LANES), lambda j, tg, src: (0, 0)),
                      w_spec(D_MODEL, D_EXPERT), w_spec(D_MODEL, D_EXPERT), w_spec(D_EXPERT, D_MODEL)],
            out_specs=pl.BlockSpec((tm * TOKEN_ROWS, LANES), lambda j, tg, src: (j, 0)),
            scratch_shapes=[pltpu.VMEM((2, tm * TOKEN_ROWS, LANES), f32), pltpu.SemaphoreType.DMA((2,))]),
        out_shape=jax.ShapeDtypeStruct((n_slots * TOKEN_ROWS, LANES), f32),
        compiler_params=_params("arbitrary"),
        name="moe_ffn",
    )(tile_group, slot_token, stage, w_route, b_route, grouped(wg), grouped(wu), grouped(wd))


def _ple_kernel(pos_ref, x_ref, ys_ref, p_ref, g_ref, wg_ref, wp_ref, fg_ref, o_ref, buf_ref, sems, *, final):
    slot = _gathered_tile(pos_ref, ys_ref, buf_ref, sems, x_ref.shape[0])
    x = x_ref[...] + _from_token_tiles(buf_ref.at[slot])
    gate = jax.nn.sigmoid(_dot(_rms(x, g_ref[...]).astype(bf16), wg_ref[...]))
    y = x + gate * _dot(p_ref[...].astype(bf16), wp_ref[...])
    o_ref[...] = _rms(y, fg_ref[...]) if final else y


def _ple(x2d, ys, pos, p3, g, wg, wp, fg, layer, final):
    T = x2d.shape[0]
    tm = 512
    full = lambda a: pl.BlockSpec(a.shape, lambda i, pos: (0,) * a.ndim)
    return pl.pallas_call(
        functools.partial(_ple_kernel, final=final),
        grid_spec=pltpu.PrefetchScalarGridSpec(
            num_scalar_prefetch=1, grid=(T // tm,),
            in_specs=[pl.BlockSpec((tm, D_MODEL), lambda i, pos: (i, 0)),
                      pl.BlockSpec(memory_space=pl.ANY),
                      pl.BlockSpec((None, tm, PLE_DIM), lambda i, pos: (layer, i, 0)),
                      full(g), full(wg), full(wp), full(fg)],
            out_specs=pl.BlockSpec((tm, D_MODEL), lambda i, pos: (i, 0)),
            scratch_shapes=[pltpu.VMEM((2, tm * TOKEN_ROWS, LANES), f32), pltpu.SemaphoreType.DMA((2,))]),
        out_shape=jax.ShapeDtypeStruct((T, D_MODEL), f32),
        compiler_params=_params("arbitrary"),
        name="ple",
    )(pos, x2d, ys, p3, g, wg, wp, fg)


def _alibi_slopes():
    s = 2.0 ** (-8.0 * np.arange(1, N_HEADS_TOTAL + 1) / N_HEADS_TOTAL)
    assert np.all(np.log2(s[1::2]) == np.round(np.log2(s[1::2])))
    return jnp.asarray(s[0::2], f32), jnp.asarray(s[1::2], f32)


def _selection_constants(seq):
    ncb = seq // CMP_STRIDE
    nsel = seq // SEL_BLOCK
    n_cmp = (seq - CMP_BLOCK) // CMP_STRIDE + 1
    cs = np.arange(n_cmp) * CMP_STRIDE
    bs = np.arange(nsel) * SEL_BLOCK
    ov = np.clip(np.minimum(cs[:, None] + CMP_BLOCK, bs[None, :] + SEL_BLOCK)
                 - np.maximum(cs[:, None], bs[None, :]), 0, None) / CMP_BLOCK
    assert ncb == LANES and seq <= POS_SPLIT * 256
    ovt = np.zeros((LANES, ncb), np.float32)
    ovt[:nsel, :n_cmp] = ov.T
    pos = np.arange(seq)
    kaug = np.zeros((seq, LANES), np.float32)
    kaug[:, POS_LANE0:POS_LANE0 + 2] = 1.0
    kaug[:, POS_LANE0 + 2] = pos // POS_SPLIT
    kaug[:, POS_LANE0 + 3] = pos % POS_SPLIT
    kaug[pos, SEL_LANE0 + pos // SEL_BLOCK] = 1.0
    cend = np.arange(ncb) * CMP_STRIDE + CMP_BLOCK - 1
    caug = np.zeros((ncb, LANES), np.float32)
    caug[:, POS_LANE0:POS_LANE0 + 2] = 1.0
    caug[:, POS_LANE0 + 2] = cend // POS_SPLIT
    caug[:, POS_LANE0 + 3] = cend % POS_SPLIT
    gsel = np.zeros((2, LANES, GQA_REP // 2, 3, 2, HEAD_DIM), np.float32)
    for pr in range(GQA_REP // 2):
        for j in range(3):
            for hh in range(2):
                gsel[:, 3 * (2 * pr + hh) + j, pr, j, hh, :] = 1.0
    gsel = gsel.reshape(2 * LANES, (GQA_REP // 2) * 3 * LANES)
    return jnp.asarray(ovt, bf16), jnp.asarray(kaug), jnp.asarray(caug), jnp.asarray(gsel, bf16)


def _invert_kernel(pos_ref, slot_ref):
    def clear(s, _):
        slot_ref[s] = 0
        return 0

    def place(t, _):
        slot_ref[pos_ref[t]] = t
        return 0

    lax.fori_loop(0, slot_ref.shape[0], clear, 0, unroll=8)
    lax.fori_loop(0, pos_ref.shape[0], place, 0, unroll=8)


def _invert_slots(pos, n_slots):
    smem = pl.BlockSpec(memory_space=pltpu.SMEM)
    return pl.pallas_call(
        _invert_kernel,
        in_specs=[smem],
        out_specs=smem,
        out_shape=jax.ShapeDtypeStruct((n_slots,), jnp.int32),
        name="moe_slot_index",
    )(pos)


def _dispatch_plan(meta, counts, n_tokens):
    i32 = jnp.int32
    n_slots = n_tokens + N_GROUPS * MOE_TILE
    group = meta[:, 0, :].reshape(n_tokens).astype(i32)
    rank = (meta[:, 1, :] * RANK_SPLIT + meta[:, 2, :]).reshape(n_tokens).astype(i32)
    cnt = counts[0, :N_GROUPS].astype(i32)
    padded = (cnt + MOE_TILE - 1) // MOE_TILE * MOE_TILE
    end = jnp.cumsum(padded)
    pos = (end - padded)[group] + rank
    slot_token = _invert_slots(pos, n_slots)
    tile_start = jnp.arange(n_slots // MOE_TILE, dtype=i32) * MOE_TILE
    tile_group = jnp.sum(tile_start[:, None] >= end[None, :], axis=1).astype(i32)
    return pos, slot_token, tile_group


def _block_diag2(w):
    z = jnp.zeros_like(w)
    return jnp.concatenate([jnp.concatenate([w, z], axis=-1), jnp.concatenate([z, w], axis=-1)], axis=-2)


def _layout_w_in(w):
    gate = w[:, N_MAIN:]
    per = GQA_REP * 3
    blocks = [jnp.pad(gate[:, g * per:(g + 1) * per], ((0, 0), (0, LANES - per))) for g in range(N_KV_B)]
    return jnp.concatenate([w[:, :N_MAIN]] + blocks, axis=1).astype(bf16)


def kernel(x, p, attn_norm, w_in, w_out, w_cmp_k1, w_cmp_k2, w_cmp_v1, w_cmp_v2, cmp_pos, ffn_norm, w_route_group, b_route_group, w_route_expert, b_route_expert, w_expert_gate, w_expert_up, w_expert_down, ple_norm, w_ple_gate, w_ple_proj, final_norm):
    B, S, D = x.shape
    depth = w_in.shape[0]
    T = B * S
    slopes_a, slopes_b = _alibi_slopes()
    ovt, kaug, caug, gsel = _selection_constants(S)
    wg_all = w_expert_gate.astype(bf16)
    wu_all = w_expert_up.astype(bf16)
    wd_all = w_expert_down.astype(bf16)
    p3 = p.reshape(depth, T, PLE_DIM)
    row = lambda v: v.reshape(1, -1)
    n_route = N_GROUPS + N_EXPERTS

    x2d = x.reshape(T, D)
    for i in range(depth):
        proj3 = _in_proj(x2d, row(attn_norm[i]), _layout_w_in(w_in[i])).reshape(B, S, N_PROJ)
        oa = _dilated(proj3, slopes_a)
        w1 = lambda w: _block_diag2(w.reshape(CMP_BLOCK, HEAD_DIM, CMP_HIDDEN)).astype(bf16)
        pos_dup = jnp.concatenate([cmp_pos[i], cmp_pos[i]], axis=-1)
        kcmp, vcmp = _compress(proj3, pos_dup, w1(w_cmp_k1[i]), _block_diag2(w_cmp_k2[i]).astype(bf16),
                               w1(w_cmp_v1[i]), _block_diag2(w_cmp_v2[i]).astype(bf16))
        ob = _nsa(proj3, kcmp, vcmp, slopes_b, ovt, kaug, caug, gsel)
        wo = w_out[i].astype(bf16)
        w_route = jnp.pad(jnp.concatenate([w_route_group[i], w_route_expert[i]], axis=1),
                          ((0, 0), (0, LANES - n_route)))
        b_route = jnp.pad(jnp.concatenate([b_route_group[i], b_route_expert[i]]), (0, LANES - n_route))
        x2d, stage, meta, counts = _out_route(x2d, oa.reshape(T, A_W), ob.reshape(T, -1), wo[:A_W], wo[A_W:],
                                              row(ffn_norm[i]), w_route, row(b_route))
        pos, slot_token, tile_group = _dispatch_plan(meta, counts, T)
        ys = _moe(stage, tile_group, slot_token, w_route, row(b_route), wg_all, wu_all, wd_all, i)
        x2d = _ple(x2d, ys, pos, p3, row(ple_norm[i]), w_ple_gate[i].astype(bf16),
                   w_ple_proj[i].astype(bf16), row(final_norm), i, i == depth - 1)
    return x2d.reshape(B, S, D)
```
